```python
import math
import jax
import jax.numpy as jnp
from jax import lax
import numpy as np

D_MODEL = 1024
BATCH = 8
SEQ = 4096
DEPTH = 2

CTX_LEN = 256
GRID_W = 64
N_MOD = 6
N_BRANCH = 4
BRANCH_W = 256

MLA_HEADS = 4
MLA_NOPE = 64
MLA_ROPE = 32
MLA_V = 64
MLA_Q_RANK = 192
MLA_KV_RANK = 128
ROPE_THETA = 10000.0

NA_HEADS = 4
NA_HEAD_DIM = 64
NA_WIN_R = 8
NA_WIN_C = 16

HY_WIDTH = 256
HY_ORDER = 2
HY_SHORT = 3
HY_BANDS = 16
HY_EMB = 2 * HY_BANDS + 1
HY_FFN = 64
HY_DECAY_MIN = 3.0
HY_DECAY_MAX = 15.0

LRU_WIDTH = 256
LRU_BLOCKS = 4
LRU_BLOCK_W = LRU_WIDTH // LRU_BLOCKS
LRU_CONV = 4
LRU_C = 8.0

N_EXPERTS = 64
TOP_K = 6
EXPERT_HIDDEN = 256
SHARED_HIDDEN = 256
ROUTED_SCALE = 2.5
MOE_BLOCK = 256

Q_BLOCK = 128
NORM_EPS = 1e-6

IN_SIZES = (MLA_Q_RANK, MLA_KV_RANK, MLA_ROPE, 3 * NA_HEADS * NA_HEAD_DIM, 3 * HY_WIDTH,
            LRU_WIDTH, LRU_WIDTH, N_BRANCH * D_MODEL)
IN_COLS = (MLA_Q_RANK + MLA_KV_RANK + MLA_ROPE + 3 * NA_HEADS * NA_HEAD_DIM + 3 * HY_WIDTH
           + 2 * LRU_WIDTH + N_BRANCH * D_MODEL)

kernel_name = 'hybrid_mla_na_hyena_rglru_moe_dit'


def rmsnorm(x, g):
    xf = x.astype(jnp.float32)
    y = xf * lax.rsqrt(jnp.mean(xf * xf, axis=-1, keepdims=True) + NORM_EPS)
    return (y * g.astype(jnp.float32)).astype(x.dtype)


def split_cols(z):
    parts = []
    start = 0
    for n in IN_SIZES:
        parts.append(z[..., start:start + n])
        start += n
    return parts


def depthwise_conv(x, w, b, left):
    n_taps = w.shape[0]
    length = x.shape[1]
    xp = jnp.pad(x, ((0, 0), (left, n_taps - 1 - left), (0, 0)))
    y = b
    for k in range(n_taps):
        y = y + xp[:, k:k + length] * w[k]
    return y


def axial_rope_tables(n_tok, rot_dim):
    t = jnp.arange(n_tok, dtype=jnp.int32)
    row = (t // GRID_W).astype(jnp.float32)
    col = (t % GRID_W).astype(jnp.float32)
    n_axis = rot_dim // 4
    inv_freq = ROPE_THETA ** (-jnp.arange(n_axis, dtype=jnp.float32) / n_axis)
    ang = jnp.concatenate([row[:, None] * inv_freq, col[:, None] * inv_freq], axis=-1)
    return jnp.cos(ang), jnp.sin(ang)


def apply_rope(x, cos, sin):
    half = x.shape[-1] // 2
    x1 = x[..., :half].astype(jnp.float32)
    x2 = x[..., half:].astype(jnp.float32)
    cb = cos[None, :, None, :]
    sb = sin[None, :, None, :]
    return jnp.concatenate([x1 * cb - x2 * sb, x1 * sb + x2 * cb], axis=-1).astype(x.dtype)


def blocked_attention(q, k, v, scale):
    b, n_q, h, dq = q.shape
    n_blk = n_q // Q_BLOCK
    qb = q.reshape(b, n_blk, Q_BLOCK, h, dq).transpose(1, 0, 2, 3, 4)

    def one_block(q_blk):
        s = jnp.einsum('bqhd,bkhd->bhqk', q_blk, k).astype(jnp.float32) * scale
        p = jax.nn.softmax(s, axis=-1).astype(v.dtype)
        return jnp.einsum('bhqk,bkhd->bqhd', p, v)

    o = lax.map(one_block, qb)
    return o.transpose(1, 0, 2, 3, 4).reshape(b, n_q, h * v.shape[-1])


def mla_queries(q_lat, lp, rope):
    b, n, _ = q_lat.shape
    q = (rmsnorm(q_lat, lp['mla_g_q']) @ lp['mla_w_uq']).reshape(b, n, MLA_HEADS, MLA_NOPE + MLA_ROPE)
    q_nope, q_rope = q[..., :MLA_NOPE], q[..., MLA_NOPE:]
    if rope is not None:
        q_rope = apply_rope(q_rope, rope[0], rope[1])
    return jnp.concatenate([q_nope, q_rope], axis=-1)


def mla_keys_values(kv_lat, k_rope, lp, rope):
    b, n, _ = kv_lat.shape
    kv = (rmsnorm(kv_lat, lp['mla_g_kv']) @ lp['mla_w_ukv']).reshape(b, n, MLA_HEADS, MLA_NOPE + MLA_V)
    k_nope, v = kv[..., :MLA_NOPE], kv[..., MLA_NOPE:]
    k_rope = k_rope[:, :, None, :]
    if rope is not None:
        k_rope = apply_rope(k_rope, rope[0], rope[1])
    k_rope = jnp.broadcast_to(k_rope, (b, n, MLA_HEADS, MLA_ROPE))
    return jnp.concatenate([k_nope, k_rope], axis=-1), v


def na_heads(z):
    b, n, _ = z.shape
    z5 = z.reshape(b, n, 3, NA_HEADS, NA_HEAD_DIM)
    return z5[:, :, 0], z5[:, :, 1], z5[:, :, 2]


def na_mixer(z_lat, z_ctx, rpb, with_ctx_out):
    q, k, v = na_heads(z_lat)
    q_c, k_c, v_c = na_heads(z_ctx)
    b, s = q.shape[0], q.shape[1]
    rows = s // GRID_W
    wr = min(NA_WIN_R, rows)
    n_loc = wr * GRID_W
    scale = NA_HEAD_DIM ** -0.5
    grid = (b, rows, GRID_W, NA_HEADS, NA_HEAD_DIM)
    q_g, k_g, v_g = q.reshape(grid), k.reshape(grid), v.reshape(grid)
    col = jnp.arange(GRID_W, dtype=jnp.int32)
    c0 = jnp.clip(col - NA_WIN_C // 2, 0, GRID_W - NA_WIN_C)
    in_win = (col[None, :] >= c0[:, None]) & (col[None, :] < c0[:, None] + NA_WIN_C)
    dc_idx = jnp.clip(col[None, :] - col[:, None], 1 - NA_WIN_C, NA_WIN_C - 1) + NA_WIN_C - 1
    rpb_f = rpb.astype(jnp.float32)

    def one_row(r):
        r0 = jnp.clip(r - wr // 2, 0, rows - wr)
        k_r = lax.dynamic_slice_in_dim(k_g, r0, wr, axis=1).reshape(b, n_loc, NA_HEADS, NA_HEAD_DIM)
        v_r = lax.dynamic_slice_in_dim(v_g, r0, wr, axis=1).reshape(b, n_loc, NA_HEADS, NA_HEAD_DIM)
        q_r = lax.dynamic_index_in_dim(q_g, r, axis=1, keepdims=False)
        dr_idx = r0 + jnp.arange(wr, dtype=jnp.int32) - r + NA_WIN_R - 1
        bias = rpb_f[:, dr_idx][:, :, dc_idx]
        bias = jnp.where(in_win[None, None], bias, -jnp.inf)
        bias = bias.transpose(0, 2, 1, 3).reshape(NA_HEADS, GRID_W, n_loc)
        s_loc = jnp.einsum('bqhd,bkhd->bhqk', q_r, k_r).astype(jnp.float32) * scale + bias
        s_ctx = jnp.einsum('bqhd,bchd->bhqc', q_r, k_c).astype(jnp.float32) * scale
        p = jax.nn.softmax(jnp.concatenate([s_loc, s_ctx], axis=-1), axis=-1).astype(v.dtype)
        return (jnp.einsum('bhqk,bkhd->bqhd', p[..., :n_loc], v_r)
                + jnp.einsum('bhqc,bchd->bqhd', p[..., n_loc:], v_c))

    o = lax.map(one_row, jnp.arange(rows, dtype=jnp.int32))
    y_lat = o.transpose(1, 0, 2, 3, 4).reshape(b, s, NA_HEADS * NA_HEAD_DIM)
    y_ctx = blocked_attention(q_c, k_c, v_c, scale) if with_ctx_out else None
    return y_lat, y_ctx


def hyena_pos_features(length):
    t = jnp.linspace(0.0, 1.0, length, dtype=jnp.float32)[:, None]
    w = 2.0 * math.pi * jnp.arange(length, dtype=jnp.float32)[:, None] / length
    f = jnp.linspace(1e-4, HY_BANDS - 1, HY_BANDS, dtype=jnp.float32)[None, :]
    z = w * f
    return jnp.concatenate([t, jnp.cos(z), -jnp.sin(z)], axis=-1)


def hyena_filter_spectra(length, lp):
    f32 = jnp.float32
    feats = hyena_pos_features(length)
    h = jnp.sin(feats @ lp['hy_w1'].astype(f32) + lp['hy_b1'].astype(f32))
    h = jnp.sin(h @ lp['hy_w2'].astype(f32) + lp['hy_b2'].astype(f32))
    h = (h @ lp['hy_w3'].astype(f32)).reshape(length, HY_ORDER, 2, HY_WIDTH)
    lag = jnp.linspace(0.0, 1.0, length, dtype=f32)[:, None, None, None]
    h = h * jnp.exp(-lag * jnp.abs(lp['hy_decay'].astype(f32))[None])
    h = h.transpose(1, 2, 0, 3)
    taps = jnp.concatenate([h[:, 0], jnp.zeros((HY_ORDER, 1, HY_WIDTH), f32), h[:, 1, :0:-1]], axis=1)
    return jnp.fft.rfft(taps, axis=1)


def fft_long_conv(u, taps_f):
    length = u.shape[1]
    u_f = jnp.fft.rfft(u, n=2 * length, axis=1)
    return jnp.fft.irfft(u_f * taps_f[None], n=2 * length, axis=1)[:, :length]


def hyena_seq(z, lp):
    zc = depthwise_conv(z, lp['hy_short_w'], lp['hy_short_b'], HY_SHORT // 2).astype(jnp.float32)
    v, x1, x2 = jnp.split(zc, 3, axis=-1)
    taps_f = hyena_filter_spectra(z.shape[1], lp)
    bias = lp['hy_bias'].astype(jnp.float32)
    y = v
    for n, gate in enumerate((x1, x2)):
        y = gate * (fft_long_conv(y, taps_f[n]) + y * bias[n])
    return y.astype(z.dtype)


def linear_scan(a, b, h0, reverse):
    def combine(left, right):
        return left[0] * right[0], right[0] * left[1] + right[1]
    a_cum, b_cum = lax.associative_scan(combine, (a, b), axis=1, reverse=reverse)
    return a_cum * h0[:, None, :] + b_cum


def rglru_dir(u, h0, conv_w, conv_b, wa, ba, wx, bx, lam, reverse):
    f32 = jnp.float32
    x = depthwise_conv(u, conv_w, conv_b, 0 if reverse else LRU_CONV - 1)
    b, length, w = x.shape
    xf = x.astype(f32)
    xb = xf.reshape(b, length, LRU_BLOCKS, LRU_BLOCK_W)
    r = jax.nn.sigmoid(jnp.einsum('blnc,ncd->blnd', xb, wa.astype(f32)).reshape(b, length, w) + ba.astype(f32))
    i = jax.nn.sigmoid(jnp.einsum('blnc,ncd->blnd', xb, wx.astype(f32)).reshape(b, length, w) + bx.astype(f32))
    log_a = -LRU_C * r * jax.nn.softplus(-lam.astype(f32))
    a = jnp.exp(log_a)
    bt = jnp.sqrt(-jnp.expm1(2.0 * log_a)) * (i * xf)
    h = linear_scan(a, bt, h0, reverse)
    final = h[:, 0] if reverse else h[:, -1]
    return h, final


def lru_mixer(u_lat, g_lat, u_ctx, g_ctx, lp, with_ctx_out):
    b = u_lat.shape[0]
    y_lat = 0.0
    y_ctx = 0.0
    for d, rev in enumerate((False, True)):
        prm = (lp['lru_conv_w'][d], lp['lru_conv_b'][d], lp['lru_wa'][d], lp['lru_ba'][d],
               lp['lru_wx'][d], lp['lru_bx'][d], lp['lru_lambda'][d])
        h_ctx, ctx_final = rglru_dir(u_ctx, jnp.zeros((b, LRU_WIDTH), jnp.float32), *prm, rev)
        h_lat, _ = rglru_dir(u_lat, ctx_final, *prm, rev)
        y_lat = y_lat + h_lat
        y_ctx = y_ctx + h_ctx
    out_lat = (y_lat * jax.nn.gelu(g_lat.astype(jnp.float32))).astype(u_lat.dtype)
    out_ctx = (y_ctx * jax.nn.gelu(g_ctx.astype(jnp.float32))).astype(u_ctx.dtype) if with_ctx_out else None
    return out_lat, out_ctx


def merge_branches(branches, gate_cols, w_branch, w_out):
    b, n, _ = gate_cols.shape
    gates = jax.nn.sigmoid(gate_cols.astype(jnp.float32)).reshape(b, n, N_BRANCH, D_MODEL)
    acc = 0.0
    for k, br in enumerate(branches):
        acc = acc + gates[:, :, k] * (br @ w_branch[k]).astype(jnp.float32)
    return acc.astype(gate_cols.dtype) @ w_out


def swiglu(h, w_gate, w_up, w_down):
    return (jax.nn.silu(h @ w_gate) * (h @ w_up)) @ w_down


def routed_experts(h, idx, wts, w_gate, w_up, w_down):
    f32 = jnp.float32
    t, d = h.shape
    e_count = w_gate.shape[0]
    m = t * TOP_K
    flat_e = idx.reshape(m).astype(jnp.int32)
    flat_tok = jnp.repeat(jnp.arange(t, dtype=jnp.int32), TOP_K)
    flat_w = wts.reshape(m)
    order = jnp.argsort(flat_e)
    e_sorted = flat_e[order]
    counts = jnp.bincount(flat_e, length=e_count)
    starts = jnp.cumsum(counts) - counts
    padded = (counts + MOE_BLOCK - 1) // MOE_BLOCK * MOE_BLOCK
    p_ends = jnp.cumsum(padded)
    p_starts = p_ends - padded
    dest = p_starts[e_sorted] + jnp.arange(m, dtype=jnp.int32) - starts[e_sorted]
    n_blocks = (m + e_count * (MOE_BLOCK - 1) + MOE_BLOCK - 1) // MOE_BLOCK
    n_rows = n_blocks * MOE_BLOCK
    row_tok = jnp.full((n_rows,), t, jnp.int32).at[dest].set(flat_tok[order])
    row_w = jnp.zeros((n_rows,), f32).at[dest].set(flat_w[order].astype(f32))
    block_e = jnp.minimum(jnp.searchsorted(p_ends, jnp.arange(n_blocks, dtype=jnp.int32) * MOE_BLOCK,
                                           side='right'), e_count - 1).astype(jnp.int32)
    h_pad = jnp.concatenate([h, jnp.zeros((1, d), h.dtype)], axis=0)

    def step(acc, blk):
        e, rows, w = blk
        xb = h_pad[rows]
        hid = jax.nn.silu(xb @ w_gate[e]) * (xb @ w_up[e])
        y = (hid @ w_down[e]).astype(f32) * w[:, None]
        return acc.at[rows].add(y), None

    acc, _ = lax.scan(step, jnp.zeros((t + 1, d), f32),
                      (block_e, row_tok.reshape(n_blocks, MOE_BLOCK), row_w.reshape(n_blocks, MOE_BLOCK)))
    return acc[:t].astype(h.dtype)


def moe_ffn(h, lp):
    scores = jax.nn.sigmoid((h @ lp['router_w']).astype(jnp.float32))
    _, idx = lax.top_k(scores + lp['router_bias'].astype(jnp.float32), TOP_K)
    sel = jnp.take_along_axis(scores, idx, axis=-1)
    wts = sel / jnp.sum(sel, axis=-1, keepdims=True) * ROUTED_SCALE
    routed = routed_experts(h, idx, wts, lp['exp_w_gate'], lp['exp_w_up'], lp['exp_w_down'])
    shared = swiglu(h, lp['sh_w_gate'], lp['sh_w_up'], lp['sh_w_down'])
    return routed + shared


def mixer(h, hc, lp, rope, with_ctx_out):
    b, s, _ = h.shape
    n_ctx = hc.shape[1]
    q_l, kv_l, kr_l, na_l, hy_l, lu_l, lg_l, gt_l = split_cols(h @ lp['w_in'])
    q_c, kv_c, kr_c, na_c, hy_c, lu_c, lg_c, gt_c = split_cols(hc @ lp['w_in'])
    mla_scale = (MLA_NOPE + MLA_ROPE) ** -0.5
    k_lat, v_lat = mla_keys_values(kv_l, kr_l, lp, rope)
    k_ctx, v_ctx = mla_keys_values(kv_c, kr_c, lp, None)
    a_l = blocked_attention(mla_queries(q_l, lp, rope), jnp.concatenate([k_lat, k_ctx], axis=1),
                            jnp.concatenate([v_lat, v_ctx], axis=1), mla_scale)
    b_l, b_c = na_mixer(na_l, na_c, lp['na_rpb'], with_ctx_out)
    c_l = hyena_seq(hy_l, lp)
    d_l, d_c = lru_mixer(lu_l, lg_l, lu_c, lg_c, lp, with_ctx_out)
    y = merge_branches((a_l, b_l, c_l, d_l), gt_l, lp['w_branch'], lp['w_out'])
    y_c = None
    if with_ctx_out:
        a_c = blocked_attention(mla_queries(q_c, lp, None), k_ctx, v_ctx, mla_scale)
        c_c = hyena_seq(hy_c, lp)
        y_c = merge_branches((a_c, b_c, c_c, d_c), gt_c, lp['w_branch'], lp['w_out'])
    return y, y_c


def layer(x, xc, c, c_ctx, lp, rope, with_ctx_out):
    b, s, d = x.shape
    n_ctx = xc.shape[1]
    mod = (jax.nn.silu(c) @ lp['w_mod'] + lp['b_mod'])[:, None, :]
    mod_c = jax.nn.silu(c_ctx) @ lp['w_mod'] + lp['b_mod']
    sh1, sc1, g1, sh2, sc2, g2 = jnp.split(mod, N_MOD, axis=-1)
    csh1, csc1, cg1, csh2, csc2, cg2 = jnp.split(mod_c, N_MOD, axis=-1)
    h = rmsnorm(x, lp['g_norm1']) * (1.0 + sc1) + sh1
    hc = rmsnorm(xc, lp['g_norm1']) * (1.0 + csc1) + csh1
    y, y_c = mixer(h, hc, lp, rope, with_ctx_out)
    x = x + g1 * y
    h2 = (rmsnorm(x, lp['g_norm2']) * (1.0 + sc2) + sh2).reshape(b * s, d)
    if with_ctx_out:
        xc = xc + cg1 * y_c
        hc2 = (rmsnorm(xc, lp['g_norm2']) * (1.0 + csc2) + csh2).reshape(b * n_ctx, d)
        f = moe_ffn(jnp.concatenate([h2, hc2], axis=0), lp)
        x = x + g2 * f[:b * s].reshape(b, s, d)
        xc = xc + cg2 * f[b * s:].reshape(b, n_ctx, d)
    else:
        x = x + g2 * moe_ffn(h2, lp).reshape(b, s, d)
    return x, xc


def setup_inputs(seed: int = 0) -> dict:
    key = jax.random.key(seed)
    keys = iter(jax.random.split(key, 64))
    f32 = jnp.float32
    L, D = DEPTH, D_MODEL

    def nrm(shape, scale):
        return jax.random.normal(next(keys), shape, f32) * scale

    def gain(shape):
        return 1.0 + nrm(shape, 0.05)

    u = jax.random.uniform(next(keys), (L, 2, LRU_WIDTH), f32, 0.9, 0.999)
    a0 = u ** (1.0 / LRU_C)
    lru_lambda = jnp.log(a0) - jnp.log1p(-a0)
    hy_decay = jax.random.uniform(next(keys), (L, HY_ORDER, 2, HY_WIDTH), f32, HY_DECAY_MIN, HY_DECAY_MAX)
    return {
        'x': nrm((BATCH, SEQ, D), 1.0),
        'c': nrm((BATCH, D), 1.0),
        'ctx': nrm((BATCH, CTX_LEN, D), 1.0),
        'c_ctx': nrm((D,), 1.0),
        'w_mod': nrm((L, D, N_MOD * D), 0.5 * D ** -0.5),
        'b_mod': nrm((L, N_MOD * D), 0.02),
        'g_norm1': gain((L, D)),
        'g_norm2': gain((L, D)),
        'w_in': nrm((L, D, IN_COLS), D ** -0.5),
        'mla_g_q': gain((L, MLA_Q_RANK)),
        'mla_w_uq': nrm((L, MLA_Q_RANK, MLA_HEADS * (MLA_NOPE + MLA_ROPE)), MLA_Q_RANK ** -0.5),
        'mla_g_kv': gain((L, MLA_KV_RANK)),
        'mla_w_ukv': nrm((L, MLA_KV_RANK, MLA_HEADS * (MLA_NOPE + MLA_V)), MLA_KV_RANK ** -0.5),
        'na_rpb': nrm((L, NA_HEADS, 2 * NA_WIN_R - 1, 2 * NA_WIN_C - 1), 0.1),
        'hy_short_w': nrm((L, HY_SHORT, 3 * HY_WIDTH), HY_SHORT ** -0.5),
        'hy_short_b': nrm((L, 3 * HY_WIDTH), 0.02),
        'hy_w1': nrm((L, HY_EMB, HY_FFN), HY_EMB ** -0.5),
        'hy_b1': nrm((L, HY_FFN), 0.5),
        'hy_w2': nrm((L, HY_FFN, HY_FFN), HY_FFN ** -0.5),
        'hy_b2': nrm((L, HY_FFN), 0.5),
        'hy_w3': nrm((L, HY_FFN, HY_ORDER * 2 * HY_WIDTH), 0.05 * HY_FFN ** -0.5),
        'hy_decay': hy_decay,
        'hy_bias': nrm((L, HY_ORDER, HY_WIDTH), 0.5),
        'lru_conv_w': nrm((L, 2, LRU_CONV, LRU_WIDTH), LRU_CONV ** -0.5),
        'lru_conv_b': nrm((L, 2, LRU_WIDTH), 0.02),
        'lru_wa': nrm((L, 2, LRU_BLOCKS, LRU_BLOCK_W, LRU_BLOCK_W), LRU_BLOCK_W ** -0.5),
        'lru_ba': nrm((L, 2, LRU_WIDTH), 0.1),
        'lru_wx': nrm((L, 2, LRU_BLOCKS, LRU_BLOCK_W, LRU_BLOCK_W), LRU_BLOCK_W ** -0.5),
        'lru_bx': nrm((L, 2, LRU_WIDTH), 0.1),
        'lru_lambda': lru_lambda,
        'w_branch': nrm((L, N_BRANCH, BRANCH_W, D), BRANCH_W ** -0.5),
        'w_out': nrm((L, D, D), D ** -0.5),
        'router_w': nrm((L, D, N_EXPERTS), D ** -0.5),
        'router_bias': nrm((L, N_EXPERTS), 0.01),
        'exp_w_gate': nrm((L, N_EXPERTS, D, EXPERT_HIDDEN), D ** -0.5),
        'exp_w_up': nrm((L, N_EXPERTS, D, EXPERT_HIDDEN), D ** -0.5),
        'exp_w_down': nrm((L, N_EXPERTS, EXPERT_HIDDEN, D), EXPERT_HIDDEN ** -0.5),
        'sh_w_gate': nrm((L, D, SHARED_HIDDEN), D ** -0.5),
        'sh_w_up': nrm((L, D, SHARED_HIDDEN), D ** -0.5),
        'sh_w_down': nrm((L, SHARED_HIDDEN, D), SHARED_HIDDEN ** -0.5),
        'g_final': gain((D,)),
    }


def reference(x, c, ctx, c_ctx, w_mod, b_mod, g_norm1, g_norm2, w_in, mla_g_q, mla_w_uq, mla_g_kv,
              mla_w_ukv, na_rpb, hy_short_w, hy_short_b, hy_w1, hy_b1, hy_w2, hy_b2, hy_w3, hy_decay,
              hy_bias, lru_conv_w, lru_conv_b, lru_wa, lru_ba, lru_wx, lru_bx, lru_lambda, w_branch, w_out,
              router_w, router_bias, exp_w_gate, exp_w_up, exp_w_down, sh_w_gate, sh_w_up, sh_w_down,
              g_final):
    rope = axial_rope_tables(x.shape[1], MLA_ROPE)
    xc = ctx
    for i in range(DEPTH):
        lp = {
            'w_mod': w_mod[i], 'b_mod': b_mod[i], 'g_norm1': g_norm1[i], 'g_norm2': g_norm2[i],
            'w_in': w_in[i], 'mla_g_q': mla_g_q[i], 'mla_w_uq': mla_w_uq[i], 'mla_g_kv': mla_g_kv[i],
            'mla_w_ukv': mla_w_ukv[i], 'na_rpb': na_rpb[i], 'hy_short_w': hy_short_w[i],
            'hy_short_b': hy_short_b[i], 'hy_w1': hy_w1[i], 'hy_b1': hy_b1[i], 'hy_w2': hy_w2[i],
            'hy_b2': hy_b2[i], 'hy_w3': hy_w3[i], 'hy_decay': hy_decay[i], 'hy_bias': hy_bias[i],
            'lru_conv_w': lru_conv_w[i], 'lru_conv_b': lru_conv_b[i], 'lru_wa': lru_wa[i],
            'lru_ba': lru_ba[i], 'lru_wx': lru_wx[i], 'lru_bx': lru_bx[i], 'lru_lambda': lru_lambda[i],
            'w_branch': w_branch[i], 'w_out': w_out[i], 'router_w': router_w[i],
            'router_bias': router_bias[i], 'exp_w_gate': exp_w_gate[i], 'exp_w_up': exp_w_up[i],
            'exp_w_down': exp_w_down[i], 'sh_w_gate': sh_w_gate[i], 'sh_w_up': sh_w_up[i],
            'sh_w_down': sh_w_down[i],
        }
        x, xc = layer(x, xc, c, c_ctx, lp, rope, i < DEPTH - 1)
    return rmsnorm(x, g_final)
```

```python
import functools
import math

import numpy as np
import jax
import jax.numpy as jnp
from jax import lax
from jax.experimental import pallas as pl
from jax.experimental.pallas import tpu as pltpu

F32 = jnp.float32
BF16 = jnp.bfloat16
I32 = jnp.int32

TM = 256
LANE = 128
GRID_W = 64
N_MOD = 6
NORM_EPS = 1e-6

MLA_HEADS, MLA_NOPE, MLA_ROPE, MLA_V = 4, 64, 32, 64
MLA_Q_RANK, MLA_KV_RANK = 192, 128
MLA_Q_PAD = 256
ROPE_THETA = 10000.0

NA_HEADS, NA_HEAD_DIM, NA_WIN_R, NA_WIN_C = 4, 64, 8, 16
NA_TILE_ROWS = TM // GRID_W
NA_KEY_ROWS = NA_TILE_ROWS + NA_WIN_R - 1
NA_KEYS = NA_KEY_ROWS * GRID_W

HY_WIDTH, HY_ORDER, HY_SHORT, HY_BANDS, HY_FFN = 256, 2, 3, 16, 64
HY_EMB = 2 * HY_BANDS + 1

LRU_WIDTH, LRU_BLOCKS, LRU_CONV, LRU_C = 256, 4, 4, 8.0
LRU_CHUNK = 256
LRU_HALO = 8

N_EXPERTS, TOP_K, EXPERT_HIDDEN, ROUTED_SCALE, MOE_BLOCK = 64, 6, 256, 2.5, 256
TOPK_PAD = 8

VMEM_LIMIT = 52 * 1024 * 1024


def _cparams(*sem):
    return pltpu.CompilerParams(dimension_semantics=sem, vmem_limit_bytes=VMEM_LIMIT)


def _dot(a, b):
    return jnp.dot(a, b, preferred_element_type=F32)


def _dot_nt(a, b):
    return lax.dot_general(a, b, (((1,), (1,)), ((), ())), preferred_element_type=F32)


def _sigmoid(x):
    return 1.0 / (1.0 + jnp.exp(-x))


def _silu(x):
    return x * _sigmoid(x)


def _gelu_tanh(x):
    return 0.5 * x * (1.0 + jnp.tanh(math.sqrt(2.0 / math.pi) * (x + 0.044715 * (x * x * x))))


def _normmod(x, g, scale, shift):
    y = x * lax.rsqrt(jnp.mean(x * x, axis=-1, keepdims=True) + NORM_EPS) * g
    return y * (1.0 + scale) + shift


def _mod_kernel(c_ref, w_ref, b_ref, o_ref):
    s = _silu(c_ref[...])
    o_ref[...] = _dot(s.astype(BF16), w_ref[...].astype(BF16)) + b_ref[...]


def _mod_table(c, c_ctx, w_mod, b_mod):
    b, d = c.shape
    rows = 16
    cc = jnp.zeros((rows, d), F32).at[:b].set(c).at[b].set(c_ctx)
    tn = 1024
    mod = pl.pallas_call(
        _mod_kernel,
        grid=(N_MOD * d // tn,),
        in_specs=[pl.BlockSpec((rows, d), lambda j: (0, 0)),
                  pl.BlockSpec((d, tn), lambda j: (0, j)),
                  pl.BlockSpec((1, tn), lambda j: (0, j))],
        out_specs=pl.BlockSpec((rows, tn), lambda j: (0, j)),
        out_shape=jax.ShapeDtypeStruct((rows, N_MOD * d), F32),
        compiler_params=_cparams("arbitrary"),
        name="mod_vectors",
    )(cc, w_mod, b_mod.reshape(1, -1))
    lat = mod[:b].reshape(b, N_MOD, d)
    ctx = jnp.broadcast_to(mod[b].reshape(1, N_MOD, d), (b, N_MOD, d))
    tab = jnp.stack([lat, ctx], axis=1)
    return jnp.pad(tab, ((0, 0), (0, 0), (0, 8 - N_MOD), (0, 0)))


_C_QLAT, _C_KVLAT, _C_KR, _C_KRR, _C_NA, _C_HY, _C_LU, _C_LG, _C_END = (
    0, 256, 384, 512, 640, 640 + 3 * NA_HEADS * LANE, 640 + 1536 + 768, 640 + 1536 + 1024, 640 + 1536 + 1280)


def _proj_kernel(x_ref, mod_ref, g1_ref, w1_ref, gq_ref, gkv_ref, wuq_ref, wuqr_ref, wk_ref, wv_ref,
                 cos_ref, sin_ref, q_o, k_o, v_o, nq_o, nk_o, nv_o, hy_o, lu_o, lg_o):
    m = mod_ref[0, 0]
    h = _normmod(x_ref[0], g1_ref[...], m[1:2], m[0:1])
    z = _dot(h.astype(BF16), w1_ref[...])
    qlat = z[:, _C_QLAT:_C_KVLAT]
    kvlat = z[:, _C_KVLAT:_C_KR]
    kr = z[:, _C_KR:_C_KRR]
    krr = z[:, _C_KRR:_C_NA]
    qn = qlat * lax.rsqrt(jnp.sum(qlat * qlat, axis=-1, keepdims=True) * (1.0 / MLA_Q_RANK) + NORM_EPS) * gq_ref[...]
    kvn = kvlat * lax.rsqrt(jnp.mean(kvlat * kvlat, axis=-1, keepdims=True) + NORM_EPS) * gkv_ref[...]
    qn = qn.astype(BF16)
    kvn = kvn.astype(BF16)
    q = _dot(qn, wuq_ref[...])
    qr = _dot(qn, wuqr_ref[...])
    kk = _dot(kvn, wk_ref[...])
    vv = _dot(kvn, wv_ref[...])
    cos = cos_ref[...]
    sin = sin_ref[...]
    krope = kr * cos + krr * sin
    for hd in range(MLA_HEADS):
        sl = slice(hd * LANE, (hd + 1) * LANE)
        q_o[0, hd] = (q[:, sl] * cos + qr[:, sl] * sin).astype(BF16)
        k_o[0, hd] = (kk[:, sl] + krope).astype(BF16)
        v_o[0, hd] = vv[:, sl].astype(BF16)
    for hd in range(NA_HEADS):
        for which, ref in enumerate((nq_o, nk_o, nv_o)):
            lo = _C_NA + (which * NA_HEADS + hd) * LANE
            ref[0, hd] = z[:, lo:lo + LANE].astype(BF16)
    hy_o[0] = z[:, _C_HY:_C_LU]
    lu_o[0] = z[:, _C_LU:_C_LG]
    lg_o[0] = z[:, _C_LG:_C_END]


def _pad_to(a, n, axis):
    pad = [(0, 0)] * a.ndim
    pad[axis] = (0, n - a.shape[axis])
    return jnp.pad(a, pad)


def _rot_cols(w):
    half = w.shape[-1] // 2
    return jnp.concatenate([-w[..., half:], w[..., :half]], axis=-1)


def _head_blocks(cols_per_head):
    out = []
    for pieces in cols_per_head:
        k = pieces[0][0].shape[0]
        blk = jnp.zeros((k, LANE), F32)
        for arr, off in pieces:
            blk = blk.at[:, off:off + arr.shape[1]].set(arr)
        out.append(blk)
    return jnp.concatenate(out, axis=1)


def _proj_weights(lp):
    w_in = lp['w_in']
    d = w_in.shape[0]
    o = 0
    parts = {}
    for name, n in (('q', MLA_Q_RANK), ('kv', MLA_KV_RANK), ('kr', MLA_ROPE), ('na', 3 * NA_HEADS * NA_HEAD_DIM),
                    ('hy', 3 * HY_WIDTH), ('lu', LRU_WIDTH), ('lg', LRU_WIDTH), ('gt', 4 * d)):
        parts[name] = w_in[:, o:o + n]
        o += n
    zeros = lambda n: jnp.zeros((d, n), F32)
    kr_blk = jnp.concatenate([zeros(MLA_NOPE), parts['kr'], zeros(LANE - MLA_NOPE - MLA_ROPE)], axis=1)
    krr_blk = jnp.concatenate([zeros(MLA_NOPE), _rot_cols(parts['kr']), zeros(LANE - MLA_NOPE - MLA_ROPE)], axis=1)
    na_scale = NA_HEAD_DIM ** -0.5
    na_cols = []
    for which in range(3):
        for hd in range(NA_HEADS):
            lo = (which * NA_HEADS + hd) * NA_HEAD_DIM
            blk = parts['na'][:, lo:lo + NA_HEAD_DIM] * (na_scale if which == 0 else 1.0)
            na_cols.append(_pad_to(blk, LANE, 1))
    w1 = jnp.concatenate([_pad_to(parts['q'], MLA_Q_PAD, 1), parts['kv'], kr_blk, krr_blk] + na_cols
                         + [parts['hy'], parts['lu'], parts['lg']], axis=1).astype(BF16)
    mla_scale = (MLA_NOPE + MLA_ROPE) ** -0.5
    wuq = _pad_to(lp['mla_w_uq'], MLA_Q_PAD, 0) * mla_scale
    dq = MLA_NOPE + MLA_ROPE
    wuq_main = _head_blocks([[(wuq[:, hd * dq:hd * dq + dq], 0)] for hd in range(MLA_HEADS)])
    wuq_rot = _head_blocks([[(_rot_cols(wuq[:, hd * dq + MLA_NOPE:hd * dq + dq]), MLA_NOPE)]
                            for hd in range(MLA_HEADS)])
    dkv = MLA_NOPE + MLA_V
    wukv = lp['mla_w_ukv']
    wk = _head_blocks([[(wukv[:, hd * dkv:hd * dkv + MLA_NOPE], 0)] for hd in range(MLA_HEADS)])
    wv = _head_blocks([[(wukv[:, hd * dkv + MLA_NOPE:hd * dkv + dkv], 0)] for hd in range(MLA_HEADS)])
    gq = _pad_to(lp['mla_g_q'].reshape(1, -1), MLA_Q_PAD, 1)
    gkv = lp['mla_g_kv'].reshape(1, -1)
    return dict(w1=w1, w_gate=parts['gt'].astype(BF16), gq=gq, gkv=gkv, wuq=wuq_main.astype(BF16),
                wuq_rot=wuq_rot.astype(BF16), wk=wk.astype(BF16), wv=wv.astype(BF16))


def _rope_tables(seq, n_tok):
    t = jnp.arange(seq, dtype=I32)
    row = (t // GRID_W).astype(F32)
    col = (t % GRID_W).astype(F32)
    n_axis = MLA_ROPE // 4
    inv_freq = ROPE_THETA ** (-jnp.arange(n_axis, dtype=F32) / n_axis)
    ang = jnp.concatenate([row[:, None] * inv_freq, col[:, None] * inv_freq], axis=-1)
    cos = jnp.concatenate([jnp.cos(ang), jnp.cos(ang)], axis=-1)
    sin = jnp.concatenate([jnp.sin(ang), jnp.sin(ang)], axis=-1)
    cos_t = jnp.ones((n_tok, LANE), F32).at[:seq, MLA_NOPE:MLA_NOPE + MLA_ROPE].set(cos)
    sin_t = jnp.zeros((n_tok, LANE), F32).at[:seq, MLA_NOPE:MLA_NOPE + MLA_ROPE].set(sin)
    return cos_t, sin_t


def _kind_map(n_lat_tiles):
    return lambda b, i: (b, jnp.where(i >= n_lat_tiles, 1, 0), 0, 0)


def _project(xa, modtab, g1, pw, rope, n_lat_tiles):
    b, nt, d = xa.shape
    cos_t, sin_t = rope
    full = lambda a: pl.BlockSpec(a.shape, lambda bb, i: (0,) * a.ndim)
    head_out = lambda: pl.BlockSpec((1, MLA_HEADS, TM, LANE), lambda bb, i: (bb, 0, i, 0))
    tok_out = lambda n: pl.BlockSpec((1, TM, n), lambda bb, i: (bb, i, 0))
    head_shape = jax.ShapeDtypeStruct((b, MLA_HEADS, nt, LANE), BF16)
    tok_shape = lambda n: jax.ShapeDtypeStruct((b, nt, n), F32)
    weights = (g1, pw['w1'], pw['gq'], pw['gkv'], pw['wuq'], pw['wuq_rot'], pw['wk'], pw['wv'])
    return pl.pallas_call(
        _proj_kernel,
        grid=(b, nt // TM),
        in_specs=[pl.BlockSpec((1, TM, d), lambda bb, i: (bb, i, 0)),
                  pl.BlockSpec((1, 1, 8, d), _kind_map(n_lat_tiles))]
                 + [full(w) for w in weights]
                 + [pl.BlockSpec((TM, LANE), lambda bb, i: (i, 0))] * 2,
        out_specs=[head_out() for _ in range(6)] + [tok_out(3 * HY_WIDTH), tok_out(LRU_WIDTH), tok_out(LRU_WIDTH)],
        out_shape=[head_shape] * 6 + [tok_shape(3 * HY_WIDTH), tok_shape(LRU_WIDTH), tok_shape(LRU_WIDTH)],
        compiler_params=_cparams("parallel", "parallel"),
        name="input_projection",
    )(xa, modtab, *weights, cos_t, sin_t)


def _softmax_pv(parts):
    m = None
    for s, _ in parts:
        mm = jnp.max(s, axis=-1, keepdims=True)
        m = mm if m is None else jnp.maximum(m, mm)
    acc = None
    den = None
    for s, v in parts:
        p = jnp.exp(s - m)
        l = jnp.sum(p, axis=-1, keepdims=True)
        o = _dot(p.astype(BF16), v)
        acc = o if acc is None else acc + o
        den = l if den is None else den + l
    return acc / den


def _mla_kernel(q_ref, k_ref, v_ref, o_ref, *, seq, n_lat_tiles):
    i = pl.program_id(1)
    nt = k_ref.shape[2]

    def attend(lo, hi):
        for hd in range(MLA_HEADS):
            s = _dot_nt(q_ref[0, hd], k_ref[0, hd, lo:hi, :])
            o = _softmax_pv([(s, v_ref[0, hd, lo:hi, :])])
            o_ref[0, :, hd * LANE:(hd + 1) * LANE] = o.astype(BF16)

    @pl.when(i < n_lat_tiles)
    def _():
        attend(0, nt)

    @pl.when(i >= n_lat_tiles)
    def _():
        attend(seq, nt)


def _mla_attention(q, k, v, seq):
    b, h, nt, _ = q.shape
    kv_spec = pl.BlockSpec((1, h, nt, LANE), lambda bb, i: (bb, 0, 0, 0))
    return pl.pallas_call(
        functools.partial(_mla_kernel, seq=seq, n_lat_tiles=seq // TM),
        grid=(b, nt // TM),
        in_specs=[pl.BlockSpec((1, h, TM, LANE), lambda bb, i: (bb, 0, i, 0)), kv_spec, kv_spec],
        out_specs=pl.BlockSpec((1, TM, h * LANE), lambda bb, i: (bb, i, 0)),
        out_shape=jax.ShapeDtypeStruct((b, nt, h * LANE), BF16),
        compiler_params=_cparams("parallel", "arbitrary"),
        name="mla_attention",
    )(q, k, v)


def _na_bias_tables(rpb, rows):
    n_blk = rows // NA_TILE_ROWS
    col = np.arange(GRID_W)
    c0 = np.clip(col - NA_WIN_C // 2, 0, GRID_W - NA_WIN_C)
    in_win = (col[None, :] >= c0[:, None]) & (col[None, :] < c0[:, None] + NA_WIN_C)
    dc = np.clip(col[None, :] - col[:, None], 1 - NA_WIN_C, NA_WIN_C - 1) + NA_WIN_C - 1
    rpb = rpb.astype(F32)
    tables = []
    for j in (0, 1, n_blk - 1):
        w0 = min(max(NA_TILE_ROWS * j - NA_WIN_R // 2, 0), rows - NA_KEY_ROWS)
        r = NA_TILE_ROWS * j + np.arange(NA_TILE_ROWS)
        kr = w0 + np.arange(NA_KEY_ROWS)
        r0 = np.clip(r - NA_WIN_R // 2, 0, rows - NA_WIN_R)
        row_ok = (kr[None, :] >= r0[:, None]) & (kr[None, :] < r0[:, None] + NA_WIN_R)
        dr = np.clip(kr[None, :] - r[:, None] + NA_WIN_R - 1, 0, 2 * NA_WIN_R - 2)
        bias = rpb[:, dr][:, :, :, dc]
        mask = row_ok[:, :, None, None] & in_win[None, None, :, :]
        bias = jnp.where(jnp.asarray(mask)[None], bias, -jnp.inf)
        tables.append(bias.transpose(0, 1, 3, 2, 4).reshape(NA_HEADS, TM, NA_KEYS))
    return jnp.stack(tables)


def _na_kernel(q_ref, k_ref, v_ref, bias_ref, o_ref, *, seq, n_lat_tiles):
    i = pl.program_id(1)
    nt = k_ref.shape[2]
    rows = seq // GRID_W

    @pl.when(i < n_lat_tiles)
    def _():
        w0 = jnp.clip(NA_TILE_ROWS * i - NA_WIN_R // 2, 0, rows - NA_KEY_ROWS)
        start = pl.multiple_of(w0 * GRID_W, GRID_W)
        for hd in range(NA_HEADS):
            q = q_ref[0, hd]
            s_loc = _dot_nt(q, k_ref[0, hd, pl.ds(start, NA_KEYS), :]) + bias_ref[0, hd]
            s_ctx = _dot_nt(q, k_ref[0, hd, seq:nt, :])
            o = _softmax_pv([(s_loc, v_ref[0, hd, pl.ds(start, NA_KEYS), :]), (s_ctx, v_ref[0, hd, seq:nt, :])])
            o_ref[0, :, hd * LANE:(hd + 1) * LANE] = o.astype(BF16)

    @pl.when(i >= n_lat_tiles)
    def _():
        for hd in range(NA_HEADS):
            s = _dot_nt(q_ref[0, hd], k_ref[0, hd, seq:nt, :])
            o = _softmax_pv([(s, v_ref[0, hd, seq:nt, :])])
            o_ref[0, :, hd * LANE:(hd + 1) * LANE] = o.astype(BF16)


def _na_attention(q, k, v, bias, seq):
    b, h, nt, _ = q.shape
    n_lat = seq // TM
    kv_spec = pl.BlockSpec((1, h, nt, LANE), lambda bb, i: (bb, 0, 0, 0))
    cfg = lambda bb, i: (jnp.where(i == 0, 0, jnp.where(i >= n_lat - 1, 2, 1)), 0, 0, 0)
    return pl.pallas_call(
        functools.partial(_na_kernel, seq=seq, n_lat_tiles=n_lat),
        grid=(b, nt // TM),
        in_specs=[pl.BlockSpec((1, h, TM, LANE), lambda bb, i: (bb, 0, i, 0)), kv_spec, kv_spec,
                  pl.BlockSpec((1, h, TM, NA_KEYS), cfg)],
        out_specs=pl.BlockSpec((1, TM, h * LANE), lambda bb, i: (bb, i, 0)),
        out_shape=jax.ShapeDtypeStruct((b, nt, h * LANE), BF16),
        compiler_params=_cparams("parallel", "arbitrary"),
        name="neighbourhood_attention",
    )(q, k, v, bias)


HY_PRE_CHUNK = 256


def _hy_pre_kernel(z_ref, w_ref, b_ref, lat_o, lat_bf_o, ctx_o, ctx_bf_o, *, seq, ctx_len):
    w = w_ref[...]
    bias = b_ref[...]
    ch = HY_PRE_CHUNK
    zero = jnp.zeros((8, HY_WIDTH), F32)
    for seg_lo, seg_len, o_ref, obf_ref in ((0, seq, lat_o, lat_bf_o), (seq, ctx_len, ctx_o, ctx_bf_o)):
        for c in range(seg_len // ch):
            s = seg_lo + c * ch
            before = zero if c == 0 else z_ref[0, s - 8:s, :]
            after = zero if c == seg_len // ch - 1 else z_ref[0, s + ch:s + ch + 8, :]
            win = jnp.concatenate([before, z_ref[0, s:s + ch, :], after], axis=0)
            n = ch + 16
            y = bias + w[1:2] * win[8:8 + ch]
            y = y + w[0:1] * pltpu.roll(win, 1, axis=0)[8:8 + ch]
            y = y + w[2:3] * pltpu.roll(win, n - 1, axis=0)[8:8 + ch]
            o_ref[0, c * ch:(c + 1) * ch, :] = y
            obf_ref[0, c * ch:(c + 1) * ch, :] = y.astype(BF16)


def _hy_pre(hy, w, bvec, seq):
    b, nt, _ = hy.shape
    ctx_len = nt - seq
    wpad = _pad_to(w, 8, 0)
    out_specs, out_shape = [], []
    for length in (seq, ctx_len):
        for dt in (F32, BF16):
            out_specs.append(pl.BlockSpec((1, length, HY_WIDTH), lambda bb, g: (g, 0, bb)))
            out_shape.append(jax.ShapeDtypeStruct((3, length, b * HY_WIDTH), dt))
    return pl.pallas_call(
        functools.partial(_hy_pre_kernel, seq=seq, ctx_len=ctx_len),
        grid=(b, 3),
        in_specs=[pl.BlockSpec((1, nt, HY_WIDTH), lambda bb, g: (bb, 0, g)),
                  pl.BlockSpec((8, HY_WIDTH), lambda bb, g: (0, g)),
                  pl.BlockSpec((1, HY_WIDTH), lambda bb, g: (0, g))],
        out_specs=out_specs,
        out_shape=out_shape,
        compiler_params=_cparams("parallel", "parallel"),
        name="hyena_short_conv",
    )(hy, wpad, bvec.reshape(1, -1))


def _hp_dot(a, b):
    return jnp.dot(a, b, preferred_element_type=F32, precision=lax.Precision.HIGHEST)


def _hy_filter_kernel(f_ref, w1_ref, b1_ref, w2_ref, b2_ref, w3_ref, dec_ref, o_ref, obf_ref):
    f = f_ref[...]
    h = jnp.sin(_hp_dot(f, w1_ref[...]) + b1_ref[...])
    h = jnp.sin(_hp_dot(h, w2_ref[...]) + b2_ref[...])
    h = _hp_dot(h, w3_ref[...])
    h = h * jnp.exp(-f[:, 0:1] * jnp.abs(dec_ref[...]))
    o_ref[...] = h
    obf_ref[...] = h.astype(BF16)


def _hy_pos_features(length):
    t = jnp.linspace(0.0, 1.0, length, dtype=F32)[:, None]
    w = 2.0 * math.pi * jnp.arange(length, dtype=F32)[:, None] / length
    f = jnp.linspace(1e-4, HY_BANDS - 1, HY_BANDS, dtype=F32)[None, :]
    z = w * f
    return jnp.concatenate([t, jnp.cos(z), -jnp.sin(z)], axis=-1)


def _hy_filters(length, lp):
    feats = _hy_pos_features(length)
    n_out = HY_ORDER * 2 * HY_WIDTH
    tl = min(length, 512)
    full = lambda a: pl.BlockSpec(a.shape, lambda i: (0,) * a.ndim)
    args = (lp['hy_w1'], lp['hy_b1'].reshape(1, -1), lp['hy_w2'], lp['hy_b2'].reshape(1, -1), lp['hy_w3'],
            lp['hy_decay'].reshape(1, n_out))
    return pl.pallas_call(
        _hy_filter_kernel,
        grid=(length // tl,),
        in_specs=[pl.BlockSpec((tl, HY_EMB), lambda i: (i, 0))] + [full(a) for a in args],
        out_specs=[pl.BlockSpec((tl, n_out), lambda i: (i, 0))] * 2,
        out_shape=[jax.ShapeDtypeStruct((length, n_out), F32), jax.ShapeDtypeStruct((length, n_out), BF16)],
        compiler_params=_cparams("parallel"),
        name="hyena_filter_mlp",
    )(feats, *args)


def _dft_matrices(length):
    n2 = 2 * length
    step = 64
    n = jnp.arange(length, dtype=I32)

    def trig(kv):
        ang = ((kv[:, None] * n[None, :]) % n2).astype(F32) * (2.0 * math.pi / n2)
        return jnp.cos(ang), jnp.sin(ang)

    c1, s1 = trig(jnp.arange(length // step, dtype=I32) * step)
    c2, s2 = trig(jnp.arange(step, dtype=I32))
    cos = (c1[:, None, :] * c2[None] - s1[:, None, :] * s2[None]).reshape(length, length)
    sin = (s1[:, None, :] * c2[None] + c1[:, None, :] * s2[None]).reshape(length, length)
    nyq = jnp.where(n % 2 == 0, 1.0, -1.0).astype(F32)
    ms = (-sin).at[0, :].set(nyq)
    mst = (-sin).at[:, 0].set(nyq)
    return cos.astype(BF16), ms.astype(BF16), mst.astype(BF16)


def _dft_fwd_kernel(mc_ref, ms_ref, x_ref, *rest, with_taps):
    ure = _dot(mc_ref[...], x_ref[...])
    uim = _dot(ms_ref[...], x_ref[...])
    if not with_taps:
        ure_o, uim_o = rest
        ure_o[...] = ure
        uim_o[...] = uim
        return
    a_ref, b_ref, c_ref, d_ref, zre_o, zim_o = rest
    a, b, c, d = a_ref[...], b_ref[...], c_ref[...], d_ref[...]
    for bb in range(x_ref.shape[1] // HY_WIDTH):
        sl = slice(bb * HY_WIDTH, (bb + 1) * HY_WIDTH)
        zre_o[:, sl] = (ure[:, sl] * a - uim[:, sl] * b).astype(BF16)
        zim_o[:, sl] = (ure[:, sl] * c + uim[:, sl] * d).astype(BF16)


def _col_block(nc, cap):
    return min(nc, cap)


def _dft_fwd(mats, x, taps=None):
    mc, ms, _ = mats
    length, nc = x.shape
    tk = min(length, 256)
    cb = _col_block(nc, 1024)
    grid = (nc // cb, length // tk)
    m_spec = pl.BlockSpec((tk, length), lambda c, j: (j, 0))
    x_spec = pl.BlockSpec((length, cb), lambda c, j: (0, c))
    o_spec = pl.BlockSpec((tk, cb), lambda c, j: (j, c))
    if taps is None:
        return pl.pallas_call(
            functools.partial(_dft_fwd_kernel, with_taps=False),
            grid=grid, in_specs=[m_spec, m_spec, x_spec], out_specs=[o_spec, o_spec],
            out_shape=[jax.ShapeDtypeStruct((length, nc), F32)] * 2,
            compiler_params=_cparams("parallel", "arbitrary"),
            name="hyena_dft_filters",
        )(mc, ms, x)
    t_spec = pl.BlockSpec((tk, HY_WIDTH), lambda c, j: (j, 0))
    return pl.pallas_call(
        functools.partial(_dft_fwd_kernel, with_taps=True),
        grid=grid, in_specs=[m_spec, m_spec, x_spec] + [t_spec] * 4, out_specs=[o_spec, o_spec],
        out_shape=[jax.ShapeDtypeStruct((length, nc), BF16)] * 2,
        compiler_params=_cparams("parallel", "arbitrary"),
        name="hyena_dft_forward",
    )(mc, ms, x, *taps)


def _dft_inv_kernel(mc_ref, mst_ref, zre_ref, zim_ref, gate_ref, prev_ref, bias_ref, *outs, last):
    conv = _dot(mc_ref[...], zre_ref[...]) + _dot(mst_ref[...], zim_ref[...])
    y = gate_ref[...] * (conv + prev_ref[...] * bias_ref[...])
    if last:
        (tok_o,) = outs
        for bb in range(y.shape[1] // HY_WIDTH):
            tok_o[bb] = y[:, bb * HY_WIDTH:(bb + 1) * HY_WIDTH].astype(BF16)
    else:
        y_o, ybf_o = outs
        y_o[...] = y
        ybf_o[...] = y.astype(BF16)


def _dft_inv(mats, zre, zim, gate, prev, bias_row, last):
    mc, _, mst = mats
    length, nc = zre.shape
    tm = min(length, 256)
    cb = _col_block(nc, 512)
    grid = (nc // cb, length // tm)
    m_spec = pl.BlockSpec((tm, length), lambda c, i: (i, 0))
    z_spec = pl.BlockSpec((length, cb), lambda c, i: (0, c))
    e_spec = pl.BlockSpec((tm, cb), lambda c, i: (i, c))
    b_spec = pl.BlockSpec((1, cb), lambda c, i: (0, c))
    if last:
        nb = nc // HY_WIDTH
        out_specs = [pl.BlockSpec((cb // HY_WIDTH, tm, HY_WIDTH), lambda c, i: (c, i, 0))]
        out_shape = [jax.ShapeDtypeStruct((nb, length, HY_WIDTH), BF16)]
    else:
        out_specs = [e_spec, e_spec]
        out_shape = [jax.ShapeDtypeStruct((length, nc), F32), jax.ShapeDtypeStruct((length, nc), BF16)]
    return pl.pallas_call(
        functools.partial(_dft_inv_kernel, last=last),
        grid=grid, in_specs=[m_spec, m_spec, z_spec, z_spec, e_spec, e_spec, b_spec],
        out_specs=out_specs, out_shape=out_shape,
        compiler_params=_cparams("parallel", "arbitrary"),
        name="hyena_dft_inverse",
    )(mc, mst, zre, zim, gate, prev, bias_row)


def _hy_tap_tables(ure, uim, filt, length):
    n2 = 2 * length
    w = HY_WIDTH
    scale = jnp.full((length, 1), 2.0 / n2, F32).at[0, 0].set(1.0 / n2)
    tables = []
    for o in range(HY_ORDER):
        f_sl = slice((2 * o) * w, (2 * o + 1) * w)
        b_sl = slice((2 * o + 1) * w, (2 * o + 2) * w)
        hb0 = filt[0:1, b_sl]
        tre = ure[:, f_sl] + ure[:, b_sl] - hb0
        tim = uim[:, f_sl] - uim[:, b_sl]
        t_nyq = uim[0:1, f_sl] + uim[0:1, b_sl] - hb0
        first = (jnp.arange(length) == 0)[:, None]
        a = tre * scale
        bm = jnp.where(first, 0.0, tim * scale)
        c = bm
        d = jnp.where(first, t_nyq * scale, tre * scale)
        tables.append((a, bm, c, d))
    return tables


def _hyena_seq(mats, v, v_bf, x1, x2, lp, n_batch):
    length = v.shape[0]
    filt, filt_bf = _hy_filters(length, lp)
    ure, uim = _dft_fwd(mats, filt_bf)
    tables = _hy_tap_tables(ure, uim, filt, length)
    bias = lp['hy_bias'].astype(F32)
    y, y_bf = v, v_bf
    for o, gate in enumerate((x1, x2)):
        zre, zim = _dft_fwd(mats, y_bf, tables[o])
        bias_row = jnp.tile(bias[o][None, :], (1, n_batch))
        last = o == HY_ORDER - 1
        res = _dft_inv(mats, zre, zim, gate, y, bias_row, last)
        if last:
            return res[0]
        y, y_bf = res


def _lru_kernel(u_ref, g_ref, cw_ref, cb_ref, wa_ref, ba_ref, wx_ref, bx_ref, lam_ref, o_ref,
                pad_ref, y_ref, *, seq, ctx_len):
    tc = LRU_CHUNK
    halo = LRU_HALO
    width = LRU_WIDTH
    lat_off = halo
    ctx_off = 2 * halo + seq
    zero = jnp.zeros((halo, width), F32)
    pad_ref[0:halo, :] = zero
    pad_ref[lat_off:lat_off + seq, :] = u_ref[0, 0:seq, :]
    pad_ref[lat_off + seq:ctx_off, :] = zero
    pad_ref[ctx_off:ctx_off + ctx_len, :] = u_ref[0, seq:seq + ctx_len, :]
    pad_ref[ctx_off + ctx_len:ctx_off + ctx_len + halo, :] = zero
    row = lax.broadcasted_iota(I32, (tc, width), 0)
    n_win = tc + 2 * halo

    def chunk(pad_off, y_off, s, carry, d):
        wstart = pl.multiple_of(pad_off + s - halo, 8)
        win = pad_ref[pl.ds(wstart, n_win), :]
        cw = cw_ref[d]
        xc = cb_ref[d]
        for k in range(LRU_CONV):
            shift = (LRU_CONV - 1 - k) if d == 0 else -k
            rolled = win if shift == 0 else pltpu.roll(win, shift % n_win, axis=0)
            xc = xc + cw[k:k + 1] * rolled[halo:halo + tc]
        xb = xc.astype(BF16)
        r = _sigmoid(_dot(xb, wa_ref[d]) + ba_ref[d])
        gi = _sigmoid(_dot(xb, wx_ref[d]) + bx_ref[d])
        lam = lam_ref[d]
        softplus = jnp.maximum(-lam, 0.0) + jnp.log1p(jnp.exp(-jnp.abs(lam)))
        log_a = -LRU_C * r * softplus
        a = jnp.exp(log_a)
        bt = jnp.sqrt(-jnp.tanh(log_a) * (a * a + 1.0)) * (gi * xc)
        sft = 1
        while sft < tc:
            if d == 0:
                keep = row >= sft
                a_s = jnp.where(keep, pltpu.roll(a, sft, axis=0), 1.0)
                b_s = jnp.where(keep, pltpu.roll(bt, sft, axis=0), 0.0)
            else:
                keep = row < tc - sft
                a_s = jnp.where(keep, pltpu.roll(a, tc - sft, axis=0), 1.0)
                b_s = jnp.where(keep, pltpu.roll(bt, tc - sft, axis=0), 0.0)
            bt = a * b_s + bt
            a = a * a_s
            sft *= 2
        h = a * carry + bt
        yo = pl.multiple_of(y_off + s, 8)
        if d == 0:
            y_ref[pl.ds(yo, tc), :] = h
            return h[tc - 1:tc]
        y_ref[pl.ds(yo, tc), :] = y_ref[pl.ds(yo, tc), :] + h
        return h[0:1]

    n_lat = seq // tc
    n_ctx = ctx_len // tc
    for d in range(2):
        carry = jnp.zeros((1, width), F32)
        order = range(n_ctx) if d == 0 else range(n_ctx - 1, -1, -1)
        for c in order:
            carry = chunk(ctx_off, seq, c * tc, carry, d)

        def body(j, cr, d=d):
            jj = j if d == 0 else n_lat - 1 - j
            return chunk(lat_off, 0, jj * tc, cr, d)

        lax.fori_loop(0, n_lat, body, carry)
    o_ref[0] = (y_ref[...] * _gelu_tanh(g_ref[0])).astype(BF16)


def _block_diag(w):
    nd, nb, c, _ = w.shape
    out = jnp.zeros((nd, nb * c, nb * c), w.dtype)
    for n in range(nb):
        out = out.at[:, n * c:(n + 1) * c, n * c:(n + 1) * c].set(w[:, n])
    return out


def _lru_mixer(lu, lg, lp, seq):
    b, nt, w = lu.shape
    ctx_len = nt - seq
    row3 = lambda a: a.reshape(2, 1, w)
    args = (_pad_to(lp['lru_conv_w'], 8, 1), row3(lp['lru_conv_b']), _block_diag(lp['lru_wa']).astype(BF16),
            row3(lp['lru_ba']), _block_diag(lp['lru_wx']).astype(BF16), row3(lp['lru_bx']), row3(lp['lru_lambda']))
    full = lambda a: pl.BlockSpec(a.shape, lambda bb: (0,) * a.ndim)
    tok = pl.BlockSpec((1, nt, w), lambda bb: (bb, 0, 0))
    return pl.pallas_call(
        functools.partial(_lru_kernel, seq=seq, ctx_len=ctx_len),
        grid=(b,),
        in_specs=[tok, tok] + [full(a) for a in args],
        out_specs=tok,
        out_shape=jax.ShapeDtypeStruct((b, nt, w), BF16),
        scratch_shapes=[pltpu.VMEM((nt + 3 * LRU_HALO, w), F32), pltpu.VMEM((nt, w), F32)],
        compiler_params=_cparams("parallel"),
        name="rglru_scan",
    )(lu, lg, *args)


def _merge_kernel(x_ref, mod_ref, g1_ref, a_ref, b_ref, c_ref, d_ref, wg_ref, wa_ref, wb_ref, wc_ref, wd_ref,
                  wo_ref, o_ref):
    m = mod_ref[0, 0]
    x = x_ref[0]
    h = _normmod(x, g1_ref[...], m[1:2], m[0:1]).astype(BF16)
    dm = x.shape[-1]
    acc = None
    for k, (br, w) in enumerate(((a_ref, wa_ref), (b_ref, wb_ref), (c_ref, wc_ref), (d_ref, wd_ref))):
        gate = _sigmoid(_dot(h, wg_ref[:, k * dm:(k + 1) * dm]))
        term = gate * _dot(br[0], w[...])
        acc = term if acc is None else acc + term
    y = _dot(acc.astype(BF16), wo_ref[...])
    o_ref[0] = x + m[2:3] * y


def _merge(xa, modtab, g1, branches, w_gate, lp, n_lat_tiles):
    b, nt, d = xa.shape
    wbr = lp['w_branch']
    head_rows = lambda w, dv: jnp.concatenate(
        [_pad_to(w[hd * dv:(hd + 1) * dv], LANE, 0) for hd in range(4)], axis=0)
    weights = (head_rows(wbr[0], MLA_V).astype(BF16), head_rows(wbr[1], NA_HEAD_DIM).astype(BF16),
               wbr[2].astype(BF16), wbr[3].astype(BF16), lp['w_out'].astype(BF16))
    full = lambda a: pl.BlockSpec(a.shape, lambda bb, i: (0,) * a.ndim)
    tok = lambda n: pl.BlockSpec((1, TM, n), lambda bb, i: (bb, i, 0))
    return pl.pallas_call(
        _merge_kernel,
        grid=(b, nt // TM),
        in_specs=[tok(d), pl.BlockSpec((1, 1, 8, d), _kind_map(n_lat_tiles)), full(g1)]
                 + [tok(br.shape[-1]) for br in branches] + [full(w_gate)] + [full(w) for w in weights],
        out_specs=tok(d),
        out_shape=jax.ShapeDtypeStruct((b, nt, d), F32),
        compiler_params=_cparams("parallel", "parallel"),
        name="merge_branches",
    )(xa, modtab, g1, *branches, w_gate, *weights)


def _router_kernel(x_ref, mod_ref, g2_ref, rw_ref, rb_ref, tri_ref, h2_o, idx_o, wts_o, rank_o, cnt_o, carry):
    i = pl.program_id(0)

    @pl.when(i == 0)
    def _():
        carry[...] = jnp.zeros_like(carry)

    m = mod_ref[0, 0]
    h2 = _normmod(x_ref[0], g2_ref[...], m[4:5], m[3:4])
    h2_o[...] = h2
    scores = _sigmoid(_hp_dot(h2, rw_ref[...]))
    lane = lax.broadcasted_iota(I32, scores.shape, 1)
    biased = jnp.where(lane < N_EXPERTS, scores + rb_ref[...], -jnp.inf)
    picks = []
    onehot_all = jnp.zeros(scores.shape, F32)
    for _ in range(TOP_K):
        best = jnp.max(biased, axis=-1, keepdims=True)
        arg = jnp.min(jnp.where(biased == best, lane, LANE), axis=-1, keepdims=True)
        hit = lane == arg
        sel = jnp.sum(jnp.where(hit, scores, 0.0), axis=-1, keepdims=True)
        biased = jnp.where(hit, -jnp.inf, biased)
        onehot_all = onehot_all + jnp.where(hit, 1.0, 0.0)
        picks.append((arg, hit, sel))
    total = picks[0][2]
    for _, _, sel in picks[1:]:
        total = total + sel
    earlier = _dot(tri_ref[...], onehot_all.astype(BF16)) + carry[...]
    slot = lax.broadcasted_iota(I32, (scores.shape[0], TOPK_PAD), 1)
    idx = jnp.zeros(slot.shape, I32)
    wts = jnp.zeros(slot.shape, F32)
    rank = jnp.zeros(slot.shape, I32)
    for k, (arg, hit, sel) in enumerate(picks):
        rk = jnp.sum(jnp.where(hit, earlier, 0.0), axis=-1, keepdims=True)
        idx = jnp.where(slot == k, arg, idx)
        wts = jnp.where(slot == k, sel / total * ROUTED_SCALE, wts)
        rank = jnp.where(slot == k, rk.astype(I32), rank)
    idx_o[...] = idx
    wts_o[...] = wts
    rank_o[...] = rank
    carry[...] = carry[...] + jnp.sum(onehot_all, axis=0, keepdims=True)
    cnt_o[...] = carry[...]


def _route(x1, modtab, g2, lp, n_lat_tiles):
    b, nt, d = x1.shape
    tiles_b = nt // TM
    n_tiles = b * tiles_b
    t = b * nt
    rw = _pad_to(lp['router_w'], LANE, 1)
    rb = _pad_to(lp['router_bias'].reshape(1, -1), LANE, 1)
    tri = (np.arange(TM)[:, None] > np.arange(TM)[None, :]).astype(np.float32)
    tri = jnp.asarray(tri, BF16)
    flat = lambda n: pl.BlockSpec((TM, n), lambda i: (i, 0))
    full = lambda a: pl.BlockSpec(a.shape, lambda i: (0,) * a.ndim)
    return pl.pallas_call(
        _router_kernel,
        grid=(n_tiles,),
        in_specs=[pl.BlockSpec((1, TM, d), lambda i: (i // tiles_b, i % tiles_b, 0)),
                  pl.BlockSpec((1, 1, 8, d), lambda i: (i // tiles_b, jnp.where(i % tiles_b >= n_lat_tiles, 1, 0), 0, 0)),
                  full(g2), full(rw), full(rb), full(tri)],
        out_specs=[flat(d), flat(TOPK_PAD), flat(TOPK_PAD), flat(TOPK_PAD), pl.BlockSpec((1, LANE), lambda i: (0, 0))],
        out_shape=[jax.ShapeDtypeStruct((t, d), F32), jax.ShapeDtypeStruct((t, TOPK_PAD), I32),
                   jax.ShapeDtypeStruct((t, TOPK_PAD), F32), jax.ShapeDtypeStruct((t, TOPK_PAD), I32),
                   jax.ShapeDtypeStruct((1, LANE), F32)],
        scratch_shapes=[pltpu.VMEM((1, LANE), F32)],
        compiler_params=_cparams("arbitrary"),
        name="moe_router",
    )(x1, modtab, g2, rw, rb, tri)


def _row_copy(src, s_row, dst, d_row, sem):
    return pltpu.make_async_copy(src.at[pl.ds(s_row, 1)], dst.at[pl.ds(d_row, 1)], sem)


def _dispatch_kernel(dest_ref, h2_hbm, zeros_hbm, xs_hbm, sem):
    del zeros_hbm
    base = pl.program_id(0) * TM

    def issue(t, carry):
        for k in range(TOP_K):
            _row_copy(h2_hbm, base + t, xs_hbm, dest_ref[0, 0, t * TOP_K + k], sem).start()
        return carry

    lax.fori_loop(0, TM, issue, 0)

    def drain(t, carry):
        for k in range(TOP_K):
            _row_copy(h2_hbm, 0, xs_hbm, 0, sem).wait()
        return carry

    lax.fori_loop(0, TM, drain, 0)


def _dispatch(h2, dest, n_rows):
    t, d = h2.shape
    n_tiles = t // TM
    zeros = jnp.zeros((n_rows, d), F32)
    return pl.pallas_call(
        _dispatch_kernel,
        grid=(n_tiles,),
        in_specs=[pl.BlockSpec((1, 1, TM * TOP_K), lambda i: (i, 0, 0), memory_space=pltpu.SMEM),
                  pl.BlockSpec(memory_space=pl.ANY), pl.BlockSpec(memory_space=pl.ANY)],
        out_specs=pl.BlockSpec(memory_space=pl.ANY),
        out_shape=jax.ShapeDtypeStruct((n_rows, d), F32),
        scratch_shapes=[pltpu.SemaphoreType.DMA(())],
        input_output_aliases={2: 0},
        compiler_params=_cparams("arbitrary"),
        name="moe_dispatch",
    )(dest, h2, zeros)


def _expert_kernel(be_ref, nu_ref, xs_ref, wg_ref, wu_ref, wd_ref, ys_o, wg_s, wu_s, wd_s):
    i = pl.program_id(0)
    prev = be_ref[jnp.maximum(i - 1, 0)]

    @pl.when((i == 0) | (be_ref[i] != prev))
    def _():
        wg_s[...] = wg_ref[0].astype(BF16)
        wu_s[...] = wu_ref[0].astype(BF16)
        wd_s[...] = wd_ref[0].astype(BF16)

    @pl.when(i < nu_ref[0])
    def _():
        x = xs_ref[...].astype(BF16)
        hid = _silu(_dot(x, wg_s[...])) * _dot(x, wu_s[...])
        ys_o[...] = _dot(hid.astype(BF16), wd_s[...])

    @pl.when(i >= nu_ref[0])
    def _():
        ys_o[...] = jnp.zeros_like(ys_o)


def _experts(xs, block_e, n_used, lp):
    n_rows, d = xs.shape
    n_blocks = n_rows // MOE_BLOCK
    hid = EXPERT_HIDDEN
    grid_spec = pltpu.PrefetchScalarGridSpec(
        num_scalar_prefetch=2,
        grid=(n_blocks,),
        in_specs=[pl.BlockSpec((MOE_BLOCK, d), lambda i, be, nu: (i, 0)),
                  pl.BlockSpec((1, d, hid), lambda i, be, nu: (be[i], 0, 0)),
                  pl.BlockSpec((1, d, hid), lambda i, be, nu: (be[i], 0, 0)),
                  pl.BlockSpec((1, hid, d), lambda i, be, nu: (be[i], 0, 0))],
        out_specs=pl.BlockSpec((MOE_BLOCK, d), lambda i, be, nu: (i, 0)),
        scratch_shapes=[pltpu.VMEM((d, hid), BF16), pltpu.VMEM((d, hid), BF16), pltpu.VMEM((hid, d), BF16)],
    )
    return pl.pallas_call(
        _expert_kernel,
        grid_spec=grid_spec,
        out_shape=jax.ShapeDtypeStruct((n_rows, d), F32),
        compiler_params=_cparams("arbitrary"),
        name="moe_experts",
    )(block_e, n_used, xs, lp['exp_w_gate'], lp['exp_w_up'], lp['exp_w_down'])


def _combine_kernel(dest_ref, ys_hbm, wts_ref, h2_ref, x_ref, mod_ref, sg_ref, su_ref, sd_ref, gf_ref, o_ref,
                    buf, sem, *, final):
    def issue(t, carry):
        for k in range(TOP_K):
            _row_copy(ys_hbm, dest_ref[0, 0, t * TOP_K + k], buf.at[k], t, sem).start()
        return carry

    lax.fori_loop(0, TM, issue, 0)
    h2 = h2_ref[...].astype(BF16)
    hid = _silu(_dot(h2, sg_ref[...])) * _dot(h2, su_ref[...])
    f = _dot(hid.astype(BF16), sd_ref[...])

    def drain(t, carry):
        for k in range(TOP_K):
            _row_copy(ys_hbm, 0, buf.at[k], 0, sem).wait()
        return carry

    lax.fori_loop(0, TM, drain, 0)
    wts = wts_ref[...]
    for k in range(TOP_K):
        f = f + wts[:, k:k + 1] * buf[k]
    m = mod_ref[0, 0]
    x2 = x_ref[0] + m[5:6] * f
    if final:
        x2 = x2 * lax.rsqrt(jnp.mean(x2 * x2, axis=-1, keepdims=True) + NORM_EPS) * gf_ref[...]
    o_ref[0] = x2


def _combine(ys, dest, wts, h2, x1, modtab, lp, g_final, n_lat_tiles, final):
    b, nt, d = x1.shape
    tiles_b = nt // TM
    weights = (lp['sh_w_gate'].astype(BF16), lp['sh_w_up'].astype(BF16), lp['sh_w_down'].astype(BF16),
               g_final.reshape(1, -1))
    full = lambda a: pl.BlockSpec(a.shape, lambda i: (0,) * a.ndim)
    tok = pl.BlockSpec((1, TM, d), lambda i: (i // tiles_b, i % tiles_b, 0))
    return pl.pallas_call(
        functools.partial(_combine_kernel, final=final),
        grid=(b * tiles_b,),
        in_specs=[pl.BlockSpec((1, 1, TM * TOP_K), lambda i: (i, 0, 0), memory_space=pltpu.SMEM),
                  pl.BlockSpec(memory_space=pl.ANY),
                  pl.BlockSpec((TM, TOPK_PAD), lambda i: (i, 0)),
                  pl.BlockSpec((TM, d), lambda i: (i, 0)),
                  tok,
                  pl.BlockSpec((1, 1, 8, d), lambda i: (i // tiles_b, jnp.where(i % tiles_b >= n_lat_tiles, 1, 0), 0, 0))]
                 + [full(w) for w in weights],
        out_specs=tok,
        out_shape=jax.ShapeDtypeStruct((b, nt, d), F32),
        scratch_shapes=[pltpu.VMEM((TOP_K, TM, d), F32), pltpu.SemaphoreType.DMA(())],
        compiler_params=_cparams("arbitrary"),
        name="moe_combine",
    )(dest, ys, wts, h2, x1, modtab, *weights)


def _moe(x1, modtab, g2, lp, g_final, n_lat_tiles, final):
    b, nt, d = x1.shape
    t = b * nt
    h2, idx, wts, rank, cnt = _route(x1, modtab, g2, lp, n_lat_tiles)
    counts = cnt[0, :N_EXPERTS].astype(I32)
    padded = (counts + MOE_BLOCK - 1) // MOE_BLOCK * MOE_BLOCK
    p_ends = jnp.cumsum(padded)
    p_starts = p_ends - padded
    n_blocks = (t * TOP_K + N_EXPERTS * (MOE_BLOCK - 1) + MOE_BLOCK - 1) // MOE_BLOCK
    dest = (p_starts[idx[:, :TOP_K]] + rank[:, :TOP_K]).reshape(t // TM, 1, TM * TOP_K)
    block_e = jnp.minimum(jnp.searchsorted(p_ends, jnp.arange(n_blocks, dtype=I32) * MOE_BLOCK, side='right'),
                          N_EXPERTS - 1).astype(I32)
    n_used = (p_ends[-1:] // MOE_BLOCK).astype(I32)
    xs = _dispatch(h2, dest, n_blocks * MOE_BLOCK)
    ys = _experts(xs, block_e, n_used, lp)
    return _combine(ys, dest, wts, h2, x1, modtab, lp, g_final, n_lat_tiles, final)


def _layer(xa, c, c_ctx, lp, consts, g_final, seq, final):
    b, nt, d = xa.shape
    n_lat_tiles = seq // TM
    rope, mats_lat, mats_ctx = consts
    modtab = _mod_table(c, c_ctx, lp['w_mod'], lp['b_mod'])
    g1 = lp['g_norm1'].reshape(1, -1)
    g2 = lp['g_norm2'].reshape(1, -1)
    pw = _proj_weights(lp)
    q, k, v, nq, nk, nv, hy, lu, lg = _project(xa, modtab, g1, pw, rope, n_lat_tiles)
    br_a = _mla_attention(q, k, v, seq)
    br_b = _na_attention(nq, nk, nv, _na_bias_tables(lp['na_rpb'], seq // GRID_W), seq)
    pre = _hy_pre(hy, lp['hy_short_w'], lp['hy_short_b'], seq)
    lat_f, lat_bf, ctx_f, ctx_bf = pre
    c_lat = _hyena_seq(mats_lat, lat_f[0], lat_bf[0], lat_f[1], lat_f[2], lp, b)
    c_ctx_out = _hyena_seq(mats_ctx, ctx_f[0], ctx_bf[0], ctx_f[1], ctx_f[2], lp, b)
    br_c = jnp.concatenate([c_lat, c_ctx_out], axis=1)
    br_d = _lru_mixer(lu, lg, lp, seq)
    x1 = _merge(xa, modtab, g1, (br_a, br_b, br_c, br_d), pw['w_gate'], lp, n_lat_tiles)
    return _moe(x1, modtab, g2, lp, g_final, n_lat_tiles, final)


_LAYER_KEYS = ('w_mod', 'b_mod', 'g_norm1', 'g_norm2', 'w_in', 'mla_g_q', 'mla_w_uq', 'mla_g_kv', 'mla_w_ukv',
               'na_rpb', 'hy_short_w', 'hy_short_b', 'hy_w1', 'hy_b1', 'hy_w2', 'hy_b2', 'hy_w3', 'hy_decay',
               'hy_bias', 'lru_conv_w', 'lru_conv_b', 'lru_wa', 'lru_ba', 'lru_wx', 'lru_bx', 'lru_lambda',
               'w_branch', 'w_out', 'router_w', 'router_bias', 'exp_w_gate', 'exp_w_up', 'exp_w_down',
               'sh_w_gate', 'sh_w_up', 'sh_w_down')


def kernel(x, c, ctx, c_ctx, w_mod, b_mod, g_norm1, g_norm2, w_in, mla_g_q, mla_w_uq, mla_g_kv, mla_w_ukv, na_rpb, hy_short_w, hy_short_b, hy_w1, hy_b1, hy_w2, hy_b2, hy_w3, hy_decay, hy_bias, lru_conv_w, lru_conv_b, lru_wa, lru_ba, lru_wx, lru_bx, lru_lambda, w_branch, w_out, router_w, router_bias, exp_w_gate, exp_w_up, exp_w_down, sh_w_gate, sh_w_up, sh_w_down, g_final):
    stacked = dict(zip(_LAYER_KEYS, (w_mod, b_mod, g_norm1, g_norm2, w_in, mla_g_q, mla_w_uq, mla_g_kv, mla_w_ukv,
                                     na_rpb, hy_short_w, hy_short_b, hy_w1, hy_b1, hy_w2, hy_b2, hy_w3, hy_decay,
                                     hy_bias, lru_conv_w, lru_conv_b, lru_wa, lru_ba, lru_wx, lru_bx, lru_lambda,
                                     w_branch, w_out, router_w, router_bias, exp_w_gate, exp_w_up, exp_w_down,
                                     sh_w_gate, sh_w_up, sh_w_down)))
    b, seq, d = x.shape
    ctx_len = ctx.shape[1]
    depth = w_mod.shape[0]
    assert seq % TM == 0 and ctx_len % TM == 0 and seq // GRID_W >= NA_KEY_ROWS + 1
    xa = jnp.concatenate([x, ctx], axis=1)
    consts = (_rope_tables(seq, seq + ctx_len), _dft_matrices(seq), _dft_matrices(ctx_len))
    for i in range(depth):
        lp = {name: w[i] for name, w in stacked.items()}
        xa = _layer(xa, c, c_ctx, lp, consts, g_final, seq, i == depth - 1)
    return xa[:, :seq]
```

```python
import functools
import math

import numpy as np
import jax
import jax.numpy as jnp
from jax import lax
from jax.experimental import pallas as pl
from jax.experimental.pallas import tpu as pltpu

F32 = jnp.float32
BF16 = jnp.bfloat16
I32 = jnp.int32

TM = 256
LANE = 128
GRID_W = 64
N_MOD = 6
NORM_EPS = 1e-6

MLA_HEADS, MLA_NOPE, MLA_ROPE, MLA_V = 4, 64, 32, 64
MLA_Q_RANK, MLA_KV_RANK = 192, 128
MLA_Q_PAD = 256
ROPE_THETA = 10000.0

NA_HEADS, NA_HEAD_DIM, NA_WIN_R, NA_WIN_C = 4, 64, 8, 16
NA_TILE_ROWS = TM // GRID_W
NA_KEY_ROWS = NA_TILE_ROWS + NA_WIN_R - 1
NA_KEYS = NA_KEY_ROWS * GRID_W

HY_WIDTH, HY_ORDER, HY_SHORT, HY_BANDS, HY_FFN = 256, 2, 3, 16, 64
HY_EMB = 2 * HY_BANDS + 1

LRU_WIDTH, LRU_BLOCKS, LRU_CONV, LRU_C = 256, 4, 4, 8.0
LRU_CHUNK = 256
LRU_HALO = 8

N_EXPERTS, TOP_K, EXPERT_HIDDEN, ROUTED_SCALE, MOE_BLOCK = 64, 6, 256, 2.5, 256
TOPK_PAD = 8

VMEM_LIMIT = 52 * 1024 * 1024


def _cparams(*sem):
    return pltpu.CompilerParams(dimension_semantics=sem, vmem_limit_bytes=VMEM_LIMIT)


def _dot(a, b):
    return jnp.dot(a, b, preferred_element_type=F32)


def _dot_nt(a, b):
    return lax.dot_general(a, b, (((1,), (1,)), ((), ())), preferred_element_type=F32)


def _sigmoid(x):
    return 1.0 / (1.0 + jnp.exp(-x))


def _silu(x):
    return x * _sigmoid(x)


def _gelu_tanh(x):
    return 0.5 * x * (1.0 + jnp.tanh(math.sqrt(2.0 / math.pi) * (x + 0.044715 * (x * x * x))))


def _normmod(x, g, scale, shift):
    y = x * lax.rsqrt(jnp.mean(x * x, axis=-1, keepdims=True) + NORM_EPS) * g
    return y * (1.0 + scale) + shift


def _mod_kernel(c_ref, w_ref, b_ref, o_ref):
    s = _silu(c_ref[...])
    o_ref[...] = _dot(s.astype(BF16), w_ref[...].astype(BF16)) + b_ref[...]


def _mod_table(c, c_ctx, w_mod, b_mod):
    b, d = c.shape
    rows = 16
    cc = jnp.zeros((rows, d), F32).at[:b].set(c).at[b].set(c_ctx)
    tn = 1024
    mod = pl.pallas_call(
        _mod_kernel,
        grid=(N_MOD * d // tn,),
        in_specs=[pl.BlockSpec((rows, d), lambda j: (0, 0)),
                  pl.BlockSpec((d, tn), lambda j: (0, j)),
                  pl.BlockSpec((1, tn), lambda j: (0, j))],
        out_specs=pl.BlockSpec((rows, tn), lambda j: (0, j)),
        out_shape=jax.ShapeDtypeStruct((rows, N_MOD * d), F32),
        compiler_params=_cparams("arbitrary"),
        name="mod_vectors",
    )(cc, w_mod, b_mod.reshape(1, -1))
    lat = mod[:b].reshape(b, N_MOD, d)
    ctx = jnp.broadcast_to(mod[b].reshape(1, N_MOD, d), (b, N_MOD, d))
    tab = jnp.stack([lat, ctx], axis=1)
    return jnp.pad(tab, ((0, 0), (0, 0), (0, 8 - N_MOD), (0, 0)))


_C_QLAT, _C_KVLAT, _C_KR, _C_KRR, _C_NA, _C_HY, _C_LU, _C_LG, _C_END = (
    0, 256, 384, 512, 640, 640 + 3 * NA_HEADS * LANE, 640 + 1536 + 768, 640 + 1536 + 1024, 640 + 1536 + 1280)


def _proj_kernel(x_ref, mod_ref, g1_ref, w1_ref, gq_ref, gkv_ref, wuq_ref, wuqr_ref, wk_ref, wv_ref,
                 cos_ref, sin_ref, q_o, k_o, v_o, nq_o, nk_o, nv_o, hy_o, lu_o, lg_o):
    m = mod_ref[0, 0]
    h = _normmod(x_ref[0], g1_ref[...], m[1:2], m[0:1])
    z = _dot(h.astype(BF16), w1_ref[...])
    qlat = z[:, _C_QLAT:_C_KVLAT]
    kvlat = z[:, _C_KVLAT:_C_KR]
    kr = z[:, _C_KR:_C_KRR]
    krr = z[:, _C_KRR:_C_NA]
    qn = qlat * lax.rsqrt(jnp.sum(qlat * qlat, axis=-1, keepdims=True) * (1.0 / MLA_Q_RANK) + NORM_EPS) * gq_ref[...]
    kvn = kvlat * lax.rsqrt(jnp.mean(kvlat * kvlat, axis=-1, keepdims=True) + NORM_EPS) * gkv_ref[...]
    qn = qn.astype(BF16)
    kvn = kvn.astype(BF16)
    q = _dot(qn, wuq_ref[...])
    qr = _dot(qn, wuqr_ref[...])
    kk = _dot(kvn, wk_ref[...])
    vv = _dot(kvn, wv_ref[...])
    cos = cos_ref[...]
    sin = sin_ref[...]
    krope = kr * cos + krr * sin
    for hd in range(MLA_HEADS):
        sl = slice(hd * LANE, (hd + 1) * LANE)
        q_o[0, hd] = (q[:, sl] * cos + qr[:, sl] * sin).astype(BF16)
        k_o[0, hd] = (kk[:, sl] + krope).astype(BF16)
        v_o[0, hd] = vv[:, sl].astype(BF16)
    for hd in range(NA_HEADS):
        for which, ref in enumerate((nq_o, nk_o, nv_o)):
            lo = _C_NA + (which * NA_HEADS + hd) * LANE
            ref[0, hd] = z[:, lo:lo + LANE].astype(BF16)
    hy_o[0] = z[:, _C_HY:_C_LU]
    lu_o[0] = z[:, _C_LU:_C_LG]
    lg_o[0] = z[:, _C_LG:_C_END]


def _pad_to(a, n, axis):
    pad = [(0, 0)] * a.ndim
    pad[axis] = (0, n - a.shape[axis])
    return jnp.pad(a, pad)


def _rot_cols(w):
    half = w.shape[-1] // 2
    return jnp.concatenate([-w[..., half:], w[..., :half]], axis=-1)


def _head_blocks(cols_per_head):
    out = []
    for pieces in cols_per_head:
        k = pieces[0][0].shape[0]
        blk = jnp.zeros((k, LANE), F32)
        for arr, off in pieces:
            blk = blk.at[:, off:off + arr.shape[1]].set(arr)
        out.append(blk)
    return jnp.concatenate(out, axis=1)


def _proj_weights(lp):
    w_in = lp['w_in']
    d = w_in.shape[0]
    o = 0
    parts = {}
    for name, n in (('q', MLA_Q_RANK), ('kv', MLA_KV_RANK), ('kr', MLA_ROPE), ('na', 3 * NA_HEADS * NA_HEAD_DIM),
                    ('hy', 3 * HY_WIDTH), ('lu', LRU_WIDTH), ('lg', LRU_WIDTH), ('gt', 4 * d)):
        parts[name] = w_in[:, o:o + n]
        o += n
    zeros = lambda n: jnp.zeros((d, n), F32)
    kr_blk = jnp.concatenate([zeros(MLA_NOPE), parts['kr'], zeros(LANE - MLA_NOPE - MLA_ROPE)], axis=1)
    krr_blk = jnp.concatenate([zeros(MLA_NOPE), _rot_cols(parts['kr']), zeros(LANE - MLA_NOPE - MLA_ROPE)], axis=1)
    na_scale = NA_HEAD_DIM ** -0.5
    na_cols = []
    for which in range(3):
        for hd in range(NA_HEADS):
            lo = (which * NA_HEADS + hd) * NA_HEAD_DIM
            blk = parts['na'][:, lo:lo + NA_HEAD_DIM] * (na_scale if which == 0 else 1.0)
            na_cols.append(_pad_to(blk, LANE, 1))
    w1 = jnp.concatenate([_pad_to(parts['q'], MLA_Q_PAD, 1), parts['kv'], kr_blk, krr_blk] + na_cols
                         + [parts['hy'], parts['lu'], parts['lg']], axis=1).astype(BF16)
    mla_scale = (MLA_NOPE + MLA_ROPE) ** -0.5
    wuq = _pad_to(lp['mla_w_uq'], MLA_Q_PAD, 0) * mla_scale
    dq = MLA_NOPE + MLA_ROPE
    wuq_main = _head_blocks([[(wuq[:, hd * dq:hd * dq + dq], 0)] for hd in range(MLA_HEADS)])
    wuq_rot = _head_blocks([[(_rot_cols(wuq[:, hd * dq + MLA_NOPE:hd * dq + dq]), MLA_NOPE)]
                            for hd in range(MLA_HEADS)])
    dkv = MLA_NOPE + MLA_V
    wukv = lp['mla_w_ukv']
    wk = _head_blocks([[(wukv[:, hd * dkv:hd * dkv + MLA_NOPE], 0)] for hd in range(MLA_HEADS)])
    wv = _head_blocks([[(wukv[:, hd * dkv + MLA_NOPE:hd * dkv + dkv], 0)] for hd in range(MLA_HEADS)])
    gq = _pad_to(lp['mla_g_q'].reshape(1, -1), MLA_Q_PAD, 1)
    gkv = lp['mla_g_kv'].reshape(1, -1)
    return dict(w1=w1, w_gate=parts['gt'].astype(BF16), gq=gq, gkv=gkv, wuq=wuq_main.astype(BF16),
                wuq_rot=wuq_rot.astype(BF16), wk=wk.astype(BF16), wv=wv.astype(BF16))


def _rope_tables(seq, n_tok):
    t = jnp.arange(seq, dtype=I32)
    row = (t // GRID_W).astype(F32)
    col = (t % GRID_W).astype(F32)
    n_axis = MLA_ROPE // 4
    inv_freq = ROPE_THETA ** (-jnp.arange(n_axis, dtype=F32) / n_axis)
    ang = jnp.concatenate([row[:, None] * inv_freq, col[:, None] * inv_freq], axis=-1)
    cos = jnp.concatenate([jnp.cos(ang), jnp.cos(ang)], axis=-1)
    sin = jnp.concatenate([jnp.sin(ang), jnp.sin(ang)], axis=-1)
    cos_t = jnp.ones((n_tok, LANE), F32).at[:seq, MLA_NOPE:MLA_NOPE + MLA_ROPE].set(cos)
    sin_t = jnp.zeros((n_tok, LANE), F32).at[:seq, MLA_NOPE:MLA_NOPE + MLA_ROPE].set(sin)
    return cos_t, sin_t


def _kind_map(n_lat_tiles):
    return lambda b, i: (b, jnp.where(i >= n_lat_tiles, 1, 0), 0, 0)


def _project(xa, modtab, g1, pw, rope, n_lat_tiles):
    b, nt, d = xa.shape
    cos_t, sin_t = rope
    full = lambda a: pl.BlockSpec(a.shape, lambda bb, i: (0,) * a.ndim)
    head_out = lambda: pl.BlockSpec((1, MLA_HEADS, TM, LANE), lambda bb, i: (bb, 0, i, 0))
    tok_out = lambda n: pl.BlockSpec((1, TM, n), lambda bb, i: (bb, i, 0))
    head_shape = jax.ShapeDtypeStruct((b, MLA_HEADS, nt, LANE), BF16)
    tok_shape = lambda n: jax.ShapeDtypeStruct((b, nt, n), F32)
    weights = (g1, pw['w1'], pw['gq'], pw['gkv'], pw['wuq'], pw['wuq_rot'], pw['wk'], pw['wv'])
    return pl.pallas_call(
        _proj_kernel,
        grid=(b, nt // TM),
        in_specs=[pl.BlockSpec((1, TM, d), lambda bb, i: (bb, i, 0)),
                  pl.BlockSpec((1, 1, 8, d), _kind_map(n_lat_tiles))]
                 + [full(w) for w in weights]
                 + [pl.BlockSpec((TM, LANE), lambda bb, i: (i, 0))] * 2,
        out_specs=[head_out() for _ in range(6)] + [tok_out(3 * HY_WIDTH), tok_out(LRU_WIDTH), tok_out(LRU_WIDTH)],
        out_shape=[head_shape] * 6 + [tok_shape(3 * HY_WIDTH), tok_shape(LRU_WIDTH), tok_shape(LRU_WIDTH)],
        compiler_params=_cparams("parallel", "parallel"),
        name="input_projection",
    )(xa, modtab, *weights, cos_t, sin_t)


def _softmax_pv(parts):
    m = None
    for s, _ in parts:
        mm = jnp.max(s, axis=-1, keepdims=True)
        m = mm if m is None else jnp.maximum(m, mm)
    acc = None
    den = None
    for s, v in parts:
        p = jnp.exp(s - m)
        l = jnp.sum(p, axis=-1, keepdims=True)
        o = _dot(p.astype(BF16), v)
        acc = o if acc is None else acc + o
        den = l if den is None else den + l
    return acc / den


def _mla_kernel(q_ref, k_ref, v_ref, o_ref, *, seq, n_lat_tiles):
    i = pl.program_id(1)
    nt = k_ref.shape[2]

    def attend(lo, hi):
        for hd in range(MLA_HEADS):
            s = _dot_nt(q_ref[0, hd], k_ref[0, hd, lo:hi, :])
            o = _softmax_pv([(s, v_ref[0, hd, lo:hi, :])])
            o_ref[0, :, hd * LANE:(hd + 1) * LANE] = o.astype(BF16)

    @pl.when(i < n_lat_tiles)
    def _():
        attend(0, nt)

    @pl.when(i >= n_lat_tiles)
    def _():
        attend(seq, nt)


def _mla_attention(q, k, v, seq):
    b, h, nt, _ = q.shape
    kv_spec = pl.BlockSpec((1, h, nt, LANE), lambda bb, i: (bb, 0, 0, 0))
    return pl.pallas_call(
        functools.partial(_mla_kernel, seq=seq, n_lat_tiles=seq // TM),
        grid=(b, nt // TM),
        in_specs=[pl.BlockSpec((1, h, TM, LANE), lambda bb, i: (bb, 0, i, 0)), kv_spec, kv_spec],
        out_specs=pl.BlockSpec((1, TM, h * LANE), lambda bb, i: (bb, i, 0)),
        out_shape=jax.ShapeDtypeStruct((b, nt, h * LANE), BF16),
        compiler_params=_cparams("parallel", "arbitrary"),
        name="mla_attention",
    )(q, k, v)


def _na_bias_tables(rpb, rows):
    n_blk = rows // NA_TILE_ROWS
    col = np.arange(GRID_W)
    c0 = np.clip(col - NA_WIN_C // 2, 0, GRID_W - NA_WIN_C)
    in_win = (col[None, :] >= c0[:, None]) & (col[None, :] < c0[:, None] + NA_WIN_C)
    dc = np.clip(col[None, :] - col[:, None], 1 - NA_WIN_C, NA_WIN_C - 1) + NA_WIN_C - 1
    rpb = rpb.astype(F32)
    tables = []
    for j in (0, 1, n_blk - 1):
        w0 = min(max(NA_TILE_ROWS * j - NA_WIN_R // 2, 0), rows - NA_KEY_ROWS)
        r = NA_TILE_ROWS * j + np.arange(NA_TILE_ROWS)
        kr = w0 + np.arange(NA_KEY_ROWS)
        r0 = np.clip(r - NA_WIN_R // 2, 0, rows - NA_WIN_R)
        row_ok = (kr[None, :] >= r0[:, None]) & (kr[None, :] < r0[:, None] + NA_WIN_R)
        dr = np.clip(kr[None, :] - r[:, None] + NA_WIN_R - 1, 0, 2 * NA_WIN_R - 2)
        bias = rpb[:, dr][:, :, :, dc]
        mask = row_ok[:, :, None, None] & in_win[None, None, :, :]
        bias = jnp.where(jnp.asarray(mask)[None], bias, -jnp.inf)
        tables.append(bias.transpose(0, 1, 3, 2, 4).reshape(NA_HEADS, TM, NA_KEYS))
    return jnp.stack(tables)


def _na_kernel(q_ref, k_ref, v_ref, bias_ref, o_ref, *, seq, n_lat_tiles):
    i = pl.program_id(1)
    nt = k_ref.shape[2]
    rows = seq // GRID_W

    @pl.when(i < n_lat_tiles)
    def _():
        w0 = jnp.clip(NA_TILE_ROWS * i - NA_WIN_R // 2, 0, rows - NA_KEY_ROWS)
        start = pl.multiple_of(w0 * GRID_W, GRID_W)
        for hd in range(NA_HEADS):
            q = q_ref[0, hd]
            s_loc = _dot_nt(q, k_ref[0, hd, pl.ds(start, NA_KEYS), :]) + bias_ref[0, hd]
            s_ctx = _dot_nt(q, k_ref[0, hd, seq:nt, :])
            o = _softmax_pv([(s_loc, v_ref[0, hd, pl.ds(start, NA_KEYS), :]), (s_ctx, v_ref[0, hd, seq:nt, :])])
            o_ref[0, :, hd * LANE:(hd + 1) * LANE] = o.astype(BF16)

    @pl.when(i >= n_lat_tiles)
    def _():
        for hd in range(NA_HEADS):
            s = _dot_nt(q_ref[0, hd], k_ref[0, hd, seq:nt, :])
            o = _softmax_pv([(s, v_ref[0, hd, seq:nt, :])])
            o_ref[0, :, hd * LANE:(hd + 1) * LANE] = o.astype(BF16)


def _na_attention(q, k, v, bias, seq):
    b, h, nt, _ = q.shape
    n_lat = seq // TM
    kv_spec = pl.BlockSpec((1, h, nt, LANE), lambda bb, i: (bb, 0, 0, 0))
    cfg = lambda bb, i: (jnp.where(i == 0, 0, jnp.where(i >= n_lat - 1, 2, 1)), 0, 0, 0)
    return pl.pallas_call(
        functools.partial(_na_kernel, seq=seq, n_lat_tiles=n_lat),
        grid=(b, nt // TM),
        in_specs=[pl.BlockSpec((1, h, TM, LANE), lambda bb, i: (bb, 0, i, 0)), kv_spec, kv_spec,
                  pl.BlockSpec((1, h, TM, NA_KEYS), cfg)],
        out_specs=pl.BlockSpec((1, TM, h * LANE), lambda bb, i: (bb, i, 0)),
        out_shape=jax.ShapeDtypeStruct((b, nt, h * LANE), BF16),
        compiler_params=_cparams("parallel", "arbitrary"),
        name="neighbourhood_attention",
    )(q, k, v, bias)


HY_PRE_CHUNK = 256


def _hy_pre_kernel(z_ref, w_ref, b_ref, lat_o, lat_bf_o, ctx_o, ctx_bf_o, *, seq, ctx_len):
    w = w_ref[...]
    bias = b_ref[...]
    ch = HY_PRE_CHUNK
    zero = jnp.zeros((8, HY_WIDTH), F32)
    for seg_lo, seg_len, o_ref, obf_ref in ((0, seq, lat_o, lat_bf_o), (seq, ctx_len, ctx_o, ctx_bf_o)):
        for c in range(seg_len // ch):
            s = seg_lo + c * ch
            before = zero if c == 0 else z_ref[0, s - 8:s, :]
            after = zero if c == seg_len // ch - 1 else z_ref[0, s + ch:s + ch + 8, :]
            win = jnp.concatenate([before, z_ref[0, s:s + ch, :], after], axis=0)
            n = ch + 16
            y = bias + w[1:2] * win[8:8 + ch]
            y = y + w[0:1] * pltpu.roll(win, 1, axis=0)[8:8 + ch]
            y = y + w[2:3] * pltpu.roll(win, n - 1, axis=0)[8:8 + ch]
            o_ref[0, c * ch:(c + 1) * ch, :] = y
            obf_ref[0, c * ch:(c + 1) * ch, :] = y.astype(BF16)


def _hy_pre(hy, w, bvec, seq):
    b, nt, _ = hy.shape
    ctx_len = nt - seq
    wpad = _pad_to(w, 8, 0)
    out_specs, out_shape = [], []
    for length in (seq, ctx_len):
        for dt in (F32, BF16):
            out_specs.append(pl.BlockSpec((1, length, HY_WIDTH), lambda bb, g: (g, 0, bb)))
            out_shape.append(jax.ShapeDtypeStruct((3, length, b * HY_WIDTH), dt))
    return pl.pallas_call(
        functools.partial(_hy_pre_kernel, seq=seq, ctx_len=ctx_len),
        grid=(b, 3),
        in_specs=[pl.BlockSpec((1, nt, HY_WIDTH), lambda bb, g: (bb, 0, g)),
                  pl.BlockSpec((8, HY_WIDTH), lambda bb, g: (0, g)),
                  pl.BlockSpec((1, HY_WIDTH), lambda bb, g: (0, g))],
        out_specs=out_specs,
        out_shape=out_shape,
        compiler_params=_cparams("parallel", "parallel"),
        name="hyena_short_conv",
    )(hy, wpad, bvec.reshape(1, -1))


def _hp_dot(a, b):
    return jnp.dot(a, b, preferred_element_type=F32, precision=lax.Precision.HIGHEST)


def _hy_filter_kernel(f_ref, w1_ref, b1_ref, w2_ref, b2_ref, w3_ref, dec_ref, o_ref, obf_ref):
    f = f_ref[...]
    h = jnp.sin(_hp_dot(f, w1_ref[...]) + b1_ref[...])
    h = jnp.sin(_hp_dot(h, w2_ref[...]) + b2_ref[...])
    h = _hp_dot(h, w3_ref[...])
    h = h * jnp.exp(-f[:, 0:1] * jnp.abs(dec_ref[...]))
    o_ref[...] = h
    obf_ref[...] = h.astype(BF16)


def _hy_pos_features(length):
    t = jnp.linspace(0.0, 1.0, length, dtype=F32)[:, None]
    w = 2.0 * math.pi * jnp.arange(length, dtype=F32)[:, None] / length
    f = jnp.linspace(1e-4, HY_BANDS - 1, HY_BANDS, dtype=F32)[None, :]
    z = w * f
    return jnp.concatenate([t, jnp.cos(z), -jnp.sin(z)], axis=-1)


def _hy_filters(length, lp):
    feats = _hy_pos_features(length)
    n_out = HY_ORDER * 2 * HY_WIDTH
    tl = min(length, 512)
    full = lambda a: pl.BlockSpec(a.shape, lambda i: (0,) * a.ndim)
    args = (lp['hy_w1'], lp['hy_b1'].reshape(1, -1), lp['hy_w2'], lp['hy_b2'].reshape(1, -1), lp['hy_w3'],
            lp['hy_decay'].reshape(1, n_out))
    return pl.pallas_call(
        _hy_filter_kernel,
        grid=(length // tl,),
        in_specs=[pl.BlockSpec((tl, HY_EMB), lambda i: (i, 0))] + [full(a) for a in args],
        out_specs=[pl.BlockSpec((tl, n_out), lambda i: (i, 0))] * 2,
        out_shape=[jax.ShapeDtypeStruct((length, n_out), F32), jax.ShapeDtypeStruct((length, n_out), BF16)],
        compiler_params=_cparams("parallel"),
        name="hyena_filter_mlp",
    )(feats, *args)


def _dft_matrices(length):
    n2 = 2 * length
    step = 64
    n = jnp.arange(length, dtype=I32)

    def trig(kv):
        ang = ((kv[:, None] * n[None, :]) % n2).astype(F32) * (2.0 * math.pi / n2)
        return jnp.cos(ang), jnp.sin(ang)

    c1, s1 = trig(jnp.arange(length // step, dtype=I32) * step)
    c2, s2 = trig(jnp.arange(step, dtype=I32))
    cos = (c1[:, None, :] * c2[None] - s1[:, None, :] * s2[None]).reshape(length, length)
    sin = (s1[:, None, :] * c2[None] + c1[:, None, :] * s2[None]).reshape(length, length)
    nyq = jnp.where(n % 2 == 0, 1.0, -1.0).astype(F32)
    ms = (-sin).at[0, :].set(nyq)
    mst = (-sin).at[:, 0].set(nyq)
    return cos.astype(BF16), ms.astype(BF16), mst.astype(BF16)


def _dft_fwd_kernel(mc_ref, ms_ref, x_ref, *rest, with_taps):
    ure = _dot(mc_ref[...], x_ref[...])
    uim = _dot(ms_ref[...], x_ref[...])
    if not with_taps:
        ure_o, uim_o = rest
        ure_o[...] = ure
        uim_o[...] = uim
        return
    a_ref, b_ref, c_ref, d_ref, zre_o, zim_o = rest
    a, b, c, d = a_ref[...], b_ref[...], c_ref[...], d_ref[...]
    for bb in range(x_ref.shape[1] // HY_WIDTH):
        sl = slice(bb * HY_WIDTH, (bb + 1) * HY_WIDTH)
        zre_o[:, sl] = (ure[:, sl] * a - uim[:, sl] * b).astype(BF16)
        zim_o[:, sl] = (ure[:, sl] * c + uim[:, sl] * d).astype(BF16)


def _col_block(nc, cap):
    return min(nc, cap)


def _dft_fwd(mats, x, taps=None):
    mc, ms, _ = mats
    length, nc = x.shape
    tk = min(length, 256)
    cb = _col_block(nc, 1024)
    grid = (nc // cb, length // tk)
    m_spec = pl.BlockSpec((tk, length), lambda c, j: (j, 0))
    x_spec = pl.BlockSpec((length, cb), lambda c, j: (0, c))
    o_spec = pl.BlockSpec((tk, cb), lambda c, j: (j, c))
    if taps is None:
        return pl.pallas_call(
            functools.partial(_dft_fwd_kernel, with_taps=False),
            grid=grid, in_specs=[m_spec, m_spec, x_spec], out_specs=[o_spec, o_spec],
            out_shape=[jax.ShapeDtypeStruct((length, nc), F32)] * 2,
            compiler_params=_cparams("parallel", "arbitrary"),
            name="hyena_dft_filters",
        )(mc, ms, x)
    t_spec = pl.BlockSpec((tk, HY_WIDTH), lambda c, j: (j, 0))
    return pl.pallas_call(
        functools.partial(_dft_fwd_kernel, with_taps=True),
        grid=grid, in_specs=[m_spec, m_spec, x_spec] + [t_spec] * 4, out_specs=[o_spec, o_spec],
        out_shape=[jax.ShapeDtypeStruct((length, nc), BF16)] * 2,
        compiler_params=_cparams("parallel", "arbitrary"),
        name="hyena_dft_forward",
    )(mc, ms, x, *taps)


def _dft_inv_kernel(mc_ref, mst_ref, zre_ref, zim_ref, gate_ref, prev_ref, bias_ref, *outs, last):
    conv = _dot(mc_ref[...], zre_ref[...]) + _dot(mst_ref[...], zim_ref[...])
    y = gate_ref[...] * (conv + prev_ref[...] * bias_ref[...])
    if last:
        (tok_o,) = outs
        for bb in range(y.shape[1] // HY_WIDTH):
            tok_o[bb] = y[:, bb * HY_WIDTH:(bb + 1) * HY_WIDTH].astype(BF16)
    else:
        y_o, ybf_o = outs
        y_o[...] = y
        ybf_o[...] = y.astype(BF16)


def _dft_inv(mats, zre, zim, gate, prev, bias_row, last):
    mc, _, mst = mats
    length, nc = zre.shape
    tm = min(length, 256)
    cb = _col_block(nc, 512)
    grid = (nc // cb, length // tm)
    m_spec = pl.BlockSpec((tm, length), lambda c, i: (i, 0))
    z_spec = pl.BlockSpec((length, cb), lambda c, i: (0, c))
    e_spec = pl.BlockSpec((tm, cb), lambda c, i: (i, c))
    b_spec = pl.BlockSpec((1, cb), lambda c, i: (0, c))
    if last:
        nb = nc // HY_WIDTH
        out_specs = [pl.BlockSpec((cb // HY_WIDTH, tm, HY_WIDTH), lambda c, i: (c, i, 0))]
        out_shape = [jax.ShapeDtypeStruct((nb, length, HY_WIDTH), BF16)]
    else:
        out_specs = [e_spec, e_spec]
        out_shape = [jax.ShapeDtypeStruct((length, nc), F32), jax.ShapeDtypeStruct((length, nc), BF16)]
    return pl.pallas_call(
        functools.partial(_dft_inv_kernel, last=last),
        grid=grid, in_specs=[m_spec, m_spec, z_spec, z_spec, e_spec, e_spec, b_spec],
        out_specs=out_specs, out_shape=out_shape,
        compiler_params=_cparams("parallel", "arbitrary"),
        name="hyena_dft_inverse",
    )(mc, mst, zre, zim, gate, prev, bias_row)


def _hy_tap_tables(ure, uim, filt, length):
    n2 = 2 * length
    w = HY_WIDTH
    scale = jnp.full((length, 1), 2.0 / n2, F32).at[0, 0].set(1.0 / n2)
    tables = []
    for o in range(HY_ORDER):
        f_sl = slice((2 * o) * w, (2 * o + 1) * w)
        b_sl = slice((2 * o + 1) * w, (2 * o + 2) * w)
        hb0 = filt[0:1, b_sl]
        tre = ure[:, f_sl] + ure[:, b_sl] - hb0
        tim = uim[:, f_sl] - uim[:, b_sl]
        t_nyq = uim[0:1, f_sl] + uim[0:1, b_sl] - hb0
        first = (jnp.arange(length) == 0)[:, None]
        a = tre * scale
        bm = jnp.where(first, 0.0, tim * scale)
        c = bm
        d = jnp.where(first, t_nyq * scale, tre * scale)
        tables.append((a, bm, c, d))
    return tables


def _hyena_seq(mats, v, v_bf, x1, x2, lp, n_batch):
    length = v.shape[0]
    filt, filt_bf = _hy_filters(length, lp)
    ure, uim = _dft_fwd(mats, filt_bf)
    tables = _hy_tap_tables(ure, uim, filt, length)
    bias = lp['hy_bias'].astype(F32)
    y, y_bf = v, v_bf
    for o, gate in enumerate((x1, x2)):
        zre, zim = _dft_fwd(mats, y_bf, tables[o])
        bias_row = jnp.tile(bias[o][None, :], (1, n_batch))
        last = o == HY_ORDER - 1
        res = _dft_inv(mats, zre, zim, gate, y, bias_row, last)
        if last:
            return res[0]
        y, y_bf = res


def _lru_kernel(u_ref, g_ref, cw_ref, cb_ref, wa_ref, ba_ref, wx_ref, bx_ref, lam_ref, o_ref,
                pad_ref, y_ref, *, seq, ctx_len):
    tc = LRU_CHUNK
    halo = LRU_HALO
    width = LRU_WIDTH
    lat_off = halo
    ctx_off = 2 * halo + seq
    zero = jnp.zeros((halo, width), F32)
    pad_ref[0:halo, :] = zero
    pad_ref[lat_off:lat_off + seq, :] = u_ref[0, 0:seq, :]
    pad_ref[lat_off + seq:ctx_off, :] = zero
    pad_ref[ctx_off:ctx_off + ctx_len, :] = u_ref[0, seq:seq + ctx_len, :]
    pad_ref[ctx_off + ctx_len:ctx_off + ctx_len + halo, :] = zero
    row = lax.broadcasted_iota(I32, (tc, width), 0)
    n_win = tc + 2 * halo

    def chunk(pad_off, y_off, s, carry, d):
        wstart = pl.multiple_of(pad_off + s - halo, 8)
        win = pad_ref[pl.ds(wstart, n_win), :]
        cw = cw_ref[d]
        xc = cb_ref[d]
        for k in range(LRU_CONV):
            shift = (LRU_CONV - 1 - k) if d == 0 else -k
            rolled = win if shift == 0 else pltpu.roll(win, shift % n_win, axis=0)
            xc = xc + cw[k:k + 1] * rolled[halo:halo + tc]
        xb = xc.astype(BF16)
        r = _sigmoid(_dot(xb, wa_ref[d]) + ba_ref[d])
        gi = _sigmoid(_dot(xb, wx_ref[d]) + bx_ref[d])
        lam = lam_ref[d]
        softplus = jnp.maximum(-lam, 0.0) + jnp.log1p(jnp.exp(-jnp.abs(lam)))
        log_a = -LRU_C * r * softplus
        a = jnp.exp(log_a)
        bt = jnp.sqrt(-jnp.tanh(log_a) * (a * a + 1.0)) * (gi * xc)
        sft = 1
        while sft < tc:
            if d == 0:
                keep = row >= sft
                a_s = jnp.where(keep, pltpu.roll(a, sft, axis=0), 1.0)
                b_s = jnp.where(keep, pltpu.roll(bt, sft, axis=0), 0.0)
            else:
                keep = row < tc - sft
                a_s = jnp.where(keep, pltpu.roll(a, tc - sft, axis=0), 1.0)
                b_s = jnp.where(keep, pltpu.roll(bt, tc - sft, axis=0), 0.0)
            bt = a * b_s + bt
            a = a * a_s
            sft *= 2
        h = a * carry + bt
        yo = pl.multiple_of(y_off + s, 8)
        if d == 0:
            y_ref[pl.ds(yo, tc), :] = h
            return h[tc - 1:tc]
        y_ref[pl.ds(yo, tc), :] = y_ref[pl.ds(yo, tc), :] + h
        return h[0:1]

    n_lat = seq // tc
    n_ctx = ctx_len // tc
    for d in range(2):
        carry = jnp.zeros((1, width), F32)
        order = range(n_ctx) if d == 0 else range(n_ctx - 1, -1, -1)
        for c in order:
            carry = chunk(ctx_off, seq, c * tc, carry, d)

        def body(j, cr, d=d):
            jj = j if d == 0 else n_lat - 1 - j
            return chunk(lat_off, 0, jj * tc, cr, d)

        lax.fori_loop(0, n_lat, body, carry)
    o_ref[0] = (y_ref[...] * _gelu_tanh(g_ref[0])).astype(BF16)


def _block_diag(w):
    nd, nb, c, _ = w.shape
    out = jnp.zeros((nd, nb * c, nb * c), w.dtype)
    for n in range(nb):
        out = out.at[:, n * c:(n + 1) * c, n * c:(n + 1) * c].set(w[:, n])
    return out


def _lru_mixer(lu, lg, lp, seq):
    b, nt, w = lu.shape
    ctx_len = nt - seq
    row3 = lambda a: a.reshape(2, 1, w)
    args = (_pad_to(lp['lru_conv_w'], 8, 1), row3(lp['lru_conv_b']), _block_diag(lp['lru_wa']).astype(BF16),
            row3(lp['lru_ba']), _block_diag(lp['lru_wx']).astype(BF16), row3(lp['lru_bx']), row3(lp['lru_lambda']))
    full = lambda a: pl.BlockSpec(a.shape, lambda bb: (0,) * a.ndim)
    tok = pl.BlockSpec((1, nt, w), lambda bb: (bb, 0, 0))
    return pl.pallas_call(
        functools.partial(_lru_kernel, seq=seq, ctx_len=ctx_len),
        grid=(b,),
        in_specs=[tok, tok] + [full(a) for a in args],
        out_specs=tok,
        out_shape=jax.ShapeDtypeStruct((b, nt, w), BF16),
        scratch_shapes=[pltpu.VMEM((nt + 3 * LRU_HALO, w), F32), pltpu.VMEM((nt, w), F32)],
        compiler_params=_cparams("parallel"),
        name="rglru_scan",
    )(lu, lg, *args)


def _merge_kernel(x_ref, mod_ref, g1_ref, a_ref, b_ref, c_ref, d_ref, wg_ref, wa_ref, wb_ref, wc_ref, wd_ref,
                  wo_ref, o_ref):
    m = mod_ref[0, 0]
    x = x_ref[0]
    h = _normmod(x, g1_ref[...], m[1:2], m[0:1]).astype(BF16)
    dm = x.shape[-1]
    acc = None
    for k, (br, w) in enumerate(((a_ref, wa_ref), (b_ref, wb_ref), (c_ref, wc_ref), (d_ref, wd_ref))):
        gate = _sigmoid(_dot(h, wg_ref[:, k * dm:(k + 1) * dm]))
        term = gate * _dot(br[0], w[...])
        acc = term if acc is None else acc + term
    y = _dot(acc.astype(BF16), wo_ref[...])
    o_ref[0] = x + m[2:3] * y


def _merge(xa, modtab, g1, branches, w_gate, lp, n_lat_tiles):
    b, nt, d = xa.shape
    wbr = lp['w_branch']
    head_rows = lambda w, dv: jnp.concatenate(
        [_pad_to(w[hd * dv:(hd + 1) * dv], LANE, 0) for hd in range(4)], axis=0)
    weights = (head_rows(wbr[0], MLA_V).astype(BF16), head_rows(wbr[1], NA_HEAD_DIM).astype(BF16),
               wbr[2].astype(BF16), wbr[3].astype(BF16), lp['w_out'].astype(BF16))
    full = lambda a: pl.BlockSpec(a.shape, lambda bb, i: (0,) * a.ndim)
    tok = lambda n: pl.BlockSpec((1, TM, n), lambda bb, i: (bb, i, 0))
    return pl.pallas_call(
        _merge_kernel,
        grid=(b, nt // TM),
        in_specs=[tok(d), pl.BlockSpec((1, 1, 8, d), _kind_map(n_lat_tiles)), full(g1)]
                 + [tok(br.shape[-1]) for br in branches] + [full(w_gate)] + [full(w) for w in weights],
        out_specs=tok(d),
        out_shape=jax.ShapeDtypeStruct((b, nt, d), F32),
        compiler_params=_cparams("parallel", "parallel"),
        name="merge_branches",
    )(xa, modtab, g1, *branches, w_gate, *weights)


def _router_kernel(x_ref, mod_ref, g2_ref, rw_ref, rb_ref, tri_ref, h2_o, idx_o, wts_o, rank_o, cnt_o, carry):
    i = pl.program_id(0)

    @pl.when(i == 0)
    def _():
        carry[...] = jnp.zeros_like(carry)

    m = mod_ref[0, 0]
    h2 = _normmod(x_ref[0], g2_ref[...], m[4:5], m[3:4])
    h2_o[...] = h2
    scores = _sigmoid(_hp_dot(h2, rw_ref[...]))
    lane = lax.broadcasted_iota(I32, scores.shape, 1)
    biased = jnp.where(lane < N_EXPERTS, scores + rb_ref[...], -jnp.inf)
    picks = []
    onehot_all = jnp.zeros(scores.shape, F32)
    for _ in range(TOP_K):
        best = jnp.max(biased, axis=-1, keepdims=True)
        arg = jnp.min(jnp.where(biased == best, lane, LANE), axis=-1, keepdims=True)
        hit = lane == arg
        sel = jnp.sum(jnp.where(hit, scores, 0.0), axis=-1, keepdims=True)
        biased = jnp.where(hit, -jnp.inf, biased)
        onehot_all = onehot_all + jnp.where(hit, 1.0, 0.0)
        picks.append((arg, hit, sel))
    total = picks[0][2]
    for _, _, sel in picks[1:]:
        total = total + sel
    earlier = _dot(tri_ref[...], onehot_all.astype(BF16)) + carry[...]
    slot = lax.broadcasted_iota(I32, (scores.shape[0], TOPK_PAD), 1)
    idx = jnp.zeros(slot.shape, I32)
    wts = jnp.zeros(slot.shape, F32)
    rank = jnp.zeros(slot.shape, I32)
    for k, (arg, hit, sel) in enumerate(picks):
        rk = jnp.sum(jnp.where(hit, earlier, 0.0), axis=-1, keepdims=True)
        idx = jnp.where(slot == k, arg, idx)
        wts = jnp.where(slot == k, sel / total * ROUTED_SCALE, wts)
        rank = jnp.where(slot == k, rk.astype(I32), rank)
    idx_o[...] = idx
    wts_o[...] = wts
    rank_o[...] = rank
    carry[...] = carry[...] + jnp.sum(onehot_all, axis=0, keepdims=True)
    cnt_o[...] = carry[...]


def _route(x1, modtab, g2, lp, n_lat_tiles):
    b, nt, d = x1.shape
    tiles_b = nt // TM
    n_tiles = b * tiles_b
    t = b * nt
    rw = _pad_to(lp['router_w'], LANE, 1)
    rb = _pad_to(lp['router_bias'].reshape(1, -1), LANE, 1)
    tri = (np.arange(TM)[:, None] > np.arange(TM)[None, :]).astype(np.float32)
    tri = jnp.asarray(tri, BF16)
    flat = lambda n: pl.BlockSpec((TM, n), lambda i: (i, 0))
    full = lambda a: pl.BlockSpec(a.shape, lambda i: (0,) * a.ndim)
    return pl.pallas_call(
        _router_kernel,
        grid=(n_tiles,),
        in_specs=[pl.BlockSpec((1, TM, d), lambda i: (i // tiles_b, i % tiles_b, 0)),
                  pl.BlockSpec((1, 1, 8, d), lambda i: (i // tiles_b, jnp.where(i % tiles_b >= n_lat_tiles, 1, 0), 0, 0)),
                  full(g2), full(rw), full(rb), full(tri)],
        out_specs=[flat(d), flat(TOPK_PAD), flat(TOPK_PAD), flat(TOPK_PAD), pl.BlockSpec((1, LANE), lambda i: (0, 0))],
        out_shape=[jax.ShapeDtypeStruct((t, d), F32), jax.ShapeDtypeStruct((t, TOPK_PAD), I32),
                   jax.ShapeDtypeStruct((t, TOPK_PAD), F32), jax.ShapeDtypeStruct((t, TOPK_PAD), I32),
                   jax.ShapeDtypeStruct((1, LANE), F32)],
        scratch_shapes=[pltpu.VMEM((1, LANE), F32)],
        compiler_params=_cparams("arbitrary"),
        name="moe_router",
    )(x1, modtab, g2, rw, rb, tri)


def _row_copy(src, s_row, dst, d_row, sem):
    return pltpu.make_async_copy(src.at[pl.ds(s_row, 1)], dst.at[pl.ds(d_row, 1)], sem)


def _dispatch_kernel(dest_ref, h2_ref, zeros_hbm, xs_hbm, sem):
    del zeros_hbm

    def issue(t, carry):
        for k in range(TOP_K):
            _row_copy(h2_ref, t, xs_hbm, dest_ref[0, 0, t * TOP_K + k], sem).start()
        return carry

    lax.fori_loop(0, TM, issue, 0)

    def drain(t, carry):
        for k in range(TOP_K):
            _row_copy(h2_ref, 0, xs_hbm, 0, sem).wait()
        return carry

    lax.fori_loop(0, TM, drain, 0)


def _dispatch(h2, dest, n_rows):
    t, d = h2.shape
    n_tiles = t // TM
    zeros = jnp.zeros((n_rows, d), F32)
    return pl.pallas_call(
        _dispatch_kernel,
        grid=(n_tiles,),
        in_specs=[pl.BlockSpec((1, 1, TM * TOP_K), lambda i: (i, 0, 0), memory_space=pltpu.SMEM),
                  pl.BlockSpec((TM, d), lambda i: (i, 0)), pl.BlockSpec(memory_space=pl.ANY)],
        out_specs=pl.BlockSpec(memory_space=pl.ANY),
        out_shape=jax.ShapeDtypeStruct((n_rows, d), F32),
        scratch_shapes=[pltpu.SemaphoreType.DMA(())],
        input_output_aliases={2: 0},
        compiler_params=_cparams("arbitrary"),
        name="moe_dispatch",
    )(dest, h2, zeros)


def _expert_kernel(be_ref, nu_ref, xs_ref, wg_ref, wu_ref, wd_ref, ys_o, wg_s, wu_s, wd_s):
    i = pl.program_id(0)
    prev = be_ref[jnp.maximum(i - 1, 0)]

    @pl.when((i == 0) | (be_ref[i] != prev))
    def _():
        wg_s[...] = wg_ref[0].astype(BF16)
        wu_s[...] = wu_ref[0].astype(BF16)
        wd_s[...] = wd_ref[0].astype(BF16)

    @pl.when(i < nu_ref[0])
    def _():
        x = xs_ref[...].astype(BF16)
        hid = _silu(_dot(x, wg_s[...])) * _dot(x, wu_s[...])
        ys_o[...] = _dot(hid.astype(BF16), wd_s[...])

    @pl.when(i >= nu_ref[0])
    def _():
        ys_o[...] = jnp.zeros_like(ys_o)


def _experts(xs, block_e, n_used, lp):
    n_rows, d = xs.shape
    n_blocks = n_rows // MOE_BLOCK
    hid = EXPERT_HIDDEN
    grid_spec = pltpu.PrefetchScalarGridSpec(
        num_scalar_prefetch=2,
        grid=(n_blocks,),
        in_specs=[pl.BlockSpec((MOE_BLOCK, d), lambda i, be, nu: (i, 0)),
                  pl.BlockSpec((1, d, hid), lambda i, be, nu: (be[i], 0, 0)),
                  pl.BlockSpec((1, d, hid), lambda i, be, nu: (be[i], 0, 0)),
                  pl.BlockSpec((1, hid, d), lambda i, be, nu: (be[i], 0, 0))],
        out_specs=pl.BlockSpec((MOE_BLOCK, d), lambda i, be, nu: (i, 0)),
        scratch_shapes=[pltpu.VMEM((d, hid), BF16), pltpu.VMEM((d, hid), BF16), pltpu.VMEM((hid, d), BF16)],
    )
    return pl.pallas_call(
        _expert_kernel,
        grid_spec=grid_spec,
        out_shape=jax.ShapeDtypeStruct((n_rows, d), F32),
        compiler_params=_cparams("arbitrary"),
        name="moe_experts",
    )(block_e, n_used, xs, lp['exp_w_gate'], lp['exp_w_up'], lp['exp_w_down'])


def _combine_kernel(dest_ref, ys_hbm, wts_ref, h2_ref, x_ref, mod_ref, sg_ref, su_ref, sd_ref, gf_ref, o_ref,
                    buf, sem, *, final):
    def issue(t, carry):
        for k in range(TOP_K):
            _row_copy(ys_hbm, dest_ref[0, 0, t * TOP_K + k], buf.at[k], t, sem).start()
        return carry

    lax.fori_loop(0, TM, issue, 0)
    h2 = h2_ref[...].astype(BF16)
    hid = _silu(_dot(h2, sg_ref[...])) * _dot(h2, su_ref[...])
    f = _dot(hid.astype(BF16), sd_ref[...])

    def drain(t, carry):
        for k in range(TOP_K):
            _row_copy(ys_hbm, 0, buf.at[k], 0, sem).wait()
        return carry

    lax.fori_loop(0, TM, drain, 0)
    wts = wts_ref[...]
    for k in range(TOP_K):
        f = f + wts[:, k:k + 1] * buf[k]
    m = mod_ref[0, 0]
    x2 = x_ref[0] + m[5:6] * f
    if final:
        x2 = x2 * lax.rsqrt(jnp.mean(x2 * x2, axis=-1, keepdims=True) + NORM_EPS) * gf_ref[...]
    o_ref[0] = x2


def _combine(ys, dest, wts, h2, x1, modtab, lp, g_final, n_lat_tiles, final):
    b, nt, d = x1.shape
    tiles_b = nt // TM
    weights = (lp['sh_w_gate'].astype(BF16), lp['sh_w_up'].astype(BF16), lp['sh_w_down'].astype(BF16),
               g_final.reshape(1, -1))
    full = lambda a: pl.BlockSpec(a.shape, lambda i: (0,) * a.ndim)
    tok = pl.BlockSpec((1, TM, d), lambda i: (i // tiles_b, i % tiles_b, 0))
    return pl.pallas_call(
        functools.partial(_combine_kernel, final=final),
        grid=(b * tiles_b,),
        in_specs=[pl.BlockSpec((1, 1, TM * TOP_K), lambda i: (i, 0, 0), memory_space=pltpu.SMEM),
                  pl.BlockSpec(memory_space=pl.ANY),
                  pl.BlockSpec((TM, TOPK_PAD), lambda i: (i, 0)),
                  pl.BlockSpec((TM, d), lambda i: (i, 0)),
                  tok,
                  pl.BlockSpec((1, 1, 8, d), lambda i: (i // tiles_b, jnp.where(i % tiles_b >= n_lat_tiles, 1, 0), 0, 0))]
                 + [full(w) for w in weights],
        out_specs=tok,
        out_shape=jax.ShapeDtypeStruct((b, nt, d), F32),
        scratch_shapes=[pltpu.VMEM((TOP_K, TM, d), F32), pltpu.SemaphoreType.DMA(())],
        compiler_params=_cparams("arbitrary"),
        name="moe_combine",
    )(dest, ys, wts, h2, x1, modtab, *weights)


def _moe(x1, modtab, g2, lp, g_final, n_lat_tiles, final):
    b, nt, d = x1.shape
    t = b * nt
    h2, idx, wts, rank, cnt = _route(x1, modtab, g2, lp, n_lat_tiles)
    counts = cnt[0, :N_EXPERTS].astype(I32)
    padded = (counts + MOE_BLOCK - 1) // MOE_BLOCK * MOE_BLOCK
    p_ends = jnp.cumsum(padded)
    p_starts = p_ends - padded
    n_blocks = (t * TOP_K + N_EXPERTS * (MOE_BLOCK - 1) + MOE_BLOCK - 1) // MOE_BLOCK
    dest = (p_starts[idx[:, :TOP_K]] + rank[:, :TOP_K]).reshape(t // TM, 1, TM * TOP_K)
    blk_start = jnp.arange(n_blocks, dtype=I32) * MOE_BLOCK
    block_e = jnp.minimum(jnp.sum((p_ends[None, :] <= blk_start[:, None]).astype(I32), axis=1), N_EXPERTS - 1)
    n_used = (p_ends[-1:] // MOE_BLOCK).astype(I32)
    xs = _dispatch(h2, dest, n_blocks * MOE_BLOCK)
    ys = _experts(xs, block_e, n_used, lp)
    return _combine(ys, dest, wts, h2, x1, modtab, lp, g_final, n_lat_tiles, final)


def _layer(xa, c, c_ctx, lp, consts, g_final, seq, final):
    b, nt, d = xa.shape
    n_lat_tiles = seq // TM
    rope, mats_lat, mats_ctx = consts
    modtab = _mod_table(c, c_ctx, lp['w_mod'], lp['b_mod'])
    g1 = lp['g_norm1'].reshape(1, -1)
    g2 = lp['g_norm2'].reshape(1, -1)
    pw = _proj_weights(lp)
    q, k, v, nq, nk, nv, hy, lu, lg = _project(xa, modtab, g1, pw, rope, n_lat_tiles)
    br_a = _mla_attention(q, k, v, seq)
    br_b = _na_attention(nq, nk, nv, _na_bias_tables(lp['na_rpb'], seq // GRID_W), seq)
    pre = _hy_pre(hy, lp['hy_short_w'], lp['hy_short_b'], seq)
    lat_f, lat_bf, ctx_f, ctx_bf = pre
    c_lat = _hyena_seq(mats_lat, lat_f[0], lat_bf[0], lat_f[1], lat_f[2], lp, b)
    c_ctx_out = _hyena_seq(mats_ctx, ctx_f[0], ctx_bf[0], ctx_f[1], ctx_f[2], lp, b)
    br_c = jnp.concatenate([c_lat, c_ctx_out], axis=1)
    br_d = _lru_mixer(lu, lg, lp, seq)
    x1 = _merge(xa, modtab, g1, (br_a, br_b, br_c, br_d), pw['w_gate'], lp, n_lat_tiles)
    return _moe(x1, modtab, g2, lp, g_final, n_lat_tiles, final)


_LAYER_KEYS = ('w_mod', 'b_mod', 'g_norm1', 'g_norm2', 'w_in', 'mla_g_q', 'mla_w_uq', 'mla_g_kv', 'mla_w_ukv',
               'na_rpb', 'hy_short_w', 'hy_short_b', 'hy_w1', 'hy_b1', 'hy_w2', 'hy_b2', 'hy_w3', 'hy_decay',
               'hy_bias', 'lru_conv_w', 'lru_conv_b', 'lru_wa', 'lru_ba', 'lru_wx', 'lru_bx', 'lru_lambda',
               'w_branch', 'w_out', 'router_w', 'router_bias', 'exp_w_gate', 'exp_w_up', 'exp_w_down',
               'sh_w_gate', 'sh_w_up', 'sh_w_down')


def kernel(x, c, ctx, c_ctx, w_mod, b_mod, g_norm1, g_norm2, w_in, mla_g_q, mla_w_uq, mla_g_kv, mla_w_ukv, na_rpb, hy_short_w, hy_short_b, hy_w1, hy_b1, hy_w2, hy_b2, hy_w3, hy_decay, hy_bias, lru_conv_w, lru_conv_b, lru_wa, lru_ba, lru_wx, lru_bx, lru_lambda, w_branch, w_out, router_w, router_bias, exp_w_gate, exp_w_up, exp_w_down, sh_w_gate, sh_w_up, sh_w_down, g_final):
    stacked = dict(zip(_LAYER_KEYS, (w_mod, b_mod, g_norm1, g_norm2, w_in, mla_g_q, mla_w_uq, mla_g_kv, mla_w_ukv,
                                     na_rpb, hy_short_w, hy_short_b, hy_w1, hy_b1, hy_w2, hy_b2, hy_w3, hy_decay,
                                     hy_bias, lru_conv_w, lru_conv_b, lru_wa, lru_ba, lru_wx, lru_bx, lru_lambda,
                                     w_branch, w_out, router_w, router_bias, exp_w_gate, exp_w_up, exp_w_down,
                                     sh_w_gate, sh_w_up, sh_w_down)))
    b, seq, d = x.shape
    ctx_len = ctx.shape[1]
    depth = w_mod.shape[0]
    assert seq % TM == 0 and ctx_len % TM == 0 and seq // GRID_W >= NA_KEY_ROWS + 1
    xa = jnp.concatenate([x, ctx], axis=1)
    consts = (_rope_tables(seq, seq + ctx_len), _dft_matrices(seq), _dft_matrices(ctx_len))
    for i in range(depth):
        lp = {name: w[i] for name, w in stacked.items()}
        xa = _layer(xa, c, c_ctx, lp, consts, g_final, seq, i == depth - 1)
    return xa[:, :seq]
```

```python
import functools
import math

import numpy as np
import jax
import jax.numpy as jnp
from jax import lax
from jax.experimental import pallas as pl
from jax.experimental.pallas import tpu as pltpu

F32 = jnp.float32
BF16 = jnp.bfloat16
I32 = jnp.int32

TM = 256
LANE = 128
GRID_W = 64
N_MOD = 6
NORM_EPS = 1e-6

MLA_HEADS, MLA_NOPE, MLA_ROPE, MLA_V = 4, 64, 32, 64
MLA_Q_RANK, MLA_KV_RANK = 192, 128
MLA_Q_PAD = 256
ROPE_THETA = 10000.0

NA_HEADS, NA_HEAD_DIM, NA_WIN_R, NA_WIN_C = 4, 64, 8, 16
NA_TILE_ROWS = TM // GRID_W
NA_KEY_ROWS = NA_TILE_ROWS + NA_WIN_R - 1
NA_KEYS = NA_KEY_ROWS * GRID_W

HY_WIDTH, HY_ORDER, HY_SHORT, HY_BANDS, HY_FFN = 256, 2, 3, 16, 64
HY_EMB = 2 * HY_BANDS + 1

LRU_WIDTH, LRU_BLOCKS, LRU_CONV, LRU_C = 256, 4, 4, 8.0
LRU_CHUNK = 256
LRU_HALO = 8

N_EXPERTS, TOP_K, EXPERT_HIDDEN, ROUTED_SCALE, MOE_BLOCK = 64, 6, 256, 2.5, 256
TOPK_PAD = 8

VMEM_LIMIT = 52 * 1024 * 1024


def _cparams(*sem):
    return pltpu.CompilerParams(dimension_semantics=sem, vmem_limit_bytes=VMEM_LIMIT)


def _dot(a, b):
    return jnp.dot(a, b, preferred_element_type=F32)


def _dot_nt(a, b):
    return lax.dot_general(a, b, (((1,), (1,)), ((), ())), preferred_element_type=F32)


def _sigmoid(x):
    return 1.0 / (1.0 + jnp.exp(-x))


def _silu(x):
    return x * _sigmoid(x)


def _gelu_tanh(x):
    return 0.5 * x * (1.0 + jnp.tanh(math.sqrt(2.0 / math.pi) * (x + 0.044715 * (x * x * x))))


def _normmod(x, g, scale, shift):
    y = x * lax.rsqrt(jnp.mean(x * x, axis=-1, keepdims=True) + NORM_EPS) * g
    return y * (1.0 + scale) + shift


def _mod_kernel(c_ref, w_ref, b_ref, o_ref):
    s = _silu(c_ref[...])
    o_ref[...] = _dot(s.astype(BF16), w_ref[...].astype(BF16)) + b_ref[...]


def _mod_table(c, c_ctx, w_mod, b_mod):
    b, d = c.shape
    rows = 16
    cc = jnp.zeros((rows, d), F32).at[:b].set(c).at[b].set(c_ctx)
    tn = 1024
    mod = pl.pallas_call(
        _mod_kernel,
        grid=(N_MOD * d // tn,),
        in_specs=[pl.BlockSpec((rows, d), lambda j: (0, 0)),
                  pl.BlockSpec((d, tn), lambda j: (0, j)),
                  pl.BlockSpec((1, tn), lambda j: (0, j))],
        out_specs=pl.BlockSpec((rows, tn), lambda j: (0, j)),
        out_shape=jax.ShapeDtypeStruct((rows, N_MOD * d), F32),
        compiler_params=_cparams("arbitrary"),
        name="mod_vectors",
    )(cc, w_mod, b_mod.reshape(1, -1))
    lat = mod[:b].reshape(b, N_MOD, d)
    ctx = jnp.broadcast_to(mod[b].reshape(1, N_MOD, d), (b, N_MOD, d))
    tab = jnp.stack([lat, ctx], axis=1)
    return jnp.pad(tab, ((0, 0), (0, 0), (0, 8 - N_MOD), (0, 0)))


_C_QLAT, _C_KVLAT, _C_KR, _C_KRR, _C_NA, _C_HY, _C_LU, _C_LG, _C_END = (
    0, 256, 384, 512, 640, 640 + 3 * NA_HEADS * LANE, 640 + 1536 + 768, 640 + 1536 + 1024, 640 + 1536 + 1280)


def _proj_kernel(x_ref, mod_ref, g1_ref, w1_ref, gq_ref, gkv_ref, wuq_ref, wuqr_ref, wk_ref, wv_ref,
                 cos_ref, sin_ref, q_o, k_o, v_o, nq_o, nk_o, nv_o, hy_o, lu_o, lg_o):
    m = mod_ref[0, 0]
    h = _normmod(x_ref[0], g1_ref[...], m[1:2], m[0:1])
    z = _dot(h.astype(BF16), w1_ref[...])
    qlat = z[:, _C_QLAT:_C_KVLAT]
    kvlat = z[:, _C_KVLAT:_C_KR]
    kr = z[:, _C_KR:_C_KRR]
    krr = z[:, _C_KRR:_C_NA]
    qn = qlat * lax.rsqrt(jnp.sum(qlat * qlat, axis=-1, keepdims=True) * (1.0 / MLA_Q_RANK) + NORM_EPS) * gq_ref[...]
    kvn = kvlat * lax.rsqrt(jnp.mean(kvlat * kvlat, axis=-1, keepdims=True) + NORM_EPS) * gkv_ref[...]
    qn = qn.astype(BF16)
    kvn = kvn.astype(BF16)
    q = _dot(qn, wuq_ref[...])
    qr = _dot(qn, wuqr_ref[...])
    kk = _dot(kvn, wk_ref[...])
    vv = _dot(kvn, wv_ref[...])
    cos = cos_ref[...]
    sin = sin_ref[...]
    krope = kr * cos + krr * sin
    for hd in range(MLA_HEADS):
        sl = slice(hd * LANE, (hd + 1) * LANE)
        q_o[0, hd] = (q[:, sl] * cos + qr[:, sl] * sin).astype(BF16)
        k_o[0, hd] = (kk[:, sl] + krope).astype(BF16)
        v_o[0, hd] = vv[:, sl].astype(BF16)
    for hd in range(NA_HEADS):
        for which, ref in enumerate((nq_o, nk_o, nv_o)):
            lo = _C_NA + (which * NA_HEADS + hd) * LANE
            ref[0, hd] = z[:, lo:lo + LANE].astype(BF16)
    hy_o[0] = z[:, _C_HY:_C_LU]
    lu_o[0] = z[:, _C_LU:_C_LG]
    lg_o[0] = z[:, _C_LG:_C_END]


def _pad_to(a, n, axis):
    pad = [(0, 0)] * a.ndim
    pad[axis] = (0, n - a.shape[axis])
    return jnp.pad(a, pad)


def _rot_cols(w):
    half = w.shape[-1] // 2
    return jnp.concatenate([-w[..., half:], w[..., :half]], axis=-1)


def _head_blocks(cols_per_head):
    out = []
    for pieces in cols_per_head:
        k = pieces[0][0].shape[0]
        blk = jnp.zeros((k, LANE), F32)
        for arr, off in pieces:
            blk = blk.at[:, off:off + arr.shape[1]].set(arr)
        out.append(blk)
    return jnp.concatenate(out, axis=1)


def _proj_weights(lp):
    w_in = lp['w_in']
    d = w_in.shape[0]
    o = 0
    parts = {}
    for name, n in (('q', MLA_Q_RANK), ('kv', MLA_KV_RANK), ('kr', MLA_ROPE), ('na', 3 * NA_HEADS * NA_HEAD_DIM),
                    ('hy', 3 * HY_WIDTH), ('lu', LRU_WIDTH), ('lg', LRU_WIDTH), ('gt', 4 * d)):
        parts[name] = w_in[:, o:o + n]
        o += n
    zeros = lambda n: jnp.zeros((d, n), F32)
    kr_blk = jnp.concatenate([zeros(MLA_NOPE), parts['kr'], zeros(LANE - MLA_NOPE - MLA_ROPE)], axis=1)
    krr_blk = jnp.concatenate([zeros(MLA_NOPE), _rot_cols(parts['kr']), zeros(LANE - MLA_NOPE - MLA_ROPE)], axis=1)
    na_scale = NA_HEAD_DIM ** -0.5
    na_cols = []
    for which in range(3):
        for hd in range(NA_HEADS):
            lo = (which * NA_HEADS + hd) * NA_HEAD_DIM
            blk = parts['na'][:, lo:lo + NA_HEAD_DIM] * (na_scale if which == 0 else 1.0)
            na_cols.append(_pad_to(blk, LANE, 1))
    w1 = jnp.concatenate([_pad_to(parts['q'], MLA_Q_PAD, 1), parts['kv'], kr_blk, krr_blk] + na_cols
                         + [parts['hy'], parts['lu'], parts['lg']], axis=1).astype(BF16)
    mla_scale = (MLA_NOPE + MLA_ROPE) ** -0.5
    wuq = _pad_to(lp['mla_w_uq'], MLA_Q_PAD, 0) * mla_scale
    dq = MLA_NOPE + MLA_ROPE
    wuq_main = _head_blocks([[(wuq[:, hd * dq:hd * dq + dq], 0)] for hd in range(MLA_HEADS)])
    wuq_rot = _head_blocks([[(_rot_cols(wuq[:, hd * dq + MLA_NOPE:hd * dq + dq]), MLA_NOPE)]
                            for hd in range(MLA_HEADS)])
    dkv = MLA_NOPE + MLA_V
    wukv = lp['mla_w_ukv']
    wk = _head_blocks([[(wukv[:, hd * dkv:hd * dkv + MLA_NOPE], 0)] for hd in range(MLA_HEADS)])
    wv = _head_blocks([[(wukv[:, hd * dkv + MLA_NOPE:hd * dkv + dkv], 0)] for hd in range(MLA_HEADS)])
    gq = _pad_to(lp['mla_g_q'].reshape(1, -1), MLA_Q_PAD, 1)
    gkv = lp['mla_g_kv'].reshape(1, -1)
    return dict(w1=w1, w_gate=parts['gt'].astype(BF16), gq=gq, gkv=gkv, wuq=wuq_main.astype(BF16),
                wuq_rot=wuq_rot.astype(BF16), wk=wk.astype(BF16), wv=wv.astype(BF16))


def _rope_tables(seq, n_tok):
    t = jnp.arange(seq, dtype=I32)
    row = (t // GRID_W).astype(F32)
    col = (t % GRID_W).astype(F32)
    n_axis = MLA_ROPE // 4
    inv_freq = ROPE_THETA ** (-jnp.arange(n_axis, dtype=F32) / n_axis)
    ang = jnp.concatenate([row[:, None] * inv_freq, col[:, None] * inv_freq], axis=-1)
    cos = jnp.concatenate([jnp.cos(ang), jnp.cos(ang)], axis=-1)
    sin = jnp.concatenate([jnp.sin(ang), jnp.sin(ang)], axis=-1)
    cos_t = jnp.ones((n_tok, LANE), F32).at[:seq, MLA_NOPE:MLA_NOPE + MLA_ROPE].set(cos)
    sin_t = jnp.zeros((n_tok, LANE), F32).at[:seq, MLA_NOPE:MLA_NOPE + MLA_ROPE].set(sin)
    return cos_t, sin_t


def _kind_map(n_lat_tiles):
    return lambda b, i: (b, jnp.where(i >= n_lat_tiles, 1, 0), 0, 0)


def _project(xa, modtab, g1, pw, rope, n_lat_tiles):
    b, nt, d = xa.shape
    cos_t, sin_t = rope
    full = lambda a: pl.BlockSpec(a.shape, lambda bb, i: (0,) * a.ndim)
    head_out = lambda: pl.BlockSpec((1, MLA_HEADS, TM, LANE), lambda bb, i: (bb, 0, i, 0))
    tok_out = lambda n: pl.BlockSpec((1, TM, n), lambda bb, i: (bb, i, 0))
    head_shape = jax.ShapeDtypeStruct((b, MLA_HEADS, nt, LANE), BF16)
    tok_shape = lambda n: jax.ShapeDtypeStruct((b, nt, n), F32)
    weights = (g1, pw['w1'], pw['gq'], pw['gkv'], pw['wuq'], pw['wuq_rot'], pw['wk'], pw['wv'])
    return pl.pallas_call(
        _proj_kernel,
        grid=(b, nt // TM),
        in_specs=[pl.BlockSpec((1, TM, d), lambda bb, i: (bb, i, 0)),
                  pl.BlockSpec((1, 1, 8, d), _kind_map(n_lat_tiles))]
                 + [full(w) for w in weights]
                 + [pl.BlockSpec((TM, LANE), lambda bb, i: (i, 0))] * 2,
        out_specs=[head_out() for _ in range(6)] + [tok_out(3 * HY_WIDTH), tok_out(LRU_WIDTH), tok_out(LRU_WIDTH)],
        out_shape=[head_shape] * 6 + [tok_shape(3 * HY_WIDTH), tok_shape(LRU_WIDTH), tok_shape(LRU_WIDTH)],
        compiler_params=_cparams("parallel", "parallel"),
        name="input_projection",
    )(xa, modtab, *weights, cos_t, sin_t)


def _softmax_pv(parts):
    m = None
    for s, _ in parts:
        mm = jnp.max(s, axis=-1, keepdims=True)
        m = mm if m is None else jnp.maximum(m, mm)
    acc = None
    den = None
    for s, v in parts:
        p = jnp.exp(s - m)
        l = jnp.sum(p, axis=-1, keepdims=True)
        o = _dot(p.astype(BF16), v)
        acc = o if acc is None else acc + o
        den = l if den is None else den + l
    return acc / den


def _mla_kernel(q_ref, k_ref, v_ref, o_ref, *, seq, n_lat_tiles):
    i = pl.program_id(1)
    nt = k_ref.shape[2]

    def attend(lo, hi):
        for hd in range(MLA_HEADS):
            s = _dot_nt(q_ref[0, hd], k_ref[0, hd, lo:hi, :])
            o = _softmax_pv([(s, v_ref[0, hd, lo:hi, :])])
            o_ref[0, :, hd * LANE:(hd + 1) * LANE] = o.astype(BF16)

    @pl.when(i < n_lat_tiles)
    def _():
        attend(0, nt)

    @pl.when(i >= n_lat_tiles)
    def _():
        attend(seq, nt)


def _mla_attention(q, k, v, seq):
    b, h, nt, _ = q.shape
    kv_spec = pl.BlockSpec((1, h, nt, LANE), lambda bb, i: (bb, 0, 0, 0))
    return pl.pallas_call(
        functools.partial(_mla_kernel, seq=seq, n_lat_tiles=seq // TM),
        grid=(b, nt // TM),
        in_specs=[pl.BlockSpec((1, h, TM, LANE), lambda bb, i: (bb, 0, i, 0)), kv_spec, kv_spec],
        out_specs=pl.BlockSpec((1, TM, h * LANE), lambda bb, i: (bb, i, 0)),
        out_shape=jax.ShapeDtypeStruct((b, nt, h * LANE), BF16),
        compiler_params=_cparams("parallel", "arbitrary"),
        name="mla_attention",
    )(q, k, v)


def _na_bias_tables(rpb, rows):
    n_blk = rows // NA_TILE_ROWS
    col = np.arange(GRID_W)
    c0 = np.clip(col - NA_WIN_C // 2, 0, GRID_W - NA_WIN_C)
    in_win = (col[None, :] >= c0[:, None]) & (col[None, :] < c0[:, None] + NA_WIN_C)
    dc = np.clip(col[None, :] - col[:, None], 1 - NA_WIN_C, NA_WIN_C - 1) + NA_WIN_C - 1
    rpb = rpb.astype(F32)
    tables = []
    for j in (0, 1, n_blk - 1):
        w0 = min(max(NA_TILE_ROWS * j - NA_WIN_R // 2, 0), rows - NA_KEY_ROWS)
        r = NA_TILE_ROWS * j + np.arange(NA_TILE_ROWS)
        kr = w0 + np.arange(NA_KEY_ROWS)
        r0 = np.clip(r - NA_WIN_R // 2, 0, rows - NA_WIN_R)
        row_ok = (kr[None, :] >= r0[:, None]) & (kr[None, :] < r0[:, None] + NA_WIN_R)
        dr = np.clip(kr[None, :] - r[:, None] + NA_WIN_R - 1, 0, 2 * NA_WIN_R - 2)
        oh_r = jnp.asarray(np.eye(2 * NA_WIN_R - 1, dtype=np.float32)[dr.reshape(-1)])
        oh_c = jnp.asarray(np.eye(2 * NA_WIN_C - 1, dtype=np.float32)[dc.reshape(-1)])
        bias = jnp.einsum('ar,hrc,bc->hab', oh_r, rpb, oh_c, precision=lax.Precision.HIGHEST)
        bias = bias.reshape(NA_HEADS, NA_TILE_ROWS, NA_KEY_ROWS, GRID_W, GRID_W)
        mask = row_ok[:, :, None, None] & in_win[None, None, :, :]
        bias = jnp.where(jnp.asarray(mask)[None], bias, -jnp.inf)
        tables.append(bias.transpose(0, 1, 3, 2, 4).reshape(NA_HEADS, TM, NA_KEYS))
    return jnp.stack(tables)


def _na_kernel(q_ref, k_ref, v_ref, bias_ref, o_ref, *, seq, n_lat_tiles):
    i = pl.program_id(1)
    nt = k_ref.shape[2]
    rows = seq // GRID_W

    @pl.when(i < n_lat_tiles)
    def _():
        w0 = jnp.clip(NA_TILE_ROWS * i - NA_WIN_R // 2, 0, rows - NA_KEY_ROWS)
        start = pl.multiple_of(w0 * GRID_W, GRID_W)
        for hd in range(NA_HEADS):
            q = q_ref[0, hd]
            s_loc = _dot_nt(q, k_ref[0, hd, pl.ds(start, NA_KEYS), :]) + bias_ref[0, hd]
            s_ctx = _dot_nt(q, k_ref[0, hd, seq:nt, :])
            o = _softmax_pv([(s_loc, v_ref[0, hd, pl.ds(start, NA_KEYS), :]), (s_ctx, v_ref[0, hd, seq:nt, :])])
            o_ref[0, :, hd * LANE:(hd + 1) * LANE] = o.astype(BF16)

    @pl.when(i >= n_lat_tiles)
    def _():
        for hd in range(NA_HEADS):
            s = _dot_nt(q_ref[0, hd], k_ref[0, hd, seq:nt, :])
            o = _softmax_pv([(s, v_ref[0, hd, seq:nt, :])])
            o_ref[0, :, hd * LANE:(hd + 1) * LANE] = o.astype(BF16)


def _na_attention(q, k, v, bias, seq):
    b, h, nt, _ = q.shape
    n_lat = seq // TM
    kv_spec = pl.BlockSpec((1, h, nt, LANE), lambda bb, i: (bb, 0, 0, 0))
    cfg = lambda bb, i: (jnp.where(i == 0, 0, jnp.where(i >= n_lat - 1, 2, 1)), 0, 0, 0)
    return pl.pallas_call(
        functools.partial(_na_kernel, seq=seq, n_lat_tiles=n_lat),
        grid=(b, nt // TM),
        in_specs=[pl.BlockSpec((1, h, TM, LANE), lambda bb, i: (bb, 0, i, 0)), kv_spec, kv_spec,
                  pl.BlockSpec((1, h, TM, NA_KEYS), cfg)],
        out_specs=pl.BlockSpec((1, TM, h * LANE), lambda bb, i: (bb, i, 0)),
        out_shape=jax.ShapeDtypeStruct((b, nt, h * LANE), BF16),
        compiler_params=_cparams("parallel", "arbitrary"),
        name="neighbourhood_attention",
    )(q, k, v, bias)


HY_PRE_CHUNK = 256


def _hy_pre_kernel(z_ref, w_ref, b_ref, lat_o, lat_bf_o, ctx_o, ctx_bf_o, *, seq, ctx_len):
    w = w_ref[...]
    bias = b_ref[...]
    ch = HY_PRE_CHUNK
    zero = jnp.zeros((8, HY_WIDTH), F32)
    for seg_lo, seg_len, o_ref, obf_ref in ((0, seq, lat_o, lat_bf_o), (seq, ctx_len, ctx_o, ctx_bf_o)):
        for c in range(seg_len // ch):
            s = seg_lo + c * ch
            before = zero if c == 0 else z_ref[0, s - 8:s, :]
            after = zero if c == seg_len // ch - 1 else z_ref[0, s + ch:s + ch + 8, :]
            win = jnp.concatenate([before, z_ref[0, s:s + ch, :], after], axis=0)
            n = ch + 16
            y = bias + w[1:2] * win[8:8 + ch]
            y = y + w[0:1] * pltpu.roll(win, 1, axis=0)[8:8 + ch]
            y = y + w[2:3] * pltpu.roll(win, n - 1, axis=0)[8:8 + ch]
            o_ref[0, c * ch:(c + 1) * ch, :] = y
            obf_ref[0, c * ch:(c + 1) * ch, :] = y.astype(BF16)


def _hy_pre(hy, w, bvec, seq):
    b, nt, _ = hy.shape
    ctx_len = nt - seq
    wpad = _pad_to(w, 8, 0)
    out_specs, out_shape = [], []
    for length in (seq, ctx_len):
        for dt in (F32, BF16):
            out_specs.append(pl.BlockSpec((1, length, HY_WIDTH), lambda bb, g: (g, 0, bb)))
            out_shape.append(jax.ShapeDtypeStruct((3, length, b * HY_WIDTH), dt))
    return pl.pallas_call(
        functools.partial(_hy_pre_kernel, seq=seq, ctx_len=ctx_len),
        grid=(b, 3),
        in_specs=[pl.BlockSpec((1, nt, HY_WIDTH), lambda bb, g: (bb, 0, g)),
                  pl.BlockSpec((8, HY_WIDTH), lambda bb, g: (0, g)),
                  pl.BlockSpec((1, HY_WIDTH), lambda bb, g: (0, g))],
        out_specs=out_specs,
        out_shape=out_shape,
        compiler_params=_cparams("parallel", "parallel"),
        name="hyena_short_conv",
    )(hy, wpad, bvec.reshape(1, -1))


def _hp_dot(a, b):
    return jnp.dot(a, b, preferred_element_type=F32, precision=lax.Precision.HIGHEST)


def _hy_filter_kernel(f_ref, w1_ref, b1_ref, w2_ref, b2_ref, w3_ref, dec_ref, o_ref, obf_ref):
    f = f_ref[...]
    h = jnp.sin(_hp_dot(f, w1_ref[...]) + b1_ref[...])
    h = jnp.sin(_hp_dot(h, w2_ref[...]) + b2_ref[...])
    h = _hp_dot(h, w3_ref[...])
    h = h * jnp.exp(-f[:, 0:1] * jnp.abs(dec_ref[...]))
    o_ref[...] = h
    obf_ref[...] = h.astype(BF16)


def _hy_pos_features(length):
    t = jnp.linspace(0.0, 1.0, length, dtype=F32)[:, None]
    w = 2.0 * math.pi * jnp.arange(length, dtype=F32)[:, None] / length
    f = jnp.linspace(1e-4, HY_BANDS - 1, HY_BANDS, dtype=F32)[None, :]
    z = w * f
    return jnp.concatenate([t, jnp.cos(z), -jnp.sin(z)], axis=-1)


def _hy_filters(length, lp):
    feats = _hy_pos_features(length)
    n_out = HY_ORDER * 2 * HY_WIDTH
    tl = min(length, 512)
    full = lambda a: pl.BlockSpec(a.shape, lambda i: (0,) * a.ndim)
    args = (lp['hy_w1'], lp['hy_b1'].reshape(1, -1), lp['hy_w2'], lp['hy_b2'].reshape(1, -1), lp['hy_w3'],
            lp['hy_decay'].reshape(1, n_out))
    return pl.pallas_call(
        _hy_filter_kernel,
        grid=(length // tl,),
        in_specs=[pl.BlockSpec((tl, HY_EMB), lambda i: (i, 0))] + [full(a) for a in args],
        out_specs=[pl.BlockSpec((tl, n_out), lambda i: (i, 0))] * 2,
        out_shape=[jax.ShapeDtypeStruct((length, n_out), F32), jax.ShapeDtypeStruct((length, n_out), BF16)],
        compiler_params=_cparams("parallel"),
        name="hyena_filter_mlp",
    )(feats, *args)


def _dft_matrices(length):
    n2 = 2 * length
    step = 64
    n = jnp.arange(length, dtype=I32)

    def trig(kv):
        ang = ((kv[:, None] * n[None, :]) % n2).astype(F32) * (2.0 * math.pi / n2)
        return jnp.cos(ang), jnp.sin(ang)

    c1, s1 = trig(jnp.arange(length // step, dtype=I32) * step)
    c2, s2 = trig(jnp.arange(step, dtype=I32))
    cos = (c1[:, None, :] * c2[None] - s1[:, None, :] * s2[None]).reshape(length, length)
    sin = (s1[:, None, :] * c2[None] + c1[:, None, :] * s2[None]).reshape(length, length)
    nyq = jnp.where(n % 2 == 0, 1.0, -1.0).astype(F32)
    ms = (-sin).at[0, :].set(nyq)
    mst = (-sin).at[:, 0].set(nyq)
    return cos.astype(BF16), ms.astype(BF16), mst.astype(BF16)


def _dft_fwd_kernel(mc_ref, ms_ref, x_ref, *rest, with_taps):
    ure = _dot(mc_ref[...], x_ref[...])
    uim = _dot(ms_ref[...], x_ref[...])
    if not with_taps:
        ure_o, uim_o = rest
        ure_o[...] = ure
        uim_o[...] = uim
        return
    a_ref, b_ref, c_ref, d_ref, zre_o, zim_o = rest
    a, b, c, d = a_ref[...], b_ref[...], c_ref[...], d_ref[...]
    for bb in range(x_ref.shape[1] // HY_WIDTH):
        sl = slice(bb * HY_WIDTH, (bb + 1) * HY_WIDTH)
        zre_o[:, sl] = (ure[:, sl] * a - uim[:, sl] * b).astype(BF16)
        zim_o[:, sl] = (ure[:, sl] * c + uim[:, sl] * d).astype(BF16)


def _col_block(nc, cap):
    return min(nc, cap)


def _dft_fwd(mats, x, plane, taps=None):
    mc, ms, _ = mats
    _, length, nc = x.shape
    tk = min(length, 256)
    cb = _col_block(nc, 1024)
    grid = (nc // cb, length // tk)
    m_spec = pl.BlockSpec((tk, length), lambda c, j: (j, 0))
    x_spec = pl.BlockSpec((None, length, cb), lambda c, j: (plane, 0, c))
    o_spec = pl.BlockSpec((tk, cb), lambda c, j: (j, c))
    if taps is None:
        return pl.pallas_call(
            functools.partial(_dft_fwd_kernel, with_taps=False),
            grid=grid, in_specs=[m_spec, m_spec, x_spec], out_specs=[o_spec, o_spec],
            out_shape=[jax.ShapeDtypeStruct((length, nc), F32)] * 2,
            compiler_params=_cparams("parallel", "arbitrary"),
            name="hyena_dft_filters",
        )(mc, ms, x)
    t_spec = pl.BlockSpec((tk, HY_WIDTH), lambda c, j: (j, 0))
    return pl.pallas_call(
        functools.partial(_dft_fwd_kernel, with_taps=True),
        grid=grid, in_specs=[m_spec, m_spec, x_spec] + [t_spec] * 4, out_specs=[o_spec, o_spec],
        out_shape=[jax.ShapeDtypeStruct((length, nc), BF16)] * 2,
        compiler_params=_cparams("parallel", "arbitrary"),
        name="hyena_dft_forward",
    )(mc, ms, x, *taps)


def _dft_inv_kernel(mc_ref, mst_ref, zre_ref, zim_ref, gate_ref, prev_ref, bias_ref, *outs, last):
    conv = _dot(mc_ref[...], zre_ref[...]) + _dot(mst_ref[...], zim_ref[...])
    y = gate_ref[...] * (conv + prev_ref[...] * bias_ref[...])
    if last:
        (tok_o,) = outs
        for bb in range(y.shape[1] // HY_WIDTH):
            tok_o[bb] = y[:, bb * HY_WIDTH:(bb + 1) * HY_WIDTH].astype(BF16)
    else:
        y_o, ybf_o = outs
        y_o[...] = y
        ybf_o[...] = y.astype(BF16)


def _dft_inv(mats, zre, zim, gate, gate_plane, prev, prev_plane, bias_row, last):
    mc, _, mst = mats
    length, nc = zre.shape
    tm = min(length, 256)
    cb = _col_block(nc, 512)
    grid = (nc // cb, length // tm)
    m_spec = pl.BlockSpec((tm, length), lambda c, i: (i, 0))
    z_spec = pl.BlockSpec((length, cb), lambda c, i: (0, c))
    e_spec = lambda plane: pl.BlockSpec((None, tm, cb), lambda c, i: (plane, i, c))
    b_spec = pl.BlockSpec((1, cb), lambda c, i: (0, c))
    if last:
        nb = nc // HY_WIDTH
        out_specs = [pl.BlockSpec((cb // HY_WIDTH, tm, HY_WIDTH), lambda c, i: (c, i, 0))]
        out_shape = [jax.ShapeDtypeStruct((nb, length, HY_WIDTH), BF16)]
    else:
        out_specs = [e_spec(0), e_spec(0)]
        out_shape = [jax.ShapeDtypeStruct((1, length, nc), F32), jax.ShapeDtypeStruct((1, length, nc), BF16)]
    return pl.pallas_call(
        functools.partial(_dft_inv_kernel, last=last),
        grid=grid, in_specs=[m_spec, m_spec, z_spec, z_spec, e_spec(gate_plane), e_spec(prev_plane), b_spec],
        out_specs=out_specs, out_shape=out_shape,
        compiler_params=_cparams("parallel", "arbitrary"),
        name="hyena_dft_inverse",
    )(mc, mst, zre, zim, gate, prev, bias_row)


def _hy_tap_tables(ure, uim, filt, length):
    n2 = 2 * length
    w = HY_WIDTH
    scale = jnp.full((length, 1), 2.0 / n2, F32).at[0, 0].set(1.0 / n2)
    tables = []
    for o in range(HY_ORDER):
        f_sl = slice((2 * o) * w, (2 * o + 1) * w)
        b_sl = slice((2 * o + 1) * w, (2 * o + 2) * w)
        hb0 = filt[0:1, b_sl]
        tre = ure[:, f_sl] + ure[:, b_sl] - hb0
        tim = uim[:, f_sl] - uim[:, b_sl]
        t_nyq = uim[0:1, f_sl] + uim[0:1, b_sl] - hb0
        first = (jnp.arange(length) == 0)[:, None]
        a = tre * scale
        bm = jnp.where(first, 0.0, tim * scale)
        c = bm
        d = jnp.where(first, t_nyq * scale, tre * scale)
        tables.append((a, bm, c, d))
    return tables


def _hyena_seq(mats, vx, vx_bf, lp, n_batch):
    length = vx.shape[1]
    filt, filt_bf = _hy_filters(length, lp)
    ure, uim = _dft_fwd(mats, filt_bf[None], 0)
    tables = _hy_tap_tables(ure, uim, filt, length)
    bias = lp['hy_bias'].astype(F32)
    y, y_bf = vx, vx_bf
    for o in range(HY_ORDER):
        zre, zim = _dft_fwd(mats, y_bf, 0, tables[o])
        bias_row = jnp.tile(bias[o][None, :], (1, n_batch))
        last = o == HY_ORDER - 1
        res = _dft_inv(mats, zre, zim, vx, o + 1, y, 0, bias_row, last)
        if last:
            return res[0]
        y, y_bf = res


def _lru_kernel(u_ref, g_ref, cw_ref, cb_ref, wa_ref, ba_ref, wx_ref, bx_ref, lam_ref, o_ref,
                pad_ref, y_ref, *, seq, ctx_len):
    tc = LRU_CHUNK
    halo = LRU_HALO
    width = LRU_WIDTH
    lat_off = halo
    ctx_off = 2 * halo + seq
    zero = jnp.zeros((halo, width), F32)
    pad_ref[0:halo, :] = zero
    pad_ref[lat_off:lat_off + seq, :] = u_ref[0, 0:seq, :]
    pad_ref[lat_off + seq:ctx_off, :] = zero
    pad_ref[ctx_off:ctx_off + ctx_len, :] = u_ref[0, seq:seq + ctx_len, :]
    pad_ref[ctx_off + ctx_len:ctx_off + ctx_len + halo, :] = zero
    row = lax.broadcasted_iota(I32, (tc, width), 0)
    n_win = tc + 2 * halo

    def chunk(pad_off, y_off, s, carry, d):
        wstart = pl.multiple_of(pad_off + s - halo, 8)
        win = pad_ref[pl.ds(wstart, n_win), :]
        cw = cw_ref[d]
        xc = cb_ref[d]
        for k in range(LRU_CONV):
            shift = (LRU_CONV - 1 - k) if d == 0 else -k
            rolled = win if shift == 0 else pltpu.roll(win, shift % n_win, axis=0)
            xc = xc + cw[k:k + 1] * rolled[halo:halo + tc]
        xb = xc.astype(BF16)
        r = _sigmoid(_dot(xb, wa_ref[d]) + ba_ref[d])
        gi = _sigmoid(_dot(xb, wx_ref[d]) + bx_ref[d])
        lam = lam_ref[d]
        softplus = jnp.maximum(-lam, 0.0) + jnp.log1p(jnp.exp(-jnp.abs(lam)))
        log_a = -LRU_C * r * softplus
        a = jnp.exp(log_a)
        bt = jnp.sqrt(-jnp.tanh(log_a) * (a * a + 1.0)) * (gi * xc)
        sft = 1
        while sft < tc:
            if d == 0:
                keep = row >= sft
                a_s = jnp.where(keep, pltpu.roll(a, sft, axis=0), 1.0)
                b_s = jnp.where(keep, pltpu.roll(bt, sft, axis=0), 0.0)
            else:
                keep = row < tc - sft
                a_s = jnp.where(keep, pltpu.roll(a, tc - sft, axis=0), 1.0)
                b_s = jnp.where(keep, pltpu.roll(bt, tc - sft, axis=0), 0.0)
            bt = a * b_s + bt
            a = a * a_s
            sft *= 2
        h = a * carry + bt
        yo = pl.multiple_of(y_off + s, 8)
        if d == 0:
            y_ref[pl.ds(yo, tc), :] = h
            return h[tc - 1:tc]
        y_ref[pl.ds(yo, tc), :] = y_ref[pl.ds(yo, tc), :] + h
        return h[0:1]

    n_lat = seq // tc
    n_ctx = ctx_len // tc
    for d in range(2):
        carry = jnp.zeros((1, width), F32)
        order = range(n_ctx) if d == 0 else range(n_ctx - 1, -1, -1)
        for c in order:
            carry = chunk(ctx_off, seq, c * tc, carry, d)

        def body(j, cr, d=d):
            jj = j if d == 0 else n_lat - 1 - j
            return chunk(lat_off, 0, jj * tc, cr, d)

        lax.fori_loop(0, n_lat, body, carry)
    o_ref[0] = (y_ref[...] * _gelu_tanh(g_ref[0])).astype(BF16)


def _block_diag(w):
    nd, nb, c, _ = w.shape
    out = jnp.zeros((nd, nb * c, nb * c), w.dtype)
    for n in range(nb):
        out = out.at[:, n * c:(n + 1) * c, n * c:(n + 1) * c].set(w[:, n])
    return out


def _lru_mixer(lu, lg, lp, seq):
    b, nt, w = lu.shape
    ctx_len = nt - seq
    row3 = lambda a: a.reshape(2, 1, w)
    args = (_pad_to(lp['lru_conv_w'], 8, 1), row3(lp['lru_conv_b']), _block_diag(lp['lru_wa']).astype(BF16),
            row3(lp['lru_ba']), _block_diag(lp['lru_wx']).astype(BF16), row3(lp['lru_bx']), row3(lp['lru_lambda']))
    full = lambda a: pl.BlockSpec(a.shape, lambda bb: (0,) * a.ndim)
    tok = pl.BlockSpec((1, nt, w), lambda bb: (bb, 0, 0))
    return pl.pallas_call(
        functools.partial(_lru_kernel, seq=seq, ctx_len=ctx_len),
        grid=(b,),
        in_specs=[tok, tok] + [full(a) for a in args],
        out_specs=tok,
        out_shape=jax.ShapeDtypeStruct((b, nt, w), BF16),
        scratch_shapes=[pltpu.VMEM((nt + 3 * LRU_HALO, w), F32), pltpu.VMEM((nt, w), F32)],
        compiler_params=_cparams("parallel"),
        name="rglru_scan",
    )(lu, lg, *args)


def _merge_kernel(x_ref, mod_ref, g1_ref, a_ref, b_ref, c_ref, d_ref, wg_ref, wa_ref, wb_ref, wc_ref, wd_ref,
                  wo_ref, o_ref):
    m = mod_ref[0, 0]
    x = x_ref[0]
    h = _normmod(x, g1_ref[...], m[1:2], m[0:1]).astype(BF16)
    dm = x.shape[-1]
    acc = None
    for k, (br, w) in enumerate(((a_ref, wa_ref), (b_ref, wb_ref), (c_ref, wc_ref), (d_ref, wd_ref))):
        gate = _sigmoid(_dot(h, wg_ref[:, k * dm:(k + 1) * dm]))
        term = gate * _dot(br[0], w[...])
        acc = term if acc is None else acc + term
    y = _dot(acc.astype(BF16), wo_ref[...])
    o_ref[0] = x + m[2:3] * y


def _merge(xa, modtab, g1, branches, w_gate, lp, n_lat_tiles):
    b, nt, d = xa.shape
    wbr = lp['w_branch']
    head_rows = lambda w, dv: jnp.concatenate(
        [_pad_to(w[hd * dv:(hd + 1) * dv], LANE, 0) for hd in range(4)], axis=0)
    weights = (head_rows(wbr[0], MLA_V).astype(BF16), head_rows(wbr[1], NA_HEAD_DIM).astype(BF16),
               wbr[2].astype(BF16), wbr[3].astype(BF16), lp['w_out'].astype(BF16))
    full = lambda a: pl.BlockSpec(a.shape, lambda bb, i: (0,) * a.ndim)
    tok = lambda n: pl.BlockSpec((1, TM, n), lambda bb, i: (bb, i, 0))
    return pl.pallas_call(
        _merge_kernel,
        grid=(b, nt // TM),
        in_specs=[tok(d), pl.BlockSpec((1, 1, 8, d), _kind_map(n_lat_tiles)), full(g1)]
                 + [tok(br.shape[-1]) for br in branches] + [full(w_gate)] + [full(w) for w in weights],
        out_specs=tok(d),
        out_shape=jax.ShapeDtypeStruct((b, nt, d), F32),
        compiler_params=_cparams("parallel", "parallel"),
        name="merge_branches",
    )(xa, modtab, g1, *branches, w_gate, *weights)


U32 = jnp.uint32
EXPERT_ROWS = 512


def _pack_pair(x):
    n = x.shape[-1] // 2
    hi = lax.bitcast_convert_type(x[:, :n].astype(BF16).astype(F32), U32)
    lo = lax.bitcast_convert_type(x[:, n:].astype(BF16).astype(F32), U32)
    return hi | (lo >> 16)


def _unpack_pair(p):
    hi = lax.bitcast_convert_type(p & jnp.uint32(0xFFFF0000), F32)
    lo = lax.bitcast_convert_type(p << 16, F32)
    return hi, lo


def _dot_pair(hi, lo, w_ref):
    half = hi.shape[-1]
    return _dot(hi.astype(BF16), w_ref[0:half, :]) + _dot(lo.astype(BF16), w_ref[half:2 * half, :])


def _router_kernel(x_ref, mod_ref, g2_ref, rw_ref, rb_ref, tri_ref, h2_o, idx_o, wts_o, rank_o, cnt_o, carry):
    i = pl.program_id(0)

    @pl.when(i == 0)
    def _():
        carry[...] = jnp.zeros_like(carry)

    m = mod_ref[0, 0]
    h2 = _normmod(x_ref[0], g2_ref[...], m[4:5], m[3:4])
    h2_o[...] = _pack_pair(h2)
    scores = _sigmoid(_hp_dot(h2, rw_ref[...]))
    lane = lax.broadcasted_iota(I32, scores.shape, 1)
    biased = jnp.where(lane < N_EXPERTS, scores + rb_ref[...], -jnp.inf)
    picks = []
    onehot_all = jnp.zeros(scores.shape, F32)
    for _ in range(TOP_K):
        best = jnp.max(biased, axis=-1, keepdims=True)
        arg = jnp.min(jnp.where(biased == best, lane, LANE), axis=-1, keepdims=True)
        hit = lane == arg
        sel = jnp.sum(jnp.where(hit, scores, 0.0), axis=-1, keepdims=True)
        biased = jnp.where(hit, -jnp.inf, biased)
        onehot_all = onehot_all + jnp.where(hit, 1.0, 0.0)
        picks.append((arg, hit, sel))
    total = picks[0][2]
    for _, _, sel in picks[1:]:
        total = total + sel
    earlier = _dot(tri_ref[...], onehot_all.astype(BF16)) + carry[...]
    slot = lax.broadcasted_iota(I32, (scores.shape[0], TOPK_PAD), 1)
    idx = jnp.zeros(slot.shape, I32)
    wts = jnp.zeros(slot.shape, F32)
    rank = jnp.zeros(slot.shape, I32)
    for k, (arg, hit, sel) in enumerate(picks):
        rk = jnp.sum(jnp.where(hit, earlier, 0.0), axis=-1, keepdims=True)
        idx = jnp.where(slot == k, arg, idx)
        wts = jnp.where(slot == k, sel / total * ROUTED_SCALE, wts)
        rank = jnp.where(slot == k, rk.astype(I32), rank)
    idx_o[...] = idx
    wts_o[...] = wts
    rank_o[...] = rank
    carry[...] = carry[...] + jnp.sum(onehot_all, axis=0, keepdims=True)
    cnt_o[...] = carry[...]


def _tile_maps(tiles_used, n_lat_tiles):
    tok = lambda i: (i // tiles_used, i % tiles_used, 0)
    mod = lambda i: (i // tiles_used, jnp.where(i % tiles_used >= n_lat_tiles, 1, 0), 0, 0)
    return tok, mod


def _route(x1, modtab, g2, lp, n_lat_tiles, tiles_used):
    b, _, d = x1.shape
    n_tiles = b * tiles_used
    t = n_tiles * TM
    rw = _pad_to(lp['router_w'], LANE, 1)
    rb = _pad_to(lp['router_bias'].reshape(1, -1), LANE, 1)
    tri = (np.arange(TM)[:, None] > np.arange(TM)[None, :]).astype(np.float32)
    tri = jnp.asarray(tri, BF16)
    flat = lambda n: pl.BlockSpec((TM, n), lambda i: (i, 0))
    full = lambda a: pl.BlockSpec(a.shape, lambda i: (0,) * a.ndim)
    tok, mod = _tile_maps(tiles_used, n_lat_tiles)
    return pl.pallas_call(
        _router_kernel,
        grid=(n_tiles,),
        in_specs=[pl.BlockSpec((1, TM, d), tok), pl.BlockSpec((1, 1, 8, d), mod),
                  full(g2), full(rw), full(rb), full(tri)],
        out_specs=[flat(d // 2), flat(TOPK_PAD), flat(TOPK_PAD), flat(TOPK_PAD),
                   pl.BlockSpec((1, LANE), lambda i: (0, 0))],
        out_shape=[jax.ShapeDtypeStruct((t, d // 2), U32), jax.ShapeDtypeStruct((t, TOPK_PAD), I32),
                   jax.ShapeDtypeStruct((t, TOPK_PAD), F32), jax.ShapeDtypeStruct((t, TOPK_PAD), I32),
                   jax.ShapeDtypeStruct((1, LANE), F32)],
        scratch_shapes=[pltpu.VMEM((1, LANE), F32)],
        compiler_params=_cparams("arbitrary"),
        name="moe_router",
    )(x1, modtab, g2, rw, rb, tri)


def _row_copy(src, s_row, dst, d_row, sem):
    return pltpu.make_async_copy(src.at[pl.ds(s_row, 1)], dst.at[pl.ds(d_row, 1)], sem)


def _dispatch_kernel(dest_ref, h2_ref, xs_hbm, sem):
    def issue(t, carry):
        for k in range(TOP_K):
            _row_copy(h2_ref, t, xs_hbm, dest_ref[0, 0, t * TOP_K + k], sem).start()
        return carry

    lax.fori_loop(0, TM, issue, 0)

    def drain(t, carry):
        for k in range(TOP_K):
            _row_copy(h2_ref, 0, xs_hbm, 0, sem).wait()
        return carry

    lax.fori_loop(0, TM, drain, 0)


def _dispatch(h2p, dest, n_rows):
    t, dh = h2p.shape
    return pl.pallas_call(
        _dispatch_kernel,
        grid=(t // TM,),
        in_specs=[pl.BlockSpec((1, 1, TM * TOP_K), lambda i: (i, 0, 0), memory_space=pltpu.SMEM),
                  pl.BlockSpec((TM, dh), lambda i: (i, 0))],
        out_specs=pl.BlockSpec(memory_space=pl.ANY),
        out_shape=jax.ShapeDtypeStruct((n_rows, dh), U32),
        scratch_shapes=[pltpu.SemaphoreType.DMA(())],
        compiler_params=_cparams("arbitrary"),
        name="moe_dispatch",
    )(dest, h2p)


def _expert_kernel(be_ref, nv_ref, xs_ref, wg_ref, wu_ref, wd_ref, ys_o, wg_s, wu_s, wd_s):
    i = pl.program_id(0)
    prev = be_ref[jnp.maximum(i - 1, 0)]

    @pl.when((i == 0) | (be_ref[i] != prev))
    def _():
        wg_s[...] = wg_ref[0].astype(BF16)
        wu_s[...] = wu_ref[0].astype(BF16)
        wd_s[...] = wd_ref[0].astype(BF16)

    @pl.when(nv_ref[i] > 0)
    def _():
        row = lax.broadcasted_iota(I32, xs_ref.shape, 0)
        hi, lo = _unpack_pair(jnp.where(row < nv_ref[i], xs_ref[...], jnp.uint32(0)))
        hid = _silu(_dot_pair(hi, lo, wg_s)) * _dot_pair(hi, lo, wu_s)
        ys_o[...] = _pack_pair(_dot(hid.astype(BF16), wd_s[...]))

    @pl.when(nv_ref[i] <= 0)
    def _():
        ys_o[...] = jnp.zeros_like(ys_o)


def _experts(xs, block_e, n_valid, lp):
    n_rows, dh = xs.shape
    d = 2 * dh
    n_blocks = n_rows // EXPERT_ROWS
    hid = EXPERT_HIDDEN
    grid_spec = pltpu.PrefetchScalarGridSpec(
        num_scalar_prefetch=2,
        grid=(n_blocks,),
        in_specs=[pl.BlockSpec((EXPERT_ROWS, dh), lambda i, be, nv: (i, 0)),
                  pl.BlockSpec((1, d, hid), lambda i, be, nv: (be[i], 0, 0)),
                  pl.BlockSpec((1, d, hid), lambda i, be, nv: (be[i], 0, 0)),
                  pl.BlockSpec((1, hid, d), lambda i, be, nv: (be[i], 0, 0))],
        out_specs=pl.BlockSpec((EXPERT_ROWS, dh), lambda i, be, nv: (i, 0)),
        scratch_shapes=[pltpu.VMEM((d, hid), BF16), pltpu.VMEM((d, hid), BF16), pltpu.VMEM((hid, d), BF16)],
    )
    return pl.pallas_call(
        _expert_kernel,
        grid_spec=grid_spec,
        out_shape=jax.ShapeDtypeStruct((n_rows, dh), U32),
        compiler_params=_cparams("arbitrary"),
        name="moe_experts",
    )(block_e, n_valid, xs, lp['exp_w_gate'], lp['exp_w_up'], lp['exp_w_down'])


def _combine_kernel(dest_ref, ys_hbm, wts_ref, h2_ref, x_ref, mod_ref, sg_ref, su_ref, sd_ref, gf_ref, o_ref,
                    buf, sem, *, final):
    def issue(t, carry):
        for k in range(TOP_K):
            _row_copy(ys_hbm, dest_ref[0, 0, t * TOP_K + k], buf.at[k], t, sem).start()
        return carry

    lax.fori_loop(0, TM, issue, 0)
    h_hi, h_lo = _unpack_pair(h2_ref[...])
    hid = _silu(_dot_pair(h_hi, h_lo, sg_ref)) * _dot_pair(h_hi, h_lo, su_ref)
    shared = _dot(hid.astype(BF16), sd_ref[...])

    def drain(t, carry):
        for k in range(TOP_K):
            _row_copy(ys_hbm, 0, buf.at[k], 0, sem).wait()
        return carry

    lax.fori_loop(0, TM, drain, 0)
    wts = wts_ref[...]
    r_hi = None
    r_lo = None
    for k in range(TOP_K):
        y_hi, y_lo = _unpack_pair(buf[k])
        w = wts[:, k:k + 1]
        r_hi = w * y_hi if r_hi is None else r_hi + w * y_hi
        r_lo = w * y_lo if r_lo is None else r_lo + w * y_lo
    f = shared + jnp.concatenate([r_hi, r_lo], axis=1)
    m = mod_ref[0, 0]
    x2 = x_ref[0] + m[5:6] * f
    if final:
        x2 = x2 * lax.rsqrt(jnp.mean(x2 * x2, axis=-1, keepdims=True) + NORM_EPS) * gf_ref[...]
    o_ref[0] = x2


def _combine(ys, dest, wts, h2p, x1, modtab, lp, g_final, n_lat_tiles, tiles_used, final):
    b, _, d = x1.shape
    dh = d // 2
    weights = (lp['sh_w_gate'].astype(BF16), lp['sh_w_up'].astype(BF16), lp['sh_w_down'].astype(BF16),
               g_final.reshape(1, -1))
    full = lambda a: pl.BlockSpec(a.shape, lambda i: (0,) * a.ndim)
    tok, mod = _tile_maps(tiles_used, n_lat_tiles)
    return pl.pallas_call(
        functools.partial(_combine_kernel, final=final),
        grid=(b * tiles_used,),
        in_specs=[pl.BlockSpec((1, 1, TM * TOP_K), lambda i: (i, 0, 0), memory_space=pltpu.SMEM),
                  pl.BlockSpec(memory_space=pl.ANY),
                  pl.BlockSpec((TM, TOPK_PAD), lambda i: (i, 0)),
                  pl.BlockSpec((TM, dh), lambda i: (i, 0)),
                  pl.BlockSpec((1, TM, d), tok), pl.BlockSpec((1, 1, 8, d), mod)]
                 + [full(w) for w in weights],
        out_specs=pl.BlockSpec((1, TM, d), tok),
        out_shape=jax.ShapeDtypeStruct((b, tiles_used * TM, d), F32),
        scratch_shapes=[pltpu.VMEM((TOP_K, TM, dh), U32), pltpu.SemaphoreType.DMA(())],
        compiler_params=_cparams("arbitrary"),
        name="moe_combine",
    )(dest, ys, wts, h2p, x1, modtab, *weights)


def _moe(x1, modtab, g2, lp, g_final, n_lat_tiles, final):
    b, nt, d = x1.shape
    tiles_used = n_lat_tiles if final else nt // TM
    t = b * tiles_used * TM
    h2p, idx, wts, rank, cnt = _route(x1, modtab, g2, lp, n_lat_tiles, tiles_used)
    counts = cnt[0, :N_EXPERTS].astype(I32)
    padded = (counts + EXPERT_ROWS - 1) // EXPERT_ROWS * EXPERT_ROWS
    p_ends = jnp.cumsum(padded)
    p_starts = p_ends - padded
    n_blocks = (t * TOP_K + N_EXPERTS * (EXPERT_ROWS - 1)) // EXPERT_ROWS
    dest = (p_starts[idx[:, :TOP_K]] + rank[:, :TOP_K]).reshape(t // TM, 1, TM * TOP_K)
    blk_start = jnp.arange(n_blocks, dtype=I32) * EXPERT_ROWS
    block_e = jnp.minimum(jnp.sum((p_ends[None, :] <= blk_start[:, None]).astype(I32), axis=1), N_EXPERTS - 1)
    n_valid = jnp.clip((p_starts + counts)[block_e] - blk_start, 0, EXPERT_ROWS).astype(I32)
    xs = _dispatch(h2p, dest, n_blocks * EXPERT_ROWS)
    ys = _experts(xs, block_e, n_valid, lp)
    return _combine(ys, dest, wts, h2p, x1, modtab, lp, g_final, n_lat_tiles, tiles_used, final)


def _layer(xa, c, c_ctx, lp, consts, g_final, seq, final):
    b, nt, d = xa.shape
    n_lat_tiles = seq // TM
    rope, mats_lat, mats_ctx = consts
    modtab = _mod_table(c, c_ctx, lp['w_mod'], lp['b_mod'])
    g1 = lp['g_norm1'].reshape(1, -1)
    g2 = lp['g_norm2'].reshape(1, -1)
    pw = _proj_weights(lp)
    q, k, v, nq, nk, nv, hy, lu, lg = _project(xa, modtab, g1, pw, rope, n_lat_tiles)
    br_a = _mla_attention(q, k, v, seq)
    br_b = _na_attention(nq, nk, nv, _na_bias_tables(lp['na_rpb'], seq // GRID_W), seq)
    pre = _hy_pre(hy, lp['hy_short_w'], lp['hy_short_b'], seq)
    lat_f, lat_bf, ctx_f, ctx_bf = pre
    c_lat = _hyena_seq(mats_lat, lat_f, lat_bf, lp, b)
    c_ctx_out = _hyena_seq(mats_ctx, ctx_f, ctx_bf, lp, b)
    br_c = jnp.concatenate([c_lat, c_ctx_out], axis=1)
    br_d = _lru_mixer(lu, lg, lp, seq)
    x1 = _merge(xa, modtab, g1, (br_a, br_b, br_c, br_d), pw['w_gate'], lp, n_lat_tiles)
    return _moe(x1, modtab, g2, lp, g_final, n_lat_tiles, final)


_LAYER_KEYS = ('w_mod', 'b_mod', 'g_norm1', 'g_norm2', 'w_in', 'mla_g_q', 'mla_w_uq', 'mla_g_kv', 'mla_w_ukv',
               'na_rpb', 'hy_short_w', 'hy_short_b', 'hy_w1', 'hy_b1', 'hy_w2', 'hy_b2', 'hy_w3', 'hy_decay',
               'hy_bias', 'lru_conv_w', 'lru_conv_b', 'lru_wa', 'lru_ba', 'lru_wx', 'lru_bx', 'lru_lambda',
               'w_branch', 'w_out', 'router_w', 'router_bias', 'exp_w_gate', 'exp_w_up', 'exp_w_down',
               'sh_w_gate', 'sh_w_up', 'sh_w_down')


def kernel(x, c, ctx, c_ctx, w_mod, b_mod, g_norm1, g_norm2, w_in, mla_g_q, mla_w_uq, mla_g_kv, mla_w_ukv, na_rpb, hy_short_w, hy_short_b, hy_w1, hy_b1, hy_w2, hy_b2, hy_w3, hy_decay, hy_bias, lru_conv_w, lru_conv_b, lru_wa, lru_ba, lru_wx, lru_bx, lru_lambda, w_branch, w_out, router_w, router_bias, exp_w_gate, exp_w_up, exp_w_down, sh_w_gate, sh_w_up, sh_w_down, g_final):
    stacked = dict(zip(_LAYER_KEYS, (w_mod, b_mod, g_norm1, g_norm2, w_in, mla_g_q, mla_w_uq, mla_g_kv, mla_w_ukv,
                                     na_rpb, hy_short_w, hy_short_b, hy_w1, hy_b1, hy_w2, hy_b2, hy_w3, hy_decay,
                                     hy_bias, lru_conv_w, lru_conv_b, lru_wa, lru_ba, lru_wx, lru_bx, lru_lambda,
                                     w_branch, w_out, router_w, router_bias, exp_w_gate, exp_w_up, exp_w_down,
                                     sh_w_gate, sh_w_up, sh_w_down)))
    b, seq, d = x.shape
    ctx_len = ctx.shape[1]
    depth = w_mod.shape[0]
    assert seq % TM == 0 and ctx_len % TM == 0 and seq // GRID_W >= NA_KEY_ROWS + 1
    xa = jnp.concatenate([x, ctx], axis=1)
    consts = (_rope_tables(seq, seq + ctx_len), _dft_matrices(seq), _dft_matrices(ctx_len))
    for i in range(depth):
        lp = {name: w[i] for name, w in stacked.items()}
        xa = _layer(xa, c, c_ctx, lp, consts, g_final, seq, i == depth - 1)
    return xa
```

```python
import functools
import math

import numpy as np
import jax
import jax.numpy as jnp
from jax import lax
from jax.experimental import pallas as pl
from jax.experimental.pallas import tpu as pltpu
from jax.experimental.pallas import tpu_sc as plsc

F32 = jnp.float32
BF16 = jnp.bfloat16
I32 = jnp.int32

TM = 256
LANE = 128
GRID_W = 64
N_MOD = 6
NORM_EPS = 1e-6

MLA_HEADS, MLA_NOPE, MLA_ROPE, MLA_V = 4, 64, 32, 64
MLA_Q_RANK, MLA_KV_RANK = 192, 128
MLA_Q_PAD = 256
ROPE_THETA = 10000.0

NA_HEADS, NA_HEAD_DIM, NA_WIN_R, NA_WIN_C = 4, 64, 8, 16
NA_TILE_ROWS = TM // GRID_W
NA_KEY_ROWS = NA_TILE_ROWS + NA_WIN_R - 1
NA_KEYS = NA_KEY_ROWS * GRID_W

HY_WIDTH, HY_ORDER, HY_SHORT, HY_BANDS, HY_FFN = 256, 2, 3, 16, 64
HY_EMB = 2 * HY_BANDS + 1

LRU_WIDTH, LRU_BLOCKS, LRU_CONV, LRU_C = 256, 4, 4, 8.0
LRU_CHUNK = 256
LRU_HALO = 8

N_EXPERTS, TOP_K, EXPERT_HIDDEN, ROUTED_SCALE, MOE_BLOCK = 64, 6, 256, 2.5, 256
TOPK_PAD = 8

VMEM_LIMIT = 52 * 1024 * 1024


def _cparams(*sem):
    return pltpu.CompilerParams(dimension_semantics=sem, vmem_limit_bytes=VMEM_LIMIT)


def _dot(a, b):
    return jnp.dot(a, b, preferred_element_type=F32)


def _dot_nt(a, b):
    return lax.dot_general(a, b, (((1,), (1,)), ((), ())), preferred_element_type=F32)


def _sigmoid(x):
    return 1.0 / (1.0 + jnp.exp(-x))


def _silu(x):
    return x * _sigmoid(x)


def _gelu_tanh(x):
    return 0.5 * x * (1.0 + jnp.tanh(math.sqrt(2.0 / math.pi) * (x + 0.044715 * (x * x * x))))


def _normmod(x, g, scale, shift):
    y = x * lax.rsqrt(jnp.mean(x * x, axis=-1, keepdims=True) + NORM_EPS) * g
    return y * (1.0 + scale) + shift


def _mod_kernel(c_ref, w_ref, b_ref, o_ref):
    s = _silu(c_ref[...])
    o_ref[...] = _dot(s.astype(BF16), w_ref[...].astype(BF16)) + b_ref[...]


def _mod_table(c, c_ctx, w_mod, b_mod):
    b, d = c.shape
    rows = 16
    cc = jnp.zeros((rows, d), F32).at[:b].set(c).at[b].set(c_ctx)
    tn = 1024
    mod = pl.pallas_call(
        _mod_kernel,
        grid=(N_MOD * d // tn,),
        in_specs=[pl.BlockSpec((rows, d), lambda j: (0, 0)),
                  pl.BlockSpec((d, tn), lambda j: (0, j)),
                  pl.BlockSpec((1, tn), lambda j: (0, j))],
        out_specs=pl.BlockSpec((rows, tn), lambda j: (0, j)),
        out_shape=jax.ShapeDtypeStruct((rows, N_MOD * d), F32),
        compiler_params=_cparams("arbitrary"),
        name="mod_vectors",
    )(cc, w_mod, b_mod.reshape(1, -1))
    lat = mod[:b].reshape(b, N_MOD, d)
    ctx = jnp.broadcast_to(mod[b].reshape(1, N_MOD, d), (b, N_MOD, d))
    tab = jnp.stack([lat, ctx], axis=1)
    return jnp.pad(tab, ((0, 0), (0, 0), (0, 8 - N_MOD), (0, 0)))


_C_QLAT, _C_KVLAT, _C_KR, _C_KRR, _C_NA, _C_HY, _C_LU, _C_LG, _C_END = (
    0, 256, 384, 512, 640, 640 + 3 * NA_HEADS * LANE, 640 + 1536 + 768, 640 + 1536 + 1024, 640 + 1536 + 1280)


def _proj_kernel(x_ref, mod_ref, g1_ref, w1_ref, gq_ref, gkv_ref, wuq_ref, wuqr_ref, wk_ref, wv_ref,
                 cos_ref, sin_ref, q_o, k_o, v_o, nq_o, nk_o, nv_o, hy_o, lu_o, lg_o):
    m = mod_ref[0, 0]
    h = _normmod(x_ref[0], g1_ref[...], m[1:2], m[0:1])
    z = _dot(h.astype(BF16), w1_ref[...])
    qlat = z[:, _C_QLAT:_C_KVLAT]
    kvlat = z[:, _C_KVLAT:_C_KR]
    kr = z[:, _C_KR:_C_KRR]
    krr = z[:, _C_KRR:_C_NA]
    qn = qlat * lax.rsqrt(jnp.sum(qlat * qlat, axis=-1, keepdims=True) * (1.0 / MLA_Q_RANK) + NORM_EPS) * gq_ref[...]
    kvn = kvlat * lax.rsqrt(jnp.mean(kvlat * kvlat, axis=-1, keepdims=True) + NORM_EPS) * gkv_ref[...]
    qn = qn.astype(BF16)
    kvn = kvn.astype(BF16)
    q = _dot(qn, wuq_ref[...])
    qr = _dot(qn, wuqr_ref[...])
    kk = _dot(kvn, wk_ref[...])
    vv = _dot(kvn, wv_ref[...])
    cos = cos_ref[...]
    sin = sin_ref[...]
    krope = kr * cos + krr * sin
    for hd in range(MLA_HEADS):
        sl = slice(hd * LANE, (hd + 1) * LANE)
        q_o[0, hd] = (q[:, sl] * cos + qr[:, sl] * sin).astype(BF16)
        k_o[0, hd] = (kk[:, sl] + krope).astype(BF16)
        v_o[0, hd] = vv[:, sl].astype(BF16)
    for hd in range(NA_HEADS):
        for which, ref in enumerate((nq_o, nk_o, nv_o)):
            lo = _C_NA + (which * NA_HEADS + hd) * LANE
            ref[0, hd] = z[:, lo:lo + LANE].astype(BF16)
    hy_o[0] = z[:, _C_HY:_C_LU]
    lu_o[0] = z[:, _C_LU:_C_LG]
    lg_o[0] = z[:, _C_LG:_C_END]


def _pad_to(a, n, axis):
    pad = [(0, 0)] * a.ndim
    pad[axis] = (0, n - a.shape[axis])
    return jnp.pad(a, pad)


def _rot_cols(w):
    half = w.shape[-1] // 2
    return jnp.concatenate([-w[..., half:], w[..., :half]], axis=-1)


def _head_blocks(cols_per_head):
    out = []
    for pieces in cols_per_head:
        k = pieces[0][0].shape[0]
        blk = jnp.zeros((k, LANE), F32)
        for arr, off in pieces:
            blk = blk.at[:, off:off + arr.shape[1]].set(arr)
        out.append(blk)
    return jnp.concatenate(out, axis=1)


def _proj_weights(lp):
    w_in = lp['w_in']
    d = w_in.shape[0]
    o = 0
    parts = {}
    for name, n in (('q', MLA_Q_RANK), ('kv', MLA_KV_RANK), ('kr', MLA_ROPE), ('na', 3 * NA_HEADS * NA_HEAD_DIM),
                    ('hy', 3 * HY_WIDTH), ('lu', LRU_WIDTH), ('lg', LRU_WIDTH), ('gt', 4 * d)):
        parts[name] = w_in[:, o:o + n]
        o += n
    zeros = lambda n: jnp.zeros((d, n), F32)
    kr_blk = jnp.concatenate([zeros(MLA_NOPE), parts['kr'], zeros(LANE - MLA_NOPE - MLA_ROPE)], axis=1)
    krr_blk = jnp.concatenate([zeros(MLA_NOPE), _rot_cols(parts['kr']), zeros(LANE - MLA_NOPE - MLA_ROPE)], axis=1)
    na_scale = NA_HEAD_DIM ** -0.5
    na_cols = []
    for which in range(3):
        for hd in range(NA_HEADS):
            lo = (which * NA_HEADS + hd) * NA_HEAD_DIM
            blk = parts['na'][:, lo:lo + NA_HEAD_DIM] * (na_scale if which == 0 else 1.0)
            na_cols.append(_pad_to(blk, LANE, 1))
    w1 = jnp.concatenate([_pad_to(parts['q'], MLA_Q_PAD, 1), parts['kv'], kr_blk, krr_blk] + na_cols
                         + [parts['hy'], parts['lu'], parts['lg']], axis=1).astype(BF16)
    mla_scale = (MLA_NOPE + MLA_ROPE) ** -0.5
    wuq = _pad_to(lp['mla_w_uq'], MLA_Q_PAD, 0) * mla_scale
    dq = MLA_NOPE + MLA_ROPE
    wuq_main = _head_blocks([[(wuq[:, hd * dq:hd * dq + dq], 0)] for hd in range(MLA_HEADS)])
    wuq_rot = _head_blocks([[(_rot_cols(wuq[:, hd * dq + MLA_NOPE:hd * dq + dq]), MLA_NOPE)]
                            for hd in range(MLA_HEADS)])
    dkv = MLA_NOPE + MLA_V
    wukv = lp['mla_w_ukv']
    wk = _head_blocks([[(wukv[:, hd * dkv:hd * dkv + MLA_NOPE], 0)] for hd in range(MLA_HEADS)])
    wv = _head_blocks([[(wukv[:, hd * dkv + MLA_NOPE:hd * dkv + dkv], 0)] for hd in range(MLA_HEADS)])
    gq = _pad_to(lp['mla_g_q'].reshape(1, -1), MLA_Q_PAD, 1)
    gkv = lp['mla_g_kv'].reshape(1, -1)
    return dict(w1=w1, w_gate=parts['gt'].astype(BF16), gq=gq, gkv=gkv, wuq=wuq_main.astype(BF16),
                wuq_rot=wuq_rot.astype(BF16), wk=wk.astype(BF16), wv=wv.astype(BF16))


def _rope_tables(seq, n_tok):
    t = jnp.arange(seq, dtype=I32)
    row = (t // GRID_W).astype(F32)
    col = (t % GRID_W).astype(F32)
    n_axis = MLA_ROPE // 4
    inv_freq = ROPE_THETA ** (-jnp.arange(n_axis, dtype=F32) / n_axis)
    ang = jnp.concatenate([row[:, None] * inv_freq, col[:, None] * inv_freq], axis=-1)
    cos = jnp.concatenate([jnp.cos(ang), jnp.cos(ang)], axis=-1)
    sin = jnp.concatenate([jnp.sin(ang), jnp.sin(ang)], axis=-1)
    cos_t = jnp.ones((n_tok, LANE), F32).at[:seq, MLA_NOPE:MLA_NOPE + MLA_ROPE].set(cos)
    sin_t = jnp.zeros((n_tok, LANE), F32).at[:seq, MLA_NOPE:MLA_NOPE + MLA_ROPE].set(sin)
    return cos_t, sin_t


def _kind_map(n_lat_tiles):
    return lambda b, i: (b, jnp.where(i >= n_lat_tiles, 1, 0), 0, 0)


def _project(xa, modtab, g1, pw, rope, n_lat_tiles):
    b, nt, d = xa.shape
    cos_t, sin_t = rope
    full = lambda a: pl.BlockSpec(a.shape, lambda bb, i: (0,) * a.ndim)
    head_out = lambda: pl.BlockSpec((1, MLA_HEADS, TM, LANE), lambda bb, i: (bb, 0, i, 0))
    tok_out = lambda n: pl.BlockSpec((1, TM, n), lambda bb, i: (bb, i, 0))
    head_shape = jax.ShapeDtypeStruct((b, MLA_HEADS, nt, LANE), BF16)
    tok_shape = lambda n: jax.ShapeDtypeStruct((b, nt, n), F32)
    weights = (g1, pw['w1'], pw['gq'], pw['gkv'], pw['wuq'], pw['wuq_rot'], pw['wk'], pw['wv'])
    return pl.pallas_call(
        _proj_kernel,
        grid=(b, nt // TM),
        in_specs=[pl.BlockSpec((1, TM, d), lambda bb, i: (bb, i, 0)),
                  pl.BlockSpec((1, 1, 8, d), _kind_map(n_lat_tiles))]
                 + [full(w) for w in weights]
                 + [pl.BlockSpec((TM, LANE), lambda bb, i: (i, 0))] * 2,
        out_specs=[head_out() for _ in range(6)] + [tok_out(3 * HY_WIDTH), tok_out(LRU_WIDTH), tok_out(LRU_WIDTH)],
        out_shape=[head_shape] * 6 + [tok_shape(3 * HY_WIDTH), tok_shape(LRU_WIDTH), tok_shape(LRU_WIDTH)],
        compiler_params=_cparams("parallel", "parallel"),
        name="input_projection",
    )(xa, modtab, *weights, cos_t, sin_t)


def _softmax_pv(parts):
    m = None
    for s, _ in parts:
        mm = jnp.max(s, axis=-1, keepdims=True)
        m = mm if m is None else jnp.maximum(m, mm)
    acc = None
    den = None
    for s, v in parts:
        p = jnp.exp(s - m)
        l = jnp.sum(p, axis=-1, keepdims=True)
        o = _dot(p.astype(BF16), v)
        acc = o if acc is None else acc + o
        den = l if den is None else den + l
    return acc / den


def _mla_kernel(q_ref, k_ref, v_ref, o_ref, *, seq, n_lat_tiles):
    i = pl.program_id(1)
    nt = k_ref.shape[2]

    def attend(lo, hi):
        for hd in range(MLA_HEADS):
            s = _dot_nt(q_ref[0, hd], k_ref[0, hd, lo:hi, :])
            o = _softmax_pv([(s, v_ref[0, hd, lo:hi, :])])
            o_ref[0, :, hd * LANE:(hd + 1) * LANE] = o.astype(BF16)

    @pl.when(i < n_lat_tiles)
    def _():
        attend(0, nt)

    @pl.when(i >= n_lat_tiles)
    def _():
        attend(seq, nt)


def _mla_attention(q, k, v, seq):
    b, h, nt, _ = q.shape
    kv_spec = pl.BlockSpec((1, h, nt, LANE), lambda bb, i: (bb, 0, 0, 0))
    return pl.pallas_call(
        functools.partial(_mla_kernel, seq=seq, n_lat_tiles=seq // TM),
        grid=(b, nt // TM),
        in_specs=[pl.BlockSpec((1, h, TM, LANE), lambda bb, i: (bb, 0, i, 0)), kv_spec, kv_spec],
        out_specs=pl.BlockSpec((1, TM, h * LANE), lambda bb, i: (bb, i, 0)),
        out_shape=jax.ShapeDtypeStruct((b, nt, h * LANE), BF16),
        compiler_params=_cparams("parallel", "arbitrary"),
        name="mla_attention",
    )(q, k, v)


def _na_bias_tables(rpb, rows):
    n_blk = rows // NA_TILE_ROWS
    col = np.arange(GRID_W)
    c0 = np.clip(col - NA_WIN_C // 2, 0, GRID_W - NA_WIN_C)
    in_win = (col[None, :] >= c0[:, None]) & (col[None, :] < c0[:, None] + NA_WIN_C)
    dc = np.clip(col[None, :] - col[:, None], 1 - NA_WIN_C, NA_WIN_C - 1) + NA_WIN_C - 1
    rpb = rpb.astype(F32)
    tables = []
    for j in (0, 1, n_blk - 1):
        w0 = min(max(NA_TILE_ROWS * j - NA_WIN_R // 2, 0), rows - NA_KEY_ROWS)
        r = NA_TILE_ROWS * j + np.arange(NA_TILE_ROWS)
        kr = w0 + np.arange(NA_KEY_ROWS)
        r0 = np.clip(r - NA_WIN_R // 2, 0, rows - NA_WIN_R)
        row_ok = (kr[None, :] >= r0[:, None]) & (kr[None, :] < r0[:, None] + NA_WIN_R)
        dr = np.clip(kr[None, :] - r[:, None] + NA_WIN_R - 1, 0, 2 * NA_WIN_R - 2)
        oh_r = jnp.asarray(np.eye(2 * NA_WIN_R - 1, dtype=np.float32)[dr.reshape(-1)])
        oh_c = jnp.asarray(np.eye(2 * NA_WIN_C - 1, dtype=np.float32)[dc.reshape(-1)])
        bias = jnp.einsum('ar,hrc,bc->hab', oh_r, rpb, oh_c, precision=lax.Precision.HIGHEST)
        bias = bias.reshape(NA_HEADS, NA_TILE_ROWS, NA_KEY_ROWS, GRID_W, GRID_W)
        mask = row_ok[:, :, None, None] & in_win[None, None, :, :]
        bias = jnp.where(jnp.asarray(mask)[None], bias, -jnp.inf)
        tables.append(bias.transpose(0, 1, 3, 2, 4).reshape(NA_HEADS, TM, NA_KEYS))
    return jnp.stack(tables)


def _na_kernel(q_ref, k_ref, v_ref, bias_ref, o_ref, *, seq, n_lat_tiles):
    i = pl.program_id(1)
    nt = k_ref.shape[2]
    rows = seq // GRID_W

    @pl.when(i < n_lat_tiles)
    def _():
        w0 = jnp.clip(NA_TILE_ROWS * i - NA_WIN_R // 2, 0, rows - NA_KEY_ROWS)
        start = pl.multiple_of(w0 * GRID_W, GRID_W)
        for hd in range(NA_HEADS):
            q = q_ref[0, hd]
            s_loc = _dot_nt(q, k_ref[0, hd, pl.ds(start, NA_KEYS), :]) + bias_ref[0, hd]
            s_ctx = _dot_nt(q, k_ref[0, hd, seq:nt, :])
            o = _softmax_pv([(s_loc, v_ref[0, hd, pl.ds(start, NA_KEYS), :]), (s_ctx, v_ref[0, hd, seq:nt, :])])
            o_ref[0, :, hd * LANE:(hd + 1) * LANE] = o.astype(BF16)

    @pl.when(i >= n_lat_tiles)
    def _():
        for hd in range(NA_HEADS):
            s = _dot_nt(q_ref[0, hd], k_ref[0, hd, seq:nt, :])
            o = _softmax_pv([(s, v_ref[0, hd, seq:nt, :])])
            o_ref[0, :, hd * LANE:(hd + 1) * LANE] = o.astype(BF16)


def _na_attention(q, k, v, bias, seq):
    b, h, nt, _ = q.shape
    n_lat = seq // TM
    kv_spec = pl.BlockSpec((1, h, nt, LANE), lambda bb, i: (bb, 0, 0, 0))
    cfg = lambda bb, i: (jnp.where(i == 0, 0, jnp.where(i >= n_lat - 1, 2, 1)), 0, 0, 0)
    return pl.pallas_call(
        functools.partial(_na_kernel, seq=seq, n_lat_tiles=n_lat),
        grid=(b, nt // TM),
        in_specs=[pl.BlockSpec((1, h, TM, LANE), lambda bb, i: (bb, 0, i, 0)), kv_spec, kv_spec,
                  pl.BlockSpec((1, h, TM, NA_KEYS), cfg)],
        out_specs=pl.BlockSpec((1, TM, h * LANE), lambda bb, i: (bb, i, 0)),
        out_shape=jax.ShapeDtypeStruct((b, nt, h * LANE), BF16),
        compiler_params=_cparams("parallel", "arbitrary"),
        name="neighbourhood_attention",
    )(q, k, v, bias)


HY_PRE_CHUNK = 256


def _hy_pre_kernel(z_ref, w_ref, b_ref, lat_o, lat_bf_o, ctx_o, ctx_bf_o, *, seq, ctx_len):
    w = w_ref[...]
    bias = b_ref[...]
    ch = HY_PRE_CHUNK
    zero = jnp.zeros((8, HY_WIDTH), F32)
    for seg_lo, seg_len, o_ref, obf_ref in ((0, seq, lat_o, lat_bf_o), (seq, ctx_len, ctx_o, ctx_bf_o)):
        for c in range(seg_len // ch):
            s = seg_lo + c * ch
            before = zero if c == 0 else z_ref[0, s - 8:s, :]
            after = zero if c == seg_len // ch - 1 else z_ref[0, s + ch:s + ch + 8, :]
            win = jnp.concatenate([before, z_ref[0, s:s + ch, :], after], axis=0)
            n = ch + 16
            y = bias + w[1:2] * win[8:8 + ch]
            y = y + w[0:1] * pltpu.roll(win, 1, axis=0)[8:8 + ch]
            y = y + w[2:3] * pltpu.roll(win, n - 1, axis=0)[8:8 + ch]
            o_ref[0, c * ch:(c + 1) * ch, :] = y
            obf_ref[0, c * ch:(c + 1) * ch, :] = y.astype(BF16)


def _hy_pre(hy, w, bvec, seq):
    b, nt, _ = hy.shape
    ctx_len = nt - seq
    wpad = _pad_to(w, 8, 0)
    out_specs, out_shape = [], []
    for length in (seq, ctx_len):
        for dt in (F32, BF16):
            out_specs.append(pl.BlockSpec((1, length, HY_WIDTH), lambda bb, g: (g, 0, bb)))
            out_shape.append(jax.ShapeDtypeStruct((3, length, b * HY_WIDTH), dt))
    return pl.pallas_call(
        functools.partial(_hy_pre_kernel, seq=seq, ctx_len=ctx_len),
        grid=(b, 3),
        in_specs=[pl.BlockSpec((1, nt, HY_WIDTH), lambda bb, g: (bb, 0, g)),
                  pl.BlockSpec((8, HY_WIDTH), lambda bb, g: (0, g)),
                  pl.BlockSpec((1, HY_WIDTH), lambda bb, g: (0, g))],
        out_specs=out_specs,
        out_shape=out_shape,
        compiler_params=_cparams("parallel", "parallel"),
        name="hyena_short_conv",
    )(hy, wpad, bvec.reshape(1, -1))


def _hp_dot(a, b):
    return jnp.dot(a, b, preferred_element_type=F32, precision=lax.Precision.HIGHEST)


def _hy_filter_kernel(f_ref, w1_ref, b1_ref, w2_ref, b2_ref, w3_ref, dec_ref, o_ref, obf_ref):
    f = f_ref[...]
    h = jnp.sin(_hp_dot(f, w1_ref[...]) + b1_ref[...])
    h = jnp.sin(_hp_dot(h, w2_ref[...]) + b2_ref[...])
    h = _hp_dot(h, w3_ref[...])
    h = h * jnp.exp(-f[:, 0:1] * jnp.abs(dec_ref[...]))
    o_ref[...] = h
    obf_ref[...] = h.astype(BF16)


def _hy_pos_features(length):
    t = jnp.linspace(0.0, 1.0, length, dtype=F32)[:, None]
    w = 2.0 * math.pi * jnp.arange(length, dtype=F32)[:, None] / length
    f = jnp.linspace(1e-4, HY_BANDS - 1, HY_BANDS, dtype=F32)[None, :]
    z = w * f
    return jnp.concatenate([t, jnp.cos(z), -jnp.sin(z)], axis=-1)


def _hy_filters(length, lp):
    feats = _hy_pos_features(length)
    n_out = HY_ORDER * 2 * HY_WIDTH
    tl = min(length, 512)
    full = lambda a: pl.BlockSpec(a.shape, lambda i: (0,) * a.ndim)
    args = (lp['hy_w1'], lp['hy_b1'].reshape(1, -1), lp['hy_w2'], lp['hy_b2'].reshape(1, -1), lp['hy_w3'],
            lp['hy_decay'].reshape(1, n_out))
    return pl.pallas_call(
        _hy_filter_kernel,
        grid=(length // tl,),
        in_specs=[pl.BlockSpec((tl, HY_EMB), lambda i: (i, 0))] + [full(a) for a in args],
        out_specs=[pl.BlockSpec((tl, n_out), lambda i: (i, 0))] * 2,
        out_shape=[jax.ShapeDtypeStruct((length, n_out), F32), jax.ShapeDtypeStruct((length, n_out), BF16)],
        compiler_params=_cparams("parallel"),
        name="hyena_filter_mlp",
    )(feats, *args)


def _dft_matrices(length):
    n2 = 2 * length
    step = 64
    n = jnp.arange(length, dtype=I32)

    def trig(kv):
        ang = ((kv[:, None] * n[None, :]) % n2).astype(F32) * (2.0 * math.pi / n2)
        return jnp.cos(ang), jnp.sin(ang)

    c1, s1 = trig(jnp.arange(length // step, dtype=I32) * step)
    c2, s2 = trig(jnp.arange(step, dtype=I32))
    cos = (c1[:, None, :] * c2[None] - s1[:, None, :] * s2[None]).reshape(length, length)
    sin = (s1[:, None, :] * c2[None] + c1[:, None, :] * s2[None]).reshape(length, length)
    nyq = jnp.where(n % 2 == 0, 1.0, -1.0).astype(F32)
    ms = (-sin).at[0, :].set(nyq)
    mst = (-sin).at[:, 0].set(nyq)
    return cos.astype(BF16), ms.astype(BF16), mst.astype(BF16)


def _dft_fwd_kernel(mc_ref, ms_ref, x_ref, *rest, with_taps):
    ure = _dot(mc_ref[...], x_ref[...])
    uim = _dot(ms_ref[...], x_ref[...])
    if not with_taps:
        ure_o, uim_o = rest
        ure_o[...] = ure
        uim_o[...] = uim
        return
    a_ref, b_ref, c_ref, d_ref, zre_o, zim_o = rest
    a, b, c, d = a_ref[...], b_ref[...], c_ref[...], d_ref[...]
    for bb in range(x_ref.shape[1] // HY_WIDTH):
        sl = slice(bb * HY_WIDTH, (bb + 1) * HY_WIDTH)
        zre_o[:, sl] = (ure[:, sl] * a - uim[:, sl] * b).astype(BF16)
        zim_o[:, sl] = (ure[:, sl] * c + uim[:, sl] * d).astype(BF16)


def _col_block(nc, cap):
    return min(nc, cap)


def _dft_fwd(mats, x, plane, taps=None):
    mc, ms, _ = mats
    _, length, nc = x.shape
    tk = min(length, 256)
    cb = _col_block(nc, 1024)
    grid = (nc // cb, length // tk)
    m_spec = pl.BlockSpec((tk, length), lambda c, j: (j, 0))
    x_spec = pl.BlockSpec((None, length, cb), lambda c, j: (plane, 0, c))
    o_spec = pl.BlockSpec((tk, cb), lambda c, j: (j, c))
    if taps is None:
        return pl.pallas_call(
            functools.partial(_dft_fwd_kernel, with_taps=False),
            grid=grid, in_specs=[m_spec, m_spec, x_spec], out_specs=[o_spec, o_spec],
            out_shape=[jax.ShapeDtypeStruct((length, nc), F32)] * 2,
            compiler_params=_cparams("parallel", "arbitrary"),
            name="hyena_dft_filters",
        )(mc, ms, x)
    t_spec = pl.BlockSpec((tk, HY_WIDTH), lambda c, j: (j, 0))
    return pl.pallas_call(
        functools.partial(_dft_fwd_kernel, with_taps=True),
        grid=grid, in_specs=[m_spec, m_spec, x_spec] + [t_spec] * 4, out_specs=[o_spec, o_spec],
        out_shape=[jax.ShapeDtypeStruct((length, nc), BF16)] * 2,
        compiler_params=_cparams("parallel", "arbitrary"),
        name="hyena_dft_forward",
    )(mc, ms, x, *taps)


def _dft_inv_kernel(mc_ref, mst_ref, zre_ref, zim_ref, gate_ref, prev_ref, bias_ref, *outs, last):
    conv = _dot(mc_ref[...], zre_ref[...]) + _dot(mst_ref[...], zim_ref[...])
    y = gate_ref[...] * (conv + prev_ref[...] * bias_ref[...])
    if last:
        (tok_o,) = outs
        for bb in range(y.shape[1] // HY_WIDTH):
            tok_o[bb] = y[:, bb * HY_WIDTH:(bb + 1) * HY_WIDTH].astype(BF16)
    else:
        y_o, ybf_o = outs
        y_o[...] = y
        ybf_o[...] = y.astype(BF16)


def _dft_inv(mats, zre, zim, gate, gate_plane, prev, prev_plane, bias_row, last):
    mc, _, mst = mats
    length, nc = zre.shape
    tm = min(length, 256)
    cb = _col_block(nc, 512)
    grid = (nc // cb, length // tm)
    m_spec = pl.BlockSpec((tm, length), lambda c, i: (i, 0))
    z_spec = pl.BlockSpec((length, cb), lambda c, i: (0, c))
    e_spec = lambda plane: pl.BlockSpec((None, tm, cb), lambda c, i: (plane, i, c))
    b_spec = pl.BlockSpec((1, cb), lambda c, i: (0, c))
    if last:
        nb = nc // HY_WIDTH
        out_specs = [pl.BlockSpec((cb // HY_WIDTH, tm, HY_WIDTH), lambda c, i: (c, i, 0))]
        out_shape = [jax.ShapeDtypeStruct((nb, length, HY_WIDTH), BF16)]
    else:
        out_specs = [e_spec(0), e_spec(0)]
        out_shape = [jax.ShapeDtypeStruct((1, length, nc), F32), jax.ShapeDtypeStruct((1, length, nc), BF16)]
    return pl.pallas_call(
        functools.partial(_dft_inv_kernel, last=last),
        grid=grid, in_specs=[m_spec, m_spec, z_spec, z_spec, e_spec(gate_plane), e_spec(prev_plane), b_spec],
        out_specs=out_specs, out_shape=out_shape,
        compiler_params=_cparams("parallel", "arbitrary"),
        name="hyena_dft_inverse",
    )(mc, mst, zre, zim, gate, prev, bias_row)


def _hy_tap_tables(ure, uim, filt, length):
    n2 = 2 * length
    w = HY_WIDTH
    scale = jnp.full((length, 1), 2.0 / n2, F32).at[0, 0].set(1.0 / n2)
    tables = []
    for o in range(HY_ORDER):
        f_sl = slice((2 * o) * w, (2 * o + 1) * w)
        b_sl = slice((2 * o + 1) * w, (2 * o + 2) * w)
        hb0 = filt[0:1, b_sl]
        tre = ure[:, f_sl] + ure[:, b_sl] - hb0
        tim = uim[:, f_sl] - uim[:, b_sl]
        t_nyq = uim[0:1, f_sl] + uim[0:1, b_sl] - hb0
        first = (jnp.arange(length) == 0)[:, None]
        a = tre * scale
        bm = jnp.where(first, 0.0, tim * scale)
        c = bm
        d = jnp.where(first, t_nyq * scale, tre * scale)
        tables.append((a, bm, c, d))
    return tables


def _hyena_seq(mats, vx, vx_bf, lp, n_batch):
    length = vx.shape[1]
    filt, filt_bf = _hy_filters(length, lp)
    ure, uim = _dft_fwd(mats, filt_bf[None], 0)
    tables = _hy_tap_tables(ure, uim, filt, length)
    bias = lp['hy_bias'].astype(F32)
    y, y_bf = vx, vx_bf
    for o in range(HY_ORDER):
        zre, zim = _dft_fwd(mats, y_bf, 0, tables[o])
        bias_row = jnp.tile(bias[o][None, :], (1, n_batch))
        last = o == HY_ORDER - 1
        res = _dft_inv(mats, zre, zim, vx, o + 1, y, 0, bias_row, last)
        if last:
            return res[0]
        y, y_bf = res


def _lru_kernel(u_ref, g_ref, cw_ref, cb_ref, wa_ref, ba_ref, wx_ref, bx_ref, lam_ref, o_ref,
                pad_ref, y_ref, *, seq, ctx_len):
    tc = LRU_CHUNK
    halo = LRU_HALO
    width = LRU_WIDTH
    lat_off = halo
    ctx_off = 2 * halo + seq
    zero = jnp.zeros((halo, width), F32)
    pad_ref[0:halo, :] = zero
    pad_ref[lat_off:lat_off + seq, :] = u_ref[0, 0:seq, :]
    pad_ref[lat_off + seq:ctx_off, :] = zero
    pad_ref[ctx_off:ctx_off + ctx_len, :] = u_ref[0, seq:seq + ctx_len, :]
    pad_ref[ctx_off + ctx_len:ctx_off + ctx_len + halo, :] = zero
    row = lax.broadcasted_iota(I32, (tc, width), 0)
    n_win = tc + 2 * halo

    def chunk(pad_off, y_off, s, carry, d):
        wstart = pl.multiple_of(pad_off + s - halo, 8)
        win = pad_ref[pl.ds(wstart, n_win), :]
        cw = cw_ref[d]
        xc = cb_ref[d]
        for k in range(LRU_CONV):
            shift = (LRU_CONV - 1 - k) if d == 0 else -k
            rolled = win if shift == 0 else pltpu.roll(win, shift % n_win, axis=0)
            xc = xc + cw[k:k + 1] * rolled[halo:halo + tc]
        xb = xc.astype(BF16)
        r = _sigmoid(_dot(xb, wa_ref[d]) + ba_ref[d])
        gi = _sigmoid(_dot(xb, wx_ref[d]) + bx_ref[d])
        lam = lam_ref[d]
        softplus = jnp.maximum(-lam, 0.0) + jnp.log1p(jnp.exp(-jnp.abs(lam)))
        log_a = -LRU_C * r * softplus
        a = jnp.exp(log_a)
        bt = jnp.sqrt(-jnp.tanh(log_a) * (a * a + 1.0)) * (gi * xc)
        sft = 1
        while sft < tc:
            if d == 0:
                keep = row >= sft
                a_s = jnp.where(keep, pltpu.roll(a, sft, axis=0), 1.0)
                b_s = jnp.where(keep, pltpu.roll(bt, sft, axis=0), 0.0)
            else:
                keep = row < tc - sft
                a_s = jnp.where(keep, pltpu.roll(a, tc - sft, axis=0), 1.0)
                b_s = jnp.where(keep, pltpu.roll(bt, tc - sft, axis=0), 0.0)
            bt = a * b_s + bt
            a = a * a_s
            sft *= 2
        h = a * carry + bt
        yo = pl.multiple_of(y_off + s, 8)
        if d == 0:
            y_ref[pl.ds(yo, tc), :] = h
            return h[tc - 1:tc]
        y_ref[pl.ds(yo, tc), :] = y_ref[pl.ds(yo, tc), :] + h
        return h[0:1]

    n_lat = seq // tc
    n_ctx = ctx_len // tc
    for d in range(2):
        carry = jnp.zeros((1, width), F32)
        order = range(n_ctx) if d == 0 else range(n_ctx - 1, -1, -1)
        for c in order:
            carry = chunk(ctx_off, seq, c * tc, carry, d)

        def body(j, cr, d=d):
            jj = j if d == 0 else n_lat - 1 - j
            return chunk(lat_off, 0, jj * tc, cr, d)

        lax.fori_loop(0, n_lat, body, carry)
    o_ref[0] = (y_ref[...] * _gelu_tanh(g_ref[0])).astype(BF16)


def _block_diag(w):
    nd, nb, c, _ = w.shape
    out = jnp.zeros((nd, nb * c, nb * c), w.dtype)
    for n in range(nb):
        out = out.at[:, n * c:(n + 1) * c, n * c:(n + 1) * c].set(w[:, n])
    return out


def _lru_mixer(lu, lg, lp, seq):
    b, nt, w = lu.shape
    ctx_len = nt - seq
    row3 = lambda a: a.reshape(2, 1, w)
    args = (_pad_to(lp['lru_conv_w'], 8, 1), row3(lp['lru_conv_b']), _block_diag(lp['lru_wa']).astype(BF16),
            row3(lp['lru_ba']), _block_diag(lp['lru_wx']).astype(BF16), row3(lp['lru_bx']), row3(lp['lru_lambda']))
    full = lambda a: pl.BlockSpec(a.shape, lambda bb: (0,) * a.ndim)
    tok = pl.BlockSpec((1, nt, w), lambda bb: (bb, 0, 0))
    return pl.pallas_call(
        functools.partial(_lru_kernel, seq=seq, ctx_len=ctx_len),
        grid=(b,),
        in_specs=[tok, tok] + [full(a) for a in args],
        out_specs=tok,
        out_shape=jax.ShapeDtypeStruct((b, nt, w), BF16),
        scratch_shapes=[pltpu.VMEM((nt + 3 * LRU_HALO, w), F32), pltpu.VMEM((nt, w), F32)],
        compiler_params=_cparams("parallel"),
        name="rglru_scan",
    )(lu, lg, *args)


def _merge_kernel(x_ref, mod_ref, g1_ref, a_ref, b_ref, c_ref, d_ref, wg_ref, wa_ref, wb_ref, wc_ref, wd_ref,
                  wo_ref, o_ref):
    m = mod_ref[0, 0]
    x = x_ref[0]
    h = _normmod(x, g1_ref[...], m[1:2], m[0:1]).astype(BF16)
    dm = x.shape[-1]
    acc = None
    for k, (br, w) in enumerate(((a_ref, wa_ref), (b_ref, wb_ref), (c_ref, wc_ref), (d_ref, wd_ref))):
        gate = _sigmoid(_dot(h, wg_ref[:, k * dm:(k + 1) * dm]))
        term = gate * _dot(br[0], w[...])
        acc = term if acc is None else acc + term
    y = _dot(acc.astype(BF16), wo_ref[...])
    o_ref[0] = x + m[2:3] * y


def _merge(xa, modtab, g1, branches, w_gate, lp, n_lat_tiles):
    b, nt, d = xa.shape
    wbr = lp['w_branch']
    head_rows = lambda w, dv: jnp.concatenate(
        [_pad_to(w[hd * dv:(hd + 1) * dv], LANE, 0) for hd in range(4)], axis=0)
    weights = (head_rows(wbr[0], MLA_V).astype(BF16), head_rows(wbr[1], NA_HEAD_DIM).astype(BF16),
               wbr[2].astype(BF16), wbr[3].astype(BF16), lp['w_out'].astype(BF16))
    full = lambda a: pl.BlockSpec(a.shape, lambda bb, i: (0,) * a.ndim)
    tok = lambda n: pl.BlockSpec((1, TM, n), lambda bb, i: (bb, i, 0))
    return pl.pallas_call(
        _merge_kernel,
        grid=(b, nt // TM),
        in_specs=[tok(d), pl.BlockSpec((1, 1, 8, d), _kind_map(n_lat_tiles)), full(g1)]
                 + [tok(br.shape[-1]) for br in branches] + [full(w_gate)] + [full(w) for w in weights],
        out_specs=tok(d),
        out_shape=jax.ShapeDtypeStruct((b, nt, d), F32),
        compiler_params=_cparams("parallel", "parallel"),
        name="merge_branches",
    )(xa, modtab, g1, *branches, w_gate, *weights)


U32 = jnp.uint32
EXPERT_ROWS = 512


def _pack_pair(x):
    n = x.shape[-1] // 2
    hi = lax.bitcast_convert_type(x[:, :n].astype(BF16).astype(F32), U32)
    lo = lax.bitcast_convert_type(x[:, n:].astype(BF16).astype(F32), U32)
    return hi | (lo >> 16)


def _unpack_pair(p):
    hi = lax.bitcast_convert_type(p & jnp.uint32(0xFFFF0000), F32)
    lo = lax.bitcast_convert_type(p << 16, F32)
    return hi, lo


def _router_kernel(x_ref, mod_ref, g2_ref, rw_ref, rb_ref, tri_ref, h2_o, idx_o, wts_o, rank_o, cnt_o, carry):
    i = pl.program_id(0)

    @pl.when(i == 0)
    def _():
        carry[...] = jnp.zeros_like(carry)

    m = mod_ref[0, 0]
    h2 = _normmod(x_ref[0], g2_ref[...], m[4:5], m[3:4])
    half = h2.shape[-1] // 2
    h2_o[0] = _pack_pair(h2[:, :half])
    h2_o[1] = _pack_pair(h2[:, half:])
    scores = _sigmoid(_hp_dot(h2, rw_ref[...]))
    lane = lax.broadcasted_iota(I32, scores.shape, 1)
    biased = jnp.where(lane < N_EXPERTS, scores + rb_ref[...], -jnp.inf)
    picks = []
    onehot_all = jnp.zeros(scores.shape, F32)
    for _ in range(TOP_K):
        best = jnp.max(biased, axis=-1, keepdims=True)
        arg = jnp.min(jnp.where(biased == best, lane, LANE), axis=-1, keepdims=True)
        hit = lane == arg
        sel = jnp.sum(jnp.where(hit, scores, 0.0), axis=-1, keepdims=True)
        biased = jnp.where(hit, -jnp.inf, biased)
        onehot_all = onehot_all + jnp.where(hit, 1.0, 0.0)
        picks.append((arg, hit, sel))
    total = picks[0][2]
    for _, _, sel in picks[1:]:
        total = total + sel
    earlier = _dot(tri_ref[...], onehot_all.astype(BF16)) + carry[...]
    slot = lax.broadcasted_iota(I32, (scores.shape[0], TOPK_PAD), 1)
    idx = jnp.zeros(slot.shape, I32)
    wts = jnp.zeros(slot.shape, F32)
    rank = jnp.zeros(slot.shape, I32)
    for k, (arg, hit, sel) in enumerate(picks):
        rk = jnp.sum(jnp.where(hit, earlier, 0.0), axis=-1, keepdims=True)
        idx = jnp.where(slot == k, arg, idx)
        wts = jnp.where(slot == k, sel / total * ROUTED_SCALE, wts)
        rank = jnp.where(slot == k, rk.astype(I32), rank)
    idx_o[...] = idx
    wts_o[...] = wts
    rank_o[...] = rank
    carry[...] = carry[...] + jnp.sum(onehot_all, axis=0, keepdims=True)
    cnt_o[...] = carry[...]


def _tile_maps(tiles_used, n_lat_tiles):
    tok = lambda i: (i // tiles_used, i % tiles_used, 0)
    mod = lambda i: (i // tiles_used, jnp.where(i % tiles_used >= n_lat_tiles, 1, 0), 0, 0)
    return tok, mod


def _route(x1, modtab, g2, lp, n_lat_tiles, tiles_used):
    b, _, d = x1.shape
    n_tiles = b * tiles_used
    t = n_tiles * TM
    rw = _pad_to(lp['router_w'], LANE, 1)
    rb = _pad_to(lp['router_bias'].reshape(1, -1), LANE, 1)
    tri = (np.arange(TM)[:, None] > np.arange(TM)[None, :]).astype(np.float32)
    tri = jnp.asarray(tri, BF16)
    flat = lambda n: pl.BlockSpec((TM, n), lambda i: (i, 0))
    full = lambda a: pl.BlockSpec(a.shape, lambda i: (0,) * a.ndim)
    tok, mod = _tile_maps(tiles_used, n_lat_tiles)
    return pl.pallas_call(
        _router_kernel,
        grid=(n_tiles,),
        in_specs=[pl.BlockSpec((1, TM, d), tok), pl.BlockSpec((1, 1, 8, d), mod),
                  full(g2), full(rw), full(rb), full(tri)],
        out_specs=[pl.BlockSpec((2, TM, d // 4), lambda i: (0, i, 0)), flat(TOPK_PAD), flat(TOPK_PAD), flat(TOPK_PAD),
                   pl.BlockSpec((1, LANE), lambda i: (0, 0))],
        out_shape=[jax.ShapeDtypeStruct((2, t, d // 4), U32), jax.ShapeDtypeStruct((t, TOPK_PAD), I32),
                   jax.ShapeDtypeStruct((t, TOPK_PAD), F32), jax.ShapeDtypeStruct((t, TOPK_PAD), I32),
                   jax.ShapeDtypeStruct((1, LANE), F32)],
        scratch_shapes=[pltpu.VMEM((1, LANE), F32)],
        compiler_params=_cparams("arbitrary"),
        name="moe_router",
    )(x1, modtab, g2, rw, rb, tri)


SC_WINDOW = 128


def _sc_mesh():
    return plsc.VectorSubcoreMesh(core_axis_name="c", subcore_axis_name="s")


def _sc_scatter_rows(src, idx, n_out):
    n, width = src.shape
    k_rep = idx.shape[0]
    half = n // SC_WINDOW // 2

    @functools.partial(pl.kernel, out_type=jax.ShapeDtypeStruct((n_out, width), src.dtype), mesh=_sc_mesh(),
                       scratch_types=[], name="moe_dispatch_sc")
    def scatter(src_hbm, idx_hbm, out_hbm):
        def body(x_vmem, *i_vmems):
            for i_vmem in i_vmems:
                pltpu.sync_copy(x_vmem, out_hbm.at[i_vmem.at[0]])

        pltpu.emit_pipeline(
            body,
            grid=(2, half),
            in_specs=[pl.BlockSpec((SC_WINDOW, width), lambda a, i: (a * half + i, 0))]
                     + [pl.BlockSpec((1, SC_WINDOW), lambda a, i, k=k: (k, a * half + i)) for k in range(k_rep)],
            out_specs=[],
            core_axis_name=("c", "s"),
            dimension_semantics=(pltpu.PARALLEL, pltpu.PARALLEL),
        )(src_hbm, *([idx_hbm] * k_rep))

    return scatter(src, idx)


def _sc_gather_rows(src, idx):
    k_rep, n = idx.shape
    width = src.shape[1]
    n_win = n // SC_WINDOW

    @functools.partial(pl.kernel, out_type=jax.ShapeDtypeStruct((k_rep * n, width), src.dtype), mesh=_sc_mesh(),
                       scratch_types=[], name="moe_gather_sc")
    def gather(src_hbm, idx_hbm, out_hbm):
        def body(i_vmem, o_vmem):
            pltpu.sync_copy(src_hbm.at[i_vmem.at[0]], o_vmem)

        pltpu.emit_pipeline(
            body,
            grid=(k_rep, n_win),
            in_specs=[pl.BlockSpec((1, SC_WINDOW), lambda k, i: (k, i))],
            out_specs=[pl.BlockSpec((SC_WINDOW, width), lambda k, i: (k * n_win + i, 0))],
            core_axis_name=("c", "s"),
            dimension_semantics=(pltpu.PARALLEL, pltpu.PARALLEL),
        )(idx_hbm, out_hbm)

    return gather(src, idx)


def _unpack_planes(p0, p1):
    return _unpack_pair(p0) + _unpack_pair(p1)


def _dot_quarters(parts, w_ref):
    q = parts[0].shape[-1]
    acc = None
    for j, part in enumerate(parts):
        term = _dot(part.astype(BF16), w_ref[j * q:(j + 1) * q, :])
        acc = term if acc is None else acc + term
    return acc


def _expert_kernel(be_ref, nv_ref, xs_ref, wg_ref, wu_ref, wd_ref, ys_o, wg_s, wu_s, wd_s):
    i = pl.program_id(0)
    prev = be_ref[jnp.maximum(i - 1, 0)]

    @pl.when((i == 0) | (be_ref[i] != prev))
    def _():
        wg_s[...] = wg_ref[0].astype(BF16)
        wu_s[...] = wu_ref[0].astype(BF16)
        wd_s[...] = wd_ref[0].astype(BF16)

    @pl.when(nv_ref[i] > 0)
    def _():
        keep = lax.broadcasted_iota(I32, xs_ref.shape[1:], 0) < nv_ref[i]
        parts = _unpack_planes(jnp.where(keep, xs_ref[0], jnp.uint32(0)), jnp.where(keep, xs_ref[1], jnp.uint32(0)))
        hid = _silu(_dot_quarters(parts, wg_s)) * _dot_quarters(parts, wu_s)
        y = _dot(hid.astype(BF16), wd_s[...])
        half = y.shape[-1] // 2
        ys_o[0] = _pack_pair(y[:, :half])
        ys_o[1] = _pack_pair(y[:, half:])

    @pl.when(nv_ref[i] <= 0)
    def _():
        ys_o[...] = jnp.zeros_like(ys_o)


def _experts(xs, block_e, n_valid, lp):
    _, n_rows, dq = xs.shape
    d = 4 * dq
    n_blocks = n_rows // EXPERT_ROWS
    hid = EXPERT_HIDDEN
    grid_spec = pltpu.PrefetchScalarGridSpec(
        num_scalar_prefetch=2,
        grid=(n_blocks,),
        in_specs=[pl.BlockSpec((2, EXPERT_ROWS, dq), lambda i, be, nv: (0, i, 0)),
                  pl.BlockSpec((1, d, hid), lambda i, be, nv: (be[i], 0, 0)),
                  pl.BlockSpec((1, d, hid), lambda i, be, nv: (be[i], 0, 0)),
                  pl.BlockSpec((1, hid, d), lambda i, be, nv: (be[i], 0, 0))],
        out_specs=pl.BlockSpec((2, EXPERT_ROWS, dq), lambda i, be, nv: (0, i, 0)),
        scratch_shapes=[pltpu.VMEM((d, hid), BF16), pltpu.VMEM((d, hid), BF16), pltpu.VMEM((hid, d), BF16)],
    )
    return pl.pallas_call(
        _expert_kernel,
        grid_spec=grid_spec,
        out_shape=jax.ShapeDtypeStruct((2, n_rows, dq), U32),
        compiler_params=_cparams("arbitrary"),
        name="moe_experts",
    )(block_e, n_valid, xs, lp['exp_w_gate'], lp['exp_w_up'], lp['exp_w_down'])


def _combine_kernel(g_ref, wts_ref, h2_ref, x_ref, mod_ref, sg_ref, su_ref, sd_ref, gf_ref, o_ref, *, final):
    parts = _unpack_planes(h2_ref[0], h2_ref[1])
    hid = _silu(_dot_quarters(parts, sg_ref)) * _dot_quarters(parts, su_ref)
    shared = _dot(hid.astype(BF16), sd_ref[...])
    wts = wts_ref[...]
    routed = None
    for k in range(TOP_K):
        w = wts[:, k:k + 1]
        terms = [w * part for part in _unpack_planes(g_ref[k, 0], g_ref[k, 1])]
        routed = terms if routed is None else [r + t for r, t in zip(routed, terms)]
    f = shared + jnp.concatenate(routed, axis=1)
    m = mod_ref[0, 0]
    x2 = x_ref[0] + m[5:6] * f
    if final:
        x2 = x2 * lax.rsqrt(jnp.mean(x2 * x2, axis=-1, keepdims=True) + NORM_EPS) * gf_ref[...]
    o_ref[0] = x2


def _combine(g, wts, h2p, x1, modtab, lp, g_final, n_lat_tiles, tiles_used, final):
    b, _, d = x1.shape
    dq = d // 4
    weights = (lp['sh_w_gate'].astype(BF16), lp['sh_w_up'].astype(BF16), lp['sh_w_down'].astype(BF16),
               g_final.reshape(1, -1))
    full = lambda a: pl.BlockSpec(a.shape, lambda i: (0,) * a.ndim)
    tok, mod = _tile_maps(tiles_used, n_lat_tiles)
    return pl.pallas_call(
        functools.partial(_combine_kernel, final=final),
        grid=(b * tiles_used,),
        in_specs=[pl.BlockSpec((TOP_K, 2, TM, dq), lambda i: (0, 0, i, 0)),
                  pl.BlockSpec((TM, TOPK_PAD), lambda i: (i, 0)),
                  pl.BlockSpec((2, TM, dq), lambda i: (0, i, 0)),
                  pl.BlockSpec((1, TM, d), tok), pl.BlockSpec((1, 1, 8, d), mod)]
                 + [full(w) for w in weights],
        out_specs=pl.BlockSpec((1, TM, d), tok),
        out_shape=jax.ShapeDtypeStruct((b, tiles_used * TM, d), F32),
        compiler_params=_cparams("parallel"),
        name="moe_combine",
    )(g, wts, h2p, x1, modtab, *weights)


def _moe(x1, modtab, g2, lp, g_final, n_lat_tiles, final):
    b, nt, d = x1.shape
    tiles_used = n_lat_tiles if final else nt // TM
    t = b * tiles_used * TM
    h2p, idx, wts, rank, cnt = _route(x1, modtab, g2, lp, n_lat_tiles, tiles_used)
    counts = cnt[0, :N_EXPERTS].astype(I32)
    padded = (counts + EXPERT_ROWS - 1) // EXPERT_ROWS * EXPERT_ROWS
    p_ends = jnp.cumsum(padded)
    p_starts = p_ends - padded
    n_blocks = (t * TOP_K + N_EXPERTS * (EXPERT_ROWS - 1)) // EXPERT_ROWS
    n_rows = n_blocks * EXPERT_ROWS
    dest = p_starts[idx[:, :TOP_K]] + rank[:, :TOP_K]
    plane_idx = (dest.T[:, None, :] + (jnp.arange(2, dtype=I32) * n_rows)[None, :, None]).reshape(TOP_K, 2 * t)
    blk_start = jnp.arange(n_blocks, dtype=I32) * EXPERT_ROWS
    block_e = jnp.minimum(jnp.sum((p_ends[None, :] <= blk_start[:, None]).astype(I32), axis=1), N_EXPERTS - 1)
    n_valid = jnp.clip((p_starts + counts)[block_e] - blk_start, 0, EXPERT_ROWS).astype(I32)
    dq = d // 4
    xs = _sc_scatter_rows(h2p.reshape(2 * t, dq), plane_idx, 2 * n_rows).reshape(2, n_rows, dq)
    ys = _experts(xs, block_e, n_valid, lp)
    g = _sc_gather_rows(ys.reshape(2 * n_rows, dq), plane_idx).reshape(TOP_K, 2, t, dq)
    return _combine(g, wts, h2p, x1, modtab, lp, g_final, n_lat_tiles, tiles_used, final)


def _layer(xa, c, c_ctx, lp, consts, g_final, seq, final):
    b, nt, d = xa.shape
    n_lat_tiles = seq // TM
    rope, mats_lat, mats_ctx = consts
    modtab = _mod_table(c, c_ctx, lp['w_mod'], lp['b_mod'])
    g1 = lp['g_norm1'].reshape(1, -1)
    g2 = lp['g_norm2'].reshape(1, -1)
    pw = _proj_weights(lp)
    q, k, v, nq, nk, nv, hy, lu, lg = _project(xa, modtab, g1, pw, rope, n_lat_tiles)
    br_a = _mla_attention(q, k, v, seq)
    br_b = _na_attention(nq, nk, nv, _na_bias_tables(lp['na_rpb'], seq // GRID_W), seq)
    pre = _hy_pre(hy, lp['hy_short_w'], lp['hy_short_b'], seq)
    lat_f, lat_bf, ctx_f, ctx_bf = pre
    c_lat = _hyena_seq(mats_lat, lat_f, lat_bf, lp, b)
    c_ctx_out = _hyena_seq(mats_ctx, ctx_f, ctx_bf, lp, b)
    br_c = jnp.concatenate([c_lat, c_ctx_out], axis=1)
    br_d = _lru_mixer(lu, lg, lp, seq)
    x1 = _merge(xa, modtab, g1, (br_a, br_b, br_c, br_d), pw['w_gate'], lp, n_lat_tiles)
    return _moe(x1, modtab, g2, lp, g_final, n_lat_tiles, final)


_LAYER_KEYS = ('w_mod', 'b_mod', 'g_norm1', 'g_norm2', 'w_in', 'mla_g_q', 'mla_w_uq', 'mla_g_kv', 'mla_w_ukv',
               'na_rpb', 'hy_short_w', 'hy_short_b', 'hy_w1', 'hy_b1', 'hy_w2', 'hy_b2', 'hy_w3', 'hy_decay',
               'hy_bias', 'lru_conv_w', 'lru_conv_b', 'lru_wa', 'lru_ba', 'lru_wx', 'lru_bx', 'lru_lambda',
               'w_branch', 'w_out', 'router_w', 'router_bias', 'exp_w_gate', 'exp_w_up', 'exp_w_down',
               'sh_w_gate', 'sh_w_up', 'sh_w_down')


def kernel(x, c, ctx, c_ctx, w_mod, b_mod, g_norm1, g_norm2, w_in, mla_g_q, mla_w_uq, mla_g_kv, mla_w_ukv, na_rpb, hy_short_w, hy_short_b, hy_w1, hy_b1, hy_w2, hy_b2, hy_w3, hy_decay, hy_bias, lru_conv_w, lru_conv_b, lru_wa, lru_ba, lru_wx, lru_bx, lru_lambda, w_branch, w_out, router_w, router_bias, exp_w_gate, exp_w_up, exp_w_down, sh_w_gate, sh_w_up, sh_w_down, g_final):
    stacked = dict(zip(_LAYER_KEYS, (w_mod, b_mod, g_norm1, g_norm2, w_in, mla_g_q, mla_w_uq, mla_g_kv, mla_w_ukv,
                                     na_rpb, hy_short_w, hy_short_b, hy_w1, hy_b1, hy_w2, hy_b2, hy_w3, hy_decay,
                                     hy_bias, lru_conv_w, lru_conv_b, lru_wa, lru_ba, lru_wx, lru_bx, lru_lambda,
                                     w_branch, w_out, router_w, router_bias, exp_w_gate, exp_w_up, exp_w_down,
                                     sh_w_gate, sh_w_up, sh_w_down)))
    b, seq, d = x.shape
    ctx_len = ctx.shape[1]
    depth = w_mod.shape[0]
    assert seq % TM == 0 and ctx_len % TM == 0 and seq // GRID_W >= NA_KEY_ROWS + 1
    xa = jnp.concatenate([x, ctx], axis=1)
    consts = (_rope_tables(seq, seq + ctx_len), _dft_matrices(seq), _dft_matrices(ctx_len))
    for i in range(depth):
        lp = {name: w[i] for name, w in stacked.items()}
        xa = _layer(xa, c, c_ctx, lp, consts, g_final, seq, i == depth - 1)
    return xa
```

```python
import functools
import math

import numpy as np
import jax
import jax.numpy as jnp
from jax import lax
from jax.experimental import pallas as pl
from jax.experimental.pallas import tpu as pltpu
from jax.experimental.pallas import tpu_sc as plsc

F32 = jnp.float32
BF16 = jnp.bfloat16
I32 = jnp.int32

TM = 256
LANE = 128
GRID_W = 64
N_MOD = 6
NORM_EPS = 1e-6

MLA_HEADS, MLA_NOPE, MLA_ROPE, MLA_V = 4, 64, 32, 64
MLA_Q_RANK, MLA_KV_RANK = 192, 128
MLA_Q_PAD = 256
ROPE_THETA = 10000.0

NA_HEADS, NA_HEAD_DIM, NA_WIN_R, NA_WIN_C = 4, 64, 8, 16
NA_TILE_ROWS = TM // GRID_W
NA_KEY_ROWS = NA_TILE_ROWS + NA_WIN_R - 1
NA_KEYS = NA_KEY_ROWS * GRID_W

HY_WIDTH, HY_ORDER, HY_SHORT, HY_BANDS, HY_FFN = 256, 2, 3, 16, 64
HY_EMB = 2 * HY_BANDS + 1

LRU_WIDTH, LRU_BLOCKS, LRU_CONV, LRU_C = 256, 4, 4, 8.0
LRU_CHUNK = 256
LRU_HALO = 8

N_EXPERTS, TOP_K, EXPERT_HIDDEN, ROUTED_SCALE, MOE_BLOCK = 64, 6, 256, 2.5, 256
TOPK_PAD = 8

VMEM_LIMIT = 52 * 1024 * 1024


def _cparams(*sem):
    return pltpu.CompilerParams(dimension_semantics=sem, vmem_limit_bytes=VMEM_LIMIT)


def _dot(a, b):
    return jnp.dot(a, b, preferred_element_type=F32)


def _dot_nt(a, b):
    return lax.dot_general(a, b, (((1,), (1,)), ((), ())), preferred_element_type=F32)


def _sigmoid(x):
    return 1.0 / (1.0 + jnp.exp(-x))


def _silu(x):
    return x * _sigmoid(x)


def _gelu_tanh(x):
    return 0.5 * x * (1.0 + jnp.tanh(math.sqrt(2.0 / math.pi) * (x + 0.044715 * (x * x * x))))


def _normmod(x, g, scale, shift):
    y = x * lax.rsqrt(jnp.mean(x * x, axis=-1, keepdims=True) + NORM_EPS) * g
    return y * (1.0 + scale) + shift


def _mod_kernel(c_ref, w_ref, b_ref, o_ref):
    s = _silu(c_ref[...])
    o_ref[...] = _dot(s.astype(BF16), w_ref[...].astype(BF16)) + b_ref[...]


def _mod_table(c, c_ctx, w_mod, b_mod, layer):
    b, d = c.shape
    rows = 16
    cc = jnp.zeros((rows, d), F32).at[:b].set(c).at[b].set(c_ctx)
    tn = 1024
    mod = pl.pallas_call(
        _mod_kernel,
        grid=(N_MOD * d // tn,),
        in_specs=[pl.BlockSpec((rows, d), lambda j: (0, 0)),
                  pl.BlockSpec((None, d, tn), lambda j: (layer, 0, j)),
                  pl.BlockSpec((None, 1, tn), lambda j: (layer, 0, j))],
        out_specs=pl.BlockSpec((rows, tn), lambda j: (0, j)),
        out_shape=jax.ShapeDtypeStruct((rows, N_MOD * d), F32),
        compiler_params=_cparams("arbitrary"),
        name="mod_vectors",
    )(cc, w_mod, b_mod[:, None, :])
    lat = mod[:b].reshape(b, N_MOD, d)
    ctx = jnp.broadcast_to(mod[b].reshape(1, N_MOD, d), (b, N_MOD, d))
    tab = jnp.stack([lat, ctx], axis=1)
    return jnp.pad(tab, ((0, 0), (0, 0), (0, 8 - N_MOD), (0, 0)))


_C_QLAT, _C_KVLAT, _C_KR, _C_KRR, _C_NA, _C_HY, _C_LU, _C_LG, _C_END = (
    0, 256, 384, 512, 640, 640 + 3 * NA_HEADS * LANE, 640 + 1536 + 768, 640 + 1536 + 1024, 640 + 1536 + 1280)


def _proj_kernel(x_ref, mod_ref, g1_ref, w1_ref, gq_ref, gkv_ref, wuq_ref, wuqr_ref, wk_ref, wv_ref,
                 cos_ref, sin_ref, q_o, k_o, v_o, nq_o, nk_o, nv_o, hy_o, lu_o, lg_o):
    m = mod_ref[0, 0]
    h = _normmod(x_ref[0], g1_ref[...], m[1:2], m[0:1])
    z = _dot(h.astype(BF16), w1_ref[...])
    qlat = z[:, _C_QLAT:_C_KVLAT]
    kvlat = z[:, _C_KVLAT:_C_KR]
    kr = z[:, _C_KR:_C_KRR]
    krr = z[:, _C_KRR:_C_NA]
    qn = qlat * lax.rsqrt(jnp.sum(qlat * qlat, axis=-1, keepdims=True) * (1.0 / MLA_Q_RANK) + NORM_EPS) * gq_ref[...]
    kvn = kvlat * lax.rsqrt(jnp.mean(kvlat * kvlat, axis=-1, keepdims=True) + NORM_EPS) * gkv_ref[...]
    qn = qn.astype(BF16)
    kvn = kvn.astype(BF16)
    q = _dot(qn, wuq_ref[...])
    qr = _dot(qn, wuqr_ref[...])
    kk = _dot(kvn, wk_ref[...])
    vv = _dot(kvn, wv_ref[...])
    cos = cos_ref[...]
    sin = sin_ref[...]
    krope = kr * cos + krr * sin
    for hd in range(MLA_HEADS):
        sl = slice(hd * LANE, (hd + 1) * LANE)
        q_o[0, hd] = (q[:, sl] * cos + qr[:, sl] * sin).astype(BF16)
        k_o[0, hd] = (kk[:, sl] + krope).astype(BF16)
        v_o[0, hd] = vv[:, sl].astype(BF16)
    for hd in range(NA_HEADS):
        for which, ref in enumerate((nq_o, nk_o, nv_o)):
            lo = _C_NA + (which * NA_HEADS + hd) * LANE
            ref[0, hd] = z[:, lo:lo + LANE].astype(BF16)
    hy_o[0] = z[:, _C_HY:_C_LU]
    lu_o[0] = z[:, _C_LU:_C_LG]
    lg_o[0] = z[:, _C_LG:_C_END]


def _pad_to(a, n, axis):
    pad = [(0, 0)] * a.ndim
    pad[axis] = (0, n - a.shape[axis])
    return jnp.pad(a, pad)


def _rot_cols(w):
    half = w.shape[-1] // 2
    return jnp.concatenate([-w[..., half:], w[..., :half]], axis=-1)


def _head_blocks(cols_per_head):
    out = []
    for pieces in cols_per_head:
        k = pieces[0][0].shape[0]
        blk = jnp.zeros((k, LANE), F32)
        for arr, off in pieces:
            blk = blk.at[:, off:off + arr.shape[1]].set(arr)
        out.append(blk)
    return jnp.concatenate(out, axis=1)


def _proj_weights(lp):
    w_in = lp['w_in']
    d = w_in.shape[0]
    o = 0
    parts = {}
    for name, n in (('q', MLA_Q_RANK), ('kv', MLA_KV_RANK), ('kr', MLA_ROPE), ('na', 3 * NA_HEADS * NA_HEAD_DIM),
                    ('hy', 3 * HY_WIDTH), ('lu', LRU_WIDTH), ('lg', LRU_WIDTH), ('gt', 4 * d)):
        parts[name] = w_in[:, o:o + n]
        o += n
    zeros = lambda n: jnp.zeros((d, n), F32)
    kr_blk = jnp.concatenate([zeros(MLA_NOPE), parts['kr'], zeros(LANE - MLA_NOPE - MLA_ROPE)], axis=1)
    krr_blk = jnp.concatenate([zeros(MLA_NOPE), _rot_cols(parts['kr']), zeros(LANE - MLA_NOPE - MLA_ROPE)], axis=1)
    na_scale = NA_HEAD_DIM ** -0.5
    na_cols = []
    for which in range(3):
        for hd in range(NA_HEADS):
            lo = (which * NA_HEADS + hd) * NA_HEAD_DIM
            blk = parts['na'][:, lo:lo + NA_HEAD_DIM] * (na_scale if which == 0 else 1.0)
            na_cols.append(_pad_to(blk, LANE, 1))
    w1 = jnp.concatenate([_pad_to(parts['q'], MLA_Q_PAD, 1), parts['kv'], kr_blk, krr_blk] + na_cols
                         + [parts['hy'], parts['lu'], parts['lg']], axis=1).astype(BF16)
    mla_scale = (MLA_NOPE + MLA_ROPE) ** -0.5
    wuq = _pad_to(lp['mla_w_uq'], MLA_Q_PAD, 0) * mla_scale
    dq = MLA_NOPE + MLA_ROPE
    wuq_main = _head_blocks([[(wuq[:, hd * dq:hd * dq + dq], 0)] for hd in range(MLA_HEADS)])
    wuq_rot = _head_blocks([[(_rot_cols(wuq[:, hd * dq + MLA_NOPE:hd * dq + dq]), MLA_NOPE)]
                            for hd in range(MLA_HEADS)])
    dkv = MLA_NOPE + MLA_V
    wukv = lp['mla_w_ukv']
    wk = _head_blocks([[(wukv[:, hd * dkv:hd * dkv + MLA_NOPE], 0)] for hd in range(MLA_HEADS)])
    wv = _head_blocks([[(wukv[:, hd * dkv + MLA_NOPE:hd * dkv + dkv], 0)] for hd in range(MLA_HEADS)])
    gq = _pad_to(lp['mla_g_q'].reshape(1, -1), MLA_Q_PAD, 1)
    gkv = lp['mla_g_kv'].reshape(1, -1)
    return dict(w1=w1, w_gate=parts['gt'].astype(BF16), gq=gq, gkv=gkv, wuq=wuq_main.astype(BF16),
                wuq_rot=wuq_rot.astype(BF16), wk=wk.astype(BF16), wv=wv.astype(BF16))


def _rope_tables(seq, n_tok):
    t = jnp.arange(seq, dtype=I32)
    row = (t // GRID_W).astype(F32)
    col = (t % GRID_W).astype(F32)
    n_axis = MLA_ROPE // 4
    inv_freq = ROPE_THETA ** (-jnp.arange(n_axis, dtype=F32) / n_axis)
    ang = jnp.concatenate([row[:, None] * inv_freq, col[:, None] * inv_freq], axis=-1)
    cos = jnp.concatenate([jnp.cos(ang), jnp.cos(ang)], axis=-1)
    sin = jnp.concatenate([jnp.sin(ang), jnp.sin(ang)], axis=-1)
    cos_t = jnp.ones((n_tok, LANE), F32).at[:seq, MLA_NOPE:MLA_NOPE + MLA_ROPE].set(cos)
    sin_t = jnp.zeros((n_tok, LANE), F32).at[:seq, MLA_NOPE:MLA_NOPE + MLA_ROPE].set(sin)
    return cos_t, sin_t


def _kind_map(n_lat_tiles):
    return lambda b, i: (b, jnp.where(i >= n_lat_tiles, 1, 0), 0, 0)


def _project(xa, modtab, g1, pw, rope, n_lat_tiles):
    b, nt, d = xa.shape
    cos_t, sin_t = rope
    full = lambda a: pl.BlockSpec(a.shape, lambda bb, i: (0,) * a.ndim)
    head_out = lambda: pl.BlockSpec((1, MLA_HEADS, TM, LANE), lambda bb, i: (bb, 0, i, 0))
    tok_out = lambda n: pl.BlockSpec((1, TM, n), lambda bb, i: (bb, i, 0))
    head_shape = jax.ShapeDtypeStruct((b, MLA_HEADS, nt, LANE), BF16)
    tok_shape = lambda n: jax.ShapeDtypeStruct((b, nt, n), F32)
    weights = (g1, pw['w1'], pw['gq'], pw['gkv'], pw['wuq'], pw['wuq_rot'], pw['wk'], pw['wv'])
    return pl.pallas_call(
        _proj_kernel,
        grid=(b, nt // TM),
        in_specs=[pl.BlockSpec((1, TM, d), lambda bb, i: (bb, i, 0)),
                  pl.BlockSpec((1, 1, 8, d), _kind_map(n_lat_tiles))]
                 + [full(w) for w in weights]
                 + [pl.BlockSpec((TM, LANE), lambda bb, i: (i, 0))] * 2,
        out_specs=[head_out() for _ in range(6)] + [tok_out(3 * HY_WIDTH), tok_out(LRU_WIDTH), tok_out(LRU_WIDTH)],
        out_shape=[head_shape] * 6 + [tok_shape(3 * HY_WIDTH), tok_shape(LRU_WIDTH), tok_shape(LRU_WIDTH)],
        compiler_params=_cparams("parallel", "parallel"),
        name="input_projection",
    )(xa, modtab, *weights, cos_t, sin_t)


def _softmax_pv(parts):
    m = None
    for s, _ in parts:
        mm = jnp.max(s, axis=-1, keepdims=True)
        m = mm if m is None else jnp.maximum(m, mm)
    acc = None
    den = None
    for s, v in parts:
        p = jnp.exp(s - m)
        l = jnp.sum(p, axis=-1, keepdims=True)
        o = _dot(p.astype(BF16), v)
        acc = o if acc is None else acc + o
        den = l if den is None else den + l
    return acc / den


def _mla_kernel(q_ref, k_ref, v_ref, o_ref, *, seq, n_lat_tiles):
    i = pl.program_id(1)
    nt = k_ref.shape[2]

    def attend(lo, hi):
        for hd in range(MLA_HEADS):
            s = _dot_nt(q_ref[0, hd], k_ref[0, hd, lo:hi, :])
            o = _softmax_pv([(s, v_ref[0, hd, lo:hi, :])])
            o_ref[0, :, hd * LANE:(hd + 1) * LANE] = o.astype(BF16)

    @pl.when(i < n_lat_tiles)
    def _():
        attend(0, nt)

    @pl.when(i >= n_lat_tiles)
    def _():
        attend(seq, nt)


def _mla_attention(q, k, v, seq):
    b, h, nt, _ = q.shape
    kv_spec = pl.BlockSpec((1, h, nt, LANE), lambda bb, i: (bb, 0, 0, 0))
    return pl.pallas_call(
        functools.partial(_mla_kernel, seq=seq, n_lat_tiles=seq // TM),
        grid=(b, nt // TM),
        in_specs=[pl.BlockSpec((1, h, TM, LANE), lambda bb, i: (bb, 0, i, 0)), kv_spec, kv_spec],
        out_specs=pl.BlockSpec((1, TM, h * LANE), lambda bb, i: (bb, i, 0)),
        out_shape=jax.ShapeDtypeStruct((b, nt, h * LANE), BF16),
        compiler_params=_cparams("parallel", "arbitrary"),
        name="mla_attention",
    )(q, k, v)


def _na_bias_tables(rpb, rows):
    n_blk = rows // NA_TILE_ROWS
    col = np.arange(GRID_W)
    c0 = np.clip(col - NA_WIN_C // 2, 0, GRID_W - NA_WIN_C)
    in_win = (col[None, :] >= c0[:, None]) & (col[None, :] < c0[:, None] + NA_WIN_C)
    dc = np.clip(col[None, :] - col[:, None], 1 - NA_WIN_C, NA_WIN_C - 1) + NA_WIN_C - 1
    rpb = rpb.astype(F32)
    tables = []
    for j in (0, 1, n_blk - 1):
        w0 = min(max(NA_TILE_ROWS * j - NA_WIN_R // 2, 0), rows - NA_KEY_ROWS)
        r = NA_TILE_ROWS * j + np.arange(NA_TILE_ROWS)
        kr = w0 + np.arange(NA_KEY_ROWS)
        r0 = np.clip(r - NA_WIN_R // 2, 0, rows - NA_WIN_R)
        row_ok = (kr[None, :] >= r0[:, None]) & (kr[None, :] < r0[:, None] + NA_WIN_R)
        dr = np.clip(kr[None, :] - r[:, None] + NA_WIN_R - 1, 0, 2 * NA_WIN_R - 2)
        oh_r = jnp.asarray(np.eye(2 * NA_WIN_R - 1, dtype=np.float32)[dr.reshape(-1)])
        oh_c = jnp.asarray(np.eye(2 * NA_WIN_C - 1, dtype=np.float32)[dc.reshape(-1)])
        bias = jnp.einsum('ar,hrc,bc->hab', oh_r, rpb, oh_c, precision=lax.Precision.HIGHEST)
        bias = bias.reshape(NA_HEADS, NA_TILE_ROWS, NA_KEY_ROWS, GRID_W, GRID_W)
        mask = row_ok[:, :, None, None] & in_win[None, None, :, :]
        bias = jnp.where(jnp.asarray(mask)[None], bias, -jnp.inf)
        tables.append(bias.transpose(0, 1, 3, 2, 4).reshape(NA_HEADS, TM, NA_KEYS))
    return jnp.stack(tables)


def _na_kernel(q_ref, k_ref, v_ref, bias_ref, o_ref, *, seq, n_lat_tiles):
    i = pl.program_id(1)
    nt = k_ref.shape[2]
    rows = seq // GRID_W

    @pl.when(i < n_lat_tiles)
    def _():
        w0 = jnp.clip(NA_TILE_ROWS * i - NA_WIN_R // 2, 0, rows - NA_KEY_ROWS)
        start = pl.multiple_of(w0 * GRID_W, GRID_W)
        for hd in range(NA_HEADS):
            q = q_ref[0, hd]
            s_loc = _dot_nt(q, k_ref[0, hd, pl.ds(start, NA_KEYS), :]) + bias_ref[0, hd]
            s_ctx = _dot_nt(q, k_ref[0, hd, seq:nt, :])
            o = _softmax_pv([(s_loc, v_ref[0, hd, pl.ds(start, NA_KEYS), :]), (s_ctx, v_ref[0, hd, seq:nt, :])])
            o_ref[0, :, hd * LANE:(hd + 1) * LANE] = o.astype(BF16)

    @pl.when(i >= n_lat_tiles)
    def _():
        for hd in range(NA_HEADS):
            s = _dot_nt(q_ref[0, hd], k_ref[0, hd, seq:nt, :])
            o = _softmax_pv([(s, v_ref[0, hd, seq:nt, :])])
            o_ref[0, :, hd * LANE:(hd + 1) * LANE] = o.astype(BF16)


def _na_attention(q, k, v, bias, seq):
    b, h, nt, _ = q.shape
    n_lat = seq // TM
    kv_spec = pl.BlockSpec((1, h, nt, LANE), lambda bb, i: (bb, 0, 0, 0))
    cfg = lambda bb, i: (jnp.where(i == 0, 0, jnp.where(i >= n_lat - 1, 2, 1)), 0, 0, 0)
    return pl.pallas_call(
        functools.partial(_na_kernel, seq=seq, n_lat_tiles=n_lat),
        grid=(b, nt // TM),
        in_specs=[pl.BlockSpec((1, h, TM, LANE), lambda bb, i: (bb, 0, i, 0)), kv_spec, kv_spec,
                  pl.BlockSpec((1, h, TM, NA_KEYS), cfg)],
        out_specs=pl.BlockSpec((1, TM, h * LANE), lambda bb, i: (bb, i, 0)),
        out_shape=jax.ShapeDtypeStruct((b, nt, h * LANE), BF16),
        compiler_params=_cparams("parallel", "arbitrary"),
        name="neighbourhood_attention",
    )(q, k, v, bias)


HY_PRE_CHUNK = 256


def _hy_pre_kernel(z_ref, w_ref, b_ref, lat_o, lat_bf_o, ctx_o, ctx_bf_o, *, seq, ctx_len):
    w = w_ref[...]
    bias = b_ref[...]
    ch = HY_PRE_CHUNK
    zero = jnp.zeros((8, HY_WIDTH), F32)
    for seg_lo, seg_len, o_ref, obf_ref in ((0, seq, lat_o, lat_bf_o), (seq, ctx_len, ctx_o, ctx_bf_o)):
        for c in range(seg_len // ch):
            s = seg_lo + c * ch
            before = zero if c == 0 else z_ref[0, s - 8:s, :]
            after = zero if c == seg_len // ch - 1 else z_ref[0, s + ch:s + ch + 8, :]
            win = jnp.concatenate([before, z_ref[0, s:s + ch, :], after], axis=0)
            n = ch + 16
            y = bias + w[1:2] * win[8:8 + ch]
            y = y + w[0:1] * pltpu.roll(win, 1, axis=0)[8:8 + ch]
            y = y + w[2:3] * pltpu.roll(win, n - 1, axis=0)[8:8 + ch]
            o_ref[0, c * ch:(c + 1) * ch, :] = y
            obf_ref[0, c * ch:(c + 1) * ch, :] = y.astype(BF16)


def _hy_pre(hy, w, bvec, seq):
    b, nt, _ = hy.shape
    ctx_len = nt - seq
    wpad = _pad_to(w, 8, 0)
    out_specs, out_shape = [], []
    for length in (seq, ctx_len):
        for dt in (F32, BF16):
            out_specs.append(pl.BlockSpec((1, length, HY_WIDTH), lambda bb, g: (g, 0, bb)))
            out_shape.append(jax.ShapeDtypeStruct((3, length, b * HY_WIDTH), dt))
    return pl.pallas_call(
        functools.partial(_hy_pre_kernel, seq=seq, ctx_len=ctx_len),
        grid=(b, 3),
        in_specs=[pl.BlockSpec((1, nt, HY_WIDTH), lambda bb, g: (bb, 0, g)),
                  pl.BlockSpec((8, HY_WIDTH), lambda bb, g: (0, g)),
                  pl.BlockSpec((1, HY_WIDTH), lambda bb, g: (0, g))],
        out_specs=out_specs,
        out_shape=out_shape,
        compiler_params=_cparams("parallel", "parallel"),
        name="hyena_short_conv",
    )(hy, wpad, bvec.reshape(1, -1))


def _hp_dot(a, b):
    return jnp.dot(a, b, preferred_element_type=F32, precision=lax.Precision.HIGHEST)


def _hy_filter_kernel(f_ref, w1_ref, b1_ref, w2_ref, b2_ref, w3_ref, dec_ref, o_ref, obf_ref):
    f = f_ref[...]
    h = jnp.sin(_hp_dot(f, w1_ref[...]) + b1_ref[...])
    h = jnp.sin(_hp_dot(h, w2_ref[...]) + b2_ref[...])
    h = _hp_dot(h, w3_ref[...])
    h = h * jnp.exp(-f[:, 0:1] * jnp.abs(dec_ref[...]))
    o_ref[...] = h
    obf_ref[...] = h.astype(BF16)


def _hy_pos_features(length):
    t = jnp.linspace(0.0, 1.0, length, dtype=F32)[:, None]
    w = 2.0 * math.pi * jnp.arange(length, dtype=F32)[:, None] / length
    f = jnp.linspace(1e-4, HY_BANDS - 1, HY_BANDS, dtype=F32)[None, :]
    z = w * f
    return jnp.concatenate([t, jnp.cos(z), -jnp.sin(z)], axis=-1)


def _hy_filters(length, lp):
    feats = _hy_pos_features(length)
    n_out = HY_ORDER * 2 * HY_WIDTH
    tl = min(length, 512)
    full = lambda a: pl.BlockSpec(a.shape, lambda i: (0,) * a.ndim)
    args = (lp['hy_w1'], lp['hy_b1'].reshape(1, -1), lp['hy_w2'], lp['hy_b2'].reshape(1, -1), lp['hy_w3'],
            lp['hy_decay'].reshape(1, n_out))
    return pl.pallas_call(
        _hy_filter_kernel,
        grid=(length // tl,),
        in_specs=[pl.BlockSpec((tl, HY_EMB), lambda i: (i, 0))] + [full(a) for a in args],
        out_specs=[pl.BlockSpec((tl, n_out), lambda i: (i, 0))] * 2,
        out_shape=[jax.ShapeDtypeStruct((length, n_out), F32), jax.ShapeDtypeStruct((length, n_out), BF16)],
        compiler_params=_cparams("parallel"),
        name="hyena_filter_mlp",
    )(feats, *args)


def _dft_matrices(length):
    n2 = 2 * length
    step = DFT_ROW_STEP
    n = jnp.arange(length, dtype=I32)

    def trig(kv):
        ang = ((kv[:, None] * n[None, :]) % n2).astype(F32) * (2.0 * math.pi / n2)
        return jnp.cos(ang), jnp.sin(ang)

    c1, s1 = trig(jnp.arange(length // step, dtype=I32) * step)
    c2, s2 = trig(jnp.arange(step, dtype=I32))
    coarse = pl.BlockSpec((1, 1, length), lambda j: (j, 0, 0))
    fine = pl.BlockSpec((step, length), lambda j: (0, 0))
    out = pl.BlockSpec((step, length), lambda j: (j, 0))
    return pl.pallas_call(
        _dft_matrix_kernel,
        grid=(length // step,),
        in_specs=[coarse, coarse, fine, fine],
        out_specs=[out, out, out],
        out_shape=[jax.ShapeDtypeStruct((length, length), BF16)] * 3,
        compiler_params=_cparams("parallel"),
        name="dft_matrices",
    )(c1[:, None, :], s1[:, None, :], c2, s2)


DFT_ROW_STEP = 64


def _dft_matrix_kernel(c1_ref, s1_ref, c2_ref, s2_ref, mc_o, ms_o, mst_o):
    c1, s1 = c1_ref[0], s1_ref[0]
    c2, s2 = c2_ref[...], s2_ref[...]
    cos = c1 * c2 - s1 * s2
    neg_sin = -(s1 * c2 + c1 * s2)
    row = lax.broadcasted_iota(I32, cos.shape, 0)
    col = lax.broadcasted_iota(I32, cos.shape, 1)
    first_row = (row + pl.program_id(0) * DFT_ROW_STEP) == 0
    mc_o[...] = cos.astype(BF16)
    ms_o[...] = jnp.where(first_row, jnp.where(col % 2 == 0, 1.0, -1.0), neg_sin).astype(BF16)
    mst_o[...] = jnp.where(col == 0, jnp.where(row % 2 == 0, 1.0, -1.0), neg_sin).astype(BF16)


def _dft_fwd_kernel(mc_ref, ms_ref, x_ref, *rest, with_taps):
    ure = _dot(mc_ref[...], x_ref[...])
    uim = _dot(ms_ref[...], x_ref[...])
    if not with_taps:
        ure_o, uim_o = rest
        ure_o[...] = ure
        uim_o[...] = uim
        return
    a_ref, b_ref, c_ref, d_ref, zre_o, zim_o = rest
    a, b, c, d = a_ref[...], b_ref[...], c_ref[...], d_ref[...]
    for bb in range(x_ref.shape[1] // HY_WIDTH):
        sl = slice(bb * HY_WIDTH, (bb + 1) * HY_WIDTH)
        zre_o[:, sl] = (ure[:, sl] * a - uim[:, sl] * b).astype(BF16)
        zim_o[:, sl] = (ure[:, sl] * c + uim[:, sl] * d).astype(BF16)


def _col_block(nc, cap):
    return min(nc, cap)


def _dft_fwd(mats, x, plane, taps=None):
    mc, ms, _ = mats
    _, length, nc = x.shape
    tk = min(length, 256)
    cb = _col_block(nc, 1024)
    grid = (nc // cb, length // tk)
    m_spec = pl.BlockSpec((tk, length), lambda c, j: (j, 0))
    x_spec = pl.BlockSpec((None, length, cb), lambda c, j: (plane, 0, c))
    o_spec = pl.BlockSpec((tk, cb), lambda c, j: (j, c))
    if taps is None:
        return pl.pallas_call(
            functools.partial(_dft_fwd_kernel, with_taps=False),
            grid=grid, in_specs=[m_spec, m_spec, x_spec], out_specs=[o_spec, o_spec],
            out_shape=[jax.ShapeDtypeStruct((length, nc), F32)] * 2,
            compiler_params=_cparams("parallel", "arbitrary"),
            name="hyena_dft_filters",
        )(mc, ms, x)
    t_spec = pl.BlockSpec((tk, HY_WIDTH), lambda c, j: (j, 0))
    return pl.pallas_call(
        functools.partial(_dft_fwd_kernel, with_taps=True),
        grid=grid, in_specs=[m_spec, m_spec, x_spec] + [t_spec] * 4, out_specs=[o_spec, o_spec],
        out_shape=[jax.ShapeDtypeStruct((length, nc), BF16)] * 2,
        compiler_params=_cparams("parallel", "arbitrary"),
        name="hyena_dft_forward",
    )(mc, ms, x, *taps)


def _dft_inv_kernel(mc_ref, mst_ref, zre_ref, zim_ref, gate_ref, prev_ref, bias_ref, *outs, last):
    conv = _dot(mc_ref[...], zre_ref[...]) + _dot(mst_ref[...], zim_ref[...])
    y = gate_ref[...] * (conv + prev_ref[...] * bias_ref[...])
    if last:
        (tok_o,) = outs
        for bb in range(y.shape[1] // HY_WIDTH):
            tok_o[bb] = y[:, bb * HY_WIDTH:(bb + 1) * HY_WIDTH].astype(BF16)
    else:
        y_o, ybf_o = outs
        y_o[...] = y
        ybf_o[...] = y.astype(BF16)


def _dft_inv(mats, zre, zim, gate, gate_plane, prev, prev_plane, bias_row, last):
    mc, _, mst = mats
    length, nc = zre.shape
    tm = min(length, 256)
    cb = _col_block(nc, 512)
    grid = (nc // cb, length // tm)
    m_spec = pl.BlockSpec((tm, length), lambda c, i: (i, 0))
    z_spec = pl.BlockSpec((length, cb), lambda c, i: (0, c))
    e_spec = lambda plane: pl.BlockSpec((None, tm, cb), lambda c, i: (plane, i, c))
    b_spec = pl.BlockSpec((1, cb), lambda c, i: (0, c))
    if last:
        nb = nc // HY_WIDTH
        out_specs = [pl.BlockSpec((cb // HY_WIDTH, tm, HY_WIDTH), lambda c, i: (c, i, 0))]
        out_shape = [jax.ShapeDtypeStruct((nb, length, HY_WIDTH), BF16)]
    else:
        out_specs = [e_spec(0), e_spec(0)]
        out_shape = [jax.ShapeDtypeStruct((1, length, nc), F32), jax.ShapeDtypeStruct((1, length, nc), BF16)]
    return pl.pallas_call(
        functools.partial(_dft_inv_kernel, last=last),
        grid=grid, in_specs=[m_spec, m_spec, z_spec, z_spec, e_spec(gate_plane), e_spec(prev_plane), b_spec],
        out_specs=out_specs, out_shape=out_shape,
        compiler_params=_cparams("parallel", "arbitrary"),
        name="hyena_dft_inverse",
    )(mc, mst, zre, zim, gate, prev, bias_row)


def _hy_tap_tables(ure, uim, filt, length):
    n2 = 2 * length
    w = HY_WIDTH
    scale = jnp.full((length, 1), 2.0 / n2, F32).at[0, 0].set(1.0 / n2)
    tables = []
    for o in range(HY_ORDER):
        f_sl = slice((2 * o) * w, (2 * o + 1) * w)
        b_sl = slice((2 * o + 1) * w, (2 * o + 2) * w)
        hb0 = filt[0:1, b_sl]
        tre = ure[:, f_sl] + ure[:, b_sl] - hb0
        tim = uim[:, f_sl] - uim[:, b_sl]
        t_nyq = uim[0:1, f_sl] + uim[0:1, b_sl] - hb0
        first = (jnp.arange(length) == 0)[:, None]
        a = tre * scale
        bm = jnp.where(first, 0.0, tim * scale)
        c = bm
        d = jnp.where(first, t_nyq * scale, tre * scale)
        tables.append((a, bm, c, d))
    return tables


def _hyena_seq(mats, vx, vx_bf, lp, n_batch):
    length = vx.shape[1]
    filt, filt_bf = _hy_filters(length, lp)
    ure, uim = _dft_fwd(mats, filt_bf[None], 0)
    tables = _hy_tap_tables(ure, uim, filt, length)
    bias = lp['hy_bias'].astype(F32)
    y, y_bf = vx, vx_bf
    for o in range(HY_ORDER):
        zre, zim = _dft_fwd(mats, y_bf, 0, tables[o])
        bias_row = jnp.tile(bias[o][None, :], (1, n_batch))
        last = o == HY_ORDER - 1
        res = _dft_inv(mats, zre, zim, vx, o + 1, y, 0, bias_row, last)
        if last:
            return res[0]
        y, y_bf = res


def _lru_kernel(u_ref, g_ref, cw_ref, cb_ref, wa_ref, ba_ref, wx_ref, bx_ref, lam_ref, o_ref,
                pad_ref, y_ref, *, seq, ctx_len):
    tc = LRU_CHUNK
    halo = LRU_HALO
    width = LRU_WIDTH
    lat_off = halo
    ctx_off = 2 * halo + seq
    zero = jnp.zeros((halo, width), F32)
    pad_ref[0:halo, :] = zero
    pad_ref[lat_off:lat_off + seq, :] = u_ref[0, 0:seq, :]
    pad_ref[lat_off + seq:ctx_off, :] = zero
    pad_ref[ctx_off:ctx_off + ctx_len, :] = u_ref[0, seq:seq + ctx_len, :]
    pad_ref[ctx_off + ctx_len:ctx_off + ctx_len + halo, :] = zero
    row = lax.broadcasted_iota(I32, (tc, width), 0)
    n_win = tc + 2 * halo

    def chunk(pad_off, y_off, s, carry, d):
        wstart = pl.multiple_of(pad_off + s - halo, 8)
        win = pad_ref[pl.ds(wstart, n_win), :]
        cw = cw_ref[d]
        xc = cb_ref[d]
        for k in range(LRU_CONV):
            shift = (LRU_CONV - 1 - k) if d == 0 else -k
            rolled = win if shift == 0 else pltpu.roll(win, shift % n_win, axis=0)
            xc = xc + cw[k:k + 1] * rolled[halo:halo + tc]
        xb = xc.astype(BF16)
        r = _sigmoid(_dot(xb, wa_ref[d]) + ba_ref[d])
        gi = _sigmoid(_dot(xb, wx_ref[d]) + bx_ref[d])
        lam = lam_ref[d]
        softplus = jnp.maximum(-lam, 0.0) + jnp.log1p(jnp.exp(-jnp.abs(lam)))
        log_a = -LRU_C * r * softplus
        a = jnp.exp(log_a)
        bt = jnp.sqrt(-jnp.tanh(log_a) * (a * a + 1.0)) * (gi * xc)
        sft = 1
        while sft < tc:
            if d == 0:
                keep = row >= sft
                a_s = jnp.where(keep, pltpu.roll(a, sft, axis=0), 1.0)
                b_s = jnp.where(keep, pltpu.roll(bt, sft, axis=0), 0.0)
            else:
                keep = row < tc - sft
                a_s = jnp.where(keep, pltpu.roll(a, tc - sft, axis=0), 1.0)
                b_s = jnp.where(keep, pltpu.roll(bt, tc - sft, axis=0), 0.0)
            bt = a * b_s + bt
            a = a * a_s
            sft *= 2
        h = a * carry + bt
        yo = pl.multiple_of(y_off + s, 8)
        if d == 0:
            y_ref[pl.ds(yo, tc), :] = h
            return h[tc - 1:tc]
        y_ref[pl.ds(yo, tc), :] = y_ref[pl.ds(yo, tc), :] + h
        return h[0:1]

    n_lat = seq // tc
    n_ctx = ctx_len // tc
    for d in range(2):
        carry = jnp.zeros((1, width), F32)
        order = range(n_ctx) if d == 0 else range(n_ctx - 1, -1, -1)
        for c in order:
            carry = chunk(ctx_off, seq, c * tc, carry, d)

        def body(j, cr, d=d):
            jj = j if d == 0 else n_lat - 1 - j
            return chunk(lat_off, 0, jj * tc, cr, d)

        lax.fori_loop(0, n_lat, body, carry)
    o_ref[0] = (y_ref[...] * _gelu_tanh(g_ref[0])).astype(BF16)


def _block_diag(w):
    nd, nb, c, _ = w.shape
    out = jnp.zeros((nd, nb * c, nb * c), w.dtype)
    for n in range(nb):
        out = out.at[:, n * c:(n + 1) * c, n * c:(n + 1) * c].set(w[:, n])
    return out


def _lru_mixer(lu, lg, lp, seq):
    b, nt, w = lu.shape
    ctx_len = nt - seq
    row3 = lambda a: a.reshape(2, 1, w)
    args = (_pad_to(lp['lru_conv_w'], 8, 1), row3(lp['lru_conv_b']), _block_diag(lp['lru_wa']).astype(BF16),
            row3(lp['lru_ba']), _block_diag(lp['lru_wx']).astype(BF16), row3(lp['lru_bx']), row3(lp['lru_lambda']))
    full = lambda a: pl.BlockSpec(a.shape, lambda bb: (0,) * a.ndim)
    tok = pl.BlockSpec((1, nt, w), lambda bb: (bb, 0, 0))
    return pl.pallas_call(
        functools.partial(_lru_kernel, seq=seq, ctx_len=ctx_len),
        grid=(b,),
        in_specs=[tok, tok] + [full(a) for a in args],
        out_specs=tok,
        out_shape=jax.ShapeDtypeStruct((b, nt, w), BF16),
        scratch_shapes=[pltpu.VMEM((nt + 3 * LRU_HALO, w), F32), pltpu.VMEM((nt, w), F32)],
        compiler_params=_cparams("parallel"),
        name="rglru_scan",
    )(lu, lg, *args)


def _merge_kernel(x_ref, mod_ref, g1_ref, a_ref, b_ref, c_ref, d_ref, wg_ref, wa_ref, wb_ref, wc_ref, wd_ref,
                  wo_ref, o_ref):
    m = mod_ref[0, 0]
    x = x_ref[0]
    h = _normmod(x, g1_ref[...], m[1:2], m[0:1]).astype(BF16)
    dm = x.shape[-1]
    acc = None
    for k, (br, w) in enumerate(((a_ref, wa_ref), (b_ref, wb_ref), (c_ref, wc_ref), (d_ref, wd_ref))):
        gate = _sigmoid(_dot(h, wg_ref[:, k * dm:(k + 1) * dm]))
        term = gate * _dot(br[0], w[...])
        acc = term if acc is None else acc + term
    y = _dot(acc.astype(BF16), wo_ref[...])
    o_ref[0] = x + m[2:3] * y


def _merge(xa, modtab, g1, branches, w_gate, lp, n_lat_tiles):
    b, nt, d = xa.shape
    wbr = lp['w_branch']
    head_rows = lambda w, dv: jnp.concatenate(
        [_pad_to(w[hd * dv:(hd + 1) * dv], LANE, 0) for hd in range(4)], axis=0)
    weights = (head_rows(wbr[0], MLA_V).astype(BF16), head_rows(wbr[1], NA_HEAD_DIM).astype(BF16),
               wbr[2].astype(BF16), wbr[3].astype(BF16), lp['w_out'].astype(BF16))
    full = lambda a: pl.BlockSpec(a.shape, lambda bb, i: (0,) * a.ndim)
    tok = lambda n: pl.BlockSpec((1, TM, n), lambda bb, i: (bb, i, 0))
    return pl.pallas_call(
        _merge_kernel,
        grid=(b, nt // TM),
        in_specs=[tok(d), pl.BlockSpec((1, 1, 8, d), _kind_map(n_lat_tiles)), full(g1)]
                 + [tok(br.shape[-1]) for br in branches] + [full(w_gate)] + [full(w) for w in weights],
        out_specs=tok(d),
        out_shape=jax.ShapeDtypeStruct((b, nt, d), F32),
        compiler_params=_cparams("parallel", "parallel"),
        name="merge_branches",
    )(xa, modtab, g1, *branches, w_gate, *weights)


U32 = jnp.uint32
EXPERT_ROWS = 512


def _pack_pair(x):
    n = x.shape[-1] // 2
    hi = lax.bitcast_convert_type(x[:, :n].astype(BF16).astype(F32), U32)
    lo = lax.bitcast_convert_type(x[:, n:].astype(BF16).astype(F32), U32)
    return hi | (lo >> 16)


def _unpack_pair(p):
    hi = lax.bitcast_convert_type(p & jnp.uint32(0xFFFF0000), F32)
    lo = lax.bitcast_convert_type(p << 16, F32)
    return hi, lo


def _router_kernel(x_ref, mod_ref, g2_ref, rw_ref, rb_ref, tri_ref, h2_o, idx_o, wts_o, rank_o, cnt_o, carry):
    i = pl.program_id(0)

    @pl.when(i == 0)
    def _():
        carry[...] = jnp.zeros_like(carry)

    m = mod_ref[0, 0]
    h2 = _normmod(x_ref[0], g2_ref[...], m[4:5], m[3:4])
    half = h2.shape[-1] // 2
    h2_o[0] = _pack_pair(h2[:, :half])
    h2_o[1] = _pack_pair(h2[:, half:])
    logits = lax.dot_general(rw_ref[...], h2, (((1,), (1,)), ((), ())), preferred_element_type=F32,
                             precision=lax.Precision.HIGHEST)
    scores = _sigmoid(logits)
    biased = scores + rb_ref[...]
    expert = lax.broadcasted_iota(I32, scores.shape, 0)
    picks = []
    onehot_all = jnp.zeros(scores.shape, F32)
    for _ in range(TOP_K):
        best = jnp.max(biased, axis=0, keepdims=True)
        arg = jnp.min(jnp.where(biased == best, expert, N_EXPERTS), axis=0, keepdims=True)
        hit = expert == arg
        sel = jnp.sum(jnp.where(hit, scores, 0.0), axis=0, keepdims=True)
        biased = jnp.where(hit, -jnp.inf, biased)
        onehot_all = onehot_all + jnp.where(hit, 1.0, 0.0)
        picks.append((arg, hit, sel))
    total = picks[0][2]
    for _, _, sel in picks[1:]:
        total = total + sel
    earlier = _dot(onehot_all.astype(BF16), tri_ref[...]) + carry[...]
    pad_rows = TOPK_PAD - TOP_K
    ranks = [jnp.sum(jnp.where(hit, earlier, 0.0), axis=0, keepdims=True).astype(I32) for _, hit, _ in picks]
    scale = ROUTED_SCALE / total
    idx_o[...] = jnp.concatenate([arg for arg, _, _ in picks] + [jnp.zeros((pad_rows, TM), I32)], axis=0)
    wts_o[...] = jnp.concatenate([sel * scale for _, _, sel in picks] + [jnp.zeros((pad_rows, TM), F32)], axis=0)
    rank_o[...] = jnp.concatenate(ranks + [jnp.zeros((pad_rows, TM), I32)], axis=0)
    carry[...] = carry[...] + jnp.sum(onehot_all, axis=1, keepdims=True)
    cnt_o[...] = carry[...]


def _tile_maps(tiles_used, n_lat_tiles):
    tok = lambda i: (i // tiles_used, i % tiles_used, 0)
    mod = lambda i: (i // tiles_used, jnp.where(i % tiles_used >= n_lat_tiles, 1, 0), 0, 0)
    return tok, mod


def _route(x1, modtab, g2, lp, n_lat_tiles, tiles_used):
    b, _, d = x1.shape
    n_tiles = b * tiles_used
    t = n_tiles * TM
    rw = lp['router_w'].T
    rb = lp['router_bias'].reshape(-1, 1)
    tri = (np.arange(TM)[:, None] < np.arange(TM)[None, :]).astype(np.float32)
    tri = jnp.asarray(tri, BF16)
    per_tok = lambda: pl.BlockSpec((TOPK_PAD, TM), lambda i: (0, i))
    full = lambda a: pl.BlockSpec(a.shape, lambda i: (0,) * a.ndim)
    tok, mod = _tile_maps(tiles_used, n_lat_tiles)
    return pl.pallas_call(
        _router_kernel,
        grid=(n_tiles,),
        in_specs=[pl.BlockSpec((1, TM, d), tok), pl.BlockSpec((1, 1, 8, d), mod),
                  full(g2), full(rw), full(rb), full(tri)],
        out_specs=[pl.BlockSpec((2, TM, d // 4), lambda i: (0, i, 0)), per_tok(), per_tok(), per_tok(),
                   pl.BlockSpec((N_EXPERTS, 1), lambda i: (0, 0))],
        out_shape=[jax.ShapeDtypeStruct((2, t, d // 4), U32), jax.ShapeDtypeStruct((TOPK_PAD, t), I32),
                   jax.ShapeDtypeStruct((TOPK_PAD, t), F32), jax.ShapeDtypeStruct((TOPK_PAD, t), I32),
                   jax.ShapeDtypeStruct((N_EXPERTS, 1), F32)],
        scratch_shapes=[pltpu.VMEM((N_EXPERTS, 1), F32)],
        compiler_params=_cparams("arbitrary"),
        name="moe_router",
    )(x1, modtab, g2, rw, rb, tri)


SC_WINDOW = 128


def _sc_mesh():
    return plsc.VectorSubcoreMesh(core_axis_name="c", subcore_axis_name="s")


def _sc_scatter_rows(src, idx, n_out):
    n, width = src.shape
    k_rep = idx.shape[0]
    half = n // SC_WINDOW // 2

    @functools.partial(pl.kernel, out_type=jax.ShapeDtypeStruct((n_out, width), src.dtype), mesh=_sc_mesh(),
                       scratch_types=[], name="moe_dispatch_sc")
    def scatter(src_hbm, idx_hbm, out_hbm):
        def body(x_vmem, *i_vmems):
            for i_vmem in i_vmems:
                pltpu.sync_copy(x_vmem, out_hbm.at[i_vmem.at[0]])

        pltpu.emit_pipeline(
            body,
            grid=(2, half),
            in_specs=[pl.BlockSpec((SC_WINDOW, width), lambda a, i: (a * half + i, 0))]
                     + [pl.BlockSpec((1, SC_WINDOW), lambda a, i, k=k: (k, a * half + i)) for k in range(k_rep)],
            out_specs=[],
            core_axis_name=("c", "s"),
            dimension_semantics=(pltpu.PARALLEL, pltpu.PARALLEL),
        )(src_hbm, *([idx_hbm] * k_rep))

    return scatter(src, idx)


def _sc_gather_rows(src, idx):
    k_rep, n = idx.shape
    width = src.shape[1]
    n_win = n // SC_WINDOW

    @functools.partial(pl.kernel, out_type=jax.ShapeDtypeStruct((k_rep * n, width), src.dtype), mesh=_sc_mesh(),
                       scratch_types=[], name="moe_gather_sc")
    def gather(src_hbm, idx_hbm, out_hbm):
        def body(i_vmem, o_vmem):
            pltpu.sync_copy(src_hbm.at[i_vmem.at[0]], o_vmem)

        pltpu.emit_pipeline(
            body,
            grid=(k_rep, n_win),
            in_specs=[pl.BlockSpec((1, SC_WINDOW), lambda k, i: (k, i))],
            out_specs=[pl.BlockSpec((SC_WINDOW, width), lambda k, i: (k * n_win + i, 0))],
            core_axis_name=("c", "s"),
            dimension_semantics=(pltpu.PARALLEL, pltpu.PARALLEL),
        )(idx_hbm, out_hbm)

    return gather(src, idx)


def _unpack_planes(p0, p1):
    return _unpack_pair(p0) + _unpack_pair(p1)


def _dot_quarters(parts, w_ref):
    q = parts[0].shape[-1]
    acc = None
    for j, part in enumerate(parts):
        term = _dot(part.astype(BF16), w_ref[j * q:(j + 1) * q, :])
        acc = term if acc is None else acc + term
    return acc


def _expert_kernel(be_ref, nv_ref, xs_ref, wg_ref, wu_ref, wd_ref, ys_o, wg_s, wu_s, wd_s):
    i = pl.program_id(0)
    prev = be_ref[jnp.maximum(i - 1, 0)]

    @pl.when((i == 0) | (be_ref[i] != prev))
    def _():
        wg_s[...] = wg_ref[0].astype(BF16)
        wu_s[...] = wu_ref[0].astype(BF16)
        wd_s[...] = wd_ref[0].astype(BF16)

    @pl.when(nv_ref[i] > 0)
    def _():
        keep = lax.broadcasted_iota(I32, xs_ref.shape[1:], 0) < nv_ref[i]
        parts = _unpack_planes(jnp.where(keep, xs_ref[0], jnp.uint32(0)), jnp.where(keep, xs_ref[1], jnp.uint32(0)))
        hid = _silu(_dot_quarters(parts, wg_s)) * _dot_quarters(parts, wu_s)
        y = _dot(hid.astype(BF16), wd_s[...])
        half = y.shape[-1] // 2
        ys_o[0] = _pack_pair(y[:, :half])
        ys_o[1] = _pack_pair(y[:, half:])

    @pl.when(nv_ref[i] <= 0)
    def _():
        ys_o[...] = jnp.zeros_like(ys_o)


def _experts(xs, block_e, n_valid, weights, layer):
    _, n_rows, dq = xs.shape
    d = 4 * dq
    n_blocks = n_rows // EXPERT_ROWS
    hid = EXPERT_HIDDEN
    grid_spec = pltpu.PrefetchScalarGridSpec(
        num_scalar_prefetch=2,
        grid=(n_blocks,),
        in_specs=[pl.BlockSpec((2, EXPERT_ROWS, dq), lambda i, be, nv: (0, i, 0)),
                  pl.BlockSpec((None, 1, d, hid), lambda i, be, nv: (layer, be[i], 0, 0)),
                  pl.BlockSpec((None, 1, d, hid), lambda i, be, nv: (layer, be[i], 0, 0)),
                  pl.BlockSpec((None, 1, hid, d), lambda i, be, nv: (layer, be[i], 0, 0))],
        out_specs=pl.BlockSpec((2, EXPERT_ROWS, dq), lambda i, be, nv: (0, i, 0)),
        scratch_shapes=[pltpu.VMEM((d, hid), BF16), pltpu.VMEM((d, hid), BF16), pltpu.VMEM((hid, d), BF16)],
    )
    return pl.pallas_call(
        _expert_kernel,
        grid_spec=grid_spec,
        out_shape=jax.ShapeDtypeStruct((2, n_rows, dq), U32),
        compiler_params=_cparams("arbitrary"),
        name="moe_experts",
    )(block_e, n_valid, xs, *weights)


def _combine_kernel(g_ref, wts_ref, h2_ref, x_ref, mod_ref, sg_ref, su_ref, sd_ref, gf_ref, o_ref, *, final):
    parts = _unpack_planes(h2_ref[0], h2_ref[1])
    hid = _silu(_dot_quarters(parts, sg_ref)) * _dot_quarters(parts, su_ref)
    shared = _dot(hid.astype(BF16), sd_ref[...])
    wts = wts_ref[...]
    routed = None
    for k in range(TOP_K):
        w = wts[:, k:k + 1]
        terms = [w * part for part in _unpack_planes(g_ref[k, 0], g_ref[k, 1])]
        routed = terms if routed is None else [r + t for r, t in zip(routed, terms)]
    f = shared + jnp.concatenate(routed, axis=1)
    m = mod_ref[0, 0]
    x2 = x_ref[0] + m[5:6] * f
    if final:
        x2 = x2 * lax.rsqrt(jnp.mean(x2 * x2, axis=-1, keepdims=True) + NORM_EPS) * gf_ref[...]
    o_ref[0] = x2


def _combine(g, wts, h2p, x1, modtab, lp, g_final, n_lat_tiles, tiles_used, final):
    b, _, d = x1.shape
    dq = d // 4
    weights = (lp['sh_w_gate'].astype(BF16), lp['sh_w_up'].astype(BF16), lp['sh_w_down'].astype(BF16),
               g_final.reshape(1, -1))
    full = lambda a: pl.BlockSpec(a.shape, lambda i: (0,) * a.ndim)
    tok, mod = _tile_maps(tiles_used, n_lat_tiles)
    return pl.pallas_call(
        functools.partial(_combine_kernel, final=final),
        grid=(b * tiles_used,),
        in_specs=[pl.BlockSpec((TOP_K, 2, TM, dq), lambda i: (0, 0, i, 0)),
                  pl.BlockSpec((TM, TOPK_PAD), lambda i: (i, 0)),
                  pl.BlockSpec((2, TM, dq), lambda i: (0, i, 0)),
                  pl.BlockSpec((1, TM, d), tok), pl.BlockSpec((1, 1, 8, d), mod)]
                 + [full(w) for w in weights],
        out_specs=pl.BlockSpec((1, TM, d), tok),
        out_shape=jax.ShapeDtypeStruct((b, tiles_used * TM, d), F32),
        compiler_params=_cparams("parallel"),
        name="moe_combine",
    )(g, wts, h2p, x1, modtab, *weights)


def _moe(x1, modtab, g2, lp, g_final, n_lat_tiles, final):
    b, nt, d = x1.shape
    tiles_used = n_lat_tiles if final else nt // TM
    t = b * tiles_used * TM
    h2p, idx, wts, rank, cnt = _route(x1, modtab, g2, lp, n_lat_tiles, tiles_used)
    counts = cnt[:, 0].astype(I32)
    padded = (counts + EXPERT_ROWS - 1) // EXPERT_ROWS * EXPERT_ROWS
    p_ends = jnp.cumsum(padded)
    p_starts = p_ends - padded
    n_blocks = (t * TOP_K + N_EXPERTS * (EXPERT_ROWS - 1)) // EXPERT_ROWS
    n_rows = n_blocks * EXPERT_ROWS
    dest = p_starts[idx[:TOP_K]] + rank[:TOP_K]
    plane_idx = (dest[:, None, :] + (jnp.arange(2, dtype=I32) * n_rows)[None, :, None]).reshape(TOP_K, 2 * t)
    blk_start = jnp.arange(n_blocks, dtype=I32) * EXPERT_ROWS
    block_e = jnp.minimum(jnp.sum((p_ends[None, :] <= blk_start[:, None]).astype(I32), axis=1), N_EXPERTS - 1)
    n_valid = jnp.clip((p_starts + counts)[block_e] - blk_start, 0, EXPERT_ROWS).astype(I32)
    dq = d // 4
    xs = _sc_scatter_rows(h2p.reshape(2 * t, dq), plane_idx, 2 * n_rows).reshape(2, n_rows, dq)
    ys = _experts(xs, block_e, n_valid, lp['expert_stacks'], lp['layer'])
    g = _sc_gather_rows(ys.reshape(2 * n_rows, dq), plane_idx).reshape(TOP_K, 2, t, dq)
    return _combine(g, wts.T, h2p, x1, modtab, lp, g_final, n_lat_tiles, tiles_used, final)


def _layer(xa, c, c_ctx, lp, consts, g_final, seq, final):
    b, nt, d = xa.shape
    n_lat_tiles = seq // TM
    rope, mats_lat, mats_ctx = consts
    modtab = _mod_table(c, c_ctx, *lp['mod_stacks'], lp['layer'])
    g1 = lp['g_norm1'].reshape(1, -1)
    g2 = lp['g_norm2'].reshape(1, -1)
    pw = _proj_weights(lp)
    q, k, v, nq, nk, nv, hy, lu, lg = _project(xa, modtab, g1, pw, rope, n_lat_tiles)
    br_a = _mla_attention(q, k, v, seq)
    br_b = _na_attention(nq, nk, nv, _na_bias_tables(lp['na_rpb'], seq // GRID_W), seq)
    pre = _hy_pre(hy, lp['hy_short_w'], lp['hy_short_b'], seq)
    lat_f, lat_bf, ctx_f, ctx_bf = pre
    c_lat = _hyena_seq(mats_lat, lat_f, lat_bf, lp, b)
    c_ctx_out = _hyena_seq(mats_ctx, ctx_f, ctx_bf, lp, b)
    br_c = jnp.concatenate([c_lat, c_ctx_out], axis=1)
    br_d = _lru_mixer(lu, lg, lp, seq)
    x1 = _merge(xa, modtab, g1, (br_a, br_b, br_c, br_d), pw['w_gate'], lp, n_lat_tiles)
    return _moe(x1, modtab, g2, lp, g_final, n_lat_tiles, final)


_LAYER_KEYS = ('w_mod', 'b_mod', 'g_norm1', 'g_norm2', 'w_in', 'mla_g_q', 'mla_w_uq', 'mla_g_kv', 'mla_w_ukv',
               'na_rpb', 'hy_short_w', 'hy_short_b', 'hy_w1', 'hy_b1', 'hy_w2', 'hy_b2', 'hy_w3', 'hy_decay',
               'hy_bias', 'lru_conv_w', 'lru_conv_b', 'lru_wa', 'lru_ba', 'lru_wx', 'lru_bx', 'lru_lambda',
               'w_branch', 'w_out', 'router_w', 'router_bias', 'exp_w_gate', 'exp_w_up', 'exp_w_down',
               'sh_w_gate', 'sh_w_up', 'sh_w_down')


def kernel(x, c, ctx, c_ctx, w_mod, b_mod, g_norm1, g_norm2, w_in, mla_g_q, mla_w_uq, mla_g_kv, mla_w_ukv, na_rpb, hy_short_w, hy_short_b, hy_w1, hy_b1, hy_w2, hy_b2, hy_w3, hy_decay, hy_bias, lru_conv_w, lru_conv_b, lru_wa, lru_ba, lru_wx, lru_bx, lru_lambda, w_branch, w_out, router_w, router_bias, exp_w_gate, exp_w_up, exp_w_down, sh_w_gate, sh_w_up, sh_w_down, g_final):
    stacked = dict(zip(_LAYER_KEYS, (w_mod, b_mod, g_norm1, g_norm2, w_in, mla_g_q, mla_w_uq, mla_g_kv, mla_w_ukv,
                                     na_rpb, hy_short_w, hy_short_b, hy_w1, hy_b1, hy_w2, hy_b2, hy_w3, hy_decay,
                                     hy_bias, lru_conv_w, lru_conv_b, lru_wa, lru_ba, lru_wx, lru_bx, lru_lambda,
                                     w_branch, w_out, router_w, router_bias, exp_w_gate, exp_w_up, exp_w_down,
                                     sh_w_gate, sh_w_up, sh_w_down)))
    b, seq, d = x.shape
    ctx_len = ctx.shape[1]
    depth = w_mod.shape[0]
    assert seq % TM == 0 and ctx_len % TM == 0 and seq // GRID_W >= NA_KEY_ROWS + 1
    xa = jnp.concatenate([x, ctx], axis=1)
    consts = (_rope_tables(seq, seq + ctx_len), _dft_matrices(seq), _dft_matrices(ctx_len))
    for i in range(depth):
        big = ('w_mod', 'b_mod', 'exp_w_gate', 'exp_w_up', 'exp_w_down')
        lp = {name: w[i] for name, w in stacked.items() if name not in big}
        lp['layer'] = i
        lp['mod_stacks'] = (w_mod, b_mod)
        lp['expert_stacks'] = (exp_w_gate, exp_w_up, exp_w_down)
        xa = _layer(xa, c, c_ctx, lp, consts, g_final, seq, i == depth - 1)
    return xa
```

```python
import functools
import math

import numpy as np
import jax
import jax.numpy as jnp
from jax import lax
from jax.experimental import pallas as pl
from jax.experimental.pallas import tpu as pltpu
from jax.experimental.pallas import tpu_sc as plsc

F32 = jnp.float32
BF16 = jnp.bfloat16
I32 = jnp.int32

TM = 256
LANE = 128
GRID_W = 64
N_MOD = 6
NORM_EPS = 1e-6

MLA_HEADS, MLA_NOPE, MLA_ROPE, MLA_V = 4, 64, 32, 64
MLA_Q_RANK, MLA_KV_RANK = 192, 128
MLA_Q_PAD = 256
ROPE_THETA = 10000.0

NA_HEADS, NA_HEAD_DIM, NA_WIN_R, NA_WIN_C = 4, 64, 8, 16
NA_TILE_ROWS = TM // GRID_W
NA_KEY_ROWS = NA_TILE_ROWS + NA_WIN_R - 1
NA_KEYS = NA_KEY_ROWS * GRID_W

HY_WIDTH, HY_ORDER, HY_SHORT, HY_BANDS, HY_FFN = 256, 2, 3, 16, 64
HY_EMB = 2 * HY_BANDS + 1

LRU_WIDTH, LRU_BLOCKS, LRU_CONV, LRU_C = 256, 4, 4, 8.0
LRU_CHUNK = 256
LRU_HALO = 8

N_EXPERTS, TOP_K, EXPERT_HIDDEN, ROUTED_SCALE, MOE_BLOCK = 64, 6, 256, 2.5, 256
TOPK_PAD = 8

VMEM_LIMIT = 52 * 1024 * 1024


def _cparams(*sem):
    return pltpu.CompilerParams(dimension_semantics=sem, vmem_limit_bytes=VMEM_LIMIT)


def _dot(a, b):
    return jnp.dot(a, b, preferred_element_type=F32)


def _dot_nt(a, b):
    return lax.dot_general(a, b, (((1,), (1,)), ((), ())), preferred_element_type=F32)


def _sigmoid(x):
    return 1.0 / (1.0 + jnp.exp(-x))


def _silu(x):
    return x * _sigmoid(x)


def _gelu_tanh(x):
    return 0.5 * x * (1.0 + jnp.tanh(math.sqrt(2.0 / math.pi) * (x + 0.044715 * (x * x * x))))


def _normmod(x, g, scale, shift):
    y = x * lax.rsqrt(jnp.mean(x * x, axis=-1, keepdims=True) + NORM_EPS) * g
    return y * (1.0 + scale) + shift


def _mod_kernel(c_ref, w_ref, b_ref, o_ref):
    s = _silu(c_ref[...])
    o_ref[...] = _dot(s.astype(BF16), w_ref[...].astype(BF16)) + b_ref[...]


def _mod_table(c, c_ctx, w_mod, b_mod, layer):
    b, d = c.shape
    rows = 16
    cc = jnp.zeros((rows, d), F32).at[:b].set(c).at[b].set(c_ctx)
    tn = 1024
    mod = pl.pallas_call(
        _mod_kernel,
        grid=(N_MOD * d // tn,),
        in_specs=[pl.BlockSpec((rows, d), lambda j: (0, 0)),
                  pl.BlockSpec((None, d, tn), lambda j: (layer, 0, j)),
                  pl.BlockSpec((None, 1, tn), lambda j: (layer, 0, j))],
        out_specs=pl.BlockSpec((rows, tn), lambda j: (0, j)),
        out_shape=jax.ShapeDtypeStruct((rows, N_MOD * d), F32),
        compiler_params=_cparams("arbitrary"),
        name="mod_vectors",
    )(cc, w_mod, b_mod[:, None, :])
    lat = mod[:b].reshape(b, N_MOD, d)
    ctx = jnp.broadcast_to(mod[b].reshape(1, N_MOD, d), (b, N_MOD, d))
    tab = jnp.stack([lat, ctx], axis=1)
    return jnp.pad(tab, ((0, 0), (0, 0), (0, 8 - N_MOD), (0, 0)))


_C_QLAT, _C_KVLAT, _C_KR, _C_KRR, _C_NA, _C_HY, _C_LU, _C_LG, _C_END = (
    0, 256, 384, 512, 640, 640 + 3 * NA_HEADS * LANE, 640 + 1536 + 768, 640 + 1536 + 1024, 640 + 1536 + 1280)


def _proj_kernel(x_ref, mod_ref, g1_ref, w1_ref, gq_ref, gkv_ref, wuq_ref, wuqr_ref, wk_ref, wv_ref,
                 cos_ref, sin_ref, q_o, k_o, v_o, nq_o, nk_o, nv_o, hy_o, lu_o, lg_o):
    m = mod_ref[0, 0]
    h = _normmod(x_ref[0], g1_ref[...], m[1:2], m[0:1])
    z = _dot(h.astype(BF16), w1_ref[...])
    qlat = z[:, _C_QLAT:_C_KVLAT]
    kvlat = z[:, _C_KVLAT:_C_KR]
    kr = z[:, _C_KR:_C_KRR]
    krr = z[:, _C_KRR:_C_NA]
    qn = qlat * lax.rsqrt(jnp.sum(qlat * qlat, axis=-1, keepdims=True) * (1.0 / MLA_Q_RANK) + NORM_EPS) * gq_ref[...]
    kvn = kvlat * lax.rsqrt(jnp.mean(kvlat * kvlat, axis=-1, keepdims=True) + NORM_EPS) * gkv_ref[...]
    qn = qn.astype(BF16)
    kvn = kvn.astype(BF16)
    q = _dot(qn, wuq_ref[...])
    qr = _dot(qn, wuqr_ref[...])
    kk = _dot(kvn, wk_ref[...])
    vv = _dot(kvn, wv_ref[...])
    cos = cos_ref[...]
    sin = sin_ref[...]
    krope = kr * cos + krr * sin
    den_lane = lax.broadcasted_iota(I32, cos.shape, 1) == SOFTMAX_DEN_LANE
    for hd in range(MLA_HEADS):
        sl = slice(hd * LANE, (hd + 1) * LANE)
        q_o[0, hd] = (q[:, sl] * cos + qr[:, sl] * sin).astype(BF16)
        k_o[0, hd] = (kk[:, sl] + krope).astype(BF16)
        v_o[0, hd] = jnp.where(den_lane, 1.0, vv[:, sl]).astype(BF16)
    for hd in range(NA_HEADS):
        for which, ref in enumerate((nq_o, nk_o, nv_o)):
            lo = _C_NA + (which * NA_HEADS + hd) * LANE
            blk = z[:, lo:lo + LANE]
            ref[0, hd] = (jnp.where(den_lane, 1.0, blk) if which == 2 else blk).astype(BF16)
    hy_o[0] = z[:, _C_HY:_C_LU]
    lu_o[0] = z[:, _C_LU:_C_LG]
    lg_o[0] = z[:, _C_LG:_C_END]


def _pad_to(a, n, axis):
    pad = [(0, 0)] * a.ndim
    pad[axis] = (0, n - a.shape[axis])
    return jnp.pad(a, pad)


def _rot_cols(w):
    half = w.shape[-1] // 2
    return jnp.concatenate([-w[..., half:], w[..., :half]], axis=-1)


def _head_blocks(cols_per_head):
    out = []
    for pieces in cols_per_head:
        k = pieces[0][0].shape[0]
        blk = jnp.zeros((k, LANE), F32)
        for arr, off in pieces:
            blk = blk.at[:, off:off + arr.shape[1]].set(arr)
        out.append(blk)
    return jnp.concatenate(out, axis=1)


def _proj_weights(lp):
    w_in = lp['w_in']
    d = w_in.shape[0]
    o = 0
    parts = {}
    for name, n in (('q', MLA_Q_RANK), ('kv', MLA_KV_RANK), ('kr', MLA_ROPE), ('na', 3 * NA_HEADS * NA_HEAD_DIM),
                    ('hy', 3 * HY_WIDTH), ('lu', LRU_WIDTH), ('lg', LRU_WIDTH), ('gt', 4 * d)):
        parts[name] = w_in[:, o:o + n]
        o += n
    zeros = lambda n: jnp.zeros((d, n), F32)
    kr_blk = jnp.concatenate([zeros(MLA_NOPE), parts['kr'], zeros(LANE - MLA_NOPE - MLA_ROPE)], axis=1)
    krr_blk = jnp.concatenate([zeros(MLA_NOPE), _rot_cols(parts['kr']), zeros(LANE - MLA_NOPE - MLA_ROPE)], axis=1)
    na_scale = NA_HEAD_DIM ** -0.5
    na_cols = []
    for which in range(3):
        for hd in range(NA_HEADS):
            lo = (which * NA_HEADS + hd) * NA_HEAD_DIM
            blk = parts['na'][:, lo:lo + NA_HEAD_DIM] * (na_scale if which == 0 else 1.0)
            na_cols.append(_pad_to(blk, LANE, 1))
    w1 = jnp.concatenate([_pad_to(parts['q'], MLA_Q_PAD, 1), parts['kv'], kr_blk, krr_blk] + na_cols
                         + [parts['hy'], parts['lu'], parts['lg']], axis=1).astype(BF16)
    mla_scale = (MLA_NOPE + MLA_ROPE) ** -0.5
    wuq = _pad_to(lp['mla_w_uq'], MLA_Q_PAD, 0) * mla_scale
    dq = MLA_NOPE + MLA_ROPE
    wuq_main = _head_blocks([[(wuq[:, hd * dq:hd * dq + dq], 0)] for hd in range(MLA_HEADS)])
    wuq_rot = _head_blocks([[(_rot_cols(wuq[:, hd * dq + MLA_NOPE:hd * dq + dq]), MLA_NOPE)]
                            for hd in range(MLA_HEADS)])
    dkv = MLA_NOPE + MLA_V
    wukv = lp['mla_w_ukv']
    wk = _head_blocks([[(wukv[:, hd * dkv:hd * dkv + MLA_NOPE], 0)] for hd in range(MLA_HEADS)])
    wv = _head_blocks([[(wukv[:, hd * dkv + MLA_NOPE:hd * dkv + dkv], 0)] for hd in range(MLA_HEADS)])
    gq = _pad_to(lp['mla_g_q'].reshape(1, -1), MLA_Q_PAD, 1)
    gkv = lp['mla_g_kv'].reshape(1, -1)
    return dict(w1=w1, w_gate=parts['gt'].astype(BF16), gq=gq, gkv=gkv, wuq=wuq_main.astype(BF16),
                wuq_rot=wuq_rot.astype(BF16), wk=wk.astype(BF16), wv=wv.astype(BF16))


def _rope_tables(seq, n_tok):
    t = jnp.arange(seq, dtype=I32)
    row = (t // GRID_W).astype(F32)
    col = (t % GRID_W).astype(F32)
    n_axis = MLA_ROPE // 4
    inv_freq = ROPE_THETA ** (-jnp.arange(n_axis, dtype=F32) / n_axis)
    ang = jnp.concatenate([row[:, None] * inv_freq, col[:, None] * inv_freq], axis=-1)
    cos = jnp.concatenate([jnp.cos(ang), jnp.cos(ang)], axis=-1)
    sin = jnp.concatenate([jnp.sin(ang), jnp.sin(ang)], axis=-1)
    cos_t = jnp.ones((n_tok, LANE), F32).at[:seq, MLA_NOPE:MLA_NOPE + MLA_ROPE].set(cos)
    sin_t = jnp.zeros((n_tok, LANE), F32).at[:seq, MLA_NOPE:MLA_NOPE + MLA_ROPE].set(sin)
    return cos_t, sin_t


def _kind_map(n_lat_tiles):
    return lambda b, i: (b, jnp.where(i >= n_lat_tiles, 1, 0), 0, 0)


def _project(xa, modtab, g1, pw, rope, n_lat_tiles):
    b, nt, d = xa.shape
    cos_t, sin_t = rope
    full = lambda a: pl.BlockSpec(a.shape, lambda bb, i: (0,) * a.ndim)
    head_out = lambda: pl.BlockSpec((1, MLA_HEADS, TM, LANE), lambda bb, i: (bb, 0, i, 0))
    tok_out = lambda n: pl.BlockSpec((1, TM, n), lambda bb, i: (bb, i, 0))
    head_shape = jax.ShapeDtypeStruct((b, MLA_HEADS, nt, LANE), BF16)
    tok_shape = lambda n: jax.ShapeDtypeStruct((b, nt, n), F32)
    weights = (g1, pw['w1'], pw['gq'], pw['gkv'], pw['wuq'], pw['wuq_rot'], pw['wk'], pw['wv'])
    return pl.pallas_call(
        _proj_kernel,
        grid=(b, nt // TM),
        in_specs=[pl.BlockSpec((1, TM, d), lambda bb, i: (bb, i, 0)),
                  pl.BlockSpec((1, 1, 8, d), _kind_map(n_lat_tiles))]
                 + [full(w) for w in weights]
                 + [pl.BlockSpec((TM, LANE), lambda bb, i: (i, 0))] * 2,
        out_specs=[head_out() for _ in range(6)] + [tok_out(3 * HY_WIDTH), tok_out(LRU_WIDTH), tok_out(LRU_WIDTH)],
        out_shape=[head_shape] * 6 + [tok_shape(3 * HY_WIDTH), tok_shape(LRU_WIDTH), tok_shape(LRU_WIDTH)],
        compiler_params=_cparams("parallel", "parallel"),
        name="input_projection",
    )(xa, modtab, *weights, cos_t, sin_t)


SOFTMAX_DEN_LANE = 64


def _softmax_pv(parts):
    m = None
    for s, _ in parts:
        mm = jnp.max(s, axis=-1, keepdims=True)
        m = mm if m is None else jnp.maximum(m, mm)
    acc = None
    for s, v in parts:
        o = _dot(jnp.exp(s - m).astype(BF16), v)
        acc = o if acc is None else acc + o
    return acc / acc[:, SOFTMAX_DEN_LANE:SOFTMAX_DEN_LANE + 1]


def _mla_kernel(q_ref, k_ref, v_ref, o_ref, *, seq, n_lat_tiles):
    i = pl.program_id(1)
    nt = k_ref.shape[2]

    def attend(lo, hi):
        for hd in range(MLA_HEADS):
            s = _dot_nt(q_ref[0, hd], k_ref[0, hd, lo:hi, :])
            o = _softmax_pv([(s, v_ref[0, hd, lo:hi, :])])
            o_ref[0, :, hd * LANE:(hd + 1) * LANE] = o.astype(BF16)

    @pl.when(i < n_lat_tiles)
    def _():
        attend(0, nt)

    @pl.when(i >= n_lat_tiles)
    def _():
        attend(seq, nt)


def _mla_attention(q, k, v, seq, tiles_used):
    b, h, nt, _ = q.shape
    kv_spec = pl.BlockSpec((1, h, nt, LANE), lambda bb, i: (bb, 0, 0, 0))
    return pl.pallas_call(
        functools.partial(_mla_kernel, seq=seq, n_lat_tiles=seq // TM),
        grid=(b, tiles_used),
        in_specs=[pl.BlockSpec((1, h, TM, LANE), lambda bb, i: (bb, 0, i, 0)), kv_spec, kv_spec],
        out_specs=pl.BlockSpec((1, TM, h * LANE), lambda bb, i: (bb, i, 0)),
        out_shape=jax.ShapeDtypeStruct((b, tiles_used * TM, h * LANE), BF16),
        compiler_params=_cparams("parallel", "arbitrary"),
        name="mla_attention",
    )(q, k, v)


def _na_bias_tables(rpb, rows):
    n_blk = rows // NA_TILE_ROWS
    col = np.arange(GRID_W)
    c0 = np.clip(col - NA_WIN_C // 2, 0, GRID_W - NA_WIN_C)
    in_win = (col[None, :] >= c0[:, None]) & (col[None, :] < c0[:, None] + NA_WIN_C)
    dc = np.clip(col[None, :] - col[:, None], 1 - NA_WIN_C, NA_WIN_C - 1) + NA_WIN_C - 1
    rpb = rpb.astype(F32)
    tables = []
    for j in (0, 1, n_blk - 1):
        w0 = min(max(NA_TILE_ROWS * j - NA_WIN_R // 2, 0), rows - NA_KEY_ROWS)
        r = NA_TILE_ROWS * j + np.arange(NA_TILE_ROWS)
        kr = w0 + np.arange(NA_KEY_ROWS)
        r0 = np.clip(r - NA_WIN_R // 2, 0, rows - NA_WIN_R)
        row_ok = (kr[None, :] >= r0[:, None]) & (kr[None, :] < r0[:, None] + NA_WIN_R)
        dr = np.clip(kr[None, :] - r[:, None] + NA_WIN_R - 1, 0, 2 * NA_WIN_R - 2)
        oh_r = jnp.asarray(np.eye(2 * NA_WIN_R - 1, dtype=np.float32)[dr.reshape(-1)])
        oh_c = jnp.asarray(np.eye(2 * NA_WIN_C - 1, dtype=np.float32)[dc.reshape(-1)])
        bias = jnp.einsum('ar,hrc,bc->hab', oh_r, rpb, oh_c, precision=lax.Precision.HIGHEST)
        bias = bias.reshape(NA_HEADS, NA_TILE_ROWS, NA_KEY_ROWS, GRID_W, GRID_W)
        mask = row_ok[:, :, None, None] & in_win[None, None, :, :]
        bias = jnp.where(jnp.asarray(mask)[None], bias, -jnp.inf)
        tables.append(bias.transpose(0, 1, 3, 2, 4).reshape(NA_HEADS, TM, NA_KEYS))
    return jnp.stack(tables)


def _na_kernel(q_ref, k_ref, v_ref, bias_ref, o_ref, *, seq, n_lat_tiles):
    i = pl.program_id(1)
    nt = k_ref.shape[2]
    rows = seq // GRID_W

    @pl.when(i < n_lat_tiles)
    def _():
        w0 = jnp.clip(NA_TILE_ROWS * i - NA_WIN_R // 2, 0, rows - NA_KEY_ROWS)
        start = pl.multiple_of(w0 * GRID_W, GRID_W)
        for hd in range(NA_HEADS):
            q = q_ref[0, hd]
            s_loc = _dot_nt(q, k_ref[0, hd, pl.ds(start, NA_KEYS), :]) + bias_ref[0, hd]
            s_ctx = _dot_nt(q, k_ref[0, hd, seq:nt, :])
            o = _softmax_pv([(s_loc, v_ref[0, hd, pl.ds(start, NA_KEYS), :]), (s_ctx, v_ref[0, hd, seq:nt, :])])
            o_ref[0, :, hd * LANE:(hd + 1) * LANE] = o.astype(BF16)

    @pl.when(i >= n_lat_tiles)
    def _():
        for hd in range(NA_HEADS):
            s = _dot_nt(q_ref[0, hd], k_ref[0, hd, seq:nt, :])
            o = _softmax_pv([(s, v_ref[0, hd, seq:nt, :])])
            o_ref[0, :, hd * LANE:(hd + 1) * LANE] = o.astype(BF16)


def _na_attention(q, k, v, bias, seq, tiles_used):
    b, h, nt, _ = q.shape
    n_lat = seq // TM
    kv_spec = pl.BlockSpec((1, h, nt, LANE), lambda bb, i: (bb, 0, 0, 0))
    cfg = lambda bb, i: (jnp.where(i == 0, 0, jnp.where(i >= n_lat - 1, 2, 1)), 0, 0, 0)
    return pl.pallas_call(
        functools.partial(_na_kernel, seq=seq, n_lat_tiles=n_lat),
        grid=(b, tiles_used),
        in_specs=[pl.BlockSpec((1, h, TM, LANE), lambda bb, i: (bb, 0, i, 0)), kv_spec, kv_spec,
                  pl.BlockSpec((1, h, TM, NA_KEYS), cfg)],
        out_specs=pl.BlockSpec((1, TM, h * LANE), lambda bb, i: (bb, i, 0)),
        out_shape=jax.ShapeDtypeStruct((b, tiles_used * TM, h * LANE), BF16),
        compiler_params=_cparams("parallel", "arbitrary"),
        name="neighbourhood_attention",
    )(q, k, v, bias)


HY_PRE_CHUNK = 256


def _hy_pre_kernel(z_ref, w_ref, b_ref, lat_o, lat_bf_o, ctx_o, ctx_bf_o, *, seq, ctx_len):
    w = w_ref[...]
    bias = b_ref[...]
    ch = HY_PRE_CHUNK
    zero = jnp.zeros((8, HY_WIDTH), F32)
    for seg_lo, seg_len, o_ref, obf_ref in ((0, seq, lat_o, lat_bf_o), (seq, ctx_len, ctx_o, ctx_bf_o)):
        for c in range(seg_len // ch):
            s = seg_lo + c * ch
            before = zero if c == 0 else z_ref[0, s - 8:s, :]
            after = zero if c == seg_len // ch - 1 else z_ref[0, s + ch:s + ch + 8, :]
            win = jnp.concatenate([before, z_ref[0, s:s + ch, :], after], axis=0)
            n = ch + 16
            y = bias + w[1:2] * win[8:8 + ch]
            y = y + w[0:1] * pltpu.roll(win, 1, axis=0)[8:8 + ch]
            y = y + w[2:3] * pltpu.roll(win, n - 1, axis=0)[8:8 + ch]
            o_ref[0, c * ch:(c + 1) * ch, :] = y
            obf_ref[0, c * ch:(c + 1) * ch, :] = y.astype(BF16)


def _hy_pre(hy, w, bvec, seq):
    b, nt, _ = hy.shape
    ctx_len = nt - seq
    wpad = _pad_to(w, 8, 0)
    out_specs, out_shape = [], []
    for length in (seq, ctx_len):
        for dt in (F32, BF16):
            out_specs.append(pl.BlockSpec((1, length, HY_WIDTH), lambda bb, g: (g, 0, bb)))
            out_shape.append(jax.ShapeDtypeStruct((3, length, b * HY_WIDTH), dt))
    return pl.pallas_call(
        functools.partial(_hy_pre_kernel, seq=seq, ctx_len=ctx_len),
        grid=(b, 3),
        in_specs=[pl.BlockSpec((1, nt, HY_WIDTH), lambda bb, g: (bb, 0, g)),
                  pl.BlockSpec((8, HY_WIDTH), lambda bb, g: (0, g)),
                  pl.BlockSpec((1, HY_WIDTH), lambda bb, g: (0, g))],
        out_specs=out_specs,
        out_shape=out_shape,
        compiler_params=_cparams("parallel", "parallel"),
        name="hyena_short_conv",
    )(hy, wpad, bvec.reshape(1, -1))


def _hp_dot(a, b):
    return jnp.dot(a, b, preferred_element_type=F32, precision=lax.Precision.HIGHEST)


def _hy_filter_kernel(f_ref, w1_ref, b1_ref, w2_ref, b2_ref, w3_ref, dec_ref, o_ref, obf_ref):
    f = f_ref[...]
    h = jnp.sin(_hp_dot(f, w1_ref[...]) + b1_ref[...])
    h = jnp.sin(_hp_dot(h, w2_ref[...]) + b2_ref[...])
    h = _hp_dot(h, w3_ref[...])
    h = h * jnp.exp(-f[:, 0:1] * jnp.abs(dec_ref[...]))
    o_ref[...] = h
    obf_ref[...] = h.astype(BF16)


def _hy_pos_features(length):
    t = jnp.linspace(0.0, 1.0, length, dtype=F32)[:, None]
    w = 2.0 * math.pi * jnp.arange(length, dtype=F32)[:, None] / length
    f = jnp.linspace(1e-4, HY_BANDS - 1, HY_BANDS, dtype=F32)[None, :]
    z = w * f
    return jnp.concatenate([t, jnp.cos(z), -jnp.sin(z)], axis=-1)


def _hy_filters(length, lp):
    feats = _hy_pos_features(length)
    n_out = HY_ORDER * 2 * HY_WIDTH
    tl = min(length, 512)
    full = lambda a: pl.BlockSpec(a.shape, lambda i: (0,) * a.ndim)
    args = (lp['hy_w1'], lp['hy_b1'].reshape(1, -1), lp['hy_w2'], lp['hy_b2'].reshape(1, -1), lp['hy_w3'],
            lp['hy_decay'].reshape(1, n_out))
    return pl.pallas_call(
        _hy_filter_kernel,
        grid=(length // tl,),
        in_specs=[pl.BlockSpec((tl, HY_EMB), lambda i: (i, 0))] + [full(a) for a in args],
        out_specs=[pl.BlockSpec((tl, n_out), lambda i: (i, 0))] * 2,
        out_shape=[jax.ShapeDtypeStruct((length, n_out), F32), jax.ShapeDtypeStruct((length, n_out), BF16)],
        compiler_params=_cparams("parallel"),
        name="hyena_filter_mlp",
    )(feats, *args)


def _dft_matrices(length):
    n2 = 2 * length
    step = DFT_ROW_STEP
    n = jnp.arange(length, dtype=I32)

    def trig(kv):
        ang = ((kv[:, None] * n[None, :]) % n2).astype(F32) * (2.0 * math.pi / n2)
        return jnp.cos(ang), jnp.sin(ang)

    c1, s1 = trig(jnp.arange(length // step, dtype=I32) * step)
    c2, s2 = trig(jnp.arange(step, dtype=I32))
    coarse = pl.BlockSpec((1, 1, length), lambda j: (j, 0, 0))
    fine = pl.BlockSpec((step, length), lambda j: (0, 0))
    out = pl.BlockSpec((step, length), lambda j: (j, 0))
    return pl.pallas_call(
        _dft_matrix_kernel,
        grid=(length // step,),
        in_specs=[coarse, coarse, fine, fine],
        out_specs=[out, out, out],
        out_shape=[jax.ShapeDtypeStruct((length, length), BF16)] * 3,
        compiler_params=_cparams("parallel"),
        name="dft_matrices",
    )(c1[:, None, :], s1[:, None, :], c2, s2)


DFT_ROW_STEP = 64


def _dft_matrix_kernel(c1_ref, s1_ref, c2_ref, s2_ref, mc_o, ms_o, mst_o):
    c1, s1 = c1_ref[0], s1_ref[0]
    c2, s2 = c2_ref[...], s2_ref[...]
    cos = c1 * c2 - s1 * s2
    neg_sin = -(s1 * c2 + c1 * s2)
    row = lax.broadcasted_iota(I32, cos.shape, 0)
    col = lax.broadcasted_iota(I32, cos.shape, 1)
    first_row = (row + pl.program_id(0) * DFT_ROW_STEP) == 0
    mc_o[...] = cos.astype(BF16)
    ms_o[...] = jnp.where(first_row, jnp.where(col % 2 == 0, 1.0, -1.0), neg_sin).astype(BF16)
    mst_o[...] = jnp.where(col == 0, jnp.where(row % 2 == 0, 1.0, -1.0), neg_sin).astype(BF16)


def _dft_fwd_kernel(mc_ref, ms_ref, x_ref, *rest, with_taps):
    ure = _dot(mc_ref[...], x_ref[...])
    uim = _dot(ms_ref[...], x_ref[...])
    if not with_taps:
        ure_o, uim_o = rest
        ure_o[...] = ure
        uim_o[...] = uim
        return
    a_ref, b_ref, c_ref, d_ref, zre_o, zim_o = rest
    a, b, c, d = a_ref[...], b_ref[...], c_ref[...], d_ref[...]
    for bb in range(x_ref.shape[1] // HY_WIDTH):
        sl = slice(bb * HY_WIDTH, (bb + 1) * HY_WIDTH)
        zre_o[:, sl] = (ure[:, sl] * a - uim[:, sl] * b).astype(BF16)
        zim_o[:, sl] = (ure[:, sl] * c + uim[:, sl] * d).astype(BF16)


def _col_block(nc, cap):
    return min(nc, cap)


def _dft_fwd(mats, x, plane, taps=None):
    mc, ms, _ = mats
    _, length, nc = x.shape
    tk = min(length, 256)
    cb = _col_block(nc, 1024)
    grid = (nc // cb, length // tk)
    m_spec = pl.BlockSpec((tk, length), lambda c, j: (j, 0))
    x_spec = pl.BlockSpec((None, length, cb), lambda c, j: (plane, 0, c))
    o_spec = pl.BlockSpec((tk, cb), lambda c, j: (j, c))
    if taps is None:
        return pl.pallas_call(
            functools.partial(_dft_fwd_kernel, with_taps=False),
            grid=grid, in_specs=[m_spec, m_spec, x_spec], out_specs=[o_spec, o_spec],
            out_shape=[jax.ShapeDtypeStruct((length, nc), F32)] * 2,
            compiler_params=_cparams("parallel", "arbitrary"),
            name="hyena_dft_filters",
        )(mc, ms, x)
    t_spec = pl.BlockSpec((tk, HY_WIDTH), lambda c, j: (j, 0))
    return pl.pallas_call(
        functools.partial(_dft_fwd_kernel, with_taps=True),
        grid=grid, in_specs=[m_spec, m_spec, x_spec] + [t_spec] * 4, out_specs=[o_spec, o_spec],
        out_shape=[jax.ShapeDtypeStruct((length, nc), BF16)] * 2,
        compiler_params=_cparams("parallel", "arbitrary"),
        name="hyena_dft_forward",
    )(mc, ms, x, *taps)


def _dft_inv_kernel(mc_ref, mst_ref, zre_ref, zim_ref, gate_ref, prev_ref, bias_ref, *outs, last):
    conv = _dot(mc_ref[...], zre_ref[...]) + _dot(mst_ref[...], zim_ref[...])
    y = gate_ref[...] * (conv + prev_ref[...] * bias_ref[...])
    if last:
        (tok_o,) = outs
        for bb in range(y.shape[1] // HY_WIDTH):
            tok_o[bb] = y[:, bb * HY_WIDTH:(bb + 1) * HY_WIDTH].astype(BF16)
    else:
        y_o, ybf_o = outs
        y_o[...] = y
        ybf_o[...] = y.astype(BF16)


def _dft_inv(mats, zre, zim, gate, gate_plane, prev, prev_plane, bias_row, last):
    mc, _, mst = mats
    length, nc = zre.shape
    tm = min(length, 256)
    cb = _col_block(nc, 512)
    grid = (nc // cb, length // tm)
    m_spec = pl.BlockSpec((tm, length), lambda c, i: (i, 0))
    z_spec = pl.BlockSpec((length, cb), lambda c, i: (0, c))
    e_spec = lambda plane: pl.BlockSpec((None, tm, cb), lambda c, i: (plane, i, c))
    b_spec = pl.BlockSpec((1, cb), lambda c, i: (0, c))
    if last:
        nb = nc // HY_WIDTH
        out_specs = [pl.BlockSpec((cb // HY_WIDTH, tm, HY_WIDTH), lambda c, i: (c, i, 0))]
        out_shape = [jax.ShapeDtypeStruct((nb, length, HY_WIDTH), BF16)]
    else:
        out_specs = [e_spec(0), e_spec(0)]
        out_shape = [jax.ShapeDtypeStruct((1, length, nc), F32), jax.ShapeDtypeStruct((1, length, nc), BF16)]
    return pl.pallas_call(
        functools.partial(_dft_inv_kernel, last=last),
        grid=grid, in_specs=[m_spec, m_spec, z_spec, z_spec, e_spec(gate_plane), e_spec(prev_plane), b_spec],
        out_specs=out_specs, out_shape=out_shape,
        compiler_params=_cparams("parallel", "arbitrary"),
        name="hyena_dft_inverse",
    )(mc, mst, zre, zim, gate, prev, bias_row)


def _hy_tap_tables(ure, uim, filt, length):
    n2 = 2 * length
    w = HY_WIDTH
    scale = jnp.full((length, 1), 2.0 / n2, F32).at[0, 0].set(1.0 / n2)
    tables = []
    for o in range(HY_ORDER):
        f_sl = slice((2 * o) * w, (2 * o + 1) * w)
        b_sl = slice((2 * o + 1) * w, (2 * o + 2) * w)
        hb0 = filt[0:1, b_sl]
        tre = ure[:, f_sl] + ure[:, b_sl] - hb0
        tim = uim[:, f_sl] - uim[:, b_sl]
        t_nyq = uim[0:1, f_sl] + uim[0:1, b_sl] - hb0
        first = (jnp.arange(length) == 0)[:, None]
        a = tre * scale
        bm = jnp.where(first, 0.0, tim * scale)
        c = bm
        d = jnp.where(first, t_nyq * scale, tre * scale)
        tables.append((a, bm, c, d))
    return tables


def _hyena_seq(mats, vx, vx_bf, lp, n_batch):
    length = vx.shape[1]
    filt, filt_bf = _hy_filters(length, lp)
    ure, uim = _dft_fwd(mats, filt_bf[None], 0)
    tables = _hy_tap_tables(ure, uim, filt, length)
    bias = lp['hy_bias'].astype(F32)
    y, y_bf = vx, vx_bf
    for o in range(HY_ORDER):
        zre, zim = _dft_fwd(mats, y_bf, 0, tables[o])
        bias_row = jnp.tile(bias[o][None, :], (1, n_batch))
        last = o == HY_ORDER - 1
        res = _dft_inv(mats, zre, zim, vx, o + 1, y, 0, bias_row, last)
        if last:
            return res[0]
        y, y_bf = res


def _lru_kernel(u_ref, g_ref, cw_ref, cb_ref, wa_ref, ba_ref, wx_ref, bx_ref, lam_ref, o_ref,
                pad_ref, y_ref, *, seq, ctx_len):
    tc = LRU_CHUNK
    halo = LRU_HALO
    width = LRU_WIDTH
    lat_off = halo
    ctx_off = 2 * halo + seq
    zero = jnp.zeros((halo, width), F32)
    pad_ref[0:halo, :] = zero
    pad_ref[lat_off:lat_off + seq, :] = u_ref[0, 0:seq, :]
    pad_ref[lat_off + seq:ctx_off, :] = zero
    pad_ref[ctx_off:ctx_off + ctx_len, :] = u_ref[0, seq:seq + ctx_len, :]
    pad_ref[ctx_off + ctx_len:ctx_off + ctx_len + halo, :] = zero
    row = lax.broadcasted_iota(I32, (tc, width), 0)
    n_win = tc + 2 * halo

    def chunk(pad_off, y_off, s, carry, d):
        wstart = pl.multiple_of(pad_off + s - halo, 8)
        win = pad_ref[pl.ds(wstart, n_win), :]
        cw = cw_ref[d]
        xc = cb_ref[d]
        for k in range(LRU_CONV):
            shift = (LRU_CONV - 1 - k) if d == 0 else -k
            rolled = win if shift == 0 else pltpu.roll(win, shift % n_win, axis=0)
            xc = xc + cw[k:k + 1] * rolled[halo:halo + tc]
        xb = xc.astype(BF16)
        r = _sigmoid(_dot(xb, wa_ref[d]) + ba_ref[d])
        gi = _sigmoid(_dot(xb, wx_ref[d]) + bx_ref[d])
        lam = lam_ref[d]
        softplus = jnp.maximum(-lam, 0.0) + jnp.log1p(jnp.exp(-jnp.abs(lam)))
        log_a = -LRU_C * r * softplus
        a = jnp.exp(log_a)
        bt = jnp.sqrt(-jnp.tanh(log_a) * (a * a + 1.0)) * (gi * xc)
        sft = 1
        while sft < tc:
            if d == 0:
                keep = row >= sft
                a_s = jnp.where(keep, pltpu.roll(a, sft, axis=0), 1.0)
                b_s = jnp.where(keep, pltpu.roll(bt, sft, axis=0), 0.0)
            else:
                keep = row < tc - sft
                a_s = jnp.where(keep, pltpu.roll(a, tc - sft, axis=0), 1.0)
                b_s = jnp.where(keep, pltpu.roll(bt, tc - sft, axis=0), 0.0)
            bt = a * b_s + bt
            a = a * a_s
            sft *= 2
        h = a * carry + bt
        yo = pl.multiple_of(y_off + s, 8)
        if d == 0:
            y_ref[pl.ds(yo, tc), :] = h
            return h[tc - 1:tc]
        y_ref[pl.ds(yo, tc), :] = y_ref[pl.ds(yo, tc), :] + h
        return h[0:1]

    n_lat = seq // tc
    n_ctx = ctx_len // tc
    for d in range(2):
        carry = jnp.zeros((1, width), F32)
        order = range(n_ctx) if d == 0 else range(n_ctx - 1, -1, -1)
        for c in order:
            carry = chunk(ctx_off, seq, c * tc, carry, d)

        def body(j, cr, d=d):
            jj = j if d == 0 else n_lat - 1 - j
            return chunk(lat_off, 0, jj * tc, cr, d)

        lax.fori_loop(0, n_lat, body, carry)
    o_ref[0] = (y_ref[...] * _gelu_tanh(g_ref[0])).astype(BF16)


def _block_diag(w):
    nd, nb, c, _ = w.shape
    out = jnp.zeros((nd, nb * c, nb * c), w.dtype)
    for n in range(nb):
        out = out.at[:, n * c:(n + 1) * c, n * c:(n + 1) * c].set(w[:, n])
    return out


def _lru_mixer(lu, lg, lp, seq):
    b, nt, w = lu.shape
    ctx_len = nt - seq
    row3 = lambda a: a.reshape(2, 1, w)
    args = (_pad_to(lp['lru_conv_w'], 8, 1), row3(lp['lru_conv_b']), _block_diag(lp['lru_wa']).astype(BF16),
            row3(lp['lru_ba']), _block_diag(lp['lru_wx']).astype(BF16), row3(lp['lru_bx']), row3(lp['lru_lambda']))
    full = lambda a: pl.BlockSpec(a.shape, lambda bb: (0,) * a.ndim)
    tok = pl.BlockSpec((1, nt, w), lambda bb: (bb, 0, 0))
    return pl.pallas_call(
        functools.partial(_lru_kernel, seq=seq, ctx_len=ctx_len),
        grid=(b,),
        in_specs=[tok, tok] + [full(a) for a in args],
        out_specs=tok,
        out_shape=jax.ShapeDtypeStruct((b, nt, w), BF16),
        scratch_shapes=[pltpu.VMEM((nt + 3 * LRU_HALO, w), F32), pltpu.VMEM((nt, w), F32)],
        compiler_params=_cparams("parallel"),
        name="rglru_scan",
    )(lu, lg, *args)


def _merge_kernel(x_ref, mod_ref, g1_ref, a_ref, b_ref, c_ref, d_ref, wg_ref, wa_ref, wb_ref, wc_ref, wd_ref,
                  wo_ref, o_ref):
    m = mod_ref[0, 0]
    x = x_ref[0]
    h = _normmod(x, g1_ref[...], m[1:2], m[0:1]).astype(BF16)
    dm = x.shape[-1]
    acc = None
    for k, (br, w) in enumerate(((a_ref, wa_ref), (b_ref, wb_ref), (c_ref, wc_ref), (d_ref, wd_ref))):
        gate = _sigmoid(_dot(h, wg_ref[:, k * dm:(k + 1) * dm]))
        term = gate * _dot(br[0], w[...])
        acc = term if acc is None else acc + term
    y = _dot(acc.astype(BF16), wo_ref[...])
    o_ref[0] = x + m[2:3] * y


def _merge(xa, modtab, g1, branches, w_gate, lp, n_lat_tiles, tiles_used):
    b, _, d = xa.shape
    wbr = lp['w_branch']
    head_rows = lambda w, dv: jnp.concatenate(
        [_pad_to(w[hd * dv:(hd + 1) * dv], LANE, 0) for hd in range(4)], axis=0)
    weights = (head_rows(wbr[0], MLA_V).astype(BF16), head_rows(wbr[1], NA_HEAD_DIM).astype(BF16),
               wbr[2].astype(BF16), wbr[3].astype(BF16), lp['w_out'].astype(BF16))
    full = lambda a: pl.BlockSpec(a.shape, lambda bb, i: (0,) * a.ndim)
    tok = lambda n: pl.BlockSpec((1, TM, n), lambda bb, i: (bb, i, 0))
    return pl.pallas_call(
        _merge_kernel,
        grid=(b, tiles_used),
        in_specs=[tok(d), pl.BlockSpec((1, 1, 8, d), _kind_map(n_lat_tiles)), full(g1)]
                 + [tok(br.shape[-1]) for br in branches] + [full(w_gate)] + [full(w) for w in weights],
        out_specs=tok(d),
        out_shape=jax.ShapeDtypeStruct((b, tiles_used * TM, d), F32),
        compiler_params=_cparams("parallel", "parallel"),
        name="merge_branches",
    )(xa, modtab, g1, *branches, w_gate, *weights)


U32 = jnp.uint32
EXPERT_ROWS = 512


def _pack_pair(x):
    n = x.shape[-1] // 2
    hi = lax.bitcast_convert_type(x[:, :n].astype(BF16).astype(F32), U32)
    lo = lax.bitcast_convert_type(x[:, n:].astype(BF16).astype(F32), U32)
    return hi | (lo >> 16)


def _unpack_pair(p):
    hi = lax.bitcast_convert_type(p & jnp.uint32(0xFFFF0000), F32)
    lo = lax.bitcast_convert_type(p << 16, F32)
    return hi, lo


def _router_kernel(x_ref, mod_ref, g2_ref, rw_ref, rb_ref, tri_ref, h2_o, idx_o, wts_o, rank_o, cnt_o, carry):
    i = pl.program_id(0)

    @pl.when(i == 0)
    def _():
        carry[...] = jnp.zeros_like(carry)

    m = mod_ref[0, 0]
    h2 = _normmod(x_ref[0], g2_ref[...], m[4:5], m[3:4])
    half = h2.shape[-1] // 2
    h2_o[0] = _pack_pair(h2[:, :half])
    h2_o[1] = _pack_pair(h2[:, half:])
    logits = lax.dot_general(rw_ref[...], h2, (((1,), (1,)), ((), ())), preferred_element_type=F32,
                             precision=lax.Precision.HIGHEST)
    scores = _sigmoid(logits)
    biased = scores + rb_ref[...]
    expert = lax.broadcasted_iota(I32, scores.shape, 0)
    picks = []
    onehot_all = jnp.zeros(scores.shape, F32)
    for _ in range(TOP_K):
        best = jnp.max(biased, axis=0, keepdims=True)
        arg = jnp.min(jnp.where(biased == best, expert, N_EXPERTS), axis=0, keepdims=True)
        hit = expert == arg
        sel = jnp.sum(jnp.where(hit, scores, 0.0), axis=0, keepdims=True)
        biased = jnp.where(hit, -jnp.inf, biased)
        onehot_all = onehot_all + jnp.where(hit, 1.0, 0.0)
        picks.append((arg, hit, sel))
    total = picks[0][2]
    for _, _, sel in picks[1:]:
        total = total + sel
    earlier = _dot(onehot_all.astype(BF16), tri_ref[...]) + carry[...]
    pad_rows = TOPK_PAD - TOP_K
    ranks = [jnp.sum(jnp.where(hit, earlier, 0.0), axis=0, keepdims=True).astype(I32) for _, hit, _ in picks]
    scale = ROUTED_SCALE / total
    idx_o[...] = jnp.concatenate([arg for arg, _, _ in picks] + [jnp.zeros((pad_rows, TM), I32)], axis=0)
    wts_o[...] = jnp.concatenate([sel * scale for _, _, sel in picks] + [jnp.zeros((pad_rows, TM), F32)], axis=0)
    rank_o[...] = jnp.concatenate(ranks + [jnp.zeros((pad_rows, TM), I32)], axis=0)
    carry[...] = carry[...] + jnp.sum(onehot_all, axis=1, keepdims=True)
    cnt_o[...] = carry[...]


def _tile_maps(tiles_used, n_lat_tiles):
    tok = lambda i: (i // tiles_used, i % tiles_used, 0)
    mod = lambda i: (i // tiles_used, jnp.where(i % tiles_used >= n_lat_tiles, 1, 0), 0, 0)
    return tok, mod


def _route(x1, modtab, g2, lp, n_lat_tiles, tiles_used):
    b, _, d = x1.shape
    n_tiles = b * tiles_used
    t = n_tiles * TM
    rw = lp['router_w'].T
    rb = lp['router_bias'].reshape(-1, 1)
    tri = (np.arange(TM)[:, None] < np.arange(TM)[None, :]).astype(np.float32)
    tri = jnp.asarray(tri, BF16)
    per_tok = lambda: pl.BlockSpec((TOPK_PAD, TM), lambda i: (0, i))
    full = lambda a: pl.BlockSpec(a.shape, lambda i: (0,) * a.ndim)
    tok, mod = _tile_maps(tiles_used, n_lat_tiles)
    return pl.pallas_call(
        _router_kernel,
        grid=(n_tiles,),
        in_specs=[pl.BlockSpec((1, TM, d), tok), pl.BlockSpec((1, 1, 8, d), mod),
                  full(g2), full(rw), full(rb), full(tri)],
        out_specs=[pl.BlockSpec((2, TM, d // 4), lambda i: (0, i, 0)), per_tok(), per_tok(), per_tok(),
                   pl.BlockSpec((N_EXPERTS, 1), lambda i: (0, 0))],
        out_shape=[jax.ShapeDtypeStruct((2, t, d // 4), U32), jax.ShapeDtypeStruct((TOPK_PAD, t), I32),
                   jax.ShapeDtypeStruct((TOPK_PAD, t), F32), jax.ShapeDtypeStruct((TOPK_PAD, t), I32),
                   jax.ShapeDtypeStruct((N_EXPERTS, 1), F32)],
        scratch_shapes=[pltpu.VMEM((N_EXPERTS, 1), F32)],
        compiler_params=_cparams("arbitrary"),
        name="moe_router",
    )(x1, modtab, g2, rw, rb, tri)


SC_WINDOW = 128


def _sc_mesh():
    return plsc.VectorSubcoreMesh(core_axis_name="c", subcore_axis_name="s")


def _sc_scatter_rows(src, idx, n_out):
    n, width = src.shape
    k_rep = idx.shape[0]
    half = n // SC_WINDOW // 2

    @functools.partial(pl.kernel, out_type=jax.ShapeDtypeStruct((n_out, width), src.dtype), mesh=_sc_mesh(),
                       scratch_types=[], name="moe_dispatch_sc")
    def scatter(src_hbm, idx_hbm, out_hbm):
        def body(x_vmem, *i_vmems):
            for i_vmem in i_vmems:
                pltpu.sync_copy(x_vmem, out_hbm.at[i_vmem.at[0]])

        pltpu.emit_pipeline(
            body,
            grid=(2, half),
            in_specs=[pl.BlockSpec((SC_WINDOW, width), lambda a, i: (a * half + i, 0))]
                     + [pl.BlockSpec((1, SC_WINDOW), lambda a, i, k=k: (k, a * half + i)) for k in range(k_rep)],
            out_specs=[],
            core_axis_name=("c", "s"),
            dimension_semantics=(pltpu.PARALLEL, pltpu.PARALLEL),
        )(src_hbm, *([idx_hbm] * k_rep))

    return scatter(src, idx)


def _sc_gather_rows(src, idx):
    k_rep, n = idx.shape
    width = src.shape[1]
    n_win = n // SC_WINDOW

    @functools.partial(pl.kernel, out_type=jax.ShapeDtypeStruct((k_rep * n, width), src.dtype), mesh=_sc_mesh(),
                       scratch_types=[], name="moe_gather_sc")
    def gather(src_hbm, idx_hbm, out_hbm):
        def body(i_vmem, o_vmem):
            pltpu.sync_copy(src_hbm.at[i_vmem.at[0]], o_vmem)

        pltpu.emit_pipeline(
            body,
            grid=(k_rep, n_win),
            in_specs=[pl.BlockSpec((1, SC_WINDOW), lambda k, i: (k, i))],
            out_specs=[pl.BlockSpec((SC_WINDOW, width), lambda k, i: (k * n_win + i, 0))],
            core_axis_name=("c", "s"),
            dimension_semantics=(pltpu.PARALLEL, pltpu.PARALLEL),
        )(idx_hbm, out_hbm)

    return gather(src, idx)


def _unpack_planes(p0, p1):
    return _unpack_pair(p0) + _unpack_pair(p1)


def _dot_quarters(parts, w_ref):
    q = parts[0].shape[-1]
    acc = None
    for j, part in enumerate(parts):
        term = _dot(part.astype(BF16), w_ref[j * q:(j + 1) * q, :])
        acc = term if acc is None else acc + term
    return acc


def _expert_kernel(be_ref, nv_ref, xs_ref, wg_ref, wu_ref, wd_ref, ys_o, wg_s, wu_s, wd_s):
    i = pl.program_id(0)
    prev = be_ref[jnp.maximum(i - 1, 0)]

    @pl.when((i == 0) | (be_ref[i] != prev))
    def _():
        wg_s[...] = wg_ref[0].astype(BF16)
        wu_s[...] = wu_ref[0].astype(BF16)
        wd_s[...] = wd_ref[0].astype(BF16)

    @pl.when(nv_ref[i] > 0)
    def _():
        keep = lax.broadcasted_iota(I32, xs_ref.shape[1:], 0) < nv_ref[i]
        parts = _unpack_planes(jnp.where(keep, xs_ref[0], jnp.uint32(0)), jnp.where(keep, xs_ref[1], jnp.uint32(0)))
        hid = _silu(_dot_quarters(parts, wg_s)) * _dot_quarters(parts, wu_s)
        y = _dot(hid.astype(BF16), wd_s[...])
        half = y.shape[-1] // 2
        ys_o[0] = _pack_pair(y[:, :half])
        ys_o[1] = _pack_pair(y[:, half:])

    @pl.when(nv_ref[i] <= 0)
    def _():
        ys_o[...] = jnp.zeros_like(ys_o)


def _experts(xs, block_e, n_valid, weights, layer):
    _, n_rows, dq = xs.shape
    d = 4 * dq
    n_blocks = n_rows // EXPERT_ROWS
    hid = EXPERT_HIDDEN
    grid_spec = pltpu.PrefetchScalarGridSpec(
        num_scalar_prefetch=2,
        grid=(n_blocks,),
        in_specs=[pl.BlockSpec((2, EXPERT_ROWS, dq), lambda i, be, nv: (0, i, 0)),
                  pl.BlockSpec((None, 1, d, hid), lambda i, be, nv: (layer, be[i], 0, 0)),
                  pl.BlockSpec((None, 1, d, hid), lambda i, be, nv: (layer, be[i], 0, 0)),
                  pl.BlockSpec((None, 1, hid, d), lambda i, be, nv: (layer, be[i], 0, 0))],
        out_specs=pl.BlockSpec((2, EXPERT_ROWS, dq), lambda i, be, nv: (0, i, 0)),
        scratch_shapes=[pltpu.VMEM((d, hid), BF16), pltpu.VMEM((d, hid), BF16), pltpu.VMEM((hid, d), BF16)],
    )
    return pl.pallas_call(
        _expert_kernel,
        grid_spec=grid_spec,
        out_shape=jax.ShapeDtypeStruct((2, n_rows, dq), U32),
        compiler_params=_cparams("arbitrary"),
        name="moe_experts",
    )(block_e, n_valid, xs, *weights)


def _combine_kernel(g_ref, wts_ref, h2_ref, x_ref, mod_ref, sg_ref, su_ref, sd_ref, gf_ref, o_ref, *, final):
    parts = _unpack_planes(h2_ref[0], h2_ref[1])
    hid = _silu(_dot_quarters(parts, sg_ref)) * _dot_quarters(parts, su_ref)
    shared = _dot(hid.astype(BF16), sd_ref[...])
    wts = wts_ref[...]
    routed = None
    for k in range(TOP_K):
        w = wts[:, k:k + 1]
        terms = [w * part for part in _unpack_planes(g_ref[k, 0], g_ref[k, 1])]
        routed = terms if routed is None else [r + t for r, t in zip(routed, terms)]
    f = shared + jnp.concatenate(routed, axis=1)
    m = mod_ref[0, 0]
    x2 = x_ref[0] + m[5:6] * f
    if final:
        x2 = x2 * lax.rsqrt(jnp.mean(x2 * x2, axis=-1, keepdims=True) + NORM_EPS) * gf_ref[...]
    o_ref[0] = x2


def _combine(g, wts, h2p, x1, modtab, lp, g_final, n_lat_tiles, tiles_used, final):
    b, _, d = x1.shape
    dq = d // 4
    weights = (lp['sh_w_gate'].astype(BF16), lp['sh_w_up'].astype(BF16), lp['sh_w_down'].astype(BF16),
               g_final.reshape(1, -1))
    full = lambda a: pl.BlockSpec(a.shape, lambda i: (0,) * a.ndim)
    tok, mod = _tile_maps(tiles_used, n_lat_tiles)
    return pl.pallas_call(
        functools.partial(_combine_kernel, final=final),
        grid=(b * tiles_used,),
        in_specs=[pl.BlockSpec((TOP_K, 2, TM, dq), lambda i: (0, 0, i, 0)),
                  pl.BlockSpec((TM, TOPK_PAD), lambda i: (i, 0)),
                  pl.BlockSpec((2, TM, dq), lambda i: (0, i, 0)),
                  pl.BlockSpec((1, TM, d), tok), pl.BlockSpec((1, 1, 8, d), mod)]
                 + [full(w) for w in weights],
        out_specs=pl.BlockSpec((1, TM, d), tok),
        out_shape=jax.ShapeDtypeStruct((b, tiles_used * TM, d), F32),
        compiler_params=_cparams("parallel"),
        name="moe_combine",
    )(g, wts, h2p, x1, modtab, *weights)


def _moe(x1, modtab, g2, lp, g_final, n_lat_tiles, final):
    b, nt, d = x1.shape
    tiles_used = n_lat_tiles if final else nt // TM
    t = b * tiles_used * TM
    h2p, idx, wts, rank, cnt = _route(x1, modtab, g2, lp, n_lat_tiles, tiles_used)
    counts = cnt[:, 0].astype(I32)
    padded = (counts + EXPERT_ROWS - 1) // EXPERT_ROWS * EXPERT_ROWS
    p_ends = jnp.cumsum(padded)
    p_starts = p_ends - padded
    n_blocks = (t * TOP_K + N_EXPERTS * (EXPERT_ROWS - 1)) // EXPERT_ROWS
    n_rows = n_blocks * EXPERT_ROWS
    chosen = idx[:TOP_K, None, :] == jnp.arange(N_EXPERTS, dtype=I32)[None, :, None]
    dest = jnp.sum(jnp.where(chosen, p_starts[None, :, None], 0), axis=1) + rank[:TOP_K]
    plane_idx = (dest[:, None, :] + (jnp.arange(2, dtype=I32) * n_rows)[None, :, None]).reshape(TOP_K, 2 * t)
    blk_start = jnp.arange(n_blocks, dtype=I32) * EXPERT_ROWS
    block_e = jnp.minimum(jnp.sum((p_ends[None, :] <= blk_start[:, None]).astype(I32), axis=1), N_EXPERTS - 1)
    n_valid = jnp.clip((p_starts + counts)[block_e] - blk_start, 0, EXPERT_ROWS).astype(I32)
    dq = d // 4
    xs = _sc_scatter_rows(h2p.reshape(2 * t, dq), plane_idx, 2 * n_rows).reshape(2, n_rows, dq)
    ys = _experts(xs, block_e, n_valid, lp['expert_stacks'], lp['layer'])
    g = _sc_gather_rows(ys.reshape(2 * n_rows, dq), plane_idx).reshape(TOP_K, 2, t, dq)
    return _combine(g, wts.T, h2p, x1, modtab, lp, g_final, n_lat_tiles, tiles_used, final)


def _layer(xa, c, c_ctx, lp, consts, g_final, seq, final):
    b, nt, d = xa.shape
    n_lat_tiles = seq // TM
    rope, mats_lat, mats_ctx = consts
    modtab = _mod_table(c, c_ctx, *lp['mod_stacks'], lp['layer'])
    g1 = lp['g_norm1'].reshape(1, -1)
    g2 = lp['g_norm2'].reshape(1, -1)
    pw = _proj_weights(lp)
    q, k, v, nq, nk, nv, hy, lu, lg = _project(xa, modtab, g1, pw, rope, n_lat_tiles)
    tiles_used = n_lat_tiles if final else nt // TM
    br_a = _mla_attention(q, k, v, seq, tiles_used)
    br_b = _na_attention(nq, nk, nv, _na_bias_tables(lp['na_rpb'], seq // GRID_W), seq, tiles_used)
    pre = _hy_pre(hy, lp['hy_short_w'], lp['hy_short_b'], seq)
    lat_f, lat_bf, ctx_f, ctx_bf = pre
    br_c = _hyena_seq(mats_lat, lat_f, lat_bf, lp, b)
    if not final:
        br_c = jnp.concatenate([br_c, _hyena_seq(mats_ctx, ctx_f, ctx_bf, lp, b)], axis=1)
    br_d = _lru_mixer(lu, lg, lp, seq)
    x1 = _merge(xa, modtab, g1, (br_a, br_b, br_c, br_d), pw['w_gate'], lp, n_lat_tiles, tiles_used)
    return _moe(x1, modtab, g2, lp, g_final, n_lat_tiles, final)


_LAYER_KEYS = ('w_mod', 'b_mod', 'g_norm1', 'g_norm2', 'w_in', 'mla_g_q', 'mla_w_uq', 'mla_g_kv', 'mla_w_ukv',
               'na_rpb', 'hy_short_w', 'hy_short_b', 'hy_w1', 'hy_b1', 'hy_w2', 'hy_b2', 'hy_w3', 'hy_decay',
               'hy_bias', 'lru_conv_w', 'lru_conv_b', 'lru_wa', 'lru_ba', 'lru_wx', 'lru_bx', 'lru_lambda',
               'w_branch', 'w_out', 'router_w', 'router_bias', 'exp_w_gate', 'exp_w_up', 'exp_w_down',
               'sh_w_gate', 'sh_w_up', 'sh_w_down')


def kernel(x, c, ctx, c_ctx, w_mod, b_mod, g_norm1, g_norm2, w_in, mla_g_q, mla_w_uq, mla_g_kv, mla_w_ukv, na_rpb, hy_short_w, hy_short_b, hy_w1, hy_b1, hy_w2, hy_b2, hy_w3, hy_decay, hy_bias, lru_conv_w, lru_conv_b, lru_wa, lru_ba, lru_wx, lru_bx, lru_lambda, w_branch, w_out, router_w, router_bias, exp_w_gate, exp_w_up, exp_w_down, sh_w_gate, sh_w_up, sh_w_down, g_final):
    stacked = dict(zip(_LAYER_KEYS, (w_mod, b_mod, g_norm1, g_norm2, w_in, mla_g_q, mla_w_uq, mla_g_kv, mla_w_ukv,
                                     na_rpb, hy_short_w, hy_short_b, hy_w1, hy_b1, hy_w2, hy_b2, hy_w3, hy_decay,
                                     hy_bias, lru_conv_w, lru_conv_b, lru_wa, lru_ba, lru_wx, lru_bx, lru_lambda,
                                     w_branch, w_out, router_w, router_bias, exp_w_gate, exp_w_up, exp_w_down,
                                     sh_w_gate, sh_w_up, sh_w_down)))
    b, seq, d = x.shape
    ctx_len = ctx.shape[1]
    depth = w_mod.shape[0]
    assert seq % TM == 0 and ctx_len % TM == 0 and seq // GRID_W >= NA_KEY_ROWS + 1
    xa = jnp.concatenate([x, ctx], axis=1)
    consts = (_rope_tables(seq, seq + ctx_len), _dft_matrices(seq), _dft_matrices(ctx_len))
    for i in range(depth):
        big = ('w_mod', 'b_mod', 'exp_w_gate', 'exp_w_up', 'exp_w_down')
        lp = {name: w[i] for name, w in stacked.items() if name not in big}
        lp['layer'] = i
        lp['mod_stacks'] = (w_mod, b_mod)
        lp['expert_stacks'] = (exp_w_gate, exp_w_up, exp_w_down)
        xa = _layer(xa, c, c_ctx, lp, consts, g_final, seq, i == depth - 1)
    return xa
```

```python
import functools
import math

import numpy as np
import jax
import jax.numpy as jnp
from jax import lax
from jax.experimental import pallas as pl
from jax.experimental.pallas import tpu as pltpu
from jax.experimental.pallas import tpu_sc as plsc

F32 = jnp.float32
BF16 = jnp.bfloat16
I32 = jnp.int32

TM = 256
LANE = 128
GRID_W = 64
N_MOD = 6
NORM_EPS = 1e-6

MLA_HEADS, MLA_NOPE, MLA_ROPE, MLA_V = 4, 64, 32, 64
MLA_Q_RANK, MLA_KV_RANK = 192, 128
MLA_Q_PAD = 256
ROPE_THETA = 10000.0

NA_HEADS, NA_HEAD_DIM, NA_WIN_R, NA_WIN_C = 4, 64, 8, 16
NA_TILE_ROWS = TM // GRID_W
NA_KEY_ROWS = NA_TILE_ROWS + NA_WIN_R - 1
NA_KEYS = NA_KEY_ROWS * GRID_W

HY_WIDTH, HY_ORDER, HY_SHORT, HY_BANDS, HY_FFN = 256, 2, 3, 16, 64
HY_EMB = 2 * HY_BANDS + 1

LRU_WIDTH, LRU_BLOCKS, LRU_CONV, LRU_C = 256, 4, 4, 8.0
LRU_CHUNK = 256
LRU_HALO = 8

N_EXPERTS, TOP_K, EXPERT_HIDDEN, ROUTED_SCALE, MOE_BLOCK = 64, 6, 256, 2.5, 256
TOPK_PAD = 8

VMEM_LIMIT = 52 * 1024 * 1024


def _cparams(*sem):
    return pltpu.CompilerParams(dimension_semantics=sem, vmem_limit_bytes=VMEM_LIMIT)


def _dot(a, b):
    return jnp.dot(a, b, preferred_element_type=F32)


def _dot_nt(a, b):
    return lax.dot_general(a, b, (((1,), (1,)), ((), ())), preferred_element_type=F32)


def _sigmoid(x):
    return 1.0 / (1.0 + jnp.exp(-x))


def _silu(x):
    return x * _sigmoid(x)


def _gelu_tanh(x):
    return 0.5 * x * (1.0 + jnp.tanh(math.sqrt(2.0 / math.pi) * (x + 0.044715 * (x * x * x))))


def _normmod(x, g, scale, shift):
    y = x * lax.rsqrt(jnp.mean(x * x, axis=-1, keepdims=True) + NORM_EPS) * g
    return y * (1.0 + scale) + shift


def _mod_kernel(c_ref, w_ref, b_ref, o_ref):
    s = _silu(c_ref[...])
    o_ref[...] = _dot(s.astype(BF16), w_ref[...].astype(BF16)) + b_ref[...]


def _mod_table(c, c_ctx, w_mod, b_mod, layer):
    b, d = c.shape
    rows = 16
    cc = jnp.zeros((rows, d), F32).at[:b].set(c).at[b].set(c_ctx)
    tn = 1024
    mod = pl.pallas_call(
        _mod_kernel,
        grid=(N_MOD * d // tn,),
        in_specs=[pl.BlockSpec((rows, d), lambda j: (0, 0)),
                  pl.BlockSpec((None, d, tn), lambda j: (layer, 0, j)),
                  pl.BlockSpec((None, 1, tn), lambda j: (layer, 0, j))],
        out_specs=pl.BlockSpec((rows, tn), lambda j: (0, j)),
        out_shape=jax.ShapeDtypeStruct((rows, N_MOD * d), F32),
        compiler_params=_cparams("arbitrary"),
        name="mod_vectors",
    )(cc, w_mod, b_mod[:, None, :])
    lat = mod[:b].reshape(b, N_MOD, d)
    ctx = jnp.broadcast_to(mod[b].reshape(1, N_MOD, d), (b, N_MOD, d))
    tab = jnp.stack([lat, ctx], axis=1)
    return jnp.pad(tab, ((0, 0), (0, 0), (0, 8 - N_MOD), (0, 0)))


_C_QLAT, _C_KVLAT, _C_KR, _C_KRR, _C_NA, _C_HY, _C_LU, _C_LG, _C_END = (
    0, 256, 384, 512, 640, 640 + 3 * NA_HEADS * LANE, 640 + 1536 + 768, 640 + 1536 + 1024, 640 + 1536 + 1280)


def _proj_kernel(x_ref, mod_ref, g1_ref, w1_ref, gq_ref, gkv_ref, wuq_ref, wuqr_ref, wk_ref, wv_ref,
                 cos_ref, sin_ref, q_o, k_o, v_o, nq_o, nk_o, nv_o, hy_o, lu_o, lg_o):
    group, tm, d = x_ref.shape
    m = mod_ref[:, 0]
    h = _normmod(x_ref[...], g1_ref[...], m[:, 1:2], m[:, 0:1]).reshape(group * tm, d)
    z = _dot(h.astype(BF16), w1_ref[...])
    qlat = z[:, _C_QLAT:_C_KVLAT]
    kvlat = z[:, _C_KVLAT:_C_KR]
    kr = z[:, _C_KR:_C_KRR]
    krr = z[:, _C_KRR:_C_NA]
    qn = qlat * lax.rsqrt(jnp.sum(qlat * qlat, axis=-1, keepdims=True) * (1.0 / MLA_Q_RANK) + NORM_EPS) * gq_ref[...]
    kvn = kvlat * lax.rsqrt(jnp.mean(kvlat * kvlat, axis=-1, keepdims=True) + NORM_EPS) * gkv_ref[...]
    qn = qn.astype(BF16)
    kvn = kvn.astype(BF16)
    q = _dot(qn, wuq_ref[...])
    qr = _dot(qn, wuqr_ref[...])
    kk = _dot(kvn, wk_ref[...])
    vv = _dot(kvn, wv_ref[...])
    cos = jnp.concatenate([cos_ref[...]] * group, axis=0)
    sin = jnp.concatenate([sin_ref[...]] * group, axis=0)
    krope = kr * cos + krr * sin
    den_lane = lax.broadcasted_iota(I32, cos.shape, 1) == SOFTMAX_DEN_LANE
    def put_heads(ref, hd, val):
        for g in range(group):
            ref[g, hd] = val[g * tm:(g + 1) * tm].astype(BF16)

    for hd in range(MLA_HEADS):
        sl = slice(hd * LANE, (hd + 1) * LANE)
        put_heads(q_o, hd, q[:, sl] * cos + qr[:, sl] * sin)
        put_heads(k_o, hd, kk[:, sl] + krope)
        put_heads(v_o, hd, jnp.where(den_lane, 1.0, vv[:, sl]))
    for hd in range(NA_HEADS):
        for which, ref in enumerate((nq_o, nk_o, nv_o)):
            lo = _C_NA + (which * NA_HEADS + hd) * LANE
            blk = z[:, lo:lo + LANE]
            put_heads(ref, hd, jnp.where(den_lane, 1.0, blk) if which == 2 else blk)
    hy_o[...] = z[:, _C_HY:_C_LU].reshape(group, tm, _C_LU - _C_HY)
    lu_o[...] = z[:, _C_LU:_C_LG].reshape(group, tm, _C_LG - _C_LU)
    lg_o[...] = z[:, _C_LG:_C_END].reshape(group, tm, _C_END - _C_LG)


def _pad_to(a, n, axis):
    pad = [(0, 0)] * a.ndim
    pad[axis] = (0, n - a.shape[axis])
    return jnp.pad(a, pad)


def _rot_cols(w):
    half = w.shape[-1] // 2
    return jnp.concatenate([-w[..., half:], w[..., :half]], axis=-1)


def _head_blocks(cols_per_head):
    out = []
    for pieces in cols_per_head:
        k = pieces[0][0].shape[0]
        blk = jnp.zeros((k, LANE), F32)
        for arr, off in pieces:
            blk = blk.at[:, off:off + arr.shape[1]].set(arr)
        out.append(blk)
    return jnp.concatenate(out, axis=1)


def _proj_weights(lp):
    w_in = lp['w_in']
    d = w_in.shape[0]
    o = 0
    parts = {}
    for name, n in (('q', MLA_Q_RANK), ('kv', MLA_KV_RANK), ('kr', MLA_ROPE), ('na', 3 * NA_HEADS * NA_HEAD_DIM),
                    ('hy', 3 * HY_WIDTH), ('lu', LRU_WIDTH), ('lg', LRU_WIDTH), ('gt', 4 * d)):
        parts[name] = w_in[:, o:o + n]
        o += n
    zeros = lambda n: jnp.zeros((d, n), F32)
    kr_blk = jnp.concatenate([zeros(MLA_NOPE), parts['kr'], zeros(LANE - MLA_NOPE - MLA_ROPE)], axis=1)
    krr_blk = jnp.concatenate([zeros(MLA_NOPE), _rot_cols(parts['kr']), zeros(LANE - MLA_NOPE - MLA_ROPE)], axis=1)
    na_scale = NA_HEAD_DIM ** -0.5
    na_cols = []
    for which in range(3):
        for hd in range(NA_HEADS):
            lo = (which * NA_HEADS + hd) * NA_HEAD_DIM
            blk = parts['na'][:, lo:lo + NA_HEAD_DIM] * (na_scale if which == 0 else 1.0)
            na_cols.append(_pad_to(blk, LANE, 1))
    w1 = jnp.concatenate([_pad_to(parts['q'], MLA_Q_PAD, 1), parts['kv'], kr_blk, krr_blk] + na_cols
                         + [parts['hy'], parts['lu'], parts['lg']], axis=1).astype(BF16)
    mla_scale = (MLA_NOPE + MLA_ROPE) ** -0.5
    wuq = _pad_to(lp['mla_w_uq'], MLA_Q_PAD, 0) * mla_scale
    dq = MLA_NOPE + MLA_ROPE
    wuq_main = _head_blocks([[(wuq[:, hd * dq:hd * dq + dq], 0)] for hd in range(MLA_HEADS)])
    wuq_rot = _head_blocks([[(_rot_cols(wuq[:, hd * dq + MLA_NOPE:hd * dq + dq]), MLA_NOPE)]
                            for hd in range(MLA_HEADS)])
    dkv = MLA_NOPE + MLA_V
    wukv = lp['mla_w_ukv']
    wk = _head_blocks([[(wukv[:, hd * dkv:hd * dkv + MLA_NOPE], 0)] for hd in range(MLA_HEADS)])
    wv = _head_blocks([[(wukv[:, hd * dkv + MLA_NOPE:hd * dkv + dkv], 0)] for hd in range(MLA_HEADS)])
    gq = _pad_to(lp['mla_g_q'].reshape(1, -1), MLA_Q_PAD, 1)
    gkv = lp['mla_g_kv'].reshape(1, -1)
    return dict(w1=w1, w_gate=parts['gt'].astype(BF16), gq=gq, gkv=gkv, wuq=wuq_main.astype(BF16),
                wuq_rot=wuq_rot.astype(BF16), wk=wk.astype(BF16), wv=wv.astype(BF16))


def _rope_tables(seq, n_tok):
    t = jnp.arange(seq, dtype=I32)
    row = (t // GRID_W).astype(F32)
    col = (t % GRID_W).astype(F32)
    n_axis = MLA_ROPE // 4
    inv_freq = ROPE_THETA ** (-jnp.arange(n_axis, dtype=F32) / n_axis)
    ang = jnp.concatenate([row[:, None] * inv_freq, col[:, None] * inv_freq], axis=-1)
    cos = jnp.concatenate([jnp.cos(ang), jnp.cos(ang)], axis=-1)
    sin = jnp.concatenate([jnp.sin(ang), jnp.sin(ang)], axis=-1)
    cos_t = jnp.ones((n_tok, LANE), F32).at[:seq, MLA_NOPE:MLA_NOPE + MLA_ROPE].set(cos)
    sin_t = jnp.zeros((n_tok, LANE), F32).at[:seq, MLA_NOPE:MLA_NOPE + MLA_ROPE].set(sin)
    return cos_t, sin_t


def _batch_group(b):
    return 2 if b % 2 == 0 else 1


def _kind_map(n_lat_tiles):
    return lambda b, i: (b, jnp.where(i >= n_lat_tiles, 1, 0), 0, 0)


def _project(xa, modtab, g1, pw, rope, n_lat_tiles):
    b, nt, d = xa.shape
    cos_t, sin_t = rope
    group = _batch_group(b)
    full = lambda a: pl.BlockSpec(a.shape, lambda bb, i: (0,) * a.ndim)
    head_out = lambda: pl.BlockSpec((group, MLA_HEADS, TM, LANE), lambda bb, i: (bb, 0, i, 0))
    tok_out = lambda n: pl.BlockSpec((group, TM, n), lambda bb, i: (bb, i, 0))
    head_shape = jax.ShapeDtypeStruct((b, MLA_HEADS, nt, LANE), BF16)
    tok_shape = lambda n: jax.ShapeDtypeStruct((b, nt, n), F32)
    weights = (g1, pw['w1'], pw['gq'], pw['gkv'], pw['wuq'], pw['wuq_rot'], pw['wk'], pw['wv'])
    return pl.pallas_call(
        _proj_kernel,
        grid=(b // group, nt // TM),
        in_specs=[pl.BlockSpec((group, TM, d), lambda bb, i: (bb, i, 0)),
                  pl.BlockSpec((group, 1, 8, d), _kind_map(n_lat_tiles))]
                 + [full(w) for w in weights]
                 + [pl.BlockSpec((TM, LANE), lambda bb, i: (i, 0))] * 2,
        out_specs=[head_out() for _ in range(6)] + [tok_out(3 * HY_WIDTH), tok_out(LRU_WIDTH), tok_out(LRU_WIDTH)],
        out_shape=[head_shape] * 6 + [tok_shape(3 * HY_WIDTH), tok_shape(LRU_WIDTH), tok_shape(LRU_WIDTH)],
        compiler_params=_cparams("parallel", "parallel"),
        name="input_projection",
    )(xa, modtab, *weights, cos_t, sin_t)


SOFTMAX_DEN_LANE = 64


def _softmax_pv(parts):
    m = None
    for s, _ in parts:
        mm = jnp.max(s, axis=-1, keepdims=True)
        m = mm if m is None else jnp.maximum(m, mm)
    acc = None
    for s, v in parts:
        o = _dot(jnp.exp(s - m).astype(BF16), v)
        acc = o if acc is None else acc + o
    return acc / acc[:, SOFTMAX_DEN_LANE:SOFTMAX_DEN_LANE + 1]


def _mla_kernel(q_ref, k_ref, v_ref, o_ref, *, seq, n_lat_tiles):
    i = pl.program_id(1)
    nt = k_ref.shape[2]

    def attend(lo, hi):
        for hd in range(MLA_HEADS):
            s = _dot_nt(q_ref[0, hd], k_ref[0, hd, lo:hi, :])
            o = _softmax_pv([(s, v_ref[0, hd, lo:hi, :])])
            o_ref[0, :, hd * LANE:(hd + 1) * LANE] = o.astype(BF16)

    @pl.when(i < n_lat_tiles)
    def _():
        attend(0, nt)

    @pl.when(i >= n_lat_tiles)
    def _():
        attend(seq, nt)


def _mla_attention(q, k, v, seq, tiles_used):
    b, h, nt, _ = q.shape
    kv_spec = pl.BlockSpec((1, h, nt, LANE), lambda bb, i: (bb, 0, 0, 0))
    return pl.pallas_call(
        functools.partial(_mla_kernel, seq=seq, n_lat_tiles=seq // TM),
        grid=(b, tiles_used),
        in_specs=[pl.BlockSpec((1, h, TM, LANE), lambda bb, i: (bb, 0, i, 0)), kv_spec, kv_spec],
        out_specs=pl.BlockSpec((1, TM, h * LANE), lambda bb, i: (bb, i, 0)),
        out_shape=jax.ShapeDtypeStruct((b, tiles_used * TM, h * LANE), BF16),
        compiler_params=_cparams("parallel", "arbitrary"),
        name="mla_attention",
    )(q, k, v)


def _na_bias_tables(rpb, rows):
    n_blk = rows // NA_TILE_ROWS
    col = np.arange(GRID_W)
    c0 = np.clip(col - NA_WIN_C // 2, 0, GRID_W - NA_WIN_C)
    in_win = (col[None, :] >= c0[:, None]) & (col[None, :] < c0[:, None] + NA_WIN_C)
    dc = np.clip(col[None, :] - col[:, None], 1 - NA_WIN_C, NA_WIN_C - 1) + NA_WIN_C - 1
    rpb = rpb.astype(F32)
    tables = []
    for j in (0, 1, n_blk - 1):
        w0 = min(max(NA_TILE_ROWS * j - NA_WIN_R // 2, 0), rows - NA_KEY_ROWS)
        r = NA_TILE_ROWS * j + np.arange(NA_TILE_ROWS)
        kr = w0 + np.arange(NA_KEY_ROWS)
        r0 = np.clip(r - NA_WIN_R // 2, 0, rows - NA_WIN_R)
        row_ok = (kr[None, :] >= r0[:, None]) & (kr[None, :] < r0[:, None] + NA_WIN_R)
        dr = np.clip(kr[None, :] - r[:, None] + NA_WIN_R - 1, 0, 2 * NA_WIN_R - 2)
        oh_r = jnp.asarray(np.eye(2 * NA_WIN_R - 1, dtype=np.float32)[dr.reshape(-1)])
        oh_c = jnp.asarray(np.eye(2 * NA_WIN_C - 1, dtype=np.float32)[dc.reshape(-1)])
        bias = jnp.einsum('ar,hrc,bc->hab', oh_r, rpb, oh_c, precision=lax.Precision.HIGHEST)
        bias = bias.reshape(NA_HEADS, NA_TILE_ROWS, NA_KEY_ROWS, GRID_W, GRID_W)
        mask = row_ok[:, :, None, None] & in_win[None, None, :, :]
        bias = jnp.where(jnp.asarray(mask)[None], bias, -jnp.inf)
        tables.append(bias.transpose(0, 1, 3, 2, 4).reshape(NA_HEADS, TM, NA_KEYS))
    return jnp.stack(tables)


def _na_kernel(q_ref, k_ref, v_ref, bias_ref, o_ref, *, seq, n_lat_tiles):
    i = pl.program_id(1)
    nt = k_ref.shape[2]
    rows = seq // GRID_W

    @pl.when(i < n_lat_tiles)
    def _():
        w0 = jnp.clip(NA_TILE_ROWS * i - NA_WIN_R // 2, 0, rows - NA_KEY_ROWS)
        start = pl.multiple_of(w0 * GRID_W, GRID_W)
        for hd in range(NA_HEADS):
            q = q_ref[0, hd]
            s_loc = _dot_nt(q, k_ref[0, hd, pl.ds(start, NA_KEYS), :]) + bias_ref[0, hd]
            s_ctx = _dot_nt(q, k_ref[0, hd, seq:nt, :])
            o = _softmax_pv([(s_loc, v_ref[0, hd, pl.ds(start, NA_KEYS), :]), (s_ctx, v_ref[0, hd, seq:nt, :])])
            o_ref[0, :, hd * LANE:(hd + 1) * LANE] = o.astype(BF16)

    @pl.when(i >= n_lat_tiles)
    def _():
        for hd in range(NA_HEADS):
            s = _dot_nt(q_ref[0, hd], k_ref[0, hd, seq:nt, :])
            o = _softmax_pv([(s, v_ref[0, hd, seq:nt, :])])
            o_ref[0, :, hd * LANE:(hd + 1) * LANE] = o.astype(BF16)


def _na_attention(q, k, v, bias, seq, tiles_used):
    b, h, nt, _ = q.shape
    n_lat = seq // TM
    kv_spec = pl.BlockSpec((1, h, nt, LANE), lambda bb, i: (bb, 0, 0, 0))
    cfg = lambda bb, i: (jnp.where(i == 0, 0, jnp.where(i >= n_lat - 1, 2, 1)), 0, 0, 0)
    return pl.pallas_call(
        functools.partial(_na_kernel, seq=seq, n_lat_tiles=n_lat),
        grid=(b, tiles_used),
        in_specs=[pl.BlockSpec((1, h, TM, LANE), lambda bb, i: (bb, 0, i, 0)), kv_spec, kv_spec,
                  pl.BlockSpec((1, h, TM, NA_KEYS), cfg)],
        out_specs=pl.BlockSpec((1, TM, h * LANE), lambda bb, i: (bb, i, 0)),
        out_shape=jax.ShapeDtypeStruct((b, tiles_used * TM, h * LANE), BF16),
        compiler_params=_cparams("parallel", "arbitrary"),
        name="neighbourhood_attention",
    )(q, k, v, bias)


HY_PRE_CHUNK = 256


def _hy_pre_kernel(z_ref, w_ref, b_ref, lat_o, lat_bf_o, ctx_o, ctx_bf_o, *, seq, ctx_len):
    w = w_ref[...]
    bias = b_ref[...]
    ch = HY_PRE_CHUNK
    zero = jnp.zeros((8, HY_WIDTH), F32)
    for seg_lo, seg_len, o_ref, obf_ref in ((0, seq, lat_o, lat_bf_o), (seq, ctx_len, ctx_o, ctx_bf_o)):
        for c in range(seg_len // ch):
            s = seg_lo + c * ch
            before = zero if c == 0 else z_ref[0, s - 8:s, :]
            after = zero if c == seg_len // ch - 1 else z_ref[0, s + ch:s + ch + 8, :]
            win = jnp.concatenate([before, z_ref[0, s:s + ch, :], after], axis=0)
            n = ch + 16
            y = bias + w[1:2] * win[8:8 + ch]
            y = y + w[0:1] * pltpu.roll(win, 1, axis=0)[8:8 + ch]
            y = y + w[2:3] * pltpu.roll(win, n - 1, axis=0)[8:8 + ch]
            o_ref[0, c * ch:(c + 1) * ch, :] = y
            obf_ref[0, c * ch:(c + 1) * ch, :] = y.astype(BF16)


def _hy_pre(hy, w, bvec, seq):
    b, nt, _ = hy.shape
    ctx_len = nt - seq
    wpad = _pad_to(w, 8, 0)
    out_specs, out_shape = [], []
    for length in (seq, ctx_len):
        for dt in (F32, BF16):
            out_specs.append(pl.BlockSpec((1, length, HY_WIDTH), lambda bb, g: (g, 0, bb)))
            out_shape.append(jax.ShapeDtypeStruct((3, length, b * HY_WIDTH), dt))
    return pl.pallas_call(
        functools.partial(_hy_pre_kernel, seq=seq, ctx_len=ctx_len),
        grid=(b, 3),
        in_specs=[pl.BlockSpec((1, nt, HY_WIDTH), lambda bb, g: (bb, 0, g)),
                  pl.BlockSpec((8, HY_WIDTH), lambda bb, g: (0, g)),
                  pl.BlockSpec((1, HY_WIDTH), lambda bb, g: (0, g))],
        out_specs=out_specs,
        out_shape=out_shape,
        compiler_params=_cparams("parallel", "parallel"),
        name="hyena_short_conv",
    )(hy, wpad, bvec.reshape(1, -1))


def _hp_dot(a, b):
    return jnp.dot(a, b, preferred_element_type=F32, precision=lax.Precision.HIGHEST)


def _hy_filter_kernel(f_ref, w1_ref, b1_ref, w2_ref, b2_ref, w3_ref, dec_ref, o_ref, obf_ref):
    f = f_ref[...]
    h = jnp.sin(_hp_dot(f, w1_ref[...]) + b1_ref[...])
    h = jnp.sin(_hp_dot(h, w2_ref[...]) + b2_ref[...])
    h = _hp_dot(h, w3_ref[...])
    h = h * jnp.exp(-f[:, 0:1] * jnp.abs(dec_ref[...]))
    o_ref[...] = h
    obf_ref[...] = h.astype(BF16)


def _hy_pos_features(length):
    t = jnp.linspace(0.0, 1.0, length, dtype=F32)[:, None]
    w = 2.0 * math.pi * jnp.arange(length, dtype=F32)[:, None] / length
    f = jnp.linspace(1e-4, HY_BANDS - 1, HY_BANDS, dtype=F32)[None, :]
    z = w * f
    return jnp.concatenate([t, jnp.cos(z), -jnp.sin(z)], axis=-1)


def _hy_filters(length, lp):
    feats = _hy_pos_features(length)
    n_out = HY_ORDER * 2 * HY_WIDTH
    tl = min(length, 512)
    full = lambda a: pl.BlockSpec(a.shape, lambda i: (0,) * a.ndim)
    args = (lp['hy_w1'], lp['hy_b1'].reshape(1, -1), lp['hy_w2'], lp['hy_b2'].reshape(1, -1), lp['hy_w3'],
            lp['hy_decay'].reshape(1, n_out))
    return pl.pallas_call(
        _hy_filter_kernel,
        grid=(length // tl,),
        in_specs=[pl.BlockSpec((tl, HY_EMB), lambda i: (i, 0))] + [full(a) for a in args],
        out_specs=[pl.BlockSpec((tl, n_out), lambda i: (i, 0))] * 2,
        out_shape=[jax.ShapeDtypeStruct((length, n_out), F32), jax.ShapeDtypeStruct((length, n_out), BF16)],
        compiler_params=_cparams("parallel"),
        name="hyena_filter_mlp",
    )(feats, *args)


def _dft_matrices(length):
    n2 = 2 * length
    step = DFT_ROW_STEP
    n = jnp.arange(length, dtype=I32)

    def trig(kv):
        ang = ((kv[:, None] * n[None, :]) % n2).astype(F32) * (2.0 * math.pi / n2)
        return jnp.cos(ang), jnp.sin(ang)

    c1, s1 = trig(jnp.arange(length // step, dtype=I32) * step)
    c2, s2 = trig(jnp.arange(step, dtype=I32))
    coarse = pl.BlockSpec((1, 1, length), lambda j: (j, 0, 0))
    fine = pl.BlockSpec((step, length), lambda j: (0, 0))
    out = pl.BlockSpec((step, length), lambda j: (j, 0))
    return pl.pallas_call(
        _dft_matrix_kernel,
        grid=(length // step,),
        in_specs=[coarse, coarse, fine, fine],
        out_specs=[out, out, out],
        out_shape=[jax.ShapeDtypeStruct((length, length), BF16)] * 3,
        compiler_params=_cparams("parallel"),
        name="dft_matrices",
    )(c1[:, None, :], s1[:, None, :], c2, s2)


DFT_ROW_STEP = 64


def _dft_matrix_kernel(c1_ref, s1_ref, c2_ref, s2_ref, mc_o, ms_o, mst_o):
    c1, s1 = c1_ref[0], s1_ref[0]
    c2, s2 = c2_ref[...], s2_ref[...]
    cos = c1 * c2 - s1 * s2
    neg_sin = -(s1 * c2 + c1 * s2)
    row = lax.broadcasted_iota(I32, cos.shape, 0)
    col = lax.broadcasted_iota(I32, cos.shape, 1)
    first_row = (row + pl.program_id(0) * DFT_ROW_STEP) == 0
    mc_o[...] = cos.astype(BF16)
    ms_o[...] = jnp.where(first_row, jnp.where(col % 2 == 0, 1.0, -1.0), neg_sin).astype(BF16)
    mst_o[...] = jnp.where(col == 0, jnp.where(row % 2 == 0, 1.0, -1.0), neg_sin).astype(BF16)


def _dft_fwd_kernel(mc_ref, ms_ref, x_ref, *rest, with_taps):
    ure = _dot(mc_ref[...], x_ref[...])
    uim = _dot(ms_ref[...], x_ref[...])
    if not with_taps:
        ure_o, uim_o = rest
        ure_o[...] = ure
        uim_o[...] = uim
        return
    a_ref, b_ref, c_ref, d_ref, zre_o, zim_o = rest
    a, b, c, d = a_ref[...], b_ref[...], c_ref[...], d_ref[...]
    for bb in range(x_ref.shape[1] // HY_WIDTH):
        sl = slice(bb * HY_WIDTH, (bb + 1) * HY_WIDTH)
        zre_o[:, sl] = (ure[:, sl] * a - uim[:, sl] * b).astype(BF16)
        zim_o[:, sl] = (ure[:, sl] * c + uim[:, sl] * d).astype(BF16)


def _col_block(nc, cap):
    return min(nc, cap)


def _dft_fwd(mats, x, plane, taps=None):
    mc, ms, _ = mats
    _, length, nc = x.shape
    tk = min(length, 256)
    cb = _col_block(nc, 1024)
    grid = (nc // cb, length // tk)
    m_spec = pl.BlockSpec((tk, length), lambda c, j: (j, 0))
    x_spec = pl.BlockSpec((None, length, cb), lambda c, j: (plane, 0, c))
    o_spec = pl.BlockSpec((tk, cb), lambda c, j: (j, c))
    if taps is None:
        return pl.pallas_call(
            functools.partial(_dft_fwd_kernel, with_taps=False),
            grid=grid, in_specs=[m_spec, m_spec, x_spec], out_specs=[o_spec, o_spec],
            out_shape=[jax.ShapeDtypeStruct((length, nc), F32)] * 2,
            compiler_params=_cparams("parallel", "arbitrary"),
            name="hyena_dft_filters",
        )(mc, ms, x)
    t_spec = pl.BlockSpec((tk, HY_WIDTH), lambda c, j: (j, 0))
    return pl.pallas_call(
        functools.partial(_dft_fwd_kernel, with_taps=True),
        grid=grid, in_specs=[m_spec, m_spec, x_spec] + [t_spec] * 4, out_specs=[o_spec, o_spec],
        out_shape=[jax.ShapeDtypeStruct((length, nc), BF16)] * 2,
        compiler_params=_cparams("parallel", "arbitrary"),
        name="hyena_dft_forward",
    )(mc, ms, x, *taps)


def _dft_inv_kernel(mc_ref, mst_ref, zre_ref, zim_ref, gate_ref, prev_ref, bias_ref, *outs, last):
    conv = _dot(mc_ref[...], zre_ref[...]) + _dot(mst_ref[...], zim_ref[...])
    y = gate_ref[...] * (conv + prev_ref[...] * bias_ref[...])
    if last:
        (tok_o,) = outs
        for bb in range(y.shape[1] // HY_WIDTH):
            tok_o[bb] = y[:, bb * HY_WIDTH:(bb + 1) * HY_WIDTH].astype(BF16)
    else:
        y_o, ybf_o = outs
        y_o[...] = y
        ybf_o[...] = y.astype(BF16)


def _dft_inv(mats, zre, zim, gate, gate_plane, prev, prev_plane, bias_row, last):
    mc, _, mst = mats
    length, nc = zre.shape
    tm = min(length, 256)
    cb = _col_block(nc, 512)
    grid = (nc // cb, length // tm)
    m_spec = pl.BlockSpec((tm, length), lambda c, i: (i, 0))
    z_spec = pl.BlockSpec((length, cb), lambda c, i: (0, c))
    e_spec = lambda plane: pl.BlockSpec((None, tm, cb), lambda c, i: (plane, i, c))
    b_spec = pl.BlockSpec((1, cb), lambda c, i: (0, c))
    if last:
        nb = nc // HY_WIDTH
        out_specs = [pl.BlockSpec((cb // HY_WIDTH, tm, HY_WIDTH), lambda c, i: (c, i, 0))]
        out_shape = [jax.ShapeDtypeStruct((nb, length, HY_WIDTH), BF16)]
    else:
        out_specs = [e_spec(0), e_spec(0)]
        out_shape = [jax.ShapeDtypeStruct((1, length, nc), F32), jax.ShapeDtypeStruct((1, length, nc), BF16)]
    return pl.pallas_call(
        functools.partial(_dft_inv_kernel, last=last),
        grid=grid, in_specs=[m_spec, m_spec, z_spec, z_spec, e_spec(gate_plane), e_spec(prev_plane), b_spec],
        out_specs=out_specs, out_shape=out_shape,
        compiler_params=_cparams("parallel", "arbitrary"),
        name="hyena_dft_inverse",
    )(mc, mst, zre, zim, gate, prev, bias_row)


def _hy_tap_tables(ure, uim, filt, length):
    n2 = 2 * length
    w = HY_WIDTH
    scale = jnp.full((length, 1), 2.0 / n2, F32).at[0, 0].set(1.0 / n2)
    tables = []
    for o in range(HY_ORDER):
        f_sl = slice((2 * o) * w, (2 * o + 1) * w)
        b_sl = slice((2 * o + 1) * w, (2 * o + 2) * w)
        hb0 = filt[0:1, b_sl]
        tre = ure[:, f_sl] + ure[:, b_sl] - hb0
        tim = uim[:, f_sl] - uim[:, b_sl]
        t_nyq = uim[0:1, f_sl] + uim[0:1, b_sl] - hb0
        first = (jnp.arange(length) == 0)[:, None]
        a = tre * scale
        bm = jnp.where(first, 0.0, tim * scale)
        c = bm
        d = jnp.where(first, t_nyq * scale, tre * scale)
        tables.append((a, bm, c, d))
    return tables


def _hyena_seq(mats, vx, vx_bf, lp, n_batch):
    length = vx.shape[1]
    filt, filt_bf = _hy_filters(length, lp)
    ure, uim = _dft_fwd(mats, filt_bf[None], 0)
    tables = _hy_tap_tables(ure, uim, filt, length)
    bias = lp['hy_bias'].astype(F32)
    y, y_bf = vx, vx_bf
    for o in range(HY_ORDER):
        zre, zim = _dft_fwd(mats, y_bf, 0, tables[o])
        bias_row = jnp.tile(bias[o][None, :], (1, n_batch))
        last = o == HY_ORDER - 1
        res = _dft_inv(mats, zre, zim, vx, o + 1, y, 0, bias_row, last)
        if last:
            return res[0]
        y, y_bf = res


def _lru_kernel(u_ref, g_ref, cw_ref, cb_ref, wa_ref, ba_ref, wx_ref, bx_ref, lam_ref, o_ref,
                pad_ref, y_ref, *, seq, ctx_len):
    tc = LRU_CHUNK
    halo = LRU_HALO
    width = LRU_WIDTH
    lat_off = halo
    ctx_off = 2 * halo + seq
    zero = jnp.zeros((halo, width), F32)
    pad_ref[0:halo, :] = zero
    pad_ref[lat_off:lat_off + seq, :] = u_ref[0, 0:seq, :]
    pad_ref[lat_off + seq:ctx_off, :] = zero
    pad_ref[ctx_off:ctx_off + ctx_len, :] = u_ref[0, seq:seq + ctx_len, :]
    pad_ref[ctx_off + ctx_len:ctx_off + ctx_len + halo, :] = zero
    row = lax.broadcasted_iota(I32, (tc, width), 0)
    n_win = tc + 2 * halo

    def chunk(pad_off, y_off, s, carry, d):
        wstart = pl.multiple_of(pad_off + s - halo, 8)
        win = pad_ref[pl.ds(wstart, n_win), :]
        cw = cw_ref[d]
        xc = cb_ref[d]
        for k in range(LRU_CONV):
            shift = (LRU_CONV - 1 - k) if d == 0 else -k
            rolled = win if shift == 0 else pltpu.roll(win, shift % n_win, axis=0)
            xc = xc + cw[k:k + 1] * rolled[halo:halo + tc]
        xb = xc.astype(BF16)
        r = _sigmoid(_dot(xb, wa_ref[d]) + ba_ref[d])
        gi = _sigmoid(_dot(xb, wx_ref[d]) + bx_ref[d])
        lam = lam_ref[d]
        softplus = jnp.maximum(-lam, 0.0) + jnp.log1p(jnp.exp(-jnp.abs(lam)))
        log_a = -LRU_C * r * softplus
        a = jnp.exp(log_a)
        bt = jnp.sqrt(-jnp.tanh(log_a) * (a * a + 1.0)) * (gi * xc)
        sft = 1
        while sft < tc:
            if d == 0:
                keep = row >= sft
                a_s = jnp.where(keep, pltpu.roll(a, sft, axis=0), 1.0)
                b_s = jnp.where(keep, pltpu.roll(bt, sft, axis=0), 0.0)
            else:
                keep = row < tc - sft
                a_s = jnp.where(keep, pltpu.roll(a, tc - sft, axis=0), 1.0)
                b_s = jnp.where(keep, pltpu.roll(bt, tc - sft, axis=0), 0.0)
            bt = a * b_s + bt
            a = a * a_s
            sft *= 2
        h = a * carry + bt
        yo = pl.multiple_of(y_off + s, 8)
        if d == 0:
            y_ref[pl.ds(yo, tc), :] = h
            return h[tc - 1:tc]
        y_ref[pl.ds(yo, tc), :] = y_ref[pl.ds(yo, tc), :] + h
        return h[0:1]

    n_lat = seq // tc
    n_ctx = ctx_len // tc
    for d in range(2):
        carry = jnp.zeros((1, width), F32)
        order = range(n_ctx) if d == 0 else range(n_ctx - 1, -1, -1)
        for c in order:
            carry = chunk(ctx_off, seq, c * tc, carry, d)

        def body(j, cr, d=d):
            jj = j if d == 0 else n_lat - 1 - j
            return chunk(lat_off, 0, jj * tc, cr, d)

        lax.fori_loop(0, n_lat, body, carry)
    o_ref[0] = (y_ref[...] * _gelu_tanh(g_ref[0])).astype(BF16)


def _block_diag(w):
    nd, nb, c, _ = w.shape
    out = jnp.zeros((nd, nb * c, nb * c), w.dtype)
    for n in range(nb):
        out = out.at[:, n * c:(n + 1) * c, n * c:(n + 1) * c].set(w[:, n])
    return out


def _lru_mixer(lu, lg, lp, seq):
    b, nt, w = lu.shape
    ctx_len = nt - seq
    row3 = lambda a: a.reshape(2, 1, w)
    args = (_pad_to(lp['lru_conv_w'], 8, 1), row3(lp['lru_conv_b']), _block_diag(lp['lru_wa']).astype(BF16),
            row3(lp['lru_ba']), _block_diag(lp['lru_wx']).astype(BF16), row3(lp['lru_bx']), row3(lp['lru_lambda']))
    full = lambda a: pl.BlockSpec(a.shape, lambda bb: (0,) * a.ndim)
    tok = pl.BlockSpec((1, nt, w), lambda bb: (bb, 0, 0))
    return pl.pallas_call(
        functools.partial(_lru_kernel, seq=seq, ctx_len=ctx_len),
        grid=(b,),
        in_specs=[tok, tok] + [full(a) for a in args],
        out_specs=tok,
        out_shape=jax.ShapeDtypeStruct((b, nt, w), BF16),
        scratch_shapes=[pltpu.VMEM((nt + 3 * LRU_HALO, w), F32), pltpu.VMEM((nt, w), F32)],
        compiler_params=_cparams("parallel"),
        name="rglru_scan",
    )(lu, lg, *args)


def _merge_kernel(x_ref, mod_ref, g1_ref, a_ref, b_ref, c_ref, d_ref, wg_ref, wa_ref, wb_ref, wc_ref, wd_ref,
                  wo_ref, o_ref):
    group, tm, dm = x_ref.shape
    m = mod_ref[:, 0]
    x = x_ref[...]
    h = _normmod(x, g1_ref[...], m[:, 1:2], m[:, 0:1]).reshape(group * tm, dm).astype(BF16)
    acc = None
    for k, (br, w) in enumerate(((a_ref, wa_ref), (b_ref, wb_ref), (c_ref, wc_ref), (d_ref, wd_ref))):
        gate = _sigmoid(_dot(h, wg_ref[:, k * dm:(k + 1) * dm]))
        term = gate * _dot(br[...].reshape(group * tm, br.shape[-1]), w[...])
        acc = term if acc is None else acc + term
    y = _dot(acc.astype(BF16), wo_ref[...])
    o_ref[...] = x + m[:, 2:3] * y.reshape(group, tm, dm)


def _merge(xa, modtab, g1, branches, w_gate, lp, n_lat_tiles, tiles_used):
    b, _, d = xa.shape
    wbr = lp['w_branch']
    head_rows = lambda w, dv: jnp.concatenate(
        [_pad_to(w[hd * dv:(hd + 1) * dv], LANE, 0) for hd in range(4)], axis=0)
    weights = (head_rows(wbr[0], MLA_V).astype(BF16), head_rows(wbr[1], NA_HEAD_DIM).astype(BF16),
               wbr[2].astype(BF16), wbr[3].astype(BF16), lp['w_out'].astype(BF16))
    group = _batch_group(b)
    full = lambda a: pl.BlockSpec(a.shape, lambda bb, i: (0,) * a.ndim)
    tok = lambda n: pl.BlockSpec((group, TM, n), lambda bb, i: (bb, i, 0))
    return pl.pallas_call(
        _merge_kernel,
        grid=(b // group, tiles_used),
        in_specs=[tok(d), pl.BlockSpec((group, 1, 8, d), _kind_map(n_lat_tiles)), full(g1)]
                 + [tok(br.shape[-1]) for br in branches] + [full(w_gate)] + [full(w) for w in weights],
        out_specs=tok(d),
        out_shape=jax.ShapeDtypeStruct((b, tiles_used * TM, d), F32),
        compiler_params=_cparams("parallel", "parallel"),
        name="merge_branches",
    )(xa, modtab, g1, *branches, w_gate, *weights)


U32 = jnp.uint32
EXPERT_ROWS = 512


def _pack_pair(x):
    n = x.shape[-1] // 2
    hi = lax.bitcast_convert_type(x[:, :n].astype(BF16).astype(F32), U32)
    lo = lax.bitcast_convert_type(x[:, n:].astype(BF16).astype(F32), U32)
    return hi | (lo >> 16)


def _unpack_pair(p):
    hi = lax.bitcast_convert_type(p & jnp.uint32(0xFFFF0000), F32)
    lo = lax.bitcast_convert_type(p << 16, F32)
    return hi, lo


def _router_kernel(x_ref, mod_ref, g2_ref, rw_ref, rb_ref, tri_ref, h2_o, idx_o, wts_o, rank_o, cnt_o, carry):
    i = pl.program_id(0)

    @pl.when(i == 0)
    def _():
        carry[...] = jnp.zeros_like(carry)

    m = mod_ref[0, 0]
    h2 = _normmod(x_ref[0], g2_ref[...], m[4:5], m[3:4])
    half = h2.shape[-1] // 2
    h2_o[0] = _pack_pair(h2[:, :half])
    h2_o[1] = _pack_pair(h2[:, half:])
    logits = lax.dot_general(rw_ref[...], h2, (((1,), (1,)), ((), ())), preferred_element_type=F32,
                             precision=lax.Precision.HIGHEST)
    scores = _sigmoid(logits)
    biased = scores + rb_ref[...]
    expert = lax.broadcasted_iota(I32, scores.shape, 0)
    picks = []
    onehot_all = jnp.zeros(scores.shape, F32)
    for _ in range(TOP_K):
        best = jnp.max(biased, axis=0, keepdims=True)
        arg = jnp.min(jnp.where(biased == best, expert, N_EXPERTS), axis=0, keepdims=True)
        hit = expert == arg
        sel = jnp.sum(jnp.where(hit, scores, 0.0), axis=0, keepdims=True)
        biased = jnp.where(hit, -jnp.inf, biased)
        onehot_all = onehot_all + jnp.where(hit, 1.0, 0.0)
        picks.append((arg, hit, sel))
    total = picks[0][2]
    for _, _, sel in picks[1:]:
        total = total + sel
    earlier = _dot(onehot_all.astype(BF16), tri_ref[...]) + carry[...]
    pad_rows = TOPK_PAD - TOP_K
    ranks = [jnp.sum(jnp.where(hit, earlier, 0.0), axis=0, keepdims=True).astype(I32) for _, hit, _ in picks]
    scale = ROUTED_SCALE / total
    idx_o[...] = jnp.concatenate([arg for arg, _, _ in picks] + [jnp.zeros((pad_rows, TM), I32)], axis=0)
    wts_o[...] = jnp.concatenate([sel * scale for _, _, sel in picks] + [jnp.zeros((pad_rows, TM), F32)], axis=0)
    rank_o[...] = jnp.concatenate(ranks + [jnp.zeros((pad_rows, TM), I32)], axis=0)
    carry[...] = carry[...] + jnp.sum(onehot_all, axis=1, keepdims=True)
    cnt_o[...] = carry[...]


def _tile_maps(tiles_used, n_lat_tiles):
    tok = lambda i: (i // tiles_used, i % tiles_used, 0)
    mod = lambda i: (i // tiles_used, jnp.where(i % tiles_used >= n_lat_tiles, 1, 0), 0, 0)
    return tok, mod


def _route(x1, modtab, g2, lp, n_lat_tiles, tiles_used):
    b, _, d = x1.shape
    n_tiles = b * tiles_used
    t = n_tiles * TM
    rw = lp['router_w'].T
    rb = lp['router_bias'].reshape(-1, 1)
    tri = (np.arange(TM)[:, None] < np.arange(TM)[None, :]).astype(np.float32)
    tri = jnp.asarray(tri, BF16)
    per_tok = lambda: pl.BlockSpec((TOPK_PAD, TM), lambda i: (0, i))
    full = lambda a: pl.BlockSpec(a.shape, lambda i: (0,) * a.ndim)
    tok, mod = _tile_maps(tiles_used, n_lat_tiles)
    return pl.pallas_call(
        _router_kernel,
        grid=(n_tiles,),
        in_specs=[pl.BlockSpec((1, TM, d), tok), pl.BlockSpec((1, 1, 8, d), mod),
                  full(g2), full(rw), full(rb), full(tri)],
        out_specs=[pl.BlockSpec((2, TM, d // 4), lambda i: (0, i, 0)), per_tok(), per_tok(), per_tok(),
                   pl.BlockSpec((N_EXPERTS, 1), lambda i: (0, 0))],
        out_shape=[jax.ShapeDtypeStruct((2, t, d // 4), U32), jax.ShapeDtypeStruct((TOPK_PAD, t), I32),
                   jax.ShapeDtypeStruct((TOPK_PAD, t), F32), jax.ShapeDtypeStruct((TOPK_PAD, t), I32),
                   jax.ShapeDtypeStruct((N_EXPERTS, 1), F32)],
        scratch_shapes=[pltpu.VMEM((N_EXPERTS, 1), F32)],
        compiler_params=_cparams("arbitrary"),
        name="moe_router",
    )(x1, modtab, g2, rw, rb, tri)


SC_WINDOW = 128


def _sc_mesh():
    return plsc.VectorSubcoreMesh(core_axis_name="c", subcore_axis_name="s")


def _sc_scatter_rows(src, idx, n_out):
    n, width = src.shape
    k_rep = idx.shape[0]
    half = n // SC_WINDOW // 2

    @functools.partial(pl.kernel, out_type=jax.ShapeDtypeStruct((n_out, width), src.dtype), mesh=_sc_mesh(),
                       scratch_types=[], name="moe_dispatch_sc")
    def scatter(src_hbm, idx_hbm, out_hbm):
        def body(x_vmem, *i_vmems):
            for i_vmem in i_vmems:
                pltpu.sync_copy(x_vmem, out_hbm.at[i_vmem.at[0]])

        pltpu.emit_pipeline(
            body,
            grid=(2, half),
            in_specs=[pl.BlockSpec((SC_WINDOW, width), lambda a, i: (a * half + i, 0))]
                     + [pl.BlockSpec((1, SC_WINDOW), lambda a, i, k=k: (k, a * half + i)) for k in range(k_rep)],
            out_specs=[],
            core_axis_name=("c", "s"),
            dimension_semantics=(pltpu.PARALLEL, pltpu.PARALLEL),
        )(src_hbm, *([idx_hbm] * k_rep))

    return scatter(src, idx)


def _sc_gather_rows(src, idx):
    k_rep, n = idx.shape
    width = src.shape[1]
    n_win = n // SC_WINDOW

    @functools.partial(pl.kernel, out_type=jax.ShapeDtypeStruct((k_rep * n, width), src.dtype), mesh=_sc_mesh(),
                       scratch_types=[], name="moe_gather_sc")
    def gather(src_hbm, idx_hbm, out_hbm):
        def body(i_vmem, o_vmem):
            pltpu.sync_copy(src_hbm.at[i_vmem.at[0]], o_vmem)

        pltpu.emit_pipeline(
            body,
            grid=(k_rep, n_win),
            in_specs=[pl.BlockSpec((1, SC_WINDOW), lambda k, i: (k, i))],
            out_specs=[pl.BlockSpec((SC_WINDOW, width), lambda k, i: (k * n_win + i, 0))],
            core_axis_name=("c", "s"),
            dimension_semantics=(pltpu.PARALLEL, pltpu.PARALLEL),
        )(idx_hbm, out_hbm)

    return gather(src, idx)


def _unpack_planes(p0, p1):
    return _unpack_pair(p0) + _unpack_pair(p1)


def _dot_quarters(parts, w_ref):
    q = parts[0].shape[-1]
    acc = None
    for j, part in enumerate(parts):
        term = _dot(part.astype(BF16), w_ref[j * q:(j + 1) * q, :])
        acc = term if acc is None else acc + term
    return acc


def _expert_kernel(be_ref, nv_ref, xs_ref, wg_ref, wu_ref, wd_ref, ys_o, wg_s, wu_s, wd_s):
    i = pl.program_id(0)
    prev = be_ref[jnp.maximum(i - 1, 0)]

    @pl.when((i == 0) | (be_ref[i] != prev))
    def _():
        wg_s[...] = wg_ref[0].astype(BF16)
        wu_s[...] = wu_ref[0].astype(BF16)
        wd_s[...] = wd_ref[0].astype(BF16)

    @pl.when(nv_ref[i] > 0)
    def _():
        keep = lax.broadcasted_iota(I32, xs_ref.shape[1:], 0) < nv_ref[i]
        parts = _unpack_planes(jnp.where(keep, xs_ref[0], jnp.uint32(0)), jnp.where(keep, xs_ref[1], jnp.uint32(0)))
        hid = _silu(_dot_quarters(parts, wg_s)) * _dot_quarters(parts, wu_s)
        y = _dot(hid.astype(BF16), wd_s[...])
        half = y.shape[-1] // 2
        ys_o[0] = _pack_pair(y[:, :half])
        ys_o[1] = _pack_pair(y[:, half:])

    @pl.when(nv_ref[i] <= 0)
    def _():
        ys_o[...] = jnp.zeros_like(ys_o)


def _experts(xs, block_e, n_valid, weights, layer):
    _, n_rows, dq = xs.shape
    d = 4 * dq
    n_blocks = n_rows // EXPERT_ROWS
    hid = EXPERT_HIDDEN
    grid_spec = pltpu.PrefetchScalarGridSpec(
        num_scalar_prefetch=2,
        grid=(n_blocks,),
        in_specs=[pl.BlockSpec((2, EXPERT_ROWS, dq), lambda i, be, nv: (0, i, 0)),
                  pl.BlockSpec((None, 1, d, hid), lambda i, be, nv: (layer, be[i], 0, 0)),
                  pl.BlockSpec((None, 1, d, hid), lambda i, be, nv: (layer, be[i], 0, 0)),
                  pl.BlockSpec((None, 1, hid, d), lambda i, be, nv: (layer, be[i], 0, 0))],
        out_specs=pl.BlockSpec((2, EXPERT_ROWS, dq), lambda i, be, nv: (0, i, 0)),
        scratch_shapes=[pltpu.VMEM((d, hid), BF16), pltpu.VMEM((d, hid), BF16), pltpu.VMEM((hid, d), BF16)],
    )
    return pl.pallas_call(
        _expert_kernel,
        grid_spec=grid_spec,
        out_shape=jax.ShapeDtypeStruct((2, n_rows, dq), U32),
        compiler_params=_cparams("arbitrary"),
        name="moe_experts",
    )(block_e, n_valid, xs, *weights)


def _combine_kernel(g_ref, wts_ref, h2_ref, x_ref, mod_ref, sg_ref, su_ref, sd_ref, gf_ref, o_ref, *, final):
    parts = _unpack_planes(h2_ref[0], h2_ref[1])
    hid = _silu(_dot_quarters(parts, sg_ref)) * _dot_quarters(parts, su_ref)
    shared = _dot(hid.astype(BF16), sd_ref[...])
    wts = wts_ref[...]
    routed = None
    for k in range(TOP_K):
        w = wts[:, k:k + 1]
        terms = [w * part for part in _unpack_planes(g_ref[k, 0], g_ref[k, 1])]
        routed = terms if routed is None else [r + t for r, t in zip(routed, terms)]
    f = shared + jnp.concatenate(routed, axis=1)
    m = mod_ref[0, 0]
    x2 = x_ref[0] + m[5:6] * f
    if final:
        x2 = x2 * lax.rsqrt(jnp.mean(x2 * x2, axis=-1, keepdims=True) + NORM_EPS) * gf_ref[...]
    o_ref[0] = x2


def _combine(g, wts, h2p, x1, modtab, lp, g_final, n_lat_tiles, tiles_used, final):
    b, _, d = x1.shape
    dq = d // 4
    weights = (lp['sh_w_gate'].astype(BF16), lp['sh_w_up'].astype(BF16), lp['sh_w_down'].astype(BF16),
               g_final.reshape(1, -1))
    full = lambda a: pl.BlockSpec(a.shape, lambda i: (0,) * a.ndim)
    tok, mod = _tile_maps(tiles_used, n_lat_tiles)
    return pl.pallas_call(
        functools.partial(_combine_kernel, final=final),
        grid=(b * tiles_used,),
        in_specs=[pl.BlockSpec((TOP_K, 2, TM, dq), lambda i: (0, 0, i, 0)),
                  pl.BlockSpec((TM, TOPK_PAD), lambda i: (i, 0)),
                  pl.BlockSpec((2, TM, dq), lambda i: (0, i, 0)),
                  pl.BlockSpec((1, TM, d), tok), pl.BlockSpec((1, 1, 8, d), mod)]
                 + [full(w) for w in weights],
        out_specs=pl.BlockSpec((1, TM, d), tok),
        out_shape=jax.ShapeDtypeStruct((b, tiles_used * TM, d), F32),
        compiler_params=_cparams("parallel"),
        name="moe_combine",
    )(g, wts, h2p, x1, modtab, *weights)


def _moe(x1, modtab, g2, lp, g_final, n_lat_tiles, final):
    b, nt, d = x1.shape
    tiles_used = n_lat_tiles if final else nt // TM
    t = b * tiles_used * TM
    h2p, idx, wts, rank, cnt = _route(x1, modtab, g2, lp, n_lat_tiles, tiles_used)
    counts = cnt[:, 0].astype(I32)
    padded = (counts + EXPERT_ROWS - 1) // EXPERT_ROWS * EXPERT_ROWS
    p_ends = jnp.cumsum(padded)
    p_starts = p_ends - padded
    n_blocks = (t * TOP_K + N_EXPERTS * (EXPERT_ROWS - 1)) // EXPERT_ROWS
    n_rows = n_blocks * EXPERT_ROWS
    chosen = idx[:TOP_K, None, :] == jnp.arange(N_EXPERTS, dtype=I32)[None, :, None]
    dest = jnp.sum(jnp.where(chosen, p_starts[None, :, None], 0), axis=1) + rank[:TOP_K]
    plane_idx = (dest[:, None, :] + (jnp.arange(2, dtype=I32) * n_rows)[None, :, None]).reshape(TOP_K, 2 * t)
    blk_start = jnp.arange(n_blocks, dtype=I32) * EXPERT_ROWS
    block_e = jnp.minimum(jnp.sum((p_ends[None, :] <= blk_start[:, None]).astype(I32), axis=1), N_EXPERTS - 1)
    n_valid = jnp.clip((p_starts + counts)[block_e] - blk_start, 0, EXPERT_ROWS).astype(I32)
    dq = d // 4
    xs = _sc_scatter_rows(h2p.reshape(2 * t, dq), plane_idx, 2 * n_rows).reshape(2, n_rows, dq)
    ys = _experts(xs, block_e, n_valid, lp['expert_stacks'], lp['layer'])
    g = _sc_gather_rows(ys.reshape(2 * n_rows, dq), plane_idx).reshape(TOP_K, 2, t, dq)
    return _combine(g, wts.T, h2p, x1, modtab, lp, g_final, n_lat_tiles, tiles_used, final)


def _layer(xa, c, c_ctx, lp, consts, g_final, seq, final):
    b, nt, d = xa.shape
    n_lat_tiles = seq // TM
    rope, mats_lat, mats_ctx = consts
    modtab = _mod_table(c, c_ctx, *lp['mod_stacks'], lp['layer'])
    g1 = lp['g_norm1'].reshape(1, -1)
    g2 = lp['g_norm2'].reshape(1, -1)
    pw = _proj_weights(lp)
    q, k, v, nq, nk, nv, hy, lu, lg = _project(xa, modtab, g1, pw, rope, n_lat_tiles)
    tiles_used = n_lat_tiles if final else nt // TM
    br_a = _mla_attention(q, k, v, seq, tiles_used)
    br_b = _na_attention(nq, nk, nv, _na_bias_tables(lp['na_rpb'], seq // GRID_W), seq, tiles_used)
    pre = _hy_pre(hy, lp['hy_short_w'], lp['hy_short_b'], seq)
    lat_f, lat_bf, ctx_f, ctx_bf = pre
    br_c = _hyena_seq(mats_lat, lat_f, lat_bf, lp, b)
    if not final:
        br_c = jnp.concatenate([br_c, _hyena_seq(mats_ctx, ctx_f, ctx_bf, lp, b)], axis=1)
    br_d = _lru_mixer(lu, lg, lp, seq)
    x1 = _merge(xa, modtab, g1, (br_a, br_b, br_c, br_d), pw['w_gate'], lp, n_lat_tiles, tiles_used)
    return _moe(x1, modtab, g2, lp, g_final, n_lat_tiles, final)


_LAYER_KEYS = ('w_mod', 'b_mod', 'g_norm1', 'g_norm2', 'w_in', 'mla_g_q', 'mla_w_uq', 'mla_g_kv', 'mla_w_ukv',
               'na_rpb', 'hy_short_w', 'hy_short_b', 'hy_w1', 'hy_b1', 'hy_w2', 'hy_b2', 'hy_w3', 'hy_decay',
               'hy_bias', 'lru_conv_w', 'lru_conv_b', 'lru_wa', 'lru_ba', 'lru_wx', 'lru_bx', 'lru_lambda',
               'w_branch', 'w_out', 'router_w', 'router_bias', 'exp_w_gate', 'exp_w_up', 'exp_w_down',
               'sh_w_gate', 'sh_w_up', 'sh_w_down')


def kernel(x, c, ctx, c_ctx, w_mod, b_mod, g_norm1, g_norm2, w_in, mla_g_q, mla_w_uq, mla_g_kv, mla_w_ukv, na_rpb, hy_short_w, hy_short_b, hy_w1, hy_b1, hy_w2, hy_b2, hy_w3, hy_decay, hy_bias, lru_conv_w, lru_conv_b, lru_wa, lru_ba, lru_wx, lru_bx, lru_lambda, w_branch, w_out, router_w, router_bias, exp_w_gate, exp_w_up, exp_w_down, sh_w_gate, sh_w_up, sh_w_down, g_final):
    stacked = dict(zip(_LAYER_KEYS, (w_mod, b_mod, g_norm1, g_norm2, w_in, mla_g_q, mla_w_uq, mla_g_kv, mla_w_ukv,
                                     na_rpb, hy_short_w, hy_short_b, hy_w1, hy_b1, hy_w2, hy_b2, hy_w3, hy_decay,
                                     hy_bias, lru_conv_w, lru_conv_b, lru_wa, lru_ba, lru_wx, lru_bx, lru_lambda,
                                     w_branch, w_out, router_w, router_bias, exp_w_gate, exp_w_up, exp_w_down,
                                     sh_w_gate, sh_w_up, sh_w_down)))
    b, seq, d = x.shape
    ctx_len = ctx.shape[1]
    depth = w_mod.shape[0]
    assert seq % TM == 0 and ctx_len % TM == 0 and seq // GRID_W >= NA_KEY_ROWS + 1
    xa = jnp.concatenate([x, ctx], axis=1)
    consts = (_rope_tables(seq, seq + ctx_len), _dft_matrices(seq), _dft_matrices(ctx_len))
    for i in range(depth):
        big = ('w_mod', 'b_mod', 'exp_w_gate', 'exp_w_up', 'exp_w_down')
        lp = {name: w[i] for name, w in stacked.items() if name not in big}
        lp['layer'] = i
        lp['mod_stacks'] = (w_mod, b_mod)
        lp['expert_stacks'] = (exp_w_gate, exp_w_up, exp_w_down)
        xa = _layer(xa, c, c_ctx, lp, consts, g_final, seq, i == depth - 1)
    return xa
```

```python
import functools
import math

import numpy as np
import jax
import jax.numpy as jnp
from jax import lax
from jax.experimental import pallas as pl
from jax.experimental.pallas import tpu as pltpu
from jax.experimental.pallas import tpu_sc as plsc

F32 = jnp.float32
BF16 = jnp.bfloat16
I32 = jnp.int32

TM = 256
LANE = 128
GRID_W = 64
N_MOD = 6
NORM_EPS = 1e-6

MLA_HEADS, MLA_NOPE, MLA_ROPE, MLA_V = 4, 64, 32, 64
MLA_Q_RANK, MLA_KV_RANK = 192, 128
MLA_Q_PAD = 256
ROPE_THETA = 10000.0

NA_HEADS, NA_HEAD_DIM, NA_WIN_R, NA_WIN_C = 4, 64, 8, 16
NA_TILE_ROWS = TM // GRID_W
NA_KEY_ROWS = NA_TILE_ROWS + NA_WIN_R - 1
NA_KEYS = NA_KEY_ROWS * GRID_W

HY_WIDTH, HY_ORDER, HY_SHORT, HY_BANDS, HY_FFN = 256, 2, 3, 16, 64
HY_EMB = 2 * HY_BANDS + 1

LRU_WIDTH, LRU_BLOCKS, LRU_CONV, LRU_C = 256, 4, 4, 8.0
LRU_CHUNK = 256
LRU_HALO = 8

N_EXPERTS, TOP_K, EXPERT_HIDDEN, ROUTED_SCALE, MOE_BLOCK = 64, 6, 256, 2.5, 256
TOPK_PAD = 8

VMEM_LIMIT = 52 * 1024 * 1024


def _cparams(*sem):
    return pltpu.CompilerParams(dimension_semantics=sem, vmem_limit_bytes=VMEM_LIMIT)


def _dot(a, b):
    return jnp.dot(a, b, preferred_element_type=F32)


def _dot_nt(a, b):
    return lax.dot_general(a, b, (((1,), (1,)), ((), ())), preferred_element_type=F32)


def _sigmoid(x):
    return jax.nn.sigmoid(x)


def _silu(x):
    return x * _sigmoid(x)


def _gelu_tanh(x):
    return 0.5 * x * (1.0 + jnp.tanh(math.sqrt(2.0 / math.pi) * (x + 0.044715 * (x * x * x))))


def _normmod(x, g, scale, shift):
    y = x * lax.rsqrt(jnp.mean(x * x, axis=-1, keepdims=True) + NORM_EPS) * g
    return y * (1.0 + scale) + shift


def _mod_kernel(c_ref, w_ref, b_ref, o_ref):
    s = _silu(c_ref[...])
    o_ref[...] = _dot(s.astype(BF16), w_ref[...].astype(BF16)) + b_ref[...]


def _mod_table(c, c_ctx, w_mod, b_mod, layer):
    b, d = c.shape
    rows = 16
    cc = jnp.zeros((rows, d), F32).at[:b].set(c).at[b].set(c_ctx)
    tn = 1024
    mod = pl.pallas_call(
        _mod_kernel,
        grid=(N_MOD * d // tn,),
        in_specs=[pl.BlockSpec((rows, d), lambda j: (0, 0)),
                  pl.BlockSpec((None, d, tn), lambda j: (layer, 0, j)),
                  pl.BlockSpec((None, 1, tn), lambda j: (layer, 0, j))],
        out_specs=pl.BlockSpec((rows, tn), lambda j: (0, j)),
        out_shape=jax.ShapeDtypeStruct((rows, N_MOD * d), F32),
        compiler_params=_cparams("arbitrary"),
        name="mod_vectors",
    )(cc, w_mod, b_mod[:, None, :])
    lat = mod[:b].reshape(b, N_MOD, d)
    ctx = jnp.broadcast_to(mod[b].reshape(1, N_MOD, d), (b, N_MOD, d))
    tab = jnp.stack([lat, ctx], axis=1)
    return jnp.pad(tab, ((0, 0), (0, 0), (0, 8 - N_MOD), (0, 0)))


_C_QLAT, _C_KVLAT, _C_KR, _C_KRR, _C_NA, _C_HY, _C_LU, _C_LG, _C_END = (
    0, 256, 384, 512, 640, 640 + 3 * NA_HEADS * LANE, 640 + 1536 + 768, 640 + 1536 + 1024, 640 + 1536 + 1280)


def _proj_kernel(x_ref, mod_ref, g1_ref, w1_ref, gq_ref, gkv_ref, wuq_ref, wuqr_ref, wk_ref, wv_ref,
                 cos_ref, sin_ref, q_o, k_o, v_o, nq_o, nk_o, nv_o, hy_o, lu_o, lg_o):
    group, tm, d = x_ref.shape
    m = mod_ref[:, 0]
    h = _normmod(x_ref[...], g1_ref[...], m[:, 1:2], m[:, 0:1]).reshape(group * tm, d)
    z = _dot(h.astype(BF16), w1_ref[...])
    qlat = z[:, _C_QLAT:_C_KVLAT]
    kvlat = z[:, _C_KVLAT:_C_KR]
    kr = z[:, _C_KR:_C_KRR]
    krr = z[:, _C_KRR:_C_NA]
    qn = qlat * lax.rsqrt(jnp.sum(qlat * qlat, axis=-1, keepdims=True) * (1.0 / MLA_Q_RANK) + NORM_EPS) * gq_ref[...]
    kvn = kvlat * lax.rsqrt(jnp.mean(kvlat * kvlat, axis=-1, keepdims=True) + NORM_EPS) * gkv_ref[...]
    qn = qn.astype(BF16)
    kvn = kvn.astype(BF16)
    q = _dot(qn, wuq_ref[...])
    qr = _dot(qn, wuqr_ref[...])
    kk = _dot(kvn, wk_ref[...])
    vv = _dot(kvn, wv_ref[...])
    cos = jnp.concatenate([cos_ref[...]] * group, axis=0)
    sin = jnp.concatenate([sin_ref[...]] * group, axis=0)
    krope = kr * cos + krr * sin
    den_lane = lax.broadcasted_iota(I32, cos.shape, 1) == SOFTMAX_DEN_LANE
    def put_heads(ref, hd, val):
        for g in range(group):
            ref[g, hd] = val[g * tm:(g + 1) * tm].astype(BF16)

    def put_pairs(ref, hd, val):
        off = (hd % 2) * LANE
        for g in range(group):
            ref[g, hd // 2, :, off:off + LANE] = val[g * tm:(g + 1) * tm].astype(BF16)

    for hd in range(MLA_HEADS):
        sl = slice(hd * LANE, (hd + 1) * LANE)
        put_heads(q_o, hd, q[:, sl] * cos + qr[:, sl] * sin)
        put_heads(k_o, hd, kk[:, sl] + krope)
        put_heads(v_o, hd, jnp.where(den_lane, 1.0, vv[:, sl]))
    for hd in range(NA_HEADS):
        for which, ref in enumerate((nq_o, nk_o, nv_o)):
            lo = _C_NA + (which * NA_HEADS + hd) * LANE
            blk = z[:, lo:lo + LANE]
            if which == 2:
                put_pairs(ref, hd, jnp.where(den_lane, 1.0, blk))
            else:
                put_heads(ref, hd, blk)
    hy_o[...] = z[:, _C_HY:_C_LU].reshape(group, tm, _C_LU - _C_HY)
    lu_o[...] = z[:, _C_LU:_C_LG].reshape(group, tm, _C_LG - _C_LU)
    lg_o[...] = z[:, _C_LG:_C_END].reshape(group, tm, _C_END - _C_LG)


def _pad_to(a, n, axis):
    pad = [(0, 0)] * a.ndim
    pad[axis] = (0, n - a.shape[axis])
    return jnp.pad(a, pad)


def _rot_cols(w):
    half = w.shape[-1] // 2
    return jnp.concatenate([-w[..., half:], w[..., :half]], axis=-1)


def _head_blocks(cols_per_head):
    out = []
    for pieces in cols_per_head:
        k = pieces[0][0].shape[0]
        blk = jnp.zeros((k, LANE), F32)
        for arr, off in pieces:
            blk = blk.at[:, off:off + arr.shape[1]].set(arr)
        out.append(blk)
    return jnp.concatenate(out, axis=1)


def _proj_weights(lp):
    w_in = lp['w_in']
    d = w_in.shape[0]
    o = 0
    parts = {}
    for name, n in (('q', MLA_Q_RANK), ('kv', MLA_KV_RANK), ('kr', MLA_ROPE), ('na', 3 * NA_HEADS * NA_HEAD_DIM),
                    ('hy', 3 * HY_WIDTH), ('lu', LRU_WIDTH), ('lg', LRU_WIDTH), ('gt', 4 * d)):
        parts[name] = w_in[:, o:o + n]
        o += n
    zeros = lambda n: jnp.zeros((d, n), F32)
    kr_blk = jnp.concatenate([zeros(MLA_NOPE), parts['kr'], zeros(LANE - MLA_NOPE - MLA_ROPE)], axis=1)
    krr_blk = jnp.concatenate([zeros(MLA_NOPE), _rot_cols(parts['kr']), zeros(LANE - MLA_NOPE - MLA_ROPE)], axis=1)
    na_scale = NA_HEAD_DIM ** -0.5
    na_cols = []
    for which in range(3):
        for hd in range(NA_HEADS):
            lo = (which * NA_HEADS + hd) * NA_HEAD_DIM
            blk = parts['na'][:, lo:lo + NA_HEAD_DIM] * (na_scale if which == 0 else 1.0)
            na_cols.append(_pad_to(blk, LANE, 1))
    w1 = jnp.concatenate([_pad_to(parts['q'], MLA_Q_PAD, 1), parts['kv'], kr_blk, krr_blk] + na_cols
                         + [parts['hy'], parts['lu'], parts['lg']], axis=1).astype(BF16)
    mla_scale = (MLA_NOPE + MLA_ROPE) ** -0.5
    wuq = _pad_to(lp['mla_w_uq'], MLA_Q_PAD, 0) * mla_scale
    dq = MLA_NOPE + MLA_ROPE
    wuq_main = _head_blocks([[(wuq[:, hd * dq:hd * dq + dq], 0)] for hd in range(MLA_HEADS)])
    wuq_rot = _head_blocks([[(_rot_cols(wuq[:, hd * dq + MLA_NOPE:hd * dq + dq]), MLA_NOPE)]
                            for hd in range(MLA_HEADS)])
    dkv = MLA_NOPE + MLA_V
    wukv = lp['mla_w_ukv']
    wk = _head_blocks([[(wukv[:, hd * dkv:hd * dkv + MLA_NOPE], 0)] for hd in range(MLA_HEADS)])
    wv = _head_blocks([[(wukv[:, hd * dkv + MLA_NOPE:hd * dkv + dkv], 0)] for hd in range(MLA_HEADS)])
    gq = _pad_to(lp['mla_g_q'].reshape(1, -1), MLA_Q_PAD, 1)
    gkv = lp['mla_g_kv'].reshape(1, -1)
    return dict(w1=w1, w_gate=parts['gt'].astype(BF16), gq=gq, gkv=gkv, wuq=wuq_main.astype(BF16),
                wuq_rot=wuq_rot.astype(BF16), wk=wk.astype(BF16), wv=wv.astype(BF16))


def _rope_tables(seq, n_tok):
    t = jnp.arange(seq, dtype=I32)
    row = (t // GRID_W).astype(F32)
    col = (t % GRID_W).astype(F32)
    n_axis = MLA_ROPE // 4
    inv_freq = ROPE_THETA ** (-jnp.arange(n_axis, dtype=F32) / n_axis)
    ang = jnp.concatenate([row[:, None] * inv_freq, col[:, None] * inv_freq], axis=-1)
    cos = jnp.concatenate([jnp.cos(ang), jnp.cos(ang)], axis=-1)
    sin = jnp.concatenate([jnp.sin(ang), jnp.sin(ang)], axis=-1)
    cos_t = jnp.ones((n_tok, LANE), F32).at[:seq, MLA_NOPE:MLA_NOPE + MLA_ROPE].set(cos)
    sin_t = jnp.zeros((n_tok, LANE), F32).at[:seq, MLA_NOPE:MLA_NOPE + MLA_ROPE].set(sin)
    return cos_t, sin_t


def _batch_group(b):
    return 2 if b % 2 == 0 else 1


def _kind_map(n_lat_tiles):
    return lambda b, i: (b, jnp.where(i >= n_lat_tiles, 1, 0), 0, 0)


def _project(xa, modtab, g1, pw, rope, n_lat_tiles):
    b, nt, d = xa.shape
    cos_t, sin_t = rope
    group = _batch_group(b)
    full = lambda a: pl.BlockSpec(a.shape, lambda bb, i: (0,) * a.ndim)
    head_out = lambda: pl.BlockSpec((group, MLA_HEADS, TM, LANE), lambda bb, i: (bb, 0, i, 0))
    tok_out = lambda n: pl.BlockSpec((group, TM, n), lambda bb, i: (bb, i, 0))
    head_shape = jax.ShapeDtypeStruct((b, MLA_HEADS, nt, LANE), BF16)
    pair_out = lambda: pl.BlockSpec((group, MLA_HEADS // 2, TM, 2 * LANE), lambda bb, i: (bb, 0, i, 0))
    pair_shape = jax.ShapeDtypeStruct((b, MLA_HEADS // 2, nt, 2 * LANE), BF16)
    tok_shape = lambda n: jax.ShapeDtypeStruct((b, nt, n), F32)
    weights = (g1, pw['w1'], pw['gq'], pw['gkv'], pw['wuq'], pw['wuq_rot'], pw['wk'], pw['wv'])
    return pl.pallas_call(
        _proj_kernel,
        grid=(b // group, nt // TM),
        in_specs=[pl.BlockSpec((group, TM, d), lambda bb, i: (bb, i, 0)),
                  pl.BlockSpec((group, 1, 8, d), _kind_map(n_lat_tiles))]
                 + [full(w) for w in weights]
                 + [pl.BlockSpec((TM, LANE), lambda bb, i: (i, 0))] * 2,
        out_specs=[head_out(), head_out(), head_out(), head_out(), head_out(), pair_out(),
                   tok_out(3 * HY_WIDTH), tok_out(LRU_WIDTH), tok_out(LRU_WIDTH)],
        out_shape=[head_shape, head_shape, head_shape, head_shape, head_shape, pair_shape,
                   tok_shape(3 * HY_WIDTH), tok_shape(LRU_WIDTH), tok_shape(LRU_WIDTH)],
        compiler_params=_cparams("parallel", "parallel"),
        name="input_projection",
    )(xa, modtab, *weights, cos_t, sin_t)


SOFTMAX_DEN_LANE = 64


def _softmax_pv(parts, lane_off=0):
    m = None
    for s, _ in parts:
        mm = jnp.max(s, axis=-1, keepdims=True)
        m = mm if m is None else jnp.maximum(m, mm)
    acc = None
    for s, v in parts:
        o = _dot(jnp.exp(s - m).astype(BF16), v)
        acc = o if acc is None else acc + o
    den = lane_off + SOFTMAX_DEN_LANE
    return acc[:, lane_off:lane_off + LANE] / acc[:, den:den + 1]


def _mla_kernel(q_ref, k_ref, v_ref, o_ref, *, seq, n_lat_tiles):
    i = pl.program_id(1)
    nt = k_ref.shape[2]

    def attend(lo, hi):
        for hd in range(MLA_HEADS):
            s = _dot_nt(q_ref[0, hd], k_ref[0, hd, lo:hi, :])
            o = _softmax_pv([(s, v_ref[0, hd, lo:hi, :])])
            o_ref[0, :, hd * LANE:(hd + 1) * LANE] = o.astype(BF16)

    @pl.when(i < n_lat_tiles)
    def _():
        attend(0, nt)

    @pl.when(i >= n_lat_tiles)
    def _():
        attend(seq, nt)


def _mla_attention(q, k, v, seq, tiles_used):
    b, h, nt, _ = q.shape
    kv_spec = pl.BlockSpec((1, h, nt, LANE), lambda bb, i: (bb, 0, 0, 0))
    return pl.pallas_call(
        functools.partial(_mla_kernel, seq=seq, n_lat_tiles=seq // TM),
        grid=(b, tiles_used),
        in_specs=[pl.BlockSpec((1, h, TM, LANE), lambda bb, i: (bb, 0, i, 0)), kv_spec, kv_spec],
        out_specs=pl.BlockSpec((1, TM, h * LANE), lambda bb, i: (bb, i, 0)),
        out_shape=jax.ShapeDtypeStruct((b, tiles_used * TM, h * LANE), BF16),
        compiler_params=_cparams("parallel", "arbitrary"),
        name="mla_attention",
    )(q, k, v)


def _na_bias_tables(rpb, rows):
    n_blk = rows // NA_TILE_ROWS
    col = np.arange(GRID_W)
    c0 = np.clip(col - NA_WIN_C // 2, 0, GRID_W - NA_WIN_C)
    in_win = (col[None, :] >= c0[:, None]) & (col[None, :] < c0[:, None] + NA_WIN_C)
    dc = np.clip(col[None, :] - col[:, None], 1 - NA_WIN_C, NA_WIN_C - 1) + NA_WIN_C - 1
    rpb = rpb.astype(F32)
    tables = []
    for j in (0, 1, n_blk - 1):
        w0 = min(max(NA_TILE_ROWS * j - NA_WIN_R // 2, 0), rows - NA_KEY_ROWS)
        r = NA_TILE_ROWS * j + np.arange(NA_TILE_ROWS)
        kr = w0 + np.arange(NA_KEY_ROWS)
        r0 = np.clip(r - NA_WIN_R // 2, 0, rows - NA_WIN_R)
        row_ok = (kr[None, :] >= r0[:, None]) & (kr[None, :] < r0[:, None] + NA_WIN_R)
        dr = np.clip(kr[None, :] - r[:, None] + NA_WIN_R - 1, 0, 2 * NA_WIN_R - 2)
        oh_r = jnp.asarray(np.eye(2 * NA_WIN_R - 1, dtype=np.float32)[dr.reshape(-1)])
        oh_c = jnp.asarray(np.eye(2 * NA_WIN_C - 1, dtype=np.float32)[dc.reshape(-1)])
        bias = jnp.einsum('ar,hrc,bc->hab', oh_r, rpb, oh_c, precision=lax.Precision.HIGHEST)
        bias = bias.reshape(NA_HEADS, NA_TILE_ROWS, NA_KEY_ROWS, GRID_W, GRID_W)
        mask = row_ok[:, :, None, None] & in_win[None, None, :, :]
        bias = jnp.where(jnp.asarray(mask)[None], bias, -jnp.inf)
        tables.append(bias.transpose(0, 1, 3, 2, 4).reshape(NA_HEADS, TM, NA_KEYS))
    return jnp.stack(tables)


def _na_kernel(q_ref, k_ref, v_ref, bias_ref, o_ref, *, seq, n_lat_tiles):
    i = pl.program_id(1)
    nt = k_ref.shape[2]
    rows = seq // GRID_W

    @pl.when(i < n_lat_tiles)
    def _():
        w0 = jnp.clip(NA_TILE_ROWS * i - NA_WIN_R // 2, 0, rows - NA_KEY_ROWS)
        start = pl.multiple_of(w0 * GRID_W, GRID_W)
        for hd in range(NA_HEADS):
            q = q_ref[0, hd]
            s_loc = _dot_nt(q, k_ref[0, hd, pl.ds(start, NA_KEYS), :]) + bias_ref[0, hd]
            s_ctx = _dot_nt(q, k_ref[0, hd, seq:nt, :])
            o = _softmax_pv([(s_loc, v_ref[0, hd // 2, pl.ds(start, NA_KEYS), :]),
                             (s_ctx, v_ref[0, hd // 2, seq:nt, :])], (hd % 2) * LANE)
            o_ref[0, :, hd * LANE:(hd + 1) * LANE] = o.astype(BF16)

    @pl.when(i >= n_lat_tiles)
    def _():
        for hd in range(NA_HEADS):
            s = _dot_nt(q_ref[0, hd], k_ref[0, hd, seq:nt, :])
            o = _softmax_pv([(s, v_ref[0, hd // 2, seq:nt, :])], (hd % 2) * LANE)
            o_ref[0, :, hd * LANE:(hd + 1) * LANE] = o.astype(BF16)


def _na_attention(q, k, v, bias, seq, tiles_used):
    b, h, nt, _ = q.shape
    n_lat = seq // TM
    kv_spec = pl.BlockSpec((1, h, nt, LANE), lambda bb, i: (bb, 0, 0, 0))
    cfg = lambda bb, i: (jnp.where(i == 0, 0, jnp.where(i >= n_lat - 1, 2, 1)), 0, 0, 0)
    return pl.pallas_call(
        functools.partial(_na_kernel, seq=seq, n_lat_tiles=n_lat),
        grid=(b, tiles_used),
        in_specs=[pl.BlockSpec((1, h, TM, LANE), lambda bb, i: (bb, 0, i, 0)), kv_spec,
                  pl.BlockSpec((1, h // 2, nt, 2 * LANE), lambda bb, i: (bb, 0, 0, 0)),
                  pl.BlockSpec((1, h, TM, NA_KEYS), cfg)],
        out_specs=pl.BlockSpec((1, TM, h * LANE), lambda bb, i: (bb, i, 0)),
        out_shape=jax.ShapeDtypeStruct((b, tiles_used * TM, h * LANE), BF16),
        compiler_params=_cparams("parallel", "arbitrary"),
        name="neighbourhood_attention",
    )(q, k, v, bias)


HY_PRE_CHUNK = 256


def _hy_pre_kernel(z_ref, w_ref, b_ref, lat_o, lat_bf_o, ctx_o, ctx_bf_o, *, seq, ctx_len):
    w = w_ref[...]
    bias = b_ref[...]
    ch = HY_PRE_CHUNK
    zero = jnp.zeros((8, HY_WIDTH), F32)
    for seg_lo, seg_len, o_ref, obf_ref in ((0, seq, lat_o, lat_bf_o), (seq, ctx_len, ctx_o, ctx_bf_o)):
        for c in range(seg_len // ch):
            s = seg_lo + c * ch
            before = zero if c == 0 else z_ref[0, s - 8:s, :]
            after = zero if c == seg_len // ch - 1 else z_ref[0, s + ch:s + ch + 8, :]
            win = jnp.concatenate([before, z_ref[0, s:s + ch, :], after], axis=0)
            n = ch + 16
            y = bias + w[1:2] * win[8:8 + ch]
            y = y + w[0:1] * pltpu.roll(win, 1, axis=0)[8:8 + ch]
            y = y + w[2:3] * pltpu.roll(win, n - 1, axis=0)[8:8 + ch]
            o_ref[0, c * ch:(c + 1) * ch, :] = y
            obf_ref[0, c * ch:(c + 1) * ch, :] = y.astype(BF16)


def _hy_pre(hy, w, bvec, seq):
    b, nt, _ = hy.shape
    ctx_len = nt - seq
    wpad = _pad_to(w, 8, 0)
    out_specs, out_shape = [], []
    for length in (seq, ctx_len):
        for dt in (F32, BF16):
            out_specs.append(pl.BlockSpec((1, length, HY_WIDTH), lambda bb, g: (g, 0, bb)))
            out_shape.append(jax.ShapeDtypeStruct((3, length, b * HY_WIDTH), dt))
    return pl.pallas_call(
        functools.partial(_hy_pre_kernel, seq=seq, ctx_len=ctx_len),
        grid=(b, 3),
        in_specs=[pl.BlockSpec((1, nt, HY_WIDTH), lambda bb, g: (bb, 0, g)),
                  pl.BlockSpec((8, HY_WIDTH), lambda bb, g: (0, g)),
                  pl.BlockSpec((1, HY_WIDTH), lambda bb, g: (0, g))],
        out_specs=out_specs,
        out_shape=out_shape,
        compiler_params=_cparams("parallel", "parallel"),
        name="hyena_short_conv",
    )(hy, wpad, bvec.reshape(1, -1))


def _hp_dot(a, b):
    return jnp.dot(a, b, preferred_element_type=F32, precision=lax.Precision.HIGHEST)


def _hy_filter_kernel(f_ref, w1_ref, b1_ref, w2_ref, b2_ref, w3_ref, dec_ref, o_ref, obf_ref):
    f = f_ref[...]
    h = jnp.sin(_hp_dot(f, w1_ref[...]) + b1_ref[...])
    h = jnp.sin(_hp_dot(h, w2_ref[...]) + b2_ref[...])
    h = _hp_dot(h, w3_ref[...])
    h = h * jnp.exp(-f[:, 0:1] * jnp.abs(dec_ref[...]))
    o_ref[...] = h
    obf_ref[...] = h.astype(BF16)


def _hy_pos_features(length):
    t = jnp.linspace(0.0, 1.0, length, dtype=F32)[:, None]
    w = 2.0 * math.pi * jnp.arange(length, dtype=F32)[:, None] / length
    f = jnp.linspace(1e-4, HY_BANDS - 1, HY_BANDS, dtype=F32)[None, :]
    z = w * f
    return jnp.concatenate([t, jnp.cos(z), -jnp.sin(z)], axis=-1)


def _hy_filters(length, lp):
    feats = _hy_pos_features(length)
    n_out = HY_ORDER * 2 * HY_WIDTH
    tl = min(length, 512)
    full = lambda a: pl.BlockSpec(a.shape, lambda i: (0,) * a.ndim)
    args = (lp['hy_w1'], lp['hy_b1'].reshape(1, -1), lp['hy_w2'], lp['hy_b2'].reshape(1, -1), lp['hy_w3'],
            lp['hy_decay'].reshape(1, n_out))
    return pl.pallas_call(
        _hy_filter_kernel,
        grid=(length // tl,),
        in_specs=[pl.BlockSpec((tl, HY_EMB), lambda i: (i, 0))] + [full(a) for a in args],
        out_specs=[pl.BlockSpec((tl, n_out), lambda i: (i, 0))] * 2,
        out_shape=[jax.ShapeDtypeStruct((length, n_out), F32), jax.ShapeDtypeStruct((length, n_out), BF16)],
        compiler_params=_cparams("parallel"),
        name="hyena_filter_mlp",
    )(feats, *args)


def _dft_matrices(length):
    n2 = 2 * length
    step = DFT_ROW_STEP
    n = jnp.arange(length, dtype=I32)

    def trig(kv):
        ang = ((kv[:, None] * n[None, :]) % n2).astype(F32) * (2.0 * math.pi / n2)
        return jnp.cos(ang), jnp.sin(ang)

    c1, s1 = trig(jnp.arange(length // step, dtype=I32) * step)
    c2, s2 = trig(jnp.arange(step, dtype=I32))
    coarse = pl.BlockSpec((1, 1, length), lambda j: (j, 0, 0))
    fine = pl.BlockSpec((step, length), lambda j: (0, 0))
    out = pl.BlockSpec((step, length), lambda j: (j, 0))
    return pl.pallas_call(
        _dft_matrix_kernel,
        grid=(length // step,),
        in_specs=[coarse, coarse, fine, fine],
        out_specs=[out, out, out],
        out_shape=[jax.ShapeDtypeStruct((length, length), BF16)] * 3,
        compiler_params=_cparams("parallel"),
        name="dft_matrices",
    )(c1[:, None, :], s1[:, None, :], c2, s2)


DFT_ROW_STEP = 64


def _dft_matrix_kernel(c1_ref, s1_ref, c2_ref, s2_ref, mc_o, ms_o, mst_o):
    c1, s1 = c1_ref[0], s1_ref[0]
    c2, s2 = c2_ref[...], s2_ref[...]
    cos = c1 * c2 - s1 * s2
    neg_sin = -(s1 * c2 + c1 * s2)
    row = lax.broadcasted_iota(I32, cos.shape, 0)
    col = lax.broadcasted_iota(I32, cos.shape, 1)
    first_row = (row + pl.program_id(0) * DFT_ROW_STEP) == 0
    mc_o[...] = cos.astype(BF16)
    ms_o[...] = jnp.where(first_row, jnp.where(col % 2 == 0, 1.0, -1.0), neg_sin).astype(BF16)
    mst_o[...] = jnp.where(col == 0, jnp.where(row % 2 == 0, 1.0, -1.0), neg_sin).astype(BF16)


def _dft_fwd_kernel(mc_ref, ms_ref, x_ref, *rest, with_taps):
    ure = _dot(mc_ref[...], x_ref[...])
    uim = _dot(ms_ref[...], x_ref[...])
    if not with_taps:
        ure_o, uim_o = rest
        ure_o[...] = ure
        uim_o[...] = uim
        return
    a_ref, b_ref, c_ref, d_ref, zre_o, zim_o = rest
    a, b, c, d = a_ref[...], b_ref[...], c_ref[...], d_ref[...]
    for bb in range(x_ref.shape[1] // HY_WIDTH):
        sl = slice(bb * HY_WIDTH, (bb + 1) * HY_WIDTH)
        zre_o[:, sl] = (ure[:, sl] * a - uim[:, sl] * b).astype(BF16)
        zim_o[:, sl] = (ure[:, sl] * c + uim[:, sl] * d).astype(BF16)


def _col_block(nc, cap):
    return min(nc, cap)


def _dft_fwd(mats, x, plane, taps=None):
    mc, ms, _ = mats
    _, length, nc = x.shape
    tk = min(length, 256)
    cb = _col_block(nc, 1024)
    grid = (nc // cb, length // tk)
    m_spec = pl.BlockSpec((tk, length), lambda c, j: (j, 0))
    x_spec = pl.BlockSpec((None, length, cb), lambda c, j: (plane, 0, c))
    o_spec = pl.BlockSpec((tk, cb), lambda c, j: (j, c))
    if taps is None:
        return pl.pallas_call(
            functools.partial(_dft_fwd_kernel, with_taps=False),
            grid=grid, in_specs=[m_spec, m_spec, x_spec], out_specs=[o_spec, o_spec],
            out_shape=[jax.ShapeDtypeStruct((length, nc), F32)] * 2,
            compiler_params=_cparams("parallel", "arbitrary"),
            name="hyena_dft_filters",
        )(mc, ms, x)
    t_spec = pl.BlockSpec((tk, HY_WIDTH), lambda c, j: (j, 0))
    return pl.pallas_call(
        functools.partial(_dft_fwd_kernel, with_taps=True),
        grid=grid, in_specs=[m_spec, m_spec, x_spec] + [t_spec] * 4, out_specs=[o_spec, o_spec],
        out_shape=[jax.ShapeDtypeStruct((length, nc), BF16)] * 2,
        compiler_params=_cparams("parallel", "arbitrary"),
        name="hyena_dft_forward",
    )(mc, ms, x, *taps)


def _dft_inv_kernel(mc_ref, mst_ref, zre_ref, zim_ref, gate_ref, prev_ref, bias_ref, *outs, last):
    conv = _dot(mc_ref[...], zre_ref[...]) + _dot(mst_ref[...], zim_ref[...])
    y = gate_ref[...] * (conv + prev_ref[...] * bias_ref[...])
    if last:
        (tok_o,) = outs
        for bb in range(y.shape[1] // HY_WIDTH):
            tok_o[bb] = y[:, bb * HY_WIDTH:(bb + 1) * HY_WIDTH].astype(BF16)
    else:
        y_o, ybf_o = outs
        y_o[...] = y
        ybf_o[...] = y.astype(BF16)


def _dft_inv(mats, zre, zim, gate, gate_plane, prev, prev_plane, bias_row, last):
    mc, _, mst = mats
    length, nc = zre.shape
    tm = min(length, 256)
    cb = _col_block(nc, 512)
    grid = (nc // cb, length // tm)
    m_spec = pl.BlockSpec((tm, length), lambda c, i: (i, 0))
    z_spec = pl.BlockSpec((length, cb), lambda c, i: (0, c))
    e_spec = lambda plane: pl.BlockSpec((None, tm, cb), lambda c, i: (plane, i, c))
    b_spec = pl.BlockSpec((1, cb), lambda c, i: (0, c))
    if last:
        nb = nc // HY_WIDTH
        out_specs = [pl.BlockSpec((cb // HY_WIDTH, tm, HY_WIDTH), lambda c, i: (c, i, 0))]
        out_shape = [jax.ShapeDtypeStruct((nb, length, HY_WIDTH), BF16)]
    else:
        out_specs = [e_spec(0), e_spec(0)]
        out_shape = [jax.ShapeDtypeStruct((1, length, nc), F32), jax.ShapeDtypeStruct((1, length, nc), BF16)]
    return pl.pallas_call(
        functools.partial(_dft_inv_kernel, last=last),
        grid=grid, in_specs=[m_spec, m_spec, z_spec, z_spec, e_spec(gate_plane), e_spec(prev_plane), b_spec],
        out_specs=out_specs, out_shape=out_shape,
        compiler_params=_cparams("parallel", "arbitrary"),
        name="hyena_dft_inverse",
    )(mc, mst, zre, zim, gate, prev, bias_row)


def _hy_tap_tables(ure, uim, filt, length):
    n2 = 2 * length
    w = HY_WIDTH
    scale = jnp.full((length, 1), 2.0 / n2, F32).at[0, 0].set(1.0 / n2)
    tables = []
    for o in range(HY_ORDER):
        f_sl = slice((2 * o) * w, (2 * o + 1) * w)
        b_sl = slice((2 * o + 1) * w, (2 * o + 2) * w)
        hb0 = filt[0:1, b_sl]
        tre = ure[:, f_sl] + ure[:, b_sl] - hb0
        tim = uim[:, f_sl] - uim[:, b_sl]
        t_nyq = uim[0:1, f_sl] + uim[0:1, b_sl] - hb0
        first = (jnp.arange(length) == 0)[:, None]
        a = tre * scale
        bm = jnp.where(first, 0.0, tim * scale)
        c = bm
        d = jnp.where(first, t_nyq * scale, tre * scale)
        tables.append((a, bm, c, d))
    return tables


def _hyena_seq(mats, vx, vx_bf, lp, n_batch):
    length = vx.shape[1]
    filt, filt_bf = _hy_filters(length, lp)
    ure, uim = _dft_fwd(mats, filt_bf[None], 0)
    tables = _hy_tap_tables(ure, uim, filt, length)
    bias = lp['hy_bias'].astype(F32)
    y, y_bf = vx, vx_bf
    for o in range(HY_ORDER):
        zre, zim = _dft_fwd(mats, y_bf, 0, tables[o])
        bias_row = jnp.tile(bias[o][None, :], (1, n_batch))
        last = o == HY_ORDER - 1
        res = _dft_inv(mats, zre, zim, vx, o + 1, y, 0, bias_row, last)
        if last:
            return res[0]
        y, y_bf = res


def _lru_kernel(u_ref, g_ref, cw_ref, cb_ref, wa_ref, ba_ref, wx_ref, bx_ref, lam_ref, o_ref,
                pad_ref, y_ref, *, seq, ctx_len):
    tc = LRU_CHUNK
    halo = LRU_HALO
    width = LRU_WIDTH
    lat_off = halo
    ctx_off = 2 * halo + seq
    zero = jnp.zeros((halo, width), F32)
    pad_ref[0:halo, :] = zero
    pad_ref[lat_off:lat_off + seq, :] = u_ref[0, 0:seq, :]
    pad_ref[lat_off + seq:ctx_off, :] = zero
    pad_ref[ctx_off:ctx_off + ctx_len, :] = u_ref[0, seq:seq + ctx_len, :]
    pad_ref[ctx_off + ctx_len:ctx_off + ctx_len + halo, :] = zero
    row = lax.broadcasted_iota(I32, (tc, width), 0)
    n_win = tc + 2 * halo

    def chunk(pad_off, y_off, s, carry, d):
        wstart = pl.multiple_of(pad_off + s - halo, 8)
        win = pad_ref[pl.ds(wstart, n_win), :]
        cw = cw_ref[d]
        xc = cb_ref[d]
        for k in range(LRU_CONV):
            shift = (LRU_CONV - 1 - k) if d == 0 else -k
            rolled = win if shift == 0 else pltpu.roll(win, shift % n_win, axis=0)
            xc = xc + cw[k:k + 1] * rolled[halo:halo + tc]
        xb = xc.astype(BF16)
        r = _sigmoid(_dot(xb, wa_ref[d]) + ba_ref[d])
        gi = _sigmoid(_dot(xb, wx_ref[d]) + bx_ref[d])
        lam = lam_ref[d]
        softplus = jnp.maximum(-lam, 0.0) + jnp.log1p(jnp.exp(-jnp.abs(lam)))
        log_a = -LRU_C * r * softplus
        a = jnp.exp(log_a)
        bt = jnp.sqrt(-jnp.tanh(log_a) * (a * a + 1.0)) * (gi * xc)
        sft = 1
        while sft < tc:
            if d == 0:
                keep = row >= sft
                a_s = jnp.where(keep, pltpu.roll(a, sft, axis=0), 1.0)
                b_s = jnp.where(keep, pltpu.roll(bt, sft, axis=0), 0.0)
            else:
                keep = row < tc - sft
                a_s = jnp.where(keep, pltpu.roll(a, tc - sft, axis=0), 1.0)
                b_s = jnp.where(keep, pltpu.roll(bt, tc - sft, axis=0), 0.0)
            bt = a * b_s + bt
            a = a * a_s
            sft *= 2
        h = a * carry + bt
        yo = pl.multiple_of(y_off + s, 8)
        if d == 0:
            y_ref[pl.ds(yo, tc), :] = h
            return h[tc - 1:tc]
        y_ref[pl.ds(yo, tc), :] = y_ref[pl.ds(yo, tc), :] + h
        return h[0:1]

    n_lat = seq // tc
    n_ctx = ctx_len // tc
    for d in range(2):
        carry = jnp.zeros((1, width), F32)
        order = range(n_ctx) if d == 0 else range(n_ctx - 1, -1, -1)
        for c in order:
            carry = chunk(ctx_off, seq, c * tc, carry, d)

        def body(j, cr, d=d):
            jj = j if d == 0 else n_lat - 1 - j
            return chunk(lat_off, 0, jj * tc, cr, d)

        lax.fori_loop(0, n_lat, body, carry)
    o_ref[0] = (y_ref[...] * _gelu_tanh(g_ref[0])).astype(BF16)


def _block_diag(w):
    nd, nb, c, _ = w.shape
    out = jnp.zeros((nd, nb * c, nb * c), w.dtype)
    for n in range(nb):
        out = out.at[:, n * c:(n + 1) * c, n * c:(n + 1) * c].set(w[:, n])
    return out


def _lru_mixer(lu, lg, lp, seq):
    b, nt, w = lu.shape
    ctx_len = nt - seq
    row3 = lambda a: a.reshape(2, 1, w)
    args = (_pad_to(lp['lru_conv_w'], 8, 1), row3(lp['lru_conv_b']), _block_diag(lp['lru_wa']).astype(BF16),
            row3(lp['lru_ba']), _block_diag(lp['lru_wx']).astype(BF16), row3(lp['lru_bx']), row3(lp['lru_lambda']))
    full = lambda a: pl.BlockSpec(a.shape, lambda bb: (0,) * a.ndim)
    tok = pl.BlockSpec((1, nt, w), lambda bb: (bb, 0, 0))
    return pl.pallas_call(
        functools.partial(_lru_kernel, seq=seq, ctx_len=ctx_len),
        grid=(b,),
        in_specs=[tok, tok] + [full(a) for a in args],
        out_specs=tok,
        out_shape=jax.ShapeDtypeStruct((b, nt, w), BF16),
        scratch_shapes=[pltpu.VMEM((nt + 3 * LRU_HALO, w), F32), pltpu.VMEM((nt, w), F32)],
        compiler_params=_cparams("parallel"),
        name="rglru_scan",
    )(lu, lg, *args)


def _merge_kernel(x_ref, mod_ref, g1_ref, a_ref, b_ref, c_ref, d_ref, wg_ref, wa_ref, wb_ref, wc_ref, wd_ref,
                  wo_ref, o_ref):
    group, tm, dm = x_ref.shape
    m = mod_ref[:, 0]
    x = x_ref[...]
    h = _normmod(x, g1_ref[...], m[:, 1:2], m[:, 0:1]).reshape(group * tm, dm).astype(BF16)
    acc = None
    for k, (br, w) in enumerate(((a_ref, wa_ref), (b_ref, wb_ref), (c_ref, wc_ref), (d_ref, wd_ref))):
        gate = _sigmoid(_dot(h, wg_ref[:, k * dm:(k + 1) * dm]))
        term = gate * _dot(br[...].reshape(group * tm, br.shape[-1]), w[...])
        acc = term if acc is None else acc + term
    y = _dot(acc.astype(BF16), wo_ref[...])
    o_ref[...] = x + m[:, 2:3] * y.reshape(group, tm, dm)


def _merge(xa, modtab, g1, branches, w_gate, lp, n_lat_tiles, tiles_used):
    b, _, d = xa.shape
    wbr = lp['w_branch']
    head_rows = lambda w, dv: jnp.concatenate(
        [_pad_to(w[hd * dv:(hd + 1) * dv], LANE, 0) for hd in range(4)], axis=0)
    weights = (head_rows(wbr[0], MLA_V).astype(BF16), head_rows(wbr[1], NA_HEAD_DIM).astype(BF16),
               wbr[2].astype(BF16), wbr[3].astype(BF16), lp['w_out'].astype(BF16))
    group = _batch_group(b)
    full = lambda a: pl.BlockSpec(a.shape, lambda bb, i: (0,) * a.ndim)
    tok = lambda n: pl.BlockSpec((group, TM, n), lambda bb, i: (bb, i, 0))
    return pl.pallas_call(
        _merge_kernel,
        grid=(b // group, tiles_used),
        in_specs=[tok(d), pl.BlockSpec((group, 1, 8, d), _kind_map(n_lat_tiles)), full(g1)]
                 + [tok(br.shape[-1]) for br in branches] + [full(w_gate)] + [full(w) for w in weights],
        out_specs=tok(d),
        out_shape=jax.ShapeDtypeStruct((b, tiles_used * TM, d), F32),
        compiler_params=_cparams("parallel", "parallel"),
        name="merge_branches",
    )(xa, modtab, g1, *branches, w_gate, *weights)


U32 = jnp.uint32
EXPERT_ROWS = 512


def _pack_pair(x):
    n = x.shape[-1] // 2
    hi = lax.bitcast_convert_type(x[:, :n].astype(BF16).astype(F32), U32)
    lo = lax.bitcast_convert_type(x[:, n:].astype(BF16).astype(F32), U32)
    return hi | (lo >> 16)


def _unpack_pair(p):
    hi = lax.bitcast_convert_type(p & jnp.uint32(0xFFFF0000), F32)
    lo = lax.bitcast_convert_type(p << 16, F32)
    return hi, lo


def _router_kernel(x_ref, mod_ref, g2_ref, rw_ref, rb_ref, tri_ref, h2_o, idx_o, wts_o, rank_o, cnt_o, carry):
    i = pl.program_id(0)

    @pl.when(i == 0)
    def _():
        carry[...] = jnp.zeros_like(carry)

    m = mod_ref[0, 0]
    h2 = _normmod(x_ref[0], g2_ref[...], m[4:5], m[3:4])
    half = h2.shape[-1] // 2
    h2_o[0] = _pack_pair(h2[:, :half])
    h2_o[1] = _pack_pair(h2[:, half:])
    logits = lax.dot_general(rw_ref[...], h2, (((1,), (1,)), ((), ())), preferred_element_type=F32,
                             precision=lax.Precision.HIGHEST)
    scores = _sigmoid(logits)
    biased = scores + rb_ref[...]
    expert = lax.broadcasted_iota(I32, scores.shape, 0)
    picks = []
    onehot_all = jnp.zeros(scores.shape, F32)
    for _ in range(TOP_K):
        best = jnp.max(biased, axis=0, keepdims=True)
        arg = jnp.min(jnp.where(biased == best, expert, N_EXPERTS), axis=0, keepdims=True)
        hit = expert == arg
        sel = jnp.sum(jnp.where(hit, scores, 0.0), axis=0, keepdims=True)
        biased = jnp.where(hit, -jnp.inf, biased)
        onehot_all = onehot_all + jnp.where(hit, 1.0, 0.0)
        picks.append((arg, hit, sel))
    total = picks[0][2]
    for _, _, sel in picks[1:]:
        total = total + sel
    earlier = _dot(onehot_all.astype(BF16), tri_ref[...]) + carry[...]
    pad_rows = TOPK_PAD - TOP_K
    ranks = [jnp.sum(jnp.where(hit, earlier, 0.0), axis=0, keepdims=True).astype(I32) for _, hit, _ in picks]
    scale = ROUTED_SCALE / total
    idx_o[...] = jnp.concatenate([arg for arg, _, _ in picks] + [jnp.zeros((pad_rows, TM), I32)], axis=0)
    wts_o[...] = jnp.concatenate([sel * scale for _, _, sel in picks] + [jnp.zeros((pad_rows, TM), F32)], axis=0)
    rank_o[...] = jnp.concatenate(ranks + [jnp.zeros((pad_rows, TM), I32)], axis=0)
    carry[...] = carry[...] + jnp.sum(onehot_all, axis=1, keepdims=True)
    cnt_o[...] = carry[...]


def _tile_maps(tiles_used, n_lat_tiles):
    tok = lambda i: (i // tiles_used, i % tiles_used, 0)
    mod = lambda i: (i // tiles_used, jnp.where(i % tiles_used >= n_lat_tiles, 1, 0), 0, 0)
    return tok, mod


def _route(x1, modtab, g2, lp, n_lat_tiles, tiles_used):
    b, _, d = x1.shape
    n_tiles = b * tiles_used
    t = n_tiles * TM
    rw = lp['router_w'].T
    rb = lp['router_bias'].reshape(-1, 1)
    tri = (np.arange(TM)[:, None] < np.arange(TM)[None, :]).astype(np.float32)
    tri = jnp.asarray(tri, BF16)
    per_tok = lambda: pl.BlockSpec((TOPK_PAD, TM), lambda i: (0, i))
    full = lambda a: pl.BlockSpec(a.shape, lambda i: (0,) * a.ndim)
    tok, mod = _tile_maps(tiles_used, n_lat_tiles)
    return pl.pallas_call(
        _router_kernel,
        grid=(n_tiles,),
        in_specs=[pl.BlockSpec((1, TM, d), tok), pl.BlockSpec((1, 1, 8, d), mod),
                  full(g2), full(rw), full(rb), full(tri)],
        out_specs=[pl.BlockSpec((2, TM, d // 4), lambda i: (0, i, 0)), per_tok(), per_tok(), per_tok(),
                   pl.BlockSpec((N_EXPERTS, 1), lambda i: (0, 0))],
        out_shape=[jax.ShapeDtypeStruct((2, t, d // 4), U32), jax.ShapeDtypeStruct((TOPK_PAD, t), I32),
                   jax.ShapeDtypeStruct((TOPK_PAD, t), F32), jax.ShapeDtypeStruct((TOPK_PAD, t), I32),
                   jax.ShapeDtypeStruct((N_EXPERTS, 1), F32)],
        scratch_shapes=[pltpu.VMEM((N_EXPERTS, 1), F32)],
        compiler_params=_cparams("arbitrary"),
        name="moe_router",
    )(x1, modtab, g2, rw, rb, tri)


SC_WINDOW = 128


def _sc_mesh():
    return plsc.VectorSubcoreMesh(core_axis_name="c", subcore_axis_name="s")


def _sc_scatter_rows(src, idx, n_out):
    n, width = src.shape
    k_rep = idx.shape[0]
    half = n // SC_WINDOW // 2

    @functools.partial(pl.kernel, out_type=jax.ShapeDtypeStruct((n_out, width), src.dtype), mesh=_sc_mesh(),
                       scratch_types=[], name="moe_dispatch_sc")
    def scatter(src_hbm, idx_hbm, out_hbm):
        def body(x_vmem, *i_vmems):
            for i_vmem in i_vmems:
                pltpu.sync_copy(x_vmem, out_hbm.at[i_vmem.at[0]])

        pltpu.emit_pipeline(
            body,
            grid=(2, half),
            in_specs=[pl.BlockSpec((SC_WINDOW, width), lambda a, i: (a * half + i, 0))]
                     + [pl.BlockSpec((1, SC_WINDOW), lambda a, i, k=k: (k, a * half + i)) for k in range(k_rep)],
            out_specs=[],
            core_axis_name=("c", "s"),
            dimension_semantics=(pltpu.PARALLEL, pltpu.PARALLEL),
        )(src_hbm, *([idx_hbm] * k_rep))

    return scatter(src, idx)


def _sc_gather_rows(src, idx):
    k_rep, n = idx.shape
    width = src.shape[1]
    n_win = n // SC_WINDOW

    @functools.partial(pl.kernel, out_type=jax.ShapeDtypeStruct((k_rep * n, width), src.dtype), mesh=_sc_mesh(),
                       scratch_types=[], name="moe_gather_sc")
    def gather(src_hbm, idx_hbm, out_hbm):
        def body(i_vmem, o_vmem):
            pltpu.sync_copy(src_hbm.at[i_vmem.at[0]], o_vmem)

        pltpu.emit_pipeline(
            body,
            grid=(k_rep, n_win),
            in_specs=[pl.BlockSpec((1, SC_WINDOW), lambda k, i: (k, i))],
            out_specs=[pl.BlockSpec((SC_WINDOW, width), lambda k, i: (k * n_win + i, 0))],
            core_axis_name=("c", "s"),
            dimension_semantics=(pltpu.PARALLEL, pltpu.PARALLEL),
        )(idx_hbm, out_hbm)

    return gather(src, idx)


def _unpack_planes(p0, p1):
    return _unpack_pair(p0) + _unpack_pair(p1)


def _dot_quarters(parts, w_ref):
    q = parts[0].shape[-1]
    acc = None
    for j, part in enumerate(parts):
        term = _dot(part.astype(BF16), w_ref[j * q:(j + 1) * q, :])
        acc = term if acc is None else acc + term
    return acc


def _expert_kernel(be_ref, nv_ref, xs_ref, wg_ref, wu_ref, wd_ref, ys_o, wg_s, wu_s, wd_s):
    i = pl.program_id(0)
    prev = be_ref[jnp.maximum(i - 1, 0)]

    @pl.when((i == 0) | (be_ref[i] != prev))
    def _():
        wg_s[...] = wg_ref[0].astype(BF16)
        wu_s[...] = wu_ref[0].astype(BF16)
        wd_s[...] = wd_ref[0].astype(BF16)

    @pl.when(nv_ref[i] > 0)
    def _():
        keep = lax.broadcasted_iota(I32, xs_ref.shape[1:], 0) < nv_ref[i]
        parts = _unpack_planes(jnp.where(keep, xs_ref[0], jnp.uint32(0)), jnp.where(keep, xs_ref[1], jnp.uint32(0)))
        hid = _silu(_dot_quarters(parts, wg_s)) * _dot_quarters(parts, wu_s)
        y = _dot(hid.astype(BF16), wd_s[...])
        half = y.shape[-1] // 2
        ys_o[0] = _pack_pair(y[:, :half])
        ys_o[1] = _pack_pair(y[:, half:])

    @pl.when(nv_ref[i] <= 0)
    def _():
        ys_o[...] = jnp.zeros_like(ys_o)


def _experts(xs, block_e, n_valid, weights, layer):
    _, n_rows, dq = xs.shape
    d = 4 * dq
    n_blocks = n_rows // EXPERT_ROWS
    hid = EXPERT_HIDDEN
    grid_spec = pltpu.PrefetchScalarGridSpec(
        num_scalar_prefetch=2,
        grid=(n_blocks,),
        in_specs=[pl.BlockSpec((2, EXPERT_ROWS, dq), lambda i, be, nv: (0, i, 0)),
                  pl.BlockSpec((None, 1, d, hid), lambda i, be, nv: (layer, be[i], 0, 0)),
                  pl.BlockSpec((None, 1, d, hid), lambda i, be, nv: (layer, be[i], 0, 0)),
                  pl.BlockSpec((None, 1, hid, d), lambda i, be, nv: (layer, be[i], 0, 0))],
        out_specs=pl.BlockSpec((2, EXPERT_ROWS, dq), lambda i, be, nv: (0, i, 0)),
        scratch_shapes=[pltpu.VMEM((d, hid), BF16), pltpu.VMEM((d, hid), BF16), pltpu.VMEM((hid, d), BF16)],
    )
    return pl.pallas_call(
        _expert_kernel,
        grid_spec=grid_spec,
        out_shape=jax.ShapeDtypeStruct((2, n_rows, dq), U32),
        compiler_params=_cparams("arbitrary"),
        name="moe_experts",
    )(block_e, n_valid, xs, *weights)


def _combine_kernel(g_ref, wts_ref, h2_ref, x_ref, mod_ref, sg_ref, su_ref, sd_ref, gf_ref, o_ref, *, final):
    parts = _unpack_planes(h2_ref[0], h2_ref[1])
    hid = _silu(_dot_quarters(parts, sg_ref)) * _dot_quarters(parts, su_ref)
    shared = _dot(hid.astype(BF16), sd_ref[...])
    wts = wts_ref[...]
    routed = None
    for k in range(TOP_K):
        w = wts[:, k:k + 1]
        terms = [w * part for part in _unpack_planes(g_ref[k, 0], g_ref[k, 1])]
        routed = terms if routed is None else [r + t for r, t in zip(routed, terms)]
    f = shared + jnp.concatenate(routed, axis=1)
    m = mod_ref[0, 0]
    x2 = x_ref[0] + m[5:6] * f
    if final:
        x2 = x2 * lax.rsqrt(jnp.mean(x2 * x2, axis=-1, keepdims=True) + NORM_EPS) * gf_ref[...]
    o_ref[0] = x2


def _combine(g, wts, h2p, x1, modtab, lp, g_final, n_lat_tiles, tiles_used, final):
    b, _, d = x1.shape
    dq = d // 4
    weights = (lp['sh_w_gate'].astype(BF16), lp['sh_w_up'].astype(BF16), lp['sh_w_down'].astype(BF16),
               g_final.reshape(1, -1))
    full = lambda a: pl.BlockSpec(a.shape, lambda i: (0,) * a.ndim)
    tok, mod = _tile_maps(tiles_used, n_lat_tiles)
    return pl.pallas_call(
        functools.partial(_combine_kernel, final=final),
        grid=(b * tiles_used,),
        in_specs=[pl.BlockSpec((TOP_K, 2, TM, dq), lambda i: (0, 0, i, 0)),
                  pl.BlockSpec((TM, TOPK_PAD), lambda i: (i, 0)),
                  pl.BlockSpec((2, TM, dq), lambda i: (0, i, 0)),
                  pl.BlockSpec((1, TM, d), tok), pl.BlockSpec((1, 1, 8, d), mod)]
                 + [full(w) for w in weights],
        out_specs=pl.BlockSpec((1, TM, d), tok),
        out_shape=jax.ShapeDtypeStruct((b, tiles_used * TM, d), F32),
        compiler_params=_cparams("parallel"),
        name="moe_combine",
    )(g, wts, h2p, x1, modtab, *weights)


def _moe(x1, modtab, g2, lp, g_final, n_lat_tiles, final):
    b, nt, d = x1.shape
    tiles_used = n_lat_tiles if final else nt // TM
    t = b * tiles_used * TM
    h2p, idx, wts, rank, cnt = _route(x1, modtab, g2, lp, n_lat_tiles, tiles_used)
    counts = cnt[:, 0].astype(I32)
    padded = (counts + EXPERT_ROWS - 1) // EXPERT_ROWS * EXPERT_ROWS
    p_ends = jnp.cumsum(padded)
    p_starts = p_ends - padded
    n_blocks = (t * TOP_K + N_EXPERTS * (EXPERT_ROWS - 1)) // EXPERT_ROWS
    n_rows = n_blocks * EXPERT_ROWS
    chosen = idx[:TOP_K, None, :] == jnp.arange(N_EXPERTS, dtype=I32)[None, :, None]
    dest = jnp.sum(jnp.where(chosen, p_starts[None, :, None], 0), axis=1) + rank[:TOP_K]
    plane_idx = (dest[:, None, :] + (jnp.arange(2, dtype=I32) * n_rows)[None, :, None]).reshape(TOP_K, 2 * t)
    blk_start = jnp.arange(n_blocks, dtype=I32) * EXPERT_ROWS
    block_e = jnp.minimum(jnp.sum((p_ends[None, :] <= blk_start[:, None]).astype(I32), axis=1), N_EXPERTS - 1)
    n_valid = jnp.clip((p_starts + counts)[block_e] - blk_start, 0, EXPERT_ROWS).astype(I32)
    dq = d // 4
    xs = _sc_scatter_rows(h2p.reshape(2 * t, dq), plane_idx, 2 * n_rows).reshape(2, n_rows, dq)
    ys = _experts(xs, block_e, n_valid, lp['expert_stacks'], lp['layer'])
    g = _sc_gather_rows(ys.reshape(2 * n_rows, dq), plane_idx).reshape(TOP_K, 2, t, dq)
    return _combine(g, wts.T, h2p, x1, modtab, lp, g_final, n_lat_tiles, tiles_used, final)


def _layer(xa, c, c_ctx, lp, consts, g_final, seq, final):
    b, nt, d = xa.shape
    n_lat_tiles = seq // TM
    rope, mats_lat, mats_ctx = consts
    modtab = _mod_table(c, c_ctx, *lp['mod_stacks'], lp['layer'])
    g1 = lp['g_norm1'].reshape(1, -1)
    g2 = lp['g_norm2'].reshape(1, -1)
    pw = _proj_weights(lp)
    q, k, v, nq, nk, nv, hy, lu, lg = _project(xa, modtab, g1, pw, rope, n_lat_tiles)
    tiles_used = n_lat_tiles if final else nt // TM
    br_a = _mla_attention(q, k, v, seq, tiles_used)
    br_b = _na_attention(nq, nk, nv, _na_bias_tables(lp['na_rpb'], seq // GRID_W), seq, tiles_used)
    pre = _hy_pre(hy, lp['hy_short_w'], lp['hy_short_b'], seq)
    lat_f, lat_bf, ctx_f, ctx_bf = pre
    br_c = _hyena_seq(mats_lat, lat_f, lat_bf, lp, b)
    if not final:
        br_c = jnp.concatenate([br_c, _hyena_seq(mats_ctx, ctx_f, ctx_bf, lp, b)], axis=1)
    br_d = _lru_mixer(lu, lg, lp, seq)
    x1 = _merge(xa, modtab, g1, (br_a, br_b, br_c, br_d), pw['w_gate'], lp, n_lat_tiles, tiles_used)
    return _moe(x1, modtab, g2, lp, g_final, n_lat_tiles, final)


_LAYER_KEYS = ('w_mod', 'b_mod', 'g_norm1', 'g_norm2', 'w_in', 'mla_g_q', 'mla_w_uq', 'mla_g_kv', 'mla_w_ukv',
               'na_rpb', 'hy_short_w', 'hy_short_b', 'hy_w1', 'hy_b1', 'hy_w2', 'hy_b2', 'hy_w3', 'hy_decay',
               'hy_bias', 'lru_conv_w', 'lru_conv_b', 'lru_wa', 'lru_ba', 'lru_wx', 'lru_bx', 'lru_lambda',
               'w_branch', 'w_out', 'router_w', 'router_bias', 'exp_w_gate', 'exp_w_up', 'exp_w_down',
               'sh_w_gate', 'sh_w_up', 'sh_w_down')


def kernel(x, c, ctx, c_ctx, w_mod, b_mod, g_norm1, g_norm2, w_in, mla_g_q, mla_w_uq, mla_g_kv, mla_w_ukv, na_rpb, hy_short_w, hy_short_b, hy_w1, hy_b1, hy_w2, hy_b2, hy_w3, hy_decay, hy_bias, lru_conv_w, lru_conv_b, lru_wa, lru_ba, lru_wx, lru_bx, lru_lambda, w_branch, w_out, router_w, router_bias, exp_w_gate, exp_w_up, exp_w_down, sh_w_gate, sh_w_up, sh_w_down, g_final):
    stacked = dict(zip(_LAYER_KEYS, (w_mod, b_mod, g_norm1, g_norm2, w_in, mla_g_q, mla_w_uq, mla_g_kv, mla_w_ukv,
                                     na_rpb, hy_short_w, hy_short_b, hy_w1, hy_b1, hy_w2, hy_b2, hy_w3, hy_decay,
                                     hy_bias, lru_conv_w, lru_conv_b, lru_wa, lru_ba, lru_wx, lru_bx, lru_lambda,
                                     w_branch, w_out, router_w, router_bias, exp_w_gate, exp_w_up, exp_w_down,
                                     sh_w_gate, sh_w_up, sh_w_down)))
    b, seq, d = x.shape
    ctx_len = ctx.shape[1]
    depth = w_mod.shape[0]
    assert seq % TM == 0 and ctx_len % TM == 0 and seq // GRID_W >= NA_KEY_ROWS + 1
    xa = jnp.concatenate([x, ctx], axis=1)
    consts = (_rope_tables(seq, seq + ctx_len), _dft_matrices(seq), _dft_matrices(ctx_len))
    for i in range(depth):
        big = ('w_mod', 'b_mod', 'exp_w_gate', 'exp_w_up', 'exp_w_down')
        lp = {name: w[i] for name, w in stacked.items() if name not in big}
        lp['layer'] = i
        lp['mod_stacks'] = (w_mod, b_mod)
        lp['expert_stacks'] = (exp_w_gate, exp_w_up, exp_w_down)
        xa = _layer(xa, c, c_ctx, lp, consts, g_final, seq, i == depth - 1)
    return xa
```

```python
import functools
import math

import numpy as np
import jax
import jax.numpy as jnp
from jax import lax
from jax.experimental import pallas as pl
from jax.experimental.pallas import tpu as pltpu
from jax.experimental.pallas import tpu_sc as plsc

F32 = jnp.float32
BF16 = jnp.bfloat16
I32 = jnp.int32

TM = 256
LANE = 128
GRID_W = 64
N_MOD = 6
NORM_EPS = 1e-6

MLA_HEADS, MLA_NOPE, MLA_ROPE, MLA_V = 4, 64, 32, 64
MLA_Q_RANK, MLA_KV_RANK = 192, 128
MLA_Q_PAD = 256
ROPE_THETA = 10000.0

NA_HEADS, NA_HEAD_DIM, NA_WIN_R, NA_WIN_C = 4, 64, 8, 16
NA_TILE_ROWS = TM // GRID_W
NA_KEY_ROWS = NA_TILE_ROWS + NA_WIN_R - 1
NA_KEYS = NA_KEY_ROWS * GRID_W

HY_WIDTH, HY_ORDER, HY_SHORT, HY_BANDS, HY_FFN = 256, 2, 3, 16, 64
HY_EMB = 2 * HY_BANDS + 1

LRU_WIDTH, LRU_BLOCKS, LRU_CONV, LRU_C = 256, 4, 4, 8.0
LRU_CHUNK = 256
LRU_HALO = 8

N_EXPERTS, TOP_K, EXPERT_HIDDEN, ROUTED_SCALE, MOE_BLOCK = 64, 6, 256, 2.5, 256
TOPK_PAD = 8

VMEM_LIMIT = 52 * 1024 * 1024


def _cparams(*sem):
    return pltpu.CompilerParams(dimension_semantics=sem, vmem_limit_bytes=VMEM_LIMIT)


def _dot(a, b):
    return jnp.dot(a, b, preferred_element_type=F32)


def _dot_nt(a, b):
    return lax.dot_general(a, b, (((1,), (1,)), ((), ())), preferred_element_type=F32)


def _sigmoid(x):
    return jax.nn.sigmoid(x)


def _silu(x):
    return x * _sigmoid(x)


def _gelu_tanh(x):
    return 0.5 * x * (1.0 + jnp.tanh(math.sqrt(2.0 / math.pi) * (x + 0.044715 * (x * x * x))))


def _normmod(x, g, scale, shift):
    y = x * lax.rsqrt(jnp.mean(x * x, axis=-1, keepdims=True) + NORM_EPS) * g
    return y * (1.0 + scale) + shift


def _mod_kernel(c_ref, w_ref, b_ref, o_ref):
    s = _silu(c_ref[...])
    o_ref[...] = _dot(s.astype(BF16), w_ref[...].astype(BF16)) + b_ref[...]


def _mod_table(c, c_ctx, w_mod, b_mod, layer):
    b, d = c.shape
    rows = 16
    cc = jnp.zeros((rows, d), F32).at[:b].set(c).at[b].set(c_ctx)
    tn = 1024
    mod = pl.pallas_call(
        _mod_kernel,
        grid=(N_MOD * d // tn,),
        in_specs=[pl.BlockSpec((rows, d), lambda j: (0, 0)),
                  pl.BlockSpec((None, d, tn), lambda j: (layer, 0, j)),
                  pl.BlockSpec((None, 1, tn), lambda j: (layer, 0, j))],
        out_specs=pl.BlockSpec((rows, tn), lambda j: (0, j)),
        out_shape=jax.ShapeDtypeStruct((rows, N_MOD * d), F32),
        compiler_params=_cparams("arbitrary"),
        name="mod_vectors",
    )(cc, w_mod, b_mod[:, None, :])
    lat = mod[:b].reshape(b, N_MOD, d)
    ctx = jnp.broadcast_to(mod[b].reshape(1, N_MOD, d), (b, N_MOD, d))
    tab = jnp.stack([lat, ctx], axis=1)
    return jnp.pad(tab, ((0, 0), (0, 0), (0, 8 - N_MOD), (0, 0)))


_C_QLAT, _C_KVLAT, _C_KR, _C_KRR, _C_NA, _C_HY, _C_LU, _C_LG, _C_END = (
    0, 256, 384, 512, 640, 640 + 3 * NA_HEADS * LANE, 640 + 1536 + 768, 640 + 1536 + 1024, 640 + 1536 + 1280)


def _proj_kernel(x_ref, mod_ref, g1_ref, w1_ref, gq_ref, gkv_ref, wuq_ref, wuqr_ref, wk_ref, wv_ref,
                 cos_ref, sin_ref, q_o, k_o, v_o, nq_o, nk_o, nv_o, hy_o, lu_o, lg_o):
    group, tm, d = x_ref.shape
    m = mod_ref[:, 0]
    h = _normmod(x_ref[...], g1_ref[...], m[:, 1:2], m[:, 0:1]).reshape(group * tm, d)
    z = _dot(h.astype(BF16), w1_ref[...])
    qlat = z[:, _C_QLAT:_C_KVLAT]
    kvlat = z[:, _C_KVLAT:_C_KR]
    kr = z[:, _C_KR:_C_KRR]
    krr = z[:, _C_KRR:_C_NA]
    qn = qlat * lax.rsqrt(jnp.sum(qlat * qlat, axis=-1, keepdims=True) * (1.0 / MLA_Q_RANK) + NORM_EPS) * gq_ref[...]
    kvn = kvlat * lax.rsqrt(jnp.mean(kvlat * kvlat, axis=-1, keepdims=True) + NORM_EPS) * gkv_ref[...]
    qn = qn.astype(BF16)
    kvn = kvn.astype(BF16)
    q = _dot(qn, wuq_ref[...])
    qr = _dot(qn, wuqr_ref[...])
    kk = _dot(kvn, wk_ref[...])
    vv = _dot(kvn, wv_ref[...])
    cos = jnp.concatenate([cos_ref[...]] * group, axis=0)
    sin = jnp.concatenate([sin_ref[...]] * group, axis=0)
    krope = kr * cos + krr * sin
    den_lane = lax.broadcasted_iota(I32, cos.shape, 1) == SOFTMAX_DEN_LANE
    def put_heads(ref, hd, val):
        for g in range(group):
            ref[g, hd] = val[g * tm:(g + 1) * tm].astype(BF16)

    def put_pairs(ref, hd, val):
        off = (hd % 2) * LANE
        for g in range(group):
            ref[g, hd // 2, :, off:off + LANE] = val[g * tm:(g + 1) * tm].astype(BF16)

    for hd in range(MLA_HEADS):
        sl = slice(hd * LANE, (hd + 1) * LANE)
        put_heads(q_o, hd, q[:, sl] * cos + qr[:, sl] * sin)
        put_heads(k_o, hd, kk[:, sl] + krope)
        put_heads(v_o, hd, jnp.where(den_lane, 1.0, vv[:, sl]))
    for hd in range(NA_HEADS):
        for which, ref in enumerate((nq_o, nk_o, nv_o)):
            lo = _C_NA + (which * NA_HEADS + hd) * LANE
            blk = z[:, lo:lo + LANE]
            if which == 2:
                put_pairs(ref, hd, jnp.where(den_lane, 1.0, blk))
            else:
                put_heads(ref, hd, blk)
    hy_o[...] = z[:, _C_HY:_C_LU].reshape(group, tm, _C_LU - _C_HY)
    lu_o[...] = z[:, _C_LU:_C_LG].reshape(group, tm, _C_LG - _C_LU)
    lg_o[...] = z[:, _C_LG:_C_END].reshape(group, tm, _C_END - _C_LG)


def _pad_to(a, n, axis):
    pad = [(0, 0)] * a.ndim
    pad[axis] = (0, n - a.shape[axis])
    return jnp.pad(a, pad)


def _rot_cols(w):
    half = w.shape[-1] // 2
    return jnp.concatenate([-w[..., half:], w[..., :half]], axis=-1)


def _head_blocks(cols_per_head):
    out = []
    for pieces in cols_per_head:
        k = pieces[0][0].shape[0]
        blk = jnp.zeros((k, LANE), F32)
        for arr, off in pieces:
            blk = blk.at[:, off:off + arr.shape[1]].set(arr)
        out.append(blk)
    return jnp.concatenate(out, axis=1)


def _proj_weights(lp):
    w_in = lp['w_in']
    d = w_in.shape[0]
    o = 0
    parts = {}
    for name, n in (('q', MLA_Q_RANK), ('kv', MLA_KV_RANK), ('kr', MLA_ROPE), ('na', 3 * NA_HEADS * NA_HEAD_DIM),
                    ('hy', 3 * HY_WIDTH), ('lu', LRU_WIDTH), ('lg', LRU_WIDTH), ('gt', 4 * d)):
        parts[name] = w_in[:, o:o + n]
        o += n
    zeros = lambda n: jnp.zeros((d, n), F32)
    kr_blk = jnp.concatenate([zeros(MLA_NOPE), parts['kr'], zeros(LANE - MLA_NOPE - MLA_ROPE)], axis=1)
    krr_blk = jnp.concatenate([zeros(MLA_NOPE), _rot_cols(parts['kr']), zeros(LANE - MLA_NOPE - MLA_ROPE)], axis=1)
    na_scale = NA_HEAD_DIM ** -0.5
    na_cols = []
    for which in range(3):
        for hd in range(NA_HEADS):
            lo = (which * NA_HEADS + hd) * NA_HEAD_DIM
            blk = parts['na'][:, lo:lo + NA_HEAD_DIM] * (na_scale if which == 0 else 1.0)
            na_cols.append(_pad_to(blk, LANE, 1))
    w1 = jnp.concatenate([_pad_to(parts['q'], MLA_Q_PAD, 1), parts['kv'], kr_blk, krr_blk] + na_cols
                         + [parts['hy'], parts['lu'], parts['lg']], axis=1).astype(BF16)
    mla_scale = (MLA_NOPE + MLA_ROPE) ** -0.5
    wuq = _pad_to(lp['mla_w_uq'], MLA_Q_PAD, 0) * mla_scale
    dq = MLA_NOPE + MLA_ROPE
    wuq_main = _head_blocks([[(wuq[:, hd * dq:hd * dq + dq], 0)] for hd in range(MLA_HEADS)])
    wuq_rot = _head_blocks([[(_rot_cols(wuq[:, hd * dq + MLA_NOPE:hd * dq + dq]), MLA_NOPE)]
                            for hd in range(MLA_HEADS)])
    dkv = MLA_NOPE + MLA_V
    wukv = lp['mla_w_ukv']
    wk = _head_blocks([[(wukv[:, hd * dkv:hd * dkv + MLA_NOPE], 0)] for hd in range(MLA_HEADS)])
    wv = _head_blocks([[(wukv[:, hd * dkv + MLA_NOPE:hd * dkv + dkv], 0)] for hd in range(MLA_HEADS)])
    gq = _pad_to(lp['mla_g_q'].reshape(1, -1), MLA_Q_PAD, 1)
    gkv = lp['mla_g_kv'].reshape(1, -1)
    return dict(w1=w1, w_gate=parts['gt'].astype(BF16), gq=gq, gkv=gkv, wuq=wuq_main.astype(BF16),
                wuq_rot=wuq_rot.astype(BF16), wk=wk.astype(BF16), wv=wv.astype(BF16))


def _rope_tables(seq, n_tok):
    t = jnp.arange(seq, dtype=I32)
    row = (t // GRID_W).astype(F32)
    col = (t % GRID_W).astype(F32)
    n_axis = MLA_ROPE // 4
    inv_freq = ROPE_THETA ** (-jnp.arange(n_axis, dtype=F32) / n_axis)
    ang = jnp.concatenate([row[:, None] * inv_freq, col[:, None] * inv_freq], axis=-1)
    cos = jnp.concatenate([jnp.cos(ang), jnp.cos(ang)], axis=-1)
    sin = jnp.concatenate([jnp.sin(ang), jnp.sin(ang)], axis=-1)
    cos_t = jnp.ones((n_tok, LANE), F32).at[:seq, MLA_NOPE:MLA_NOPE + MLA_ROPE].set(cos)
    sin_t = jnp.zeros((n_tok, LANE), F32).at[:seq, MLA_NOPE:MLA_NOPE + MLA_ROPE].set(sin)
    return cos_t, sin_t


def _batch_group(b):
    return 2 if b % 2 == 0 else 1


def _kind_map(n_lat_tiles):
    return lambda b, i: (b, jnp.where(i >= n_lat_tiles, 1, 0), 0, 0)


def _project(xa, modtab, g1, pw, rope, n_lat_tiles):
    b, nt, d = xa.shape
    cos_t, sin_t = rope
    group = _batch_group(b)
    full = lambda a: pl.BlockSpec(a.shape, lambda bb, i: (0,) * a.ndim)
    head_out = lambda: pl.BlockSpec((group, MLA_HEADS, TM, LANE), lambda bb, i: (bb, 0, i, 0))
    tok_out = lambda n: pl.BlockSpec((group, TM, n), lambda bb, i: (bb, i, 0))
    head_shape = jax.ShapeDtypeStruct((b, MLA_HEADS, nt, LANE), BF16)
    pair_out = lambda: pl.BlockSpec((group, MLA_HEADS // 2, TM, 2 * LANE), lambda bb, i: (bb, 0, i, 0))
    pair_shape = jax.ShapeDtypeStruct((b, MLA_HEADS // 2, nt, 2 * LANE), BF16)
    tok_shape = lambda n: jax.ShapeDtypeStruct((b, nt, n), F32)
    weights = (g1, pw['w1'], pw['gq'], pw['gkv'], pw['wuq'], pw['wuq_rot'], pw['wk'], pw['wv'])
    return pl.pallas_call(
        _proj_kernel,
        grid=(b // group, nt // TM),
        in_specs=[pl.BlockSpec((group, TM, d), lambda bb, i: (bb, i, 0)),
                  pl.BlockSpec((group, 1, 8, d), _kind_map(n_lat_tiles))]
                 + [full(w) for w in weights]
                 + [pl.BlockSpec((TM, LANE), lambda bb, i: (i, 0))] * 2,
        out_specs=[head_out(), head_out(), head_out(), head_out(), head_out(), pair_out(),
                   tok_out(3 * HY_WIDTH), tok_out(LRU_WIDTH), tok_out(LRU_WIDTH)],
        out_shape=[head_shape, head_shape, head_shape, head_shape, head_shape, pair_shape,
                   tok_shape(3 * HY_WIDTH), tok_shape(LRU_WIDTH), tok_shape(LRU_WIDTH)],
        compiler_params=_cparams("parallel", "parallel"),
        name="input_projection",
    )(xa, modtab, *weights, cos_t, sin_t)


SOFTMAX_DEN_LANE = 64


def _softmax_pv(parts, lane_off=0):
    m = None
    for s, _ in parts:
        mm = jnp.max(s, axis=-1, keepdims=True)
        m = mm if m is None else jnp.maximum(m, mm)
    acc = None
    for s, v in parts:
        o = _dot(jnp.exp(s - m).astype(BF16), v)
        acc = o if acc is None else acc + o
    den = lane_off + SOFTMAX_DEN_LANE
    return acc[:, lane_off:lane_off + LANE] / acc[:, den:den + 1]


def _mla_kernel(q_ref, k_ref, v_ref, o_ref, *, seq, n_lat_tiles):
    i = pl.program_id(1)
    nt = k_ref.shape[2]

    def attend(lo, hi):
        for hd in range(MLA_HEADS):
            s = _dot_nt(q_ref[0, hd], k_ref[0, hd, lo:hi, :])
            o = _softmax_pv([(s, v_ref[0, hd, lo:hi, :])])
            o_ref[0, :, hd * LANE:(hd + 1) * LANE] = o.astype(BF16)

    @pl.when(i < n_lat_tiles)
    def _():
        attend(0, nt)

    @pl.when(i >= n_lat_tiles)
    def _():
        attend(seq, nt)


def _mla_attention(q, k, v, seq, tiles_used):
    b, h, nt, _ = q.shape
    kv_spec = pl.BlockSpec((1, h, nt, LANE), lambda bb, i: (bb, 0, 0, 0))
    return pl.pallas_call(
        functools.partial(_mla_kernel, seq=seq, n_lat_tiles=seq // TM),
        grid=(b, tiles_used),
        in_specs=[pl.BlockSpec((1, h, TM, LANE), lambda bb, i: (bb, 0, i, 0)), kv_spec, kv_spec],
        out_specs=pl.BlockSpec((1, TM, h * LANE), lambda bb, i: (bb, i, 0)),
        out_shape=jax.ShapeDtypeStruct((b, tiles_used * TM, h * LANE), BF16),
        compiler_params=_cparams("parallel", "arbitrary"),
        name="mla_attention",
    )(q, k, v)


def _na_bias_tables(rpb, rows):
    n_blk = rows // NA_TILE_ROWS
    col = np.arange(GRID_W)
    c0 = np.clip(col - NA_WIN_C // 2, 0, GRID_W - NA_WIN_C)
    in_win = (col[None, :] >= c0[:, None]) & (col[None, :] < c0[:, None] + NA_WIN_C)
    dc = np.clip(col[None, :] - col[:, None], 1 - NA_WIN_C, NA_WIN_C - 1) + NA_WIN_C - 1
    rpb = rpb.astype(F32)
    tables = []
    for j in (0, 1, n_blk - 1):
        w0 = min(max(NA_TILE_ROWS * j - NA_WIN_R // 2, 0), rows - NA_KEY_ROWS)
        r = NA_TILE_ROWS * j + np.arange(NA_TILE_ROWS)
        kr = w0 + np.arange(NA_KEY_ROWS)
        r0 = np.clip(r - NA_WIN_R // 2, 0, rows - NA_WIN_R)
        row_ok = (kr[None, :] >= r0[:, None]) & (kr[None, :] < r0[:, None] + NA_WIN_R)
        dr = np.clip(kr[None, :] - r[:, None] + NA_WIN_R - 1, 0, 2 * NA_WIN_R - 2)
        oh_r = jnp.asarray(np.eye(2 * NA_WIN_R - 1, dtype=np.float32)[dr.reshape(-1)])
        oh_c = jnp.asarray(np.eye(2 * NA_WIN_C - 1, dtype=np.float32)[dc.reshape(-1)])
        bias = jnp.einsum('ar,hrc,bc->hab', oh_r, rpb, oh_c, precision=lax.Precision.HIGHEST)
        bias = bias.reshape(NA_HEADS, NA_TILE_ROWS, NA_KEY_ROWS, GRID_W, GRID_W)
        mask = row_ok[:, :, None, None] & in_win[None, None, :, :]
        bias = jnp.where(jnp.asarray(mask)[None], bias, -jnp.inf)
        tables.append(bias.transpose(0, 1, 3, 2, 4).reshape(NA_HEADS, TM, NA_KEYS))
    return jnp.stack(tables)


def _na_kernel(q_ref, k_ref, v_ref, bias_ref, o_ref, *, seq, n_lat_tiles):
    i = pl.program_id(1)
    nt = k_ref.shape[2]
    rows = seq // GRID_W

    @pl.when(i < n_lat_tiles)
    def _():
        w0 = jnp.clip(NA_TILE_ROWS * i - NA_WIN_R // 2, 0, rows - NA_KEY_ROWS)
        start = pl.multiple_of(w0 * GRID_W, GRID_W)
        for hd in range(NA_HEADS):
            q = q_ref[0, hd]
            s_loc = _dot_nt(q, k_ref[0, hd, pl.ds(start, NA_KEYS), :]) + bias_ref[0, hd]
            s_ctx = _dot_nt(q, k_ref[0, hd, seq:nt, :])
            o = _softmax_pv([(s_loc, v_ref[0, hd // 2, pl.ds(start, NA_KEYS), :]),
                             (s_ctx, v_ref[0, hd // 2, seq:nt, :])], (hd % 2) * LANE)
            o_ref[0, :, hd * LANE:(hd + 1) * LANE] = o.astype(BF16)

    @pl.when(i >= n_lat_tiles)
    def _():
        for hd in range(NA_HEADS):
            s = _dot_nt(q_ref[0, hd], k_ref[0, hd, seq:nt, :])
            o = _softmax_pv([(s, v_ref[0, hd // 2, seq:nt, :])], (hd % 2) * LANE)
            o_ref[0, :, hd * LANE:(hd + 1) * LANE] = o.astype(BF16)


def _na_attention(q, k, v, bias, seq, tiles_used):
    b, h, nt, _ = q.shape
    n_lat = seq // TM
    kv_spec = pl.BlockSpec((1, h, nt, LANE), lambda bb, i: (bb, 0, 0, 0))
    cfg = lambda bb, i: (jnp.where(i == 0, 0, jnp.where(i >= n_lat - 1, 2, 1)), 0, 0, 0)
    return pl.pallas_call(
        functools.partial(_na_kernel, seq=seq, n_lat_tiles=n_lat),
        grid=(b, tiles_used),
        in_specs=[pl.BlockSpec((1, h, TM, LANE), lambda bb, i: (bb, 0, i, 0)), kv_spec,
                  pl.BlockSpec((1, h // 2, nt, 2 * LANE), lambda bb, i: (bb, 0, 0, 0)),
                  pl.BlockSpec((1, h, TM, NA_KEYS), cfg)],
        out_specs=pl.BlockSpec((1, TM, h * LANE), lambda bb, i: (bb, i, 0)),
        out_shape=jax.ShapeDtypeStruct((b, tiles_used * TM, h * LANE), BF16),
        compiler_params=_cparams("parallel", "arbitrary"),
        name="neighbourhood_attention",
    )(q, k, v, bias)


HY_PRE_CHUNK = 256


def _hy_pre_kernel(z_ref, w_ref, b_ref, lat_o, lat_bf_o, ctx_o, ctx_bf_o, *, seq, ctx_len):
    w = w_ref[...]
    bias = b_ref[...]
    ch = HY_PRE_CHUNK
    zero = jnp.zeros((8, HY_WIDTH), F32)
    for seg_lo, seg_len, o_ref, obf_ref in ((0, seq, lat_o, lat_bf_o), (seq, ctx_len, ctx_o, ctx_bf_o)):
        for c in range(seg_len // ch):
            s = seg_lo + c * ch
            before = zero if c == 0 else z_ref[0, s - 8:s, :]
            after = zero if c == seg_len // ch - 1 else z_ref[0, s + ch:s + ch + 8, :]
            win = jnp.concatenate([before, z_ref[0, s:s + ch, :], after], axis=0)
            n = ch + 16
            y = bias + w[1:2] * win[8:8 + ch]
            y = y + w[0:1] * pltpu.roll(win, 1, axis=0)[8:8 + ch]
            y = y + w[2:3] * pltpu.roll(win, n - 1, axis=0)[8:8 + ch]
            o_ref[0, c * ch:(c + 1) * ch, :] = y
            obf_ref[0, c * ch:(c + 1) * ch, :] = y.astype(BF16)


def _hy_pre(hy, w, bvec, seq):
    b, nt, _ = hy.shape
    ctx_len = nt - seq
    wpad = _pad_to(w, 8, 0)
    out_specs, out_shape = [], []
    for length in (seq, ctx_len):
        for dt in (F32, BF16):
            out_specs.append(pl.BlockSpec((1, length, HY_WIDTH), lambda bb, g: (g, 0, bb)))
            out_shape.append(jax.ShapeDtypeStruct((3, length, b * HY_WIDTH), dt))
    return pl.pallas_call(
        functools.partial(_hy_pre_kernel, seq=seq, ctx_len=ctx_len),
        grid=(b, 3),
        in_specs=[pl.BlockSpec((1, nt, HY_WIDTH), lambda bb, g: (bb, 0, g)),
                  pl.BlockSpec((8, HY_WIDTH), lambda bb, g: (0, g)),
                  pl.BlockSpec((1, HY_WIDTH), lambda bb, g: (0, g))],
        out_specs=out_specs,
        out_shape=out_shape,
        compiler_params=_cparams("parallel", "parallel"),
        name="hyena_short_conv",
    )(hy, wpad, bvec.reshape(1, -1))


def _hp_dot(a, b):
    return jnp.dot(a, b, preferred_element_type=F32, precision=lax.Precision.HIGHEST)


def _hy_filter_kernel(f_ref, w1_ref, b1_ref, w2_ref, b2_ref, w3_ref, dec_ref, o_ref, obf_ref):
    f = f_ref[...]
    h = jnp.sin(_hp_dot(f, w1_ref[...]) + b1_ref[...])
    h = jnp.sin(_hp_dot(h, w2_ref[...]) + b2_ref[...])
    h = _hp_dot(h, w3_ref[...])
    h = h * jnp.exp(-f[:, 0:1] * jnp.abs(dec_ref[...]))
    o_ref[...] = h
    obf_ref[...] = h.astype(BF16)


def _hy_pos_features(length):
    t = jnp.linspace(0.0, 1.0, length, dtype=F32)[:, None]
    w = 2.0 * math.pi * jnp.arange(length, dtype=F32)[:, None] / length
    f = jnp.linspace(1e-4, HY_BANDS - 1, HY_BANDS, dtype=F32)[None, :]
    z = w * f
    return jnp.concatenate([t, jnp.cos(z), -jnp.sin(z)], axis=-1)


def _hy_filters(length, lp):
    feats = _hy_pos_features(length)
    n_out = HY_ORDER * 2 * HY_WIDTH
    tl = min(length, 512)
    full = lambda a: pl.BlockSpec(a.shape, lambda i: (0,) * a.ndim)
    args = (lp['hy_w1'], lp['hy_b1'].reshape(1, -1), lp['hy_w2'], lp['hy_b2'].reshape(1, -1), lp['hy_w3'],
            lp['hy_decay'].reshape(1, n_out))
    return pl.pallas_call(
        _hy_filter_kernel,
        grid=(length // tl,),
        in_specs=[pl.BlockSpec((tl, HY_EMB), lambda i: (i, 0))] + [full(a) for a in args],
        out_specs=[pl.BlockSpec((tl, n_out), lambda i: (i, 0))] * 2,
        out_shape=[jax.ShapeDtypeStruct((length, n_out), F32), jax.ShapeDtypeStruct((length, n_out), BF16)],
        compiler_params=_cparams("parallel"),
        name="hyena_filter_mlp",
    )(feats, *args)


def _dft_matrices(length):
    n2 = 2 * length
    step = DFT_ROW_STEP
    n = jnp.arange(length, dtype=I32)

    def trig(kv):
        ang = ((kv[:, None] * n[None, :]) % n2).astype(F32) * (2.0 * math.pi / n2)
        return jnp.cos(ang), jnp.sin(ang)

    c1, s1 = trig(jnp.arange(length // step, dtype=I32) * step)
    c2, s2 = trig(jnp.arange(step, dtype=I32))
    coarse = pl.BlockSpec((1, 1, length), lambda j: (j, 0, 0))
    fine = pl.BlockSpec((step, length), lambda j: (0, 0))
    out = pl.BlockSpec((step, length), lambda j: (j, 0))
    return pl.pallas_call(
        _dft_matrix_kernel,
        grid=(length // step,),
        in_specs=[coarse, coarse, fine, fine],
        out_specs=[out, out, out],
        out_shape=[jax.ShapeDtypeStruct((length, length), BF16)] * 3,
        compiler_params=_cparams("parallel"),
        name="dft_matrices",
    )(c1[:, None, :], s1[:, None, :], c2, s2)


DFT_ROW_STEP = 64


def _dft_matrix_kernel(c1_ref, s1_ref, c2_ref, s2_ref, mc_o, ms_o, mst_o):
    c1, s1 = c1_ref[0], s1_ref[0]
    c2, s2 = c2_ref[...], s2_ref[...]
    cos = c1 * c2 - s1 * s2
    neg_sin = -(s1 * c2 + c1 * s2)
    row = lax.broadcasted_iota(I32, cos.shape, 0)
    col = lax.broadcasted_iota(I32, cos.shape, 1)
    first_row = (row + pl.program_id(0) * DFT_ROW_STEP) == 0
    mc_o[...] = cos.astype(BF16)
    ms_o[...] = jnp.where(first_row, jnp.where(col % 2 == 0, 1.0, -1.0), neg_sin).astype(BF16)
    mst_o[...] = jnp.where(col == 0, jnp.where(row % 2 == 0, 1.0, -1.0), neg_sin).astype(BF16)


def _dft_fwd_kernel(mc_ref, ms_ref, x_ref, *rest, with_taps):
    ure = _dot(mc_ref[...], x_ref[...])
    uim = _dot(ms_ref[...], x_ref[...])
    if not with_taps:
        ure_o, uim_o = rest
        ure_o[...] = ure
        uim_o[...] = uim
        return
    a_ref, b_ref, c_ref, d_ref, zre_o, zim_o = rest
    a, b, c, d = a_ref[...], b_ref[...], c_ref[...], d_ref[...]
    for bb in range(x_ref.shape[1] // HY_WIDTH):
        sl = slice(bb * HY_WIDTH, (bb + 1) * HY_WIDTH)
        zre_o[:, sl] = (ure[:, sl] * a - uim[:, sl] * b).astype(BF16)
        zim_o[:, sl] = (ure[:, sl] * c + uim[:, sl] * d).astype(BF16)


def _col_block(nc, cap):
    return min(nc, cap)


def _dft_fwd(mats, x, plane, taps=None):
    mc, ms, _ = mats
    _, length, nc = x.shape
    tk = min(length, 256)
    cb = _col_block(nc, 1024)
    grid = (nc // cb, length // tk)
    m_spec = pl.BlockSpec((tk, length), lambda c, j: (j, 0))
    x_spec = pl.BlockSpec((None, length, cb), lambda c, j: (plane, 0, c))
    o_spec = pl.BlockSpec((tk, cb), lambda c, j: (j, c))
    if taps is None:
        return pl.pallas_call(
            functools.partial(_dft_fwd_kernel, with_taps=False),
            grid=grid, in_specs=[m_spec, m_spec, x_spec], out_specs=[o_spec, o_spec],
            out_shape=[jax.ShapeDtypeStruct((length, nc), F32)] * 2,
            compiler_params=_cparams("parallel", "arbitrary"),
            name="hyena_dft_filters",
        )(mc, ms, x)
    t_spec = pl.BlockSpec((tk, HY_WIDTH), lambda c, j: (j, 0))
    return pl.pallas_call(
        functools.partial(_dft_fwd_kernel, with_taps=True),
        grid=grid, in_specs=[m_spec, m_spec, x_spec] + [t_spec] * 4, out_specs=[o_spec, o_spec],
        out_shape=[jax.ShapeDtypeStruct((length, nc), BF16)] * 2,
        compiler_params=_cparams("parallel", "arbitrary"),
        name="hyena_dft_forward",
    )(mc, ms, x, *taps)


def _dft_inv_kernel(mc_ref, mst_ref, zre_ref, zim_ref, gate_ref, prev_ref, bias_ref, *outs, last):
    conv = _dot(mc_ref[...], zre_ref[...]) + _dot(mst_ref[...], zim_ref[...])
    y = gate_ref[...] * (conv + prev_ref[...] * bias_ref[...])
    if last:
        (tok_o,) = outs
        for bb in range(y.shape[1] // HY_WIDTH):
            tok_o[bb] = y[:, bb * HY_WIDTH:(bb + 1) * HY_WIDTH].astype(BF16)
    else:
        y_o, ybf_o = outs
        y_o[...] = y
        ybf_o[...] = y.astype(BF16)


def _dft_inv(mats, zre, zim, gate, gate_plane, prev, prev_plane, bias_row, last):
    mc, _, mst = mats
    length, nc = zre.shape
    tm = min(length, 256)
    cb = _col_block(nc, 512)
    grid = (nc // cb, length // tm)
    m_spec = pl.BlockSpec((tm, length), lambda c, i: (i, 0))
    z_spec = pl.BlockSpec((length, cb), lambda c, i: (0, c))
    e_spec = lambda plane: pl.BlockSpec((None, tm, cb), lambda c, i: (plane, i, c))
    b_spec = pl.BlockSpec((1, cb), lambda c, i: (0, c))
    if last:
        nb = nc // HY_WIDTH
        out_specs = [pl.BlockSpec((cb // HY_WIDTH, tm, HY_WIDTH), lambda c, i: (c, i, 0))]
        out_shape = [jax.ShapeDtypeStruct((nb, length, HY_WIDTH), BF16)]
    else:
        out_specs = [e_spec(0), e_spec(0)]
        out_shape = [jax.ShapeDtypeStruct((1, length, nc), F32), jax.ShapeDtypeStruct((1, length, nc), BF16)]
    return pl.pallas_call(
        functools.partial(_dft_inv_kernel, last=last),
        grid=grid, in_specs=[m_spec, m_spec, z_spec, z_spec, e_spec(gate_plane), e_spec(prev_plane), b_spec],
        out_specs=out_specs, out_shape=out_shape,
        compiler_params=_cparams("parallel", "arbitrary"),
        name="hyena_dft_inverse",
    )(mc, mst, zre, zim, gate, prev, bias_row)


def _hy_tap_tables(ure, uim, filt, length):
    n2 = 2 * length
    w = HY_WIDTH
    scale = jnp.full((length, 1), 2.0 / n2, F32).at[0, 0].set(1.0 / n2)
    tables = []
    for o in range(HY_ORDER):
        f_sl = slice((2 * o) * w, (2 * o + 1) * w)
        b_sl = slice((2 * o + 1) * w, (2 * o + 2) * w)
        hb0 = filt[0:1, b_sl]
        tre = ure[:, f_sl] + ure[:, b_sl] - hb0
        tim = uim[:, f_sl] - uim[:, b_sl]
        t_nyq = uim[0:1, f_sl] + uim[0:1, b_sl] - hb0
        first = (jnp.arange(length) == 0)[:, None]
        a = tre * scale
        bm = jnp.where(first, 0.0, tim * scale)
        c = bm
        d = jnp.where(first, t_nyq * scale, tre * scale)
        tables.append((a, bm, c, d))
    return tables


def _hyena_seq(mats, vx, vx_bf, lp, n_batch):
    length = vx.shape[1]
    filt, filt_bf = _hy_filters(length, lp)
    ure, uim = _dft_fwd(mats, filt_bf[None], 0)
    tables = _hy_tap_tables(ure, uim, filt, length)
    bias = lp['hy_bias'].astype(F32)
    y, y_bf = vx, vx_bf
    for o in range(HY_ORDER):
        zre, zim = _dft_fwd(mats, y_bf, 0, tables[o])
        bias_row = jnp.tile(bias[o][None, :], (1, n_batch))
        last = o == HY_ORDER - 1
        res = _dft_inv(mats, zre, zim, vx, o + 1, y, 0, bias_row, last)
        if last:
            return res[0]
        y, y_bf = res


def _hyp_pre_kernel(z_ref, w_ref, b_ref, lat_o, lat_bf_o, ctx_o, ctx_bf_o, *, seq, ctx_len):
    w = w_ref[...]
    bias = b_ref[...]
    for lo, length, o_ref, obf_ref in ((0, seq, lat_o, lat_bf_o), (seq, ctx_len, ctx_o, ctx_bf_o)):
        h = length // 2
        even = z_ref[0, pl.ds(lo, h, stride=2), :]
        odd = z_ref[0, pl.ds(lo + 1, h, stride=2), :]
        row = lax.broadcasted_iota(I32, even.shape, 0)
        odd_prev = jnp.where(row == 0, 0.0, pltpu.roll(odd, 1, axis=0))
        even_next = jnp.where(row == h - 1, 0.0, pltpu.roll(even, h - 1, axis=0))
        y_even = bias + w[0:1] * odd_prev + w[1:2] * even + w[2:3] * odd
        y_odd = bias + w[0:1] * even + w[1:2] * odd + w[2:3] * even_next
        for r, y in enumerate((y_even, y_odd)):
            o_ref[0, r] = y
            obf_ref[0, r] = y.astype(BF16)


def _hyp_pre(hy, w, bvec, seq):
    b, nt, _ = hy.shape
    ctx_len = nt - seq
    wpad = _pad_to(w, 8, 0)
    per_plane = HY_WIDTH // LANE
    out_specs, out_shape = [], []
    for length in (seq, ctx_len):
        for dt in (F32, BF16):
            out_specs.append(pl.BlockSpec((1, 2, length // 2, LANE),
                                          lambda bb, g: (g // per_plane, 0, 0, bb * per_plane + g % per_plane)))
            out_shape.append(jax.ShapeDtypeStruct((3, 2, length // 2, b * HY_WIDTH), dt))
    return pl.pallas_call(
        functools.partial(_hyp_pre_kernel, seq=seq, ctx_len=ctx_len),
        grid=(b, 3 * per_plane),
        in_specs=[pl.BlockSpec((1, nt, LANE), lambda bb, g: (bb, 0, g)),
                  pl.BlockSpec((8, LANE), lambda bb, g: (0, g)),
                  pl.BlockSpec((1, LANE), lambda bb, g: (0, g))],
        out_specs=out_specs,
        out_shape=out_shape,
        compiler_params=_cparams("parallel", "parallel"),
        name="hyena_short_conv",
    )(hy, wpad, bvec.reshape(1, -1))


def _hyp_filters(length, lp):
    feats = _hy_pos_features(length)
    feats = feats.reshape(length // 2, 2, HY_EMB).transpose(1, 0, 2).reshape(length, HY_EMB)
    n_out = HY_ORDER * 2 * HY_WIDTH
    tl = min(length // 2, 512)
    full = lambda a: pl.BlockSpec(a.shape, lambda i: (0,) * a.ndim)
    args = (lp['hy_w1'], lp['hy_b1'].reshape(1, -1), lp['hy_w2'], lp['hy_b2'].reshape(1, -1), lp['hy_w3'],
            lp['hy_decay'].reshape(1, n_out))
    filt, filt_bf = pl.pallas_call(
        _hy_filter_kernel,
        grid=(length // tl,),
        in_specs=[pl.BlockSpec((tl, HY_EMB), lambda i: (i, 0))] + [full(a) for a in args],
        out_specs=[pl.BlockSpec((tl, n_out), lambda i: (i, 0))] * 2,
        out_shape=[jax.ShapeDtypeStruct((length, n_out), F32), jax.ShapeDtypeStruct((length, n_out), BF16)],
        compiler_params=_cparams("parallel"),
        name="hyena_filter_mlp",
    )(feats, *args)
    return filt.reshape(2, length // 2, n_out), filt_bf.reshape(2, length // 2, n_out)


def _hyp_matrix_kernel(c1_ref, s1_ref, c2_ref, s2_ref, ck_ref, sk_ref, ckr_ref, skr_ref,
                       ce_o, se_o, co_o, so_o, set_o, cot_o, sot_o):
    c1, s1 = c1_ref[0], s1_ref[0]
    c2, s2 = c2_ref[...], s2_ref[...]
    cos = c1 * c2 - s1 * s2
    sin = s1 * c2 + c1 * s2
    row = lax.broadcasted_iota(I32, cos.shape, 0)
    col = lax.broadcasted_iota(I32, cos.shape, 1)
    first_row = (row + pl.program_id(0) * DFT_ROW_STEP) == 0
    alt_col = jnp.where(col % 2 == 0, 1.0, -1.0)
    alt_row = jnp.where(row % 2 == 0, 1.0, -1.0)
    ck, sk = ck_ref[...], sk_ref[...]
    ce_o[...] = cos.astype(BF16)
    se_o[...] = jnp.where(first_row, alt_col, -sin).astype(BF16)
    co_o[...] = (cos * ck - sin * sk).astype(BF16)
    so_o[...] = jnp.where(first_row, alt_col, -(sin * ck + cos * sk)).astype(BF16)
    ckr, skr = ckr_ref[...], skr_ref[...]
    set_o[...] = jnp.where(col == 0, alt_row, -sin).astype(BF16)
    cot_o[...] = (cos * ckr - sin * skr).astype(BF16)
    sot_o[...] = jnp.where(col == 0, alt_row, -(sin * ckr + cos * skr)).astype(BF16)


def _hyp_matrices(length):
    h = length // 2
    step = DFT_ROW_STEP
    m = jnp.arange(h, dtype=I32)

    def trig(kv):
        ang = ((kv[:, None] * m[None, :]) % length).astype(F32) * (2.0 * math.pi / length)
        return jnp.cos(ang), jnp.sin(ang)

    c1, s1 = trig(jnp.arange(h // step, dtype=I32) * step)
    c2, s2 = trig(jnp.arange(step, dtype=I32))
    half_angle = m.astype(F32) * (math.pi / length)
    ck, sk = jnp.cos(half_angle), jnp.sin(half_angle)
    coarse = pl.BlockSpec((1, 1, h), lambda j: (j, 0, 0))
    fine = pl.BlockSpec((step, h), lambda j: (0, 0))
    per_row = pl.BlockSpec((step, 1), lambda j: (j, 0))
    per_col = pl.BlockSpec((1, h), lambda j: (0, 0))
    out = pl.BlockSpec((step, h), lambda j: (j, 0))
    ce, se, co, so, se_t, co_t, so_t = pl.pallas_call(
        _hyp_matrix_kernel,
        grid=(h // step,),
        in_specs=[coarse, coarse, fine, fine, per_row, per_row, per_col, per_col],
        out_specs=[out] * 7,
        out_shape=[jax.ShapeDtypeStruct((h, h), BF16)] * 7,
        compiler_params=_cparams("parallel"),
        name="dft_matrices",
    )(c1[:, None, :], s1[:, None, :], c2, s2, ck[:, None], sk[:, None], ck[None, :], sk[None, :])
    return dict(fwd=(ce, se, co, so), inv=(ce, se_t, co_t, so_t))


def _hyp_fwd_kernel(ce_ref, se_ref, co_ref, so_ref, xe_ref, xo_ref, *rest, with_taps):
    xe, xo = xe_ref[...], xo_ref[...]
    a_re, a_im = _dot(ce_ref[...], xe), _dot(se_ref[...], xe)
    b_re, b_im = _dot(co_ref[...], xo), _dot(so_ref[...], xo)
    if not with_taps:
        for ref, val in zip(rest, (a_re, a_im, b_re, b_im)):
            ref[...] = val
        return
    t1re_ref, t1im_ref, t2re_ref, t2im_ref, sp_ref, gere_o, geim_o, gore_o, goim_o = rest
    t1re, t1im, t2re, t2im = t1re_ref[...], t1im_ref[...], t2re_ref[...], t2im_ref[...]
    sp = sp_ref[...]
    first_block = pl.program_id(1) == 0
    row8 = lax.broadcasted_iota(I32, (8, HY_WIDTH), 0)
    for bb in range(xe.shape[1] // HY_WIDTH):
        sl = slice(bb * HY_WIDTH, (bb + 1) * HY_WIDTH)
        are, aim, bre, bim = a_re[:, sl], a_im[:, sl], b_re[:, sl], b_im[:, sl]
        u1re, u1im = are + bre, aim + bim
        u2re, u2im = are - bre, bim - aim
        z1re, z1im = u1re * t1re - u1im * t1im, u1re * t1im + u1im * t1re
        z2re, z2im = u2re * t2re - u2im * t2im, u2re * t2im + u2im * t2re
        gere_o[:, sl] = (z1re + z2re).astype(BF16)
        geim_o[:, sl] = (z1im - z2im).astype(BF16)
        gore_o[:, sl] = (z1re - z2re).astype(BF16)
        goim_o[:, sl] = (z1im + z2im).astype(BF16)

        @pl.when(first_block)
        def _():
            u0, ul = u1re[0:8], u2re[0:8]
            a_s, b_s = aim[0:8], bim[0:8]
            dc, ny, mre, mim = sp[0:1], sp[1:2], sp[2:3], sp[3:4]
            first = row8 == 0
            gere_o[0:8, sl] = jnp.where(first, u0 * dc + ul * ny, (z1re + z2re)[0:8]).astype(BF16)
            gore_o[0:8, sl] = jnp.where(first, u0 * dc - ul * ny, (z1re - z2re)[0:8]).astype(BF16)
            geim_o[0:8, sl] = jnp.where(first, a_s * mre + b_s * mim, (z1im - z2im)[0:8]).astype(BF16)
            goim_o[0:8, sl] = jnp.where(first, b_s * mre - a_s * mim, (z1im + z2im)[0:8]).astype(BF16)


def _hyp_fwd(mats, x, plane, taps=None):
    _, _, h, nc = x.shape
    tk = min(h, 256)
    cb = _col_block(nc, 1024)
    grid = (nc // cb, h // tk)
    m_spec = pl.BlockSpec((tk, h), lambda c, j: (j, 0))
    x_spec = lambda r: pl.BlockSpec((None, None, h, cb), lambda c, j: (plane, r, 0, c))
    o_spec = pl.BlockSpec((tk, cb), lambda c, j: (j, c))
    if taps is None:
        return pl.pallas_call(
            functools.partial(_hyp_fwd_kernel, with_taps=False),
            grid=grid, in_specs=[m_spec] * 4 + [x_spec(0), x_spec(1)], out_specs=[o_spec] * 4,
            out_shape=[jax.ShapeDtypeStruct((h, nc), F32)] * 4,
            compiler_params=_cparams("parallel", "arbitrary"),
            name="hyena_dft_filters",
        )(*mats['fwd'], x, x)
    t_spec = pl.BlockSpec((tk, HY_WIDTH), lambda c, j: (j, 0))
    sp_spec = pl.BlockSpec((8, HY_WIDTH), lambda c, j: (0, 0))
    return pl.pallas_call(
        functools.partial(_hyp_fwd_kernel, with_taps=True),
        grid=grid, in_specs=[m_spec] * 4 + [x_spec(0), x_spec(1)] + [t_spec] * 4 + [sp_spec],
        out_specs=[o_spec] * 4,
        out_shape=[jax.ShapeDtypeStruct((h, nc), BF16)] * 4,
        compiler_params=_cparams("parallel", "arbitrary"),
        name="hyena_dft_forward",
    )(*mats['fwd'], x, x, *taps)


def _hyp_inv_kernel(ce_ref, set_ref, cot_ref, sot_ref, gere_ref, geim_ref, gore_ref, goim_ref,
                    gate_ref, prev_ref, bias_ref, *outs, last):
    conv = (_dot(ce_ref[...], gere_ref[...]) + _dot(set_ref[...], geim_ref[...]),
            _dot(cot_ref[...], gore_ref[...]) + _dot(sot_ref[...], goim_ref[...]))
    bias = bias_ref[...]
    for r in range(2):
        y = gate_ref[r] * (conv[r] + prev_ref[r] * bias)
        if last:
            (tok_o,) = outs
            for bb in range(y.shape[1] // HY_WIDTH):
                tok_o[bb, :, r * HY_WIDTH:(r + 1) * HY_WIDTH] = y[:, bb * HY_WIDTH:(bb + 1) * HY_WIDTH].astype(BF16)
        else:
            y_o, ybf_o = outs
            y_o[r] = y
            ybf_o[r] = y.astype(BF16)


def _hyp_inv(mats, g, gate, gate_plane, prev, prev_plane, bias_row, last):
    h, nc = g[0].shape
    tm = min(h, 256)
    cb = _col_block(nc, 512)
    grid = (nc // cb, h // tm)
    m_spec = pl.BlockSpec((tm, h), lambda c, i: (i, 0))
    g_spec = pl.BlockSpec((h, cb), lambda c, i: (0, c))
    e_spec = lambda plane: pl.BlockSpec((None, 2, tm, cb), lambda c, i: (plane, 0, i, c))
    b_spec = pl.BlockSpec((1, cb), lambda c, i: (0, c))
    if last:
        out_specs = [pl.BlockSpec((cb // HY_WIDTH, tm, 2 * HY_WIDTH), lambda c, i: (c, i, 0))]
        out_shape = [jax.ShapeDtypeStruct((nc // HY_WIDTH, h, 2 * HY_WIDTH), BF16)]
    else:
        out_specs = [e_spec(0), e_spec(0)]
        out_shape = [jax.ShapeDtypeStruct((1, 2, h, nc), F32), jax.ShapeDtypeStruct((1, 2, h, nc), BF16)]
    return pl.pallas_call(
        functools.partial(_hyp_inv_kernel, last=last),
        grid=grid, in_specs=[m_spec] * 4 + [g_spec] * 4 + [e_spec(gate_plane), e_spec(prev_plane), b_spec],
        out_specs=out_specs, out_shape=out_shape,
        compiler_params=_cparams("parallel", "arbitrary"),
        name="hyena_dft_inverse",
    )(*mats['inv'], *g, gate, prev, bias_row)


def _hyp_tap_tables(spec, filt, length):
    a_re, a_im, b_re, b_im = spec
    w = HY_WIDTH
    inv_n = 1.0 / (2 * length)
    tables = []
    for o in range(HY_ORDER):
        f_sl = slice((2 * o) * w, (2 * o + 1) * w)
        r_sl = slice((2 * o + 1) * w, (2 * o + 2) * w)
        hb0 = filt[0, 0:1, r_sl]
        f1re = (a_re + b_re)[:, f_sl] + (a_re + b_re)[:, r_sl] - hb0
        f1im = (a_im + b_im)[:, f_sl] - (a_im + b_im)[:, r_sl]
        f2re = (a_re - b_re)[:, f_sl] + (a_re - b_re)[:, r_sl] - hb0
        f2im = (b_im - a_im)[:, f_sl] - (b_im - a_im)[:, r_sl]
        dc = f1re[0:1]
        ny = f2re[0:1]
        mid_re = a_im[0:1, f_sl] + a_im[0:1, r_sl] - hb0
        mid_im = -b_im[0:1, f_sl] + b_im[0:1, r_sl]
        sp = jnp.concatenate([dc * inv_n, ny * inv_n, mid_re * (2 * inv_n), mid_im * (2 * inv_n),
                              jnp.zeros((4, w), F32)], axis=0)
        tables.append((f1re * (2 * inv_n), f1im * (2 * inv_n), f2re * (2 * inv_n), f2im * (2 * inv_n), sp))
    return tables


def _hyena_seq(mats, vx, vx_bf, lp, n_batch):
    h = vx.shape[2]
    length = 2 * h
    filt, filt_bf = _hyp_filters(length, lp)
    spec = _hyp_fwd(mats, filt_bf[None], 0)
    tables = _hyp_tap_tables(spec, filt, length)
    bias = lp['hy_bias'].astype(F32)
    y, y_bf = vx, vx_bf
    for o in range(HY_ORDER):
        g = _hyp_fwd(mats, y_bf, 0, tables[o])
        bias_row = jnp.tile(bias[o][None, :], (1, n_batch))
        last = o == HY_ORDER - 1
        res = _hyp_inv(mats, g, vx, o + 1, y, 0, bias_row, last)
        if last:
            return res[0].reshape(n_batch, length, HY_WIDTH)
        y, y_bf = res


def _lru_kernel(u_ref, g_ref, cw_ref, cb_ref, wa_ref, ba_ref, wx_ref, bx_ref, lam_ref, o_ref,
                pad_ref, y_ref, *, seq, ctx_len):
    tc = LRU_CHUNK
    halo = LRU_HALO
    width = LRU_WIDTH
    lat_off = halo
    ctx_off = 2 * halo + seq
    zero = jnp.zeros((halo, width), F32)
    pad_ref[0:halo, :] = zero
    pad_ref[lat_off:lat_off + seq, :] = u_ref[0, 0:seq, :]
    pad_ref[lat_off + seq:ctx_off, :] = zero
    pad_ref[ctx_off:ctx_off + ctx_len, :] = u_ref[0, seq:seq + ctx_len, :]
    pad_ref[ctx_off + ctx_len:ctx_off + ctx_len + halo, :] = zero
    row = lax.broadcasted_iota(I32, (tc, width), 0)
    n_win = tc + 2 * halo

    def chunk(pad_off, y_off, s, carry, d):
        wstart = pl.multiple_of(pad_off + s - halo, 8)
        win = pad_ref[pl.ds(wstart, n_win), :]
        cw = cw_ref[d]
        xc = cb_ref[d]
        for k in range(LRU_CONV):
            shift = (LRU_CONV - 1 - k) if d == 0 else -k
            rolled = win if shift == 0 else pltpu.roll(win, shift % n_win, axis=0)
            xc = xc + cw[k:k + 1] * rolled[halo:halo + tc]
        xb = xc.astype(BF16)
        r = _sigmoid(_dot(xb, wa_ref[d]) + ba_ref[d])
        gi = _sigmoid(_dot(xb, wx_ref[d]) + bx_ref[d])
        lam = lam_ref[d]
        softplus = jnp.maximum(-lam, 0.0) + jnp.log1p(jnp.exp(-jnp.abs(lam)))
        log_a = -LRU_C * r * softplus
        a = jnp.exp(log_a)
        bt = jnp.sqrt(-jnp.tanh(log_a) * (a * a + 1.0)) * (gi * xc)
        sft = 1
        while sft < tc:
            if d == 0:
                keep = row >= sft
                a_s = jnp.where(keep, pltpu.roll(a, sft, axis=0), 1.0)
                b_s = jnp.where(keep, pltpu.roll(bt, sft, axis=0), 0.0)
            else:
                keep = row < tc - sft
                a_s = jnp.where(keep, pltpu.roll(a, tc - sft, axis=0), 1.0)
                b_s = jnp.where(keep, pltpu.roll(bt, tc - sft, axis=0), 0.0)
            bt = a * b_s + bt
            a = a * a_s
            sft *= 2
        h = a * carry + bt
        yo = pl.multiple_of(y_off + s, 8)
        if d == 0:
            y_ref[pl.ds(yo, tc), :] = h
            return h[tc - 1:tc]
        y_ref[pl.ds(yo, tc), :] = y_ref[pl.ds(yo, tc), :] + h
        return h[0:1]

    n_lat = seq // tc
    n_ctx = ctx_len // tc
    for d in range(2):
        carry = jnp.zeros((1, width), F32)
        order = range(n_ctx) if d == 0 else range(n_ctx - 1, -1, -1)
        for c in order:
            carry = chunk(ctx_off, seq, c * tc, carry, d)

        def body(j, cr, d=d):
            jj = j if d == 0 else n_lat - 1 - j
            return chunk(lat_off, 0, jj * tc, cr, d)

        lax.fori_loop(0, n_lat, body, carry)
    o_ref[0] = (y_ref[...] * _gelu_tanh(g_ref[0])).astype(BF16)


def _block_diag(w):
    nd, nb, c, _ = w.shape
    out = jnp.zeros((nd, nb * c, nb * c), w.dtype)
    for n in range(nb):
        out = out.at[:, n * c:(n + 1) * c, n * c:(n + 1) * c].set(w[:, n])
    return out


def _lru_mixer(lu, lg, lp, seq):
    b, nt, w = lu.shape
    ctx_len = nt - seq
    row3 = lambda a: a.reshape(2, 1, w)
    args = (_pad_to(lp['lru_conv_w'], 8, 1), row3(lp['lru_conv_b']), _block_diag(lp['lru_wa']).astype(BF16),
            row3(lp['lru_ba']), _block_diag(lp['lru_wx']).astype(BF16), row3(lp['lru_bx']), row3(lp['lru_lambda']))
    full = lambda a: pl.BlockSpec(a.shape, lambda bb: (0,) * a.ndim)
    tok = pl.BlockSpec((1, nt, w), lambda bb: (bb, 0, 0))
    return pl.pallas_call(
        functools.partial(_lru_kernel, seq=seq, ctx_len=ctx_len),
        grid=(b,),
        in_specs=[tok, tok] + [full(a) for a in args],
        out_specs=tok,
        out_shape=jax.ShapeDtypeStruct((b, nt, w), BF16),
        scratch_shapes=[pltpu.VMEM((nt + 3 * LRU_HALO, w), F32), pltpu.VMEM((nt, w), F32)],
        compiler_params=_cparams("parallel"),
        name="rglru_scan",
    )(lu, lg, *args)


def _merge_kernel(x_ref, mod_ref, g1_ref, a_ref, b_ref, c_ref, d_ref, wg_ref, wa_ref, wb_ref, wc_ref, wd_ref,
                  wo_ref, o_ref):
    group, tm, dm = x_ref.shape
    m = mod_ref[:, 0]
    x = x_ref[...]
    h = _normmod(x, g1_ref[...], m[:, 1:2], m[:, 0:1]).reshape(group * tm, dm).astype(BF16)
    acc = None
    for k, (br, w) in enumerate(((a_ref, wa_ref), (b_ref, wb_ref), (c_ref, wc_ref), (d_ref, wd_ref))):
        gate = _sigmoid(_dot(h, wg_ref[:, k * dm:(k + 1) * dm]))
        term = gate * _dot(br[...].reshape(group * tm, br.shape[-1]), w[...])
        acc = term if acc is None else acc + term
    y = _dot(acc.astype(BF16), wo_ref[...])
    o_ref[...] = x + m[:, 2:3] * y.reshape(group, tm, dm)


def _merge(xa, modtab, g1, branches, w_gate, lp, n_lat_tiles, tiles_used):
    b, _, d = xa.shape
    wbr = lp['w_branch']
    head_rows = lambda w, dv: jnp.concatenate(
        [_pad_to(w[hd * dv:(hd + 1) * dv], LANE, 0) for hd in range(4)], axis=0)
    weights = (head_rows(wbr[0], MLA_V).astype(BF16), head_rows(wbr[1], NA_HEAD_DIM).astype(BF16),
               wbr[2].astype(BF16), wbr[3].astype(BF16), lp['w_out'].astype(BF16))
    group = _batch_group(b)
    full = lambda a: pl.BlockSpec(a.shape, lambda bb, i: (0,) * a.ndim)
    tok = lambda n: pl.BlockSpec((group, TM, n), lambda bb, i: (bb, i, 0))
    return pl.pallas_call(
        _merge_kernel,
        grid=(b // group, tiles_used),
        in_specs=[tok(d), pl.BlockSpec((group, 1, 8, d), _kind_map(n_lat_tiles)), full(g1)]
                 + [tok(br.shape[-1]) for br in branches] + [full(w_gate)] + [full(w) for w in weights],
        out_specs=tok(d),
        out_shape=jax.ShapeDtypeStruct((b, tiles_used * TM, d), F32),
        compiler_params=_cparams("parallel", "parallel"),
        name="merge_branches",
    )(xa, modtab, g1, *branches, w_gate, *weights)


U32 = jnp.uint32
EXPERT_ROWS = 512


def _pack_pair(x):
    n = x.shape[-1] // 2
    hi = lax.bitcast_convert_type(x[:, :n].astype(BF16).astype(F32), U32)
    lo = lax.bitcast_convert_type(x[:, n:].astype(BF16).astype(F32), U32)
    return hi | (lo >> 16)


def _unpack_pair(p):
    hi = lax.bitcast_convert_type(p & jnp.uint32(0xFFFF0000), F32)
    lo = lax.bitcast_convert_type(p << 16, F32)
    return hi, lo


def _router_kernel(x_ref, mod_ref, g2_ref, rw_ref, rb_ref, tri_ref, h2_o, idx_o, wts_o, rank_o, cnt_o, carry):
    i = pl.program_id(0)

    @pl.when(i == 0)
    def _():
        carry[...] = jnp.zeros_like(carry)

    m = mod_ref[0, 0]
    h2 = _normmod(x_ref[0], g2_ref[...], m[4:5], m[3:4])
    half = h2.shape[-1] // 2
    h2_o[0] = _pack_pair(h2[:, :half])
    h2_o[1] = _pack_pair(h2[:, half:])
    logits = lax.dot_general(rw_ref[...], h2, (((1,), (1,)), ((), ())), preferred_element_type=F32,
                             precision=lax.Precision.HIGHEST)
    scores = _sigmoid(logits)
    biased = scores + rb_ref[...]
    expert = lax.broadcasted_iota(I32, scores.shape, 0)
    picks = []
    onehot_all = jnp.zeros(scores.shape, F32)
    for _ in range(TOP_K):
        best = jnp.max(biased, axis=0, keepdims=True)
        arg = jnp.min(jnp.where(biased == best, expert, N_EXPERTS), axis=0, keepdims=True)
        hit = expert == arg
        sel = jnp.sum(jnp.where(hit, scores, 0.0), axis=0, keepdims=True)
        biased = jnp.where(hit, -jnp.inf, biased)
        onehot_all = onehot_all + jnp.where(hit, 1.0, 0.0)
        picks.append((arg, hit, sel))
    total = picks[0][2]
    for _, _, sel in picks[1:]:
        total = total + sel
    earlier = _dot(onehot_all.astype(BF16), tri_ref[...]) + carry[...]
    pad_rows = TOPK_PAD - TOP_K
    ranks = [jnp.sum(jnp.where(hit, earlier, 0.0), axis=0, keepdims=True).astype(I32) for _, hit, _ in picks]
    scale = ROUTED_SCALE / total
    idx_o[...] = jnp.concatenate([arg for arg, _, _ in picks] + [jnp.zeros((pad_rows, TM), I32)], axis=0)
    wts_o[...] = jnp.concatenate([sel * scale for _, _, sel in picks] + [jnp.zeros((pad_rows, TM), F32)], axis=0)
    rank_o[...] = jnp.concatenate(ranks + [jnp.zeros((pad_rows, TM), I32)], axis=0)
    carry[...] = carry[...] + jnp.sum(onehot_all, axis=1, keepdims=True)
    cnt_o[...] = carry[...]


def _tile_maps(tiles_used, n_lat_tiles):
    tok = lambda i: (i // tiles_used, i % tiles_used, 0)
    mod = lambda i: (i // tiles_used, jnp.where(i % tiles_used >= n_lat_tiles, 1, 0), 0, 0)
    return tok, mod


def _route(x1, modtab, g2, lp, n_lat_tiles, tiles_used):
    b, _, d = x1.shape
    n_tiles = b * tiles_used
    t = n_tiles * TM
    rw = lp['router_w'].T
    rb = lp['router_bias'].reshape(-1, 1)
    tri = (np.arange(TM)[:, None] < np.arange(TM)[None, :]).astype(np.float32)
    tri = jnp.asarray(tri, BF16)
    per_tok = lambda: pl.BlockSpec((TOPK_PAD, TM), lambda i: (0, i))
    full = lambda a: pl.BlockSpec(a.shape, lambda i: (0,) * a.ndim)
    tok, mod = _tile_maps(tiles_used, n_lat_tiles)
    return pl.pallas_call(
        _router_kernel,
        grid=(n_tiles,),
        in_specs=[pl.BlockSpec((1, TM, d), tok), pl.BlockSpec((1, 1, 8, d), mod),
                  full(g2), full(rw), full(rb), full(tri)],
        out_specs=[pl.BlockSpec((2, TM, d // 4), lambda i: (0, i, 0)), per_tok(), per_tok(), per_tok(),
                   pl.BlockSpec((N_EXPERTS, 1), lambda i: (0, 0))],
        out_shape=[jax.ShapeDtypeStruct((2, t, d // 4), U32), jax.ShapeDtypeStruct((TOPK_PAD, t), I32),
                   jax.ShapeDtypeStruct((TOPK_PAD, t), F32), jax.ShapeDtypeStruct((TOPK_PAD, t), I32),
                   jax.ShapeDtypeStruct((N_EXPERTS, 1), F32)],
        scratch_shapes=[pltpu.VMEM((N_EXPERTS, 1), F32)],
        compiler_params=_cparams("arbitrary"),
        name="moe_router",
    )(x1, modtab, g2, rw, rb, tri)


SC_WINDOW = 128


def _sc_mesh():
    return plsc.VectorSubcoreMesh(core_axis_name="c", subcore_axis_name="s")


def _sc_scatter_rows(src, idx, n_out):
    n, width = src.shape
    k_rep = idx.shape[0]
    half = n // SC_WINDOW // 2

    @functools.partial(pl.kernel, out_type=jax.ShapeDtypeStruct((n_out, width), src.dtype), mesh=_sc_mesh(),
                       scratch_types=[], name="moe_dispatch_sc")
    def scatter(src_hbm, idx_hbm, out_hbm):
        def body(x_vmem, *i_vmems):
            for i_vmem in i_vmems:
                pltpu.sync_copy(x_vmem, out_hbm.at[i_vmem.at[0]])

        pltpu.emit_pipeline(
            body,
            grid=(2, half),
            in_specs=[pl.BlockSpec((SC_WINDOW, width), lambda a, i: (a * half + i, 0))]
                     + [pl.BlockSpec((1, SC_WINDOW), lambda a, i, k=k: (k, a * half + i)) for k in range(k_rep)],
            out_specs=[],
            core_axis_name=("c", "s"),
            dimension_semantics=(pltpu.PARALLEL, pltpu.PARALLEL),
        )(src_hbm, *([idx_hbm] * k_rep))

    return scatter(src, idx)


def _sc_gather_rows(src, idx):
    k_rep, n = idx.shape
    width = src.shape[1]
    n_win = n // SC_WINDOW

    @functools.partial(pl.kernel, out_type=jax.ShapeDtypeStruct((k_rep * n, width), src.dtype), mesh=_sc_mesh(),
                       scratch_types=[], name="moe_gather_sc")
    def gather(src_hbm, idx_hbm, out_hbm):
        def body(i_vmem, o_vmem):
            pltpu.sync_copy(src_hbm.at[i_vmem.at[0]], o_vmem)

        pltpu.emit_pipeline(
            body,
            grid=(k_rep, n_win),
            in_specs=[pl.BlockSpec((1, SC_WINDOW), lambda k, i: (k, i))],
            out_specs=[pl.BlockSpec((SC_WINDOW, width), lambda k, i: (k * n_win + i, 0))],
            core_axis_name=("c", "s"),
            dimension_semantics=(pltpu.PARALLEL, pltpu.PARALLEL),
        )(idx_hbm, out_hbm)

    return gather(src, idx)


def _unpack_planes(p0, p1):
    return _unpack_pair(p0) + _unpack_pair(p1)


def _dot_quarters(parts, w_ref):
    q = parts[0].shape[-1]
    acc = None
    for j, part in enumerate(parts):
        term = _dot(part.astype(BF16), w_ref[j * q:(j + 1) * q, :])
        acc = term if acc is None else acc + term
    return acc


def _expert_kernel(be_ref, nv_ref, xs_ref, wg_ref, wu_ref, wd_ref, ys_o, wg_s, wu_s, wd_s):
    i = pl.program_id(0)
    prev = be_ref[jnp.maximum(i - 1, 0)]

    @pl.when((i == 0) | (be_ref[i] != prev))
    def _():
        wg_s[...] = wg_ref[0].astype(BF16)
        wu_s[...] = wu_ref[0].astype(BF16)
        wd_s[...] = wd_ref[0].astype(BF16)

    @pl.when(nv_ref[i] > 0)
    def _():
        keep = lax.broadcasted_iota(I32, xs_ref.shape[1:], 0) < nv_ref[i]
        parts = _unpack_planes(jnp.where(keep, xs_ref[0], jnp.uint32(0)), jnp.where(keep, xs_ref[1], jnp.uint32(0)))
        hid = _silu(_dot_quarters(parts, wg_s)) * _dot_quarters(parts, wu_s)
        y = _dot(hid.astype(BF16), wd_s[...])
        half = y.shape[-1] // 2
        ys_o[0] = _pack_pair(y[:, :half])
        ys_o[1] = _pack_pair(y[:, half:])

    @pl.when(nv_ref[i] <= 0)
    def _():
        ys_o[...] = jnp.zeros_like(ys_o)


def _experts(xs, block_e, n_valid, weights, layer):
    _, n_rows, dq = xs.shape
    d = 4 * dq
    n_blocks = n_rows // EXPERT_ROWS
    hid = EXPERT_HIDDEN
    grid_spec = pltpu.PrefetchScalarGridSpec(
        num_scalar_prefetch=2,
        grid=(n_blocks,),
        in_specs=[pl.BlockSpec((2, EXPERT_ROWS, dq), lambda i, be, nv: (0, i, 0)),
                  pl.BlockSpec((None, 1, d, hid), lambda i, be, nv: (layer, be[i], 0, 0)),
                  pl.BlockSpec((None, 1, d, hid), lambda i, be, nv: (layer, be[i], 0, 0)),
                  pl.BlockSpec((None, 1, hid, d), lambda i, be, nv: (layer, be[i], 0, 0))],
        out_specs=pl.BlockSpec((2, EXPERT_ROWS, dq), lambda i, be, nv: (0, i, 0)),
        scratch_shapes=[pltpu.VMEM((d, hid), BF16), pltpu.VMEM((d, hid), BF16), pltpu.VMEM((hid, d), BF16)],
    )
    return pl.pallas_call(
        _expert_kernel,
        grid_spec=grid_spec,
        out_shape=jax.ShapeDtypeStruct((2, n_rows, dq), U32),
        compiler_params=_cparams("arbitrary"),
        name="moe_experts",
    )(block_e, n_valid, xs, *weights)


def _combine_kernel(g_ref, wts_ref, h2_ref, x_ref, mod_ref, sg_ref, su_ref, sd_ref, gf_ref, o_ref, *, final):
    parts = _unpack_planes(h2_ref[0], h2_ref[1])
    hid = _silu(_dot_quarters(parts, sg_ref)) * _dot_quarters(parts, su_ref)
    shared = _dot(hid.astype(BF16), sd_ref[...])
    wts = wts_ref[...]
    routed = None
    for k in range(TOP_K):
        w = wts[:, k:k + 1]
        terms = [w * part for part in _unpack_planes(g_ref[k, 0], g_ref[k, 1])]
        routed = terms if routed is None else [r + t for r, t in zip(routed, terms)]
    f = shared + jnp.concatenate(routed, axis=1)
    m = mod_ref[0, 0]
    x2 = x_ref[0] + m[5:6] * f
    if final:
        x2 = x2 * lax.rsqrt(jnp.mean(x2 * x2, axis=-1, keepdims=True) + NORM_EPS) * gf_ref[...]
    o_ref[0] = x2


def _combine(g, wts, h2p, x1, modtab, lp, g_final, n_lat_tiles, tiles_used, final):
    b, _, d = x1.shape
    dq = d // 4
    weights = (lp['sh_w_gate'].astype(BF16), lp['sh_w_up'].astype(BF16), lp['sh_w_down'].astype(BF16),
               g_final.reshape(1, -1))
    full = lambda a: pl.BlockSpec(a.shape, lambda i: (0,) * a.ndim)
    tok, mod = _tile_maps(tiles_used, n_lat_tiles)
    return pl.pallas_call(
        functools.partial(_combine_kernel, final=final),
        grid=(b * tiles_used,),
        in_specs=[pl.BlockSpec((TOP_K, 2, TM, dq), lambda i: (0, 0, i, 0)),
                  pl.BlockSpec((TM, TOPK_PAD), lambda i: (i, 0)),
                  pl.BlockSpec((2, TM, dq), lambda i: (0, i, 0)),
                  pl.BlockSpec((1, TM, d), tok), pl.BlockSpec((1, 1, 8, d), mod)]
                 + [full(w) for w in weights],
        out_specs=pl.BlockSpec((1, TM, d), tok),
        out_shape=jax.ShapeDtypeStruct((b, tiles_used * TM, d), F32),
        compiler_params=_cparams("parallel"),
        name="moe_combine",
    )(g, wts, h2p, x1, modtab, *weights)


def _moe(x1, modtab, g2, lp, g_final, n_lat_tiles, final):
    b, nt, d = x1.shape
    tiles_used = n_lat_tiles if final else nt // TM
    t = b * tiles_used * TM
    h2p, idx, wts, rank, cnt = _route(x1, modtab, g2, lp, n_lat_tiles, tiles_used)
    counts = cnt[:, 0].astype(I32)
    padded = (counts + EXPERT_ROWS - 1) // EXPERT_ROWS * EXPERT_ROWS
    p_ends = jnp.cumsum(padded)
    p_starts = p_ends - padded
    n_blocks = (t * TOP_K + N_EXPERTS * (EXPERT_ROWS - 1)) // EXPERT_ROWS
    n_rows = n_blocks * EXPERT_ROWS
    chosen = idx[:TOP_K, None, :] == jnp.arange(N_EXPERTS, dtype=I32)[None, :, None]
    dest = jnp.sum(jnp.where(chosen, p_starts[None, :, None], 0), axis=1) + rank[:TOP_K]
    plane_idx = (dest[:, None, :] + (jnp.arange(2, dtype=I32) * n_rows)[None, :, None]).reshape(TOP_K, 2 * t)
    blk_start = jnp.arange(n_blocks, dtype=I32) * EXPERT_ROWS
    block_e = jnp.minimum(jnp.sum((p_ends[None, :] <= blk_start[:, None]).astype(I32), axis=1), N_EXPERTS - 1)
    n_valid = jnp.clip((p_starts + counts)[block_e] - blk_start, 0, EXPERT_ROWS).astype(I32)
    dq = d // 4
    xs = _sc_scatter_rows(h2p.reshape(2 * t, dq), plane_idx, 2 * n_rows).reshape(2, n_rows, dq)
    ys = _experts(xs, block_e, n_valid, lp['expert_stacks'], lp['layer'])
    g = _sc_gather_rows(ys.reshape(2 * n_rows, dq), plane_idx).reshape(TOP_K, 2, t, dq)
    return _combine(g, wts.T, h2p, x1, modtab, lp, g_final, n_lat_tiles, tiles_used, final)


def _layer(xa, c, c_ctx, lp, consts, g_final, seq, final):
    b, nt, d = xa.shape
    n_lat_tiles = seq // TM
    rope, mats_lat, mats_ctx = consts
    modtab = _mod_table(c, c_ctx, *lp['mod_stacks'], lp['layer'])
    g1 = lp['g_norm1'].reshape(1, -1)
    g2 = lp['g_norm2'].reshape(1, -1)
    pw = _proj_weights(lp)
    q, k, v, nq, nk, nv, hy, lu, lg = _project(xa, modtab, g1, pw, rope, n_lat_tiles)
    tiles_used = n_lat_tiles if final else nt // TM
    br_a = _mla_attention(q, k, v, seq, tiles_used)
    br_b = _na_attention(nq, nk, nv, _na_bias_tables(lp['na_rpb'], seq // GRID_W), seq, tiles_used)
    pre = _hyp_pre(hy, lp['hy_short_w'], lp['hy_short_b'], seq)
    lat_f, lat_bf, ctx_f, ctx_bf = pre
    br_c = _hyena_seq(mats_lat, lat_f, lat_bf, lp, b)
    if not final:
        br_c = jnp.concatenate([br_c, _hyena_seq(mats_ctx, ctx_f, ctx_bf, lp, b)], axis=1)
    br_d = _lru_mixer(lu, lg, lp, seq)
    x1 = _merge(xa, modtab, g1, (br_a, br_b, br_c, br_d), pw['w_gate'], lp, n_lat_tiles, tiles_used)
    return _moe(x1, modtab, g2, lp, g_final, n_lat_tiles, final)


_LAYER_KEYS = ('w_mod', 'b_mod', 'g_norm1', 'g_norm2', 'w_in', 'mla_g_q', 'mla_w_uq', 'mla_g_kv', 'mla_w_ukv',
               'na_rpb', 'hy_short_w', 'hy_short_b', 'hy_w1', 'hy_b1', 'hy_w2', 'hy_b2', 'hy_w3', 'hy_decay',
               'hy_bias', 'lru_conv_w', 'lru_conv_b', 'lru_wa', 'lru_ba', 'lru_wx', 'lru_bx', 'lru_lambda',
               'w_branch', 'w_out', 'router_w', 'router_bias', 'exp_w_gate', 'exp_w_up', 'exp_w_down',
               'sh_w_gate', 'sh_w_up', 'sh_w_down')


def kernel(x, c, ctx, c_ctx, w_mod, b_mod, g_norm1, g_norm2, w_in, mla_g_q, mla_w_uq, mla_g_kv, mla_w_ukv, na_rpb, hy_short_w, hy_short_b, hy_w1, hy_b1, hy_w2, hy_b2, hy_w3, hy_decay, hy_bias, lru_conv_w, lru_conv_b, lru_wa, lru_ba, lru_wx, lru_bx, lru_lambda, w_branch, w_out, router_w, router_bias, exp_w_gate, exp_w_up, exp_w_down, sh_w_gate, sh_w_up, sh_w_down, g_final):
    stacked = dict(zip(_LAYER_KEYS, (w_mod, b_mod, g_norm1, g_norm2, w_in, mla_g_q, mla_w_uq, mla_g_kv, mla_w_ukv,
                                     na_rpb, hy_short_w, hy_short_b, hy_w1, hy_b1, hy_w2, hy_b2, hy_w3, hy_decay,
                                     hy_bias, lru_conv_w, lru_conv_b, lru_wa, lru_ba, lru_wx, lru_bx, lru_lambda,
                                     w_branch, w_out, router_w, router_bias, exp_w_gate, exp_w_up, exp_w_down,
                                     sh_w_gate, sh_w_up, sh_w_down)))
    b, seq, d = x.shape
    ctx_len = ctx.shape[1]
    depth = w_mod.shape[0]
    assert seq % TM == 0 and ctx_len % TM == 0 and seq // GRID_W >= NA_KEY_ROWS + 1
    xa = jnp.concatenate([x, ctx], axis=1)
    consts = (_rope_tables(seq, seq + ctx_len), _hyp_matrices(seq), _hyp_matrices(ctx_len))
    for i in range(depth):
        big = ('w_mod', 'b_mod', 'exp_w_gate', 'exp_w_up', 'exp_w_down')
        lp = {name: w[i] for name, w in stacked.items() if name not in big}
        lp['layer'] = i
        lp['mod_stacks'] = (w_mod, b_mod)
        lp['expert_stacks'] = (exp_w_gate, exp_w_up, exp_w_down)
        xa = _layer(xa, c, c_ctx, lp, consts, g_final, seq, i == depth - 1)
    return xa
```

```python
import functools
import math

import numpy as np
import jax
import jax.numpy as jnp
from jax import lax
from jax.experimental import pallas as pl
from jax.experimental.pallas import tpu as pltpu
from jax.experimental.pallas import tpu_sc as plsc

F32 = jnp.float32
BF16 = jnp.bfloat16
I32 = jnp.int32

TM = 256
LANE = 128
GRID_W = 64
N_MOD = 6
NORM_EPS = 1e-6

MLA_HEADS, MLA_NOPE, MLA_ROPE, MLA_V = 4, 64, 32, 64
MLA_Q_RANK, MLA_KV_RANK = 192, 128
MLA_Q_PAD = 256
ROPE_THETA = 10000.0

NA_HEADS, NA_HEAD_DIM, NA_WIN_R, NA_WIN_C = 4, 64, 8, 16
NA_TILE_ROWS = TM // GRID_W
NA_KEY_ROWS = NA_TILE_ROWS + NA_WIN_R - 1
NA_KEYS = NA_KEY_ROWS * GRID_W

HY_WIDTH, HY_ORDER, HY_SHORT, HY_BANDS, HY_FFN = 256, 2, 3, 16, 64
HY_EMB = 2 * HY_BANDS + 1

LRU_WIDTH, LRU_BLOCKS, LRU_CONV, LRU_C = 256, 4, 4, 8.0
LRU_CHUNK = 256
LRU_HALO = 8

N_EXPERTS, TOP_K, EXPERT_HIDDEN, ROUTED_SCALE, MOE_BLOCK = 64, 6, 256, 2.5, 256
TOPK_PAD = 8

VMEM_LIMIT = 52 * 1024 * 1024


def _cparams(*sem):
    return pltpu.CompilerParams(dimension_semantics=sem, vmem_limit_bytes=VMEM_LIMIT)


def _dot(a, b):
    return jnp.dot(a, b, preferred_element_type=F32)


def _dot_nt(a, b):
    return lax.dot_general(a, b, (((1,), (1,)), ((), ())), preferred_element_type=F32)


def _sigmoid(x):
    return jax.nn.sigmoid(x)


def _half_silu(g):
    return g * (1.0 + jnp.tanh(g))


def _silu(x):
    return x * _sigmoid(x)


def _gelu_tanh(x):
    return 0.5 * x * (1.0 + jnp.tanh(math.sqrt(2.0 / math.pi) * (x + 0.044715 * (x * x * x))))


def _normmod(x, g, scale, shift):
    y = x * lax.rsqrt(jnp.mean(x * x, axis=-1, keepdims=True) + NORM_EPS) * g
    return y * (1.0 + scale) + shift


def _mod_kernel(c_ref, w_ref, b_ref, o_ref):
    s = _silu(c_ref[...])
    o_ref[...] = _dot(s.astype(BF16), w_ref[...].astype(BF16)) + b_ref[...]


def _mod_table(c, c_ctx, w_mod, b_mod, layer):
    b, d = c.shape
    rows = 16
    cc = jnp.zeros((rows, d), F32).at[:b].set(c).at[b].set(c_ctx)
    tn = 1024
    mod = pl.pallas_call(
        _mod_kernel,
        grid=(N_MOD * d // tn,),
        in_specs=[pl.BlockSpec((rows, d), lambda j: (0, 0)),
                  pl.BlockSpec((None, d, tn), lambda j: (layer, 0, j)),
                  pl.BlockSpec((None, 1, tn), lambda j: (layer, 0, j))],
        out_specs=pl.BlockSpec((rows, tn), lambda j: (0, j)),
        out_shape=jax.ShapeDtypeStruct((rows, N_MOD * d), F32),
        compiler_params=_cparams("arbitrary"),
        name="mod_vectors",
    )(cc, w_mod, b_mod[:, None, :])
    lat = mod[:b].reshape(b, N_MOD, d)
    ctx = jnp.broadcast_to(mod[b].reshape(1, N_MOD, d), (b, N_MOD, d))
    tab = jnp.stack([lat, ctx], axis=1)
    return jnp.pad(tab, ((0, 0), (0, 0), (0, 8 - N_MOD), (0, 0)))


_C_QLAT, _C_KVLAT, _C_KR, _C_KRR, _C_NA, _C_HY, _C_LU, _C_LG, _C_END = (
    0, 256, 384, 512, 640, 640 + 3 * NA_HEADS * LANE, 640 + 1536 + 768, 640 + 1536 + 1024, 640 + 1536 + 1280)


def _proj_kernel(x_ref, mod_ref, g1_ref, w1_ref, gq_ref, gkv_ref, wuq_ref, wuqr_ref, wk_ref, wv_ref,
                 cos_ref, sin_ref, q_o, k_o, v_o, nq_o, nk_o, nv_o, hy_o, lu_o, lg_o):
    group, tm, d = x_ref.shape
    m = mod_ref[:, 0]
    h = _normmod(x_ref[...], g1_ref[...], m[:, 1:2], m[:, 0:1]).reshape(group * tm, d)
    z = _dot(h.astype(BF16), w1_ref[...])
    qlat = z[:, _C_QLAT:_C_KVLAT]
    kvlat = z[:, _C_KVLAT:_C_KR]
    kr = z[:, _C_KR:_C_KRR]
    krr = z[:, _C_KRR:_C_NA]
    qn = qlat * lax.rsqrt(jnp.sum(qlat * qlat, axis=-1, keepdims=True) * (1.0 / MLA_Q_RANK) + NORM_EPS) * gq_ref[...]
    kvn = kvlat * lax.rsqrt(jnp.mean(kvlat * kvlat, axis=-1, keepdims=True) + NORM_EPS) * gkv_ref[...]
    qn = qn.astype(BF16)
    kvn = kvn.astype(BF16)
    q = _dot(qn, wuq_ref[...])
    qr = _dot(qn, wuqr_ref[...])
    kk = _dot(kvn, wk_ref[...])
    vv = _dot(kvn, wv_ref[...])
    cos = jnp.concatenate([cos_ref[...]] * group, axis=0)
    sin = jnp.concatenate([sin_ref[...]] * group, axis=0)
    krope = kr * cos + krr * sin
    den_lane = lax.broadcasted_iota(I32, cos.shape, 1) == SOFTMAX_DEN_LANE
    def put_heads(ref, hd, val):
        for g in range(group):
            ref[g, hd] = val[g * tm:(g + 1) * tm].astype(BF16)

    def put_pairs(ref, hd, val):
        off = (hd % 2) * LANE
        for g in range(group):
            ref[g, hd // 2, :, off:off + LANE] = val[g * tm:(g + 1) * tm].astype(BF16)

    for hd in range(MLA_HEADS):
        sl = slice(hd * LANE, (hd + 1) * LANE)
        put_heads(q_o, hd, q[:, sl] * cos + qr[:, sl] * sin)
        put_heads(k_o, hd, kk[:, sl] + krope)
        put_heads(v_o, hd, jnp.where(den_lane, 1.0, vv[:, sl]))
    for hd in range(NA_HEADS):
        for which, ref in enumerate((nq_o, nk_o, nv_o)):
            lo = _C_NA + (which * NA_HEADS + hd) * LANE
            blk = z[:, lo:lo + LANE]
            if which == 2:
                put_pairs(ref, hd, jnp.where(den_lane, 1.0, blk))
            else:
                put_heads(ref, hd, blk)
    hy_o[...] = z[:, _C_HY:_C_LU].reshape(group, tm, _C_LU - _C_HY)
    lu_o[...] = z[:, _C_LU:_C_LG].reshape(group, tm, _C_LG - _C_LU)
    lg_o[...] = z[:, _C_LG:_C_END].reshape(group, tm, _C_END - _C_LG)


def _pad_to(a, n, axis):
    pad = [(0, 0)] * a.ndim
    pad[axis] = (0, n - a.shape[axis])
    return jnp.pad(a, pad)


def _rot_cols(w):
    half = w.shape[-1] // 2
    return jnp.concatenate([-w[..., half:], w[..., :half]], axis=-1)


def _head_blocks(cols_per_head):
    out = []
    for pieces in cols_per_head:
        k = pieces[0][0].shape[0]
        blk = jnp.zeros((k, LANE), F32)
        for arr, off in pieces:
            blk = blk.at[:, off:off + arr.shape[1]].set(arr)
        out.append(blk)
    return jnp.concatenate(out, axis=1)


def _proj_weights(lp):
    w_in = lp['w_in']
    d = w_in.shape[0]
    o = 0
    parts = {}
    for name, n in (('q', MLA_Q_RANK), ('kv', MLA_KV_RANK), ('kr', MLA_ROPE), ('na', 3 * NA_HEADS * NA_HEAD_DIM),
                    ('hy', 3 * HY_WIDTH), ('lu', LRU_WIDTH), ('lg', LRU_WIDTH), ('gt', 4 * d)):
        parts[name] = w_in[:, o:o + n]
        o += n
    zeros = lambda n: jnp.zeros((d, n), F32)
    kr_blk = jnp.concatenate([zeros(MLA_NOPE), parts['kr'], zeros(LANE - MLA_NOPE - MLA_ROPE)], axis=1)
    krr_blk = jnp.concatenate([zeros(MLA_NOPE), _rot_cols(parts['kr']), zeros(LANE - MLA_NOPE - MLA_ROPE)], axis=1)
    na_scale = NA_HEAD_DIM ** -0.5
    na_cols = []
    for which in range(3):
        for hd in range(NA_HEADS):
            lo = (which * NA_HEADS + hd) * NA_HEAD_DIM
            blk = parts['na'][:, lo:lo + NA_HEAD_DIM] * (na_scale if which == 0 else 1.0)
            na_cols.append(_pad_to(blk, LANE, 1))
    w1 = jnp.concatenate([_pad_to(parts['q'], MLA_Q_PAD, 1), parts['kv'], kr_blk, krr_blk] + na_cols
                         + [parts['hy'], parts['lu'], parts['lg']], axis=1).astype(BF16)
    mla_scale = (MLA_NOPE + MLA_ROPE) ** -0.5
    wuq = _pad_to(lp['mla_w_uq'], MLA_Q_PAD, 0) * mla_scale
    dq = MLA_NOPE + MLA_ROPE
    wuq_main = _head_blocks([[(wuq[:, hd * dq:hd * dq + dq], 0)] for hd in range(MLA_HEADS)])
    wuq_rot = _head_blocks([[(_rot_cols(wuq[:, hd * dq + MLA_NOPE:hd * dq + dq]), MLA_NOPE)]
                            for hd in range(MLA_HEADS)])
    dkv = MLA_NOPE + MLA_V
    wukv = lp['mla_w_ukv']
    wk = _head_blocks([[(wukv[:, hd * dkv:hd * dkv + MLA_NOPE], 0)] for hd in range(MLA_HEADS)])
    wv = _head_blocks([[(wukv[:, hd * dkv + MLA_NOPE:hd * dkv + dkv], 0)] for hd in range(MLA_HEADS)])
    gq = _pad_to(lp['mla_g_q'].reshape(1, -1), MLA_Q_PAD, 1)
    gkv = lp['mla_g_kv'].reshape(1, -1)
    return dict(w1=w1, w_gate=(0.5 * parts['gt']).astype(BF16), gq=gq, gkv=gkv, wuq=wuq_main.astype(BF16),
                wuq_rot=wuq_rot.astype(BF16), wk=wk.astype(BF16), wv=wv.astype(BF16))


def _rope_tables(seq, n_tok):
    t = jnp.arange(seq, dtype=I32)
    row = (t // GRID_W).astype(F32)
    col = (t % GRID_W).astype(F32)
    n_axis = MLA_ROPE // 4
    inv_freq = ROPE_THETA ** (-jnp.arange(n_axis, dtype=F32) / n_axis)
    ang = jnp.concatenate([row[:, None] * inv_freq, col[:, None] * inv_freq], axis=-1)
    cos = jnp.concatenate([jnp.cos(ang), jnp.cos(ang)], axis=-1)
    sin = jnp.concatenate([jnp.sin(ang), jnp.sin(ang)], axis=-1)
    cos_t = jnp.ones((n_tok, LANE), F32).at[:seq, MLA_NOPE:MLA_NOPE + MLA_ROPE].set(cos)
    sin_t = jnp.zeros((n_tok, LANE), F32).at[:seq, MLA_NOPE:MLA_NOPE + MLA_ROPE].set(sin)
    return cos_t, sin_t


def _batch_group(b):
    return 2 if b % 2 == 0 else 1


def _kind_map(n_lat_tiles):
    return lambda b, i: (b, jnp.where(i >= n_lat_tiles, 1, 0), 0, 0)


def _project(xa, modtab, g1, pw, rope, n_lat_tiles):
    b, nt, d = xa.shape
    cos_t, sin_t = rope
    group = _batch_group(b)
    full = lambda a: pl.BlockSpec(a.shape, lambda bb, i: (0,) * a.ndim)
    head_out = lambda: pl.BlockSpec((group, MLA_HEADS, TM, LANE), lambda bb, i: (bb, 0, i, 0))
    tok_out = lambda n: pl.BlockSpec((group, TM, n), lambda bb, i: (bb, i, 0))
    head_shape = jax.ShapeDtypeStruct((b, MLA_HEADS, nt, LANE), BF16)
    pair_out = lambda: pl.BlockSpec((group, MLA_HEADS // 2, TM, 2 * LANE), lambda bb, i: (bb, 0, i, 0))
    pair_shape = jax.ShapeDtypeStruct((b, MLA_HEADS // 2, nt, 2 * LANE), BF16)
    tok_shape = lambda n: jax.ShapeDtypeStruct((b, nt, n), F32)
    weights = (g1, pw['w1'], pw['gq'], pw['gkv'], pw['wuq'], pw['wuq_rot'], pw['wk'], pw['wv'])
    return pl.pallas_call(
        _proj_kernel,
        grid=(b // group, nt // TM),
        in_specs=[pl.BlockSpec((group, TM, d), lambda bb, i: (bb, i, 0)),
                  pl.BlockSpec((group, 1, 8, d), _kind_map(n_lat_tiles))]
                 + [full(w) for w in weights]
                 + [pl.BlockSpec((TM, LANE), lambda bb, i: (i, 0))] * 2,
        out_specs=[head_out(), head_out(), head_out(), head_out(), head_out(), pair_out(),
                   tok_out(3 * HY_WIDTH), tok_out(LRU_WIDTH), tok_out(LRU_WIDTH)],
        out_shape=[head_shape, head_shape, head_shape, head_shape, head_shape, pair_shape,
                   tok_shape(3 * HY_WIDTH), tok_shape(LRU_WIDTH), tok_shape(LRU_WIDTH)],
        compiler_params=_cparams("parallel", "parallel"),
        name="input_projection",
    )(xa, modtab, *weights, cos_t, sin_t)


SOFTMAX_DEN_LANE = 64


def _softmax_pv(parts, lane_off=0):
    m = None
    for s, _ in parts:
        mm = jnp.max(s, axis=-1, keepdims=True)
        m = mm if m is None else jnp.maximum(m, mm)
    acc = None
    for s, v in parts:
        o = _dot(jnp.exp(s - m).astype(BF16), v)
        acc = o if acc is None else acc + o
    den = lane_off + SOFTMAX_DEN_LANE
    return acc[:, lane_off:lane_off + LANE] / acc[:, den:den + 1]


def _mla_kernel(q_ref, k_ref, v_ref, o_ref, *, seq, n_lat_tiles):
    i = pl.program_id(1)
    nt = k_ref.shape[2]

    def attend(lo, hi):
        for hd in range(MLA_HEADS):
            s = _dot_nt(q_ref[0, hd], k_ref[0, hd, lo:hi, :])
            o = _softmax_pv([(s, v_ref[0, hd, lo:hi, :])])
            o_ref[0, :, hd * LANE:(hd + 1) * LANE] = o.astype(BF16)

    @pl.when(i < n_lat_tiles)
    def _():
        attend(0, nt)

    @pl.when(i >= n_lat_tiles)
    def _():
        attend(seq, nt)


def _mla_attention(q, k, v, seq, tiles_used):
    b, h, nt, _ = q.shape
    kv_spec = pl.BlockSpec((1, h, nt, LANE), lambda bb, i: (bb, 0, 0, 0))
    return pl.pallas_call(
        functools.partial(_mla_kernel, seq=seq, n_lat_tiles=seq // TM),
        grid=(b, tiles_used),
        in_specs=[pl.BlockSpec((1, h, TM, LANE), lambda bb, i: (bb, 0, i, 0)), kv_spec, kv_spec],
        out_specs=pl.BlockSpec((1, TM, h * LANE), lambda bb, i: (bb, i, 0)),
        out_shape=jax.ShapeDtypeStruct((b, tiles_used * TM, h * LANE), BF16),
        compiler_params=_cparams("parallel", "arbitrary"),
        name="mla_attention",
    )(q, k, v)


def _na_bias_tables(rpb, rows):
    n_blk = rows // NA_TILE_ROWS
    col = np.arange(GRID_W)
    c0 = np.clip(col - NA_WIN_C // 2, 0, GRID_W - NA_WIN_C)
    in_win = (col[None, :] >= c0[:, None]) & (col[None, :] < c0[:, None] + NA_WIN_C)
    dc = np.clip(col[None, :] - col[:, None], 1 - NA_WIN_C, NA_WIN_C - 1) + NA_WIN_C - 1
    rpb = rpb.astype(F32)
    tables = []
    for j in (0, 1, n_blk - 1):
        w0 = min(max(NA_TILE_ROWS * j - NA_WIN_R // 2, 0), rows - NA_KEY_ROWS)
        r = NA_TILE_ROWS * j + np.arange(NA_TILE_ROWS)
        kr = w0 + np.arange(NA_KEY_ROWS)
        r0 = np.clip(r - NA_WIN_R // 2, 0, rows - NA_WIN_R)
        row_ok = (kr[None, :] >= r0[:, None]) & (kr[None, :] < r0[:, None] + NA_WIN_R)
        dr = np.clip(kr[None, :] - r[:, None] + NA_WIN_R - 1, 0, 2 * NA_WIN_R - 2)
        oh_r = jnp.asarray(np.eye(2 * NA_WIN_R - 1, dtype=np.float32)[dr.reshape(-1)])
        oh_c = jnp.asarray(np.eye(2 * NA_WIN_C - 1, dtype=np.float32)[dc.reshape(-1)])
        bias = jnp.einsum('ar,hrc,bc->hab', oh_r, rpb, oh_c, precision=lax.Precision.HIGHEST)
        bias = bias.reshape(NA_HEADS, NA_TILE_ROWS, NA_KEY_ROWS, GRID_W, GRID_W)
        mask = row_ok[:, :, None, None] & in_win[None, None, :, :]
        bias = jnp.where(jnp.asarray(mask)[None], bias, -jnp.inf)
        tables.append(bias.transpose(0, 1, 3, 2, 4).reshape(NA_HEADS, TM, NA_KEYS))
    return jnp.stack(tables)


def _na_kernel(q_ref, k_ref, v_ref, bias_ref, o_ref, *, seq, n_lat_tiles):
    i = pl.program_id(1)
    nt = k_ref.shape[2]
    rows = seq // GRID_W

    @pl.when(i < n_lat_tiles)
    def _():
        w0 = jnp.clip(NA_TILE_ROWS * i - NA_WIN_R // 2, 0, rows - NA_KEY_ROWS)
        start = pl.multiple_of(w0 * GRID_W, GRID_W)
        for hd in range(NA_HEADS):
            q = q_ref[0, hd]
            s_loc = _dot_nt(q, k_ref[0, hd, pl.ds(start, NA_KEYS), :]) + bias_ref[0, hd]
            s_ctx = _dot_nt(q, k_ref[0, hd, seq:nt, :])
            o = _softmax_pv([(s_loc, v_ref[0, hd // 2, pl.ds(start, NA_KEYS), :]),
                             (s_ctx, v_ref[0, hd // 2, seq:nt, :])], (hd % 2) * LANE)
            o_ref[0, :, hd * LANE:(hd + 1) * LANE] = o.astype(BF16)

    @pl.when(i >= n_lat_tiles)
    def _():
        for hd in range(NA_HEADS):
            s = _dot_nt(q_ref[0, hd], k_ref[0, hd, seq:nt, :])
            o = _softmax_pv([(s, v_ref[0, hd // 2, seq:nt, :])], (hd % 2) * LANE)
            o_ref[0, :, hd * LANE:(hd + 1) * LANE] = o.astype(BF16)


def _na_attention(q, k, v, bias, seq, tiles_used):
    b, h, nt, _ = q.shape
    n_lat = seq // TM
    kv_spec = pl.BlockSpec((1, h, nt, LANE), lambda bb, i: (bb, 0, 0, 0))
    cfg = lambda bb, i: (jnp.where(i == 0, 0, jnp.where(i >= n_lat - 1, 2, 1)), 0, 0, 0)
    return pl.pallas_call(
        functools.partial(_na_kernel, seq=seq, n_lat_tiles=n_lat),
        grid=(b, tiles_used),
        in_specs=[pl.BlockSpec((1, h, TM, LANE), lambda bb, i: (bb, 0, i, 0)), kv_spec,
                  pl.BlockSpec((1, h // 2, nt, 2 * LANE), lambda bb, i: (bb, 0, 0, 0)),
                  pl.BlockSpec((1, h, TM, NA_KEYS), cfg)],
        out_specs=pl.BlockSpec((1, TM, h * LANE), lambda bb, i: (bb, i, 0)),
        out_shape=jax.ShapeDtypeStruct((b, tiles_used * TM, h * LANE), BF16),
        compiler_params=_cparams("parallel", "arbitrary"),
        name="neighbourhood_attention",
    )(q, k, v, bias)


HY_PRE_CHUNK = 256


def _hy_pre_kernel(z_ref, w_ref, b_ref, lat_o, lat_bf_o, ctx_o, ctx_bf_o, *, seq, ctx_len):
    w = w_ref[...]
    bias = b_ref[...]
    ch = HY_PRE_CHUNK
    zero = jnp.zeros((8, HY_WIDTH), F32)
    for seg_lo, seg_len, o_ref, obf_ref in ((0, seq, lat_o, lat_bf_o), (seq, ctx_len, ctx_o, ctx_bf_o)):
        for c in range(seg_len // ch):
            s = seg_lo + c * ch
            before = zero if c == 0 else z_ref[0, s - 8:s, :]
            after = zero if c == seg_len // ch - 1 else z_ref[0, s + ch:s + ch + 8, :]
            win = jnp.concatenate([before, z_ref[0, s:s + ch, :], after], axis=0)
            n = ch + 16
            y = bias + w[1:2] * win[8:8 + ch]
            y = y + w[0:1] * pltpu.roll(win, 1, axis=0)[8:8 + ch]
            y = y + w[2:3] * pltpu.roll(win, n - 1, axis=0)[8:8 + ch]
            o_ref[0, c * ch:(c + 1) * ch, :] = y
            obf_ref[0, c * ch:(c + 1) * ch, :] = y.astype(BF16)


def _hy_pre(hy, w, bvec, seq):
    b, nt, _ = hy.shape
    ctx_len = nt - seq
    wpad = _pad_to(w, 8, 0)
    out_specs, out_shape = [], []
    for length in (seq, ctx_len):
        for dt in (F32, BF16):
            out_specs.append(pl.BlockSpec((1, length, HY_WIDTH), lambda bb, g: (g, 0, bb)))
            out_shape.append(jax.ShapeDtypeStruct((3, length, b * HY_WIDTH), dt))
    return pl.pallas_call(
        functools.partial(_hy_pre_kernel, seq=seq, ctx_len=ctx_len),
        grid=(b, 3),
        in_specs=[pl.BlockSpec((1, nt, HY_WIDTH), lambda bb, g: (bb, 0, g)),
                  pl.BlockSpec((8, HY_WIDTH), lambda bb, g: (0, g)),
                  pl.BlockSpec((1, HY_WIDTH), lambda bb, g: (0, g))],
        out_specs=out_specs,
        out_shape=out_shape,
        compiler_params=_cparams("parallel", "parallel"),
        name="hyena_short_conv",
    )(hy, wpad, bvec.reshape(1, -1))


def _hp_dot(a, b):
    return jnp.dot(a, b, preferred_element_type=F32, precision=lax.Precision.HIGHEST)


def _hy_filter_kernel(f_ref, w1_ref, b1_ref, w2_ref, b2_ref, w3_ref, dec_ref, o_ref, obf_ref):
    f = f_ref[...]
    h = jnp.sin(_hp_dot(f, w1_ref[...]) + b1_ref[...])
    h = jnp.sin(_hp_dot(h, w2_ref[...]) + b2_ref[...])
    h = _hp_dot(h, w3_ref[...])
    h = h * jnp.exp(-f[:, 0:1] * jnp.abs(dec_ref[...]))
    o_ref[...] = h
    obf_ref[...] = h.astype(BF16)


def _hy_pos_features(length):
    t = jnp.linspace(0.0, 1.0, length, dtype=F32)[:, None]
    w = 2.0 * math.pi * jnp.arange(length, dtype=F32)[:, None] / length
    f = jnp.linspace(1e-4, HY_BANDS - 1, HY_BANDS, dtype=F32)[None, :]
    z = w * f
    return jnp.concatenate([t, jnp.cos(z), -jnp.sin(z)], axis=-1)


def _hy_filters(length, lp):
    feats = _hy_pos_features(length)
    n_out = HY_ORDER * 2 * HY_WIDTH
    tl = min(length, 512)
    full = lambda a: pl.BlockSpec(a.shape, lambda i: (0,) * a.ndim)
    args = (lp['hy_w1'], lp['hy_b1'].reshape(1, -1), lp['hy_w2'], lp['hy_b2'].reshape(1, -1), lp['hy_w3'],
            lp['hy_decay'].reshape(1, n_out))
    return pl.pallas_call(
        _hy_filter_kernel,
        grid=(length // tl,),
        in_specs=[pl.BlockSpec((tl, HY_EMB), lambda i: (i, 0))] + [full(a) for a in args],
        out_specs=[pl.BlockSpec((tl, n_out), lambda i: (i, 0))] * 2,
        out_shape=[jax.ShapeDtypeStruct((length, n_out), F32), jax.ShapeDtypeStruct((length, n_out), BF16)],
        compiler_params=_cparams("parallel"),
        name="hyena_filter_mlp",
    )(feats, *args)


def _dft_matrices(length):
    n2 = 2 * length
    step = DFT_ROW_STEP
    n = jnp.arange(length, dtype=I32)

    def trig(kv):
        ang = ((kv[:, None] * n[None, :]) % n2).astype(F32) * (2.0 * math.pi / n2)
        return jnp.cos(ang), jnp.sin(ang)

    c1, s1 = trig(jnp.arange(length // step, dtype=I32) * step)
    c2, s2 = trig(jnp.arange(step, dtype=I32))
    coarse = pl.BlockSpec((1, 1, length), lambda j: (j, 0, 0))
    fine = pl.BlockSpec((step, length), lambda j: (0, 0))
    out = pl.BlockSpec((step, length), lambda j: (j, 0))
    return pl.pallas_call(
        _dft_matrix_kernel,
        grid=(length // step,),
        in_specs=[coarse, coarse, fine, fine],
        out_specs=[out, out, out],
        out_shape=[jax.ShapeDtypeStruct((length, length), BF16)] * 3,
        compiler_params=_cparams("parallel"),
        name="dft_matrices",
    )(c1[:, None, :], s1[:, None, :], c2, s2)


DFT_ROW_STEP = 64


def _dft_matrix_kernel(c1_ref, s1_ref, c2_ref, s2_ref, mc_o, ms_o, mst_o):
    c1, s1 = c1_ref[0], s1_ref[0]
    c2, s2 = c2_ref[...], s2_ref[...]
    cos = c1 * c2 - s1 * s2
    neg_sin = -(s1 * c2 + c1 * s2)
    row = lax.broadcasted_iota(I32, cos.shape, 0)
    col = lax.broadcasted_iota(I32, cos.shape, 1)
    first_row = (row + pl.program_id(0) * DFT_ROW_STEP) == 0
    mc_o[...] = cos.astype(BF16)
    ms_o[...] = jnp.where(first_row, jnp.where(col % 2 == 0, 1.0, -1.0), neg_sin).astype(BF16)
    mst_o[...] = jnp.where(col == 0, jnp.where(row % 2 == 0, 1.0, -1.0), neg_sin).astype(BF16)


def _dft_fwd_kernel(mc_ref, ms_ref, x_ref, *rest, with_taps):
    ure = _dot(mc_ref[...], x_ref[...])
    uim = _dot(ms_ref[...], x_ref[...])
    if not with_taps:
        ure_o, uim_o = rest
        ure_o[...] = ure
        uim_o[...] = uim
        return
    a_ref, b_ref, c_ref, d_ref, zre_o, zim_o = rest
    a, b, c, d = a_ref[...], b_ref[...], c_ref[...], d_ref[...]
    for bb in range(x_ref.shape[1] // HY_WIDTH):
        sl = slice(bb * HY_WIDTH, (bb + 1) * HY_WIDTH)
        zre_o[:, sl] = (ure[:, sl] * a - uim[:, sl] * b).astype(BF16)
        zim_o[:, sl] = (ure[:, sl] * c + uim[:, sl] * d).astype(BF16)


def _col_block(nc, cap):
    return min(nc, cap)


def _dft_fwd(mats, x, plane, taps=None):
    mc, ms, _ = mats
    _, length, nc = x.shape
    tk = min(length, 256)
    cb = _col_block(nc, 1024)
    grid = (nc // cb, length // tk)
    m_spec = pl.BlockSpec((tk, length), lambda c, j: (j, 0))
    x_spec = pl.BlockSpec((None, length, cb), lambda c, j: (plane, 0, c))
    o_spec = pl.BlockSpec((tk, cb), lambda c, j: (j, c))
    if taps is None:
        return pl.pallas_call(
            functools.partial(_dft_fwd_kernel, with_taps=False),
            grid=grid, in_specs=[m_spec, m_spec, x_spec], out_specs=[o_spec, o_spec],
            out_shape=[jax.ShapeDtypeStruct((length, nc), F32)] * 2,
            compiler_params=_cparams("parallel", "arbitrary"),
            name="hyena_dft_filters",
        )(mc, ms, x)
    t_spec = pl.BlockSpec((tk, HY_WIDTH), lambda c, j: (j, 0))
    return pl.pallas_call(
        functools.partial(_dft_fwd_kernel, with_taps=True),
        grid=grid, in_specs=[m_spec, m_spec, x_spec] + [t_spec] * 4, out_specs=[o_spec, o_spec],
        out_shape=[jax.ShapeDtypeStruct((length, nc), BF16)] * 2,
        compiler_params=_cparams("parallel", "arbitrary"),
        name="hyena_dft_forward",
    )(mc, ms, x, *taps)


def _dft_inv_kernel(mc_ref, mst_ref, zre_ref, zim_ref, gate_ref, prev_ref, bias_ref, *outs, last):
    conv = _dot(mc_ref[...], zre_ref[...]) + _dot(mst_ref[...], zim_ref[...])
    y = gate_ref[...] * (conv + prev_ref[...] * bias_ref[...])
    if last:
        (tok_o,) = outs
        for bb in range(y.shape[1] // HY_WIDTH):
            tok_o[bb] = y[:, bb * HY_WIDTH:(bb + 1) * HY_WIDTH].astype(BF16)
    else:
        y_o, ybf_o = outs
        y_o[...] = y
        ybf_o[...] = y.astype(BF16)


def _dft_inv(mats, zre, zim, gate, gate_plane, prev, prev_plane, bias_row, last):
    mc, _, mst = mats
    length, nc = zre.shape
    tm = min(length, 256)
    cb = _col_block(nc, 512)
    grid = (nc // cb, length // tm)
    m_spec = pl.BlockSpec((tm, length), lambda c, i: (i, 0))
    z_spec = pl.BlockSpec((length, cb), lambda c, i: (0, c))
    e_spec = lambda plane: pl.BlockSpec((None, tm, cb), lambda c, i: (plane, i, c))
    b_spec = pl.BlockSpec((1, cb), lambda c, i: (0, c))
    if last:
        nb = nc // HY_WIDTH
        out_specs = [pl.BlockSpec((cb // HY_WIDTH, tm, HY_WIDTH), lambda c, i: (c, i, 0))]
        out_shape = [jax.ShapeDtypeStruct((nb, length, HY_WIDTH), BF16)]
    else:
        out_specs = [e_spec(0), e_spec(0)]
        out_shape = [jax.ShapeDtypeStruct((1, length, nc), F32), jax.ShapeDtypeStruct((1, length, nc), BF16)]
    return pl.pallas_call(
        functools.partial(_dft_inv_kernel, last=last),
        grid=grid, in_specs=[m_spec, m_spec, z_spec, z_spec, e_spec(gate_plane), e_spec(prev_plane), b_spec],
        out_specs=out_specs, out_shape=out_shape,
        compiler_params=_cparams("parallel", "arbitrary"),
        name="hyena_dft_inverse",
    )(mc, mst, zre, zim, gate, prev, bias_row)


def _hy_tap_tables(ure, uim, filt, length):
    n2 = 2 * length
    w = HY_WIDTH
    scale = jnp.full((length, 1), 2.0 / n2, F32).at[0, 0].set(1.0 / n2)
    tables = []
    for o in range(HY_ORDER):
        f_sl = slice((2 * o) * w, (2 * o + 1) * w)
        b_sl = slice((2 * o + 1) * w, (2 * o + 2) * w)
        hb0 = filt[0:1, b_sl]
        tre = ure[:, f_sl] + ure[:, b_sl] - hb0
        tim = uim[:, f_sl] - uim[:, b_sl]
        t_nyq = uim[0:1, f_sl] + uim[0:1, b_sl] - hb0
        first = (jnp.arange(length) == 0)[:, None]
        a = tre * scale
        bm = jnp.where(first, 0.0, tim * scale)
        c = bm
        d = jnp.where(first, t_nyq * scale, tre * scale)
        tables.append((a, bm, c, d))
    return tables


def _hyena_seq(mats, vx, vx_bf, lp, n_batch):
    length = vx.shape[1]
    filt, filt_bf = _hy_filters(length, lp)
    ure, uim = _dft_fwd(mats, filt_bf[None], 0)
    tables = _hy_tap_tables(ure, uim, filt, length)
    bias = lp['hy_bias'].astype(F32)
    y, y_bf = vx, vx_bf
    for o in range(HY_ORDER):
        zre, zim = _dft_fwd(mats, y_bf, 0, tables[o])
        bias_row = jnp.tile(bias[o][None, :], (1, n_batch))
        last = o == HY_ORDER - 1
        res = _dft_inv(mats, zre, zim, vx, o + 1, y, 0, bias_row, last)
        if last:
            return res[0]
        y, y_bf = res


def _hyp_pre_kernel(z_ref, w_ref, b_ref, lat_o, lat_bf_o, ctx_o, ctx_bf_o, *, seq, ctx_len):
    w = w_ref[...]
    bias = b_ref[...]
    for lo, length, o_ref, obf_ref in ((0, seq, lat_o, lat_bf_o), (seq, ctx_len, ctx_o, ctx_bf_o)):
        h = length // 2
        even = z_ref[0, pl.ds(lo, h, stride=2), :]
        odd = z_ref[0, pl.ds(lo + 1, h, stride=2), :]
        row = lax.broadcasted_iota(I32, even.shape, 0)
        odd_prev = jnp.where(row == 0, 0.0, pltpu.roll(odd, 1, axis=0))
        even_next = jnp.where(row == h - 1, 0.0, pltpu.roll(even, h - 1, axis=0))
        y_even = bias + w[0:1] * odd_prev + w[1:2] * even + w[2:3] * odd
        y_odd = bias + w[0:1] * even + w[1:2] * odd + w[2:3] * even_next
        for r, y in enumerate((y_even, y_odd)):
            o_ref[0, r] = y
            obf_ref[0, r] = y.astype(BF16)


def _hyp_pre(hy, w, bvec, seq):
    b, nt, _ = hy.shape
    ctx_len = nt - seq
    wpad = _pad_to(w, 8, 0)
    per_plane = HY_WIDTH // LANE
    out_specs, out_shape = [], []
    for length in (seq, ctx_len):
        for dt in (F32, BF16):
            out_specs.append(pl.BlockSpec((1, 2, length // 2, LANE),
                                          lambda bb, g: (g // per_plane, 0, 0, bb * per_plane + g % per_plane)))
            out_shape.append(jax.ShapeDtypeStruct((3, 2, length // 2, b * HY_WIDTH), dt))
    return pl.pallas_call(
        functools.partial(_hyp_pre_kernel, seq=seq, ctx_len=ctx_len),
        grid=(b, 3 * per_plane),
        in_specs=[pl.BlockSpec((1, nt, LANE), lambda bb, g: (bb, 0, g)),
                  pl.BlockSpec((8, LANE), lambda bb, g: (0, g)),
                  pl.BlockSpec((1, LANE), lambda bb, g: (0, g))],
        out_specs=out_specs,
        out_shape=out_shape,
        compiler_params=_cparams("parallel", "parallel"),
        name="hyena_short_conv",
    )(hy, wpad, bvec.reshape(1, -1))


def _hyp_filters(length, lp):
    feats = _hy_pos_features(length)
    feats = feats.reshape(length // 2, 2, HY_EMB).transpose(1, 0, 2).reshape(length, HY_EMB)
    n_out = HY_ORDER * 2 * HY_WIDTH
    tl = min(length // 2, 512)
    full = lambda a: pl.BlockSpec(a.shape, lambda i: (0,) * a.ndim)
    args = (lp['hy_w1'], lp['hy_b1'].reshape(1, -1), lp['hy_w2'], lp['hy_b2'].reshape(1, -1), lp['hy_w3'],
            lp['hy_decay'].reshape(1, n_out))
    filt, filt_bf = pl.pallas_call(
        _hy_filter_kernel,
        grid=(length // tl,),
        in_specs=[pl.BlockSpec((tl, HY_EMB), lambda i: (i, 0))] + [full(a) for a in args],
        out_specs=[pl.BlockSpec((tl, n_out), lambda i: (i, 0))] * 2,
        out_shape=[jax.ShapeDtypeStruct((length, n_out), F32), jax.ShapeDtypeStruct((length, n_out), BF16)],
        compiler_params=_cparams("parallel"),
        name="hyena_filter_mlp",
    )(feats, *args)
    return filt.reshape(2, length // 2, n_out), filt_bf.reshape(2, length // 2, n_out)


def _hyp_matrix_kernel(c1_ref, s1_ref, c2_ref, s2_ref, ck_ref, sk_ref, ckr_ref, skr_ref,
                       ce_o, se_o, co_o, so_o, set_o, cot_o, sot_o):
    c1, s1 = c1_ref[0], s1_ref[0]
    c2, s2 = c2_ref[...], s2_ref[...]
    cos = c1 * c2 - s1 * s2
    sin = s1 * c2 + c1 * s2
    row = lax.broadcasted_iota(I32, cos.shape, 0)
    col = lax.broadcasted_iota(I32, cos.shape, 1)
    first_row = (row + pl.program_id(0) * DFT_ROW_STEP) == 0
    alt_col = jnp.where(col % 2 == 0, 1.0, -1.0)
    alt_row = jnp.where(row % 2 == 0, 1.0, -1.0)
    ck, sk = ck_ref[...], sk_ref[...]
    ce_o[...] = cos.astype(BF16)
    se_o[...] = jnp.where(first_row, alt_col, -sin).astype(BF16)
    co_o[...] = (cos * ck - sin * sk).astype(BF16)
    so_o[...] = jnp.where(first_row, alt_col, -(sin * ck + cos * sk)).astype(BF16)
    ckr, skr = ckr_ref[...], skr_ref[...]
    set_o[...] = jnp.where(col == 0, alt_row, -sin).astype(BF16)
    cot_o[...] = (cos * ckr - sin * skr).astype(BF16)
    sot_o[...] = jnp.where(col == 0, alt_row, -(sin * ckr + cos * skr)).astype(BF16)


def _hyp_matrices(length):
    h = length // 2
    step = DFT_ROW_STEP
    m = jnp.arange(h, dtype=I32)

    def trig(kv):
        ang = ((kv[:, None] * m[None, :]) % length).astype(F32) * (2.0 * math.pi / length)
        return jnp.cos(ang), jnp.sin(ang)

    c1, s1 = trig(jnp.arange(h // step, dtype=I32) * step)
    c2, s2 = trig(jnp.arange(step, dtype=I32))
    half_angle = m.astype(F32) * (math.pi / length)
    ck, sk = jnp.cos(half_angle), jnp.sin(half_angle)
    coarse = pl.BlockSpec((1, 1, h), lambda j: (j, 0, 0))
    fine = pl.BlockSpec((step, h), lambda j: (0, 0))
    per_row = pl.BlockSpec((step, 1), lambda j: (j, 0))
    per_col = pl.BlockSpec((1, h), lambda j: (0, 0))
    out = pl.BlockSpec((step, h), lambda j: (j, 0))
    ce, se, co, so, se_t, co_t, so_t = pl.pallas_call(
        _hyp_matrix_kernel,
        grid=(h // step,),
        in_specs=[coarse, coarse, fine, fine, per_row, per_row, per_col, per_col],
        out_specs=[out] * 7,
        out_shape=[jax.ShapeDtypeStruct((h, h), BF16)] * 7,
        compiler_params=_cparams("parallel"),
        name="dft_matrices",
    )(c1[:, None, :], s1[:, None, :], c2, s2, ck[:, None], sk[:, None], ck[None, :], sk[None, :])
    return dict(fwd=(ce, se, co, so), inv=(ce, se_t, co_t, so_t))


def _hyp_fwd_kernel(ce_ref, se_ref, co_ref, so_ref, xe_ref, xo_ref, *rest, with_taps):
    xe, xo = xe_ref[...], xo_ref[...]
    a_re, a_im = _dot(ce_ref[...], xe), _dot(se_ref[...], xe)
    b_re, b_im = _dot(co_ref[...], xo), _dot(so_ref[...], xo)
    if not with_taps:
        for ref, val in zip(rest, (a_re, a_im, b_re, b_im)):
            ref[...] = val
        return
    t1re_ref, t1im_ref, t2re_ref, t2im_ref, sp_ref, gere_o, geim_o, gore_o, goim_o = rest
    t1re, t1im, t2re, t2im = t1re_ref[...], t1im_ref[...], t2re_ref[...], t2im_ref[...]
    sp = sp_ref[...]
    first_block = pl.program_id(1) == 0
    row8 = lax.broadcasted_iota(I32, (8, HY_WIDTH), 0)
    for bb in range(xe.shape[1] // HY_WIDTH):
        sl = slice(bb * HY_WIDTH, (bb + 1) * HY_WIDTH)
        are, aim, bre, bim = a_re[:, sl], a_im[:, sl], b_re[:, sl], b_im[:, sl]
        u1re, u1im = are + bre, aim + bim
        u2re, u2im = are - bre, bim - aim
        z1re, z1im = u1re * t1re - u1im * t1im, u1re * t1im + u1im * t1re
        z2re, z2im = u2re * t2re - u2im * t2im, u2re * t2im + u2im * t2re
        gere_o[:, sl] = (z1re + z2re).astype(BF16)
        geim_o[:, sl] = (z1im - z2im).astype(BF16)
        gore_o[:, sl] = (z1re - z2re).astype(BF16)
        goim_o[:, sl] = (z1im + z2im).astype(BF16)

        @pl.when(first_block)
        def _():
            u0, ul = u1re[0:8], u2re[0:8]
            a_s, b_s = aim[0:8], bim[0:8]
            dc, ny, mre, mim = sp[0:1], sp[1:2], sp[2:3], sp[3:4]
            first = row8 == 0
            gere_o[0:8, sl] = jnp.where(first, u0 * dc + ul * ny, (z1re + z2re)[0:8]).astype(BF16)
            gore_o[0:8, sl] = jnp.where(first, u0 * dc - ul * ny, (z1re - z2re)[0:8]).astype(BF16)
            geim_o[0:8, sl] = jnp.where(first, a_s * mre + b_s * mim, (z1im - z2im)[0:8]).astype(BF16)
            goim_o[0:8, sl] = jnp.where(first, b_s * mre - a_s * mim, (z1im + z2im)[0:8]).astype(BF16)


def _hyp_fwd(mats, x, plane, taps=None):
    _, _, h, nc = x.shape
    tk = min(h, 256)
    cb = _col_block(nc, 1024)
    grid = (nc // cb, h // tk)
    m_spec = pl.BlockSpec((tk, h), lambda c, j: (j, 0))
    x_spec = lambda r: pl.BlockSpec((None, None, h, cb), lambda c, j: (plane, r, 0, c))
    o_spec = pl.BlockSpec((tk, cb), lambda c, j: (j, c))
    if taps is None:
        return pl.pallas_call(
            functools.partial(_hyp_fwd_kernel, with_taps=False),
            grid=grid, in_specs=[m_spec] * 4 + [x_spec(0), x_spec(1)], out_specs=[o_spec] * 4,
            out_shape=[jax.ShapeDtypeStruct((h, nc), F32)] * 4,
            compiler_params=_cparams("parallel", "arbitrary"),
            name="hyena_dft_filters",
        )(*mats['fwd'], x, x)
    t_spec = pl.BlockSpec((tk, HY_WIDTH), lambda c, j: (j, 0))
    sp_spec = pl.BlockSpec((8, HY_WIDTH), lambda c, j: (0, 0))
    return pl.pallas_call(
        functools.partial(_hyp_fwd_kernel, with_taps=True),
        grid=grid, in_specs=[m_spec] * 4 + [x_spec(0), x_spec(1)] + [t_spec] * 4 + [sp_spec],
        out_specs=[o_spec] * 4,
        out_shape=[jax.ShapeDtypeStruct((h, nc), BF16)] * 4,
        compiler_params=_cparams("parallel", "arbitrary"),
        name="hyena_dft_forward",
    )(*mats['fwd'], x, x, *taps)


def _hyp_inv_kernel(ce_ref, set_ref, cot_ref, sot_ref, gere_ref, geim_ref, gore_ref, goim_ref,
                    gate_ref, prev_ref, bias_ref, *outs, last):
    conv = (_dot(ce_ref[...], gere_ref[...]) + _dot(set_ref[...], geim_ref[...]),
            _dot(cot_ref[...], gore_ref[...]) + _dot(sot_ref[...], goim_ref[...]))
    bias = bias_ref[...]
    for r in range(2):
        y = gate_ref[r] * (conv[r] + prev_ref[r] * bias)
        if last:
            (tok_o,) = outs
            for bb in range(y.shape[1] // HY_WIDTH):
                tok_o[bb, :, r * HY_WIDTH:(r + 1) * HY_WIDTH] = y[:, bb * HY_WIDTH:(bb + 1) * HY_WIDTH].astype(BF16)
        else:
            y_o, ybf_o = outs
            y_o[r] = y
            ybf_o[r] = y.astype(BF16)


def _hyp_inv(mats, g, gate, gate_plane, prev, prev_plane, bias_row, last):
    h, nc = g[0].shape
    tm = min(h, 256)
    cb = _col_block(nc, 512)
    grid = (nc // cb, h // tm)
    m_spec = pl.BlockSpec((tm, h), lambda c, i: (i, 0))
    g_spec = pl.BlockSpec((h, cb), lambda c, i: (0, c))
    e_spec = lambda plane: pl.BlockSpec((None, 2, tm, cb), lambda c, i: (plane, 0, i, c))
    b_spec = pl.BlockSpec((1, cb), lambda c, i: (0, c))
    if last:
        out_specs = [pl.BlockSpec((cb // HY_WIDTH, tm, 2 * HY_WIDTH), lambda c, i: (c, i, 0))]
        out_shape = [jax.ShapeDtypeStruct((nc // HY_WIDTH, h, 2 * HY_WIDTH), BF16)]
    else:
        out_specs = [e_spec(0), e_spec(0)]
        out_shape = [jax.ShapeDtypeStruct((1, 2, h, nc), F32), jax.ShapeDtypeStruct((1, 2, h, nc), BF16)]
    return pl.pallas_call(
        functools.partial(_hyp_inv_kernel, last=last),
        grid=grid, in_specs=[m_spec] * 4 + [g_spec] * 4 + [e_spec(gate_plane), e_spec(prev_plane), b_spec],
        out_specs=out_specs, out_shape=out_shape,
        compiler_params=_cparams("parallel", "arbitrary"),
        name="hyena_dft_inverse",
    )(*mats['inv'], *g, gate, prev, bias_row)


def _hyp_tap_tables(spec, filt, length):
    a_re, a_im, b_re, b_im = spec
    w = HY_WIDTH
    inv_n = 1.0 / (2 * length)
    tables = []
    for o in range(HY_ORDER):
        f_sl = slice((2 * o) * w, (2 * o + 1) * w)
        r_sl = slice((2 * o + 1) * w, (2 * o + 2) * w)
        hb0 = filt[0, 0:1, r_sl]
        f1re = (a_re + b_re)[:, f_sl] + (a_re + b_re)[:, r_sl] - hb0
        f1im = (a_im + b_im)[:, f_sl] - (a_im + b_im)[:, r_sl]
        f2re = (a_re - b_re)[:, f_sl] + (a_re - b_re)[:, r_sl] - hb0
        f2im = (b_im - a_im)[:, f_sl] - (b_im - a_im)[:, r_sl]
        dc = f1re[0:1]
        ny = f2re[0:1]
        mid_re = a_im[0:1, f_sl] + a_im[0:1, r_sl] - hb0
        mid_im = -b_im[0:1, f_sl] + b_im[0:1, r_sl]
        sp = jnp.concatenate([dc * inv_n, ny * inv_n, mid_re * (2 * inv_n), mid_im * (2 * inv_n),
                              jnp.zeros((4, w), F32)], axis=0)
        tables.append((f1re * (2 * inv_n), f1im * (2 * inv_n), f2re * (2 * inv_n), f2im * (2 * inv_n), sp))
    return tables


def _hyena_seq(mats, vx, vx_bf, lp, n_batch):
    h = vx.shape[2]
    length = 2 * h
    filt, filt_bf = _hyp_filters(length, lp)
    spec = _hyp_fwd(mats, filt_bf[None], 0)
    tables = _hyp_tap_tables(spec, filt, length)
    bias = lp['hy_bias'].astype(F32)
    y, y_bf = vx, vx_bf
    for o in range(HY_ORDER):
        g = _hyp_fwd(mats, y_bf, 0, tables[o])
        bias_row = jnp.tile(bias[o][None, :], (1, n_batch))
        last = o == HY_ORDER - 1
        res = _hyp_inv(mats, g, vx, o + 1, y, 0, bias_row, last)
        if last:
            return res[0].reshape(n_batch, length, HY_WIDTH)
        y, y_bf = res


def _lru_kernel(u_ref, g_ref, cw_ref, cb_ref, wa_ref, ba_ref, wx_ref, bx_ref, lam_ref, o_ref,
                pad_ref, y_ref, *, seq, ctx_len):
    tc = LRU_CHUNK
    halo = LRU_HALO
    width = LRU_WIDTH
    lat_off = halo
    ctx_off = 2 * halo + seq
    zero = jnp.zeros((halo, width), F32)
    pad_ref[0:halo, :] = zero
    pad_ref[lat_off:lat_off + seq, :] = u_ref[0, 0:seq, :]
    pad_ref[lat_off + seq:ctx_off, :] = zero
    pad_ref[ctx_off:ctx_off + ctx_len, :] = u_ref[0, seq:seq + ctx_len, :]
    pad_ref[ctx_off + ctx_len:ctx_off + ctx_len + halo, :] = zero
    row = lax.broadcasted_iota(I32, (tc, width), 0)
    n_win = tc + 2 * halo

    def chunk(pad_off, y_off, s, carry, d):
        wstart = pl.multiple_of(pad_off + s - halo, 8)
        win = pad_ref[pl.ds(wstart, n_win), :]
        cw = cw_ref[d]
        xc = cb_ref[d]
        for k in range(LRU_CONV):
            shift = (LRU_CONV - 1 - k) if d == 0 else -k
            rolled = win if shift == 0 else pltpu.roll(win, shift % n_win, axis=0)
            xc = xc + cw[k:k + 1] * rolled[halo:halo + tc]
        xb = xc.astype(BF16)
        r = _sigmoid(_dot(xb, wa_ref[d]) + ba_ref[d])
        gi = _sigmoid(_dot(xb, wx_ref[d]) + bx_ref[d])
        lam = lam_ref[d]
        softplus = jnp.maximum(-lam, 0.0) + jnp.log1p(jnp.exp(-jnp.abs(lam)))
        log_a = -LRU_C * r * softplus
        a = jnp.exp(log_a)
        bt = jnp.sqrt(-jnp.tanh(log_a) * (a * a + 1.0)) * (gi * xc)
        sft = 1
        while sft < tc:
            if d == 0:
                keep = row >= sft
                a_s = jnp.where(keep, pltpu.roll(a, sft, axis=0), 1.0)
                b_s = jnp.where(keep, pltpu.roll(bt, sft, axis=0), 0.0)
            else:
                keep = row < tc - sft
                a_s = jnp.where(keep, pltpu.roll(a, tc - sft, axis=0), 1.0)
                b_s = jnp.where(keep, pltpu.roll(bt, tc - sft, axis=0), 0.0)
            bt = a * b_s + bt
            a = a * a_s
            sft *= 2
        h = a * carry + bt
        yo = pl.multiple_of(y_off + s, 8)
        if d == 0:
            y_ref[pl.ds(yo, tc), :] = h
            return h[tc - 1:tc]
        y_ref[pl.ds(yo, tc), :] = y_ref[pl.ds(yo, tc), :] + h
        return h[0:1]

    n_lat = seq // tc
    n_ctx = ctx_len // tc
    for d in range(2):
        carry = jnp.zeros((1, width), F32)
        order = range(n_ctx) if d == 0 else range(n_ctx - 1, -1, -1)
        for c in order:
            carry = chunk(ctx_off, seq, c * tc, carry, d)

        def body(j, cr, d=d):
            jj = j if d == 0 else n_lat - 1 - j
            return chunk(lat_off, 0, jj * tc, cr, d)

        lax.fori_loop(0, n_lat, body, carry)
    o_ref[0] = (y_ref[...] * _gelu_tanh(g_ref[0])).astype(BF16)


def _block_diag(w):
    nd, nb, c, _ = w.shape
    out = jnp.zeros((nd, nb * c, nb * c), w.dtype)
    for n in range(nb):
        out = out.at[:, n * c:(n + 1) * c, n * c:(n + 1) * c].set(w[:, n])
    return out


def _lru_mixer(lu, lg, lp, seq):
    b, nt, w = lu.shape
    ctx_len = nt - seq
    row3 = lambda a: a.reshape(2, 1, w)
    args = (_pad_to(lp['lru_conv_w'], 8, 1), row3(lp['lru_conv_b']), _block_diag(lp['lru_wa']).astype(BF16),
            row3(lp['lru_ba']), _block_diag(lp['lru_wx']).astype(BF16), row3(lp['lru_bx']), row3(lp['lru_lambda']))
    full = lambda a: pl.BlockSpec(a.shape, lambda bb: (0,) * a.ndim)
    tok = pl.BlockSpec((1, nt, w), lambda bb: (bb, 0, 0))
    return pl.pallas_call(
        functools.partial(_lru_kernel, seq=seq, ctx_len=ctx_len),
        grid=(b,),
        in_specs=[tok, tok] + [full(a) for a in args],
        out_specs=tok,
        out_shape=jax.ShapeDtypeStruct((b, nt, w), BF16),
        scratch_shapes=[pltpu.VMEM((nt + 3 * LRU_HALO, w), F32), pltpu.VMEM((nt, w), F32)],
        compiler_params=_cparams("parallel"),
        name="rglru_scan",
    )(lu, lg, *args)


def _merge_kernel(x_ref, mod_ref, g1_ref, a_ref, b_ref, c_ref, d_ref, wg_ref, wa_ref, wb_ref, wc_ref, wd_ref,
                  wo_ref, o_ref):
    group, tm, dm = x_ref.shape
    m = mod_ref[:, 0]
    x = x_ref[...]
    h = _normmod(x, g1_ref[...], m[:, 1:2], m[:, 0:1]).reshape(group * tm, dm).astype(BF16)
    acc = None
    for k, (br, w) in enumerate(((a_ref, wa_ref), (b_ref, wb_ref), (c_ref, wc_ref), (d_ref, wd_ref))):
        term = (1.0 + jnp.tanh(_dot(h, wg_ref[:, k * dm:(k + 1) * dm]))) * _dot(
            br[...].reshape(group * tm, br.shape[-1]), w[...])
        acc = term if acc is None else acc + term
    y = _dot(acc.astype(BF16), wo_ref[...])
    o_ref[...] = x + m[:, 2:3] * y.reshape(group, tm, dm)


def _merge(xa, modtab, g1, branches, w_gate, lp, n_lat_tiles, tiles_used):
    b, _, d = xa.shape
    wbr = lp['w_branch']
    head_rows = lambda w, dv: jnp.concatenate(
        [_pad_to(w[hd * dv:(hd + 1) * dv], LANE, 0) for hd in range(4)], axis=0)
    wbr = 0.5 * wbr
    weights = (head_rows(wbr[0], MLA_V).astype(BF16), head_rows(wbr[1], NA_HEAD_DIM).astype(BF16),
               wbr[2].astype(BF16), wbr[3].astype(BF16), lp['w_out'].astype(BF16))
    group = _batch_group(b)
    full = lambda a: pl.BlockSpec(a.shape, lambda bb, i: (0,) * a.ndim)
    tok = lambda n: pl.BlockSpec((group, TM, n), lambda bb, i: (bb, i, 0))
    return pl.pallas_call(
        _merge_kernel,
        grid=(b // group, tiles_used),
        in_specs=[tok(d), pl.BlockSpec((group, 1, 8, d), _kind_map(n_lat_tiles)), full(g1)]
                 + [tok(br.shape[-1]) for br in branches] + [full(w_gate)] + [full(w) for w in weights],
        out_specs=tok(d),
        out_shape=jax.ShapeDtypeStruct((b, tiles_used * TM, d), F32),
        compiler_params=_cparams("parallel", "parallel"),
        name="merge_branches",
    )(xa, modtab, g1, *branches, w_gate, *weights)


U32 = jnp.uint32
EXPERT_ROWS = 512


def _pack_pair(x):
    n = x.shape[-1] // 2
    hi = lax.bitcast_convert_type(x[:, :n].astype(BF16).astype(F32), U32)
    lo = lax.bitcast_convert_type(x[:, n:].astype(BF16).astype(F32), U32)
    return hi | (lo >> 16)


def _unpack_pair(p):
    hi = lax.bitcast_convert_type(p & jnp.uint32(0xFFFF0000), F32)
    lo = lax.bitcast_convert_type(p << 16, F32)
    return hi, lo


def _router_kernel(x_ref, mod_ref, g2_ref, rw_ref, rb_ref, tri_ref, h2_o, idx_o, wts_o, rank_o, cnt_o, carry):
    i = pl.program_id(0)

    @pl.when(i == 0)
    def _():
        carry[...] = jnp.zeros_like(carry)

    m = mod_ref[0, 0]
    h2 = _normmod(x_ref[0], g2_ref[...], m[4:5], m[3:4])
    half = h2.shape[-1] // 2
    h2_o[0] = _pack_pair(h2[:, :half])
    h2_o[1] = _pack_pair(h2[:, half:])
    logits = lax.dot_general(rw_ref[...], h2, (((1,), (1,)), ((), ())), preferred_element_type=F32,
                             precision=lax.Precision.HIGHEST)
    scores = _sigmoid(logits)
    biased = scores + rb_ref[...]
    expert = lax.broadcasted_iota(I32, scores.shape, 0)
    picks = []
    onehot_all = jnp.zeros(scores.shape, F32)
    for _ in range(TOP_K):
        best = jnp.max(biased, axis=0, keepdims=True)
        arg = jnp.min(jnp.where(biased == best, expert, N_EXPERTS), axis=0, keepdims=True)
        hit = expert == arg
        sel = jnp.sum(jnp.where(hit, scores, 0.0), axis=0, keepdims=True)
        biased = jnp.where(hit, -jnp.inf, biased)
        onehot_all = onehot_all + jnp.where(hit, 1.0, 0.0)
        picks.append((arg, hit, sel))
    total = picks[0][2]
    for _, _, sel in picks[1:]:
        total = total + sel
    earlier = _dot(onehot_all.astype(BF16), tri_ref[...]) + carry[...]
    pad_rows = TOPK_PAD - TOP_K
    ranks = [jnp.sum(jnp.where(hit, earlier, 0.0), axis=0, keepdims=True).astype(I32) for _, hit, _ in picks]
    scale = ROUTED_SCALE / total
    idx_o[...] = jnp.concatenate([arg for arg, _, _ in picks] + [jnp.zeros((pad_rows, TM), I32)], axis=0)
    wts_o[...] = jnp.concatenate([sel * scale for _, _, sel in picks] + [jnp.zeros((pad_rows, TM), F32)], axis=0)
    rank_o[...] = jnp.concatenate(ranks + [jnp.zeros((pad_rows, TM), I32)], axis=0)
    carry[...] = carry[...] + jnp.sum(onehot_all, axis=1, keepdims=True)
    cnt_o[...] = carry[...]


def _tile_maps(tiles_used, n_lat_tiles):
    tok = lambda i: (i // tiles_used, i % tiles_used, 0)
    mod = lambda i: (i // tiles_used, jnp.where(i % tiles_used >= n_lat_tiles, 1, 0), 0, 0)
    return tok, mod


def _route(x1, modtab, g2, lp, n_lat_tiles, tiles_used):
    b, _, d = x1.shape
    n_tiles = b * tiles_used
    t = n_tiles * TM
    rw = lp['router_w'].T
    rb = lp['router_bias'].reshape(-1, 1)
    tri = (np.arange(TM)[:, None] < np.arange(TM)[None, :]).astype(np.float32)
    tri = jnp.asarray(tri, BF16)
    per_tok = lambda: pl.BlockSpec((TOPK_PAD, TM), lambda i: (0, i))
    full = lambda a: pl.BlockSpec(a.shape, lambda i: (0,) * a.ndim)
    tok, mod = _tile_maps(tiles_used, n_lat_tiles)
    return pl.pallas_call(
        _router_kernel,
        grid=(n_tiles,),
        in_specs=[pl.BlockSpec((1, TM, d), tok), pl.BlockSpec((1, 1, 8, d), mod),
                  full(g2), full(rw), full(rb), full(tri)],
        out_specs=[pl.BlockSpec((2, TM, d // 4), lambda i: (0, i, 0)), per_tok(), per_tok(), per_tok(),
                   pl.BlockSpec((N_EXPERTS, 1), lambda i: (0, 0))],
        out_shape=[jax.ShapeDtypeStruct((2, t, d // 4), U32), jax.ShapeDtypeStruct((TOPK_PAD, t), I32),
                   jax.ShapeDtypeStruct((TOPK_PAD, t), F32), jax.ShapeDtypeStruct((TOPK_PAD, t), I32),
                   jax.ShapeDtypeStruct((N_EXPERTS, 1), F32)],
        scratch_shapes=[pltpu.VMEM((N_EXPERTS, 1), F32)],
        compiler_params=_cparams("arbitrary"),
        name="moe_router",
    )(x1, modtab, g2, rw, rb, tri)


SC_WINDOW = 128


def _sc_mesh():
    return plsc.VectorSubcoreMesh(core_axis_name="c", subcore_axis_name="s")


def _sc_scatter_rows(src, idx, n_out):
    n, width = src.shape
    k_rep = idx.shape[0]
    half = n // SC_WINDOW // 2

    @functools.partial(pl.kernel, out_type=jax.ShapeDtypeStruct((n_out, width), src.dtype), mesh=_sc_mesh(),
                       scratch_types=[], name="moe_dispatch_sc")
    def scatter(src_hbm, idx_hbm, out_hbm):
        def body(x_vmem, *i_vmems):
            for i_vmem in i_vmems:
                pltpu.sync_copy(x_vmem, out_hbm.at[i_vmem.at[0]])

        pltpu.emit_pipeline(
            body,
            grid=(2, half),
            in_specs=[pl.BlockSpec((SC_WINDOW, width), lambda a, i: (a * half + i, 0))]
                     + [pl.BlockSpec((1, SC_WINDOW), lambda a, i, k=k: (k, a * half + i)) for k in range(k_rep)],
            out_specs=[],
            core_axis_name=("c", "s"),
            dimension_semantics=(pltpu.PARALLEL, pltpu.PARALLEL),
        )(src_hbm, *([idx_hbm] * k_rep))

    return scatter(src, idx)


def _sc_gather_rows(src, idx):
    k_rep, n = idx.shape
    width = src.shape[1]
    n_win = n // SC_WINDOW

    @functools.partial(pl.kernel, out_type=jax.ShapeDtypeStruct((k_rep * n, width), src.dtype), mesh=_sc_mesh(),
                       scratch_types=[], name="moe_gather_sc")
    def gather(src_hbm, idx_hbm, out_hbm):
        def body(i_vmem, o_vmem):
            pltpu.sync_copy(src_hbm.at[i_vmem.at[0]], o_vmem)

        pltpu.emit_pipeline(
            body,
            grid=(k_rep, n_win),
            in_specs=[pl.BlockSpec((1, SC_WINDOW), lambda k, i: (k, i))],
            out_specs=[pl.BlockSpec((SC_WINDOW, width), lambda k, i: (k * n_win + i, 0))],
            core_axis_name=("c", "s"),
            dimension_semantics=(pltpu.PARALLEL, pltpu.PARALLEL),
        )(idx_hbm, out_hbm)

    return gather(src, idx)


def _unpack_planes(p0, p1):
    return _unpack_pair(p0) + _unpack_pair(p1)


def _dot_quarters(parts, w_ref):
    q = parts[0].shape[-1]
    acc = None
    for j, part in enumerate(parts):
        term = _dot(part.astype(BF16), w_ref[j * q:(j + 1) * q, :])
        acc = term if acc is None else acc + term
    return acc


def _expert_kernel(be_ref, nv_ref, xs_ref, wg_ref, wu_ref, wd_ref, ys_o, wg_s, wu_s, wd_s):
    i = pl.program_id(0)
    prev = be_ref[jnp.maximum(i - 1, 0)]

    @pl.when((i == 0) | (be_ref[i] != prev))
    def _():
        wg_s[...] = (0.5 * wg_ref[0]).astype(BF16)
        wu_s[...] = wu_ref[0].astype(BF16)
        wd_s[...] = wd_ref[0].astype(BF16)

    @pl.when(nv_ref[i] > 0)
    def _():
        keep = lax.broadcasted_iota(I32, xs_ref.shape[1:], 0) < nv_ref[i]
        parts = _unpack_planes(jnp.where(keep, xs_ref[0], jnp.uint32(0)), jnp.where(keep, xs_ref[1], jnp.uint32(0)))
        hid = _half_silu(_dot_quarters(parts, wg_s)) * _dot_quarters(parts, wu_s)
        y = _dot(hid.astype(BF16), wd_s[...])
        half = y.shape[-1] // 2
        ys_o[0] = _pack_pair(y[:, :half])
        ys_o[1] = _pack_pair(y[:, half:])

    @pl.when(nv_ref[i] <= 0)
    def _():
        ys_o[...] = jnp.zeros_like(ys_o)


def _experts(xs, block_e, n_valid, weights, layer):
    _, n_rows, dq = xs.shape
    d = 4 * dq
    n_blocks = n_rows // EXPERT_ROWS
    hid = EXPERT_HIDDEN
    grid_spec = pltpu.PrefetchScalarGridSpec(
        num_scalar_prefetch=2,
        grid=(n_blocks,),
        in_specs=[pl.BlockSpec((2, EXPERT_ROWS, dq), lambda i, be, nv: (0, i, 0)),
                  pl.BlockSpec((None, 1, d, hid), lambda i, be, nv: (layer, be[i], 0, 0)),
                  pl.BlockSpec((None, 1, d, hid), lambda i, be, nv: (layer, be[i], 0, 0)),
                  pl.BlockSpec((None, 1, hid, d), lambda i, be, nv: (layer, be[i], 0, 0))],
        out_specs=pl.BlockSpec((2, EXPERT_ROWS, dq), lambda i, be, nv: (0, i, 0)),
        scratch_shapes=[pltpu.VMEM((d, hid), BF16), pltpu.VMEM((d, hid), BF16), pltpu.VMEM((hid, d), BF16)],
    )
    return pl.pallas_call(
        _expert_kernel,
        grid_spec=grid_spec,
        out_shape=jax.ShapeDtypeStruct((2, n_rows, dq), U32),
        compiler_params=_cparams("arbitrary"),
        name="moe_experts",
    )(block_e, n_valid, xs, *weights)


def _combine_kernel(g_ref, wts_ref, h2_ref, x_ref, mod_ref, sg_ref, su_ref, sd_ref, gf_ref, o_ref, *, final):
    parts = _unpack_planes(h2_ref[0], h2_ref[1])
    hid = _half_silu(_dot_quarters(parts, sg_ref)) * _dot_quarters(parts, su_ref)
    shared = _dot(hid.astype(BF16), sd_ref[...])
    wts = wts_ref[...]
    routed = None
    for k in range(TOP_K):
        w = wts[:, k:k + 1]
        terms = [w * part for part in _unpack_planes(g_ref[k, 0], g_ref[k, 1])]
        routed = terms if routed is None else [r + t for r, t in zip(routed, terms)]
    f = shared + jnp.concatenate(routed, axis=1)
    m = mod_ref[0, 0]
    x2 = x_ref[0] + m[5:6] * f
    if final:
        x2 = x2 * lax.rsqrt(jnp.mean(x2 * x2, axis=-1, keepdims=True) + NORM_EPS) * gf_ref[...]
    o_ref[0] = x2


def _combine(g, wts, h2p, x1, modtab, lp, g_final, n_lat_tiles, tiles_used, final):
    b, _, d = x1.shape
    dq = d // 4
    weights = ((0.5 * lp['sh_w_gate']).astype(BF16), lp['sh_w_up'].astype(BF16), lp['sh_w_down'].astype(BF16),
               g_final.reshape(1, -1))
    full = lambda a: pl.BlockSpec(a.shape, lambda i: (0,) * a.ndim)
    tok, mod = _tile_maps(tiles_used, n_lat_tiles)
    return pl.pallas_call(
        functools.partial(_combine_kernel, final=final),
        grid=(b * tiles_used,),
        in_specs=[pl.BlockSpec((TOP_K, 2, TM, dq), lambda i: (0, 0, i, 0)),
                  pl.BlockSpec((TM, TOPK_PAD), lambda i: (i, 0)),
                  pl.BlockSpec((2, TM, dq), lambda i: (0, i, 0)),
                  pl.BlockSpec((1, TM, d), tok), pl.BlockSpec((1, 1, 8, d), mod)]
                 + [full(w) for w in weights],
        out_specs=pl.BlockSpec((1, TM, d), tok),
        out_shape=jax.ShapeDtypeStruct((b, tiles_used * TM, d), F32),
        compiler_params=_cparams("parallel"),
        name="moe_combine",
    )(g, wts, h2p, x1, modtab, *weights)


def _moe(x1, modtab, g2, lp, g_final, n_lat_tiles, final):
    b, nt, d = x1.shape
    tiles_used = n_lat_tiles if final else nt // TM
    t = b * tiles_used * TM
    h2p, idx, wts, rank, cnt = _route(x1, modtab, g2, lp, n_lat_tiles, tiles_used)
    counts = cnt[:, 0].astype(I32)
    padded = (counts + EXPERT_ROWS - 1) // EXPERT_ROWS * EXPERT_ROWS
    p_ends = jnp.cumsum(padded)
    p_starts = p_ends - padded
    n_blocks = (t * TOP_K + N_EXPERTS * (EXPERT_ROWS - 1)) // EXPERT_ROWS
    n_rows = n_blocks * EXPERT_ROWS
    chosen = idx[:TOP_K, None, :] == jnp.arange(N_EXPERTS, dtype=I32)[None, :, None]
    dest = jnp.sum(jnp.where(chosen, p_starts[None, :, None], 0), axis=1) + rank[:TOP_K]
    plane_idx = (dest[:, None, :] + (jnp.arange(2, dtype=I32) * n_rows)[None, :, None]).reshape(TOP_K, 2 * t)
    blk_start = jnp.arange(n_blocks, dtype=I32) * EXPERT_ROWS
    block_e = jnp.minimum(jnp.sum((p_ends[None, :] <= blk_start[:, None]).astype(I32), axis=1), N_EXPERTS - 1)
    n_valid = jnp.clip((p_starts + counts)[block_e] - blk_start, 0, EXPERT_ROWS).astype(I32)
    dq = d // 4
    xs = _sc_scatter_rows(h2p.reshape(2 * t, dq), plane_idx, 2 * n_rows).reshape(2, n_rows, dq)
    ys = _experts(xs, block_e, n_valid, lp['expert_stacks'], lp['layer'])
    g = _sc_gather_rows(ys.reshape(2 * n_rows, dq), plane_idx).reshape(TOP_K, 2, t, dq)
    return _combine(g, wts.T, h2p, x1, modtab, lp, g_final, n_lat_tiles, tiles_used, final)


def _layer(xa, c, c_ctx, lp, consts, g_final, seq, final):
    b, nt, d = xa.shape
    n_lat_tiles = seq // TM
    rope, mats_lat, mats_ctx = consts
    modtab = _mod_table(c, c_ctx, *lp['mod_stacks'], lp['layer'])
    g1 = lp['g_norm1'].reshape(1, -1)
    g2 = lp['g_norm2'].reshape(1, -1)
    pw = _proj_weights(lp)
    q, k, v, nq, nk, nv, hy, lu, lg = _project(xa, modtab, g1, pw, rope, n_lat_tiles)
    tiles_used = n_lat_tiles if final else nt // TM
    br_a = _mla_attention(q, k, v, seq, tiles_used)
    br_b = _na_attention(nq, nk, nv, _na_bias_tables(lp['na_rpb'], seq // GRID_W), seq, tiles_used)
    pre = _hyp_pre(hy, lp['hy_short_w'], lp['hy_short_b'], seq)
    lat_f, lat_bf, ctx_f, ctx_bf = pre
    br_c = _hyena_seq(mats_lat, lat_f, lat_bf, lp, b)
    if not final:
        br_c = jnp.concatenate([br_c, _hyena_seq(mats_ctx, ctx_f, ctx_bf, lp, b)], axis=1)
    br_d = _lru_mixer(lu, lg, lp, seq)
    x1 = _merge(xa, modtab, g1, (br_a, br_b, br_c, br_d), pw['w_gate'], lp, n_lat_tiles, tiles_used)
    return _moe(x1, modtab, g2, lp, g_final, n_lat_tiles, final)


_LAYER_KEYS = ('w_mod', 'b_mod', 'g_norm1', 'g_norm2', 'w_in', 'mla_g_q', 'mla_w_uq', 'mla_g_kv', 'mla_w_ukv',
               'na_rpb', 'hy_short_w', 'hy_short_b', 'hy_w1', 'hy_b1', 'hy_w2', 'hy_b2', 'hy_w3', 'hy_decay',
               'hy_bias', 'lru_conv_w', 'lru_conv_b', 'lru_wa', 'lru_ba', 'lru_wx', 'lru_bx', 'lru_lambda',
               'w_branch', 'w_out', 'router_w', 'router_bias', 'exp_w_gate', 'exp_w_up', 'exp_w_down',
               'sh_w_gate', 'sh_w_up', 'sh_w_down')


def kernel(x, c, ctx, c_ctx, w_mod, b_mod, g_norm1, g_norm2, w_in, mla_g_q, mla_w_uq, mla_g_kv, mla_w_ukv, na_rpb, hy_short_w, hy_short_b, hy_w1, hy_b1, hy_w2, hy_b2, hy_w3, hy_decay, hy_bias, lru_conv_w, lru_conv_b, lru_wa, lru_ba, lru_wx, lru_bx, lru_lambda, w_branch, w_out, router_w, router_bias, exp_w_gate, exp_w_up, exp_w_down, sh_w_gate, sh_w_up, sh_w_down, g_final):
    stacked = dict(zip(_LAYER_KEYS, (w_mod, b_mod, g_norm1, g_norm2, w_in, mla_g_q, mla_w_uq, mla_g_kv, mla_w_ukv,
                                     na_rpb, hy_short_w, hy_short_b, hy_w1, hy_b1, hy_w2, hy_b2, hy_w3, hy_decay,
                                     hy_bias, lru_conv_w, lru_conv_b, lru_wa, lru_ba, lru_wx, lru_bx, lru_lambda,
                                     w_branch, w_out, router_w, router_bias, exp_w_gate, exp_w_up, exp_w_down,
                                     sh_w_gate, sh_w_up, sh_w_down)))
    b, seq, d = x.shape
    ctx_len = ctx.shape[1]
    depth = w_mod.shape[0]
    assert seq % TM == 0 and ctx_len % TM == 0 and seq // GRID_W >= NA_KEY_ROWS + 1
    xa = jnp.concatenate([x, ctx], axis=1)
    consts = (_rope_tables(seq, seq + ctx_len), _hyp_matrices(seq), _hyp_matrices(ctx_len))
    for i in range(depth):
        big = ('w_mod', 'b_mod', 'exp_w_gate', 'exp_w_up', 'exp_w_down')
        lp = {name: w[i] for name, w in stacked.items() if name not in big}
        lp['layer'] = i
        lp['mod_stacks'] = (w_mod, b_mod)
        lp['expert_stacks'] = (exp_w_gate, exp_w_up, exp_w_down)
        xa = _layer(xa, c, c_ctx, lp, consts, g_final, seq, i == depth - 1)
    return xa
```

```python
import functools
import math

import numpy as np
import jax
import jax.numpy as jnp
from jax import lax
from jax.experimental import pallas as pl
from jax.experimental.pallas import tpu as pltpu
from jax.experimental.pallas import tpu_sc as plsc

F32 = jnp.float32
BF16 = jnp.bfloat16
I32 = jnp.int32

TM = 256
LANE = 128
GRID_W = 64
N_MOD = 6
NORM_EPS = 1e-6

MLA_HEADS, MLA_NOPE, MLA_ROPE, MLA_V = 4, 64, 32, 64
MLA_Q_RANK, MLA_KV_RANK = 192, 128
MLA_Q_PAD = 256
ROPE_THETA = 10000.0

NA_HEADS, NA_HEAD_DIM, NA_WIN_R, NA_WIN_C = 4, 64, 8, 16
NA_TILE_ROWS = TM // GRID_W
NA_KEY_ROWS = NA_TILE_ROWS + NA_WIN_R - 1
NA_KEYS = NA_KEY_ROWS * GRID_W

HY_WIDTH, HY_ORDER, HY_SHORT, HY_BANDS, HY_FFN = 256, 2, 3, 16, 64
HY_EMB = 2 * HY_BANDS + 1

LRU_WIDTH, LRU_BLOCKS, LRU_CONV, LRU_C = 256, 4, 4, 8.0
LRU_CHUNK = 256
LRU_HALO = 8

N_EXPERTS, TOP_K, EXPERT_HIDDEN, ROUTED_SCALE, MOE_BLOCK = 64, 6, 256, 2.5, 256
TOPK_PAD = 8

VMEM_LIMIT = 52 * 1024 * 1024


def _cparams(*sem):
    return pltpu.CompilerParams(dimension_semantics=sem, vmem_limit_bytes=VMEM_LIMIT)


def _dot(a, b):
    return jnp.dot(a, b, preferred_element_type=F32)


def _dot_nt(a, b):
    return lax.dot_general(a, b, (((1,), (1,)), ((), ())), preferred_element_type=F32)


def _sigmoid(x):
    return jax.nn.sigmoid(x)


def _half_silu(g):
    return g * (1.0 + jnp.tanh(g))


def _silu(x):
    return x * _sigmoid(x)


def _gelu_tanh(x):
    return 0.5 * x * (1.0 + jnp.tanh(math.sqrt(2.0 / math.pi) * (x + 0.044715 * (x * x * x))))


def _normmod(x, g, scale, shift):
    y = x * lax.rsqrt(jnp.mean(x * x, axis=-1, keepdims=True) + NORM_EPS) * g
    return y * (1.0 + scale) + shift


def _mod_kernel(c_ref, w_ref, b_ref, o_ref):
    s = _silu(c_ref[...])
    o_ref[...] = _dot(s.astype(BF16), w_ref[...].astype(BF16)) + b_ref[...]


def _mod_table(c, c_ctx, w_mod, b_mod, layer):
    b, d = c.shape
    rows = 16
    cc = jnp.zeros((rows, d), F32).at[:b].set(c).at[b].set(c_ctx)
    tn = 1024
    mod = pl.pallas_call(
        _mod_kernel,
        grid=(N_MOD * d // tn,),
        in_specs=[pl.BlockSpec((rows, d), lambda j: (0, 0)),
                  pl.BlockSpec((None, d, tn), lambda j: (layer, 0, j)),
                  pl.BlockSpec((None, 1, tn), lambda j: (layer, 0, j))],
        out_specs=pl.BlockSpec((rows, tn), lambda j: (0, j)),
        out_shape=jax.ShapeDtypeStruct((rows, N_MOD * d), F32),
        compiler_params=_cparams("arbitrary"),
        name="mod_vectors",
    )(cc, w_mod, b_mod[:, None, :])
    lat = mod[:b].reshape(b, N_MOD, d)
    ctx = jnp.broadcast_to(mod[b].reshape(1, N_MOD, d), (b, N_MOD, d))
    tab = jnp.stack([lat, ctx], axis=1)
    return jnp.pad(tab, ((0, 0), (0, 0), (0, 8 - N_MOD), (0, 0)))


_C_QLAT, _C_KVLAT, _C_KR, _C_KRR, _C_NA, _C_HY, _C_LU, _C_LG, _C_END = (
    0, 256, 384, 512, 640, 640 + 3 * NA_HEADS * LANE, 640 + 1536 + 768, 640 + 1536 + 1024, 640 + 1536 + 1280)


def _stream_tile(lat_ref, ctx_ref, n_lat_tiles):
    return jnp.where(pl.program_id(1) < n_lat_tiles, lat_ref[...], ctx_ref[...])


def _stream_specs(stream, group, n_lat_tiles):
    lat_src, ctx_src, ctx_tile0 = stream
    d = lat_src.shape[-1]
    return [pl.BlockSpec((group, TM, d), lambda bb, i: (bb, jnp.minimum(i, n_lat_tiles - 1), 0)),
            pl.BlockSpec((group, TM, d), lambda bb, i: (bb, ctx_tile0 + jnp.maximum(i - n_lat_tiles, 0), 0))]


def _proj_kernel(xl_ref, xc_ref, mod_ref, g1_ref, w1_ref, gq_ref, gkv_ref, wuq_ref, wuqr_ref, wk_ref, wv_ref,
                 cos_ref, sin_ref, q_o, k_o, v_o, nq_o, nk_o, nv_o, hy_o, lu_o, lg_o, *, n_lat_tiles):
    group, tm, d = xl_ref.shape
    m = mod_ref[:, 0]
    x = _stream_tile(xl_ref, xc_ref, n_lat_tiles)
    h = _normmod(x, g1_ref[...], m[:, 1:2], m[:, 0:1]).reshape(group * tm, d)
    z = _dot(h.astype(BF16), w1_ref[...])
    qlat = z[:, _C_QLAT:_C_KVLAT]
    kvlat = z[:, _C_KVLAT:_C_KR]
    kr = z[:, _C_KR:_C_KRR]
    krr = z[:, _C_KRR:_C_NA]
    qn = qlat * lax.rsqrt(jnp.sum(qlat * qlat, axis=-1, keepdims=True) * (1.0 / MLA_Q_RANK) + NORM_EPS) * gq_ref[...]
    kvn = kvlat * lax.rsqrt(jnp.mean(kvlat * kvlat, axis=-1, keepdims=True) + NORM_EPS) * gkv_ref[...]
    qn = qn.astype(BF16)
    kvn = kvn.astype(BF16)
    q = _dot(qn, wuq_ref[...])
    qr = _dot(qn, wuqr_ref[...])
    kk = _dot(kvn, wk_ref[...])
    vv = _dot(kvn, wv_ref[...])
    cos = jnp.concatenate([cos_ref[...]] * group, axis=0)
    sin = jnp.concatenate([sin_ref[...]] * group, axis=0)
    krope = kr * cos + krr * sin
    den_lane = lax.broadcasted_iota(I32, cos.shape, 1) == SOFTMAX_DEN_LANE
    def put_heads(ref, hd, val):
        for g in range(group):
            ref[g, hd] = val[g * tm:(g + 1) * tm].astype(BF16)

    def put_pairs(ref, hd, val):
        off = (hd % 2) * LANE
        for g in range(group):
            ref[g, hd // 2, :, off:off + LANE] = val[g * tm:(g + 1) * tm].astype(BF16)

    for hd in range(MLA_HEADS):
        sl = slice(hd * LANE, (hd + 1) * LANE)
        put_heads(q_o, hd, q[:, sl] * cos + qr[:, sl] * sin)
        put_heads(k_o, hd, kk[:, sl] + krope)
        put_heads(v_o, hd, jnp.where(den_lane, 1.0, vv[:, sl]))
    for hd in range(NA_HEADS):
        for which, ref in enumerate((nq_o, nk_o, nv_o)):
            lo = _C_NA + (which * NA_HEADS + hd) * LANE
            blk = z[:, lo:lo + LANE]
            if which == 2:
                put_pairs(ref, hd, jnp.where(den_lane, 1.0, blk))
            else:
                put_heads(ref, hd, blk)
    hy_o[...] = z[:, _C_HY:_C_LU].reshape(group, tm, _C_LU - _C_HY)
    lu_o[...] = z[:, _C_LU:_C_LG].reshape(group, tm, _C_LG - _C_LU)
    lg_o[...] = z[:, _C_LG:_C_END].reshape(group, tm, _C_END - _C_LG)


def _pad_to(a, n, axis):
    pad = [(0, 0)] * a.ndim
    pad[axis] = (0, n - a.shape[axis])
    return jnp.pad(a, pad)


def _rot_cols(w):
    half = w.shape[-1] // 2
    return jnp.concatenate([-w[..., half:], w[..., :half]], axis=-1)


def _head_blocks(cols_per_head):
    out = []
    for pieces in cols_per_head:
        k = pieces[0][0].shape[0]
        blk = jnp.zeros((k, LANE), F32)
        for arr, off in pieces:
            blk = blk.at[:, off:off + arr.shape[1]].set(arr)
        out.append(blk)
    return jnp.concatenate(out, axis=1)


def _proj_weights(lp):
    w_in = lp['w_in']
    d = w_in.shape[0]
    o = 0
    parts = {}
    for name, n in (('q', MLA_Q_RANK), ('kv', MLA_KV_RANK), ('kr', MLA_ROPE), ('na', 3 * NA_HEADS * NA_HEAD_DIM),
                    ('hy', 3 * HY_WIDTH), ('lu', LRU_WIDTH), ('lg', LRU_WIDTH), ('gt', 4 * d)):
        parts[name] = w_in[:, o:o + n]
        o += n
    zeros = lambda n: jnp.zeros((d, n), F32)
    kr_blk = jnp.concatenate([zeros(MLA_NOPE), parts['kr'], zeros(LANE - MLA_NOPE - MLA_ROPE)], axis=1)
    krr_blk = jnp.concatenate([zeros(MLA_NOPE), _rot_cols(parts['kr']), zeros(LANE - MLA_NOPE - MLA_ROPE)], axis=1)
    na_scale = NA_HEAD_DIM ** -0.5
    na_cols = []
    for which in range(3):
        for hd in range(NA_HEADS):
            lo = (which * NA_HEADS + hd) * NA_HEAD_DIM
            blk = parts['na'][:, lo:lo + NA_HEAD_DIM] * (na_scale if which == 0 else 1.0)
            na_cols.append(_pad_to(blk, LANE, 1))
    w1 = jnp.concatenate([_pad_to(parts['q'], MLA_Q_PAD, 1), parts['kv'], kr_blk, krr_blk] + na_cols
                         + [parts['hy'], parts['lu'], parts['lg']], axis=1).astype(BF16)
    mla_scale = (MLA_NOPE + MLA_ROPE) ** -0.5
    wuq = _pad_to(lp['mla_w_uq'], MLA_Q_PAD, 0) * mla_scale
    dq = MLA_NOPE + MLA_ROPE
    wuq_main = _head_blocks([[(wuq[:, hd * dq:hd * dq + dq], 0)] for hd in range(MLA_HEADS)])
    wuq_rot = _head_blocks([[(_rot_cols(wuq[:, hd * dq + MLA_NOPE:hd * dq + dq]), MLA_NOPE)]
                            for hd in range(MLA_HEADS)])
    dkv = MLA_NOPE + MLA_V
    wukv = lp['mla_w_ukv']
    wk = _head_blocks([[(wukv[:, hd * dkv:hd * dkv + MLA_NOPE], 0)] for hd in range(MLA_HEADS)])
    wv = _head_blocks([[(wukv[:, hd * dkv + MLA_NOPE:hd * dkv + dkv], 0)] for hd in range(MLA_HEADS)])
    gq = _pad_to(lp['mla_g_q'].reshape(1, -1), MLA_Q_PAD, 1)
    gkv = lp['mla_g_kv'].reshape(1, -1)
    return dict(w1=w1, w_gate=(0.5 * parts['gt']).astype(BF16), gq=gq, gkv=gkv, wuq=wuq_main.astype(BF16),
                wuq_rot=wuq_rot.astype(BF16), wk=wk.astype(BF16), wv=wv.astype(BF16))


def _rope_tables(seq, n_tok):
    t = jnp.arange(seq, dtype=I32)
    row = (t // GRID_W).astype(F32)
    col = (t % GRID_W).astype(F32)
    n_axis = MLA_ROPE // 4
    inv_freq = ROPE_THETA ** (-jnp.arange(n_axis, dtype=F32) / n_axis)
    ang = jnp.concatenate([row[:, None] * inv_freq, col[:, None] * inv_freq], axis=-1)
    cos = jnp.concatenate([jnp.cos(ang), jnp.cos(ang)], axis=-1)
    sin = jnp.concatenate([jnp.sin(ang), jnp.sin(ang)], axis=-1)
    cos_t = jnp.ones((n_tok, LANE), F32).at[:seq, MLA_NOPE:MLA_NOPE + MLA_ROPE].set(cos)
    sin_t = jnp.zeros((n_tok, LANE), F32).at[:seq, MLA_NOPE:MLA_NOPE + MLA_ROPE].set(sin)
    return cos_t, sin_t


def _batch_group(b):
    return 2 if b % 2 == 0 else 1


def _kind_map(n_lat_tiles):
    return lambda b, i: (b, jnp.where(i >= n_lat_tiles, 1, 0), 0, 0)


def _project(stream, nt, modtab, g1, pw, rope, n_lat_tiles):
    b, _, d = stream[0].shape
    cos_t, sin_t = rope
    group = _batch_group(b)
    full = lambda a: pl.BlockSpec(a.shape, lambda bb, i: (0,) * a.ndim)
    head_out = lambda: pl.BlockSpec((group, MLA_HEADS, TM, LANE), lambda bb, i: (bb, 0, i, 0))
    tok_out = lambda n: pl.BlockSpec((group, TM, n), lambda bb, i: (bb, i, 0))
    head_shape = jax.ShapeDtypeStruct((b, MLA_HEADS, nt, LANE), BF16)
    pair_out = lambda: pl.BlockSpec((group, MLA_HEADS // 2, TM, 2 * LANE), lambda bb, i: (bb, 0, i, 0))
    pair_shape = jax.ShapeDtypeStruct((b, MLA_HEADS // 2, nt, 2 * LANE), BF16)
    tok_shape = lambda n: jax.ShapeDtypeStruct((b, nt, n), F32)
    weights = (g1, pw['w1'], pw['gq'], pw['gkv'], pw['wuq'], pw['wuq_rot'], pw['wk'], pw['wv'])
    return pl.pallas_call(
        functools.partial(_proj_kernel, n_lat_tiles=n_lat_tiles),
        grid=(b // group, nt // TM),
        in_specs=_stream_specs(stream, group, n_lat_tiles)
                 + [pl.BlockSpec((group, 1, 8, d), _kind_map(n_lat_tiles))]
                 + [full(w) for w in weights]
                 + [pl.BlockSpec((TM, LANE), lambda bb, i: (i, 0))] * 2,
        out_specs=[head_out(), head_out(), head_out(), head_out(), head_out(), pair_out(),
                   tok_out(3 * HY_WIDTH), tok_out(LRU_WIDTH), tok_out(LRU_WIDTH)],
        out_shape=[head_shape, head_shape, head_shape, head_shape, head_shape, pair_shape,
                   tok_shape(3 * HY_WIDTH), tok_shape(LRU_WIDTH), tok_shape(LRU_WIDTH)],
        compiler_params=_cparams("parallel", "parallel"),
        name="input_projection",
    )(stream[0], stream[1], modtab, *weights, cos_t, sin_t)


SOFTMAX_DEN_LANE = 64


def _softmax_pv(parts, lane_off=0):
    m = None
    for s, _ in parts:
        mm = jnp.max(s, axis=-1, keepdims=True)
        m = mm if m is None else jnp.maximum(m, mm)
    acc = None
    for s, v in parts:
        o = _dot(jnp.exp(s - m).astype(BF16), v)
        acc = o if acc is None else acc + o
    den = lane_off + SOFTMAX_DEN_LANE
    return acc[:, lane_off:lane_off + LANE] / acc[:, den:den + 1]


def _mla_kernel(q_ref, k_ref, v_ref, o_ref, *, seq, n_lat_tiles):
    i = pl.program_id(1)
    nt = k_ref.shape[2]

    def attend(lo, hi):
        for hd in range(MLA_HEADS):
            s = _dot_nt(q_ref[0, hd], k_ref[0, hd, lo:hi, :])
            o = _softmax_pv([(s, v_ref[0, hd, lo:hi, :])])
            o_ref[0, :, hd * LANE:(hd + 1) * LANE] = o.astype(BF16)

    @pl.when(i < n_lat_tiles)
    def _():
        attend(0, nt)

    @pl.when(i >= n_lat_tiles)
    def _():
        attend(seq, nt)


def _mla_attention(q, k, v, seq, tiles_used):
    b, h, nt, _ = q.shape
    kv_spec = pl.BlockSpec((1, h, nt, LANE), lambda bb, i: (bb, 0, 0, 0))
    return pl.pallas_call(
        functools.partial(_mla_kernel, seq=seq, n_lat_tiles=seq // TM),
        grid=(b, tiles_used),
        in_specs=[pl.BlockSpec((1, h, TM, LANE), lambda bb, i: (bb, 0, i, 0)), kv_spec, kv_spec],
        out_specs=pl.BlockSpec((1, TM, h * LANE), lambda bb, i: (bb, i, 0)),
        out_shape=jax.ShapeDtypeStruct((b, tiles_used * TM, h * LANE), BF16),
        compiler_params=_cparams("parallel", "arbitrary"),
        name="mla_attention",
    )(q, k, v)


def _na_bias_tables(rpb, rows):
    n_blk = rows // NA_TILE_ROWS
    col = np.arange(GRID_W)
    c0 = np.clip(col - NA_WIN_C // 2, 0, GRID_W - NA_WIN_C)
    in_win = (col[None, :] >= c0[:, None]) & (col[None, :] < c0[:, None] + NA_WIN_C)
    dc = np.clip(col[None, :] - col[:, None], 1 - NA_WIN_C, NA_WIN_C - 1) + NA_WIN_C - 1
    rpb = rpb.astype(F32)
    tables = []
    for j in (0, 1, n_blk - 1):
        w0 = min(max(NA_TILE_ROWS * j - NA_WIN_R // 2, 0), rows - NA_KEY_ROWS)
        r = NA_TILE_ROWS * j + np.arange(NA_TILE_ROWS)
        kr = w0 + np.arange(NA_KEY_ROWS)
        r0 = np.clip(r - NA_WIN_R // 2, 0, rows - NA_WIN_R)
        row_ok = (kr[None, :] >= r0[:, None]) & (kr[None, :] < r0[:, None] + NA_WIN_R)
        dr = np.clip(kr[None, :] - r[:, None] + NA_WIN_R - 1, 0, 2 * NA_WIN_R - 2)
        oh_r = jnp.asarray(np.eye(2 * NA_WIN_R - 1, dtype=np.float32)[dr.reshape(-1)])
        oh_c = jnp.asarray(np.eye(2 * NA_WIN_C - 1, dtype=np.float32)[dc.reshape(-1)])
        bias = jnp.einsum('ar,hrc,bc->hab', oh_r, rpb, oh_c, precision=lax.Precision.HIGHEST)
        bias = bias.reshape(NA_HEADS, NA_TILE_ROWS, NA_KEY_ROWS, GRID_W, GRID_W)
        mask = row_ok[:, :, None, None] & in_win[None, None, :, :]
        bias = jnp.where(jnp.asarray(mask)[None], bias, -jnp.inf)
        tables.append(bias.transpose(0, 1, 3, 2, 4).reshape(NA_HEADS, TM, NA_KEYS))
    return jnp.stack(tables)


def _na_kernel(q_ref, k_ref, v_ref, bias_ref, o_ref, *, seq, n_lat_tiles):
    i = pl.program_id(1)
    nt = k_ref.shape[2]
    rows = seq // GRID_W

    @pl.when(i < n_lat_tiles)
    def _():
        w0 = jnp.clip(NA_TILE_ROWS * i - NA_WIN_R // 2, 0, rows - NA_KEY_ROWS)
        start = pl.multiple_of(w0 * GRID_W, GRID_W)
        for hd in range(NA_HEADS):
            q = q_ref[0, hd]
            s_loc = _dot_nt(q, k_ref[0, hd, pl.ds(start, NA_KEYS), :]) + bias_ref[0, hd]
            s_ctx = _dot_nt(q, k_ref[0, hd, seq:nt, :])
            o = _softmax_pv([(s_loc, v_ref[0, hd // 2, pl.ds(start, NA_KEYS), :]),
                             (s_ctx, v_ref[0, hd // 2, seq:nt, :])], (hd % 2) * LANE)
            o_ref[0, :, hd * LANE:(hd + 1) * LANE] = o.astype(BF16)

    @pl.when(i >= n_lat_tiles)
    def _():
        for hd in range(NA_HEADS):
            s = _dot_nt(q_ref[0, hd], k_ref[0, hd, seq:nt, :])
            o = _softmax_pv([(s, v_ref[0, hd // 2, seq:nt, :])], (hd % 2) * LANE)
            o_ref[0, :, hd * LANE:(hd + 1) * LANE] = o.astype(BF16)


def _na_attention(q, k, v, bias, seq, tiles_used):
    b, h, nt, _ = q.shape
    n_lat = seq // TM
    kv_spec = pl.BlockSpec((1, h, nt, LANE), lambda bb, i: (bb, 0, 0, 0))
    cfg = lambda bb, i: (jnp.where(i == 0, 0, jnp.where(i >= n_lat - 1, 2, 1)), 0, 0, 0)
    return pl.pallas_call(
        functools.partial(_na_kernel, seq=seq, n_lat_tiles=n_lat),
        grid=(b, tiles_used),
        in_specs=[pl.BlockSpec((1, h, TM, LANE), lambda bb, i: (bb, 0, i, 0)), kv_spec,
                  pl.BlockSpec((1, h // 2, nt, 2 * LANE), lambda bb, i: (bb, 0, 0, 0)),
                  pl.BlockSpec((1, h, TM, NA_KEYS), cfg)],
        out_specs=pl.BlockSpec((1, TM, h * LANE), lambda bb, i: (bb, i, 0)),
        out_shape=jax.ShapeDtypeStruct((b, tiles_used * TM, h * LANE), BF16),
        compiler_params=_cparams("parallel", "arbitrary"),
        name="neighbourhood_attention",
    )(q, k, v, bias)


HY_PRE_CHUNK = 256


def _hy_pre_kernel(z_ref, w_ref, b_ref, lat_o, lat_bf_o, ctx_o, ctx_bf_o, *, seq, ctx_len):
    w = w_ref[...]
    bias = b_ref[...]
    ch = HY_PRE_CHUNK
    zero = jnp.zeros((8, HY_WIDTH), F32)
    for seg_lo, seg_len, o_ref, obf_ref in ((0, seq, lat_o, lat_bf_o), (seq, ctx_len, ctx_o, ctx_bf_o)):
        for c in range(seg_len // ch):
            s = seg_lo + c * ch
            before = zero if c == 0 else z_ref[0, s - 8:s, :]
            after = zero if c == seg_len // ch - 1 else z_ref[0, s + ch:s + ch + 8, :]
            win = jnp.concatenate([before, z_ref[0, s:s + ch, :], after], axis=0)
            n = ch + 16
            y = bias + w[1:2] * win[8:8 + ch]
            y = y + w[0:1] * pltpu.roll(win, 1, axis=0)[8:8 + ch]
            y = y + w[2:3] * pltpu.roll(win, n - 1, axis=0)[8:8 + ch]
            o_ref[0, c * ch:(c + 1) * ch, :] = y
            obf_ref[0, c * ch:(c + 1) * ch, :] = y.astype(BF16)


def _hy_pre(hy, w, bvec, seq):
    b, nt, _ = hy.shape
    ctx_len = nt - seq
    wpad = _pad_to(w, 8, 0)
    out_specs, out_shape = [], []
    for length in (seq, ctx_len):
        for dt in (F32, BF16):
            out_specs.append(pl.BlockSpec((1, length, HY_WIDTH), lambda bb, g: (g, 0, bb)))
            out_shape.append(jax.ShapeDtypeStruct((3, length, b * HY_WIDTH), dt))
    return pl.pallas_call(
        functools.partial(_hy_pre_kernel, seq=seq, ctx_len=ctx_len),
        grid=(b, 3),
        in_specs=[pl.BlockSpec((1, nt, HY_WIDTH), lambda bb, g: (bb, 0, g)),
                  pl.BlockSpec((8, HY_WIDTH), lambda bb, g: (0, g)),
                  pl.BlockSpec((1, HY_WIDTH), lambda bb, g: (0, g))],
        out_specs=out_specs,
        out_shape=out_shape,
        compiler_params=_cparams("parallel", "parallel"),
        name="hyena_short_conv",
    )(hy, wpad, bvec.reshape(1, -1))


def _hp_dot(a, b):
    return jnp.dot(a, b, preferred_element_type=F32, precision=lax.Precision.HIGHEST)


def _hy_filter_kernel(f_ref, w1_ref, b1_ref, w2_ref, b2_ref, w3_ref, dec_ref, o_ref, obf_ref):
    f = f_ref[...]
    h = jnp.sin(_hp_dot(f, w1_ref[...]) + b1_ref[...])
    h = jnp.sin(_hp_dot(h, w2_ref[...]) + b2_ref[...])
    h = _hp_dot(h, w3_ref[...])
    h = h * jnp.exp(-f[:, 0:1] * jnp.abs(dec_ref[...]))
    o_ref[...] = h
    obf_ref[...] = h.astype(BF16)


def _hy_pos_features(length):
    t = jnp.linspace(0.0, 1.0, length, dtype=F32)[:, None]
    w = 2.0 * math.pi * jnp.arange(length, dtype=F32)[:, None] / length
    f = jnp.linspace(1e-4, HY_BANDS - 1, HY_BANDS, dtype=F32)[None, :]
    z = w * f
    return jnp.concatenate([t, jnp.cos(z), -jnp.sin(z)], axis=-1)


def _hy_filters(length, lp):
    feats = _hy_pos_features(length)
    n_out = HY_ORDER * 2 * HY_WIDTH
    tl = min(length, 512)
    full = lambda a: pl.BlockSpec(a.shape, lambda i: (0,) * a.ndim)
    args = (lp['hy_w1'], lp['hy_b1'].reshape(1, -1), lp['hy_w2'], lp['hy_b2'].reshape(1, -1), lp['hy_w3'],
            lp['hy_decay'].reshape(1, n_out))
    return pl.pallas_call(
        _hy_filter_kernel,
        grid=(length // tl,),
        in_specs=[pl.BlockSpec((tl, HY_EMB), lambda i: (i, 0))] + [full(a) for a in args],
        out_specs=[pl.BlockSpec((tl, n_out), lambda i: (i, 0))] * 2,
        out_shape=[jax.ShapeDtypeStruct((length, n_out), F32), jax.ShapeDtypeStruct((length, n_out), BF16)],
        compiler_params=_cparams("parallel"),
        name="hyena_filter_mlp",
    )(feats, *args)


def _dft_matrices(length):
    n2 = 2 * length
    step = DFT_ROW_STEP
    n = jnp.arange(length, dtype=I32)

    def trig(kv):
        ang = ((kv[:, None] * n[None, :]) % n2).astype(F32) * (2.0 * math.pi / n2)
        return jnp.cos(ang), jnp.sin(ang)

    c1, s1 = trig(jnp.arange(length // step, dtype=I32) * step)
    c2, s2 = trig(jnp.arange(step, dtype=I32))
    coarse = pl.BlockSpec((1, 1, length), lambda j: (j, 0, 0))
    fine = pl.BlockSpec((step, length), lambda j: (0, 0))
    out = pl.BlockSpec((step, length), lambda j: (j, 0))
    return pl.pallas_call(
        _dft_matrix_kernel,
        grid=(length // step,),
        in_specs=[coarse, coarse, fine, fine],
        out_specs=[out, out, out],
        out_shape=[jax.ShapeDtypeStruct((length, length), BF16)] * 3,
        compiler_params=_cparams("parallel"),
        name="dft_matrices",
    )(c1[:, None, :], s1[:, None, :], c2, s2)


DFT_ROW_STEP = 64


def _dft_matrix_kernel(c1_ref, s1_ref, c2_ref, s2_ref, mc_o, ms_o, mst_o):
    c1, s1 = c1_ref[0], s1_ref[0]
    c2, s2 = c2_ref[...], s2_ref[...]
    cos = c1 * c2 - s1 * s2
    neg_sin = -(s1 * c2 + c1 * s2)
    row = lax.broadcasted_iota(I32, cos.shape, 0)
    col = lax.broadcasted_iota(I32, cos.shape, 1)
    first_row = (row + pl.program_id(0) * DFT_ROW_STEP) == 0
    mc_o[...] = cos.astype(BF16)
    ms_o[...] = jnp.where(first_row, jnp.where(col % 2 == 0, 1.0, -1.0), neg_sin).astype(BF16)
    mst_o[...] = jnp.where(col == 0, jnp.where(row % 2 == 0, 1.0, -1.0), neg_sin).astype(BF16)


def _dft_fwd_kernel(mc_ref, ms_ref, x_ref, *rest, with_taps):
    ure = _dot(mc_ref[...], x_ref[...])
    uim = _dot(ms_ref[...], x_ref[...])
    if not with_taps:
        ure_o, uim_o = rest
        ure_o[...] = ure
        uim_o[...] = uim
        return
    a_ref, b_ref, c_ref, d_ref, zre_o, zim_o = rest
    a, b, c, d = a_ref[...], b_ref[...], c_ref[...], d_ref[...]
    for bb in range(x_ref.shape[1] // HY_WIDTH):
        sl = slice(bb * HY_WIDTH, (bb + 1) * HY_WIDTH)
        zre_o[:, sl] = (ure[:, sl] * a - uim[:, sl] * b).astype(BF16)
        zim_o[:, sl] = (ure[:, sl] * c + uim[:, sl] * d).astype(BF16)


def _col_block(nc, cap):
    return min(nc, cap)


def _dft_fwd(mats, x, plane, taps=None):
    mc, ms, _ = mats
    _, length, nc = x.shape
    tk = min(length, 256)
    cb = _col_block(nc, 1024)
    grid = (nc // cb, length // tk)
    m_spec = pl.BlockSpec((tk, length), lambda c, j: (j, 0))
    x_spec = pl.BlockSpec((None, length, cb), lambda c, j: (plane, 0, c))
    o_spec = pl.BlockSpec((tk, cb), lambda c, j: (j, c))
    if taps is None:
        return pl.pallas_call(
            functools.partial(_dft_fwd_kernel, with_taps=False),
            grid=grid, in_specs=[m_spec, m_spec, x_spec], out_specs=[o_spec, o_spec],
            out_shape=[jax.ShapeDtypeStruct((length, nc), F32)] * 2,
            compiler_params=_cparams("parallel", "arbitrary"),
            name="hyena_dft_filters",
        )(mc, ms, x)
    t_spec = pl.BlockSpec((tk, HY_WIDTH), lambda c, j: (j, 0))
    return pl.pallas_call(
        functools.partial(_dft_fwd_kernel, with_taps=True),
        grid=grid, in_specs=[m_spec, m_spec, x_spec] + [t_spec] * 4, out_specs=[o_spec, o_spec],
        out_shape=[jax.ShapeDtypeStruct((length, nc), BF16)] * 2,
        compiler_params=_cparams("parallel", "arbitrary"),
        name="hyena_dft_forward",
    )(mc, ms, x, *taps)


def _dft_inv_kernel(mc_ref, mst_ref, zre_ref, zim_ref, gate_ref, prev_ref, bias_ref, *outs, last):
    conv = _dot(mc_ref[...], zre_ref[...]) + _dot(mst_ref[...], zim_ref[...])
    y = gate_ref[...] * (conv + prev_ref[...] * bias_ref[...])
    if last:
        (tok_o,) = outs
        for bb in range(y.shape[1] // HY_WIDTH):
            tok_o[bb] = y[:, bb * HY_WIDTH:(bb + 1) * HY_WIDTH].astype(BF16)
    else:
        y_o, ybf_o = outs
        y_o[...] = y
        ybf_o[...] = y.astype(BF16)


def _dft_inv(mats, zre, zim, gate, gate_plane, prev, prev_plane, bias_row, last):
    mc, _, mst = mats
    length, nc = zre.shape
    tm = min(length, 256)
    cb = _col_block(nc, 512)
    grid = (nc // cb, length // tm)
    m_spec = pl.BlockSpec((tm, length), lambda c, i: (i, 0))
    z_spec = pl.BlockSpec((length, cb), lambda c, i: (0, c))
    e_spec = lambda plane: pl.BlockSpec((None, tm, cb), lambda c, i: (plane, i, c))
    b_spec = pl.BlockSpec((1, cb), lambda c, i: (0, c))
    if last:
        nb = nc // HY_WIDTH
        out_specs = [pl.BlockSpec((cb // HY_WIDTH, tm, HY_WIDTH), lambda c, i: (c, i, 0))]
        out_shape = [jax.ShapeDtypeStruct((nb, length, HY_WIDTH), BF16)]
    else:
        out_specs = [e_spec(0), e_spec(0)]
        out_shape = [jax.ShapeDtypeStruct((1, length, nc), F32), jax.ShapeDtypeStruct((1, length, nc), BF16)]
    return pl.pallas_call(
        functools.partial(_dft_inv_kernel, last=last),
        grid=grid, in_specs=[m_spec, m_spec, z_spec, z_spec, e_spec(gate_plane), e_spec(prev_plane), b_spec],
        out_specs=out_specs, out_shape=out_shape,
        compiler_params=_cparams("parallel", "arbitrary"),
        name="hyena_dft_inverse",
    )(mc, mst, zre, zim, gate, prev, bias_row)


def _hy_tap_tables(ure, uim, filt, length):
    n2 = 2 * length
    w = HY_WIDTH
    scale = jnp.full((length, 1), 2.0 / n2, F32).at[0, 0].set(1.0 / n2)
    tables = []
    for o in range(HY_ORDER):
        f_sl = slice((2 * o) * w, (2 * o + 1) * w)
        b_sl = slice((2 * o + 1) * w, (2 * o + 2) * w)
        hb0 = filt[0:1, b_sl]
        tre = ure[:, f_sl] + ure[:, b_sl] - hb0
        tim = uim[:, f_sl] - uim[:, b_sl]
        t_nyq = uim[0:1, f_sl] + uim[0:1, b_sl] - hb0
        first = (jnp.arange(length) == 0)[:, None]
        a = tre * scale
        bm = jnp.where(first, 0.0, tim * scale)
        c = bm
        d = jnp.where(first, t_nyq * scale, tre * scale)
        tables.append((a, bm, c, d))
    return tables


def _hyena_seq(mats, vx, vx_bf, lp, n_batch):
    length = vx.shape[1]
    filt, filt_bf = _hy_filters(length, lp)
    ure, uim = _dft_fwd(mats, filt_bf[None], 0)
    tables = _hy_tap_tables(ure, uim, filt, length)
    bias = lp['hy_bias'].astype(F32)
    y, y_bf = vx, vx_bf
    for o in range(HY_ORDER):
        zre, zim = _dft_fwd(mats, y_bf, 0, tables[o])
        bias_row = jnp.tile(bias[o][None, :], (1, n_batch))
        last = o == HY_ORDER - 1
        res = _dft_inv(mats, zre, zim, vx, o + 1, y, 0, bias_row, last)
        if last:
            return res[0]
        y, y_bf = res


def _hyp_pre_kernel(z_ref, w_ref, b_ref, lat_o, lat_bf_o, ctx_o, ctx_bf_o, *, seq, ctx_len):
    w = w_ref[...]
    bias = b_ref[...]
    for lo, length, o_ref, obf_ref in ((0, seq, lat_o, lat_bf_o), (seq, ctx_len, ctx_o, ctx_bf_o)):
        h = length // 2
        even = z_ref[0, pl.ds(lo, h, stride=2), :]
        odd = z_ref[0, pl.ds(lo + 1, h, stride=2), :]
        row = lax.broadcasted_iota(I32, even.shape, 0)
        odd_prev = jnp.where(row == 0, 0.0, pltpu.roll(odd, 1, axis=0))
        even_next = jnp.where(row == h - 1, 0.0, pltpu.roll(even, h - 1, axis=0))
        y_even = bias + w[0:1] * odd_prev + w[1:2] * even + w[2:3] * odd
        y_odd = bias + w[0:1] * even + w[1:2] * odd + w[2:3] * even_next
        for r, y in enumerate((y_even, y_odd)):
            o_ref[0, r] = y
            obf_ref[0, r] = y.astype(BF16)


def _hyp_pre(hy, w, bvec, seq):
    b, nt, _ = hy.shape
    ctx_len = nt - seq
    wpad = _pad_to(w, 8, 0)
    per_plane = HY_WIDTH // LANE
    out_specs, out_shape = [], []
    for length in (seq, ctx_len):
        for dt in (F32, BF16):
            out_specs.append(pl.BlockSpec((1, 2, length // 2, LANE),
                                          lambda bb, g: (g // per_plane, 0, 0, bb * per_plane + g % per_plane)))
            out_shape.append(jax.ShapeDtypeStruct((3, 2, length // 2, b * HY_WIDTH), dt))
    return pl.pallas_call(
        functools.partial(_hyp_pre_kernel, seq=seq, ctx_len=ctx_len),
        grid=(b, 3 * per_plane),
        in_specs=[pl.BlockSpec((1, nt, LANE), lambda bb, g: (bb, 0, g)),
                  pl.BlockSpec((8, LANE), lambda bb, g: (0, g)),
                  pl.BlockSpec((1, LANE), lambda bb, g: (0, g))],
        out_specs=out_specs,
        out_shape=out_shape,
        compiler_params=_cparams("parallel", "parallel"),
        name="hyena_short_conv",
    )(hy, wpad, bvec.reshape(1, -1))


def _hyp_filters(length, lp):
    feats = _hy_pos_features(length)
    feats = feats.reshape(length // 2, 2, HY_EMB).transpose(1, 0, 2).reshape(length, HY_EMB)
    n_out = HY_ORDER * 2 * HY_WIDTH
    tl = min(length // 2, 512)
    full = lambda a: pl.BlockSpec(a.shape, lambda i: (0,) * a.ndim)
    args = (lp['hy_w1'], lp['hy_b1'].reshape(1, -1), lp['hy_w2'], lp['hy_b2'].reshape(1, -1), lp['hy_w3'],
            lp['hy_decay'].reshape(1, n_out))
    filt, filt_bf = pl.pallas_call(
        _hy_filter_kernel,
        grid=(length // tl,),
        in_specs=[pl.BlockSpec((tl, HY_EMB), lambda i: (i, 0))] + [full(a) for a in args],
        out_specs=[pl.BlockSpec((tl, n_out), lambda i: (i, 0))] * 2,
        out_shape=[jax.ShapeDtypeStruct((length, n_out), F32), jax.ShapeDtypeStruct((length, n_out), BF16)],
        compiler_params=_cparams("parallel"),
        name="hyena_filter_mlp",
    )(feats, *args)
    return filt.reshape(2, length // 2, n_out), filt_bf.reshape(2, length // 2, n_out)


def _hyp_matrix_kernel(c1_ref, s1_ref, c2_ref, s2_ref, ck_ref, sk_ref, ckr_ref, skr_ref,
                       ce_o, se_o, co_o, so_o, set_o, cot_o, sot_o):
    c1, s1 = c1_ref[0], s1_ref[0]
    c2, s2 = c2_ref[...], s2_ref[...]
    cos = c1 * c2 - s1 * s2
    sin = s1 * c2 + c1 * s2
    row = lax.broadcasted_iota(I32, cos.shape, 0)
    col = lax.broadcasted_iota(I32, cos.shape, 1)
    first_row = (row + pl.program_id(0) * DFT_ROW_STEP) == 0
    alt_col = jnp.where(col % 2 == 0, 1.0, -1.0)
    alt_row = jnp.where(row % 2 == 0, 1.0, -1.0)
    ck, sk = ck_ref[...], sk_ref[...]
    ce_o[...] = cos.astype(BF16)
    se_o[...] = jnp.where(first_row, alt_col, -sin).astype(BF16)
    co_o[...] = (cos * ck - sin * sk).astype(BF16)
    so_o[...] = jnp.where(first_row, alt_col, -(sin * ck + cos * sk)).astype(BF16)
    ckr, skr = ckr_ref[...], skr_ref[...]
    set_o[...] = jnp.where(col == 0, alt_row, -sin).astype(BF16)
    cot_o[...] = (cos * ckr - sin * skr).astype(BF16)
    sot_o[...] = jnp.where(col == 0, alt_row, -(sin * ckr + cos * skr)).astype(BF16)


def _hyp_matrices(length):
    h = length // 2
    step = DFT_ROW_STEP
    m = jnp.arange(h, dtype=I32)

    def trig(kv):
        ang = ((kv[:, None] * m[None, :]) % length).astype(F32) * (2.0 * math.pi / length)
        return jnp.cos(ang), jnp.sin(ang)

    c1, s1 = trig(jnp.arange(h // step, dtype=I32) * step)
    c2, s2 = trig(jnp.arange(step, dtype=I32))
    half_angle = m.astype(F32) * (math.pi / length)
    ck, sk = jnp.cos(half_angle), jnp.sin(half_angle)
    coarse = pl.BlockSpec((1, 1, h), lambda j: (j, 0, 0))
    fine = pl.BlockSpec((step, h), lambda j: (0, 0))
    per_row = pl.BlockSpec((step, 1), lambda j: (j, 0))
    per_col = pl.BlockSpec((1, h), lambda j: (0, 0))
    out = pl.BlockSpec((step, h), lambda j: (j, 0))
    ce, se, co, so, se_t, co_t, so_t = pl.pallas_call(
        _hyp_matrix_kernel,
        grid=(h // step,),
        in_specs=[coarse, coarse, fine, fine, per_row, per_row, per_col, per_col],
        out_specs=[out] * 7,
        out_shape=[jax.ShapeDtypeStruct((h, h), BF16)] * 7,
        compiler_params=_cparams("parallel"),
        name="dft_matrices",
    )(c1[:, None, :], s1[:, None, :], c2, s2, ck[:, None], sk[:, None], ck[None, :], sk[None, :])
    return dict(fwd=(ce, se, co, so), inv=(ce, se_t, co_t, so_t))


def _hyp_fwd_kernel(ce_ref, se_ref, co_ref, so_ref, xe_ref, xo_ref, *rest, with_taps):
    xe, xo = xe_ref[...], xo_ref[...]
    a_re, a_im = _dot(ce_ref[...], xe), _dot(se_ref[...], xe)
    b_re, b_im = _dot(co_ref[...], xo), _dot(so_ref[...], xo)
    if not with_taps:
        for ref, val in zip(rest, (a_re, a_im, b_re, b_im)):
            ref[...] = val
        return
    t1re_ref, t1im_ref, t2re_ref, t2im_ref, sp_ref, gere_o, geim_o, gore_o, goim_o = rest
    t1re, t1im, t2re, t2im = t1re_ref[...], t1im_ref[...], t2re_ref[...], t2im_ref[...]
    sp = sp_ref[...]
    first_block = pl.program_id(1) == 0
    row8 = lax.broadcasted_iota(I32, (8, HY_WIDTH), 0)
    for bb in range(xe.shape[1] // HY_WIDTH):
        sl = slice(bb * HY_WIDTH, (bb + 1) * HY_WIDTH)
        are, aim, bre, bim = a_re[:, sl], a_im[:, sl], b_re[:, sl], b_im[:, sl]
        u1re, u1im = are + bre, aim + bim
        u2re, u2im = are - bre, bim - aim
        z1re, z1im = u1re * t1re - u1im * t1im, u1re * t1im + u1im * t1re
        z2re, z2im = u2re * t2re - u2im * t2im, u2re * t2im + u2im * t2re
        gere_o[:, sl] = (z1re + z2re).astype(BF16)
        geim_o[:, sl] = (z1im - z2im).astype(BF16)
        gore_o[:, sl] = (z1re - z2re).astype(BF16)
        goim_o[:, sl] = (z1im + z2im).astype(BF16)

        @pl.when(first_block)
        def _():
            u0, ul = u1re[0:8], u2re[0:8]
            a_s, b_s = aim[0:8], bim[0:8]
            dc, ny, mre, mim = sp[0:1], sp[1:2], sp[2:3], sp[3:4]
            first = row8 == 0
            gere_o[0:8, sl] = jnp.where(first, u0 * dc + ul * ny, (z1re + z2re)[0:8]).astype(BF16)
            gore_o[0:8, sl] = jnp.where(first, u0 * dc - ul * ny, (z1re - z2re)[0:8]).astype(BF16)
            geim_o[0:8, sl] = jnp.where(first, a_s * mre + b_s * mim, (z1im - z2im)[0:8]).astype(BF16)
            goim_o[0:8, sl] = jnp.where(first, b_s * mre - a_s * mim, (z1im + z2im)[0:8]).astype(BF16)


def _hyp_fwd(mats, x, plane, taps=None):
    _, _, h, nc = x.shape
    tk = min(h, 256)
    cb = _col_block(nc, 1024)
    grid = (nc // cb, h // tk)
    m_spec = pl.BlockSpec((tk, h), lambda c, j: (j, 0))
    x_spec = lambda r: pl.BlockSpec((None, None, h, cb), lambda c, j: (plane, r, 0, c))
    o_spec = pl.BlockSpec((tk, cb), lambda c, j: (j, c))
    if taps is None:
        return pl.pallas_call(
            functools.partial(_hyp_fwd_kernel, with_taps=False),
            grid=grid, in_specs=[m_spec] * 4 + [x_spec(0), x_spec(1)], out_specs=[o_spec] * 4,
            out_shape=[jax.ShapeDtypeStruct((h, nc), F32)] * 4,
            compiler_params=_cparams("parallel", "arbitrary"),
            name="hyena_dft_filters",
        )(*mats['fwd'], x, x)
    t_spec = pl.BlockSpec((tk, HY_WIDTH), lambda c, j: (j, 0))
    sp_spec = pl.BlockSpec((8, HY_WIDTH), lambda c, j: (0, 0))
    return pl.pallas_call(
        functools.partial(_hyp_fwd_kernel, with_taps=True),
        grid=grid, in_specs=[m_spec] * 4 + [x_spec(0), x_spec(1)] + [t_spec] * 4 + [sp_spec],
        out_specs=[o_spec] * 4,
        out_shape=[jax.ShapeDtypeStruct((h, nc), BF16)] * 4,
        compiler_params=_cparams("parallel", "arbitrary"),
        name="hyena_dft_forward",
    )(*mats['fwd'], x, x, *taps)


def _hyp_inv_kernel(ce_ref, set_ref, cot_ref, sot_ref, gere_ref, geim_ref, gore_ref, goim_ref,
                    gate_ref, prev_ref, bias_ref, *outs, last):
    conv = (_dot(ce_ref[...], gere_ref[...]) + _dot(set_ref[...], geim_ref[...]),
            _dot(cot_ref[...], gore_ref[...]) + _dot(sot_ref[...], goim_ref[...]))
    bias = bias_ref[...]
    for r in range(2):
        y = gate_ref[r] * (conv[r] + prev_ref[r] * bias)
        if last:
            (tok_o,) = outs
            for bb in range(y.shape[1] // HY_WIDTH):
                tok_o[bb, :, r * HY_WIDTH:(r + 1) * HY_WIDTH] = y[:, bb * HY_WIDTH:(bb + 1) * HY_WIDTH].astype(BF16)
        else:
            y_o, ybf_o = outs
            y_o[r] = y
            ybf_o[r] = y.astype(BF16)


def _hyp_inv(mats, g, gate, gate_plane, prev, prev_plane, bias_row, last):
    h, nc = g[0].shape
    tm = min(h, 256)
    cb = _col_block(nc, 512)
    grid = (nc // cb, h // tm)
    m_spec = pl.BlockSpec((tm, h), lambda c, i: (i, 0))
    g_spec = pl.BlockSpec((h, cb), lambda c, i: (0, c))
    e_spec = lambda plane: pl.BlockSpec((None, 2, tm, cb), lambda c, i: (plane, 0, i, c))
    b_spec = pl.BlockSpec((1, cb), lambda c, i: (0, c))
    if last:
        out_specs = [pl.BlockSpec((cb // HY_WIDTH, tm, 2 * HY_WIDTH), lambda c, i: (c, i, 0))]
        out_shape = [jax.ShapeDtypeStruct((nc // HY_WIDTH, h, 2 * HY_WIDTH), BF16)]
    else:
        out_specs = [e_spec(0), e_spec(0)]
        out_shape = [jax.ShapeDtypeStruct((1, 2, h, nc), F32), jax.ShapeDtypeStruct((1, 2, h, nc), BF16)]
    return pl.pallas_call(
        functools.partial(_hyp_inv_kernel, last=last),
        grid=grid, in_specs=[m_spec] * 4 + [g_spec] * 4 + [e_spec(gate_plane), e_spec(prev_plane), b_spec],
        out_specs=out_specs, out_shape=out_shape,
        compiler_params=_cparams("parallel", "arbitrary"),
        name="hyena_dft_inverse",
    )(*mats['inv'], *g, gate, prev, bias_row)


def _hyp_tap_tables(spec, filt, length):
    a_re, a_im, b_re, b_im = spec
    w = HY_WIDTH
    inv_n = 1.0 / (2 * length)
    tables = []
    for o in range(HY_ORDER):
        f_sl = slice((2 * o) * w, (2 * o + 1) * w)
        r_sl = slice((2 * o + 1) * w, (2 * o + 2) * w)
        hb0 = filt[0, 0:1, r_sl]
        f1re = (a_re + b_re)[:, f_sl] + (a_re + b_re)[:, r_sl] - hb0
        f1im = (a_im + b_im)[:, f_sl] - (a_im + b_im)[:, r_sl]
        f2re = (a_re - b_re)[:, f_sl] + (a_re - b_re)[:, r_sl] - hb0
        f2im = (b_im - a_im)[:, f_sl] - (b_im - a_im)[:, r_sl]
        dc = f1re[0:1]
        ny = f2re[0:1]
        mid_re = a_im[0:1, f_sl] + a_im[0:1, r_sl] - hb0
        mid_im = -b_im[0:1, f_sl] + b_im[0:1, r_sl]
        sp = jnp.concatenate([dc * inv_n, ny * inv_n, mid_re * (2 * inv_n), mid_im * (2 * inv_n),
                              jnp.zeros((4, w), F32)], axis=0)
        tables.append((f1re * (2 * inv_n), f1im * (2 * inv_n), f2re * (2 * inv_n), f2im * (2 * inv_n), sp))
    return tables


def _hyena_seq(mats, vx, vx_bf, lp, n_batch):
    h = vx.shape[2]
    length = 2 * h
    filt, filt_bf = _hyp_filters(length, lp)
    spec = _hyp_fwd(mats, filt_bf[None], 0)
    tables = _hyp_tap_tables(spec, filt, length)
    bias = lp['hy_bias'].astype(F32)
    y, y_bf = vx, vx_bf
    for o in range(HY_ORDER):
        g = _hyp_fwd(mats, y_bf, 0, tables[o])
        bias_row = jnp.tile(bias[o][None, :], (1, n_batch))
        last = o == HY_ORDER - 1
        res = _hyp_inv(mats, g, vx, o + 1, y, 0, bias_row, last)
        if last:
            return res[0].reshape(n_batch, length, HY_WIDTH)
        y, y_bf = res


def _lru_kernel(u_ref, g_ref, cw_ref, cb_ref, wa_ref, ba_ref, wx_ref, bx_ref, lam_ref, o_ref,
                pad_ref, y_ref, *, seq, ctx_len):
    tc = LRU_CHUNK
    halo = LRU_HALO
    width = LRU_WIDTH
    lat_off = halo
    ctx_off = 2 * halo + seq
    zero = jnp.zeros((halo, width), F32)
    pad_ref[0:halo, :] = zero
    pad_ref[lat_off:lat_off + seq, :] = u_ref[0, 0:seq, :]
    pad_ref[lat_off + seq:ctx_off, :] = zero
    pad_ref[ctx_off:ctx_off + ctx_len, :] = u_ref[0, seq:seq + ctx_len, :]
    pad_ref[ctx_off + ctx_len:ctx_off + ctx_len + halo, :] = zero
    row = lax.broadcasted_iota(I32, (tc, width), 0)
    n_win = tc + 2 * halo

    def chunk(pad_off, y_off, s, carry, d):
        wstart = pl.multiple_of(pad_off + s - halo, 8)
        win = pad_ref[pl.ds(wstart, n_win), :]
        cw = cw_ref[d]
        xc = cb_ref[d]
        for k in range(LRU_CONV):
            shift = (LRU_CONV - 1 - k) if d == 0 else -k
            rolled = win if shift == 0 else pltpu.roll(win, shift % n_win, axis=0)
            xc = xc + cw[k:k + 1] * rolled[halo:halo + tc]
        xb = xc.astype(BF16)
        r = _sigmoid(_dot(xb, wa_ref[d]) + ba_ref[d])
        gi = _sigmoid(_dot(xb, wx_ref[d]) + bx_ref[d])
        lam = lam_ref[d]
        softplus = jnp.maximum(-lam, 0.0) + jnp.log1p(jnp.exp(-jnp.abs(lam)))
        log_a = -LRU_C * r * softplus
        a = jnp.exp(log_a)
        bt = jnp.sqrt(-jnp.tanh(log_a) * (a * a + 1.0)) * (gi * xc)
        sft = 1
        while sft < tc:
            if d == 0:
                keep = row >= sft
                a_s = jnp.where(keep, pltpu.roll(a, sft, axis=0), 1.0)
                b_s = jnp.where(keep, pltpu.roll(bt, sft, axis=0), 0.0)
            else:
                keep = row < tc - sft
                a_s = jnp.where(keep, pltpu.roll(a, tc - sft, axis=0), 1.0)
                b_s = jnp.where(keep, pltpu.roll(bt, tc - sft, axis=0), 0.0)
            bt = a * b_s + bt
            a = a * a_s
            sft *= 2
        h = a * carry + bt
        yo = pl.multiple_of(y_off + s, 8)
        if d == 0:
            y_ref[pl.ds(yo, tc), :] = h
            return h[tc - 1:tc]
        y_ref[pl.ds(yo, tc), :] = y_ref[pl.ds(yo, tc), :] + h
        return h[0:1]

    n_lat = seq // tc
    n_ctx = ctx_len // tc
    for d in range(2):
        carry = jnp.zeros((1, width), F32)
        order = range(n_ctx) if d == 0 else range(n_ctx - 1, -1, -1)
        for c in order:
            carry = chunk(ctx_off, seq, c * tc, carry, d)

        def body(j, cr, d=d):
            jj = j if d == 0 else n_lat - 1 - j
            return chunk(lat_off, 0, jj * tc, cr, d)

        lax.fori_loop(0, n_lat, body, carry)
    o_ref[0] = (y_ref[...] * _gelu_tanh(g_ref[0])).astype(BF16)


def _block_diag(w):
    nd, nb, c, _ = w.shape
    out = jnp.zeros((nd, nb * c, nb * c), w.dtype)
    for n in range(nb):
        out = out.at[:, n * c:(n + 1) * c, n * c:(n + 1) * c].set(w[:, n])
    return out


def _lru_mixer(lu, lg, lp, seq):
    b, nt, w = lu.shape
    ctx_len = nt - seq
    row3 = lambda a: a.reshape(2, 1, w)
    args = (_pad_to(lp['lru_conv_w'], 8, 1), row3(lp['lru_conv_b']), _block_diag(lp['lru_wa']).astype(BF16),
            row3(lp['lru_ba']), _block_diag(lp['lru_wx']).astype(BF16), row3(lp['lru_bx']), row3(lp['lru_lambda']))
    full = lambda a: pl.BlockSpec(a.shape, lambda bb: (0,) * a.ndim)
    tok = pl.BlockSpec((1, nt, w), lambda bb: (bb, 0, 0))
    return pl.pallas_call(
        functools.partial(_lru_kernel, seq=seq, ctx_len=ctx_len),
        grid=(b,),
        in_specs=[tok, tok] + [full(a) for a in args],
        out_specs=tok,
        out_shape=jax.ShapeDtypeStruct((b, nt, w), BF16),
        scratch_shapes=[pltpu.VMEM((nt + 3 * LRU_HALO, w), F32), pltpu.VMEM((nt, w), F32)],
        compiler_params=_cparams("parallel"),
        name="rglru_scan",
    )(lu, lg, *args)


def _merge_kernel(xl_ref, xc_ref, mod_ref, g1_ref, a_ref, b_ref, c_ref, d_ref, wg_ref, wa_ref, wb_ref, wc_ref,
                  wd_ref, wo_ref, o_ref, *, n_lat_tiles):
    group, tm, dm = xl_ref.shape
    m = mod_ref[:, 0]
    x = _stream_tile(xl_ref, xc_ref, n_lat_tiles)
    h = _normmod(x, g1_ref[...], m[:, 1:2], m[:, 0:1]).reshape(group * tm, dm).astype(BF16)
    acc = None
    for k, (br, w) in enumerate(((a_ref, wa_ref), (b_ref, wb_ref), (c_ref, wc_ref), (d_ref, wd_ref))):
        term = (1.0 + jnp.tanh(_dot(h, wg_ref[:, k * dm:(k + 1) * dm]))) * _dot(
            br[...].reshape(group * tm, br.shape[-1]), w[...])
        acc = term if acc is None else acc + term
    y = _dot(acc.astype(BF16), wo_ref[...])
    o_ref[...] = x + m[:, 2:3] * y.reshape(group, tm, dm)


def _merge(stream, modtab, g1, branches, w_gate, lp, n_lat_tiles, tiles_used):
    b, _, d = stream[0].shape
    wbr = lp['w_branch']
    head_rows = lambda w, dv: jnp.concatenate(
        [_pad_to(w[hd * dv:(hd + 1) * dv], LANE, 0) for hd in range(4)], axis=0)
    wbr = 0.5 * wbr
    weights = (head_rows(wbr[0], MLA_V).astype(BF16), head_rows(wbr[1], NA_HEAD_DIM).astype(BF16),
               wbr[2].astype(BF16), wbr[3].astype(BF16), lp['w_out'].astype(BF16))
    group = _batch_group(b)
    full = lambda a: pl.BlockSpec(a.shape, lambda bb, i: (0,) * a.ndim)
    tok = lambda n: pl.BlockSpec((group, TM, n), lambda bb, i: (bb, i, 0))
    return pl.pallas_call(
        functools.partial(_merge_kernel, n_lat_tiles=n_lat_tiles),
        grid=(b // group, tiles_used),
        in_specs=_stream_specs(stream, group, n_lat_tiles)
                 + [pl.BlockSpec((group, 1, 8, d), _kind_map(n_lat_tiles)), full(g1)]
                 + [tok(br.shape[-1]) for br in branches] + [full(w_gate)] + [full(w) for w in weights],
        out_specs=tok(d),
        out_shape=jax.ShapeDtypeStruct((b, tiles_used * TM, d), F32),
        compiler_params=_cparams("parallel", "parallel"),
        name="merge_branches",
    )(stream[0], stream[1], modtab, g1, *branches, w_gate, *weights)


U32 = jnp.uint32
EXPERT_ROWS = 512


def _pack_pair(x):
    n = x.shape[-1] // 2
    hi = lax.bitcast_convert_type(x[:, :n].astype(BF16).astype(F32), U32)
    lo = lax.bitcast_convert_type(x[:, n:].astype(BF16).astype(F32), U32)
    return hi | (lo >> 16)


def _unpack_pair(p):
    hi = lax.bitcast_convert_type(p & jnp.uint32(0xFFFF0000), F32)
    lo = lax.bitcast_convert_type(p << 16, F32)
    return hi, lo


def _router_kernel(x_ref, mod_ref, g2_ref, rw_ref, rb_ref, tri_ref, h2_o, idx_o, wts_o, rank_o, cnt_o, carry):
    i = pl.program_id(0)

    @pl.when(i == 0)
    def _():
        carry[...] = jnp.zeros_like(carry)

    m = mod_ref[0, 0]
    h2 = _normmod(x_ref[0], g2_ref[...], m[4:5], m[3:4])
    half = h2.shape[-1] // 2
    h2_o[0] = _pack_pair(h2[:, :half])
    h2_o[1] = _pack_pair(h2[:, half:])
    logits = lax.dot_general(rw_ref[...], h2, (((1,), (1,)), ((), ())), preferred_element_type=F32,
                             precision=lax.Precision.HIGHEST)
    scores = _sigmoid(logits)
    biased = scores + rb_ref[...]
    expert = lax.broadcasted_iota(I32, scores.shape, 0)
    picks = []
    onehot_all = jnp.zeros(scores.shape, F32)
    for _ in range(TOP_K):
        best = jnp.max(biased, axis=0, keepdims=True)
        arg = jnp.min(jnp.where(biased == best, expert, N_EXPERTS), axis=0, keepdims=True)
        hit = expert == arg
        sel = jnp.sum(jnp.where(hit, scores, 0.0), axis=0, keepdims=True)
        biased = jnp.where(hit, -jnp.inf, biased)
        onehot_all = onehot_all + jnp.where(hit, 1.0, 0.0)
        picks.append((arg, hit, sel))
    total = picks[0][2]
    for _, _, sel in picks[1:]:
        total = total + sel
    earlier = _dot(onehot_all.astype(BF16), tri_ref[...]) + carry[...]
    pad_rows = TOPK_PAD - TOP_K
    ranks = [jnp.sum(jnp.where(hit, earlier, 0.0), axis=0, keepdims=True).astype(I32) for _, hit, _ in picks]
    scale = ROUTED_SCALE / total
    idx_o[...] = jnp.concatenate([arg for arg, _, _ in picks] + [jnp.zeros((pad_rows, TM), I32)], axis=0)
    wts_o[...] = jnp.concatenate([sel * scale for _, _, sel in picks] + [jnp.zeros((pad_rows, TM), F32)], axis=0)
    rank_o[...] = jnp.concatenate(ranks + [jnp.zeros((pad_rows, TM), I32)], axis=0)
    carry[...] = carry[...] + jnp.sum(onehot_all, axis=1, keepdims=True)
    cnt_o[...] = carry[...]


def _tile_maps(tiles_used, n_lat_tiles):
    tok = lambda i: (i // tiles_used, i % tiles_used, 0)
    mod = lambda i: (i // tiles_used, jnp.where(i % tiles_used >= n_lat_tiles, 1, 0), 0, 0)
    return tok, mod


def _route(x1, modtab, g2, lp, n_lat_tiles, tiles_used):
    b, _, d = x1.shape
    n_tiles = b * tiles_used
    t = n_tiles * TM
    rw = lp['router_w'].T
    rb = lp['router_bias'].reshape(-1, 1)
    tri = (np.arange(TM)[:, None] < np.arange(TM)[None, :]).astype(np.float32)
    tri = jnp.asarray(tri, BF16)
    per_tok = lambda: pl.BlockSpec((TOPK_PAD, TM), lambda i: (0, i))
    full = lambda a: pl.BlockSpec(a.shape, lambda i: (0,) * a.ndim)
    tok, mod = _tile_maps(tiles_used, n_lat_tiles)
    return pl.pallas_call(
        _router_kernel,
        grid=(n_tiles,),
        in_specs=[pl.BlockSpec((1, TM, d), tok), pl.BlockSpec((1, 1, 8, d), mod),
                  full(g2), full(rw), full(rb), full(tri)],
        out_specs=[pl.BlockSpec((2, TM, d // 4), lambda i: (0, i, 0)), per_tok(), per_tok(), per_tok(),
                   pl.BlockSpec((N_EXPERTS, 1), lambda i: (0, 0))],
        out_shape=[jax.ShapeDtypeStruct((2, t, d // 4), U32), jax.ShapeDtypeStruct((TOPK_PAD, t), I32),
                   jax.ShapeDtypeStruct((TOPK_PAD, t), F32), jax.ShapeDtypeStruct((TOPK_PAD, t), I32),
                   jax.ShapeDtypeStruct((N_EXPERTS, 1), F32)],
        scratch_shapes=[pltpu.VMEM((N_EXPERTS, 1), F32)],
        compiler_params=_cparams("arbitrary"),
        name="moe_router",
    )(x1, modtab, g2, rw, rb, tri)


SC_WINDOW = 128


def _sc_mesh():
    return plsc.VectorSubcoreMesh(core_axis_name="c", subcore_axis_name="s")


def _sc_scatter_rows(src, idx, n_out):
    n, width = src.shape
    k_rep = idx.shape[0]
    half = n // SC_WINDOW // 2

    @functools.partial(pl.kernel, out_type=jax.ShapeDtypeStruct((n_out, width), src.dtype), mesh=_sc_mesh(),
                       scratch_types=[], name="moe_dispatch_sc")
    def scatter(src_hbm, idx_hbm, out_hbm):
        def body(x_vmem, *i_vmems):
            for i_vmem in i_vmems:
                pltpu.sync_copy(x_vmem, out_hbm.at[i_vmem.at[0]])

        pltpu.emit_pipeline(
            body,
            grid=(2, half),
            in_specs=[pl.BlockSpec((SC_WINDOW, width), lambda a, i: (a * half + i, 0))]
                     + [pl.BlockSpec((1, SC_WINDOW), lambda a, i, k=k: (k, a * half + i)) for k in range(k_rep)],
            out_specs=[],
            core_axis_name=("c", "s"),
            dimension_semantics=(pltpu.PARALLEL, pltpu.PARALLEL),
        )(src_hbm, *([idx_hbm] * k_rep))

    return scatter(src, idx)


def _sc_gather_rows(src, idx):
    k_rep, n = idx.shape
    width = src.shape[1]
    n_win = n // SC_WINDOW

    @functools.partial(pl.kernel, out_type=jax.ShapeDtypeStruct((k_rep * n, width), src.dtype), mesh=_sc_mesh(),
                       scratch_types=[], name="moe_gather_sc")
    def gather(src_hbm, idx_hbm, out_hbm):
        def body(i_vmem, o_vmem):
            pltpu.sync_copy(src_hbm.at[i_vmem.at[0]], o_vmem)

        pltpu.emit_pipeline(
            body,
            grid=(k_rep, n_win),
            in_specs=[pl.BlockSpec((1, SC_WINDOW), lambda k, i: (k, i))],
            out_specs=[pl.BlockSpec((SC_WINDOW, width), lambda k, i: (k * n_win + i, 0))],
            core_axis_name=("c", "s"),
            dimension_semantics=(pltpu.PARALLEL, pltpu.PARALLEL),
        )(idx_hbm, out_hbm)

    return gather(src, idx)


def _unpack_planes(p0, p1):
    return _unpack_pair(p0) + _unpack_pair(p1)


def _dot_quarters(parts, w_ref):
    q = parts[0].shape[-1]
    acc = None
    for j, part in enumerate(parts):
        term = _dot(part.astype(BF16), w_ref[j * q:(j + 1) * q, :])
        acc = term if acc is None else acc + term
    return acc


def _expert_kernel(be_ref, nv_ref, xs_ref, wg_ref, wu_ref, wd_ref, ys_o, wg_s, wu_s, wd_s):
    i = pl.program_id(0)
    prev = be_ref[jnp.maximum(i - 1, 0)]

    @pl.when((i == 0) | (be_ref[i] != prev))
    def _():
        wg_s[...] = (0.5 * wg_ref[0]).astype(BF16)
        wu_s[...] = wu_ref[0].astype(BF16)
        wd_s[...] = wd_ref[0].astype(BF16)

    @pl.when(nv_ref[i] > 0)
    def _():
        keep = lax.broadcasted_iota(I32, xs_ref.shape[1:], 0) < nv_ref[i]
        parts = _unpack_planes(jnp.where(keep, xs_ref[0], jnp.uint32(0)), jnp.where(keep, xs_ref[1], jnp.uint32(0)))
        hid = _half_silu(_dot_quarters(parts, wg_s)) * _dot_quarters(parts, wu_s)
        y = _dot(hid.astype(BF16), wd_s[...])
        half = y.shape[-1] // 2
        ys_o[0] = _pack_pair(y[:, :half])
        ys_o[1] = _pack_pair(y[:, half:])

    @pl.when(nv_ref[i] <= 0)
    def _():
        ys_o[...] = jnp.zeros_like(ys_o)


def _experts(xs, block_e, n_valid, weights, layer):
    _, n_rows, dq = xs.shape
    d = 4 * dq
    n_blocks = n_rows // EXPERT_ROWS
    hid = EXPERT_HIDDEN
    grid_spec = pltpu.PrefetchScalarGridSpec(
        num_scalar_prefetch=2,
        grid=(n_blocks,),
        in_specs=[pl.BlockSpec((2, EXPERT_ROWS, dq), lambda i, be, nv: (0, i, 0)),
                  pl.BlockSpec((None, 1, d, hid), lambda i, be, nv: (layer, be[i], 0, 0)),
                  pl.BlockSpec((None, 1, d, hid), lambda i, be, nv: (layer, be[i], 0, 0)),
                  pl.BlockSpec((None, 1, hid, d), lambda i, be, nv: (layer, be[i], 0, 0))],
        out_specs=pl.BlockSpec((2, EXPERT_ROWS, dq), lambda i, be, nv: (0, i, 0)),
        scratch_shapes=[pltpu.VMEM((d, hid), BF16), pltpu.VMEM((d, hid), BF16), pltpu.VMEM((hid, d), BF16)],
    )
    return pl.pallas_call(
        _expert_kernel,
        grid_spec=grid_spec,
        out_shape=jax.ShapeDtypeStruct((2, n_rows, dq), U32),
        compiler_params=_cparams("arbitrary"),
        name="moe_experts",
    )(block_e, n_valid, xs, *weights)


def _combine_kernel(g_ref, wts_ref, h2_ref, x_ref, mod_ref, sg_ref, su_ref, sd_ref, gf_ref, o_ref, *, final):
    parts = _unpack_planes(h2_ref[0], h2_ref[1])
    hid = _half_silu(_dot_quarters(parts, sg_ref)) * _dot_quarters(parts, su_ref)
    shared = _dot(hid.astype(BF16), sd_ref[...])
    wts = wts_ref[...]
    routed = None
    for k in range(TOP_K):
        w = wts[:, k:k + 1]
        terms = [w * part for part in _unpack_planes(g_ref[k, 0], g_ref[k, 1])]
        routed = terms if routed is None else [r + t for r, t in zip(routed, terms)]
    f = shared + jnp.concatenate(routed, axis=1)
    m = mod_ref[0, 0]
    x2 = x_ref[0] + m[5:6] * f
    if final:
        x2 = x2 * lax.rsqrt(jnp.mean(x2 * x2, axis=-1, keepdims=True) + NORM_EPS) * gf_ref[...]
    o_ref[0] = x2


def _combine(g, wts, h2p, x1, modtab, lp, g_final, n_lat_tiles, tiles_used, final):
    b, _, d = x1.shape
    dq = d // 4
    weights = ((0.5 * lp['sh_w_gate']).astype(BF16), lp['sh_w_up'].astype(BF16), lp['sh_w_down'].astype(BF16),
               g_final.reshape(1, -1))
    full = lambda a: pl.BlockSpec(a.shape, lambda i: (0,) * a.ndim)
    tok, mod = _tile_maps(tiles_used, n_lat_tiles)
    return pl.pallas_call(
        functools.partial(_combine_kernel, final=final),
        grid=(b * tiles_used,),
        in_specs=[pl.BlockSpec((TOP_K, 2, TM, dq), lambda i: (0, 0, i, 0)),
                  pl.BlockSpec((TM, TOPK_PAD), lambda i: (i, 0)),
                  pl.BlockSpec((2, TM, dq), lambda i: (0, i, 0)),
                  pl.BlockSpec((1, TM, d), tok), pl.BlockSpec((1, 1, 8, d), mod)]
                 + [full(w) for w in weights],
        out_specs=pl.BlockSpec((1, TM, d), tok),
        out_shape=jax.ShapeDtypeStruct((b, tiles_used * TM, d), F32),
        compiler_params=_cparams("parallel"),
        name="moe_combine",
    )(g, wts, h2p, x1, modtab, *weights)


def _sorted_rows_kernel(idx_ref, rank_ref, ps_ref, o_ref):
    expert = lax.broadcasted_iota(I32, (N_EXPERTS, idx_ref.shape[1]), 0)
    starts = ps_ref[...]
    idx = idx_ref[...]
    rows = [jnp.sum(jnp.where(idx[k:k + 1] == expert, starts, 0), axis=0, keepdims=True) for k in range(TOP_K)]
    pad = jnp.zeros((TOPK_PAD - TOP_K, idx.shape[1]), I32)
    o_ref[...] = jnp.concatenate(rows + [pad], axis=0) + rank_ref[...]


def _sorted_rows(idx, rank, p_starts):
    t = idx.shape[1]
    per_tok = pl.BlockSpec((TOPK_PAD, TM), lambda i: (0, i))
    return pl.pallas_call(
        _sorted_rows_kernel,
        grid=(t // TM,),
        in_specs=[per_tok, per_tok, pl.BlockSpec((N_EXPERTS, 1), lambda i: (0, 0))],
        out_specs=per_tok,
        out_shape=jax.ShapeDtypeStruct((TOPK_PAD, t), I32),
        compiler_params=_cparams("parallel"),
        name="moe_sorted_rows",
    )(idx, rank, p_starts.reshape(-1, 1))


def _moe(x1, modtab, g2, lp, g_final, n_lat_tiles, final):
    b, nt, d = x1.shape
    tiles_used = n_lat_tiles if final else nt // TM
    t = b * tiles_used * TM
    h2p, idx, wts, rank, cnt = _route(x1, modtab, g2, lp, n_lat_tiles, tiles_used)
    counts = cnt[:, 0].astype(I32)
    padded = (counts + EXPERT_ROWS - 1) // EXPERT_ROWS * EXPERT_ROWS
    p_ends = jnp.cumsum(padded)
    p_starts = p_ends - padded
    n_blocks = (t * TOP_K + N_EXPERTS * (EXPERT_ROWS - 1)) // EXPERT_ROWS
    n_rows = n_blocks * EXPERT_ROWS
    dest = _sorted_rows(idx, rank, p_starts)[:TOP_K]
    plane_idx = jnp.concatenate([dest, dest + n_rows], axis=1)
    blk_start = jnp.arange(n_blocks, dtype=I32) * EXPERT_ROWS
    block_e = jnp.minimum(jnp.sum((p_ends[None, :] <= blk_start[:, None]).astype(I32), axis=1), N_EXPERTS - 1)
    n_valid = jnp.clip((p_starts + counts)[block_e] - blk_start, 0, EXPERT_ROWS).astype(I32)
    dq = d // 4
    xs = _sc_scatter_rows(h2p.reshape(2 * t, dq), plane_idx, 2 * n_rows).reshape(2, n_rows, dq)
    ys = _experts(xs, block_e, n_valid, lp['expert_stacks'], lp['layer'])
    g = _sc_gather_rows(ys.reshape(2 * n_rows, dq), plane_idx).reshape(TOP_K, 2, t, dq)
    return _combine(g, wts.T, h2p, x1, modtab, lp, g_final, n_lat_tiles, tiles_used, final)


def _layer(stream, nt, c, c_ctx, lp, consts, g_final, seq, final):
    b = stream[0].shape[0]
    n_lat_tiles = seq // TM
    rope, mats_lat, mats_ctx = consts
    modtab = _mod_table(c, c_ctx, *lp['mod_stacks'], lp['layer'])
    g1 = lp['g_norm1'].reshape(1, -1)
    g2 = lp['g_norm2'].reshape(1, -1)
    pw = _proj_weights(lp)
    q, k, v, nq, nk, nv, hy, lu, lg = _project(stream, nt, modtab, g1, pw, rope, n_lat_tiles)
    tiles_used = n_lat_tiles if final else nt // TM
    br_a = _mla_attention(q, k, v, seq, tiles_used)
    br_b = _na_attention(nq, nk, nv, _na_bias_tables(lp['na_rpb'], seq // GRID_W), seq, tiles_used)
    pre = _hyp_pre(hy, lp['hy_short_w'], lp['hy_short_b'], seq)
    lat_f, lat_bf, ctx_f, ctx_bf = pre
    br_c = _hyena_seq(mats_lat, lat_f, lat_bf, lp, b)
    if not final:
        br_c = jnp.concatenate([br_c, _hyena_seq(mats_ctx, ctx_f, ctx_bf, lp, b)], axis=1)
    br_d = _lru_mixer(lu, lg, lp, seq)
    x1 = _merge(stream, modtab, g1, (br_a, br_b, br_c, br_d), pw['w_gate'], lp, n_lat_tiles, tiles_used)
    return _moe(x1, modtab, g2, lp, g_final, n_lat_tiles, final)


_LAYER_KEYS = ('w_mod', 'b_mod', 'g_norm1', 'g_norm2', 'w_in', 'mla_g_q', 'mla_w_uq', 'mla_g_kv', 'mla_w_ukv',
               'na_rpb', 'hy_short_w', 'hy_short_b', 'hy_w1', 'hy_b1', 'hy_w2', 'hy_b2', 'hy_w3', 'hy_decay',
               'hy_bias', 'lru_conv_w', 'lru_conv_b', 'lru_wa', 'lru_ba', 'lru_wx', 'lru_bx', 'lru_lambda',
               'w_branch', 'w_out', 'router_w', 'router_bias', 'exp_w_gate', 'exp_w_up', 'exp_w_down',
               'sh_w_gate', 'sh_w_up', 'sh_w_down')


def kernel(x, c, ctx, c_ctx, w_mod, b_mod, g_norm1, g_norm2, w_in, mla_g_q, mla_w_uq, mla_g_kv, mla_w_ukv, na_rpb, hy_short_w, hy_short_b, hy_w1, hy_b1, hy_w2, hy_b2, hy_w3, hy_decay, hy_bias, lru_conv_w, lru_conv_b, lru_wa, lru_ba, lru_wx, lru_bx, lru_lambda, w_branch, w_out, router_w, router_bias, exp_w_gate, exp_w_up, exp_w_down, sh_w_gate, sh_w_up, sh_w_down, g_final):
    stacked = dict(zip(_LAYER_KEYS, (w_mod, b_mod, g_norm1, g_norm2, w_in, mla_g_q, mla_w_uq, mla_g_kv, mla_w_ukv,
                                     na_rpb, hy_short_w, hy_short_b, hy_w1, hy_b1, hy_w2, hy_b2, hy_w3, hy_decay,
                                     hy_bias, lru_conv_w, lru_conv_b, lru_wa, lru_ba, lru_wx, lru_bx, lru_lambda,
                                     w_branch, w_out, router_w, router_bias, exp_w_gate, exp_w_up, exp_w_down,
                                     sh_w_gate, sh_w_up, sh_w_down)))
    b, seq, d = x.shape
    ctx_len = ctx.shape[1]
    depth = w_mod.shape[0]
    assert seq % TM == 0 and ctx_len % TM == 0 and seq // GRID_W >= NA_KEY_ROWS + 1
    nt = seq + ctx_len
    stream = (x, ctx, 0)
    consts = (_rope_tables(seq, seq + ctx_len), _hyp_matrices(seq), _hyp_matrices(ctx_len))
    for i in range(depth):
        big = ('w_mod', 'b_mod', 'exp_w_gate', 'exp_w_up', 'exp_w_down')
        lp = {name: w[i] for name, w in stacked.items() if name not in big}
        lp['layer'] = i
        lp['mod_stacks'] = (w_mod, b_mod)
        lp['expert_stacks'] = (exp_w_gate, exp_w_up, exp_w_down)
        xa = _layer(stream, nt, c, c_ctx, lp, consts, g_final, seq, i == depth - 1)
        stream = (xa, xa, seq // TM)
    return xa
```

```python
import functools
import math

import numpy as np
import jax
import jax.numpy as jnp
from jax import lax
from jax.experimental import pallas as pl
from jax.experimental.pallas import tpu as pltpu
from jax.experimental.pallas import tpu_sc as plsc

F32 = jnp.float32
BF16 = jnp.bfloat16
I32 = jnp.int32

TM = 256
LANE = 128
GRID_W = 64
N_MOD = 6
NORM_EPS = 1e-6

MLA_HEADS, MLA_NOPE, MLA_ROPE, MLA_V = 4, 64, 32, 64
MLA_Q_RANK, MLA_KV_RANK = 192, 128
MLA_Q_PAD = 256
ROPE_THETA = 10000.0

NA_HEADS, NA_HEAD_DIM, NA_WIN_R, NA_WIN_C = 4, 64, 8, 16
NA_TILE_ROWS = TM // GRID_W
NA_KEY_ROWS = NA_TILE_ROWS + NA_WIN_R - 1
NA_KEYS = NA_KEY_ROWS * GRID_W

HY_WIDTH, HY_ORDER, HY_SHORT, HY_BANDS, HY_FFN = 256, 2, 3, 16, 64
HY_EMB = 2 * HY_BANDS + 1

LRU_WIDTH, LRU_BLOCKS, LRU_CONV, LRU_C = 256, 4, 4, 8.0
LRU_CHUNK = 256
LRU_HALO = 8

N_EXPERTS, TOP_K, EXPERT_HIDDEN, ROUTED_SCALE, MOE_BLOCK = 64, 6, 256, 2.5, 256
TOPK_PAD = 8

VMEM_LIMIT = 52 * 1024 * 1024


def _cparams(*sem):
    return pltpu.CompilerParams(dimension_semantics=sem, vmem_limit_bytes=VMEM_LIMIT)


def _dot(a, b):
    return jnp.dot(a, b, preferred_element_type=F32)


def _dot_nt(a, b):
    return lax.dot_general(a, b, (((1,), (1,)), ((), ())), preferred_element_type=F32)


def _sigmoid(x):
    return jax.nn.sigmoid(x)


def _half_silu(g):
    return g * (1.0 + jnp.tanh(g))


def _silu(x):
    return x * _sigmoid(x)


def _gelu_tanh(x):
    return 0.5 * x * (1.0 + jnp.tanh(math.sqrt(2.0 / math.pi) * (x + 0.044715 * (x * x * x))))


def _normmod(x, g, scale, shift):
    y = x * lax.rsqrt(jnp.mean(x * x, axis=-1, keepdims=True) + NORM_EPS) * g
    return y * (1.0 + scale) + shift


def _mod_kernel(c_ref, w_ref, b_ref, o_ref):
    s = _silu(c_ref[...])
    o_ref[...] = _dot(s.astype(BF16), w_ref[...].astype(BF16)) + b_ref[...]


def _mod_table(c, c_ctx, w_mod, b_mod, layer):
    b, d = c.shape
    rows = 16
    cc = jnp.zeros((rows, d), F32).at[:b].set(c).at[b].set(c_ctx)
    tn = 1024
    mod = pl.pallas_call(
        _mod_kernel,
        grid=(N_MOD * d // tn,),
        in_specs=[pl.BlockSpec((rows, d), lambda j: (0, 0)),
                  pl.BlockSpec((None, d, tn), lambda j: (layer, 0, j)),
                  pl.BlockSpec((None, 1, tn), lambda j: (layer, 0, j))],
        out_specs=pl.BlockSpec((rows, tn), lambda j: (0, j)),
        out_shape=jax.ShapeDtypeStruct((rows, N_MOD * d), F32),
        compiler_params=_cparams("arbitrary"),
        name="mod_vectors",
    )(cc, w_mod, b_mod[:, None, :])
    lat = mod[:b].reshape(b, N_MOD, d)
    ctx = jnp.broadcast_to(mod[b].reshape(1, N_MOD, d), (b, N_MOD, d))
    tab = jnp.stack([lat, ctx], axis=1)
    return jnp.pad(tab, ((0, 0), (0, 0), (0, 8 - N_MOD), (0, 0)))


_C_QLAT, _C_KVLAT, _C_KR, _C_KRR, _C_NA, _C_HY, _C_LU, _C_LG, _C_END = (
    0, 256, 384, 512, 640, 640 + 3 * NA_HEADS * LANE, 640 + 1536 + 768, 640 + 1536 + 1024, 640 + 1536 + 1280)


def _stream_tile(lat_ref, ctx_ref, n_lat_tiles):
    return jnp.where(pl.program_id(1) < n_lat_tiles, lat_ref[...], ctx_ref[...])


def _stream_specs(stream, group, n_lat_tiles):
    lat_src, ctx_src, ctx_tile0 = stream
    d = lat_src.shape[-1]
    return [pl.BlockSpec((group, TM, d), lambda bb, i: (bb, jnp.minimum(i, n_lat_tiles - 1), 0)),
            pl.BlockSpec((group, TM, d), lambda bb, i: (bb, ctx_tile0 + jnp.maximum(i - n_lat_tiles, 0), 0))]


def _proj_kernel(xl_ref, xc_ref, mod_ref, g1_ref, w1_ref, gq_ref, gkv_ref, wuq_ref, wuqr_ref, wk_ref, wv_ref,
                 cos_ref, sin_ref, q_o, k_o, v_o, nq_o, nk_o, nv_o, hy_o, lu_o, lg_o, *, n_lat_tiles):
    group, tm, d = xl_ref.shape
    m = mod_ref[:, 0]
    x = _stream_tile(xl_ref, xc_ref, n_lat_tiles)
    h = _normmod(x, g1_ref[...], m[:, 1:2], m[:, 0:1]).reshape(group * tm, d)
    z = _dot(h.astype(BF16), w1_ref[...])
    qlat = z[:, _C_QLAT:_C_KVLAT]
    kvlat = z[:, _C_KVLAT:_C_KR]
    kr = z[:, _C_KR:_C_KRR]
    krr = z[:, _C_KRR:_C_NA]
    qn = qlat * lax.rsqrt(jnp.sum(qlat * qlat, axis=-1, keepdims=True) * (1.0 / MLA_Q_RANK) + NORM_EPS) * gq_ref[...]
    kvn = kvlat * lax.rsqrt(jnp.mean(kvlat * kvlat, axis=-1, keepdims=True) + NORM_EPS) * gkv_ref[...]
    qn = qn.astype(BF16)
    kvn = kvn.astype(BF16)
    q = _dot(qn, wuq_ref[...])
    qr = _dot(qn, wuqr_ref[...])
    kk = _dot(kvn, wk_ref[...])
    vv = _dot(kvn, wv_ref[...])
    cos = jnp.concatenate([cos_ref[...]] * group, axis=0)
    sin = jnp.concatenate([sin_ref[...]] * group, axis=0)
    krope = kr * cos + krr * sin
    den_lane = lax.broadcasted_iota(I32, cos.shape, 1) == SOFTMAX_DEN_LANE
    def put_heads(ref, hd, val):
        for g in range(group):
            ref[g, hd] = val[g * tm:(g + 1) * tm].astype(BF16)

    def put_pairs(ref, hd, val):
        off = (hd % 2) * LANE
        for g in range(group):
            ref[g, hd // 2, :, off:off + LANE] = val[g * tm:(g + 1) * tm].astype(BF16)

    for hd in range(MLA_HEADS):
        sl = slice(hd * LANE, (hd + 1) * LANE)
        put_heads(q_o, hd, q[:, sl] * cos + qr[:, sl] * sin)
        put_heads(k_o, hd, kk[:, sl] + krope)
        put_heads(v_o, hd, jnp.where(den_lane, 1.0, vv[:, sl]))
    for hd in range(NA_HEADS):
        for which, ref in enumerate((nq_o, nk_o, nv_o)):
            lo = _C_NA + (which * NA_HEADS + hd) * LANE
            blk = z[:, lo:lo + LANE]
            if which == 2:
                put_pairs(ref, hd, jnp.where(den_lane, 1.0, blk))
            else:
                put_heads(ref, hd, blk)
    hy_o[...] = z[:, _C_HY:_C_LU].reshape(group, tm, _C_LU - _C_HY)
    lu_o[...] = z[:, _C_LU:_C_LG].reshape(group, tm, _C_LG - _C_LU)
    lg_o[...] = z[:, _C_LG:_C_END].reshape(group, tm, _C_END - _C_LG)


def _pad_to(a, n, axis):
    pad = [(0, 0)] * a.ndim
    pad[axis] = (0, n - a.shape[axis])
    return jnp.pad(a, pad)


def _rot_cols(w):
    half = w.shape[-1] // 2
    return jnp.concatenate([-w[..., half:], w[..., :half]], axis=-1)


def _head_blocks(cols_per_head):
    out = []
    for pieces in cols_per_head:
        k = pieces[0][0].shape[0]
        blk = jnp.zeros((k, LANE), F32)
        for arr, off in pieces:
            blk = blk.at[:, off:off + arr.shape[1]].set(arr)
        out.append(blk)
    return jnp.concatenate(out, axis=1)


def _proj_weights(lp):
    w_in = lp['w_in']
    d = w_in.shape[0]
    o = 0
    parts = {}
    for name, n in (('q', MLA_Q_RANK), ('kv', MLA_KV_RANK), ('kr', MLA_ROPE), ('na', 3 * NA_HEADS * NA_HEAD_DIM),
                    ('hy', 3 * HY_WIDTH), ('lu', LRU_WIDTH), ('lg', LRU_WIDTH), ('gt', 4 * d)):
        parts[name] = w_in[:, o:o + n]
        o += n
    zeros = lambda n: jnp.zeros((d, n), F32)
    kr_blk = jnp.concatenate([zeros(MLA_NOPE), parts['kr'], zeros(LANE - MLA_NOPE - MLA_ROPE)], axis=1)
    krr_blk = jnp.concatenate([zeros(MLA_NOPE), _rot_cols(parts['kr']), zeros(LANE - MLA_NOPE - MLA_ROPE)], axis=1)
    na_scale = NA_HEAD_DIM ** -0.5
    na_cols = []
    for which in range(3):
        for hd in range(NA_HEADS):
            lo = (which * NA_HEADS + hd) * NA_HEAD_DIM
            blk = parts['na'][:, lo:lo + NA_HEAD_DIM] * (na_scale if which == 0 else 1.0)
            na_cols.append(_pad_to(blk, LANE, 1))
    w1 = jnp.concatenate([_pad_to(parts['q'], MLA_Q_PAD, 1), parts['kv'], kr_blk, krr_blk] + na_cols
                         + [parts['hy'], parts['lu'], parts['lg']], axis=1).astype(BF16)
    mla_scale = (MLA_NOPE + MLA_ROPE) ** -0.5
    wuq = _pad_to(lp['mla_w_uq'], MLA_Q_PAD, 0) * mla_scale
    dq = MLA_NOPE + MLA_ROPE
    wuq_main = _head_blocks([[(wuq[:, hd * dq:hd * dq + dq], 0)] for hd in range(MLA_HEADS)])
    wuq_rot = _head_blocks([[(_rot_cols(wuq[:, hd * dq + MLA_NOPE:hd * dq + dq]), MLA_NOPE)]
                            for hd in range(MLA_HEADS)])
    dkv = MLA_NOPE + MLA_V
    wukv = lp['mla_w_ukv']
    wk = _head_blocks([[(wukv[:, hd * dkv:hd * dkv + MLA_NOPE], 0)] for hd in range(MLA_HEADS)])
    wv = _head_blocks([[(wukv[:, hd * dkv + MLA_NOPE:hd * dkv + dkv], 0)] for hd in range(MLA_HEADS)])
    gq = _pad_to(lp['mla_g_q'].reshape(1, -1), MLA_Q_PAD, 1)
    gkv = lp['mla_g_kv'].reshape(1, -1)
    return dict(w1=w1, w_gate=(0.5 * parts['gt']).astype(BF16), gq=gq, gkv=gkv, wuq=wuq_main.astype(BF16),
                wuq_rot=wuq_rot.astype(BF16), wk=wk.astype(BF16), wv=wv.astype(BF16))


def _rope_tables(seq, n_tok):
    t = jnp.arange(seq, dtype=I32)
    row = (t // GRID_W).astype(F32)
    col = (t % GRID_W).astype(F32)
    n_axis = MLA_ROPE // 4
    inv_freq = ROPE_THETA ** (-jnp.arange(n_axis, dtype=F32) / n_axis)
    ang = jnp.concatenate([row[:, None] * inv_freq, col[:, None] * inv_freq], axis=-1)
    cos = jnp.concatenate([jnp.cos(ang), jnp.cos(ang)], axis=-1)
    sin = jnp.concatenate([jnp.sin(ang), jnp.sin(ang)], axis=-1)
    cos_t = jnp.ones((n_tok, LANE), F32).at[:seq, MLA_NOPE:MLA_NOPE + MLA_ROPE].set(cos)
    sin_t = jnp.zeros((n_tok, LANE), F32).at[:seq, MLA_NOPE:MLA_NOPE + MLA_ROPE].set(sin)
    return cos_t, sin_t


def _batch_group(b):
    return 2 if b % 2 == 0 else 1


def _kind_map(n_lat_tiles):
    return lambda b, i: (b, jnp.where(i >= n_lat_tiles, 1, 0), 0, 0)


def _project(stream, nt, modtab, g1, pw, rope, n_lat_tiles):
    b, _, d = stream[0].shape
    cos_t, sin_t = rope
    group = _batch_group(b)
    full = lambda a: pl.BlockSpec(a.shape, lambda bb, i: (0,) * a.ndim)
    head_out = lambda: pl.BlockSpec((group, MLA_HEADS, TM, LANE), lambda bb, i: (bb, 0, i, 0))
    tok_out = lambda n: pl.BlockSpec((group, TM, n), lambda bb, i: (bb, i, 0))
    head_shape = jax.ShapeDtypeStruct((b, MLA_HEADS, nt, LANE), BF16)
    pair_out = lambda: pl.BlockSpec((group, MLA_HEADS // 2, TM, 2 * LANE), lambda bb, i: (bb, 0, i, 0))
    pair_shape = jax.ShapeDtypeStruct((b, MLA_HEADS // 2, nt, 2 * LANE), BF16)
    tok_shape = lambda n: jax.ShapeDtypeStruct((b, nt, n), F32)
    weights = (g1, pw['w1'], pw['gq'], pw['gkv'], pw['wuq'], pw['wuq_rot'], pw['wk'], pw['wv'])
    return pl.pallas_call(
        functools.partial(_proj_kernel, n_lat_tiles=n_lat_tiles),
        grid=(b // group, nt // TM),
        in_specs=_stream_specs(stream, group, n_lat_tiles)
                 + [pl.BlockSpec((group, 1, 8, d), _kind_map(n_lat_tiles))]
                 + [full(w) for w in weights]
                 + [pl.BlockSpec((TM, LANE), lambda bb, i: (i, 0))] * 2,
        out_specs=[head_out(), head_out(), head_out(), head_out(), head_out(), pair_out(),
                   tok_out(3 * HY_WIDTH), tok_out(LRU_WIDTH), tok_out(LRU_WIDTH)],
        out_shape=[head_shape, head_shape, head_shape, head_shape, head_shape, pair_shape,
                   tok_shape(3 * HY_WIDTH), tok_shape(LRU_WIDTH), tok_shape(LRU_WIDTH)],
        compiler_params=_cparams("parallel", "parallel"),
        name="input_projection",
    )(stream[0], stream[1], modtab, *weights, cos_t, sin_t)


SOFTMAX_DEN_LANE = 64


def _softmax_pv(parts, lane_off=0):
    m = None
    for s, _ in parts:
        mm = jnp.max(s, axis=-1, keepdims=True)
        m = mm if m is None else jnp.maximum(m, mm)
    acc = None
    for s, v in parts:
        o = _dot(jnp.exp(s - m).astype(BF16), v)
        acc = o if acc is None else acc + o
    den = lane_off + SOFTMAX_DEN_LANE
    return acc[:, lane_off:lane_off + LANE] / acc[:, den:den + 1]


def _mla_kernel(q_ref, k_ref, v_ref, o_ref, *, seq, n_lat_tiles):
    i = pl.program_id(1)
    nt = k_ref.shape[2]

    def attend(lo, hi):
        for hd in range(MLA_HEADS):
            s = _dot_nt(q_ref[0, hd], k_ref[0, hd, lo:hi, :])
            o = _softmax_pv([(s, v_ref[0, hd, lo:hi, :])])
            o_ref[0, :, hd * LANE:(hd + 1) * LANE] = o.astype(BF16)

    @pl.when(i < n_lat_tiles)
    def _():
        attend(0, nt)

    @pl.when(i >= n_lat_tiles)
    def _():
        attend(seq, nt)


def _mla_attention(q, k, v, seq, tiles_used):
    b, h, nt, _ = q.shape
    kv_spec = pl.BlockSpec((1, h, nt, LANE), lambda bb, i: (bb, 0, 0, 0))
    return pl.pallas_call(
        functools.partial(_mla_kernel, seq=seq, n_lat_tiles=seq // TM),
        grid=(b, tiles_used),
        in_specs=[pl.BlockSpec((1, h, TM, LANE), lambda bb, i: (bb, 0, i, 0)), kv_spec, kv_spec],
        out_specs=pl.BlockSpec((1, TM, h * LANE), lambda bb, i: (bb, i, 0)),
        out_shape=jax.ShapeDtypeStruct((b, tiles_used * TM, h * LANE), BF16),
        compiler_params=_cparams("parallel", "arbitrary"),
        name="mla_attention",
    )(q, k, v)


def _na_bias_tables(rpb, rows):
    n_blk = rows // NA_TILE_ROWS
    col = np.arange(GRID_W)
    c0 = np.clip(col - NA_WIN_C // 2, 0, GRID_W - NA_WIN_C)
    in_win = (col[None, :] >= c0[:, None]) & (col[None, :] < c0[:, None] + NA_WIN_C)
    dc = np.clip(col[None, :] - col[:, None], 1 - NA_WIN_C, NA_WIN_C - 1) + NA_WIN_C - 1
    rpb = rpb.astype(F32)
    tables = []
    for j in (0, 1, n_blk - 1):
        w0 = min(max(NA_TILE_ROWS * j - NA_WIN_R // 2, 0), rows - NA_KEY_ROWS)
        r = NA_TILE_ROWS * j + np.arange(NA_TILE_ROWS)
        kr = w0 + np.arange(NA_KEY_ROWS)
        r0 = np.clip(r - NA_WIN_R // 2, 0, rows - NA_WIN_R)
        row_ok = (kr[None, :] >= r0[:, None]) & (kr[None, :] < r0[:, None] + NA_WIN_R)
        dr = np.clip(kr[None, :] - r[:, None] + NA_WIN_R - 1, 0, 2 * NA_WIN_R - 2)
        oh_r = jnp.asarray(np.eye(2 * NA_WIN_R - 1, dtype=np.float32)[dr.reshape(-1)])
        oh_c = jnp.asarray(np.eye(2 * NA_WIN_C - 1, dtype=np.float32)[dc.reshape(-1)])
        bias = jnp.einsum('ar,hrc,bc->hab', oh_r, rpb, oh_c, precision=lax.Precision.HIGHEST)
        bias = bias.reshape(NA_HEADS, NA_TILE_ROWS, NA_KEY_ROWS, GRID_W, GRID_W)
        mask = row_ok[:, :, None, None] & in_win[None, None, :, :]
        bias = jnp.where(jnp.asarray(mask)[None], bias, -jnp.inf)
        tables.append(bias.transpose(0, 1, 3, 2, 4).reshape(NA_HEADS, TM, NA_KEYS))
    return jnp.stack(tables)


def _na_kernel(q_ref, k_ref, v_ref, bias_ref, o_ref, *, seq, n_lat_tiles):
    i = pl.program_id(1)
    nt = k_ref.shape[2]
    rows = seq // GRID_W

    @pl.when(i < n_lat_tiles)
    def _():
        w0 = jnp.clip(NA_TILE_ROWS * i - NA_WIN_R // 2, 0, rows - NA_KEY_ROWS)
        start = pl.multiple_of(w0 * GRID_W, GRID_W)
        for hd in range(NA_HEADS):
            q = q_ref[0, hd]
            s_loc = _dot_nt(q, k_ref[0, hd, pl.ds(start, NA_KEYS), :]) + bias_ref[0, hd]
            s_ctx = _dot_nt(q, k_ref[0, hd, seq:nt, :])
            o = _softmax_pv([(s_loc, v_ref[0, hd // 2, pl.ds(start, NA_KEYS), :]),
                             (s_ctx, v_ref[0, hd // 2, seq:nt, :])], (hd % 2) * LANE)
            o_ref[0, :, hd * LANE:(hd + 1) * LANE] = o.astype(BF16)

    @pl.when(i >= n_lat_tiles)
    def _():
        for hd in range(NA_HEADS):
            s = _dot_nt(q_ref[0, hd], k_ref[0, hd, seq:nt, :])
            o = _softmax_pv([(s, v_ref[0, hd // 2, seq:nt, :])], (hd % 2) * LANE)
            o_ref[0, :, hd * LANE:(hd + 1) * LANE] = o.astype(BF16)


def _na_attention(q, k, v, bias, seq, tiles_used):
    b, h, nt, _ = q.shape
    n_lat = seq // TM
    kv_spec = pl.BlockSpec((1, h, nt, LANE), lambda bb, i: (bb, 0, 0, 0))
    cfg = lambda bb, i: (jnp.where(i == 0, 0, jnp.where(i >= n_lat - 1, 2, 1)), 0, 0, 0)
    return pl.pallas_call(
        functools.partial(_na_kernel, seq=seq, n_lat_tiles=n_lat),
        grid=(b, tiles_used),
        in_specs=[pl.BlockSpec((1, h, TM, LANE), lambda bb, i: (bb, 0, i, 0)), kv_spec,
                  pl.BlockSpec((1, h // 2, nt, 2 * LANE), lambda bb, i: (bb, 0, 0, 0)),
                  pl.BlockSpec((1, h, TM, NA_KEYS), cfg)],
        out_specs=pl.BlockSpec((1, TM, h * LANE), lambda bb, i: (bb, i, 0)),
        out_shape=jax.ShapeDtypeStruct((b, tiles_used * TM, h * LANE), BF16),
        compiler_params=_cparams("parallel", "arbitrary"),
        name="neighbourhood_attention",
    )(q, k, v, bias)


def _hp_dot(a, b):
    return jnp.dot(a, b, preferred_element_type=F32, precision=lax.Precision.HIGHEST)


def _hy_filter_kernel(f_ref, w1_ref, b1_ref, w2_ref, b2_ref, w3_ref, dec_ref, o_ref, obf_ref):
    f = f_ref[...]
    h = jnp.sin(_hp_dot(f, w1_ref[...]) + b1_ref[...])
    h = jnp.sin(_hp_dot(h, w2_ref[...]) + b2_ref[...])
    h = _hp_dot(h, w3_ref[...])
    h = h * jnp.exp(-f[:, 0:1] * jnp.abs(dec_ref[...]))
    o_ref[...] = h
    obf_ref[...] = h.astype(BF16)


def _hy_pos_features(length):
    t = jnp.linspace(0.0, 1.0, length, dtype=F32)[:, None]
    w = 2.0 * math.pi * jnp.arange(length, dtype=F32)[:, None] / length
    f = jnp.linspace(1e-4, HY_BANDS - 1, HY_BANDS, dtype=F32)[None, :]
    z = w * f
    return jnp.concatenate([t, jnp.cos(z), -jnp.sin(z)], axis=-1)


DFT_ROW_STEP = 64


def _col_block(nc, cap):
    return min(nc, cap)


def _hyp_pre_kernel(z_ref, w_ref, b_ref, lat_o, lat_bf_o, ctx_o, ctx_bf_o, *, seq, ctx_len):
    w = w_ref[...]
    bias = b_ref[...]
    for lo, length, o_ref, obf_ref in ((0, seq, lat_o, lat_bf_o), (seq, ctx_len, ctx_o, ctx_bf_o)):
        h = length // 2
        even = z_ref[0, pl.ds(lo, h, stride=2), :]
        odd = z_ref[0, pl.ds(lo + 1, h, stride=2), :]
        row = lax.broadcasted_iota(I32, even.shape, 0)
        odd_prev = jnp.where(row == 0, 0.0, pltpu.roll(odd, 1, axis=0))
        even_next = jnp.where(row == h - 1, 0.0, pltpu.roll(even, h - 1, axis=0))
        y_even = bias + w[0:1] * odd_prev + w[1:2] * even + w[2:3] * odd
        y_odd = bias + w[0:1] * even + w[1:2] * odd + w[2:3] * even_next
        for r, y in enumerate((y_even, y_odd)):
            o_ref[0, r] = y
            obf_ref[0, r] = y.astype(BF16)


def _hyp_pre(hy, w, bvec, seq):
    b, nt, _ = hy.shape
    ctx_len = nt - seq
    wpad = _pad_to(w, 8, 0)
    per_plane = HY_WIDTH // LANE
    out_specs, out_shape = [], []
    for length in (seq, ctx_len):
        for dt in (F32, BF16):
            out_specs.append(pl.BlockSpec((1, 2, length // 2, LANE),
                                          lambda bb, g: (g // per_plane, 0, 0, bb * per_plane + g % per_plane)))
            out_shape.append(jax.ShapeDtypeStruct((3, 2, length // 2, b * HY_WIDTH), dt))
    return pl.pallas_call(
        functools.partial(_hyp_pre_kernel, seq=seq, ctx_len=ctx_len),
        grid=(b, 3 * per_plane),
        in_specs=[pl.BlockSpec((1, nt, LANE), lambda bb, g: (bb, 0, g)),
                  pl.BlockSpec((8, LANE), lambda bb, g: (0, g)),
                  pl.BlockSpec((1, LANE), lambda bb, g: (0, g))],
        out_specs=out_specs,
        out_shape=out_shape,
        compiler_params=_cparams("parallel", "parallel"),
        name="hyena_short_conv",
    )(hy, wpad, bvec.reshape(1, -1))


def _hyp_filters(length, lp):
    feats = _hy_pos_features(length)
    feats = feats.reshape(length // 2, 2, HY_EMB).transpose(1, 0, 2).reshape(length, HY_EMB)
    n_out = HY_ORDER * 2 * HY_WIDTH
    tl = min(length // 2, 512)
    full = lambda a: pl.BlockSpec(a.shape, lambda i: (0,) * a.ndim)
    args = (lp['hy_w1'], lp['hy_b1'].reshape(1, -1), lp['hy_w2'], lp['hy_b2'].reshape(1, -1), lp['hy_w3'],
            lp['hy_decay'].reshape(1, n_out))
    filt, filt_bf = pl.pallas_call(
        _hy_filter_kernel,
        grid=(length // tl,),
        in_specs=[pl.BlockSpec((tl, HY_EMB), lambda i: (i, 0))] + [full(a) for a in args],
        out_specs=[pl.BlockSpec((tl, n_out), lambda i: (i, 0))] * 2,
        out_shape=[jax.ShapeDtypeStruct((length, n_out), F32), jax.ShapeDtypeStruct((length, n_out), BF16)],
        compiler_params=_cparams("parallel"),
        name="hyena_filter_mlp",
    )(feats, *args)
    return filt.reshape(2, length // 2, n_out), filt_bf.reshape(2, length // 2, n_out)


def _hyp_matrix_kernel(c1_ref, s1_ref, c2_ref, s2_ref, ck_ref, sk_ref, ckr_ref, skr_ref,
                       ce_o, se_o, co_o, so_o, set_o, cot_o, sot_o):
    c1, s1 = c1_ref[0], s1_ref[0]
    c2, s2 = c2_ref[...], s2_ref[...]
    cos = c1 * c2 - s1 * s2
    sin = s1 * c2 + c1 * s2
    row = lax.broadcasted_iota(I32, cos.shape, 0)
    col = lax.broadcasted_iota(I32, cos.shape, 1)
    first_row = (row + pl.program_id(0) * DFT_ROW_STEP) == 0
    alt_col = jnp.where(col % 2 == 0, 1.0, -1.0)
    alt_row = jnp.where(row % 2 == 0, 1.0, -1.0)
    ck, sk = ck_ref[...], sk_ref[...]
    ce_o[...] = cos.astype(BF16)
    se_o[...] = jnp.where(first_row, alt_col, -sin).astype(BF16)
    co_o[...] = (cos * ck - sin * sk).astype(BF16)
    so_o[...] = jnp.where(first_row, alt_col, -(sin * ck + cos * sk)).astype(BF16)
    ckr, skr = ckr_ref[...], skr_ref[...]
    set_o[...] = jnp.where(col == 0, alt_row, -sin).astype(BF16)
    cot_o[...] = (cos * ckr - sin * skr).astype(BF16)
    sot_o[...] = jnp.where(col == 0, alt_row, -(sin * ckr + cos * skr)).astype(BF16)


def _hyp_matrices(length):
    h = length // 2
    step = DFT_ROW_STEP
    m = jnp.arange(h, dtype=I32)

    def trig(kv):
        ang = ((kv[:, None] * m[None, :]) % length).astype(F32) * (2.0 * math.pi / length)
        return jnp.cos(ang), jnp.sin(ang)

    c1, s1 = trig(jnp.arange(h // step, dtype=I32) * step)
    c2, s2 = trig(jnp.arange(step, dtype=I32))
    half_angle = m.astype(F32) * (math.pi / length)
    ck, sk = jnp.cos(half_angle), jnp.sin(half_angle)
    coarse = pl.BlockSpec((1, 1, h), lambda j: (j, 0, 0))
    fine = pl.BlockSpec((step, h), lambda j: (0, 0))
    per_row = pl.BlockSpec((step, 1), lambda j: (j, 0))
    per_col = pl.BlockSpec((1, h), lambda j: (0, 0))
    out = pl.BlockSpec((step, h), lambda j: (j, 0))
    ce, se, co, so, se_t, co_t, so_t = pl.pallas_call(
        _hyp_matrix_kernel,
        grid=(h // step,),
        in_specs=[coarse, coarse, fine, fine, per_row, per_row, per_col, per_col],
        out_specs=[out] * 7,
        out_shape=[jax.ShapeDtypeStruct((h, h), BF16)] * 7,
        compiler_params=_cparams("parallel"),
        name="dft_matrices",
    )(c1[:, None, :], s1[:, None, :], c2, s2, ck[:, None], sk[:, None], ck[None, :], sk[None, :])
    return dict(fwd=(ce, se, co, so), inv=(ce, se_t, co_t, so_t))


def _hyp_fwd_kernel(ce_ref, se_ref, co_ref, so_ref, xe_ref, xo_ref, *rest, with_taps):
    xe, xo = xe_ref[...], xo_ref[...]
    a_re, a_im = _dot(ce_ref[...], xe), _dot(se_ref[...], xe)
    b_re, b_im = _dot(co_ref[...], xo), _dot(so_ref[...], xo)
    if not with_taps:
        for ref, val in zip(rest, (a_re, a_im, b_re, b_im)):
            ref[...] = val
        return
    t1re_ref, t1im_ref, t2re_ref, t2im_ref, sp_ref, gere_o, geim_o, gore_o, goim_o = rest
    t1re, t1im, t2re, t2im = t1re_ref[...], t1im_ref[...], t2re_ref[...], t2im_ref[...]
    sp = sp_ref[...]
    first_block = pl.program_id(1) == 0
    row8 = lax.broadcasted_iota(I32, (8, HY_WIDTH), 0)
    for bb in range(xe.shape[1] // HY_WIDTH):
        sl = slice(bb * HY_WIDTH, (bb + 1) * HY_WIDTH)
        are, aim, bre, bim = a_re[:, sl], a_im[:, sl], b_re[:, sl], b_im[:, sl]
        u1re, u1im = are + bre, aim + bim
        u2re, u2im = are - bre, bim - aim
        z1re, z1im = u1re * t1re - u1im * t1im, u1re * t1im + u1im * t1re
        z2re, z2im = u2re * t2re - u2im * t2im, u2re * t2im + u2im * t2re
        gere_o[:, sl] = (z1re + z2re).astype(BF16)
        geim_o[:, sl] = (z1im - z2im).astype(BF16)
        gore_o[:, sl] = (z1re - z2re).astype(BF16)
        goim_o[:, sl] = (z1im + z2im).astype(BF16)

        @pl.when(first_block)
        def _():
            u0, ul = u1re[0:8], u2re[0:8]
            a_s, b_s = aim[0:8], bim[0:8]
            dc, ny, mre, mim = sp[0:1], sp[1:2], sp[2:3], sp[3:4]
            first = row8 == 0
            gere_o[0:8, sl] = jnp.where(first, u0 * dc + ul * ny, (z1re + z2re)[0:8]).astype(BF16)
            gore_o[0:8, sl] = jnp.where(first, u0 * dc - ul * ny, (z1re - z2re)[0:8]).astype(BF16)
            geim_o[0:8, sl] = jnp.where(first, a_s * mre + b_s * mim, (z1im - z2im)[0:8]).astype(BF16)
            goim_o[0:8, sl] = jnp.where(first, b_s * mre - a_s * mim, (z1im + z2im)[0:8]).astype(BF16)


def _hyp_fwd(mats, x, plane, taps=None):
    _, _, h, nc = x.shape
    tk = min(h, 256)
    cb = _col_block(nc, 1024)
    grid = (nc // cb, h // tk)
    m_spec = pl.BlockSpec((tk, h), lambda c, j: (j, 0))
    x_spec = lambda r: pl.BlockSpec((None, None, h, cb), lambda c, j: (plane, r, 0, c))
    o_spec = pl.BlockSpec((tk, cb), lambda c, j: (j, c))
    if taps is None:
        return pl.pallas_call(
            functools.partial(_hyp_fwd_kernel, with_taps=False),
            grid=grid, in_specs=[m_spec] * 4 + [x_spec(0), x_spec(1)], out_specs=[o_spec] * 4,
            out_shape=[jax.ShapeDtypeStruct((h, nc), F32)] * 4,
            compiler_params=_cparams("parallel", "arbitrary"),
            name="hyena_dft_filters",
        )(*mats['fwd'], x, x)
    t_spec = pl.BlockSpec((tk, HY_WIDTH), lambda c, j: (j, 0))
    sp_spec = pl.BlockSpec((8, HY_WIDTH), lambda c, j: (0, 0))
    return pl.pallas_call(
        functools.partial(_hyp_fwd_kernel, with_taps=True),
        grid=grid, in_specs=[m_spec] * 4 + [x_spec(0), x_spec(1)] + [t_spec] * 4 + [sp_spec],
        out_specs=[o_spec] * 4,
        out_shape=[jax.ShapeDtypeStruct((h, nc), BF16)] * 4,
        compiler_params=_cparams("parallel", "arbitrary"),
        name="hyena_dft_forward",
    )(*mats['fwd'], x, x, *taps)


def _hyp_inv_kernel(ce_ref, set_ref, cot_ref, sot_ref, gere_ref, geim_ref, gore_ref, goim_ref,
                    gate_ref, prev_ref, bias_ref, *outs, last):
    conv = (_dot(ce_ref[...], gere_ref[...]) + _dot(set_ref[...], geim_ref[...]),
            _dot(cot_ref[...], gore_ref[...]) + _dot(sot_ref[...], goim_ref[...]))
    bias = bias_ref[...]
    for r in range(2):
        y = gate_ref[r] * (conv[r] + prev_ref[r] * bias)
        if last:
            (tok_o,) = outs
            for bb in range(y.shape[1] // HY_WIDTH):
                tok_o[bb, :, r * HY_WIDTH:(r + 1) * HY_WIDTH] = y[:, bb * HY_WIDTH:(bb + 1) * HY_WIDTH].astype(BF16)
        else:
            y_o, ybf_o = outs
            y_o[r] = y
            ybf_o[r] = y.astype(BF16)


def _hyp_inv(mats, g, gate, gate_plane, prev, prev_plane, bias_row, last):
    h, nc = g[0].shape
    tm = min(h, 256)
    cb = _col_block(nc, 512)
    grid = (nc // cb, h // tm)
    m_spec = pl.BlockSpec((tm, h), lambda c, i: (i, 0))
    g_spec = pl.BlockSpec((h, cb), lambda c, i: (0, c))
    e_spec = lambda plane: pl.BlockSpec((None, 2, tm, cb), lambda c, i: (plane, 0, i, c))
    b_spec = pl.BlockSpec((1, cb), lambda c, i: (0, c))
    if last:
        out_specs = [pl.BlockSpec((cb // HY_WIDTH, tm, 2 * HY_WIDTH), lambda c, i: (c, i, 0))]
        out_shape = [jax.ShapeDtypeStruct((nc // HY_WIDTH, h, 2 * HY_WIDTH), BF16)]
    else:
        out_specs = [e_spec(0), e_spec(0)]
        out_shape = [jax.ShapeDtypeStruct((1, 2, h, nc), F32), jax.ShapeDtypeStruct((1, 2, h, nc), BF16)]
    return pl.pallas_call(
        functools.partial(_hyp_inv_kernel, last=last),
        grid=grid, in_specs=[m_spec] * 4 + [g_spec] * 4 + [e_spec(gate_plane), e_spec(prev_plane), b_spec],
        out_specs=out_specs, out_shape=out_shape,
        compiler_params=_cparams("parallel", "arbitrary"),
        name="hyena_dft_inverse",
    )(*mats['inv'], *g, gate, prev, bias_row)


def _hyp_tap_tables(spec, filt, length):
    a_re, a_im, b_re, b_im = spec
    w = HY_WIDTH
    inv_n = 1.0 / (2 * length)
    tables = []
    for o in range(HY_ORDER):
        f_sl = slice((2 * o) * w, (2 * o + 1) * w)
        r_sl = slice((2 * o + 1) * w, (2 * o + 2) * w)
        hb0 = filt[0, 0:1, r_sl]
        f1re = (a_re + b_re)[:, f_sl] + (a_re + b_re)[:, r_sl] - hb0
        f1im = (a_im + b_im)[:, f_sl] - (a_im + b_im)[:, r_sl]
        f2re = (a_re - b_re)[:, f_sl] + (a_re - b_re)[:, r_sl] - hb0
        f2im = (b_im - a_im)[:, f_sl] - (b_im - a_im)[:, r_sl]
        dc = f1re[0:1]
        ny = f2re[0:1]
        mid_re = a_im[0:1, f_sl] + a_im[0:1, r_sl] - hb0
        mid_im = -b_im[0:1, f_sl] + b_im[0:1, r_sl]
        sp = jnp.concatenate([dc * inv_n, ny * inv_n, mid_re * (2 * inv_n), mid_im * (2 * inv_n),
                              jnp.zeros((4, w), F32)], axis=0)
        tables.append((f1re * (2 * inv_n), f1im * (2 * inv_n), f2re * (2 * inv_n), f2im * (2 * inv_n), sp))
    return tables


def _hyena_seq(mats, vx, vx_bf, lp, n_batch):
    h = vx.shape[2]
    length = 2 * h
    filt, filt_bf = _hyp_filters(length, lp)
    spec = _hyp_fwd(mats, filt_bf[None], 0)
    tables = _hyp_tap_tables(spec, filt, length)
    bias = lp['hy_bias'].astype(F32)
    y, y_bf = vx, vx_bf
    for o in range(HY_ORDER):
        g = _hyp_fwd(mats, y_bf, 0, tables[o])
        bias_row = jnp.tile(bias[o][None, :], (1, n_batch))
        last = o == HY_ORDER - 1
        res = _hyp_inv(mats, g, vx, o + 1, y, 0, bias_row, last)
        if last:
            return res[0].reshape(n_batch, length, HY_WIDTH)
        y, y_bf = res


def _lru_kernel(u_ref, g_ref, cw_ref, cb_ref, wa_ref, ba_ref, wx_ref, bx_ref, lam_ref, o_ref,
                pad_ref, y_ref, *, seq, ctx_len):
    tc = LRU_CHUNK
    halo = LRU_HALO
    width = LRU_WIDTH
    lat_off = halo
    ctx_off = 2 * halo + seq
    zero = jnp.zeros((halo, width), F32)
    pad_ref[0:halo, :] = zero
    pad_ref[lat_off:lat_off + seq, :] = u_ref[0, 0:seq, :]
    pad_ref[lat_off + seq:ctx_off, :] = zero
    pad_ref[ctx_off:ctx_off + ctx_len, :] = u_ref[0, seq:seq + ctx_len, :]
    pad_ref[ctx_off + ctx_len:ctx_off + ctx_len + halo, :] = zero
    row = lax.broadcasted_iota(I32, (tc, width), 0)
    n_win = tc + 2 * halo

    def chunk(pad_off, y_off, s, carry, d):
        wstart = pl.multiple_of(pad_off + s - halo, 8)
        win = pad_ref[pl.ds(wstart, n_win), :]
        cw = cw_ref[d]
        xc = cb_ref[d]
        for k in range(LRU_CONV):
            shift = (LRU_CONV - 1 - k) if d == 0 else -k
            rolled = win if shift == 0 else pltpu.roll(win, shift % n_win, axis=0)
            xc = xc + cw[k:k + 1] * rolled[halo:halo + tc]
        xb = xc.astype(BF16)
        r = _sigmoid(_dot(xb, wa_ref[d]) + ba_ref[d])
        gi = _sigmoid(_dot(xb, wx_ref[d]) + bx_ref[d])
        lam = lam_ref[d]
        softplus = jnp.maximum(-lam, 0.0) + jnp.log1p(jnp.exp(-jnp.abs(lam)))
        log_a = -LRU_C * r * softplus
        a = jnp.exp(log_a)
        bt = jnp.sqrt(-jnp.tanh(log_a) * (a * a + 1.0)) * (gi * xc)
        sft = 1
        while sft < tc:
            if d == 0:
                keep = row >= sft
                a_s = jnp.where(keep, pltpu.roll(a, sft, axis=0), 1.0)
                b_s = jnp.where(keep, pltpu.roll(bt, sft, axis=0), 0.0)
            else:
                keep = row < tc - sft
                a_s = jnp.where(keep, pltpu.roll(a, tc - sft, axis=0), 1.0)
                b_s = jnp.where(keep, pltpu.roll(bt, tc - sft, axis=0), 0.0)
            bt = a * b_s + bt
            a = a * a_s
            sft *= 2
        h = a * carry + bt
        yo = pl.multiple_of(y_off + s, 8)
        if d == 0:
            y_ref[pl.ds(yo, tc), :] = h
            return h[tc - 1:tc]
        y_ref[pl.ds(yo, tc), :] = y_ref[pl.ds(yo, tc), :] + h
        return h[0:1]

    n_lat = seq // tc
    n_ctx = ctx_len // tc
    for d in range(2):
        carry = jnp.zeros((1, width), F32)
        order = range(n_ctx) if d == 0 else range(n_ctx - 1, -1, -1)
        for c in order:
            carry = chunk(ctx_off, seq, c * tc, carry, d)

        def body(j, cr, d=d):
            jj = j if d == 0 else n_lat - 1 - j
            return chunk(lat_off, 0, jj * tc, cr, d)

        lax.fori_loop(0, n_lat, body, carry)
    o_ref[0] = (y_ref[...] * _gelu_tanh(g_ref[0])).astype(BF16)


def _block_diag(w):
    nd, nb, c, _ = w.shape
    out = jnp.zeros((nd, nb * c, nb * c), w.dtype)
    for n in range(nb):
        out = out.at[:, n * c:(n + 1) * c, n * c:(n + 1) * c].set(w[:, n])
    return out


def _lru_mixer(lu, lg, lp, seq):
    b, nt, w = lu.shape
    ctx_len = nt - seq
    row3 = lambda a: a.reshape(2, 1, w)
    args = (_pad_to(lp['lru_conv_w'], 8, 1), row3(lp['lru_conv_b']), _block_diag(lp['lru_wa']).astype(BF16),
            row3(lp['lru_ba']), _block_diag(lp['lru_wx']).astype(BF16), row3(lp['lru_bx']), row3(lp['lru_lambda']))
    full = lambda a: pl.BlockSpec(a.shape, lambda bb: (0,) * a.ndim)
    tok = pl.BlockSpec((1, nt, w), lambda bb: (bb, 0, 0))
    return pl.pallas_call(
        functools.partial(_lru_kernel, seq=seq, ctx_len=ctx_len),
        grid=(b,),
        in_specs=[tok, tok] + [full(a) for a in args],
        out_specs=tok,
        out_shape=jax.ShapeDtypeStruct((b, nt, w), BF16),
        scratch_shapes=[pltpu.VMEM((nt + 3 * LRU_HALO, w), F32), pltpu.VMEM((nt, w), F32)],
        compiler_params=_cparams("parallel"),
        name="rglru_scan",
    )(lu, lg, *args)


def _merge_kernel(xl_ref, xc_ref, mod_ref, g1_ref, a_ref, b_ref, c_ref, d_ref, wg_ref, wa_ref, wb_ref, wc_ref,
                  wd_ref, wo_ref, o_ref, *, n_lat_tiles):
    group, tm, dm = xl_ref.shape
    m = mod_ref[:, 0]
    x = _stream_tile(xl_ref, xc_ref, n_lat_tiles)
    h = _normmod(x, g1_ref[...], m[:, 1:2], m[:, 0:1]).reshape(group * tm, dm).astype(BF16)
    acc = None
    for k, (br, w) in enumerate(((a_ref, wa_ref), (b_ref, wb_ref), (c_ref, wc_ref), (d_ref, wd_ref))):
        term = (1.0 + jnp.tanh(_dot(h, wg_ref[:, k * dm:(k + 1) * dm]))) * _dot(
            br[...].reshape(group * tm, br.shape[-1]), w[...])
        acc = term if acc is None else acc + term
    y = _dot(acc.astype(BF16), wo_ref[...])
    o_ref[...] = x + m[:, 2:3] * y.reshape(group, tm, dm)


def _merge(stream, modtab, g1, branches, w_gate, lp, n_lat_tiles, tiles_used):
    b, _, d = stream[0].shape
    wbr = lp['w_branch']
    head_rows = lambda w, dv: jnp.concatenate(
        [_pad_to(w[hd * dv:(hd + 1) * dv], LANE, 0) for hd in range(4)], axis=0)
    wbr = 0.5 * wbr
    weights = (head_rows(wbr[0], MLA_V).astype(BF16), head_rows(wbr[1], NA_HEAD_DIM).astype(BF16),
               wbr[2].astype(BF16), wbr[3].astype(BF16), lp['w_out'].astype(BF16))
    group = _batch_group(b)
    full = lambda a: pl.BlockSpec(a.shape, lambda bb, i: (0,) * a.ndim)
    tok = lambda n: pl.BlockSpec((group, TM, n), lambda bb, i: (bb, i, 0))
    return pl.pallas_call(
        functools.partial(_merge_kernel, n_lat_tiles=n_lat_tiles),
        grid=(b // group, tiles_used),
        in_specs=_stream_specs(stream, group, n_lat_tiles)
                 + [pl.BlockSpec((group, 1, 8, d), _kind_map(n_lat_tiles)), full(g1)]
                 + [tok(br.shape[-1]) for br in branches] + [full(w_gate)] + [full(w) for w in weights],
        out_specs=tok(d),
        out_shape=jax.ShapeDtypeStruct((b, tiles_used * TM, d), F32),
        compiler_params=_cparams("parallel", "parallel"),
        name="merge_branches",
    )(stream[0], stream[1], modtab, g1, *branches, w_gate, *weights)


U32 = jnp.uint32
EXPERT_ROWS = 512


def _pack_pair(x):
    n = x.shape[-1] // 2
    hi = lax.bitcast_convert_type(x[:, :n].astype(BF16).astype(F32), U32)
    lo = lax.bitcast_convert_type(x[:, n:].astype(BF16).astype(F32), U32)
    return hi | (lo >> 16)


def _unpack_pair(p):
    hi = lax.bitcast_convert_type(p & jnp.uint32(0xFFFF0000), F32)
    lo = lax.bitcast_convert_type(p << 16, F32)
    return hi, lo


def _router_kernel(x_ref, mod_ref, g2_ref, rw_ref, rb_ref, tri_ref, h2_o, idx_o, wts_o, rank_o, cnt_o, carry):
    i = pl.program_id(0)

    @pl.when(i == 0)
    def _():
        carry[...] = jnp.zeros_like(carry)

    m = mod_ref[0, 0]
    h2 = _normmod(x_ref[0], g2_ref[...], m[4:5], m[3:4])
    half = h2.shape[-1] // 2
    h2_o[0] = _pack_pair(h2[:, :half])
    h2_o[1] = _pack_pair(h2[:, half:])
    logits = lax.dot_general(rw_ref[...], h2, (((1,), (1,)), ((), ())), preferred_element_type=F32,
                             precision=lax.Precision.HIGHEST)
    scores = _sigmoid(logits)
    biased = scores + rb_ref[...]
    expert = lax.broadcasted_iota(I32, scores.shape, 0)
    picks = []
    onehot_all = jnp.zeros(scores.shape, F32)
    for _ in range(TOP_K):
        best = jnp.max(biased, axis=0, keepdims=True)
        arg = jnp.min(jnp.where(biased == best, expert, N_EXPERTS), axis=0, keepdims=True)
        hit = expert == arg
        sel = jnp.sum(jnp.where(hit, scores, 0.0), axis=0, keepdims=True)
        biased = jnp.where(hit, -jnp.inf, biased)
        onehot_all = onehot_all + jnp.where(hit, 1.0, 0.0)
        picks.append((arg, hit, sel))
    total = picks[0][2]
    for _, _, sel in picks[1:]:
        total = total + sel
    earlier = _dot(onehot_all.astype(BF16), tri_ref[...]) + carry[...]
    pad_rows = TOPK_PAD - TOP_K
    ranks = [jnp.sum(jnp.where(hit, earlier, 0.0), axis=0, keepdims=True).astype(I32) for _, hit, _ in picks]
    scale = ROUTED_SCALE / total
    idx_o[...] = jnp.concatenate([arg for arg, _, _ in picks] + [jnp.zeros((pad_rows, TM), I32)], axis=0)
    wts_o[...] = jnp.concatenate([sel * scale for _, _, sel in picks] + [jnp.zeros((pad_rows, TM), F32)], axis=0)
    rank_o[...] = jnp.concatenate(ranks + [jnp.zeros((pad_rows, TM), I32)], axis=0)
    carry[...] = carry[...] + jnp.sum(onehot_all, axis=1, keepdims=True)
    cnt_o[...] = carry[...]


def _tile_maps(tiles_used, n_lat_tiles):
    tok = lambda i: (i // tiles_used, i % tiles_used, 0)
    mod = lambda i: (i // tiles_used, jnp.where(i % tiles_used >= n_lat_tiles, 1, 0), 0, 0)
    return tok, mod


def _route(x1, modtab, g2, lp, n_lat_tiles, tiles_used):
    b, _, d = x1.shape
    n_tiles = b * tiles_used
    t = n_tiles * TM
    rw = lp['router_w'].T
    rb = lp['router_bias'].reshape(-1, 1)
    tri = (np.arange(TM)[:, None] < np.arange(TM)[None, :]).astype(np.float32)
    tri = jnp.asarray(tri, BF16)
    per_tok = lambda: pl.BlockSpec((TOPK_PAD, TM), lambda i: (0, i))
    full = lambda a: pl.BlockSpec(a.shape, lambda i: (0,) * a.ndim)
    tok, mod = _tile_maps(tiles_used, n_lat_tiles)
    return pl.pallas_call(
        _router_kernel,
        grid=(n_tiles,),
        in_specs=[pl.BlockSpec((1, TM, d), tok), pl.BlockSpec((1, 1, 8, d), mod),
                  full(g2), full(rw), full(rb), full(tri)],
        out_specs=[pl.BlockSpec((2, TM, d // 4), lambda i: (0, i, 0)), per_tok(), per_tok(), per_tok(),
                   pl.BlockSpec((N_EXPERTS, 1), lambda i: (0, 0))],
        out_shape=[jax.ShapeDtypeStruct((2, t, d // 4), U32), jax.ShapeDtypeStruct((TOPK_PAD, t), I32),
                   jax.ShapeDtypeStruct((TOPK_PAD, t), F32), jax.ShapeDtypeStruct((TOPK_PAD, t), I32),
                   jax.ShapeDtypeStruct((N_EXPERTS, 1), F32)],
        scratch_shapes=[pltpu.VMEM((N_EXPERTS, 1), F32)],
        compiler_params=_cparams("arbitrary"),
        name="moe_router",
    )(x1, modtab, g2, rw, rb, tri)


SC_WINDOW = 128


def _sc_mesh():
    return plsc.VectorSubcoreMesh(core_axis_name="c", subcore_axis_name="s")


def _sc_scatter_rows(src, idx, n_out):
    n, width = src.shape
    k_rep = idx.shape[0]
    half = n // SC_WINDOW // 2

    @functools.partial(pl.kernel, out_type=jax.ShapeDtypeStruct((n_out, width), src.dtype), mesh=_sc_mesh(),
                       scratch_types=[], name="moe_dispatch_sc")
    def scatter(src_hbm, idx_hbm, out_hbm):
        def body(x_vmem, *i_vmems):
            for i_vmem in i_vmems:
                pltpu.sync_copy(x_vmem, out_hbm.at[i_vmem.at[0]])

        pltpu.emit_pipeline(
            body,
            grid=(2, half),
            in_specs=[pl.BlockSpec((SC_WINDOW, width), lambda a, i: (a * half + i, 0))]
                     + [pl.BlockSpec((1, SC_WINDOW), lambda a, i, k=k: (k, a * half + i)) for k in range(k_rep)],
            out_specs=[],
            core_axis_name=("c", "s"),
            dimension_semantics=(pltpu.PARALLEL, pltpu.PARALLEL),
        )(src_hbm, *([idx_hbm] * k_rep))

    return scatter(src, idx)


def _sc_gather_rows(src, idx):
    k_rep, n = idx.shape
    width = src.shape[1]
    n_win = n // SC_WINDOW

    @functools.partial(pl.kernel, out_type=jax.ShapeDtypeStruct((k_rep * n, width), src.dtype), mesh=_sc_mesh(),
                       scratch_types=[], name="moe_gather_sc")
    def gather(src_hbm, idx_hbm, out_hbm):
        def body(i_vmem, o_vmem):
            pltpu.sync_copy(src_hbm.at[i_vmem.at[0]], o_vmem)

        pltpu.emit_pipeline(
            body,
            grid=(k_rep, n_win),
            in_specs=[pl.BlockSpec((1, SC_WINDOW), lambda k, i: (k, i))],
            out_specs=[pl.BlockSpec((SC_WINDOW, width), lambda k, i: (k * n_win + i, 0))],
            core_axis_name=("c", "s"),
            dimension_semantics=(pltpu.PARALLEL, pltpu.PARALLEL),
        )(idx_hbm, out_hbm)

    return gather(src, idx)


def _unpack_planes(p0, p1):
    return _unpack_pair(p0) + _unpack_pair(p1)


def _dot_quarters(parts, w_ref):
    q = parts[0].shape[-1]
    acc = None
    for j, part in enumerate(parts):
        term = _dot(part.astype(BF16), w_ref[j * q:(j + 1) * q, :])
        acc = term if acc is None else acc + term
    return acc


def _expert_kernel(be_ref, nv_ref, xs_ref, wg_ref, wu_ref, wd_ref, ys_o, wg_s, wu_s, wd_s):
    i = pl.program_id(0)
    prev = be_ref[jnp.maximum(i - 1, 0)]

    @pl.when((i == 0) | (be_ref[i] != prev))
    def _():
        wg_s[...] = (0.5 * wg_ref[0]).astype(BF16)
        wu_s[...] = wu_ref[0].astype(BF16)
        wd_s[...] = wd_ref[0].astype(BF16)

    @pl.when(nv_ref[i] > 0)
    def _():
        keep = lax.broadcasted_iota(I32, xs_ref.shape[1:], 0) < nv_ref[i]
        parts = _unpack_planes(jnp.where(keep, xs_ref[0], jnp.uint32(0)), jnp.where(keep, xs_ref[1], jnp.uint32(0)))
        hid = _half_silu(_dot_quarters(parts, wg_s)) * _dot_quarters(parts, wu_s)
        y = _dot(hid.astype(BF16), wd_s[...])
        half = y.shape[-1] // 2
        ys_o[0] = _pack_pair(y[:, :half])
        ys_o[1] = _pack_pair(y[:, half:])

    @pl.when(nv_ref[i] <= 0)
    def _():
        ys_o[...] = jnp.zeros_like(ys_o)


def _experts(xs, block_e, n_valid, weights, layer):
    _, n_rows, dq = xs.shape
    d = 4 * dq
    n_blocks = n_rows // EXPERT_ROWS
    hid = EXPERT_HIDDEN
    grid_spec = pltpu.PrefetchScalarGridSpec(
        num_scalar_prefetch=2,
        grid=(n_blocks,),
        in_specs=[pl.BlockSpec((2, EXPERT_ROWS, dq), lambda i, be, nv: (0, i, 0)),
                  pl.BlockSpec((None, 1, d, hid), lambda i, be, nv: (layer, be[i], 0, 0)),
                  pl.BlockSpec((None, 1, d, hid), lambda i, be, nv: (layer, be[i], 0, 0)),
                  pl.BlockSpec((None, 1, hid, d), lambda i, be, nv: (layer, be[i], 0, 0))],
        out_specs=pl.BlockSpec((2, EXPERT_ROWS, dq), lambda i, be, nv: (0, i, 0)),
        scratch_shapes=[pltpu.VMEM((d, hid), BF16), pltpu.VMEM((d, hid), BF16), pltpu.VMEM((hid, d), BF16)],
    )
    return pl.pallas_call(
        _expert_kernel,
        grid_spec=grid_spec,
        out_shape=jax.ShapeDtypeStruct((2, n_rows, dq), U32),
        compiler_params=_cparams("arbitrary"),
        name="moe_experts",
    )(block_e, n_valid, xs, *weights)


def _combine_kernel(g_ref, wts_ref, h2_ref, x_ref, mod_ref, sg_ref, su_ref, sd_ref, gf_ref, o_ref, *, final):
    parts = _unpack_planes(h2_ref[0], h2_ref[1])
    hid = _half_silu(_dot_quarters(parts, sg_ref)) * _dot_quarters(parts, su_ref)
    shared = _dot(hid.astype(BF16), sd_ref[...])
    wts = wts_ref[...]
    routed = None
    for k in range(TOP_K):
        w = wts[:, k:k + 1]
        terms = [w * part for part in _unpack_planes(g_ref[k, 0], g_ref[k, 1])]
        routed = terms if routed is None else [r + t for r, t in zip(routed, terms)]
    f = shared + jnp.concatenate(routed, axis=1)
    m = mod_ref[0, 0]
    x2 = x_ref[0] + m[5:6] * f
    if final:
        x2 = x2 * lax.rsqrt(jnp.mean(x2 * x2, axis=-1, keepdims=True) + NORM_EPS) * gf_ref[...]
    o_ref[0] = x2


def _combine(g, wts, h2p, x1, modtab, lp, g_final, n_lat_tiles, tiles_used, final):
    b, _, d = x1.shape
    dq = d // 4
    weights = ((0.5 * lp['sh_w_gate']).astype(BF16), lp['sh_w_up'].astype(BF16), lp['sh_w_down'].astype(BF16),
               g_final.reshape(1, -1))
    full = lambda a: pl.BlockSpec(a.shape, lambda i: (0,) * a.ndim)
    tok, mod = _tile_maps(tiles_used, n_lat_tiles)
    return pl.pallas_call(
        functools.partial(_combine_kernel, final=final),
        grid=(b * tiles_used,),
        in_specs=[pl.BlockSpec((TOP_K, 2, TM, dq), lambda i: (0, 0, i, 0)),
                  pl.BlockSpec((TM, TOPK_PAD), lambda i: (i, 0)),
                  pl.BlockSpec((2, TM, dq), lambda i: (0, i, 0)),
                  pl.BlockSpec((1, TM, d), tok), pl.BlockSpec((1, 1, 8, d), mod)]
                 + [full(w) for w in weights],
        out_specs=pl.BlockSpec((1, TM, d), tok),
        out_shape=jax.ShapeDtypeStruct((b, tiles_used * TM, d), F32),
        compiler_params=_cparams("parallel"),
        name="moe_combine",
    )(g, wts, h2p, x1, modtab, *weights)


def _sorted_rows_kernel(idx_ref, rank_ref, ps_ref, o_ref):
    expert = lax.broadcasted_iota(I32, (N_EXPERTS, idx_ref.shape[1]), 0)
    starts = ps_ref[...]
    idx = idx_ref[...]
    rows = [jnp.sum(jnp.where(idx[k:k + 1] == expert, starts, 0), axis=0, keepdims=True) for k in range(TOP_K)]
    pad = jnp.zeros((TOPK_PAD - TOP_K, idx.shape[1]), I32)
    o_ref[...] = jnp.concatenate(rows + [pad], axis=0) + rank_ref[...]


def _sorted_rows(idx, rank, p_starts):
    t = idx.shape[1]
    cols = TM * max(k for k in (8, 4, 2, 1) if (t // TM) % k == 0)
    per_tok = pl.BlockSpec((TOPK_PAD, cols), lambda i: (0, i))
    return pl.pallas_call(
        _sorted_rows_kernel,
        grid=(t // cols,),
        in_specs=[per_tok, per_tok, pl.BlockSpec((N_EXPERTS, 1), lambda i: (0, 0))],
        out_specs=per_tok,
        out_shape=jax.ShapeDtypeStruct((TOPK_PAD, t), I32),
        compiler_params=_cparams("parallel"),
        name="moe_sorted_rows",
    )(idx, rank, p_starts.reshape(-1, 1))


def _moe(x1, modtab, g2, lp, g_final, n_lat_tiles, final):
    b, nt, d = x1.shape
    tiles_used = n_lat_tiles if final else nt // TM
    t = b * tiles_used * TM
    h2p, idx, wts, rank, cnt = _route(x1, modtab, g2, lp, n_lat_tiles, tiles_used)
    counts = cnt[:, 0].astype(I32)
    padded = (counts + EXPERT_ROWS - 1) // EXPERT_ROWS * EXPERT_ROWS
    p_ends = jnp.cumsum(padded)
    p_starts = p_ends - padded
    n_blocks = (t * TOP_K + N_EXPERTS * (EXPERT_ROWS - 1)) // EXPERT_ROWS
    n_rows = n_blocks * EXPERT_ROWS
    dest = _sorted_rows(idx, rank, p_starts)[:TOP_K]
    plane_idx = jnp.concatenate([dest, dest + n_rows], axis=1)
    blk_start = jnp.arange(n_blocks, dtype=I32) * EXPERT_ROWS
    block_e = jnp.minimum(jnp.sum((p_ends[None, :] <= blk_start[:, None]).astype(I32), axis=1), N_EXPERTS - 1)
    n_valid = jnp.clip((p_starts + counts)[block_e] - blk_start, 0, EXPERT_ROWS).astype(I32)
    dq = d // 4
    xs = _sc_scatter_rows(h2p.reshape(2 * t, dq), plane_idx, 2 * n_rows).reshape(2, n_rows, dq)
    ys = _experts(xs, block_e, n_valid, lp['expert_stacks'], lp['layer'])
    g = _sc_gather_rows(ys.reshape(2 * n_rows, dq), plane_idx).reshape(TOP_K, 2, t, dq)
    return _combine(g, wts.T, h2p, x1, modtab, lp, g_final, n_lat_tiles, tiles_used, final)


def _layer(stream, nt, c, c_ctx, lp, consts, g_final, seq, final):
    b = stream[0].shape[0]
    n_lat_tiles = seq // TM
    rope, mats_lat, mats_ctx = consts
    modtab = _mod_table(c, c_ctx, *lp['mod_stacks'], lp['layer'])
    g1 = lp['g_norm1'].reshape(1, -1)
    g2 = lp['g_norm2'].reshape(1, -1)
    pw = _proj_weights(lp)
    q, k, v, nq, nk, nv, hy, lu, lg = _project(stream, nt, modtab, g1, pw, rope, n_lat_tiles)
    tiles_used = n_lat_tiles if final else nt // TM
    br_a = _mla_attention(q, k, v, seq, tiles_used)
    br_b = _na_attention(nq, nk, nv, _na_bias_tables(lp['na_rpb'], seq // GRID_W), seq, tiles_used)
    pre = _hyp_pre(hy, lp['hy_short_w'], lp['hy_short_b'], seq)
    lat_f, lat_bf, ctx_f, ctx_bf = pre
    br_c = _hyena_seq(mats_lat, lat_f, lat_bf, lp, b)
    if not final:
        br_c = jnp.concatenate([br_c, _hyena_seq(mats_ctx, ctx_f, ctx_bf, lp, b)], axis=1)
    br_d = _lru_mixer(lu, lg, lp, seq)
    x1 = _merge(stream, modtab, g1, (br_a, br_b, br_c, br_d), pw['w_gate'], lp, n_lat_tiles, tiles_used)
    return _moe(x1, modtab, g2, lp, g_final, n_lat_tiles, final)


_LAYER_KEYS = ('w_mod', 'b_mod', 'g_norm1', 'g_norm2', 'w_in', 'mla_g_q', 'mla_w_uq', 'mla_g_kv', 'mla_w_ukv',
               'na_rpb', 'hy_short_w', 'hy_short_b', 'hy_w1', 'hy_b1', 'hy_w2', 'hy_b2', 'hy_w3', 'hy_decay',
               'hy_bias', 'lru_conv_w', 'lru_conv_b', 'lru_wa', 'lru_ba', 'lru_wx', 'lru_bx', 'lru_lambda',
               'w_branch', 'w_out', 'router_w', 'router_bias', 'exp_w_gate', 'exp_w_up', 'exp_w_down',
               'sh_w_gate', 'sh_w_up', 'sh_w_down')


def kernel(x, c, ctx, c_ctx, w_mod, b_mod, g_norm1, g_norm2, w_in, mla_g_q, mla_w_uq, mla_g_kv, mla_w_ukv, na_rpb, hy_short_w, hy_short_b, hy_w1, hy_b1, hy_w2, hy_b2, hy_w3, hy_decay, hy_bias, lru_conv_w, lru_conv_b, lru_wa, lru_ba, lru_wx, lru_bx, lru_lambda, w_branch, w_out, router_w, router_bias, exp_w_gate, exp_w_up, exp_w_down, sh_w_gate, sh_w_up, sh_w_down, g_final):
    stacked = dict(zip(_LAYER_KEYS, (w_mod, b_mod, g_norm1, g_norm2, w_in, mla_g_q, mla_w_uq, mla_g_kv, mla_w_ukv,
                                     na_rpb, hy_short_w, hy_short_b, hy_w1, hy_b1, hy_w2, hy_b2, hy_w3, hy_decay,
                                     hy_bias, lru_conv_w, lru_conv_b, lru_wa, lru_ba, lru_wx, lru_bx, lru_lambda,
                                     w_branch, w_out, router_w, router_bias, exp_w_gate, exp_w_up, exp_w_down,
                                     sh_w_gate, sh_w_up, sh_w_down)))
    b, seq, d = x.shape
    ctx_len = ctx.shape[1]
    depth = w_mod.shape[0]
    assert seq % TM == 0 and ctx_len % TM == 0 and seq // GRID_W >= NA_KEY_ROWS + 1
    nt = seq + ctx_len
    stream = (x, ctx, 0)
    consts = (_rope_tables(seq, seq + ctx_len), _hyp_matrices(seq), _hyp_matrices(ctx_len))
    for i in range(depth):
        big = ('w_mod', 'b_mod', 'exp_w_gate', 'exp_w_up', 'exp_w_down')
        lp = {name: w[i] for name, w in stacked.items() if name not in big}
        lp['layer'] = i
        lp['mod_stacks'] = (w_mod, b_mod)
        lp['expert_stacks'] = (exp_w_gate, exp_w_up, exp_w_down)
        xa = _layer(stream, nt, c, c_ctx, lp, consts, g_final, seq, i == depth - 1)
        stream = (xa, xa, seq // TM)
    return xa
```

```python
import functools
import math

import numpy as np
import jax
import jax.numpy as jnp
from jax import lax
from jax.experimental import pallas as pl
from jax.experimental.pallas import tpu as pltpu
from jax.experimental.pallas import tpu_sc as plsc

F32 = jnp.float32
BF16 = jnp.bfloat16
I32 = jnp.int32

TM = 256
LANE = 128
GRID_W = 64
N_MOD = 6
NORM_EPS = 1e-6

MLA_HEADS, MLA_NOPE, MLA_ROPE, MLA_V = 4, 64, 32, 64
MLA_Q_RANK, MLA_KV_RANK = 192, 128
MLA_Q_PAD = 256
ROPE_THETA = 10000.0

NA_HEADS, NA_HEAD_DIM, NA_WIN_R, NA_WIN_C = 4, 64, 8, 16
NA_TILE_ROWS = TM // GRID_W
NA_KEY_ROWS = NA_TILE_ROWS + NA_WIN_R - 1
NA_KEYS = NA_KEY_ROWS * GRID_W

HY_WIDTH, HY_ORDER, HY_SHORT, HY_BANDS, HY_FFN = 256, 2, 3, 16, 64
HY_EMB = 2 * HY_BANDS + 1

LRU_WIDTH, LRU_BLOCKS, LRU_CONV, LRU_C = 256, 4, 4, 8.0
LRU_CHUNK = 256
LRU_HALO = 8

N_EXPERTS, TOP_K, EXPERT_HIDDEN, ROUTED_SCALE, MOE_BLOCK = 64, 6, 256, 2.5, 256
TOPK_PAD = 8

VMEM_LIMIT = 52 * 1024 * 1024


def _cparams(*sem):
    return pltpu.CompilerParams(dimension_semantics=sem, vmem_limit_bytes=VMEM_LIMIT)


def _dot(a, b):
    return jnp.dot(a, b, preferred_element_type=F32)


def _dot_nt(a, b):
    return lax.dot_general(a, b, (((1,), (1,)), ((), ())), preferred_element_type=F32)


def _sigmoid(x):
    return jax.nn.sigmoid(x)


def _half_silu(g):
    return g * (1.0 + jnp.tanh(g))


def _silu(x):
    return x * _sigmoid(x)


def _gelu_tanh(x):
    return 0.5 * x * (1.0 + jnp.tanh(math.sqrt(2.0 / math.pi) * (x + 0.044715 * (x * x * x))))


def _normmod(x, g, scale, shift):
    y = x * lax.rsqrt(jnp.mean(x * x, axis=-1, keepdims=True) + NORM_EPS) * g
    return y * (1.0 + scale) + shift


def _mod_kernel(c_ref, w_ref, b_ref, o_ref):
    s = _silu(c_ref[...])
    o_ref[...] = _dot(s.astype(BF16), w_ref[...].astype(BF16)) + b_ref[...]


def _mod_table(c, c_ctx, w_mod, b_mod, layer):
    b, d = c.shape
    rows = 16
    cc = jnp.zeros((rows, d), F32).at[:b].set(c).at[b].set(c_ctx)
    tn = 1024
    mod = pl.pallas_call(
        _mod_kernel,
        grid=(N_MOD * d // tn,),
        in_specs=[pl.BlockSpec((rows, d), lambda j: (0, 0)),
                  pl.BlockSpec((None, d, tn), lambda j: (layer, 0, j)),
                  pl.BlockSpec((None, 1, tn), lambda j: (layer, 0, j))],
        out_specs=pl.BlockSpec((rows, tn), lambda j: (0, j)),
        out_shape=jax.ShapeDtypeStruct((rows, N_MOD * d), F32),
        compiler_params=_cparams("arbitrary"),
        name="mod_vectors",
    )(cc, w_mod, b_mod[:, None, :])
    lat = mod[:b].reshape(b, N_MOD, d)
    ctx = jnp.broadcast_to(mod[b].reshape(1, N_MOD, d), (b, N_MOD, d))
    tab = jnp.stack([lat, ctx], axis=1)
    return jnp.pad(tab, ((0, 0), (0, 0), (0, 8 - N_MOD), (0, 0)))


_C_QLAT, _C_KVLAT, _C_KR, _C_KRR, _C_NA, _C_HY, _C_LU, _C_LG, _C_END = (
    0, 256, 384, 512, 640, 640 + 3 * NA_HEADS * LANE, 640 + 1536 + 768, 640 + 1536 + 1024, 640 + 1536 + 1280)


def _stream_tile(lat_ref, ctx_ref, n_lat_tiles):
    return jnp.where(pl.program_id(1) < n_lat_tiles, lat_ref[...], ctx_ref[...])


def _stream_specs(stream, group, n_lat_tiles):
    lat_src, ctx_src, ctx_tile0 = stream
    d = lat_src.shape[-1]
    return [pl.BlockSpec((group, TM, d), lambda bb, i: (bb, jnp.minimum(i, n_lat_tiles - 1), 0)),
            pl.BlockSpec((group, TM, d), lambda bb, i: (bb, ctx_tile0 + jnp.maximum(i - n_lat_tiles, 0), 0))]


def _proj_kernel(xl_ref, xc_ref, mod_ref, g1_ref, w1_ref, gq_ref, gkv_ref, wuq_ref, wuqr_ref, wk_ref, wv_ref,
                 cos_ref, sin_ref, q_o, k_o, v_o, nq_o, nk_o, nv_o, hy_o, lu_o, lg_o, *, n_lat_tiles):
    group, tm, d = xl_ref.shape
    m = mod_ref[:, 0]
    x = _stream_tile(xl_ref, xc_ref, n_lat_tiles)
    h = _normmod(x, g1_ref[...], m[:, 1:2], m[:, 0:1]).reshape(group * tm, d)
    z = _dot(h.astype(BF16), w1_ref[...])
    qlat = z[:, _C_QLAT:_C_KVLAT]
    kvlat = z[:, _C_KVLAT:_C_KR]
    kr = z[:, _C_KR:_C_KRR]
    krr = z[:, _C_KRR:_C_NA]
    qn = qlat * lax.rsqrt(jnp.sum(qlat * qlat, axis=-1, keepdims=True) * (1.0 / MLA_Q_RANK) + NORM_EPS) * gq_ref[...]
    kvn = kvlat * lax.rsqrt(jnp.mean(kvlat * kvlat, axis=-1, keepdims=True) + NORM_EPS) * gkv_ref[...]
    qn = qn.astype(BF16)
    kvn = kvn.astype(BF16)
    q = _dot(qn, wuq_ref[...])
    qr = _dot(qn, wuqr_ref[...])
    kk = _dot(kvn, wk_ref[...])
    vv = _dot(kvn, wv_ref[...])
    cos = jnp.concatenate([cos_ref[...]] * group, axis=0)
    sin = jnp.concatenate([sin_ref[...]] * group, axis=0)
    krope = kr * cos + krr * sin
    den_lane = lax.broadcasted_iota(I32, cos.shape, 1) == SOFTMAX_DEN_LANE
    def put_heads(ref, hd, val):
        for g in range(group):
            ref[g, hd] = val[g * tm:(g + 1) * tm].astype(BF16)

    def put_pairs(ref, hd, val):
        off = (hd % 2) * LANE
        for g in range(group):
            ref[g, hd // 2, :, off:off + LANE] = val[g * tm:(g + 1) * tm].astype(BF16)

    for hd in range(MLA_HEADS):
        sl = slice(hd * LANE, (hd + 1) * LANE)
        put_heads(q_o, hd, q[:, sl] * cos + qr[:, sl] * sin)
        put_heads(k_o, hd, kk[:, sl] + krope)
        put_heads(v_o, hd, jnp.where(den_lane, 1.0, vv[:, sl]))
    for hd in range(NA_HEADS):
        for which, ref in enumerate((nq_o, nk_o, nv_o)):
            lo = _C_NA + (which * NA_HEADS + hd) * LANE
            blk = z[:, lo:lo + LANE]
            if which == 2:
                put_pairs(ref, hd, jnp.where(den_lane, 1.0, blk))
            else:
                put_heads(ref, hd, blk)
    hy_o[...] = z[:, _C_HY:_C_LU].reshape(group, tm, _C_LU - _C_HY)
    lu_o[...] = z[:, _C_LU:_C_LG].reshape(group, tm, _C_LG - _C_LU)
    lg_o[...] = z[:, _C_LG:_C_END].reshape(group, tm, _C_END - _C_LG)


def _pad_to(a, n, axis):
    pad = [(0, 0)] * a.ndim
    pad[axis] = (0, n - a.shape[axis])
    return jnp.pad(a, pad)


def _rot_cols(w):
    half = w.shape[-1] // 2
    return jnp.concatenate([-w[..., half:], w[..., :half]], axis=-1)


def _head_blocks(cols_per_head):
    out = []
    for pieces in cols_per_head:
        k = pieces[0][0].shape[0]
        blk = jnp.zeros((k, LANE), F32)
        for arr, off in pieces:
            blk = blk.at[:, off:off + arr.shape[1]].set(arr)
        out.append(blk)
    return jnp.concatenate(out, axis=1)


def _proj_weights(lp):
    w_in = lp['w_in']
    d = w_in.shape[0]
    o = 0
    parts = {}
    for name, n in (('q', MLA_Q_RANK), ('kv', MLA_KV_RANK), ('kr', MLA_ROPE), ('na', 3 * NA_HEADS * NA_HEAD_DIM),
                    ('hy', 3 * HY_WIDTH), ('lu', LRU_WIDTH), ('lg', LRU_WIDTH), ('gt', 4 * d)):
        parts[name] = w_in[:, o:o + n]
        o += n
    zeros = lambda n: jnp.zeros((d, n), F32)
    kr_blk = jnp.concatenate([zeros(MLA_NOPE), parts['kr'], zeros(LANE - MLA_NOPE - MLA_ROPE)], axis=1)
    krr_blk = jnp.concatenate([zeros(MLA_NOPE), _rot_cols(parts['kr']), zeros(LANE - MLA_NOPE - MLA_ROPE)], axis=1)
    na_scale = NA_HEAD_DIM ** -0.5
    na_cols = []
    for which in range(3):
        for hd in range(NA_HEADS):
            lo = (which * NA_HEADS + hd) * NA_HEAD_DIM
            blk = parts['na'][:, lo:lo + NA_HEAD_DIM] * (na_scale if which == 0 else 1.0)
            na_cols.append(_pad_to(blk, LANE, 1))
    w1 = jnp.concatenate([_pad_to(parts['q'], MLA_Q_PAD, 1), parts['kv'], kr_blk, krr_blk] + na_cols
                         + [parts['hy'], parts['lu'], parts['lg']], axis=1).astype(BF16)
    mla_scale = (MLA_NOPE + MLA_ROPE) ** -0.5
    wuq = _pad_to(lp['mla_w_uq'], MLA_Q_PAD, 0) * mla_scale
    dq = MLA_NOPE + MLA_ROPE
    wuq_main = _head_blocks([[(wuq[:, hd * dq:hd * dq + dq], 0)] for hd in range(MLA_HEADS)])
    wuq_rot = _head_blocks([[(_rot_cols(wuq[:, hd * dq + MLA_NOPE:hd * dq + dq]), MLA_NOPE)]
                            for hd in range(MLA_HEADS)])
    dkv = MLA_NOPE + MLA_V
    wukv = lp['mla_w_ukv']
    wk = _head_blocks([[(wukv[:, hd * dkv:hd * dkv + MLA_NOPE], 0)] for hd in range(MLA_HEADS)])
    wv = _head_blocks([[(wukv[:, hd * dkv + MLA_NOPE:hd * dkv + dkv], 0)] for hd in range(MLA_HEADS)])
    gq = _pad_to(lp['mla_g_q'].reshape(1, -1), MLA_Q_PAD, 1)
    gkv = lp['mla_g_kv'].reshape(1, -1)
    return dict(w1=w1, w_gate=(0.5 * parts['gt']).astype(BF16), gq=gq, gkv=gkv, wuq=wuq_main.astype(BF16),
                wuq_rot=wuq_rot.astype(BF16), wk=wk.astype(BF16), wv=wv.astype(BF16))


def _rope_tables(seq, n_tok):
    t = jnp.arange(seq, dtype=I32)
    row = (t // GRID_W).astype(F32)
    col = (t % GRID_W).astype(F32)
    n_axis = MLA_ROPE // 4
    inv_freq = ROPE_THETA ** (-jnp.arange(n_axis, dtype=F32) / n_axis)
    ang = jnp.concatenate([row[:, None] * inv_freq, col[:, None] * inv_freq], axis=-1)
    cos = jnp.concatenate([jnp.cos(ang), jnp.cos(ang)], axis=-1)
    sin = jnp.concatenate([jnp.sin(ang), jnp.sin(ang)], axis=-1)
    cos_t = jnp.ones((n_tok, LANE), F32).at[:seq, MLA_NOPE:MLA_NOPE + MLA_ROPE].set(cos)
    sin_t = jnp.zeros((n_tok, LANE), F32).at[:seq, MLA_NOPE:MLA_NOPE + MLA_ROPE].set(sin)
    return cos_t, sin_t


def _batch_group(b):
    return 2 if b % 2 == 0 else 1


def _kind_map(n_lat_tiles):
    return lambda b, i: (b, jnp.where(i >= n_lat_tiles, 1, 0), 0, 0)


def _project(stream, nt, modtab, g1, pw, rope, n_lat_tiles):
    b, _, d = stream[0].shape
    cos_t, sin_t = rope
    group = _batch_group(b)
    full = lambda a: pl.BlockSpec(a.shape, lambda bb, i: (0,) * a.ndim)
    head_out = lambda: pl.BlockSpec((group, MLA_HEADS, TM, LANE), lambda bb, i: (bb, 0, i, 0))
    tok_out = lambda n: pl.BlockSpec((group, TM, n), lambda bb, i: (bb, i, 0))
    head_shape = jax.ShapeDtypeStruct((b, MLA_HEADS, nt, LANE), BF16)
    pair_out = lambda: pl.BlockSpec((group, MLA_HEADS // 2, TM, 2 * LANE), lambda bb, i: (bb, 0, i, 0))
    pair_shape = jax.ShapeDtypeStruct((b, MLA_HEADS // 2, nt, 2 * LANE), BF16)
    tok_shape = lambda n: jax.ShapeDtypeStruct((b, nt, n), F32)
    weights = (g1, pw['w1'], pw['gq'], pw['gkv'], pw['wuq'], pw['wuq_rot'], pw['wk'], pw['wv'])
    return pl.pallas_call(
        functools.partial(_proj_kernel, n_lat_tiles=n_lat_tiles),
        grid=(b // group, nt // TM),
        in_specs=_stream_specs(stream, group, n_lat_tiles)
                 + [pl.BlockSpec((group, 1, 8, d), _kind_map(n_lat_tiles))]
                 + [full(w) for w in weights]
                 + [pl.BlockSpec((TM, LANE), lambda bb, i: (i, 0))] * 2,
        out_specs=[head_out(), head_out(), head_out(), head_out(), head_out(), pair_out(),
                   tok_out(3 * HY_WIDTH), tok_out(LRU_WIDTH), tok_out(LRU_WIDTH)],
        out_shape=[head_shape, head_shape, head_shape, head_shape, head_shape, pair_shape,
                   tok_shape(3 * HY_WIDTH), tok_shape(LRU_WIDTH), tok_shape(LRU_WIDTH)],
        compiler_params=_cparams("parallel", "parallel"),
        name="input_projection",
    )(stream[0], stream[1], modtab, *weights, cos_t, sin_t)


SOFTMAX_DEN_LANE = 64


def _softmax_pv(parts, lane_off=0):
    m = None
    for s, _ in parts:
        mm = jnp.max(s, axis=-1, keepdims=True)
        m = mm if m is None else jnp.maximum(m, mm)
    acc = None
    for s, v in parts:
        o = _dot(jnp.exp(s - m).astype(BF16), v)
        acc = o if acc is None else acc + o
    den = lane_off + SOFTMAX_DEN_LANE
    return acc[:, lane_off:lane_off + LANE] / acc[:, den:den + 1]


def _mla_kernel(q_ref, k_ref, v_ref, o_ref, *, seq, n_lat_tiles):
    i = pl.program_id(1)
    nt = k_ref.shape[2]

    def attend(lo, hi):
        for hd in range(MLA_HEADS):
            s = _dot_nt(q_ref[0, hd], k_ref[0, hd, lo:hi, :])
            o = _softmax_pv([(s, v_ref[0, hd, lo:hi, :])])
            o_ref[0, :, hd * LANE:(hd + 1) * LANE] = o.astype(BF16)

    @pl.when(i < n_lat_tiles)
    def _():
        attend(0, nt)

    @pl.when(i >= n_lat_tiles)
    def _():
        attend(seq, nt)


def _mla_attention(q, k, v, seq, tiles_used):
    b, h, nt, _ = q.shape
    kv_spec = pl.BlockSpec((1, h, nt, LANE), lambda bb, i: (bb, 0, 0, 0))
    return pl.pallas_call(
        functools.partial(_mla_kernel, seq=seq, n_lat_tiles=seq // TM),
        grid=(b, tiles_used),
        in_specs=[pl.BlockSpec((1, h, TM, LANE), lambda bb, i: (bb, 0, i, 0)), kv_spec, kv_spec],
        out_specs=pl.BlockSpec((1, TM, h * LANE), lambda bb, i: (bb, i, 0)),
        out_shape=jax.ShapeDtypeStruct((b, tiles_used * TM, h * LANE), BF16),
        compiler_params=_cparams("parallel", "arbitrary"),
        name="mla_attention",
    )(q, k, v)


def _na_bias_tables(rpb, rows):
    n_blk = rows // NA_TILE_ROWS
    col = np.arange(GRID_W)
    c0 = np.clip(col - NA_WIN_C // 2, 0, GRID_W - NA_WIN_C)
    in_win = (col[None, :] >= c0[:, None]) & (col[None, :] < c0[:, None] + NA_WIN_C)
    dc = np.clip(col[None, :] - col[:, None], 1 - NA_WIN_C, NA_WIN_C - 1) + NA_WIN_C - 1
    rpb = rpb.astype(F32)
    tables = []
    for j in (0, 1, n_blk - 1):
        w0 = min(max(NA_TILE_ROWS * j - NA_WIN_R // 2, 0), rows - NA_KEY_ROWS)
        r = NA_TILE_ROWS * j + np.arange(NA_TILE_ROWS)
        kr = w0 + np.arange(NA_KEY_ROWS)
        r0 = np.clip(r - NA_WIN_R // 2, 0, rows - NA_WIN_R)
        row_ok = (kr[None, :] >= r0[:, None]) & (kr[None, :] < r0[:, None] + NA_WIN_R)
        dr = np.clip(kr[None, :] - r[:, None] + NA_WIN_R - 1, 0, 2 * NA_WIN_R - 2)
        oh_r = jnp.asarray(np.eye(2 * NA_WIN_R - 1, dtype=np.float32)[dr.reshape(-1)])
        oh_c = jnp.asarray(np.eye(2 * NA_WIN_C - 1, dtype=np.float32)[dc.reshape(-1)])
        bias = jnp.einsum('ar,hrc,bc->hab', oh_r, rpb, oh_c, precision=lax.Precision.HIGHEST)
        bias = bias.reshape(NA_HEADS, NA_TILE_ROWS, NA_KEY_ROWS, GRID_W, GRID_W)
        mask = row_ok[:, :, None, None] & in_win[None, None, :, :]
        bias = jnp.where(jnp.asarray(mask)[None], bias, -jnp.inf)
        tables.append(bias.transpose(0, 1, 3, 2, 4).reshape(NA_HEADS, TM, NA_KEYS))
    return jnp.stack(tables)


def _na_kernel(q_ref, k_ref, v_ref, bias_ref, o_ref, *, seq, n_lat_tiles):
    i = pl.program_id(1)
    nt = k_ref.shape[2]
    rows = seq // GRID_W

    @pl.when(i < n_lat_tiles)
    def _():
        w0 = jnp.clip(NA_TILE_ROWS * i - NA_WIN_R // 2, 0, rows - NA_KEY_ROWS)
        start = pl.multiple_of(w0 * GRID_W, GRID_W)
        for hd in range(NA_HEADS):
            q = q_ref[0, hd]
            s_loc = _dot_nt(q, k_ref[0, hd, pl.ds(start, NA_KEYS), :]) + bias_ref[0, hd]
            s_ctx = _dot_nt(q, k_ref[0, hd, seq:nt, :])
            o = _softmax_pv([(s_loc, v_ref[0, hd // 2, pl.ds(start, NA_KEYS), :]),
                             (s_ctx, v_ref[0, hd // 2, seq:nt, :])], (hd % 2) * LANE)
            o_ref[0, :, hd * LANE:(hd + 1) * LANE] = o.astype(BF16)

    @pl.when(i >= n_lat_tiles)
    def _():
        for hd in range(NA_HEADS):
            s = _dot_nt(q_ref[0, hd], k_ref[0, hd, seq:nt, :])
            o = _softmax_pv([(s, v_ref[0, hd // 2, seq:nt, :])], (hd % 2) * LANE)
            o_ref[0, :, hd * LANE:(hd + 1) * LANE] = o.astype(BF16)


def _na_attention(q, k, v, bias, seq, tiles_used):
    b, h, nt, _ = q.shape
    n_lat = seq // TM
    kv_spec = pl.BlockSpec((1, h, nt, LANE), lambda bb, i: (bb, 0, 0, 0))
    cfg = lambda bb, i: (jnp.where(i == 0, 0, jnp.where(i >= n_lat - 1, 2, 1)), 0, 0, 0)
    return pl.pallas_call(
        functools.partial(_na_kernel, seq=seq, n_lat_tiles=n_lat),
        grid=(b, tiles_used),
        in_specs=[pl.BlockSpec((1, h, TM, LANE), lambda bb, i: (bb, 0, i, 0)), kv_spec,
                  pl.BlockSpec((1, h // 2, nt, 2 * LANE), lambda bb, i: (bb, 0, 0, 0)),
                  pl.BlockSpec((1, h, TM, NA_KEYS), cfg)],
        out_specs=pl.BlockSpec((1, TM, h * LANE), lambda bb, i: (bb, i, 0)),
        out_shape=jax.ShapeDtypeStruct((b, tiles_used * TM, h * LANE), BF16),
        compiler_params=_cparams("parallel", "arbitrary"),
        name="neighbourhood_attention",
    )(q, k, v, bias)


def _hp_dot(a, b):
    return jnp.dot(a, b, preferred_element_type=F32, precision=lax.Precision.HIGHEST)


def _hy_filter_kernel(f_ref, w1_ref, b1_ref, w2_ref, b2_ref, w3_ref, dec_ref, o_ref, obf_ref):
    f = f_ref[...]
    h = jnp.sin(_hp_dot(f, w1_ref[...]) + b1_ref[...])
    h = jnp.sin(_hp_dot(h, w2_ref[...]) + b2_ref[...])
    h = _hp_dot(h, w3_ref[...])
    h = h * jnp.exp(-f[:, 0:1] * jnp.abs(dec_ref[...]))
    o_ref[...] = h
    obf_ref[...] = h.astype(BF16)


def _hy_pos_features(length):
    t = jnp.linspace(0.0, 1.0, length, dtype=F32)[:, None]
    w = 2.0 * math.pi * jnp.arange(length, dtype=F32)[:, None] / length
    f = jnp.linspace(1e-4, HY_BANDS - 1, HY_BANDS, dtype=F32)[None, :]
    z = w * f
    return jnp.concatenate([t, jnp.cos(z), -jnp.sin(z)], axis=-1)


DFT_ROW_STEP = 64


def _col_block(nc, cap):
    return min(nc, cap)


def _hyp_pre_kernel(z_ref, w_ref, b_ref, lat_o, lat_bf_o, ctx_o, ctx_bf_o, *, seq, ctx_len):
    w = w_ref[...]
    bias = b_ref[...]
    for lo, length, o_ref, obf_ref in ((0, seq, lat_o, lat_bf_o), (seq, ctx_len, ctx_o, ctx_bf_o)):
        h = length // 2
        even = z_ref[0, pl.ds(lo, h, stride=2), :]
        odd = z_ref[0, pl.ds(lo + 1, h, stride=2), :]
        row = lax.broadcasted_iota(I32, even.shape, 0)
        odd_prev = jnp.where(row == 0, 0.0, pltpu.roll(odd, 1, axis=0))
        even_next = jnp.where(row == h - 1, 0.0, pltpu.roll(even, h - 1, axis=0))
        y_even = bias + w[0:1] * odd_prev + w[1:2] * even + w[2:3] * odd
        y_odd = bias + w[0:1] * even + w[1:2] * odd + w[2:3] * even_next
        for r, y in enumerate((y_even, y_odd)):
            o_ref[0, r] = y
            obf_ref[0, r] = y.astype(BF16)


def _hyp_pre(hy, w, bvec, seq):
    b, nt, _ = hy.shape
    ctx_len = nt - seq
    wpad = _pad_to(w, 8, 0)
    per_plane = HY_WIDTH // LANE
    out_specs, out_shape = [], []
    for length in (seq, ctx_len):
        for dt in (F32, BF16):
            out_specs.append(pl.BlockSpec((1, 2, length // 2, LANE),
                                          lambda bb, g: (g // per_plane, 0, 0, bb * per_plane + g % per_plane)))
            out_shape.append(jax.ShapeDtypeStruct((3, 2, length // 2, b * HY_WIDTH), dt))
    return pl.pallas_call(
        functools.partial(_hyp_pre_kernel, seq=seq, ctx_len=ctx_len),
        grid=(b, 3 * per_plane),
        in_specs=[pl.BlockSpec((1, nt, LANE), lambda bb, g: (bb, 0, g)),
                  pl.BlockSpec((8, LANE), lambda bb, g: (0, g)),
                  pl.BlockSpec((1, LANE), lambda bb, g: (0, g))],
        out_specs=out_specs,
        out_shape=out_shape,
        compiler_params=_cparams("parallel", "parallel"),
        name="hyena_short_conv",
    )(hy, wpad, bvec.reshape(1, -1))


def _hyp_filters(length, lp):
    feats = _hy_pos_features(length)
    feats = feats.reshape(length // 2, 2, HY_EMB).transpose(1, 0, 2).reshape(length, HY_EMB)
    n_out = HY_ORDER * 2 * HY_WIDTH
    tl = min(length // 2, 512)
    full = lambda a: pl.BlockSpec(a.shape, lambda i: (0,) * a.ndim)
    args = (lp['hy_w1'], lp['hy_b1'].reshape(1, -1), lp['hy_w2'], lp['hy_b2'].reshape(1, -1), lp['hy_w3'],
            lp['hy_decay'].reshape(1, n_out))
    filt, filt_bf = pl.pallas_call(
        _hy_filter_kernel,
        grid=(length // tl,),
        in_specs=[pl.BlockSpec((tl, HY_EMB), lambda i: (i, 0))] + [full(a) for a in args],
        out_specs=[pl.BlockSpec((tl, n_out), lambda i: (i, 0))] * 2,
        out_shape=[jax.ShapeDtypeStruct((length, n_out), F32), jax.ShapeDtypeStruct((length, n_out), BF16)],
        compiler_params=_cparams("parallel"),
        name="hyena_filter_mlp",
    )(feats, *args)
    return filt.reshape(2, length // 2, n_out), filt_bf.reshape(2, length // 2, n_out)


def _hyp_matrix_kernel(c1_ref, s1_ref, c2_ref, s2_ref, ck_ref, sk_ref, ckr_ref, skr_ref,
                       ce_o, se_o, co_o, so_o, set_o, cot_o, sot_o):
    c1, s1 = c1_ref[0], s1_ref[0]
    c2, s2 = c2_ref[...], s2_ref[...]
    cos = c1 * c2 - s1 * s2
    sin = s1 * c2 + c1 * s2
    row = lax.broadcasted_iota(I32, cos.shape, 0)
    col = lax.broadcasted_iota(I32, cos.shape, 1)
    first_row = (row + pl.program_id(0) * DFT_ROW_STEP) == 0
    alt_col = jnp.where(col % 2 == 0, 1.0, -1.0)
    alt_row = jnp.where(row % 2 == 0, 1.0, -1.0)
    ck, sk = ck_ref[...], sk_ref[...]
    ce_o[...] = cos.astype(BF16)
    se_o[...] = jnp.where(first_row, alt_col, -sin).astype(BF16)
    co_o[...] = (cos * ck - sin * sk).astype(BF16)
    so_o[...] = jnp.where(first_row, alt_col, -(sin * ck + cos * sk)).astype(BF16)
    ckr, skr = ckr_ref[...], skr_ref[...]
    set_o[...] = jnp.where(col == 0, alt_row, -sin).astype(BF16)
    cot_o[...] = (cos * ckr - sin * skr).astype(BF16)
    sot_o[...] = jnp.where(col == 0, alt_row, -(sin * ckr + cos * skr)).astype(BF16)


def _hyp_matrices(length):
    h = length // 2
    step = DFT_ROW_STEP
    m = jnp.arange(h, dtype=I32)

    def trig(kv):
        ang = ((kv[:, None] * m[None, :]) % length).astype(F32) * (2.0 * math.pi / length)
        return jnp.cos(ang), jnp.sin(ang)

    c1, s1 = trig(jnp.arange(h // step, dtype=I32) * step)
    c2, s2 = trig(jnp.arange(step, dtype=I32))
    half_angle = m.astype(F32) * (math.pi / length)
    ck, sk = jnp.cos(half_angle), jnp.sin(half_angle)
    coarse = pl.BlockSpec((1, 1, h), lambda j: (j, 0, 0))
    fine = pl.BlockSpec((step, h), lambda j: (0, 0))
    per_row = pl.BlockSpec((step, 1), lambda j: (j, 0))
    per_col = pl.BlockSpec((1, h), lambda j: (0, 0))
    out = pl.BlockSpec((step, h), lambda j: (j, 0))
    ce, se, co, so, se_t, co_t, so_t = pl.pallas_call(
        _hyp_matrix_kernel,
        grid=(h // step,),
        in_specs=[coarse, coarse, fine, fine, per_row, per_row, per_col, per_col],
        out_specs=[out] * 7,
        out_shape=[jax.ShapeDtypeStruct((h, h), BF16)] * 7,
        compiler_params=_cparams("parallel"),
        name="dft_matrices",
    )(c1[:, None, :], s1[:, None, :], c2, s2, ck[:, None], sk[:, None], ck[None, :], sk[None, :])
    return dict(fwd=(ce, se, co, so), inv=(ce, se_t, co_t, so_t))


def _hyp_fwd_kernel(ce_ref, se_ref, co_ref, so_ref, xe_ref, xo_ref, *rest, with_taps):
    xe, xo = xe_ref[...], xo_ref[...]
    a_re, a_im = _dot(ce_ref[...], xe), _dot(se_ref[...], xe)
    b_re, b_im = _dot(co_ref[...], xo), _dot(so_ref[...], xo)
    if not with_taps:
        for ref, val in zip(rest, (a_re, a_im, b_re, b_im)):
            ref[...] = val
        return
    t1re_ref, t1im_ref, t2re_ref, t2im_ref, sp_ref, gere_o, geim_o, gore_o, goim_o = rest
    t1re, t1im, t2re, t2im = t1re_ref[...], t1im_ref[...], t2re_ref[...], t2im_ref[...]
    sp = sp_ref[...]
    first_block = pl.program_id(1) == 0
    row8 = lax.broadcasted_iota(I32, (8, HY_WIDTH), 0)
    for bb in range(xe.shape[1] // HY_WIDTH):
        sl = slice(bb * HY_WIDTH, (bb + 1) * HY_WIDTH)
        are, aim, bre, bim = a_re[:, sl], a_im[:, sl], b_re[:, sl], b_im[:, sl]
        u1re, u1im = are + bre, aim + bim
        u2re, u2im = are - bre, bim - aim
        z1re, z1im = u1re * t1re - u1im * t1im, u1re * t1im + u1im * t1re
        z2re, z2im = u2re * t2re - u2im * t2im, u2re * t2im + u2im * t2re
        gere_o[:, sl] = (z1re + z2re).astype(BF16)
        geim_o[:, sl] = (z1im - z2im).astype(BF16)
        gore_o[:, sl] = (z1re - z2re).astype(BF16)
        goim_o[:, sl] = (z1im + z2im).astype(BF16)

        @pl.when(first_block)
        def _():
            u0, ul = u1re[0:8], u2re[0:8]
            a_s, b_s = aim[0:8], bim[0:8]
            dc, ny, mre, mim = sp[0:1], sp[1:2], sp[2:3], sp[3:4]
            first = row8 == 0
            gere_o[0:8, sl] = jnp.where(first, u0 * dc + ul * ny, (z1re + z2re)[0:8]).astype(BF16)
            gore_o[0:8, sl] = jnp.where(first, u0 * dc - ul * ny, (z1re - z2re)[0:8]).astype(BF16)
            geim_o[0:8, sl] = jnp.where(first, a_s * mre + b_s * mim, (z1im - z2im)[0:8]).astype(BF16)
            goim_o[0:8, sl] = jnp.where(first, b_s * mre - a_s * mim, (z1im + z2im)[0:8]).astype(BF16)


def _hyp_fwd(mats, x, plane, taps=None):
    _, _, h, nc = x.shape
    tk = min(h, 256)
    cb = _col_block(nc, 1024)
    grid = (nc // cb, h // tk)
    m_spec = pl.BlockSpec((tk, h), lambda c, j: (j, 0))
    x_spec = lambda r: pl.BlockSpec((None, None, h, cb), lambda c, j: (plane, r, 0, c))
    o_spec = pl.BlockSpec((tk, cb), lambda c, j: (j, c))
    if taps is None:
        return pl.pallas_call(
            functools.partial(_hyp_fwd_kernel, with_taps=False),
            grid=grid, in_specs=[m_spec] * 4 + [x_spec(0), x_spec(1)], out_specs=[o_spec] * 4,
            out_shape=[jax.ShapeDtypeStruct((h, nc), F32)] * 4,
            compiler_params=_cparams("parallel", "arbitrary"),
            name="hyena_dft_filters",
        )(*mats['fwd'], x, x)
    t_spec = pl.BlockSpec((tk, HY_WIDTH), lambda c, j: (j, 0))
    sp_spec = pl.BlockSpec((8, HY_WIDTH), lambda c, j: (0, 0))
    return pl.pallas_call(
        functools.partial(_hyp_fwd_kernel, with_taps=True),
        grid=grid, in_specs=[m_spec] * 4 + [x_spec(0), x_spec(1)] + [t_spec] * 4 + [sp_spec],
        out_specs=[o_spec] * 4,
        out_shape=[jax.ShapeDtypeStruct((h, nc), BF16)] * 4,
        compiler_params=_cparams("parallel", "arbitrary"),
        name="hyena_dft_forward",
    )(*mats['fwd'], x, x, *taps)


def _hyp_inv_kernel(ce_ref, set_ref, cot_ref, sot_ref, gere_ref, geim_ref, gore_ref, goim_ref,
                    gate_ref, prev_ref, bias_ref, *outs, last):
    conv = (_dot(ce_ref[...], gere_ref[...]) + _dot(set_ref[...], geim_ref[...]),
            _dot(cot_ref[...], gore_ref[...]) + _dot(sot_ref[...], goim_ref[...]))
    bias = bias_ref[...]
    for r in range(2):
        y = gate_ref[r] * (conv[r] + prev_ref[r] * bias)
        if last:
            (tok_o,) = outs
            for bb in range(y.shape[1] // HY_WIDTH):
                tok_o[bb, :, r * HY_WIDTH:(r + 1) * HY_WIDTH] = y[:, bb * HY_WIDTH:(bb + 1) * HY_WIDTH].astype(BF16)
        else:
            y_o, ybf_o = outs
            y_o[r] = y
            ybf_o[r] = y.astype(BF16)


def _hyp_inv(mats, g, gate, gate_plane, prev, prev_plane, bias_row, last):
    h, nc = g[0].shape
    tm = min(h, 256)
    cb = _col_block(nc, 512)
    grid = (nc // cb, h // tm)
    m_spec = pl.BlockSpec((tm, h), lambda c, i: (i, 0))
    g_spec = pl.BlockSpec((h, cb), lambda c, i: (0, c))
    e_spec = lambda plane: pl.BlockSpec((None, 2, tm, cb), lambda c, i: (plane, 0, i, c))
    b_spec = pl.BlockSpec((1, cb), lambda c, i: (0, c))
    if last:
        out_specs = [pl.BlockSpec((cb // HY_WIDTH, tm, 2 * HY_WIDTH), lambda c, i: (c, i, 0))]
        out_shape = [jax.ShapeDtypeStruct((nc // HY_WIDTH, h, 2 * HY_WIDTH), BF16)]
    else:
        out_specs = [e_spec(0), e_spec(0)]
        out_shape = [jax.ShapeDtypeStruct((1, 2, h, nc), F32), jax.ShapeDtypeStruct((1, 2, h, nc), BF16)]
    return pl.pallas_call(
        functools.partial(_hyp_inv_kernel, last=last),
        grid=grid, in_specs=[m_spec] * 4 + [g_spec] * 4 + [e_spec(gate_plane), e_spec(prev_plane), b_spec],
        out_specs=out_specs, out_shape=out_shape,
        compiler_params=_cparams("parallel", "arbitrary"),
        name="hyena_dft_inverse",
    )(*mats['inv'], *g, gate, prev, bias_row)


def _hyp_tap_tables(spec, filt, length):
    a_re, a_im, b_re, b_im = spec
    w = HY_WIDTH
    inv_n = 1.0 / (2 * length)
    tables = []
    for o in range(HY_ORDER):
        f_sl = slice((2 * o) * w, (2 * o + 1) * w)
        r_sl = slice((2 * o + 1) * w, (2 * o + 2) * w)
        hb0 = filt[0, 0:1, r_sl]
        f1re = (a_re + b_re)[:, f_sl] + (a_re + b_re)[:, r_sl] - hb0
        f1im = (a_im + b_im)[:, f_sl] - (a_im + b_im)[:, r_sl]
        f2re = (a_re - b_re)[:, f_sl] + (a_re - b_re)[:, r_sl] - hb0
        f2im = (b_im - a_im)[:, f_sl] - (b_im - a_im)[:, r_sl]
        dc = f1re[0:1]
        ny = f2re[0:1]
        mid_re = a_im[0:1, f_sl] + a_im[0:1, r_sl] - hb0
        mid_im = -b_im[0:1, f_sl] + b_im[0:1, r_sl]
        sp = jnp.concatenate([dc * inv_n, ny * inv_n, mid_re * (2 * inv_n), mid_im * (2 * inv_n),
                              jnp.zeros((4, w), F32)], axis=0)
        tables.append((f1re * (2 * inv_n), f1im * (2 * inv_n), f2re * (2 * inv_n), f2im * (2 * inv_n), sp))
    return tables


def _hyena_seq(mats, vx, vx_bf, lp, n_batch):
    h = vx.shape[2]
    length = 2 * h
    filt, filt_bf = _hyp_filters(length, lp)
    spec = _hyp_fwd(mats, filt_bf[None], 0)
    tables = _hyp_tap_tables(spec, filt, length)
    bias = lp['hy_bias'].astype(F32)
    y, y_bf = vx, vx_bf
    for o in range(HY_ORDER):
        g = _hyp_fwd(mats, y_bf, 0, tables[o])
        bias_row = jnp.tile(bias[o][None, :], (1, n_batch))
        last = o == HY_ORDER - 1
        res = _hyp_inv(mats, g, vx, o + 1, y, 0, bias_row, last)
        if last:
            return res[0].reshape(n_batch, length, HY_WIDTH)
        y, y_bf = res


def _lru_kernel(u_ref, g_ref, cw_ref, cb_ref, wa_ref, ba_ref, wx_ref, bx_ref, lam_ref, o_ref,
                pad_ref, y_ref, *, seq, ctx_len):
    tc = LRU_CHUNK
    halo = LRU_HALO
    width = LRU_WIDTH
    lat_off = halo
    ctx_off = 2 * halo + seq
    zero = jnp.zeros((halo, width), F32)
    pad_ref[0:halo, :] = zero
    pad_ref[lat_off:lat_off + seq, :] = u_ref[0, 0:seq, :]
    pad_ref[lat_off + seq:ctx_off, :] = zero
    pad_ref[ctx_off:ctx_off + ctx_len, :] = u_ref[0, seq:seq + ctx_len, :]
    pad_ref[ctx_off + ctx_len:ctx_off + ctx_len + halo, :] = zero
    row = lax.broadcasted_iota(I32, (tc, width), 0)
    n_win = tc + 2 * halo

    def chunk(pad_off, y_off, s, carry, d):
        wstart = pl.multiple_of(pad_off + s - halo, 8)
        win = pad_ref[pl.ds(wstart, n_win), :]
        cw = cw_ref[d]
        xc = cb_ref[d]
        for k in range(LRU_CONV):
            shift = (LRU_CONV - 1 - k) if d == 0 else -k
            rolled = win if shift == 0 else pltpu.roll(win, shift % n_win, axis=0)
            xc = xc + cw[k:k + 1] * rolled[halo:halo + tc]
        xb = xc.astype(BF16)
        r = _sigmoid(_dot(xb, wa_ref[d]) + ba_ref[d])
        gi = _sigmoid(_dot(xb, wx_ref[d]) + bx_ref[d])
        lam = lam_ref[d]
        softplus = jnp.maximum(-lam, 0.0) + jnp.log1p(jnp.exp(-jnp.abs(lam)))
        log_a = -LRU_C * r * softplus
        a = jnp.exp(log_a)
        bt = jnp.sqrt(-jnp.tanh(log_a) * (a * a + 1.0)) * (gi * xc)
        sft = 1
        while sft < tc:
            if d == 0:
                keep = row >= sft
                a_s = jnp.where(keep, pltpu.roll(a, sft, axis=0), 1.0)
                b_s = jnp.where(keep, pltpu.roll(bt, sft, axis=0), 0.0)
            else:
                keep = row < tc - sft
                a_s = jnp.where(keep, pltpu.roll(a, tc - sft, axis=0), 1.0)
                b_s = jnp.where(keep, pltpu.roll(bt, tc - sft, axis=0), 0.0)
            bt = a * b_s + bt
            a = a * a_s
            sft *= 2
        h = a * carry + bt
        yo = pl.multiple_of(y_off + s, 8)
        if d == 0:
            y_ref[pl.ds(yo, tc), :] = h
            return h[tc - 1:tc]
        y_ref[pl.ds(yo, tc), :] = y_ref[pl.ds(yo, tc), :] + h
        return h[0:1]

    n_lat = seq // tc
    n_ctx = ctx_len // tc
    for d in range(2):
        carry = jnp.zeros((1, width), F32)
        order = range(n_ctx) if d == 0 else range(n_ctx - 1, -1, -1)
        for c in order:
            carry = chunk(ctx_off, seq, c * tc, carry, d)

        def body(j, cr, d=d):
            jj = j if d == 0 else n_lat - 1 - j
            return chunk(lat_off, 0, jj * tc, cr, d)

        lax.fori_loop(0, n_lat, body, carry)
    o_ref[0] = (y_ref[...] * _gelu_tanh(g_ref[0])).astype(BF16)


def _block_diag(w):
    nd, nb, c, _ = w.shape
    out = jnp.zeros((nd, nb * c, nb * c), w.dtype)
    for n in range(nb):
        out = out.at[:, n * c:(n + 1) * c, n * c:(n + 1) * c].set(w[:, n])
    return out


def _lru_mixer(lu, lg, lp, seq):
    b, nt, w = lu.shape
    ctx_len = nt - seq
    row3 = lambda a: a.reshape(2, 1, w)
    args = (_pad_to(lp['lru_conv_w'], 8, 1), row3(lp['lru_conv_b']), _block_diag(lp['lru_wa']).astype(BF16),
            row3(lp['lru_ba']), _block_diag(lp['lru_wx']).astype(BF16), row3(lp['lru_bx']), row3(lp['lru_lambda']))
    full = lambda a: pl.BlockSpec(a.shape, lambda bb: (0,) * a.ndim)
    tok = pl.BlockSpec((1, nt, w), lambda bb: (bb, 0, 0))
    return pl.pallas_call(
        functools.partial(_lru_kernel, seq=seq, ctx_len=ctx_len),
        grid=(b,),
        in_specs=[tok, tok] + [full(a) for a in args],
        out_specs=tok,
        out_shape=jax.ShapeDtypeStruct((b, nt, w), BF16),
        scratch_shapes=[pltpu.VMEM((nt + 3 * LRU_HALO, w), F32), pltpu.VMEM((nt, w), F32)],
        compiler_params=_cparams("parallel"),
        name="rglru_scan",
    )(lu, lg, *args)


def _merge_kernel(xl_ref, xc_ref, mod_ref, g1_ref, a_ref, b_ref, c_ref, d_ref, wg_ref, wa_ref, wb_ref, wc_ref,
                  wd_ref, wo_ref, o_ref, *, n_lat_tiles):
    group, tm, dm = xl_ref.shape
    m = mod_ref[:, 0]
    x = _stream_tile(xl_ref, xc_ref, n_lat_tiles)
    h = _normmod(x, g1_ref[...], m[:, 1:2], m[:, 0:1]).reshape(group * tm, dm).astype(BF16)
    acc = None
    for k, (br, w) in enumerate(((a_ref, wa_ref), (b_ref, wb_ref), (c_ref, wc_ref), (d_ref, wd_ref))):
        term = (1.0 + jnp.tanh(_dot(h, wg_ref[:, k * dm:(k + 1) * dm]))) * _dot(
            br[...].reshape(group * tm, br.shape[-1]), w[...])
        acc = term if acc is None else acc + term
    y = _dot(acc.astype(BF16), wo_ref[...])
    o_ref[...] = x + m[:, 2:3] * y.reshape(group, tm, dm)


def _merge(stream, modtab, g1, branches, w_gate, lp, n_lat_tiles, tiles_used):
    b, _, d = stream[0].shape
    wbr = lp['w_branch']
    head_rows = lambda w, dv: jnp.concatenate(
        [_pad_to(w[hd * dv:(hd + 1) * dv], LANE, 0) for hd in range(4)], axis=0)
    wbr = 0.5 * wbr
    weights = (head_rows(wbr[0], MLA_V).astype(BF16), head_rows(wbr[1], NA_HEAD_DIM).astype(BF16),
               wbr[2].astype(BF16), wbr[3].astype(BF16), lp['w_out'].astype(BF16))
    group = _batch_group(b)
    full = lambda a: pl.BlockSpec(a.shape, lambda bb, i: (0,) * a.ndim)
    tok = lambda n: pl.BlockSpec((group, TM, n), lambda bb, i: (bb, i, 0))
    return pl.pallas_call(
        functools.partial(_merge_kernel, n_lat_tiles=n_lat_tiles),
        grid=(b // group, tiles_used),
        in_specs=_stream_specs(stream, group, n_lat_tiles)
                 + [pl.BlockSpec((group, 1, 8, d), _kind_map(n_lat_tiles)), full(g1)]
                 + [tok(br.shape[-1]) for br in branches] + [full(w_gate)] + [full(w) for w in weights],
        out_specs=tok(d),
        out_shape=jax.ShapeDtypeStruct((b, tiles_used * TM, d), F32),
        compiler_params=_cparams("parallel", "parallel"),
        name="merge_branches",
    )(stream[0], stream[1], modtab, g1, *branches, w_gate, *weights)


U32 = jnp.uint32
EXPERT_ROWS = 1024


def _pack_pair(x):
    n = x.shape[-1] // 2
    hi = lax.bitcast_convert_type(x[:, :n].astype(BF16).astype(F32), U32)
    lo = lax.bitcast_convert_type(x[:, n:].astype(BF16).astype(F32), U32)
    return hi | (lo >> 16)


def _unpack_pair(p):
    hi = lax.bitcast_convert_type(p & jnp.uint32(0xFFFF0000), F32)
    lo = lax.bitcast_convert_type(p << 16, F32)
    return hi, lo


def _router_kernel(x_ref, mod_ref, g2_ref, rw_ref, rb_ref, tri_ref, h2_o, idx_o, wts_o, rank_o, cnt_o, carry):
    i = pl.program_id(0)

    @pl.when(i == 0)
    def _():
        carry[...] = jnp.zeros_like(carry)

    m = mod_ref[0, 0]
    h2 = _normmod(x_ref[0], g2_ref[...], m[4:5], m[3:4])
    half = h2.shape[-1] // 2
    h2_o[0] = _pack_pair(h2[:, :half])
    h2_o[1] = _pack_pair(h2[:, half:])
    logits = lax.dot_general(rw_ref[...], h2, (((1,), (1,)), ((), ())), preferred_element_type=F32,
                             precision=lax.Precision.HIGHEST)
    scores = _sigmoid(logits)
    biased = scores + rb_ref[...]
    expert = lax.broadcasted_iota(I32, scores.shape, 0)
    picks = []
    onehot_all = jnp.zeros(scores.shape, F32)
    for _ in range(TOP_K):
        best = jnp.max(biased, axis=0, keepdims=True)
        arg = jnp.min(jnp.where(biased == best, expert, N_EXPERTS), axis=0, keepdims=True)
        hit = expert == arg
        sel = jnp.sum(jnp.where(hit, scores, 0.0), axis=0, keepdims=True)
        biased = jnp.where(hit, -jnp.inf, biased)
        onehot_all = onehot_all + jnp.where(hit, 1.0, 0.0)
        picks.append((arg, hit, sel))
    total = picks[0][2]
    for _, _, sel in picks[1:]:
        total = total + sel
    earlier = _dot(onehot_all.astype(BF16), tri_ref[...]) + carry[...]
    pad_rows = TOPK_PAD - TOP_K
    ranks = [jnp.sum(jnp.where(hit, earlier, 0.0), axis=0, keepdims=True).astype(I32) for _, hit, _ in picks]
    scale = ROUTED_SCALE / total
    idx_o[...] = jnp.concatenate([arg for arg, _, _ in picks] + [jnp.zeros((pad_rows, TM), I32)], axis=0)
    wts_o[...] = jnp.concatenate([sel * scale for _, _, sel in picks] + [jnp.zeros((pad_rows, TM), F32)], axis=0)
    rank_o[...] = jnp.concatenate(ranks + [jnp.zeros((pad_rows, TM), I32)], axis=0)
    carry[...] = carry[...] + jnp.sum(onehot_all, axis=1, keepdims=True)
    cnt_o[...] = carry[...]


def _tile_maps(tiles_used, n_lat_tiles):
    tok = lambda i: (i // tiles_used, i % tiles_used, 0)
    mod = lambda i: (i // tiles_used, jnp.where(i % tiles_used >= n_lat_tiles, 1, 0), 0, 0)
    return tok, mod


def _route(x1, modtab, g2, lp, n_lat_tiles, tiles_used):
    b, _, d = x1.shape
    n_tiles = b * tiles_used
    t = n_tiles * TM
    rw = lp['router_w'].T
    rb = lp['router_bias'].reshape(-1, 1)
    tri = (np.arange(TM)[:, None] < np.arange(TM)[None, :]).astype(np.float32)
    tri = jnp.asarray(tri, BF16)
    per_tok = lambda: pl.BlockSpec((TOPK_PAD, TM), lambda i: (0, i))
    full = lambda a: pl.BlockSpec(a.shape, lambda i: (0,) * a.ndim)
    tok, mod = _tile_maps(tiles_used, n_lat_tiles)
    return pl.pallas_call(
        _router_kernel,
        grid=(n_tiles,),
        in_specs=[pl.BlockSpec((1, TM, d), tok), pl.BlockSpec((1, 1, 8, d), mod),
                  full(g2), full(rw), full(rb), full(tri)],
        out_specs=[pl.BlockSpec((2, TM, d // 4), lambda i: (0, i, 0)), per_tok(), per_tok(), per_tok(),
                   pl.BlockSpec((N_EXPERTS, 1), lambda i: (0, 0))],
        out_shape=[jax.ShapeDtypeStruct((2, t, d // 4), U32), jax.ShapeDtypeStruct((TOPK_PAD, t), I32),
                   jax.ShapeDtypeStruct((TOPK_PAD, t), F32), jax.ShapeDtypeStruct((TOPK_PAD, t), I32),
                   jax.ShapeDtypeStruct((N_EXPERTS, 1), F32)],
        scratch_shapes=[pltpu.VMEM((N_EXPERTS, 1), F32)],
        compiler_params=_cparams("arbitrary"),
        name="moe_router",
    )(x1, modtab, g2, rw, rb, tri)


SC_WINDOW = 128


def _sc_mesh():
    return plsc.VectorSubcoreMesh(core_axis_name="c", subcore_axis_name="s")


def _sc_scatter_rows(src, idx, n_out):
    n, width = src.shape
    k_rep = idx.shape[0]
    half = n // SC_WINDOW // 2

    @functools.partial(pl.kernel, out_type=jax.ShapeDtypeStruct((n_out, width), src.dtype), mesh=_sc_mesh(),
                       scratch_types=[], name="moe_dispatch_sc")
    def scatter(src_hbm, idx_hbm, out_hbm):
        def body(x_vmem, *i_vmems):
            for i_vmem in i_vmems:
                pltpu.sync_copy(x_vmem, out_hbm.at[i_vmem.at[0]])

        pltpu.emit_pipeline(
            body,
            grid=(2, half),
            in_specs=[pl.BlockSpec((SC_WINDOW, width), lambda a, i: (a * half + i, 0))]
                     + [pl.BlockSpec((1, SC_WINDOW), lambda a, i, k=k: (k, a * half + i)) for k in range(k_rep)],
            out_specs=[],
            core_axis_name=("c", "s"),
            dimension_semantics=(pltpu.PARALLEL, pltpu.PARALLEL),
        )(src_hbm, *([idx_hbm] * k_rep))

    return scatter(src, idx)


def _sc_gather_rows(src, idx):
    k_rep, n = idx.shape
    width = src.shape[1]
    n_win = n // SC_WINDOW

    @functools.partial(pl.kernel, out_type=jax.ShapeDtypeStruct((k_rep * n, width), src.dtype), mesh=_sc_mesh(),
                       scratch_types=[], name="moe_gather_sc")
    def gather(src_hbm, idx_hbm, out_hbm):
        def body(i_vmem, o_vmem):
            pltpu.sync_copy(src_hbm.at[i_vmem.at[0]], o_vmem)

        pltpu.emit_pipeline(
            body,
            grid=(k_rep, n_win),
            in_specs=[pl.BlockSpec((1, SC_WINDOW), lambda k, i: (k, i))],
            out_specs=[pl.BlockSpec((SC_WINDOW, width), lambda k, i: (k * n_win + i, 0))],
            core_axis_name=("c", "s"),
            dimension_semantics=(pltpu.PARALLEL, pltpu.PARALLEL),
        )(idx_hbm, out_hbm)

    return gather(src, idx)


def _unpack_planes(p0, p1):
    return _unpack_pair(p0) + _unpack_pair(p1)


def _dot_quarters(parts, w_ref):
    q = parts[0].shape[-1]
    acc = None
    for j, part in enumerate(parts):
        term = _dot(part.astype(BF16), w_ref[j * q:(j + 1) * q, :])
        acc = term if acc is None else acc + term
    return acc


def _expert_kernel(be_ref, nv_ref, xs_ref, wg_ref, wu_ref, wd_ref, ys_o, wg_s, wu_s, wd_s):
    i = pl.program_id(0)
    prev = be_ref[jnp.maximum(i - 1, 0)]

    @pl.when((i == 0) | (be_ref[i] != prev))
    def _():
        wg_s[...] = (0.5 * wg_ref[0]).astype(BF16)
        wu_s[...] = wu_ref[0].astype(BF16)
        wd_s[...] = wd_ref[0].astype(BF16)

    @pl.when(nv_ref[i] > 0)
    def _():
        keep = lax.broadcasted_iota(I32, xs_ref.shape[1:], 0) < nv_ref[i]
        parts = _unpack_planes(jnp.where(keep, xs_ref[0], jnp.uint32(0)), jnp.where(keep, xs_ref[1], jnp.uint32(0)))
        hid = _half_silu(_dot_quarters(parts, wg_s)) * _dot_quarters(parts, wu_s)
        y = _dot(hid.astype(BF16), wd_s[...])
        half = y.shape[-1] // 2
        ys_o[0] = _pack_pair(y[:, :half])
        ys_o[1] = _pack_pair(y[:, half:])

    @pl.when(nv_ref[i] <= 0)
    def _():
        ys_o[...] = jnp.zeros_like(ys_o)


def _experts(xs, block_e, n_valid, weights, layer):
    _, n_rows, dq = xs.shape
    d = 4 * dq
    n_blocks = n_rows // EXPERT_ROWS
    hid = EXPERT_HIDDEN
    grid_spec = pltpu.PrefetchScalarGridSpec(
        num_scalar_prefetch=2,
        grid=(n_blocks,),
        in_specs=[pl.BlockSpec((2, EXPERT_ROWS, dq), lambda i, be, nv: (0, i, 0)),
                  pl.BlockSpec((None, 1, d, hid), lambda i, be, nv: (layer, be[i], 0, 0)),
                  pl.BlockSpec((None, 1, d, hid), lambda i, be, nv: (layer, be[i], 0, 0)),
                  pl.BlockSpec((None, 1, hid, d), lambda i, be, nv: (layer, be[i], 0, 0))],
        out_specs=pl.BlockSpec((2, EXPERT_ROWS, dq), lambda i, be, nv: (0, i, 0)),
        scratch_shapes=[pltpu.VMEM((d, hid), BF16), pltpu.VMEM((d, hid), BF16), pltpu.VMEM((hid, d), BF16)],
    )
    return pl.pallas_call(
        _expert_kernel,
        grid_spec=grid_spec,
        out_shape=jax.ShapeDtypeStruct((2, n_rows, dq), U32),
        compiler_params=_cparams("arbitrary"),
        name="moe_experts",
    )(block_e, n_valid, xs, *weights)


def _combine_kernel(g_ref, wts_ref, h2_ref, x_ref, mod_ref, sg_ref, su_ref, sd_ref, gf_ref, o_ref, *, final):
    parts = _unpack_planes(h2_ref[0], h2_ref[1])
    hid = _half_silu(_dot_quarters(parts, sg_ref)) * _dot_quarters(parts, su_ref)
    shared = _dot(hid.astype(BF16), sd_ref[...])
    wts = wts_ref[...]
    routed = None
    for k in range(TOP_K):
        w = wts[:, k:k + 1]
        terms = [w * part for part in _unpack_planes(g_ref[k, 0], g_ref[k, 1])]
        routed = terms if routed is None else [r + t for r, t in zip(routed, terms)]
    f = shared + jnp.concatenate(routed, axis=1)
    m = mod_ref[0, 0]
    x2 = x_ref[0] + m[5:6] * f
    if final:
        x2 = x2 * lax.rsqrt(jnp.mean(x2 * x2, axis=-1, keepdims=True) + NORM_EPS) * gf_ref[...]
    o_ref[0] = x2


def _combine(g, wts, h2p, x1, modtab, lp, g_final, n_lat_tiles, tiles_used, final):
    b, _, d = x1.shape
    dq = d // 4
    weights = ((0.5 * lp['sh_w_gate']).astype(BF16), lp['sh_w_up'].astype(BF16), lp['sh_w_down'].astype(BF16),
               g_final.reshape(1, -1))
    full = lambda a: pl.BlockSpec(a.shape, lambda i: (0,) * a.ndim)
    tok, mod = _tile_maps(tiles_used, n_lat_tiles)
    return pl.pallas_call(
        functools.partial(_combine_kernel, final=final),
        grid=(b * tiles_used,),
        in_specs=[pl.BlockSpec((TOP_K, 2, TM, dq), lambda i: (0, 0, i, 0)),
                  pl.BlockSpec((TM, TOPK_PAD), lambda i: (i, 0)),
                  pl.BlockSpec((2, TM, dq), lambda i: (0, i, 0)),
                  pl.BlockSpec((1, TM, d), tok), pl.BlockSpec((1, 1, 8, d), mod)]
                 + [full(w) for w in weights],
        out_specs=pl.BlockSpec((1, TM, d), tok),
        out_shape=jax.ShapeDtypeStruct((b, tiles_used * TM, d), F32),
        compiler_params=_cparams("parallel"),
        name="moe_combine",
    )(g, wts, h2p, x1, modtab, *weights)


def _sorted_rows_kernel(idx_ref, rank_ref, ps_ref, o_ref):
    expert = lax.broadcasted_iota(I32, (N_EXPERTS, idx_ref.shape[1]), 0)
    starts = ps_ref[...]
    idx = idx_ref[...]
    rows = [jnp.sum(jnp.where(idx[k:k + 1] == expert, starts, 0), axis=0, keepdims=True) for k in range(TOP_K)]
    pad = jnp.zeros((TOPK_PAD - TOP_K, idx.shape[1]), I32)
    o_ref[...] = jnp.concatenate(rows + [pad], axis=0) + rank_ref[...]


def _sorted_rows(idx, rank, p_starts):
    t = idx.shape[1]
    cols = TM * max(k for k in (8, 4, 2, 1) if (t // TM) % k == 0)
    per_tok = pl.BlockSpec((TOPK_PAD, cols), lambda i: (0, i))
    return pl.pallas_call(
        _sorted_rows_kernel,
        grid=(t // cols,),
        in_specs=[per_tok, per_tok, pl.BlockSpec((N_EXPERTS, 1), lambda i: (0, 0))],
        out_specs=per_tok,
        out_shape=jax.ShapeDtypeStruct((TOPK_PAD, t), I32),
        compiler_params=_cparams("parallel"),
        name="moe_sorted_rows",
    )(idx, rank, p_starts.reshape(-1, 1))


def _moe(x1, modtab, g2, lp, g_final, n_lat_tiles, final):
    b, nt, d = x1.shape
    tiles_used = n_lat_tiles if final else nt // TM
    t = b * tiles_used * TM
    h2p, idx, wts, rank, cnt = _route(x1, modtab, g2, lp, n_lat_tiles, tiles_used)
    counts = cnt[:, 0].astype(I32)
    padded = (counts + EXPERT_ROWS - 1) // EXPERT_ROWS * EXPERT_ROWS
    p_ends = jnp.cumsum(padded)
    p_starts = p_ends - padded
    n_blocks = (t * TOP_K + N_EXPERTS * (EXPERT_ROWS - 1)) // EXPERT_ROWS
    n_rows = n_blocks * EXPERT_ROWS
    dest = _sorted_rows(idx, rank, p_starts)[:TOP_K]
    plane_idx = jnp.concatenate([dest, dest + n_rows], axis=1)
    blk_start = jnp.arange(n_blocks, dtype=I32) * EXPERT_ROWS
    block_e = jnp.minimum(jnp.sum((p_ends[None, :] <= blk_start[:, None]).astype(I32), axis=1), N_EXPERTS - 1)
    n_valid = jnp.clip((p_starts + counts)[block_e] - blk_start, 0, EXPERT_ROWS).astype(I32)
    dq = d // 4
    xs = _sc_scatter_rows(h2p.reshape(2 * t, dq), plane_idx, 2 * n_rows).reshape(2, n_rows, dq)
    ys = _experts(xs, block_e, n_valid, lp['expert_stacks'], lp['layer'])
    g = _sc_gather_rows(ys.reshape(2 * n_rows, dq), plane_idx).reshape(TOP_K, 2, t, dq)
    return _combine(g, wts.T, h2p, x1, modtab, lp, g_final, n_lat_tiles, tiles_used, final)


def _layer(stream, nt, c, c_ctx, lp, consts, g_final, seq, final):
    b = stream[0].shape[0]
    n_lat_tiles = seq // TM
    rope, mats_lat, mats_ctx = consts
    modtab = _mod_table(c, c_ctx, *lp['mod_stacks'], lp['layer'])
    g1 = lp['g_norm1'].reshape(1, -1)
    g2 = lp['g_norm2'].reshape(1, -1)
    pw = _proj_weights(lp)
    q, k, v, nq, nk, nv, hy, lu, lg = _project(stream, nt, modtab, g1, pw, rope, n_lat_tiles)
    tiles_used = n_lat_tiles if final else nt // TM
    br_a = _mla_attention(q, k, v, seq, tiles_used)
    br_b = _na_attention(nq, nk, nv, _na_bias_tables(lp['na_rpb'], seq // GRID_W), seq, tiles_used)
    pre = _hyp_pre(hy, lp['hy_short_w'], lp['hy_short_b'], seq)
    lat_f, lat_bf, ctx_f, ctx_bf = pre
    br_c = _hyena_seq(mats_lat, lat_f, lat_bf, lp, b)
    if not final:
        br_c = jnp.concatenate([br_c, _hyena_seq(mats_ctx, ctx_f, ctx_bf, lp, b)], axis=1)
    br_d = _lru_mixer(lu, lg, lp, seq)
    x1 = _merge(stream, modtab, g1, (br_a, br_b, br_c, br_d), pw['w_gate'], lp, n_lat_tiles, tiles_used)
    return _moe(x1, modtab, g2, lp, g_final, n_lat_tiles, final)


_LAYER_KEYS = ('w_mod', 'b_mod', 'g_norm1', 'g_norm2', 'w_in', 'mla_g_q', 'mla_w_uq', 'mla_g_kv', 'mla_w_ukv',
               'na_rpb', 'hy_short_w', 'hy_short_b', 'hy_w1', 'hy_b1', 'hy_w2', 'hy_b2', 'hy_w3', 'hy_decay',
               'hy_bias', 'lru_conv_w', 'lru_conv_b', 'lru_wa', 'lru_ba', 'lru_wx', 'lru_bx', 'lru_lambda',
               'w_branch', 'w_out', 'router_w', 'router_bias', 'exp_w_gate', 'exp_w_up', 'exp_w_down',
               'sh_w_gate', 'sh_w_up', 'sh_w_down')


def kernel(x, c, ctx, c_ctx, w_mod, b_mod, g_norm1, g_norm2, w_in, mla_g_q, mla_w_uq, mla_g_kv, mla_w_ukv, na_rpb, hy_short_w, hy_short_b, hy_w1, hy_b1, hy_w2, hy_b2, hy_w3, hy_decay, hy_bias, lru_conv_w, lru_conv_b, lru_wa, lru_ba, lru_wx, lru_bx, lru_lambda, w_branch, w_out, router_w, router_bias, exp_w_gate, exp_w_up, exp_w_down, sh_w_gate, sh_w_up, sh_w_down, g_final):
    stacked = dict(zip(_LAYER_KEYS, (w_mod, b_mod, g_norm1, g_norm2, w_in, mla_g_q, mla_w_uq, mla_g_kv, mla_w_ukv,
                                     na_rpb, hy_short_w, hy_short_b, hy_w1, hy_b1, hy_w2, hy_b2, hy_w3, hy_decay,
                                     hy_bias, lru_conv_w, lru_conv_b, lru_wa, lru_ba, lru_wx, lru_bx, lru_lambda,
                                     w_branch, w_out, router_w, router_bias, exp_w_gate, exp_w_up, exp_w_down,
                                     sh_w_gate, sh_w_up, sh_w_down)))
    b, seq, d = x.shape
    ctx_len = ctx.shape[1]
    depth = w_mod.shape[0]
    assert seq % TM == 0 and ctx_len % TM == 0 and seq // GRID_W >= NA_KEY_ROWS + 1
    nt = seq + ctx_len
    stream = (x, ctx, 0)
    consts = (_rope_tables(seq, seq + ctx_len), _hyp_matrices(seq), _hyp_matrices(ctx_len))
    for i in range(depth):
        big = ('w_mod', 'b_mod', 'exp_w_gate', 'exp_w_up', 'exp_w_down')
        lp = {name: w[i] for name, w in stacked.items() if name not in big}
        lp['layer'] = i
        lp['mod_stacks'] = (w_mod, b_mod)
        lp['expert_stacks'] = (exp_w_gate, exp_w_up, exp_w_down)
        xa = _layer(stream, nt, c, c_ctx, lp, consts, g_final, seq, i == depth - 1)
        stream = (xa, xa, seq // TM)
    return xa
```

```python
import functools
import math

import numpy as np
import jax
import jax.numpy as jnp
from jax import lax
from jax.experimental import pallas as pl
from jax.experimental.pallas import tpu as pltpu
from jax.experimental.pallas import tpu_sc as plsc

F32 = jnp.float32
BF16 = jnp.bfloat16
I32 = jnp.int32

TM = 256
LANE = 128
GRID_W = 64
N_MOD = 6
NORM_EPS = 1e-6

MLA_HEADS, MLA_NOPE, MLA_ROPE, MLA_V = 4, 64, 32, 64
MLA_Q_RANK, MLA_KV_RANK = 192, 128
MLA_Q_PAD = 256
ROPE_THETA = 10000.0

NA_HEADS, NA_HEAD_DIM, NA_WIN_R, NA_WIN_C = 4, 64, 8, 16
NA_TILE_ROWS = TM // GRID_W
NA_KEY_ROWS = NA_TILE_ROWS + NA_WIN_R - 1
NA_KEYS = NA_KEY_ROWS * GRID_W

HY_WIDTH, HY_ORDER, HY_SHORT, HY_BANDS, HY_FFN = 256, 2, 3, 16, 64
HY_EMB = 2 * HY_BANDS + 1

LRU_WIDTH, LRU_BLOCKS, LRU_CONV, LRU_C = 256, 4, 4, 8.0
LRU_CHUNK = 256
LRU_HALO = 8

N_EXPERTS, TOP_K, EXPERT_HIDDEN, ROUTED_SCALE, MOE_BLOCK = 64, 6, 256, 2.5, 256
TOPK_PAD = 8

VMEM_LIMIT = 52 * 1024 * 1024


def _cparams(*sem):
    return pltpu.CompilerParams(dimension_semantics=sem, vmem_limit_bytes=VMEM_LIMIT)


def _dot(a, b):
    return jnp.dot(a, b, preferred_element_type=F32)


def _dot_nt(a, b):
    return lax.dot_general(a, b, (((1,), (1,)), ((), ())), preferred_element_type=F32)


def _sigmoid(x):
    return jax.nn.sigmoid(x)


def _half_silu(g):
    return g * (1.0 + jnp.tanh(g))


def _silu(x):
    return x * _sigmoid(x)


def _gelu_tanh(x):
    return 0.5 * x * (1.0 + jnp.tanh(math.sqrt(2.0 / math.pi) * (x + 0.044715 * (x * x * x))))


def _normmod(x, g, scale, shift):
    y = x * lax.rsqrt(jnp.mean(x * x, axis=-1, keepdims=True) + NORM_EPS) * g
    return y * (1.0 + scale) + shift


MOD_COLS = 1024


def _mod_kernel(c_ref, w_ref, b_ref, o_ref):
    s = _silu(c_ref[...])
    o_ref[...] = _dot(s.astype(BF16), w_ref[...].astype(BF16)) + b_ref[...]


def _mod_table(c, c_ctx, w_mod, b_mod, layer):
    b, d = c.shape
    rows = -(-(b + 1) // 16) * 16
    cc = jnp.zeros((rows, d), F32).at[:b].set(c).at[b].set(c_ctx)
    tn = MOD_COLS
    mod = pl.pallas_call(
        _mod_kernel,
        grid=(N_MOD * d // tn,),
        in_specs=[pl.BlockSpec((rows, d), lambda j: (0, 0)),
                  pl.BlockSpec((None, d, tn), lambda j: (layer, 0, j)),
                  pl.BlockSpec((None, 1, tn), lambda j: (layer, 0, j))],
        out_specs=pl.BlockSpec((rows, tn), lambda j: (0, j)),
        out_shape=jax.ShapeDtypeStruct((rows, N_MOD * d), F32),
        compiler_params=_cparams("arbitrary"),
        name="mod_vectors",
    )(cc, w_mod, b_mod[:, None, :])
    lat = mod[:b].reshape(b, N_MOD, d)
    ctx = jnp.broadcast_to(mod[b].reshape(1, N_MOD, d), (b, N_MOD, d))
    tab = jnp.stack([lat, ctx], axis=1)
    return jnp.pad(tab, ((0, 0), (0, 0), (0, 8 - N_MOD), (0, 0)))


_C_QLAT, _C_KVLAT, _C_KR, _C_KRR, _C_NA, _C_HY, _C_LU, _C_LG, _C_END = (
    0, 256, 384, 512, 640, 640 + 3 * NA_HEADS * LANE, 640 + 1536 + 768, 640 + 1536 + 1024, 640 + 1536 + 1280)


def _stream_tile(lat_ref, ctx_ref, n_lat_tiles):
    return jnp.where(pl.program_id(1) < n_lat_tiles, lat_ref[...], ctx_ref[...])


def _stream_specs(stream, group, n_lat_tiles):
    lat_src, ctx_src, ctx_tile0 = stream
    d = lat_src.shape[-1]
    return [pl.BlockSpec((group, TM, d), lambda bb, i: (bb, jnp.minimum(i, n_lat_tiles - 1), 0)),
            pl.BlockSpec((group, TM, d), lambda bb, i: (bb, ctx_tile0 + jnp.maximum(i - n_lat_tiles, 0), 0))]


def _proj_kernel(xl_ref, xc_ref, mod_ref, g1_ref, w1_ref, gq_ref, gkv_ref, wuq_ref, wuqr_ref, wk_ref, wv_ref,
                 cos_ref, sin_ref, q_o, k_o, v_o, nq_o, nk_o, nv_o, hy_o, lu_o, lg_o, *, n_lat_tiles):
    group, tm, d = xl_ref.shape
    m = mod_ref[:, 0]
    x = _stream_tile(xl_ref, xc_ref, n_lat_tiles)
    h = _normmod(x, g1_ref[...], m[:, 1:2], m[:, 0:1]).reshape(group * tm, d)
    z = _dot(h.astype(BF16), w1_ref[...])
    qlat = z[:, _C_QLAT:_C_KVLAT]
    kvlat = z[:, _C_KVLAT:_C_KR]
    kr = z[:, _C_KR:_C_KRR]
    krr = z[:, _C_KRR:_C_NA]
    qn = qlat * lax.rsqrt(jnp.sum(qlat * qlat, axis=-1, keepdims=True) * (1.0 / MLA_Q_RANK) + NORM_EPS) * gq_ref[...]
    kvn = kvlat * lax.rsqrt(jnp.mean(kvlat * kvlat, axis=-1, keepdims=True) + NORM_EPS) * gkv_ref[...]
    qn = qn.astype(BF16)
    kvn = kvn.astype(BF16)
    q = _dot(qn, wuq_ref[...])
    qr = _dot(qn, wuqr_ref[...])
    kk = _dot(kvn, wk_ref[...])
    vv = _dot(kvn, wv_ref[...])
    cos = jnp.concatenate([cos_ref[...]] * group, axis=0)
    sin = jnp.concatenate([sin_ref[...]] * group, axis=0)
    krope = kr * cos + krr * sin
    den_lane = lax.broadcasted_iota(I32, cos.shape, 1) == SOFTMAX_DEN_LANE
    def put_heads(ref, hd, val):
        for g in range(group):
            ref[g, hd] = val[g * tm:(g + 1) * tm].astype(BF16)

    def put_pairs(ref, hd, val):
        off = (hd % 2) * LANE
        for g in range(group):
            ref[g, hd // 2, :, off:off + LANE] = val[g * tm:(g + 1) * tm].astype(BF16)

    for hd in range(MLA_HEADS):
        sl = slice(hd * LANE, (hd + 1) * LANE)
        put_heads(q_o, hd, q[:, sl] * cos + qr[:, sl] * sin)
        put_heads(k_o, hd, kk[:, sl] + krope)
        put_heads(v_o, hd, jnp.where(den_lane, 1.0, vv[:, sl]))
    for hd in range(NA_HEADS):
        for which, ref in enumerate((nq_o, nk_o, nv_o)):
            lo = _C_NA + (which * NA_HEADS + hd) * LANE
            blk = z[:, lo:lo + LANE]
            if which == 2:
                put_pairs(ref, hd, jnp.where(den_lane, 1.0, blk))
            else:
                put_heads(ref, hd, blk)
    hy_o[...] = z[:, _C_HY:_C_LU].reshape(group, tm, _C_LU - _C_HY)
    lu_o[...] = z[:, _C_LU:_C_LG].reshape(group, tm, _C_LG - _C_LU)
    lg_o[...] = z[:, _C_LG:_C_END].reshape(group, tm, _C_END - _C_LG)


def _pad_to(a, n, axis):
    pad = [(0, 0)] * a.ndim
    pad[axis] = (0, n - a.shape[axis])
    return jnp.pad(a, pad)


def _rot_cols(w):
    half = w.shape[-1] // 2
    return jnp.concatenate([-w[..., half:], w[..., :half]], axis=-1)


def _head_blocks(cols_per_head):
    out = []
    for pieces in cols_per_head:
        k = pieces[0][0].shape[0]
        blk = jnp.zeros((k, LANE), F32)
        for arr, off in pieces:
            blk = blk.at[:, off:off + arr.shape[1]].set(arr)
        out.append(blk)
    return jnp.concatenate(out, axis=1)


def _proj_weights(lp):
    w_in = lp['w_in']
    d = w_in.shape[0]
    o = 0
    parts = {}
    for name, n in (('q', MLA_Q_RANK), ('kv', MLA_KV_RANK), ('kr', MLA_ROPE), ('na', 3 * NA_HEADS * NA_HEAD_DIM),
                    ('hy', 3 * HY_WIDTH), ('lu', LRU_WIDTH), ('lg', LRU_WIDTH), ('gt', 4 * d)):
        parts[name] = w_in[:, o:o + n]
        o += n
    zeros = lambda n: jnp.zeros((d, n), F32)
    kr_blk = jnp.concatenate([zeros(MLA_NOPE), parts['kr'], zeros(LANE - MLA_NOPE - MLA_ROPE)], axis=1)
    krr_blk = jnp.concatenate([zeros(MLA_NOPE), _rot_cols(parts['kr']), zeros(LANE - MLA_NOPE - MLA_ROPE)], axis=1)
    na_scale = NA_HEAD_DIM ** -0.5
    na_cols = []
    for which in range(3):
        for hd in range(NA_HEADS):
            lo = (which * NA_HEADS + hd) * NA_HEAD_DIM
            blk = parts['na'][:, lo:lo + NA_HEAD_DIM] * (na_scale if which == 0 else 1.0)
            na_cols.append(_pad_to(blk, LANE, 1))
    w1 = jnp.concatenate([_pad_to(parts['q'], MLA_Q_PAD, 1), parts['kv'], kr_blk, krr_blk] + na_cols
                         + [parts['hy'], parts['lu'], parts['lg']], axis=1).astype(BF16)
    mla_scale = (MLA_NOPE + MLA_ROPE) ** -0.5
    wuq = _pad_to(lp['mla_w_uq'], MLA_Q_PAD, 0) * mla_scale
    dq = MLA_NOPE + MLA_ROPE
    wuq_main = _head_blocks([[(wuq[:, hd * dq:hd * dq + dq], 0)] for hd in range(MLA_HEADS)])
    wuq_rot = _head_blocks([[(_rot_cols(wuq[:, hd * dq + MLA_NOPE:hd * dq + dq]), MLA_NOPE)]
                            for hd in range(MLA_HEADS)])
    dkv = MLA_NOPE + MLA_V
    wukv = lp['mla_w_ukv']
    wk = _head_blocks([[(wukv[:, hd * dkv:hd * dkv + MLA_NOPE], 0)] for hd in range(MLA_HEADS)])
    wv = _head_blocks([[(wukv[:, hd * dkv + MLA_NOPE:hd * dkv + dkv], 0)] for hd in range(MLA_HEADS)])
    gq = _pad_to(lp['mla_g_q'].reshape(1, -1), MLA_Q_PAD, 1)
    gkv = lp['mla_g_kv'].reshape(1, -1)
    return dict(w1=w1, w_gate=(0.5 * parts['gt']).astype(BF16), gq=gq, gkv=gkv, wuq=wuq_main.astype(BF16),
                wuq_rot=wuq_rot.astype(BF16), wk=wk.astype(BF16), wv=wv.astype(BF16))


def _rope_tables(seq, n_tok):
    t = jnp.arange(seq, dtype=I32)
    row = (t // GRID_W).astype(F32)
    col = (t % GRID_W).astype(F32)
    n_axis = MLA_ROPE // 4
    inv_freq = ROPE_THETA ** (-jnp.arange(n_axis, dtype=F32) / n_axis)
    ang = jnp.concatenate([row[:, None] * inv_freq, col[:, None] * inv_freq], axis=-1)
    cos = jnp.concatenate([jnp.cos(ang), jnp.cos(ang)], axis=-1)
    sin = jnp.concatenate([jnp.sin(ang), jnp.sin(ang)], axis=-1)
    cos_t = jnp.ones((n_tok, LANE), F32).at[:seq, MLA_NOPE:MLA_NOPE + MLA_ROPE].set(cos)
    sin_t = jnp.zeros((n_tok, LANE), F32).at[:seq, MLA_NOPE:MLA_NOPE + MLA_ROPE].set(sin)
    return cos_t, sin_t


def _batch_group(b):
    return 2 if b % 2 == 0 else 1


def _kind_map(n_lat_tiles):
    return lambda b, i: (b, jnp.where(i >= n_lat_tiles, 1, 0), 0, 0)


def _project(stream, nt, modtab, g1, pw, rope, n_lat_tiles):
    b, _, d = stream[0].shape
    cos_t, sin_t = rope
    group = _batch_group(b)
    full = lambda a: pl.BlockSpec(a.shape, lambda bb, i: (0,) * a.ndim)
    head_out = lambda: pl.BlockSpec((group, MLA_HEADS, TM, LANE), lambda bb, i: (bb, 0, i, 0))
    tok_out = lambda n: pl.BlockSpec((group, TM, n), lambda bb, i: (bb, i, 0))
    head_shape = jax.ShapeDtypeStruct((b, MLA_HEADS, nt, LANE), BF16)
    pair_out = lambda: pl.BlockSpec((group, MLA_HEADS // 2, TM, 2 * LANE), lambda bb, i: (bb, 0, i, 0))
    pair_shape = jax.ShapeDtypeStruct((b, MLA_HEADS // 2, nt, 2 * LANE), BF16)
    tok_shape = lambda n: jax.ShapeDtypeStruct((b, nt, n), F32)
    weights = (g1, pw['w1'], pw['gq'], pw['gkv'], pw['wuq'], pw['wuq_rot'], pw['wk'], pw['wv'])
    return pl.pallas_call(
        functools.partial(_proj_kernel, n_lat_tiles=n_lat_tiles),
        grid=(b // group, nt // TM),
        in_specs=_stream_specs(stream, group, n_lat_tiles)
                 + [pl.BlockSpec((group, 1, 8, d), _kind_map(n_lat_tiles))]
                 + [full(w) for w in weights]
                 + [pl.BlockSpec((TM, LANE), lambda bb, i: (i, 0))] * 2,
        out_specs=[head_out(), head_out(), head_out(), head_out(), head_out(), pair_out(),
                   tok_out(3 * HY_WIDTH), tok_out(LRU_WIDTH), tok_out(LRU_WIDTH)],
        out_shape=[head_shape, head_shape, head_shape, head_shape, head_shape, pair_shape,
                   tok_shape(3 * HY_WIDTH), tok_shape(LRU_WIDTH), tok_shape(LRU_WIDTH)],
        compiler_params=_cparams("parallel", "parallel"),
        name="input_projection",
    )(stream[0], stream[1], modtab, *weights, cos_t, sin_t)


SOFTMAX_DEN_LANE = 64


def _softmax_pv(parts, lane_off=0):
    m = None
    for s, _ in parts:
        mm = jnp.max(s, axis=-1, keepdims=True)
        m = mm if m is None else jnp.maximum(m, mm)
    acc = None
    for s, v in parts:
        o = _dot(jnp.exp(s - m).astype(BF16), v)
        acc = o if acc is None else acc + o
    den = lane_off + SOFTMAX_DEN_LANE
    return acc[:, lane_off:lane_off + LANE] / acc[:, den:den + 1]


def _mla_kernel(q_ref, k_ref, v_ref, o_ref, *, seq, n_lat_tiles):
    i = pl.program_id(1)
    nt = k_ref.shape[2]

    def attend(lo, hi):
        for hd in range(MLA_HEADS):
            s = _dot_nt(q_ref[0, hd], k_ref[0, hd, lo:hi, :])
            o = _softmax_pv([(s, v_ref[0, hd, lo:hi, :])])
            o_ref[0, :, hd * LANE:(hd + 1) * LANE] = o.astype(BF16)

    @pl.when(i < n_lat_tiles)
    def _():
        attend(0, nt)

    @pl.when(i >= n_lat_tiles)
    def _():
        attend(seq, nt)


def _mla_attention(q, k, v, seq, tiles_used):
    b, h, nt, _ = q.shape
    kv_spec = pl.BlockSpec((1, h, nt, LANE), lambda bb, i: (bb, 0, 0, 0))
    return pl.pallas_call(
        functools.partial(_mla_kernel, seq=seq, n_lat_tiles=seq // TM),
        grid=(b, tiles_used),
        in_specs=[pl.BlockSpec((1, h, TM, LANE), lambda bb, i: (bb, 0, i, 0)), kv_spec, kv_spec],
        out_specs=pl.BlockSpec((1, TM, h * LANE), lambda bb, i: (bb, i, 0)),
        out_shape=jax.ShapeDtypeStruct((b, tiles_used * TM, h * LANE), BF16),
        compiler_params=_cparams("parallel", "arbitrary"),
        name="mla_attention",
    )(q, k, v)


def _na_bias_tables(rpb, rows):
    n_blk = rows // NA_TILE_ROWS
    col = np.arange(GRID_W)
    c0 = np.clip(col - NA_WIN_C // 2, 0, GRID_W - NA_WIN_C)
    in_win = (col[None, :] >= c0[:, None]) & (col[None, :] < c0[:, None] + NA_WIN_C)
    dc = np.clip(col[None, :] - col[:, None], 1 - NA_WIN_C, NA_WIN_C - 1) + NA_WIN_C - 1
    rpb = rpb.astype(F32)
    tables = []
    for j in (0, 1, n_blk - 1):
        w0 = min(max(NA_TILE_ROWS * j - NA_WIN_R // 2, 0), rows - NA_KEY_ROWS)
        r = NA_TILE_ROWS * j + np.arange(NA_TILE_ROWS)
        kr = w0 + np.arange(NA_KEY_ROWS)
        r0 = np.clip(r - NA_WIN_R // 2, 0, rows - NA_WIN_R)
        row_ok = (kr[None, :] >= r0[:, None]) & (kr[None, :] < r0[:, None] + NA_WIN_R)
        dr = np.clip(kr[None, :] - r[:, None] + NA_WIN_R - 1, 0, 2 * NA_WIN_R - 2)
        oh_r = jnp.asarray(np.eye(2 * NA_WIN_R - 1, dtype=np.float32)[dr.reshape(-1)])
        oh_c = jnp.asarray(np.eye(2 * NA_WIN_C - 1, dtype=np.float32)[dc.reshape(-1)])
        bias = jnp.einsum('ar,hrc,bc->hab', oh_r, rpb, oh_c, precision=lax.Precision.HIGHEST)
        bias = bias.reshape(NA_HEADS, NA_TILE_ROWS, NA_KEY_ROWS, GRID_W, GRID_W)
        mask = row_ok[:, :, None, None] & in_win[None, None, :, :]
        bias = jnp.where(jnp.asarray(mask)[None], bias, -jnp.inf)
        tables.append(bias.transpose(0, 1, 3, 2, 4).reshape(NA_HEADS, TM, NA_KEYS))
    return jnp.stack(tables)


def _na_kernel(q_ref, k_ref, v_ref, bias_ref, o_ref, *, seq, n_lat_tiles):
    i = pl.program_id(1)
    nt = k_ref.shape[2]
    rows = seq // GRID_W

    @pl.when(i < n_lat_tiles)
    def _():
        w0 = jnp.clip(NA_TILE_ROWS * i - NA_WIN_R // 2, 0, rows - NA_KEY_ROWS)
        start = pl.multiple_of(w0 * GRID_W, GRID_W)
        for hd in range(NA_HEADS):
            q = q_ref[0, hd]
            s_loc = _dot_nt(q, k_ref[0, hd, pl.ds(start, NA_KEYS), :]) + bias_ref[0, hd]
            s_ctx = _dot_nt(q, k_ref[0, hd, seq:nt, :])
            o = _softmax_pv([(s_loc, v_ref[0, hd // 2, pl.ds(start, NA_KEYS), :]),
                             (s_ctx, v_ref[0, hd // 2, seq:nt, :])], (hd % 2) * LANE)
            o_ref[0, :, hd * LANE:(hd + 1) * LANE] = o.astype(BF16)

    @pl.when(i >= n_lat_tiles)
    def _():
        for hd in range(NA_HEADS):
            s = _dot_nt(q_ref[0, hd], k_ref[0, hd, seq:nt, :])
            o = _softmax_pv([(s, v_ref[0, hd // 2, seq:nt, :])], (hd % 2) * LANE)
            o_ref[0, :, hd * LANE:(hd + 1) * LANE] = o.astype(BF16)


def _na_attention(q, k, v, bias, seq, tiles_used):
    b, h, nt, _ = q.shape
    n_lat = seq // TM
    kv_spec = pl.BlockSpec((1, h, nt, LANE), lambda bb, i: (bb, 0, 0, 0))
    cfg = lambda bb, i: (jnp.where(i == 0, 0, jnp.where(i >= n_lat - 1, 2, 1)), 0, 0, 0)
    return pl.pallas_call(
        functools.partial(_na_kernel, seq=seq, n_lat_tiles=n_lat),
        grid=(b, tiles_used),
        in_specs=[pl.BlockSpec((1, h, TM, LANE), lambda bb, i: (bb, 0, i, 0)), kv_spec,
                  pl.BlockSpec((1, h // 2, nt, 2 * LANE), lambda bb, i: (bb, 0, 0, 0)),
                  pl.BlockSpec((1, h, TM, NA_KEYS), cfg)],
        out_specs=pl.BlockSpec((1, TM, h * LANE), lambda bb, i: (bb, i, 0)),
        out_shape=jax.ShapeDtypeStruct((b, tiles_used * TM, h * LANE), BF16),
        compiler_params=_cparams("parallel", "arbitrary"),
        name="neighbourhood_attention",
    )(q, k, v, bias)


def _hp_dot(a, b):
    return jnp.dot(a, b, preferred_element_type=F32, precision=lax.Precision.HIGHEST)


def _hy_filter_kernel(f_ref, w1_ref, b1_ref, w2_ref, b2_ref, w3_ref, dec_ref, o_ref, obf_ref):
    f = f_ref[...]
    h = jnp.sin(_hp_dot(f, w1_ref[...]) + b1_ref[...])
    h = jnp.sin(_hp_dot(h, w2_ref[...]) + b2_ref[...])
    h = _hp_dot(h, w3_ref[...])
    h = h * jnp.exp(-f[:, 0:1] * jnp.abs(dec_ref[...]))
    o_ref[...] = h
    obf_ref[...] = h.astype(BF16)


def _hy_pos_features(length):
    t = jnp.linspace(0.0, 1.0, length, dtype=F32)[:, None]
    w = 2.0 * math.pi * jnp.arange(length, dtype=F32)[:, None] / length
    f = jnp.linspace(1e-4, HY_BANDS - 1, HY_BANDS, dtype=F32)[None, :]
    z = w * f
    return jnp.concatenate([t, jnp.cos(z), -jnp.sin(z)], axis=-1)


DFT_ROW_STEP = 64
DFT_ROWS = 256
DFT_FWD_COLS = 1024
DFT_INV_COLS = 512
FILTER_ROWS = 512


def _col_block(nc, cap):
    return min(nc, cap)


def _hyp_pre_kernel(z_ref, w_ref, b_ref, lat_o, lat_bf_o, ctx_o, ctx_bf_o, *, seq, ctx_len):
    w = w_ref[...]
    bias = b_ref[...]
    for lo, length, o_ref, obf_ref in ((0, seq, lat_o, lat_bf_o), (seq, ctx_len, ctx_o, ctx_bf_o)):
        h = length // 2
        even = z_ref[0, pl.ds(lo, h, stride=2), :]
        odd = z_ref[0, pl.ds(lo + 1, h, stride=2), :]
        row = lax.broadcasted_iota(I32, even.shape, 0)
        odd_prev = jnp.where(row == 0, 0.0, pltpu.roll(odd, 1, axis=0))
        even_next = jnp.where(row == h - 1, 0.0, pltpu.roll(even, h - 1, axis=0))
        y_even = bias + w[0:1] * odd_prev + w[1:2] * even + w[2:3] * odd
        y_odd = bias + w[0:1] * even + w[1:2] * odd + w[2:3] * even_next
        for r, y in enumerate((y_even, y_odd)):
            o_ref[0, r] = y
            obf_ref[0, r] = y.astype(BF16)


def _hyp_pre(hy, w, bvec, seq):
    b, nt, _ = hy.shape
    ctx_len = nt - seq
    wpad = _pad_to(w, 8, 0)
    per_plane = HY_WIDTH // LANE
    out_specs, out_shape = [], []
    for length in (seq, ctx_len):
        for dt in (F32, BF16):
            out_specs.append(pl.BlockSpec((1, 2, length // 2, LANE),
                                          lambda bb, g: (g // per_plane, 0, 0, bb * per_plane + g % per_plane)))
            out_shape.append(jax.ShapeDtypeStruct((3, 2, length // 2, b * HY_WIDTH), dt))
    return pl.pallas_call(
        functools.partial(_hyp_pre_kernel, seq=seq, ctx_len=ctx_len),
        grid=(b, 3 * per_plane),
        in_specs=[pl.BlockSpec((1, nt, LANE), lambda bb, g: (bb, 0, g)),
                  pl.BlockSpec((8, LANE), lambda bb, g: (0, g)),
                  pl.BlockSpec((1, LANE), lambda bb, g: (0, g))],
        out_specs=out_specs,
        out_shape=out_shape,
        compiler_params=_cparams("parallel", "parallel"),
        name="hyena_short_conv",
    )(hy, wpad, bvec.reshape(1, -1))


def _hyp_filters(length, lp):
    feats = _hy_pos_features(length)
    feats = feats.reshape(length // 2, 2, HY_EMB).transpose(1, 0, 2).reshape(length, HY_EMB)
    n_out = HY_ORDER * 2 * HY_WIDTH
    tl = min(length // 2, FILTER_ROWS)
    full = lambda a: pl.BlockSpec(a.shape, lambda i: (0,) * a.ndim)
    args = (lp['hy_w1'], lp['hy_b1'].reshape(1, -1), lp['hy_w2'], lp['hy_b2'].reshape(1, -1), lp['hy_w3'],
            lp['hy_decay'].reshape(1, n_out))
    filt, filt_bf = pl.pallas_call(
        _hy_filter_kernel,
        grid=(length // tl,),
        in_specs=[pl.BlockSpec((tl, HY_EMB), lambda i: (i, 0))] + [full(a) for a in args],
        out_specs=[pl.BlockSpec((tl, n_out), lambda i: (i, 0))] * 2,
        out_shape=[jax.ShapeDtypeStruct((length, n_out), F32), jax.ShapeDtypeStruct((length, n_out), BF16)],
        compiler_params=_cparams("parallel"),
        name="hyena_filter_mlp",
    )(feats, *args)
    return filt.reshape(2, length // 2, n_out), filt_bf.reshape(2, length // 2, n_out)


def _hyp_matrix_kernel(c1_ref, s1_ref, c2_ref, s2_ref, ck_ref, sk_ref, ckr_ref, skr_ref,
                       ce_o, se_o, co_o, so_o, set_o, cot_o, sot_o):
    c1, s1 = c1_ref[0], s1_ref[0]
    c2, s2 = c2_ref[...], s2_ref[...]
    cos = c1 * c2 - s1 * s2
    sin = s1 * c2 + c1 * s2
    row = lax.broadcasted_iota(I32, cos.shape, 0)
    col = lax.broadcasted_iota(I32, cos.shape, 1)
    first_row = (row + pl.program_id(0) * DFT_ROW_STEP) == 0
    alt_col = jnp.where(col % 2 == 0, 1.0, -1.0)
    alt_row = jnp.where(row % 2 == 0, 1.0, -1.0)
    ck, sk = ck_ref[...], sk_ref[...]
    ce_o[...] = cos.astype(BF16)
    se_o[...] = jnp.where(first_row, alt_col, -sin).astype(BF16)
    co_o[...] = (cos * ck - sin * sk).astype(BF16)
    so_o[...] = jnp.where(first_row, alt_col, -(sin * ck + cos * sk)).astype(BF16)
    ckr, skr = ckr_ref[...], skr_ref[...]
    set_o[...] = jnp.where(col == 0, alt_row, -sin).astype(BF16)
    cot_o[...] = (cos * ckr - sin * skr).astype(BF16)
    sot_o[...] = jnp.where(col == 0, alt_row, -(sin * ckr + cos * skr)).astype(BF16)


def _hyp_matrices(length):
    h = length // 2
    step = DFT_ROW_STEP
    m = jnp.arange(h, dtype=I32)

    def trig(kv):
        ang = ((kv[:, None] * m[None, :]) % length).astype(F32) * (2.0 * math.pi / length)
        return jnp.cos(ang), jnp.sin(ang)

    c1, s1 = trig(jnp.arange(h // step, dtype=I32) * step)
    c2, s2 = trig(jnp.arange(step, dtype=I32))
    half_angle = m.astype(F32) * (math.pi / length)
    ck, sk = jnp.cos(half_angle), jnp.sin(half_angle)
    coarse = pl.BlockSpec((1, 1, h), lambda j: (j, 0, 0))
    fine = pl.BlockSpec((step, h), lambda j: (0, 0))
    per_row = pl.BlockSpec((step, 1), lambda j: (j, 0))
    per_col = pl.BlockSpec((1, h), lambda j: (0, 0))
    out = pl.BlockSpec((step, h), lambda j: (j, 0))
    ce, se, co, so, se_t, co_t, so_t = pl.pallas_call(
        _hyp_matrix_kernel,
        grid=(h // step,),
        in_specs=[coarse, coarse, fine, fine, per_row, per_row, per_col, per_col],
        out_specs=[out] * 7,
        out_shape=[jax.ShapeDtypeStruct((h, h), BF16)] * 7,
        compiler_params=_cparams("parallel"),
        name="dft_matrices",
    )(c1[:, None, :], s1[:, None, :], c2, s2, ck[:, None], sk[:, None], ck[None, :], sk[None, :])
    return dict(fwd=(ce, se, co, so), inv=(ce, se_t, co_t, so_t))


def _hyp_fwd_kernel(ce_ref, se_ref, co_ref, so_ref, xe_ref, xo_ref, *rest, with_taps):
    xe, xo = xe_ref[...], xo_ref[...]
    a_re, a_im = _dot(ce_ref[...], xe), _dot(se_ref[...], xe)
    b_re, b_im = _dot(co_ref[...], xo), _dot(so_ref[...], xo)
    if not with_taps:
        for ref, val in zip(rest, (a_re, a_im, b_re, b_im)):
            ref[...] = val
        return
    t1re_ref, t1im_ref, t2re_ref, t2im_ref, sp_ref, gere_o, geim_o, gore_o, goim_o = rest
    t1re, t1im, t2re, t2im = t1re_ref[...], t1im_ref[...], t2re_ref[...], t2im_ref[...]
    sp = sp_ref[...]
    first_block = pl.program_id(1) == 0
    row8 = lax.broadcasted_iota(I32, (8, HY_WIDTH), 0)
    for bb in range(xe.shape[1] // HY_WIDTH):
        sl = slice(bb * HY_WIDTH, (bb + 1) * HY_WIDTH)
        are, aim, bre, bim = a_re[:, sl], a_im[:, sl], b_re[:, sl], b_im[:, sl]
        u1re, u1im = are + bre, aim + bim
        u2re, u2im = are - bre, bim - aim
        z1re, z1im = u1re * t1re - u1im * t1im, u1re * t1im + u1im * t1re
        z2re, z2im = u2re * t2re - u2im * t2im, u2re * t2im + u2im * t2re
        gere_o[:, sl] = (z1re + z2re).astype(BF16)
        geim_o[:, sl] = (z1im - z2im).astype(BF16)
        gore_o[:, sl] = (z1re - z2re).astype(BF16)
        goim_o[:, sl] = (z1im + z2im).astype(BF16)

        @pl.when(first_block)
        def _():
            u0, ul = u1re[0:8], u2re[0:8]
            a_s, b_s = aim[0:8], bim[0:8]
            dc, ny, mre, mim = sp[0:1], sp[1:2], sp[2:3], sp[3:4]
            first = row8 == 0
            gere_o[0:8, sl] = jnp.where(first, u0 * dc + ul * ny, (z1re + z2re)[0:8]).astype(BF16)
            gore_o[0:8, sl] = jnp.where(first, u0 * dc - ul * ny, (z1re - z2re)[0:8]).astype(BF16)
            geim_o[0:8, sl] = jnp.where(first, a_s * mre + b_s * mim, (z1im - z2im)[0:8]).astype(BF16)
            goim_o[0:8, sl] = jnp.where(first, b_s * mre - a_s * mim, (z1im + z2im)[0:8]).astype(BF16)


def _hyp_fwd(mats, x, plane, taps=None):
    _, _, h, nc = x.shape
    tk = min(h, DFT_ROWS)
    cb = _col_block(nc, DFT_FWD_COLS)
    grid = (nc // cb, h // tk)
    m_spec = pl.BlockSpec((tk, h), lambda c, j: (j, 0))
    x_spec = lambda r: pl.BlockSpec((None, None, h, cb), lambda c, j: (plane, r, 0, c))
    o_spec = pl.BlockSpec((tk, cb), lambda c, j: (j, c))
    if taps is None:
        return pl.pallas_call(
            functools.partial(_hyp_fwd_kernel, with_taps=False),
            grid=grid, in_specs=[m_spec] * 4 + [x_spec(0), x_spec(1)], out_specs=[o_spec] * 4,
            out_shape=[jax.ShapeDtypeStruct((h, nc), F32)] * 4,
            compiler_params=_cparams("parallel", "arbitrary"),
            name="hyena_dft_filters",
        )(*mats['fwd'], x, x)
    t_spec = pl.BlockSpec((tk, HY_WIDTH), lambda c, j: (j, 0))
    sp_spec = pl.BlockSpec((8, HY_WIDTH), lambda c, j: (0, 0))
    return pl.pallas_call(
        functools.partial(_hyp_fwd_kernel, with_taps=True),
        grid=grid, in_specs=[m_spec] * 4 + [x_spec(0), x_spec(1)] + [t_spec] * 4 + [sp_spec],
        out_specs=[o_spec] * 4,
        out_shape=[jax.ShapeDtypeStruct((h, nc), BF16)] * 4,
        compiler_params=_cparams("parallel", "arbitrary"),
        name="hyena_dft_forward",
    )(*mats['fwd'], x, x, *taps)


def _hyp_inv_kernel(ce_ref, set_ref, cot_ref, sot_ref, gere_ref, geim_ref, gore_ref, goim_ref,
                    gate_ref, prev_ref, bias_ref, *outs, last):
    conv = (_dot(ce_ref[...], gere_ref[...]) + _dot(set_ref[...], geim_ref[...]),
            _dot(cot_ref[...], gore_ref[...]) + _dot(sot_ref[...], goim_ref[...]))
    bias = bias_ref[...]
    for r in range(2):
        y = gate_ref[r] * (conv[r] + prev_ref[r] * bias)
        if last:
            (tok_o,) = outs
            for bb in range(y.shape[1] // HY_WIDTH):
                tok_o[bb, :, r * HY_WIDTH:(r + 1) * HY_WIDTH] = y[:, bb * HY_WIDTH:(bb + 1) * HY_WIDTH].astype(BF16)
        else:
            y_o, ybf_o = outs
            y_o[r] = y
            ybf_o[r] = y.astype(BF16)


def _hyp_inv(mats, g, gate, gate_plane, prev, prev_plane, bias_row, last):
    h, nc = g[0].shape
    tm = min(h, DFT_ROWS)
    cb = _col_block(nc, DFT_INV_COLS)
    grid = (nc // cb, h // tm)
    m_spec = pl.BlockSpec((tm, h), lambda c, i: (i, 0))
    g_spec = pl.BlockSpec((h, cb), lambda c, i: (0, c))
    e_spec = lambda plane: pl.BlockSpec((None, 2, tm, cb), lambda c, i: (plane, 0, i, c))
    b_spec = pl.BlockSpec((1, cb), lambda c, i: (0, c))
    if last:
        out_specs = [pl.BlockSpec((cb // HY_WIDTH, tm, 2 * HY_WIDTH), lambda c, i: (c, i, 0))]
        out_shape = [jax.ShapeDtypeStruct((nc // HY_WIDTH, h, 2 * HY_WIDTH), BF16)]
    else:
        out_specs = [e_spec(0), e_spec(0)]
        out_shape = [jax.ShapeDtypeStruct((1, 2, h, nc), F32), jax.ShapeDtypeStruct((1, 2, h, nc), BF16)]
    return pl.pallas_call(
        functools.partial(_hyp_inv_kernel, last=last),
        grid=grid, in_specs=[m_spec] * 4 + [g_spec] * 4 + [e_spec(gate_plane), e_spec(prev_plane), b_spec],
        out_specs=out_specs, out_shape=out_shape,
        compiler_params=_cparams("parallel", "arbitrary"),
        name="hyena_dft_inverse",
    )(*mats['inv'], *g, gate, prev, bias_row)


def _hyp_tap_tables(spec, filt, length):
    a_re, a_im, b_re, b_im = spec
    w = HY_WIDTH
    inv_n = 1.0 / (2 * length)
    tables = []
    for o in range(HY_ORDER):
        f_sl = slice((2 * o) * w, (2 * o + 1) * w)
        r_sl = slice((2 * o + 1) * w, (2 * o + 2) * w)
        hb0 = filt[0, 0:1, r_sl]
        f1re = (a_re + b_re)[:, f_sl] + (a_re + b_re)[:, r_sl] - hb0
        f1im = (a_im + b_im)[:, f_sl] - (a_im + b_im)[:, r_sl]
        f2re = (a_re - b_re)[:, f_sl] + (a_re - b_re)[:, r_sl] - hb0
        f2im = (b_im - a_im)[:, f_sl] - (b_im - a_im)[:, r_sl]
        dc = f1re[0:1]
        ny = f2re[0:1]
        mid_re = a_im[0:1, f_sl] + a_im[0:1, r_sl] - hb0
        mid_im = -b_im[0:1, f_sl] + b_im[0:1, r_sl]
        sp = jnp.concatenate([dc * inv_n, ny * inv_n, mid_re * (2 * inv_n), mid_im * (2 * inv_n),
                              jnp.zeros((4, w), F32)], axis=0)
        tables.append((f1re * (2 * inv_n), f1im * (2 * inv_n), f2re * (2 * inv_n), f2im * (2 * inv_n), sp))
    return tables


def _hyena_seq(mats, vx, vx_bf, lp, n_batch):
    h = vx.shape[2]
    length = 2 * h
    filt, filt_bf = _hyp_filters(length, lp)
    spec = _hyp_fwd(mats, filt_bf[None], 0)
    tables = _hyp_tap_tables(spec, filt, length)
    bias = lp['hy_bias'].astype(F32)
    y, y_bf = vx, vx_bf
    for o in range(HY_ORDER):
        g = _hyp_fwd(mats, y_bf, 0, tables[o])
        bias_row = jnp.tile(bias[o][None, :], (1, n_batch))
        last = o == HY_ORDER - 1
        res = _hyp_inv(mats, g, vx, o + 1, y, 0, bias_row, last)
        if last:
            return res[0].reshape(n_batch, length, HY_WIDTH)
        y, y_bf = res


def _lru_kernel(u_ref, g_ref, cw_ref, cb_ref, wa_ref, ba_ref, wx_ref, bx_ref, lam_ref, o_ref,
                pad_ref, y_ref, *, seq, ctx_len):
    tc = LRU_CHUNK
    halo = LRU_HALO
    width = LRU_WIDTH
    lat_off = halo
    ctx_off = 2 * halo + seq
    zero = jnp.zeros((halo, width), F32)
    pad_ref[0:halo, :] = zero
    pad_ref[lat_off:lat_off + seq, :] = u_ref[0, 0:seq, :]
    pad_ref[lat_off + seq:ctx_off, :] = zero
    pad_ref[ctx_off:ctx_off + ctx_len, :] = u_ref[0, seq:seq + ctx_len, :]
    pad_ref[ctx_off + ctx_len:ctx_off + ctx_len + halo, :] = zero
    row = lax.broadcasted_iota(I32, (tc, width), 0)
    n_win = tc + 2 * halo

    def chunk(pad_off, y_off, s, carry, d):
        wstart = pl.multiple_of(pad_off + s - halo, 8)
        win = pad_ref[pl.ds(wstart, n_win), :]
        cw = cw_ref[d]
        xc = cb_ref[d]
        for k in range(LRU_CONV):
            shift = (LRU_CONV - 1 - k) if d == 0 else -k
            rolled = win if shift == 0 else pltpu.roll(win, shift % n_win, axis=0)
            xc = xc + cw[k:k + 1] * rolled[halo:halo + tc]
        xb = xc.astype(BF16)
        r = _sigmoid(_dot(xb, wa_ref[d]) + ba_ref[d])
        gi = _sigmoid(_dot(xb, wx_ref[d]) + bx_ref[d])
        lam = lam_ref[d]
        softplus = jnp.maximum(-lam, 0.0) + jnp.log1p(jnp.exp(-jnp.abs(lam)))
        log_a = -LRU_C * r * softplus
        a = jnp.exp(log_a)
        bt = jnp.sqrt(-jnp.tanh(log_a) * (a * a + 1.0)) * (gi * xc)
        sft = 1
        while sft < tc:
            if d == 0:
                keep = row >= sft
                a_s = jnp.where(keep, pltpu.roll(a, sft, axis=0), 1.0)
                b_s = jnp.where(keep, pltpu.roll(bt, sft, axis=0), 0.0)
            else:
                keep = row < tc - sft
                a_s = jnp.where(keep, pltpu.roll(a, tc - sft, axis=0), 1.0)
                b_s = jnp.where(keep, pltpu.roll(bt, tc - sft, axis=0), 0.0)
            bt = a * b_s + bt
            a = a * a_s
            sft *= 2
        h = a * carry + bt
        yo = pl.multiple_of(y_off + s, 8)
        if d == 0:
            y_ref[pl.ds(yo, tc), :] = h
            return h[tc - 1:tc]
        y_ref[pl.ds(yo, tc), :] = y_ref[pl.ds(yo, tc), :] + h
        return h[0:1]

    n_lat = seq // tc
    n_ctx = ctx_len // tc
    for d in range(2):
        carry = jnp.zeros((1, width), F32)
        order = range(n_ctx) if d == 0 else range(n_ctx - 1, -1, -1)
        for c in order:
            carry = chunk(ctx_off, seq, c * tc, carry, d)

        def body(j, cr, d=d):
            jj = j if d == 0 else n_lat - 1 - j
            return chunk(lat_off, 0, jj * tc, cr, d)

        lax.fori_loop(0, n_lat, body, carry)
    o_ref[0] = (y_ref[...] * _gelu_tanh(g_ref[0])).astype(BF16)


def _block_diag(w):
    nd, nb, c, _ = w.shape
    out = jnp.zeros((nd, nb * c, nb * c), w.dtype)
    for n in range(nb):
        out = out.at[:, n * c:(n + 1) * c, n * c:(n + 1) * c].set(w[:, n])
    return out


def _lru_mixer(lu, lg, lp, seq):
    b, nt, w = lu.shape
    ctx_len = nt - seq
    row3 = lambda a: a.reshape(2, 1, w)
    args = (_pad_to(lp['lru_conv_w'], 8, 1), row3(lp['lru_conv_b']), _block_diag(lp['lru_wa']).astype(BF16),
            row3(lp['lru_ba']), _block_diag(lp['lru_wx']).astype(BF16), row3(lp['lru_bx']), row3(lp['lru_lambda']))
    full = lambda a: pl.BlockSpec(a.shape, lambda bb: (0,) * a.ndim)
    tok = pl.BlockSpec((1, nt, w), lambda bb: (bb, 0, 0))
    return pl.pallas_call(
        functools.partial(_lru_kernel, seq=seq, ctx_len=ctx_len),
        grid=(b,),
        in_specs=[tok, tok] + [full(a) for a in args],
        out_specs=tok,
        out_shape=jax.ShapeDtypeStruct((b, nt, w), BF16),
        scratch_shapes=[pltpu.VMEM((nt + 3 * LRU_HALO, w), F32), pltpu.VMEM((nt, w), F32)],
        compiler_params=_cparams("parallel"),
        name="rglru_scan",
    )(lu, lg, *args)


def _merge_kernel(xl_ref, xc_ref, mod_ref, g1_ref, a_ref, b_ref, c_ref, d_ref, wg_ref, wa_ref, wb_ref, wc_ref,
                  wd_ref, wo_ref, o_ref, *, n_lat_tiles):
    group, tm, dm = xl_ref.shape
    m = mod_ref[:, 0]
    x = _stream_tile(xl_ref, xc_ref, n_lat_tiles)
    h = _normmod(x, g1_ref[...], m[:, 1:2], m[:, 0:1]).reshape(group * tm, dm).astype(BF16)
    acc = None
    for k, (br, w) in enumerate(((a_ref, wa_ref), (b_ref, wb_ref), (c_ref, wc_ref), (d_ref, wd_ref))):
        term = (1.0 + jnp.tanh(_dot(h, wg_ref[:, k * dm:(k + 1) * dm]))) * _dot(
            br[...].reshape(group * tm, br.shape[-1]), w[...])
        acc = term if acc is None else acc + term
    y = _dot(acc.astype(BF16), wo_ref[...])
    o_ref[...] = x + m[:, 2:3] * y.reshape(group, tm, dm)


def _merge(stream, modtab, g1, branches, w_gate, lp, n_lat_tiles, tiles_used):
    b, _, d = stream[0].shape
    wbr = lp['w_branch']
    head_rows = lambda w, dv: jnp.concatenate(
        [_pad_to(w[hd * dv:(hd + 1) * dv], LANE, 0) for hd in range(4)], axis=0)
    wbr = 0.5 * wbr
    weights = (head_rows(wbr[0], MLA_V).astype(BF16), head_rows(wbr[1], NA_HEAD_DIM).astype(BF16),
               wbr[2].astype(BF16), wbr[3].astype(BF16), lp['w_out'].astype(BF16))
    group = _batch_group(b)
    full = lambda a: pl.BlockSpec(a.shape, lambda bb, i: (0,) * a.ndim)
    tok = lambda n: pl.BlockSpec((group, TM, n), lambda bb, i: (bb, i, 0))
    return pl.pallas_call(
        functools.partial(_merge_kernel, n_lat_tiles=n_lat_tiles),
        grid=(b // group, tiles_used),
        in_specs=_stream_specs(stream, group, n_lat_tiles)
                 + [pl.BlockSpec((group, 1, 8, d), _kind_map(n_lat_tiles)), full(g1)]
                 + [tok(br.shape[-1]) for br in branches] + [full(w_gate)] + [full(w) for w in weights],
        out_specs=tok(d),
        out_shape=jax.ShapeDtypeStruct((b, tiles_used * TM, d), F32),
        compiler_params=_cparams("parallel", "parallel"),
        name="merge_branches",
    )(stream[0], stream[1], modtab, g1, *branches, w_gate, *weights)


U32 = jnp.uint32
EXPERT_ROWS = 1024


def _pack_pair(x):
    n = x.shape[-1] // 2
    hi = lax.bitcast_convert_type(x[:, :n].astype(BF16).astype(F32), U32)
    lo = lax.bitcast_convert_type(x[:, n:].astype(BF16).astype(F32), U32)
    return hi | (lo >> 16)


def _unpack_pair(p):
    hi = lax.bitcast_convert_type(p & jnp.uint32(0xFFFF0000), F32)
    lo = lax.bitcast_convert_type(p << 16, F32)
    return hi, lo


def _router_kernel(x_ref, mod_ref, g2_ref, rw_ref, rb_ref, tri_ref, h2_o, idx_o, wts_o, rank_o, cnt_o, carry):
    i = pl.program_id(0)

    @pl.when(i == 0)
    def _():
        carry[...] = jnp.zeros_like(carry)

    group, tm, d = x_ref.shape
    m = mod_ref[:, 0]
    h2 = _normmod(x_ref[...], g2_ref[...], m[:, 4:5], m[:, 3:4]).reshape(group * tm, d)
    half = h2.shape[-1] // 2
    h2_o[0] = _pack_pair(h2[:, :half])
    h2_o[1] = _pack_pair(h2[:, half:])
    logits = lax.dot_general(rw_ref[...], h2, (((1,), (1,)), ((), ())), preferred_element_type=F32,
                             precision=lax.Precision.HIGHEST)
    scores = _sigmoid(logits)
    biased = scores + rb_ref[...]
    expert = lax.broadcasted_iota(I32, scores.shape, 0)
    picks = []
    onehot_all = jnp.zeros(scores.shape, F32)
    for _ in range(TOP_K):
        best = jnp.max(biased, axis=0, keepdims=True)
        arg = jnp.min(jnp.where(biased == best, expert, N_EXPERTS), axis=0, keepdims=True)
        hit = expert == arg
        sel = jnp.sum(jnp.where(hit, scores, 0.0), axis=0, keepdims=True)
        biased = jnp.where(hit, -jnp.inf, biased)
        onehot_all = onehot_all + jnp.where(hit, 1.0, 0.0)
        picks.append((arg, hit, sel))
    total = picks[0][2]
    for _, _, sel in picks[1:]:
        total = total + sel
    earlier = _dot(onehot_all.astype(BF16), tri_ref[...]) + carry[...]
    pad_rows = TOPK_PAD - TOP_K
    ranks = [jnp.sum(jnp.where(hit, earlier, 0.0), axis=0, keepdims=True).astype(I32) for _, hit, _ in picks]
    scale = ROUTED_SCALE / total
    n_tok = scores.shape[1]
    idx_o[...] = jnp.concatenate([arg for arg, _, _ in picks] + [jnp.zeros((pad_rows, n_tok), I32)], axis=0)
    wts_o[...] = jnp.concatenate([sel * scale for _, _, sel in picks] + [jnp.zeros((pad_rows, n_tok), F32)], axis=0)
    rank_o[...] = jnp.concatenate(ranks + [jnp.zeros((pad_rows, n_tok), I32)], axis=0)
    carry[...] = carry[...] + jnp.sum(onehot_all, axis=1, keepdims=True)
    cnt_o[...] = carry[...]


def _tile_maps(n_groups, n_lat_tiles):
    tok = lambda f: (f % n_groups, f // n_groups, 0)
    mod = lambda f: (f % n_groups, jnp.where(f // n_groups >= n_lat_tiles, 1, 0), 0, 0)
    return tok, mod


def _route(x1, modtab, g2, lp, n_lat_tiles, tiles_used):
    b, _, d = x1.shape
    group = _batch_group(b)
    rows = group * TM
    t = b * tiles_used * TM
    rw = lp['router_w'].T
    rb = lp['router_bias'].reshape(-1, 1)
    tri = (np.arange(rows)[:, None] < np.arange(rows)[None, :]).astype(np.float32)
    tri = jnp.asarray(tri, BF16)
    per_tok = lambda: pl.BlockSpec((TOPK_PAD, rows), lambda i: (0, i))
    full = lambda a: pl.BlockSpec(a.shape, lambda i: (0,) * a.ndim)
    tok, mod = _tile_maps(b // group, n_lat_tiles)
    return pl.pallas_call(
        _router_kernel,
        grid=(t // rows,),
        in_specs=[pl.BlockSpec((group, TM, d), tok), pl.BlockSpec((group, 1, 8, d), mod),
                  full(g2), full(rw), full(rb), full(tri)],
        out_specs=[pl.BlockSpec((2, rows, d // 4), lambda i: (0, i, 0)), per_tok(), per_tok(), per_tok(),
                   pl.BlockSpec((N_EXPERTS, 1), lambda i: (0, 0))],
        out_shape=[jax.ShapeDtypeStruct((2, t, d // 4), U32), jax.ShapeDtypeStruct((TOPK_PAD, t), I32),
                   jax.ShapeDtypeStruct((TOPK_PAD, t), F32), jax.ShapeDtypeStruct((TOPK_PAD, t), I32),
                   jax.ShapeDtypeStruct((N_EXPERTS, 1), F32)],
        scratch_shapes=[pltpu.VMEM((N_EXPERTS, 1), F32)],
        compiler_params=_cparams("arbitrary"),
        name="moe_router",
    )(x1, modtab, g2, rw, rb, tri)


SC_WINDOW = 128


def _sc_mesh():
    return plsc.VectorSubcoreMesh(core_axis_name="c", subcore_axis_name="s")


def _sc_scatter_rows(src, idx, n_out):
    n, width = src.shape
    k_rep = idx.shape[0]
    half = n // SC_WINDOW // 2

    @functools.partial(pl.kernel, out_type=jax.ShapeDtypeStruct((n_out, width), src.dtype), mesh=_sc_mesh(),
                       scratch_types=[], name="moe_dispatch_sc")
    def scatter(src_hbm, idx_hbm, out_hbm):
        def body(x_vmem, *i_vmems):
            for i_vmem in i_vmems:
                pltpu.sync_copy(x_vmem, out_hbm.at[i_vmem.at[0]])

        pltpu.emit_pipeline(
            body,
            grid=(2, half),
            in_specs=[pl.BlockSpec((SC_WINDOW, width), lambda a, i: (a * half + i, 0))]
                     + [pl.BlockSpec((1, SC_WINDOW), lambda a, i, k=k: (k, a * half + i)) for k in range(k_rep)],
            out_specs=[],
            core_axis_name=("c", "s"),
            dimension_semantics=(pltpu.PARALLEL, pltpu.PARALLEL),
        )(src_hbm, *([idx_hbm] * k_rep))

    return scatter(src, idx)


def _sc_gather_rows(src, idx):
    k_rep, n = idx.shape
    width = src.shape[1]
    n_win = n // SC_WINDOW

    @functools.partial(pl.kernel, out_type=jax.ShapeDtypeStruct((k_rep * n, width), src.dtype), mesh=_sc_mesh(),
                       scratch_types=[], name="moe_gather_sc")
    def gather(src_hbm, idx_hbm, out_hbm):
        def body(i_vmem, o_vmem):
            pltpu.sync_copy(src_hbm.at[i_vmem.at[0]], o_vmem)

        pltpu.emit_pipeline(
            body,
            grid=(k_rep, n_win),
            in_specs=[pl.BlockSpec((1, SC_WINDOW), lambda k, i: (k, i))],
            out_specs=[pl.BlockSpec((SC_WINDOW, width), lambda k, i: (k * n_win + i, 0))],
            core_axis_name=("c", "s"),
            dimension_semantics=(pltpu.PARALLEL, pltpu.PARALLEL),
        )(idx_hbm, out_hbm)

    return gather(src, idx)


def _unpack_planes(p0, p1):
    return _unpack_pair(p0) + _unpack_pair(p1)


def _dot_quarters(parts, w_ref):
    q = parts[0].shape[-1]
    acc = None
    for j, part in enumerate(parts):
        term = _dot(part.astype(BF16), w_ref[j * q:(j + 1) * q, :])
        acc = term if acc is None else acc + term
    return acc


def _expert_kernel(be_ref, nv_ref, xs_ref, wg_ref, wu_ref, wd_ref, ys_o, wg_s, wu_s, wd_s):
    i = pl.program_id(0)
    prev = be_ref[jnp.maximum(i - 1, 0)]

    @pl.when((i == 0) | (be_ref[i] != prev))
    def _():
        wg_s[...] = (0.5 * wg_ref[0]).astype(BF16)
        wu_s[...] = wu_ref[0].astype(BF16)
        wd_s[...] = wd_ref[0].astype(BF16)

    @pl.when(nv_ref[i] > 0)
    def _():
        keep = lax.broadcasted_iota(I32, xs_ref.shape[1:], 0) < nv_ref[i]
        parts = _unpack_planes(jnp.where(keep, xs_ref[0], jnp.uint32(0)), jnp.where(keep, xs_ref[1], jnp.uint32(0)))
        hid = _half_silu(_dot_quarters(parts, wg_s)) * _dot_quarters(parts, wu_s)
        y = _dot(hid.astype(BF16), wd_s[...])
        half = y.shape[-1] // 2
        ys_o[0] = _pack_pair(y[:, :half])
        ys_o[1] = _pack_pair(y[:, half:])

    @pl.when(nv_ref[i] <= 0)
    def _():
        ys_o[...] = jnp.zeros_like(ys_o)


def _experts(xs, block_e, n_valid, weights, layer):
    _, n_rows, dq = xs.shape
    d = 4 * dq
    n_blocks = n_rows // EXPERT_ROWS
    hid = EXPERT_HIDDEN
    grid_spec = pltpu.PrefetchScalarGridSpec(
        num_scalar_prefetch=2,
        grid=(n_blocks,),
        in_specs=[pl.BlockSpec((2, EXPERT_ROWS, dq), lambda i, be, nv: (0, i, 0)),
                  pl.BlockSpec((None, 1, d, hid), lambda i, be, nv: (layer, be[i], 0, 0)),
                  pl.BlockSpec((None, 1, d, hid), lambda i, be, nv: (layer, be[i], 0, 0)),
                  pl.BlockSpec((None, 1, hid, d), lambda i, be, nv: (layer, be[i], 0, 0))],
        out_specs=pl.BlockSpec((2, EXPERT_ROWS, dq), lambda i, be, nv: (0, i, 0)),
        scratch_shapes=[pltpu.VMEM((d, hid), BF16), pltpu.VMEM((d, hid), BF16), pltpu.VMEM((hid, d), BF16)],
    )
    return pl.pallas_call(
        _expert_kernel,
        grid_spec=grid_spec,
        out_shape=jax.ShapeDtypeStruct((2, n_rows, dq), U32),
        compiler_params=_cparams("arbitrary"),
        name="moe_experts",
    )(block_e, n_valid, xs, *weights)


def _combine_kernel(g_ref, wts_ref, h2_ref, x_ref, mod_ref, sg_ref, su_ref, sd_ref, gf_ref, o_ref, *, final):
    parts = _unpack_planes(h2_ref[0], h2_ref[1])
    hid = _half_silu(_dot_quarters(parts, sg_ref)) * _dot_quarters(parts, su_ref)
    shared = _dot(hid.astype(BF16), sd_ref[...])
    wts = wts_ref[...]
    routed = None
    for k in range(TOP_K):
        w = wts[:, k:k + 1]
        terms = [w * part for part in _unpack_planes(g_ref[k, 0], g_ref[k, 1])]
        routed = terms if routed is None else [r + t for r, t in zip(routed, terms)]
    f = shared + jnp.concatenate(routed, axis=1)
    m = mod_ref[:, 0]
    x2 = x_ref[...] + m[:, 5:6] * f.reshape(x_ref.shape)
    if final:
        x2 = x2 * lax.rsqrt(jnp.mean(x2 * x2, axis=-1, keepdims=True) + NORM_EPS) * gf_ref[...]
    o_ref[...] = x2


def _combine(g, wts, h2p, x1, modtab, lp, g_final, n_lat_tiles, tiles_used, final):
    b, _, d = x1.shape
    dq = d // 4
    weights = ((0.5 * lp['sh_w_gate']).astype(BF16), lp['sh_w_up'].astype(BF16), lp['sh_w_down'].astype(BF16),
               g_final.reshape(1, -1))
    full = lambda a: pl.BlockSpec(a.shape, lambda i: (0,) * a.ndim)
    group = _batch_group(b)
    rows = group * TM
    tok, mod = _tile_maps(b // group, n_lat_tiles)
    return pl.pallas_call(
        functools.partial(_combine_kernel, final=final),
        grid=(b * tiles_used * TM // rows,),
        in_specs=[pl.BlockSpec((TOP_K, 2, rows, dq), lambda i: (0, 0, i, 0)),
                  pl.BlockSpec((rows, TOPK_PAD), lambda i: (i, 0)),
                  pl.BlockSpec((2, rows, dq), lambda i: (0, i, 0)),
                  pl.BlockSpec((group, TM, d), tok), pl.BlockSpec((group, 1, 8, d), mod)]
                 + [full(w) for w in weights],
        out_specs=pl.BlockSpec((group, TM, d), tok),
        out_shape=jax.ShapeDtypeStruct((b, tiles_used * TM, d), F32),
        compiler_params=_cparams("parallel"),
        name="moe_combine",
    )(g, wts, h2p, x1, modtab, *weights)


def _sorted_rows_kernel(idx_ref, rank_ref, ps_ref, o_ref):
    expert = lax.broadcasted_iota(I32, (N_EXPERTS, idx_ref.shape[1]), 0)
    starts = ps_ref[...]
    idx = idx_ref[...]
    rows = [jnp.sum(jnp.where(idx[k:k + 1] == expert, starts, 0), axis=0, keepdims=True) for k in range(TOP_K)]
    pad = jnp.zeros((TOPK_PAD - TOP_K, idx.shape[1]), I32)
    o_ref[...] = jnp.concatenate(rows + [pad], axis=0) + rank_ref[...]


def _sorted_rows(idx, rank, p_starts):
    t = idx.shape[1]
    cols = TM * max(k for k in (8, 4, 2, 1) if (t // TM) % k == 0)
    per_tok = pl.BlockSpec((TOPK_PAD, cols), lambda i: (0, i))
    return pl.pallas_call(
        _sorted_rows_kernel,
        grid=(t // cols,),
        in_specs=[per_tok, per_tok, pl.BlockSpec((N_EXPERTS, 1), lambda i: (0, 0))],
        out_specs=per_tok,
        out_shape=jax.ShapeDtypeStruct((TOPK_PAD, t), I32),
        compiler_params=_cparams("parallel"),
        name="moe_sorted_rows",
    )(idx, rank, p_starts.reshape(-1, 1))


def _moe(x1, modtab, g2, lp, g_final, n_lat_tiles, final):
    b, nt, d = x1.shape
    tiles_used = n_lat_tiles if final else nt // TM
    t = b * tiles_used * TM
    h2p, idx, wts, rank, cnt = _route(x1, modtab, g2, lp, n_lat_tiles, tiles_used)
    counts = cnt[:, 0].astype(I32)
    padded = (counts + EXPERT_ROWS - 1) // EXPERT_ROWS * EXPERT_ROWS
    p_ends = jnp.cumsum(padded)
    p_starts = p_ends - padded
    n_blocks = (t * TOP_K + N_EXPERTS * (EXPERT_ROWS - 1)) // EXPERT_ROWS
    n_rows = n_blocks * EXPERT_ROWS
    dest = _sorted_rows(idx, rank, p_starts)[:TOP_K]
    plane_idx = jnp.concatenate([dest, dest + n_rows], axis=1)
    blk_start = jnp.arange(n_blocks, dtype=I32) * EXPERT_ROWS
    block_e = jnp.minimum(jnp.sum((p_ends[None, :] <= blk_start[:, None]).astype(I32), axis=1), N_EXPERTS - 1)
    n_valid = jnp.clip((p_starts + counts)[block_e] - blk_start, 0, EXPERT_ROWS).astype(I32)
    dq = d // 4
    xs = _sc_scatter_rows(h2p.reshape(2 * t, dq), plane_idx, 2 * n_rows).reshape(2, n_rows, dq)
    ys = _experts(xs, block_e, n_valid, lp['expert_stacks'], lp['layer'])
    g = _sc_gather_rows(ys.reshape(2 * n_rows, dq), plane_idx).reshape(TOP_K, 2, t, dq)
    return _combine(g, wts.T, h2p, x1, modtab, lp, g_final, n_lat_tiles, tiles_used, final)


def _layer(stream, nt, c, c_ctx, lp, consts, g_final, seq, final):
    b = stream[0].shape[0]
    n_lat_tiles = seq // TM
    rope, mats_lat, mats_ctx = consts
    modtab = _mod_table(c, c_ctx, *lp['mod_stacks'], lp['layer'])
    g1 = lp['g_norm1'].reshape(1, -1)
    g2 = lp['g_norm2'].reshape(1, -1)
    pw = _proj_weights(lp)
    q, k, v, nq, nk, nv, hy, lu, lg = _project(stream, nt, modtab, g1, pw, rope, n_lat_tiles)
    tiles_used = n_lat_tiles if final else nt // TM
    br_a = _mla_attention(q, k, v, seq, tiles_used)
    br_b = _na_attention(nq, nk, nv, _na_bias_tables(lp['na_rpb'], seq // GRID_W), seq, tiles_used)
    pre = _hyp_pre(hy, lp['hy_short_w'], lp['hy_short_b'], seq)
    lat_f, lat_bf, ctx_f, ctx_bf = pre
    br_c = _hyena_seq(mats_lat, lat_f, lat_bf, lp, b)
    if not final:
        br_c = jnp.concatenate([br_c, _hyena_seq(mats_ctx, ctx_f, ctx_bf, lp, b)], axis=1)
    br_d = _lru_mixer(lu, lg, lp, seq)
    x1 = _merge(stream, modtab, g1, (br_a, br_b, br_c, br_d), pw['w_gate'], lp, n_lat_tiles, tiles_used)
    return _moe(x1, modtab, g2, lp, g_final, n_lat_tiles, final)


_LAYER_KEYS = ('w_mod', 'b_mod', 'g_norm1', 'g_norm2', 'w_in', 'mla_g_q', 'mla_w_uq', 'mla_g_kv', 'mla_w_ukv',
               'na_rpb', 'hy_short_w', 'hy_short_b', 'hy_w1', 'hy_b1', 'hy_w2', 'hy_b2', 'hy_w3', 'hy_decay',
               'hy_bias', 'lru_conv_w', 'lru_conv_b', 'lru_wa', 'lru_ba', 'lru_wx', 'lru_bx', 'lru_lambda',
               'w_branch', 'w_out', 'router_w', 'router_bias', 'exp_w_gate', 'exp_w_up', 'exp_w_down',
               'sh_w_gate', 'sh_w_up', 'sh_w_down')


def kernel(x, c, ctx, c_ctx, w_mod, b_mod, g_norm1, g_norm2, w_in, mla_g_q, mla_w_uq, mla_g_kv, mla_w_ukv, na_rpb, hy_short_w, hy_short_b, hy_w1, hy_b1, hy_w2, hy_b2, hy_w3, hy_decay, hy_bias, lru_conv_w, lru_conv_b, lru_wa, lru_ba, lru_wx, lru_bx, lru_lambda, w_branch, w_out, router_w, router_bias, exp_w_gate, exp_w_up, exp_w_down, sh_w_gate, sh_w_up, sh_w_down, g_final):
    stacked = dict(zip(_LAYER_KEYS, (w_mod, b_mod, g_norm1, g_norm2, w_in, mla_g_q, mla_w_uq, mla_g_kv, mla_w_ukv,
                                     na_rpb, hy_short_w, hy_short_b, hy_w1, hy_b1, hy_w2, hy_b2, hy_w3, hy_decay,
                                     hy_bias, lru_conv_w, lru_conv_b, lru_wa, lru_ba, lru_wx, lru_bx, lru_lambda,
                                     w_branch, w_out, router_w, router_bias, exp_w_gate, exp_w_up, exp_w_down,
                                     sh_w_gate, sh_w_up, sh_w_down)))
    b, seq, d = x.shape
    ctx_len = ctx.shape[1]
    depth = w_mod.shape[0]
    assert seq % TM == 0 and ctx_len % TM == 0 and seq // GRID_W >= NA_KEY_ROWS + 1
    nt = seq + ctx_len
    stream = (x, ctx, 0)
    consts = (_rope_tables(seq, seq + ctx_len), _hyp_matrices(seq), _hyp_matrices(ctx_len))
    for i in range(depth):
        big = ('w_mod', 'b_mod', 'exp_w_gate', 'exp_w_up', 'exp_w_down')
        lp = {name: w[i] for name, w in stacked.items() if name not in big}
        lp['layer'] = i
        lp['mod_stacks'] = (w_mod, b_mod)
        lp['expert_stacks'] = (exp_w_gate, exp_w_up, exp_w_down)
        xa = _layer(stream, nt, c, c_ctx, lp, consts, g_final, seq, i == depth - 1)
        stream = (xa, xa, seq // TM)
    return xa
```

```python
import functools
import math

import numpy as np
import jax
import jax.numpy as jnp
from jax import lax
from jax.experimental import pallas as pl
from jax.experimental.pallas import tpu as pltpu
from jax.experimental.pallas import tpu_sc as plsc

F32 = jnp.float32
BF16 = jnp.bfloat16
I32 = jnp.int32

TM = 256
LANE = 128
GRID_W = 64
N_MOD = 6
NORM_EPS = 1e-6

MLA_HEADS, MLA_NOPE, MLA_ROPE, MLA_V = 4, 64, 32, 64
MLA_Q_RANK, MLA_KV_RANK = 192, 128
MLA_Q_PAD = 256
ROPE_THETA = 10000.0

NA_HEADS, NA_HEAD_DIM, NA_WIN_R, NA_WIN_C = 4, 64, 8, 16
NA_TILE_ROWS = TM // GRID_W
NA_KEY_ROWS = NA_TILE_ROWS + NA_WIN_R - 1
NA_KEYS = NA_KEY_ROWS * GRID_W

HY_WIDTH, HY_ORDER, HY_SHORT, HY_BANDS, HY_FFN = 256, 2, 3, 16, 64
HY_EMB = 2 * HY_BANDS + 1

LRU_WIDTH, LRU_BLOCKS, LRU_CONV, LRU_C = 256, 4, 4, 8.0
LRU_CHUNK = 256
LRU_HALO = 8

N_EXPERTS, TOP_K, EXPERT_HIDDEN, ROUTED_SCALE, MOE_BLOCK = 64, 6, 256, 2.5, 256
TOPK_PAD = 8

VMEM_LIMIT = 56 * 1024 * 1024


def _cparams(*sem):
    return pltpu.CompilerParams(dimension_semantics=sem, vmem_limit_bytes=VMEM_LIMIT)


def _dot(a, b):
    return jnp.dot(a, b, preferred_element_type=F32)


def _dot_nt(a, b):
    return lax.dot_general(a, b, (((1,), (1,)), ((), ())), preferred_element_type=F32)


def _sigmoid(x):
    return jax.nn.sigmoid(x)


def _half_silu(g):
    return g * (1.0 + jnp.tanh(g))


def _silu(x):
    return x * _sigmoid(x)


def _gelu_tanh(x):
    return 0.5 * x * (1.0 + jnp.tanh(math.sqrt(2.0 / math.pi) * (x + 0.044715 * (x * x * x))))


def _normmod(x, g, scale, shift):
    y = x * lax.rsqrt(jnp.mean(x * x, axis=-1, keepdims=True) + NORM_EPS) * g
    return y * (1.0 + scale) + shift


MOD_COLS = 1024


def _mod_kernel(c_ref, w_ref, b_ref, o_ref):
    s = _silu(c_ref[...])
    o_ref[...] = _dot(s.astype(BF16), w_ref[...].astype(BF16)) + b_ref[...]


def _mod_table(c, c_ctx, w_mod, b_mod, layer):
    b, d = c.shape
    rows = -(-(b + 1) // 16) * 16
    cc = jnp.zeros((rows, d), F32).at[:b].set(c).at[b].set(c_ctx)
    tn = MOD_COLS
    mod = pl.pallas_call(
        _mod_kernel,
        grid=(N_MOD * d // tn,),
        in_specs=[pl.BlockSpec((rows, d), lambda j: (0, 0)),
                  pl.BlockSpec((None, d, tn), lambda j: (layer, 0, j)),
                  pl.BlockSpec((None, 1, tn), lambda j: (layer, 0, j))],
        out_specs=pl.BlockSpec((rows, tn), lambda j: (0, j)),
        out_shape=jax.ShapeDtypeStruct((rows, N_MOD * d), F32),
        compiler_params=_cparams("arbitrary"),
        name="mod_vectors",
    )(cc, w_mod, b_mod[:, None, :])
    lat = mod[:b].reshape(b, N_MOD, d)
    ctx = jnp.broadcast_to(mod[b].reshape(1, N_MOD, d), (b, N_MOD, d))
    tab = jnp.stack([lat, ctx], axis=1)
    return jnp.pad(tab, ((0, 0), (0, 0), (0, 8 - N_MOD), (0, 0)))


_C_QLAT, _C_KVLAT, _C_KR, _C_KRR, _C_NA, _C_HY, _C_LU, _C_LG, _C_END = (
    0, 256, 384, 512, 640, 640 + 3 * NA_HEADS * LANE, 640 + 1536 + 768, 640 + 1536 + 1024, 640 + 1536 + 1280)


def _stream_tile(lat_ref, ctx_ref, n_lat_tiles):
    return jnp.where(pl.program_id(1) < n_lat_tiles, lat_ref[...], ctx_ref[...])


def _stream_specs(stream, group, n_lat_tiles):
    lat_src, ctx_src, ctx_tile0 = stream
    d = lat_src.shape[-1]
    return [pl.BlockSpec((group, TM, d), lambda bb, i: (bb, jnp.minimum(i, n_lat_tiles - 1), 0)),
            pl.BlockSpec((group, TM, d), lambda bb, i: (bb, ctx_tile0 + jnp.maximum(i - n_lat_tiles, 0), 0))]


def _proj_kernel(xl_ref, xc_ref, mod_ref, g1_ref, w1_ref, gq_ref, gkv_ref, wuq_ref, wuqr_ref, wk_ref, wv_ref,
                 cos_ref, sin_ref, q_o, k_o, v_o, nq_o, nk_o, nv_o, hy_o, lu_o, lg_o, *, n_lat_tiles):
    group, tm, d = xl_ref.shape
    m = mod_ref[:, 0]
    x = _stream_tile(xl_ref, xc_ref, n_lat_tiles)
    h = _normmod(x, g1_ref[...], m[:, 1:2], m[:, 0:1]).reshape(group * tm, d)
    z = _dot(h.astype(BF16), w1_ref[...])
    qlat = z[:, _C_QLAT:_C_KVLAT]
    kvlat = z[:, _C_KVLAT:_C_KR]
    kr = z[:, _C_KR:_C_KRR]
    krr = z[:, _C_KRR:_C_NA]
    qn = qlat * lax.rsqrt(jnp.sum(qlat * qlat, axis=-1, keepdims=True) * (1.0 / MLA_Q_RANK) + NORM_EPS) * gq_ref[...]
    kvn = kvlat * lax.rsqrt(jnp.mean(kvlat * kvlat, axis=-1, keepdims=True) + NORM_EPS) * gkv_ref[...]
    qn = qn.astype(BF16)
    kvn = kvn.astype(BF16)
    q = _dot(qn, wuq_ref[...])
    qr = _dot(qn, wuqr_ref[...])
    kk = _dot(kvn, wk_ref[...])
    vv = _dot(kvn, wv_ref[...])
    cos = jnp.concatenate([cos_ref[...]] * group, axis=0)
    sin = jnp.concatenate([sin_ref[...]] * group, axis=0)
    krope = kr * cos + krr * sin
    den_lane = lax.broadcasted_iota(I32, cos.shape, 1) == SOFTMAX_DEN_LANE
    def put_heads(ref, hd, val):
        for g in range(group):
            ref[g, hd] = val[g * tm:(g + 1) * tm].astype(BF16)

    def put_pairs(ref, hd, val):
        off = (hd % 2) * LANE
        for g in range(group):
            ref[g, hd // 2, :, off:off + LANE] = val[g * tm:(g + 1) * tm].astype(BF16)

    for hd in range(MLA_HEADS):
        sl = slice(hd * LANE, (hd + 1) * LANE)
        put_heads(q_o, hd, q[:, sl] * cos + qr[:, sl] * sin)
        put_heads(k_o, hd, kk[:, sl] + krope)
        put_heads(v_o, hd, jnp.where(den_lane, 1.0, vv[:, sl]))
    for hd in range(NA_HEADS):
        for which, ref in enumerate((nq_o, nk_o, nv_o)):
            lo = _C_NA + (which * NA_HEADS + hd) * LANE
            blk = z[:, lo:lo + LANE]
            if which == 2:
                put_pairs(ref, hd, jnp.where(den_lane, 1.0, blk))
            else:
                put_heads(ref, hd, blk)
    hy_o[...] = z[:, _C_HY:_C_LU].reshape(group, tm, _C_LU - _C_HY)
    lu_o[...] = z[:, _C_LU:_C_LG].reshape(group, tm, _C_LG - _C_LU)
    lg_o[...] = z[:, _C_LG:_C_END].reshape(group, tm, _C_END - _C_LG)


def _pad_to(a, n, axis):
    pad = [(0, 0)] * a.ndim
    pad[axis] = (0, n - a.shape[axis])
    return jnp.pad(a, pad)


def _rot_cols(w):
    half = w.shape[-1] // 2
    return jnp.concatenate([-w[..., half:], w[..., :half]], axis=-1)


def _head_blocks(cols_per_head):
    out = []
    for pieces in cols_per_head:
        k = pieces[0][0].shape[0]
        blk = jnp.zeros((k, LANE), F32)
        for arr, off in pieces:
            blk = blk.at[:, off:off + arr.shape[1]].set(arr)
        out.append(blk)
    return jnp.concatenate(out, axis=1)


def _proj_weights(lp):
    w_in = lp['w_in']
    d = w_in.shape[0]
    o = 0
    parts = {}
    for name, n in (('q', MLA_Q_RANK), ('kv', MLA_KV_RANK), ('kr', MLA_ROPE), ('na', 3 * NA_HEADS * NA_HEAD_DIM),
                    ('hy', 3 * HY_WIDTH), ('lu', LRU_WIDTH), ('lg', LRU_WIDTH), ('gt', 4 * d)):
        parts[name] = w_in[:, o:o + n]
        o += n
    zeros = lambda n: jnp.zeros((d, n), F32)
    kr_blk = jnp.concatenate([zeros(MLA_NOPE), parts['kr'], zeros(LANE - MLA_NOPE - MLA_ROPE)], axis=1)
    krr_blk = jnp.concatenate([zeros(MLA_NOPE), _rot_cols(parts['kr']), zeros(LANE - MLA_NOPE - MLA_ROPE)], axis=1)
    na_scale = NA_HEAD_DIM ** -0.5
    na_cols = []
    for which in range(3):
        for hd in range(NA_HEADS):
            lo = (which * NA_HEADS + hd) * NA_HEAD_DIM
            blk = parts['na'][:, lo:lo + NA_HEAD_DIM] * (na_scale if which == 0 else 1.0)
            na_cols.append(_pad_to(blk, LANE, 1))
    w1 = jnp.concatenate([_pad_to(parts['q'], MLA_Q_PAD, 1), parts['kv'], kr_blk, krr_blk] + na_cols
                         + [parts['hy'], parts['lu'], parts['lg']], axis=1).astype(BF16)
    mla_scale = (MLA_NOPE + MLA_ROPE) ** -0.5
    wuq = _pad_to(lp['mla_w_uq'], MLA_Q_PAD, 0) * mla_scale
    dq = MLA_NOPE + MLA_ROPE
    wuq_main = _head_blocks([[(wuq[:, hd * dq:hd * dq + dq], 0)] for hd in range(MLA_HEADS)])
    wuq_rot = _head_blocks([[(_rot_cols(wuq[:, hd * dq + MLA_NOPE:hd * dq + dq]), MLA_NOPE)]
                            for hd in range(MLA_HEADS)])
    dkv = MLA_NOPE + MLA_V
    wukv = lp['mla_w_ukv']
    wk = _head_blocks([[(wukv[:, hd * dkv:hd * dkv + MLA_NOPE], 0)] for hd in range(MLA_HEADS)])
    wv = _head_blocks([[(wukv[:, hd * dkv + MLA_NOPE:hd * dkv + dkv], 0)] for hd in range(MLA_HEADS)])
    gq = _pad_to(lp['mla_g_q'].reshape(1, -1), MLA_Q_PAD, 1)
    gkv = lp['mla_g_kv'].reshape(1, -1)
    return dict(w1=w1, w_gate=(0.5 * parts['gt']).astype(BF16), gq=gq, gkv=gkv, wuq=wuq_main.astype(BF16),
                wuq_rot=wuq_rot.astype(BF16), wk=wk.astype(BF16), wv=wv.astype(BF16))


def _rope_tables(seq, n_tok):
    t = jnp.arange(seq, dtype=I32)
    row = (t // GRID_W).astype(F32)
    col = (t % GRID_W).astype(F32)
    n_axis = MLA_ROPE // 4
    inv_freq = ROPE_THETA ** (-jnp.arange(n_axis, dtype=F32) / n_axis)
    ang = jnp.concatenate([row[:, None] * inv_freq, col[:, None] * inv_freq], axis=-1)
    cos = jnp.concatenate([jnp.cos(ang), jnp.cos(ang)], axis=-1)
    sin = jnp.concatenate([jnp.sin(ang), jnp.sin(ang)], axis=-1)
    cos_t = jnp.ones((n_tok, LANE), F32).at[:seq, MLA_NOPE:MLA_NOPE + MLA_ROPE].set(cos)
    sin_t = jnp.zeros((n_tok, LANE), F32).at[:seq, MLA_NOPE:MLA_NOPE + MLA_ROPE].set(sin)
    return cos_t, sin_t


def _batch_group(b, cap=2):
    return max(g for g in (4, 2, 1) if g <= cap and b % g == 0)


MOE_BATCH_GROUP = 4


def _kind_map(n_lat_tiles):
    return lambda b, i: (b, jnp.where(i >= n_lat_tiles, 1, 0), 0, 0)


def _project(stream, nt, modtab, g1, pw, rope, n_lat_tiles):
    b, _, d = stream[0].shape
    cos_t, sin_t = rope
    group = _batch_group(b)
    full = lambda a: pl.BlockSpec(a.shape, lambda bb, i: (0,) * a.ndim)
    head_out = lambda: pl.BlockSpec((group, MLA_HEADS, TM, LANE), lambda bb, i: (bb, 0, i, 0))
    tok_out = lambda n: pl.BlockSpec((group, TM, n), lambda bb, i: (bb, i, 0))
    head_shape = jax.ShapeDtypeStruct((b, MLA_HEADS, nt, LANE), BF16)
    pair_out = lambda: pl.BlockSpec((group, MLA_HEADS // 2, TM, 2 * LANE), lambda bb, i: (bb, 0, i, 0))
    pair_shape = jax.ShapeDtypeStruct((b, MLA_HEADS // 2, nt, 2 * LANE), BF16)
    tok_shape = lambda n: jax.ShapeDtypeStruct((b, nt, n), F32)
    weights = (g1, pw['w1'], pw['gq'], pw['gkv'], pw['wuq'], pw['wuq_rot'], pw['wk'], pw['wv'])
    return pl.pallas_call(
        functools.partial(_proj_kernel, n_lat_tiles=n_lat_tiles),
        grid=(b // group, nt // TM),
        in_specs=_stream_specs(stream, group, n_lat_tiles)
                 + [pl.BlockSpec((group, 1, 8, d), _kind_map(n_lat_tiles))]
                 + [full(w) for w in weights]
                 + [pl.BlockSpec((TM, LANE), lambda bb, i: (i, 0))] * 2,
        out_specs=[head_out(), head_out(), head_out(), head_out(), head_out(), pair_out(),
                   tok_out(3 * HY_WIDTH), tok_out(LRU_WIDTH), tok_out(LRU_WIDTH)],
        out_shape=[head_shape, head_shape, head_shape, head_shape, head_shape, pair_shape,
                   tok_shape(3 * HY_WIDTH), tok_shape(LRU_WIDTH), tok_shape(LRU_WIDTH)],
        compiler_params=_cparams("parallel", "parallel"),
        name="input_projection",
    )(stream[0], stream[1], modtab, *weights, cos_t, sin_t)


SOFTMAX_DEN_LANE = 64


def _softmax_pv(parts, lane_off=0):
    m = None
    for s, _ in parts:
        mm = jnp.max(s, axis=-1, keepdims=True)
        m = mm if m is None else jnp.maximum(m, mm)
    acc = None
    for s, v in parts:
        o = _dot(jnp.exp(s - m).astype(BF16), v)
        acc = o if acc is None else acc + o
    den = lane_off + SOFTMAX_DEN_LANE
    return acc[:, lane_off:lane_off + LANE] / acc[:, den:den + 1]


def _mla_kernel(q_ref, k_ref, v_ref, o_ref, *, seq, n_lat_tiles):
    i = pl.program_id(1)
    nt = k_ref.shape[2]

    def attend(lo, hi):
        for hd in range(MLA_HEADS):
            s = _dot_nt(q_ref[0, hd], k_ref[0, hd, lo:hi, :])
            o = _softmax_pv([(s, v_ref[0, hd, lo:hi, :])])
            o_ref[0, :, hd * LANE:(hd + 1) * LANE] = o.astype(BF16)

    @pl.when(i < n_lat_tiles)
    def _():
        attend(0, nt)

    @pl.when(i >= n_lat_tiles)
    def _():
        attend(seq, nt)


def _mla_attention(q, k, v, seq, tiles_used):
    b, h, nt, _ = q.shape
    kv_spec = pl.BlockSpec((1, h, nt, LANE), lambda bb, i: (bb, 0, 0, 0))
    return pl.pallas_call(
        functools.partial(_mla_kernel, seq=seq, n_lat_tiles=seq // TM),
        grid=(b, tiles_used),
        in_specs=[pl.BlockSpec((1, h, TM, LANE), lambda bb, i: (bb, 0, i, 0)), kv_spec, kv_spec],
        out_specs=pl.BlockSpec((1, TM, h * LANE), lambda bb, i: (bb, i, 0)),
        out_shape=jax.ShapeDtypeStruct((b, tiles_used * TM, h * LANE), BF16),
        compiler_params=_cparams("parallel", "arbitrary"),
        name="mla_attention",
    )(q, k, v)


def _na_bias_tables(rpb, rows):
    n_blk = rows // NA_TILE_ROWS
    col = np.arange(GRID_W)
    c0 = np.clip(col - NA_WIN_C // 2, 0, GRID_W - NA_WIN_C)
    in_win = (col[None, :] >= c0[:, None]) & (col[None, :] < c0[:, None] + NA_WIN_C)
    dc = np.clip(col[None, :] - col[:, None], 1 - NA_WIN_C, NA_WIN_C - 1) + NA_WIN_C - 1
    rpb = rpb.astype(F32)
    tables = []
    for j in (0, 1, n_blk - 1):
        w0 = min(max(NA_TILE_ROWS * j - NA_WIN_R // 2, 0), rows - NA_KEY_ROWS)
        r = NA_TILE_ROWS * j + np.arange(NA_TILE_ROWS)
        kr = w0 + np.arange(NA_KEY_ROWS)
        r0 = np.clip(r - NA_WIN_R // 2, 0, rows - NA_WIN_R)
        row_ok = (kr[None, :] >= r0[:, None]) & (kr[None, :] < r0[:, None] + NA_WIN_R)
        dr = np.clip(kr[None, :] - r[:, None] + NA_WIN_R - 1, 0, 2 * NA_WIN_R - 2)
        oh_r = jnp.asarray(np.eye(2 * NA_WIN_R - 1, dtype=np.float32)[dr.reshape(-1)])
        oh_c = jnp.asarray(np.eye(2 * NA_WIN_C - 1, dtype=np.float32)[dc.reshape(-1)])
        bias = jnp.einsum('ar,hrc,bc->hab', oh_r, rpb, oh_c, precision=lax.Precision.HIGHEST)
        bias = bias.reshape(NA_HEADS, NA_TILE_ROWS, NA_KEY_ROWS, GRID_W, GRID_W)
        mask = row_ok[:, :, None, None] & in_win[None, None, :, :]
        bias = jnp.where(jnp.asarray(mask)[None], bias, -jnp.inf)
        tables.append(bias.transpose(0, 1, 3, 2, 4).reshape(NA_HEADS, TM, NA_KEYS))
    return jnp.stack(tables)


def _na_kernel(q_ref, k_ref, v_ref, bias_ref, o_ref, *, seq, n_lat_tiles):
    i = pl.program_id(1)
    nt = k_ref.shape[2]
    rows = seq // GRID_W

    @pl.when(i < n_lat_tiles)
    def _():
        w0 = jnp.clip(NA_TILE_ROWS * i - NA_WIN_R // 2, 0, rows - NA_KEY_ROWS)
        start = pl.multiple_of(w0 * GRID_W, GRID_W)
        for hd in range(NA_HEADS):
            q = q_ref[0, hd]
            s_loc = _dot_nt(q, k_ref[0, hd, pl.ds(start, NA_KEYS), :]) + bias_ref[0, hd]
            s_ctx = _dot_nt(q, k_ref[0, hd, seq:nt, :])
            o = _softmax_pv([(s_loc, v_ref[0, hd // 2, pl.ds(start, NA_KEYS), :]),
                             (s_ctx, v_ref[0, hd // 2, seq:nt, :])], (hd % 2) * LANE)
            o_ref[0, :, hd * LANE:(hd + 1) * LANE] = o.astype(BF16)

    @pl.when(i >= n_lat_tiles)
    def _():
        for hd in range(NA_HEADS):
            s = _dot_nt(q_ref[0, hd], k_ref[0, hd, seq:nt, :])
            o = _softmax_pv([(s, v_ref[0, hd // 2, seq:nt, :])], (hd % 2) * LANE)
            o_ref[0, :, hd * LANE:(hd + 1) * LANE] = o.astype(BF16)


def _na_attention(q, k, v, bias, seq, tiles_used):
    b, h, nt, _ = q.shape
    n_lat = seq // TM
    kv_spec = pl.BlockSpec((1, h, nt, LANE), lambda bb, i: (bb, 0, 0, 0))
    cfg = lambda bb, i: (jnp.where(i == 0, 0, jnp.where(i >= n_lat - 1, 2, 1)), 0, 0, 0)
    return pl.pallas_call(
        functools.partial(_na_kernel, seq=seq, n_lat_tiles=n_lat),
        grid=(b, tiles_used),
        in_specs=[pl.BlockSpec((1, h, TM, LANE), lambda bb, i: (bb, 0, i, 0)), kv_spec,
                  pl.BlockSpec((1, h // 2, nt, 2 * LANE), lambda bb, i: (bb, 0, 0, 0)),
                  pl.BlockSpec((1, h, TM, NA_KEYS), cfg)],
        out_specs=pl.BlockSpec((1, TM, h * LANE), lambda bb, i: (bb, i, 0)),
        out_shape=jax.ShapeDtypeStruct((b, tiles_used * TM, h * LANE), BF16),
        compiler_params=_cparams("parallel", "arbitrary"),
        name="neighbourhood_attention",
    )(q, k, v, bias)


def _hp_dot(a, b):
    return jnp.dot(a, b, preferred_element_type=F32, precision=lax.Precision.HIGHEST)


def _hy_filter_kernel(f_ref, w1_ref, b1_ref, w2_ref, b2_ref, w3_ref, dec_ref, o_ref, obf_ref):
    f = f_ref[...]
    h = jnp.sin(_hp_dot(f, w1_ref[...]) + b1_ref[...])
    h = jnp.sin(_hp_dot(h, w2_ref[...]) + b2_ref[...])
    h = _hp_dot(h, w3_ref[...])
    h = h * jnp.exp(-f[:, 0:1] * jnp.abs(dec_ref[...]))
    o_ref[...] = h
    obf_ref[...] = h.astype(BF16)


def _hy_pos_features(length):
    t = jnp.linspace(0.0, 1.0, length, dtype=F32)[:, None]
    w = 2.0 * math.pi * jnp.arange(length, dtype=F32)[:, None] / length
    f = jnp.linspace(1e-4, HY_BANDS - 1, HY_BANDS, dtype=F32)[None, :]
    z = w * f
    return jnp.concatenate([t, jnp.cos(z), -jnp.sin(z)], axis=-1)


DFT_ROW_STEP = 64
DFT_ROWS = 256
DFT_FWD_COLS = 1024
DFT_INV_COLS = 512
FILTER_ROWS = 512


def _col_block(nc, cap):
    return min(nc, cap)


def _hyp_pre_kernel(z_ref, w_ref, b_ref, lat_o, lat_bf_o, ctx_o, ctx_bf_o, *, seq, ctx_len):
    w = w_ref[...]
    bias = b_ref[...]
    for lo, length, o_ref, obf_ref in ((0, seq, lat_o, lat_bf_o), (seq, ctx_len, ctx_o, ctx_bf_o)):
        h = length // 2
        even = z_ref[0, pl.ds(lo, h, stride=2), :]
        odd = z_ref[0, pl.ds(lo + 1, h, stride=2), :]
        row = lax.broadcasted_iota(I32, even.shape, 0)
        odd_prev = jnp.where(row == 0, 0.0, pltpu.roll(odd, 1, axis=0))
        even_next = jnp.where(row == h - 1, 0.0, pltpu.roll(even, h - 1, axis=0))
        y_even = bias + w[0:1] * odd_prev + w[1:2] * even + w[2:3] * odd
        y_odd = bias + w[0:1] * even + w[1:2] * odd + w[2:3] * even_next
        for r, y in enumerate((y_even, y_odd)):
            o_ref[0, r] = y
            obf_ref[0, r] = y.astype(BF16)


def _hyp_pre(hy, w, bvec, seq):
    b, nt, _ = hy.shape
    ctx_len = nt - seq
    wpad = _pad_to(w, 8, 0)
    per_plane = HY_WIDTH // LANE
    out_specs, out_shape = [], []
    for length in (seq, ctx_len):
        for dt in (F32, BF16):
            out_specs.append(pl.BlockSpec((1, 2, length // 2, LANE),
                                          lambda bb, g: (g // per_plane, 0, 0, bb * per_plane + g % per_plane)))
            out_shape.append(jax.ShapeDtypeStruct((3, 2, length // 2, b * HY_WIDTH), dt))
    return pl.pallas_call(
        functools.partial(_hyp_pre_kernel, seq=seq, ctx_len=ctx_len),
        grid=(b, 3 * per_plane),
        in_specs=[pl.BlockSpec((1, nt, LANE), lambda bb, g: (bb, 0, g)),
                  pl.BlockSpec((8, LANE), lambda bb, g: (0, g)),
                  pl.BlockSpec((1, LANE), lambda bb, g: (0, g))],
        out_specs=out_specs,
        out_shape=out_shape,
        compiler_params=_cparams("parallel", "parallel"),
        name="hyena_short_conv",
    )(hy, wpad, bvec.reshape(1, -1))


def _hyp_filters(length, lp):
    feats = _hy_pos_features(length)
    feats = feats.reshape(length // 2, 2, HY_EMB).transpose(1, 0, 2).reshape(length, HY_EMB)
    n_out = HY_ORDER * 2 * HY_WIDTH
    tl = min(length // 2, FILTER_ROWS)
    full = lambda a: pl.BlockSpec(a.shape, lambda i: (0,) * a.ndim)
    args = (lp['hy_w1'], lp['hy_b1'].reshape(1, -1), lp['hy_w2'], lp['hy_b2'].reshape(1, -1), lp['hy_w3'],
            lp['hy_decay'].reshape(1, n_out))
    filt, filt_bf = pl.pallas_call(
        _hy_filter_kernel,
        grid=(length // tl,),
        in_specs=[pl.BlockSpec((tl, HY_EMB), lambda i: (i, 0))] + [full(a) for a in args],
        out_specs=[pl.BlockSpec((tl, n_out), lambda i: (i, 0))] * 2,
        out_shape=[jax.ShapeDtypeStruct((length, n_out), F32), jax.ShapeDtypeStruct((length, n_out), BF16)],
        compiler_params=_cparams("parallel"),
        name="hyena_filter_mlp",
    )(feats, *args)
    return filt.reshape(2, length // 2, n_out), filt_bf.reshape(2, length // 2, n_out)


def _hyp_matrix_kernel(c1_ref, s1_ref, c2_ref, s2_ref, ck_ref, sk_ref, ckr_ref, skr_ref,
                       ce_o, se_o, co_o, so_o, set_o, cot_o, sot_o):
    c1, s1 = c1_ref[0], s1_ref[0]
    c2, s2 = c2_ref[...], s2_ref[...]
    cos = c1 * c2 - s1 * s2
    sin = s1 * c2 + c1 * s2
    row = lax.broadcasted_iota(I32, cos.shape, 0)
    col = lax.broadcasted_iota(I32, cos.shape, 1)
    first_row = (row + pl.program_id(0) * DFT_ROW_STEP) == 0
    alt_col = jnp.where(col % 2 == 0, 1.0, -1.0)
    alt_row = jnp.where(row % 2 == 0, 1.0, -1.0)
    ck, sk = ck_ref[...], sk_ref[...]
    ce_o[...] = cos.astype(BF16)
    se_o[...] = jnp.where(first_row, alt_col, -sin).astype(BF16)
    co_o[...] = (cos * ck - sin * sk).astype(BF16)
    so_o[...] = jnp.where(first_row, alt_col, -(sin * ck + cos * sk)).astype(BF16)
    ckr, skr = ckr_ref[...], skr_ref[...]
    set_o[...] = jnp.where(col == 0, alt_row, -sin).astype(BF16)
    cot_o[...] = (cos * ckr - sin * skr).astype(BF16)
    sot_o[...] = jnp.where(col == 0, alt_row, -(sin * ckr + cos * skr)).astype(BF16)


def _hyp_matrices(length):
    h = length // 2
    step = DFT_ROW_STEP
    m = jnp.arange(h, dtype=I32)

    def trig(kv):
        ang = ((kv[:, None] * m[None, :]) % length).astype(F32) * (2.0 * math.pi / length)
        return jnp.cos(ang), jnp.sin(ang)

    c1, s1 = trig(jnp.arange(h // step, dtype=I32) * step)
    c2, s2 = trig(jnp.arange(step, dtype=I32))
    half_angle = m.astype(F32) * (math.pi / length)
    ck, sk = jnp.cos(half_angle), jnp.sin(half_angle)
    coarse = pl.BlockSpec((1, 1, h), lambda j: (j, 0, 0))
    fine = pl.BlockSpec((step, h), lambda j: (0, 0))
    per_row = pl.BlockSpec((step, 1), lambda j: (j, 0))
    per_col = pl.BlockSpec((1, h), lambda j: (0, 0))
    out = pl.BlockSpec((step, h), lambda j: (j, 0))
    ce, se, co, so, se_t, co_t, so_t = pl.pallas_call(
        _hyp_matrix_kernel,
        grid=(h // step,),
        in_specs=[coarse, coarse, fine, fine, per_row, per_row, per_col, per_col],
        out_specs=[out] * 7,
        out_shape=[jax.ShapeDtypeStruct((h, h), BF16)] * 7,
        compiler_params=_cparams("parallel"),
        name="dft_matrices",
    )(c1[:, None, :], s1[:, None, :], c2, s2, ck[:, None], sk[:, None], ck[None, :], sk[None, :])
    return dict(fwd=(ce, se, co, so), inv=(ce, se_t, co_t, so_t))


def _hyp_fwd_kernel(ce_ref, se_ref, co_ref, so_ref, xe_ref, xo_ref, *rest, with_taps):
    xe, xo = xe_ref[...], xo_ref[...]
    a_re, a_im = _dot(ce_ref[...], xe), _dot(se_ref[...], xe)
    b_re, b_im = _dot(co_ref[...], xo), _dot(so_ref[...], xo)
    if not with_taps:
        for ref, val in zip(rest, (a_re, a_im, b_re, b_im)):
            ref[...] = val
        return
    t1re_ref, t1im_ref, t2re_ref, t2im_ref, sp_ref, gere_o, geim_o, gore_o, goim_o = rest
    t1re, t1im, t2re, t2im = t1re_ref[...], t1im_ref[...], t2re_ref[...], t2im_ref[...]
    sp = sp_ref[...]
    first_block = pl.program_id(1) == 0
    row8 = lax.broadcasted_iota(I32, (8, HY_WIDTH), 0)
    for bb in range(xe.shape[1] // HY_WIDTH):
        sl = slice(bb * HY_WIDTH, (bb + 1) * HY_WIDTH)
        are, aim, bre, bim = a_re[:, sl], a_im[:, sl], b_re[:, sl], b_im[:, sl]
        u1re, u1im = are + bre, aim + bim
        u2re, u2im = are - bre, bim - aim
        z1re, z1im = u1re * t1re - u1im * t1im, u1re * t1im + u1im * t1re
        z2re, z2im = u2re * t2re - u2im * t2im, u2re * t2im + u2im * t2re
        gere_o[:, sl] = (z1re + z2re).astype(BF16)
        geim_o[:, sl] = (z1im - z2im).astype(BF16)
        gore_o[:, sl] = (z1re - z2re).astype(BF16)
        goim_o[:, sl] = (z1im + z2im).astype(BF16)

        @pl.when(first_block)
        def _():
            u0, ul = u1re[0:8], u2re[0:8]
            a_s, b_s = aim[0:8], bim[0:8]
            dc, ny, mre, mim = sp[0:1], sp[1:2], sp[2:3], sp[3:4]
            first = row8 == 0
            gere_o[0:8, sl] = jnp.where(first, u0 * dc + ul * ny, (z1re + z2re)[0:8]).astype(BF16)
            gore_o[0:8, sl] = jnp.where(first, u0 * dc - ul * ny, (z1re - z2re)[0:8]).astype(BF16)
            geim_o[0:8, sl] = jnp.where(first, a_s * mre + b_s * mim, (z1im - z2im)[0:8]).astype(BF16)
            goim_o[0:8, sl] = jnp.where(first, b_s * mre - a_s * mim, (z1im + z2im)[0:8]).astype(BF16)


def _hyp_fwd(mats, x, plane, taps=None):
    _, _, h, nc = x.shape
    tk = min(h, DFT_ROWS)
    cb = _col_block(nc, DFT_FWD_COLS)
    grid = (nc // cb, h // tk)
    m_spec = pl.BlockSpec((tk, h), lambda c, j: (j, 0))
    x_spec = lambda r: pl.BlockSpec((None, None, h, cb), lambda c, j: (plane, r, 0, c))
    o_spec = pl.BlockSpec((tk, cb), lambda c, j: (j, c))
    if taps is None:
        return pl.pallas_call(
            functools.partial(_hyp_fwd_kernel, with_taps=False),
            grid=grid, in_specs=[m_spec] * 4 + [x_spec(0), x_spec(1)], out_specs=[o_spec] * 4,
            out_shape=[jax.ShapeDtypeStruct((h, nc), F32)] * 4,
            compiler_params=_cparams("parallel", "arbitrary"),
            name="hyena_dft_filters",
        )(*mats['fwd'], x, x)
    t_spec = pl.BlockSpec((tk, HY_WIDTH), lambda c, j: (j, 0))
    sp_spec = pl.BlockSpec((8, HY_WIDTH), lambda c, j: (0, 0))
    return pl.pallas_call(
        functools.partial(_hyp_fwd_kernel, with_taps=True),
        grid=grid, in_specs=[m_spec] * 4 + [x_spec(0), x_spec(1)] + [t_spec] * 4 + [sp_spec],
        out_specs=[o_spec] * 4,
        out_shape=[jax.ShapeDtypeStruct((h, nc), BF16)] * 4,
        compiler_params=_cparams("parallel", "arbitrary"),
        name="hyena_dft_forward",
    )(*mats['fwd'], x, x, *taps)


def _hyp_inv_kernel(ce_ref, set_ref, cot_ref, sot_ref, gere_ref, geim_ref, gore_ref, goim_ref,
                    gate_ref, prev_ref, bias_ref, *outs, last):
    conv = (_dot(ce_ref[...], gere_ref[...]) + _dot(set_ref[...], geim_ref[...]),
            _dot(cot_ref[...], gore_ref[...]) + _dot(sot_ref[...], goim_ref[...]))
    bias = bias_ref[...]
    for r in range(2):
        y = gate_ref[r] * (conv[r] + prev_ref[r] * bias)
        if last:
            (tok_o,) = outs
            for bb in range(y.shape[1] // HY_WIDTH):
                tok_o[bb, :, r * HY_WIDTH:(r + 1) * HY_WIDTH] = y[:, bb * HY_WIDTH:(bb + 1) * HY_WIDTH].astype(BF16)
        else:
            y_o, ybf_o = outs
            y_o[r] = y
            ybf_o[r] = y.astype(BF16)


def _hyp_inv(mats, g, gate, gate_plane, prev, prev_plane, bias_row, last):
    h, nc = g[0].shape
    tm = min(h, DFT_ROWS)
    cb = _col_block(nc, DFT_INV_COLS)
    grid = (nc // cb, h // tm)
    m_spec = pl.BlockSpec((tm, h), lambda c, i: (i, 0))
    g_spec = pl.BlockSpec((h, cb), lambda c, i: (0, c))
    e_spec = lambda plane: pl.BlockSpec((None, 2, tm, cb), lambda c, i: (plane, 0, i, c))
    b_spec = pl.BlockSpec((1, cb), lambda c, i: (0, c))
    if last:
        out_specs = [pl.BlockSpec((cb // HY_WIDTH, tm, 2 * HY_WIDTH), lambda c, i: (c, i, 0))]
        out_shape = [jax.ShapeDtypeStruct((nc // HY_WIDTH, h, 2 * HY_WIDTH), BF16)]
    else:
        out_specs = [e_spec(0), e_spec(0)]
        out_shape = [jax.ShapeDtypeStruct((1, 2, h, nc), F32), jax.ShapeDtypeStruct((1, 2, h, nc), BF16)]
    return pl.pallas_call(
        functools.partial(_hyp_inv_kernel, last=last),
        grid=grid, in_specs=[m_spec] * 4 + [g_spec] * 4 + [e_spec(gate_plane), e_spec(prev_plane), b_spec],
        out_specs=out_specs, out_shape=out_shape,
        compiler_params=_cparams("parallel", "arbitrary"),
        name="hyena_dft_inverse",
    )(*mats['inv'], *g, gate, prev, bias_row)


def _hyp_tap_tables(spec, filt, length):
    a_re, a_im, b_re, b_im = spec
    w = HY_WIDTH
    inv_n = 1.0 / (2 * length)
    tables = []
    for o in range(HY_ORDER):
        f_sl = slice((2 * o) * w, (2 * o + 1) * w)
        r_sl = slice((2 * o + 1) * w, (2 * o + 2) * w)
        hb0 = filt[0, 0:1, r_sl]
        f1re = (a_re + b_re)[:, f_sl] + (a_re + b_re)[:, r_sl] - hb0
        f1im = (a_im + b_im)[:, f_sl] - (a_im + b_im)[:, r_sl]
        f2re = (a_re - b_re)[:, f_sl] + (a_re - b_re)[:, r_sl] - hb0
        f2im = (b_im - a_im)[:, f_sl] - (b_im - a_im)[:, r_sl]
        dc = f1re[0:1]
        ny = f2re[0:1]
        mid_re = a_im[0:1, f_sl] + a_im[0:1, r_sl] - hb0
        mid_im = -b_im[0:1, f_sl] + b_im[0:1, r_sl]
        sp = jnp.concatenate([dc * inv_n, ny * inv_n, mid_re * (2 * inv_n), mid_im * (2 * inv_n),
                              jnp.zeros((4, w), F32)], axis=0)
        tables.append((f1re * (2 * inv_n), f1im * (2 * inv_n), f2re * (2 * inv_n), f2im * (2 * inv_n), sp))
    return tables


def _hyena_seq(mats, vx, vx_bf, lp, n_batch):
    h = vx.shape[2]
    length = 2 * h
    filt, filt_bf = _hyp_filters(length, lp)
    spec = _hyp_fwd(mats, filt_bf[None], 0)
    tables = _hyp_tap_tables(spec, filt, length)
    bias = lp['hy_bias'].astype(F32)
    y, y_bf = vx, vx_bf
    for o in range(HY_ORDER):
        g = _hyp_fwd(mats, y_bf, 0, tables[o])
        bias_row = jnp.tile(bias[o][None, :], (1, n_batch))
        last = o == HY_ORDER - 1
        res = _hyp_inv(mats, g, vx, o + 1, y, 0, bias_row, last)
        if last:
            return res[0].reshape(n_batch, length, HY_WIDTH)
        y, y_bf = res


def _lru_kernel(u_ref, g_ref, cw_ref, cb_ref, wa_ref, ba_ref, wx_ref, bx_ref, lam_ref, o_ref,
                pad_ref, y_ref, *, seq, ctx_len):
    tc = LRU_CHUNK
    halo = LRU_HALO
    width = LRU_WIDTH
    lat_off = halo
    ctx_off = 2 * halo + seq
    zero = jnp.zeros((halo, width), F32)
    pad_ref[0:halo, :] = zero
    pad_ref[lat_off:lat_off + seq, :] = u_ref[0, 0:seq, :]
    pad_ref[lat_off + seq:ctx_off, :] = zero
    pad_ref[ctx_off:ctx_off + ctx_len, :] = u_ref[0, seq:seq + ctx_len, :]
    pad_ref[ctx_off + ctx_len:ctx_off + ctx_len + halo, :] = zero
    row = lax.broadcasted_iota(I32, (tc, width), 0)
    n_win = tc + 2 * halo

    def chunk(pad_off, y_off, s, carry, d):
        wstart = pl.multiple_of(pad_off + s - halo, 8)
        win = pad_ref[pl.ds(wstart, n_win), :]
        cw = cw_ref[d]
        xc = cb_ref[d]
        for k in range(LRU_CONV):
            shift = (LRU_CONV - 1 - k) if d == 0 else -k
            rolled = win if shift == 0 else pltpu.roll(win, shift % n_win, axis=0)
            xc = xc + cw[k:k + 1] * rolled[halo:halo + tc]
        xb = xc.astype(BF16)
        r = _sigmoid(_dot(xb, wa_ref[d]) + ba_ref[d])
        gi = _sigmoid(_dot(xb, wx_ref[d]) + bx_ref[d])
        lam = lam_ref[d]
        softplus = jnp.maximum(-lam, 0.0) + jnp.log1p(jnp.exp(-jnp.abs(lam)))
        log_a = -LRU_C * r * softplus
        a = jnp.exp(log_a)
        bt = jnp.sqrt(-jnp.tanh(log_a) * (a * a + 1.0)) * (gi * xc)
        sft = 1
        while sft < tc:
            if d == 0:
                keep = row >= sft
                a_s = jnp.where(keep, pltpu.roll(a, sft, axis=0), 1.0)
                b_s = jnp.where(keep, pltpu.roll(bt, sft, axis=0), 0.0)
            else:
                keep = row < tc - sft
                a_s = jnp.where(keep, pltpu.roll(a, tc - sft, axis=0), 1.0)
                b_s = jnp.where(keep, pltpu.roll(bt, tc - sft, axis=0), 0.0)
            bt = a * b_s + bt
            a = a * a_s
            sft *= 2
        h = a * carry + bt
        yo = pl.multiple_of(y_off + s, 8)
        if d == 0:
            y_ref[pl.ds(yo, tc), :] = h
            return h[tc - 1:tc]
        y_ref[pl.ds(yo, tc), :] = y_ref[pl.ds(yo, tc), :] + h
        return h[0:1]

    n_lat = seq // tc
    n_ctx = ctx_len // tc
    for d in range(2):
        carry = jnp.zeros((1, width), F32)
        order = range(n_ctx) if d == 0 else range(n_ctx - 1, -1, -1)
        for c in order:
            carry = chunk(ctx_off, seq, c * tc, carry, d)

        def body(j, cr, d=d):
            jj = j if d == 0 else n_lat - 1 - j
            return chunk(lat_off, 0, jj * tc, cr, d)

        lax.fori_loop(0, n_lat, body, carry)
    o_ref[0] = (y_ref[...] * _gelu_tanh(g_ref[0])).astype(BF16)


def _block_diag(w):
    nd, nb, c, _ = w.shape
    out = jnp.zeros((nd, nb * c, nb * c), w.dtype)
    for n in range(nb):
        out = out.at[:, n * c:(n + 1) * c, n * c:(n + 1) * c].set(w[:, n])
    return out


def _lru_mixer(lu, lg, lp, seq):
    b, nt, w = lu.shape
    ctx_len = nt - seq
    row3 = lambda a: a.reshape(2, 1, w)
    args = (_pad_to(lp['lru_conv_w'], 8, 1), row3(lp['lru_conv_b']), _block_diag(lp['lru_wa']).astype(BF16),
            row3(lp['lru_ba']), _block_diag(lp['lru_wx']).astype(BF16), row3(lp['lru_bx']), row3(lp['lru_lambda']))
    full = lambda a: pl.BlockSpec(a.shape, lambda bb: (0,) * a.ndim)
    tok = pl.BlockSpec((1, nt, w), lambda bb: (bb, 0, 0))
    return pl.pallas_call(
        functools.partial(_lru_kernel, seq=seq, ctx_len=ctx_len),
        grid=(b,),
        in_specs=[tok, tok] + [full(a) for a in args],
        out_specs=tok,
        out_shape=jax.ShapeDtypeStruct((b, nt, w), BF16),
        scratch_shapes=[pltpu.VMEM((nt + 3 * LRU_HALO, w), F32), pltpu.VMEM((nt, w), F32)],
        compiler_params=_cparams("parallel"),
        name="rglru_scan",
    )(lu, lg, *args)


def _merge_kernel(xl_ref, xc_ref, mod_ref, g1_ref, a_ref, b_ref, c_ref, d_ref, wg_ref, wa_ref, wb_ref, wc_ref,
                  wd_ref, wo_ref, o_ref, *, n_lat_tiles):
    group, tm, dm = xl_ref.shape
    m = mod_ref[:, 0]
    x = _stream_tile(xl_ref, xc_ref, n_lat_tiles)
    h = _normmod(x, g1_ref[...], m[:, 1:2], m[:, 0:1]).reshape(group * tm, dm).astype(BF16)
    acc = None
    for k, (br, w) in enumerate(((a_ref, wa_ref), (b_ref, wb_ref), (c_ref, wc_ref), (d_ref, wd_ref))):
        term = (1.0 + jnp.tanh(_dot(h, wg_ref[:, k * dm:(k + 1) * dm]))) * _dot(
            br[...].reshape(group * tm, br.shape[-1]), w[...])
        acc = term if acc is None else acc + term
    y = _dot(acc.astype(BF16), wo_ref[...])
    o_ref[...] = x + m[:, 2:3] * y.reshape(group, tm, dm)


def _merge(stream, modtab, g1, branches, w_gate, lp, n_lat_tiles, tiles_used):
    b, _, d = stream[0].shape
    wbr = lp['w_branch']
    head_rows = lambda w, dv: jnp.concatenate(
        [_pad_to(w[hd * dv:(hd + 1) * dv], LANE, 0) for hd in range(4)], axis=0)
    wbr = 0.5 * wbr
    weights = (head_rows(wbr[0], MLA_V).astype(BF16), head_rows(wbr[1], NA_HEAD_DIM).astype(BF16),
               wbr[2].astype(BF16), wbr[3].astype(BF16), lp['w_out'].astype(BF16))
    group = _batch_group(b)
    full = lambda a: pl.BlockSpec(a.shape, lambda bb, i: (0,) * a.ndim)
    tok = lambda n: pl.BlockSpec((group, TM, n), lambda bb, i: (bb, i, 0))
    return pl.pallas_call(
        functools.partial(_merge_kernel, n_lat_tiles=n_lat_tiles),
        grid=(b // group, tiles_used),
        in_specs=_stream_specs(stream, group, n_lat_tiles)
                 + [pl.BlockSpec((group, 1, 8, d), _kind_map(n_lat_tiles)), full(g1)]
                 + [tok(br.shape[-1]) for br in branches] + [full(w_gate)] + [full(w) for w in weights],
        out_specs=tok(d),
        out_shape=jax.ShapeDtypeStruct((b, tiles_used * TM, d), F32),
        compiler_params=_cparams("parallel", "parallel"),
        name="merge_branches",
    )(stream[0], stream[1], modtab, g1, *branches, w_gate, *weights)


U32 = jnp.uint32
EXPERT_ROWS = 1024


def _pack_pair(x):
    n = x.shape[-1] // 2
    hi = lax.bitcast_convert_type(x[:, :n].astype(BF16).astype(F32), U32)
    lo = lax.bitcast_convert_type(x[:, n:].astype(BF16).astype(F32), U32)
    return hi | (lo >> 16)


def _unpack_pair(p):
    hi = lax.bitcast_convert_type(p & jnp.uint32(0xFFFF0000), F32)
    lo = lax.bitcast_convert_type(p << 16, F32)
    return hi, lo


def _router_kernel(x_ref, mod_ref, g2_ref, rw_ref, rb_ref, tri_ref, h2_o, idx_o, wts_o, rank_o, cnt_o, carry):
    i = pl.program_id(0)

    @pl.when(i == 0)
    def _():
        carry[...] = jnp.zeros_like(carry)

    group, tm, d = x_ref.shape
    m = mod_ref[:, 0]
    h2 = _normmod(x_ref[...], g2_ref[...], m[:, 4:5], m[:, 3:4]).reshape(group * tm, d)
    half = h2.shape[-1] // 2
    h2_o[0] = _pack_pair(h2[:, :half])
    h2_o[1] = _pack_pair(h2[:, half:])
    logits = lax.dot_general(rw_ref[...], h2, (((1,), (1,)), ((), ())), preferred_element_type=F32,
                             precision=lax.Precision.HIGHEST)
    scores = _sigmoid(logits)
    biased = scores + rb_ref[...]
    expert = lax.broadcasted_iota(I32, scores.shape, 0)
    picks = []
    onehot_all = jnp.zeros(scores.shape, F32)
    for _ in range(TOP_K):
        best = jnp.max(biased, axis=0, keepdims=True)
        arg = jnp.min(jnp.where(biased == best, expert, N_EXPERTS), axis=0, keepdims=True)
        hit = expert == arg
        sel = jnp.sum(jnp.where(hit, scores, 0.0), axis=0, keepdims=True)
        biased = jnp.where(hit, -jnp.inf, biased)
        onehot_all = onehot_all + jnp.where(hit, 1.0, 0.0)
        picks.append((arg, hit, sel))
    total = picks[0][2]
    for _, _, sel in picks[1:]:
        total = total + sel
    earlier = _dot(onehot_all.astype(BF16), tri_ref[...]) + carry[...]
    pad_rows = TOPK_PAD - TOP_K
    ranks = [jnp.sum(jnp.where(hit, earlier, 0.0), axis=0, keepdims=True).astype(I32) for _, hit, _ in picks]
    scale = ROUTED_SCALE / total
    n_tok = scores.shape[1]
    idx_o[...] = jnp.concatenate([arg for arg, _, _ in picks] + [jnp.zeros((pad_rows, n_tok), I32)], axis=0)
    wts_o[...] = jnp.concatenate([sel * scale for _, _, sel in picks] + [jnp.zeros((pad_rows, n_tok), F32)], axis=0)
    rank_o[...] = jnp.concatenate(ranks + [jnp.zeros((pad_rows, n_tok), I32)], axis=0)
    carry[...] = carry[...] + jnp.sum(onehot_all, axis=1, keepdims=True)
    cnt_o[...] = carry[...]


def _tile_maps(n_groups, n_lat_tiles):
    tok = lambda f: (f % n_groups, f // n_groups, 0)
    mod = lambda f: (f % n_groups, jnp.where(f // n_groups >= n_lat_tiles, 1, 0), 0, 0)
    return tok, mod


def _route(x1, modtab, g2, lp, n_lat_tiles, tiles_used):
    b, _, d = x1.shape
    group = _batch_group(b, MOE_BATCH_GROUP)
    rows = group * TM
    t = b * tiles_used * TM
    rw = lp['router_w'].T
    rb = lp['router_bias'].reshape(-1, 1)
    tri = (np.arange(rows)[:, None] < np.arange(rows)[None, :]).astype(np.float32)
    tri = jnp.asarray(tri, BF16)
    per_tok = lambda: pl.BlockSpec((TOPK_PAD, rows), lambda i: (0, i))
    full = lambda a: pl.BlockSpec(a.shape, lambda i: (0,) * a.ndim)
    tok, mod = _tile_maps(b // group, n_lat_tiles)
    return pl.pallas_call(
        _router_kernel,
        grid=(t // rows,),
        in_specs=[pl.BlockSpec((group, TM, d), tok), pl.BlockSpec((group, 1, 8, d), mod),
                  full(g2), full(rw), full(rb), full(tri)],
        out_specs=[pl.BlockSpec((2, rows, d // 4), lambda i: (0, i, 0)), per_tok(), per_tok(), per_tok(),
                   pl.BlockSpec((N_EXPERTS, 1), lambda i: (0, 0))],
        out_shape=[jax.ShapeDtypeStruct((2, t, d // 4), U32), jax.ShapeDtypeStruct((TOPK_PAD, t), I32),
                   jax.ShapeDtypeStruct((TOPK_PAD, t), F32), jax.ShapeDtypeStruct((TOPK_PAD, t), I32),
                   jax.ShapeDtypeStruct((N_EXPERTS, 1), F32)],
        scratch_shapes=[pltpu.VMEM((N_EXPERTS, 1), F32)],
        compiler_params=_cparams("arbitrary"),
        name="moe_router",
    )(x1, modtab, g2, rw, rb, tri)


SC_WINDOW = 128


def _sc_mesh():
    return plsc.VectorSubcoreMesh(core_axis_name="c", subcore_axis_name="s")


def _sc_scatter_rows(src, idx, n_out):
    n, width = src.shape
    k_rep = idx.shape[0]
    half = n // SC_WINDOW // 2

    @functools.partial(pl.kernel, out_type=jax.ShapeDtypeStruct((n_out, width), src.dtype), mesh=_sc_mesh(),
                       scratch_types=[], name="moe_dispatch_sc")
    def scatter(src_hbm, idx_hbm, out_hbm):
        def body(x_vmem, *i_vmems):
            for i_vmem in i_vmems:
                pltpu.sync_copy(x_vmem, out_hbm.at[i_vmem.at[0]])

        pltpu.emit_pipeline(
            body,
            grid=(2, half),
            in_specs=[pl.BlockSpec((SC_WINDOW, width), lambda a, i: (a * half + i, 0))]
                     + [pl.BlockSpec((1, SC_WINDOW), lambda a, i, k=k: (k, a * half + i)) for k in range(k_rep)],
            out_specs=[],
            core_axis_name=("c", "s"),
            dimension_semantics=(pltpu.PARALLEL, pltpu.PARALLEL),
        )(src_hbm, *([idx_hbm] * k_rep))

    return scatter(src, idx)


def _sc_gather_rows(src, idx):
    k_rep, n = idx.shape
    width = src.shape[1]
    n_win = n // SC_WINDOW

    @functools.partial(pl.kernel, out_type=jax.ShapeDtypeStruct((k_rep * n, width), src.dtype), mesh=_sc_mesh(),
                       scratch_types=[], name="moe_gather_sc")
    def gather(src_hbm, idx_hbm, out_hbm):
        def body(i_vmem, o_vmem):
            pltpu.sync_copy(src_hbm.at[i_vmem.at[0]], o_vmem)

        pltpu.emit_pipeline(
            body,
            grid=(k_rep, n_win),
            in_specs=[pl.BlockSpec((1, SC_WINDOW), lambda k, i: (k, i))],
            out_specs=[pl.BlockSpec((SC_WINDOW, width), lambda k, i: (k * n_win + i, 0))],
            core_axis_name=("c", "s"),
            dimension_semantics=(pltpu.PARALLEL, pltpu.PARALLEL),
        )(idx_hbm, out_hbm)

    return gather(src, idx)


def _unpack_planes(p0, p1):
    return _unpack_pair(p0) + _unpack_pair(p1)


def _dot_quarters(parts, w_ref):
    q = parts[0].shape[-1]
    acc = None
    for j, part in enumerate(parts):
        term = _dot(part.astype(BF16), w_ref[j * q:(j + 1) * q, :])
        acc = term if acc is None else acc + term
    return acc


def _expert_kernel(be_ref, nv_ref, xs_ref, wg_ref, wu_ref, wd_ref, ys_o, wg_s, wu_s, wd_s):
    i = pl.program_id(0)
    prev = be_ref[jnp.maximum(i - 1, 0)]

    @pl.when((i == 0) | (be_ref[i] != prev))
    def _():
        wg_s[...] = (0.5 * wg_ref[0]).astype(BF16)
        wu_s[...] = wu_ref[0].astype(BF16)
        wd_s[...] = wd_ref[0].astype(BF16)

    @pl.when(nv_ref[i] > 0)
    def _():
        keep = lax.broadcasted_iota(I32, xs_ref.shape[1:], 0) < nv_ref[i]
        parts = _unpack_planes(jnp.where(keep, xs_ref[0], jnp.uint32(0)), jnp.where(keep, xs_ref[1], jnp.uint32(0)))
        hid = _half_silu(_dot_quarters(parts, wg_s)) * _dot_quarters(parts, wu_s)
        y = _dot(hid.astype(BF16), wd_s[...])
        half = y.shape[-1] // 2
        ys_o[0] = _pack_pair(y[:, :half])
        ys_o[1] = _pack_pair(y[:, half:])

    @pl.when(nv_ref[i] <= 0)
    def _():
        ys_o[...] = jnp.zeros_like(ys_o)


def _experts(xs, block_e, n_valid, weights, layer):
    _, n_rows, dq = xs.shape
    d = 4 * dq
    n_blocks = n_rows // EXPERT_ROWS
    hid = EXPERT_HIDDEN
    grid_spec = pltpu.PrefetchScalarGridSpec(
        num_scalar_prefetch=2,
        grid=(n_blocks,),
        in_specs=[pl.BlockSpec((2, EXPERT_ROWS, dq), lambda i, be, nv: (0, i, 0)),
                  pl.BlockSpec((None, 1, d, hid), lambda i, be, nv: (layer, be[i], 0, 0)),
                  pl.BlockSpec((None, 1, d, hid), lambda i, be, nv: (layer, be[i], 0, 0)),
                  pl.BlockSpec((None, 1, hid, d), lambda i, be, nv: (layer, be[i], 0, 0))],
        out_specs=pl.BlockSpec((2, EXPERT_ROWS, dq), lambda i, be, nv: (0, i, 0)),
        scratch_shapes=[pltpu.VMEM((d, hid), BF16), pltpu.VMEM((d, hid), BF16), pltpu.VMEM((hid, d), BF16)],
    )
    return pl.pallas_call(
        _expert_kernel,
        grid_spec=grid_spec,
        out_shape=jax.ShapeDtypeStruct((2, n_rows, dq), U32),
        compiler_params=_cparams("arbitrary"),
        name="moe_experts",
    )(block_e, n_valid, xs, *weights)


def _combine_kernel(g_ref, wts_ref, h2_ref, x_ref, mod_ref, sg_ref, su_ref, sd_ref, gf_ref, o_ref, *, final):
    parts = _unpack_planes(h2_ref[0], h2_ref[1])
    hid = _half_silu(_dot_quarters(parts, sg_ref)) * _dot_quarters(parts, su_ref)
    shared = _dot(hid.astype(BF16), sd_ref[...])
    wts = wts_ref[...]
    routed = None
    for k in range(TOP_K):
        w = wts[:, k:k + 1]
        terms = [w * part for part in _unpack_planes(g_ref[k, 0], g_ref[k, 1])]
        routed = terms if routed is None else [r + t for r, t in zip(routed, terms)]
    f = shared + jnp.concatenate(routed, axis=1)
    m = mod_ref[:, 0]
    x2 = x_ref[...] + m[:, 5:6] * f.reshape(x_ref.shape)
    if final:
        x2 = x2 * lax.rsqrt(jnp.mean(x2 * x2, axis=-1, keepdims=True) + NORM_EPS) * gf_ref[...]
    o_ref[...] = x2


def _combine(g, wts, h2p, x1, modtab, lp, g_final, n_lat_tiles, tiles_used, final):
    b, _, d = x1.shape
    dq = d // 4
    weights = ((0.5 * lp['sh_w_gate']).astype(BF16), lp['sh_w_up'].astype(BF16), lp['sh_w_down'].astype(BF16),
               g_final.reshape(1, -1))
    full = lambda a: pl.BlockSpec(a.shape, lambda i: (0,) * a.ndim)
    group = _batch_group(b, MOE_BATCH_GROUP)
    rows = group * TM
    tok, mod = _tile_maps(b // group, n_lat_tiles)
    return pl.pallas_call(
        functools.partial(_combine_kernel, final=final),
        grid=(b * tiles_used * TM // rows,),
        in_specs=[pl.BlockSpec((TOP_K, 2, rows, dq), lambda i: (0, 0, i, 0)),
                  pl.BlockSpec((rows, TOPK_PAD), lambda i: (i, 0)),
                  pl.BlockSpec((2, rows, dq), lambda i: (0, i, 0)),
                  pl.BlockSpec((group, TM, d), tok), pl.BlockSpec((group, 1, 8, d), mod)]
                 + [full(w) for w in weights],
        out_specs=pl.BlockSpec((group, TM, d), tok),
        out_shape=jax.ShapeDtypeStruct((b, tiles_used * TM, d), F32),
        compiler_params=_cparams("parallel"),
        name="moe_combine",
    )(g, wts, h2p, x1, modtab, *weights)


def _sorted_rows_kernel(idx_ref, rank_ref, ps_ref, o_ref):
    expert = lax.broadcasted_iota(I32, (N_EXPERTS, idx_ref.shape[1]), 0)
    starts = ps_ref[...]
    idx = idx_ref[...]
    rows = [jnp.sum(jnp.where(idx[k:k + 1] == expert, starts, 0), axis=0, keepdims=True) for k in range(TOP_K)]
    pad = jnp.zeros((TOPK_PAD - TOP_K, idx.shape[1]), I32)
    o_ref[...] = jnp.concatenate(rows + [pad], axis=0) + rank_ref[...]


def _sorted_rows(idx, rank, p_starts):
    t = idx.shape[1]
    cols = TM * max(k for k in (8, 4, 2, 1) if (t // TM) % k == 0)
    per_tok = pl.BlockSpec((TOPK_PAD, cols), lambda i: (0, i))
    return pl.pallas_call(
        _sorted_rows_kernel,
        grid=(t // cols,),
        in_specs=[per_tok, per_tok, pl.BlockSpec((N_EXPERTS, 1), lambda i: (0, 0))],
        out_specs=per_tok,
        out_shape=jax.ShapeDtypeStruct((TOPK_PAD, t), I32),
        compiler_params=_cparams("parallel"),
        name="moe_sorted_rows",
    )(idx, rank, p_starts.reshape(-1, 1))


def _moe(x1, modtab, g2, lp, g_final, n_lat_tiles, final):
    b, nt, d = x1.shape
    tiles_used = n_lat_tiles if final else nt // TM
    t = b * tiles_used * TM
    h2p, idx, wts, rank, cnt = _route(x1, modtab, g2, lp, n_lat_tiles, tiles_used)
    counts = cnt[:, 0].astype(I32)
    padded = (counts + EXPERT_ROWS - 1) // EXPERT_ROWS * EXPERT_ROWS
    p_ends = jnp.cumsum(padded)
    p_starts = p_ends - padded
    n_blocks = (t * TOP_K + N_EXPERTS * (EXPERT_ROWS - 1)) // EXPERT_ROWS
    n_rows = n_blocks * EXPERT_ROWS
    dest = _sorted_rows(idx, rank, p_starts)[:TOP_K]
    plane_idx = jnp.concatenate([dest, dest + n_rows], axis=1)
    blk_start = jnp.arange(n_blocks, dtype=I32) * EXPERT_ROWS
    block_e = jnp.minimum(jnp.sum((p_ends[None, :] <= blk_start[:, None]).astype(I32), axis=1), N_EXPERTS - 1)
    n_valid = jnp.clip((p_starts + counts)[block_e] - blk_start, 0, EXPERT_ROWS).astype(I32)
    dq = d // 4
    xs = _sc_scatter_rows(h2p.reshape(2 * t, dq), plane_idx, 2 * n_rows).reshape(2, n_rows, dq)
    ys = _experts(xs, block_e, n_valid, lp['expert_stacks'], lp['layer'])
    g = _sc_gather_rows(ys.reshape(2 * n_rows, dq), plane_idx).reshape(TOP_K, 2, t, dq)
    return _combine(g, wts.T, h2p, x1, modtab, lp, g_final, n_lat_tiles, tiles_used, final)


def _layer(stream, nt, c, c_ctx, lp, consts, g_final, seq, final):
    b = stream[0].shape[0]
    n_lat_tiles = seq // TM
    rope, mats_lat, mats_ctx = consts
    modtab = _mod_table(c, c_ctx, *lp['mod_stacks'], lp['layer'])
    g1 = lp['g_norm1'].reshape(1, -1)
    g2 = lp['g_norm2'].reshape(1, -1)
    pw = _proj_weights(lp)
    q, k, v, nq, nk, nv, hy, lu, lg = _project(stream, nt, modtab, g1, pw, rope, n_lat_tiles)
    tiles_used = n_lat_tiles if final else nt // TM
    br_a = _mla_attention(q, k, v, seq, tiles_used)
    br_b = _na_attention(nq, nk, nv, _na_bias_tables(lp['na_rpb'], seq // GRID_W), seq, tiles_used)
    pre = _hyp_pre(hy, lp['hy_short_w'], lp['hy_short_b'], seq)
    lat_f, lat_bf, ctx_f, ctx_bf = pre
    br_c = _hyena_seq(mats_lat, lat_f, lat_bf, lp, b)
    if not final:
        br_c = jnp.concatenate([br_c, _hyena_seq(mats_ctx, ctx_f, ctx_bf, lp, b)], axis=1)
    br_d = _lru_mixer(lu, lg, lp, seq)
    x1 = _merge(stream, modtab, g1, (br_a, br_b, br_c, br_d), pw['w_gate'], lp, n_lat_tiles, tiles_used)
    return _moe(x1, modtab, g2, lp, g_final, n_lat_tiles, final)


_LAYER_KEYS = ('w_mod', 'b_mod', 'g_norm1', 'g_norm2', 'w_in', 'mla_g_q', 'mla_w_uq', 'mla_g_kv', 'mla_w_ukv',
               'na_rpb', 'hy_short_w', 'hy_short_b', 'hy_w1', 'hy_b1', 'hy_w2', 'hy_b2', 'hy_w3', 'hy_decay',
               'hy_bias', 'lru_conv_w', 'lru_conv_b', 'lru_wa', 'lru_ba', 'lru_wx', 'lru_bx', 'lru_lambda',
               'w_branch', 'w_out', 'router_w', 'router_bias', 'exp_w_gate', 'exp_w_up', 'exp_w_down',
               'sh_w_gate', 'sh_w_up', 'sh_w_down')


def kernel(x, c, ctx, c_ctx, w_mod, b_mod, g_norm1, g_norm2, w_in, mla_g_q, mla_w_uq, mla_g_kv, mla_w_ukv, na_rpb, hy_short_w, hy_short_b, hy_w1, hy_b1, hy_w2, hy_b2, hy_w3, hy_decay, hy_bias, lru_conv_w, lru_conv_b, lru_wa, lru_ba, lru_wx, lru_bx, lru_lambda, w_branch, w_out, router_w, router_bias, exp_w_gate, exp_w_up, exp_w_down, sh_w_gate, sh_w_up, sh_w_down, g_final):
    stacked = dict(zip(_LAYER_KEYS, (w_mod, b_mod, g_norm1, g_norm2, w_in, mla_g_q, mla_w_uq, mla_g_kv, mla_w_ukv,
                                     na_rpb, hy_short_w, hy_short_b, hy_w1, hy_b1, hy_w2, hy_b2, hy_w3, hy_decay,
                                     hy_bias, lru_conv_w, lru_conv_b, lru_wa, lru_ba, lru_wx, lru_bx, lru_lambda,
                                     w_branch, w_out, router_w, router_bias, exp_w_gate, exp_w_up, exp_w_down,
                                     sh_w_gate, sh_w_up, sh_w_down)))
    b, seq, d = x.shape
    ctx_len = ctx.shape[1]
    depth = w_mod.shape[0]
    assert seq % TM == 0 and ctx_len % TM == 0 and seq // GRID_W >= NA_KEY_ROWS + 1
    nt = seq + ctx_len
    stream = (x, ctx, 0)
    consts = (_rope_tables(seq, seq + ctx_len), _hyp_matrices(seq), _hyp_matrices(ctx_len))
    for i in range(depth):
        big = ('w_mod', 'b_mod', 'exp_w_gate', 'exp_w_up', 'exp_w_down')
        lp = {name: w[i] for name, w in stacked.items() if name not in big}
        lp['layer'] = i
        lp['mod_stacks'] = (w_mod, b_mod)
        lp['expert_stacks'] = (exp_w_gate, exp_w_up, exp_w_down)
        xa = _layer(stream, nt, c, c_ctx, lp, consts, g_final, seq, i == depth - 1)
        stream = (xa, xa, seq // TM)
    return xa
```

```python
import functools
import math

import numpy as np
import jax
import jax.numpy as jnp
from jax import lax
from jax.experimental import pallas as pl
from jax.experimental.pallas import tpu as pltpu
from jax.experimental.pallas import tpu_sc as plsc

F32 = jnp.float32
BF16 = jnp.bfloat16
I32 = jnp.int32

TM = 256
LANE = 128
GRID_W = 64
N_MOD = 6
NORM_EPS = 1e-6

MLA_HEADS, MLA_NOPE, MLA_ROPE, MLA_V = 4, 64, 32, 64
MLA_Q_RANK, MLA_KV_RANK = 192, 128
MLA_Q_PAD = 256
ROPE_THETA = 10000.0

NA_HEADS, NA_HEAD_DIM, NA_WIN_R, NA_WIN_C = 4, 64, 8, 16
NA_TILE_ROWS = TM // GRID_W
NA_KEY_ROWS = NA_TILE_ROWS + NA_WIN_R - 1
NA_KEYS = NA_KEY_ROWS * GRID_W

HY_WIDTH, HY_ORDER, HY_SHORT, HY_BANDS, HY_FFN = 256, 2, 3, 16, 64
HY_EMB = 2 * HY_BANDS + 1

LRU_WIDTH, LRU_BLOCKS, LRU_CONV, LRU_C = 256, 4, 4, 8.0
LRU_CHUNK = 256
LRU_HALO = 8

N_EXPERTS, TOP_K, EXPERT_HIDDEN, ROUTED_SCALE, MOE_BLOCK = 64, 6, 256, 2.5, 256
TOPK_PAD = 8

VMEM_LIMIT = 56 * 1024 * 1024


def _cparams(*sem):
    return pltpu.CompilerParams(dimension_semantics=sem, vmem_limit_bytes=VMEM_LIMIT)


def _dot(a, b):
    return jnp.dot(a, b, preferred_element_type=F32)


def _dot_nt(a, b):
    return lax.dot_general(a, b, (((1,), (1,)), ((), ())), preferred_element_type=F32)


def _sigmoid(x):
    return jax.nn.sigmoid(x)


def _half_silu(g):
    return g * (1.0 + jnp.tanh(g))


def _silu(x):
    return x * _sigmoid(x)


def _gelu_tanh(x):
    return 0.5 * x * (1.0 + jnp.tanh(math.sqrt(2.0 / math.pi) * (x + 0.044715 * (x * x * x))))


def _normmod(x, g, scale, shift):
    y = x * lax.rsqrt(jnp.mean(x * x, axis=-1, keepdims=True) + NORM_EPS) * g
    return y * (1.0 + scale) + shift


MOD_COLS = 1024


def _mod_kernel(c_ref, w_ref, b_ref, o_ref):
    s = _silu(c_ref[...])
    o_ref[...] = _dot(s.astype(BF16), w_ref[...].astype(BF16)) + b_ref[...]


def _mod_table(c, c_ctx, w_mod, b_mod, layer):
    b, d = c.shape
    rows = -(-(b + 1) // 16) * 16
    cc = jnp.zeros((rows, d), F32).at[:b].set(c).at[b].set(c_ctx)
    tn = MOD_COLS
    mod = pl.pallas_call(
        _mod_kernel,
        grid=(N_MOD * d // tn,),
        in_specs=[pl.BlockSpec((rows, d), lambda j: (0, 0)),
                  pl.BlockSpec((None, d, tn), lambda j: (layer, 0, j)),
                  pl.BlockSpec((None, 1, tn), lambda j: (layer, 0, j))],
        out_specs=pl.BlockSpec((rows, tn), lambda j: (0, j)),
        out_shape=jax.ShapeDtypeStruct((rows, N_MOD * d), F32),
        compiler_params=_cparams("arbitrary"),
        name="mod_vectors",
    )(cc, w_mod, b_mod[:, None, :])
    lat = mod[:b].reshape(b, N_MOD, d)
    ctx = jnp.broadcast_to(mod[b].reshape(1, N_MOD, d), (b, N_MOD, d))
    tab = jnp.stack([lat, ctx], axis=1)
    return jnp.pad(tab, ((0, 0), (0, 0), (0, 8 - N_MOD), (0, 0)))


_C_QLAT, _C_KVLAT, _C_KR, _C_KRR, _C_NA, _C_HY, _C_LU, _C_LG, _C_END = (
    0, 256, 384, 512, 640, 640 + 3 * NA_HEADS * LANE, 640 + 1536 + 768, 640 + 1536 + 1024, 640 + 1536 + 1280)


def _stream_tile(lat_ref, ctx_ref, n_lat_tiles):
    return jnp.where(pl.program_id(1) < n_lat_tiles, lat_ref[...], ctx_ref[...])


def _stream_specs(stream, group, n_lat_tiles):
    lat_src, ctx_src, ctx_tile0 = stream
    d = lat_src.shape[-1]
    return [pl.BlockSpec((group, TM, d), lambda bb, i: (bb, jnp.minimum(i, n_lat_tiles - 1), 0)),
            pl.BlockSpec((group, TM, d), lambda bb, i: (bb, ctx_tile0 + jnp.maximum(i - n_lat_tiles, 0), 0))]


def _proj_kernel(xl_ref, xc_ref, mod_ref, g1_ref, w1_ref, gq_ref, gkv_ref, wuq_ref, wuqr_ref, wk_ref, wv_ref,
                 cos_ref, sin_ref, q_o, k_o, v_o, nq_o, nk_o, nv_o, hy_o, lu_o, lg_o, *, n_lat_tiles):
    group, tm, d = xl_ref.shape
    m = mod_ref[:, 0]
    x = _stream_tile(xl_ref, xc_ref, n_lat_tiles)
    h = _normmod(x, g1_ref[...], m[:, 1:2], m[:, 0:1]).reshape(group * tm, d)
    z = _dot(h.astype(BF16), w1_ref[...])
    qlat = z[:, _C_QLAT:_C_KVLAT]
    kvlat = z[:, _C_KVLAT:_C_KR]
    kr = z[:, _C_KR:_C_KRR]
    krr = z[:, _C_KRR:_C_NA]
    qn = qlat * lax.rsqrt(jnp.sum(qlat * qlat, axis=-1, keepdims=True) * (1.0 / MLA_Q_RANK) + NORM_EPS) * gq_ref[...]
    kvn = kvlat * lax.rsqrt(jnp.mean(kvlat * kvlat, axis=-1, keepdims=True) + NORM_EPS) * gkv_ref[...]
    qn = qn.astype(BF16)
    kvn = kvn.astype(BF16)
    q = _dot(qn, wuq_ref[...])
    qr = _dot(qn, wuqr_ref[...])
    kk = _dot(kvn, wk_ref[...])
    vv = _dot(kvn, wv_ref[...])
    cos = jnp.concatenate([cos_ref[...]] * group, axis=0)
    sin = jnp.concatenate([sin_ref[...]] * group, axis=0)
    krope = kr * cos + krr * sin
    den_lane = lax.broadcasted_iota(I32, cos.shape, 1) == SOFTMAX_DEN_LANE
    def put_heads(ref, hd, val):
        for g in range(group):
            ref[g, hd] = val[g * tm:(g + 1) * tm].astype(BF16)

    def put_pairs(ref, hd, val):
        off = (hd % 2) * LANE
        for g in range(group):
            ref[g, hd // 2, :, off:off + LANE] = val[g * tm:(g + 1) * tm].astype(BF16)

    for hd in range(MLA_HEADS):
        sl = slice(hd * LANE, (hd + 1) * LANE)
        put_heads(q_o, hd, q[:, sl] * cos + qr[:, sl] * sin)
        put_heads(k_o, hd, kk[:, sl] + krope)
        put_heads(v_o, hd, jnp.where(den_lane, 1.0, vv[:, sl]))
    for hd in range(NA_HEADS):
        for which, ref in enumerate((nq_o, nk_o, nv_o)):
            lo = _C_NA + (which * NA_HEADS + hd) * LANE
            blk = z[:, lo:lo + LANE]
            if which == 2:
                put_pairs(ref, hd, jnp.where(den_lane, 1.0, blk))
            else:
                put_heads(ref, hd, blk)
    hy_o[...] = z[:, _C_HY:_C_LU].reshape(group, tm, _C_LU - _C_HY)
    lu_o[...] = z[:, _C_LU:_C_LG].reshape(group, tm, _C_LG - _C_LU)
    lg_o[...] = z[:, _C_LG:_C_END].reshape(group, tm, _C_END - _C_LG)


def _pad_to(a, n, axis):
    pad = [(0, 0)] * a.ndim
    pad[axis] = (0, n - a.shape[axis])
    return jnp.pad(a, pad)


def _rot_cols(w):
    half = w.shape[-1] // 2
    return jnp.concatenate([-w[..., half:], w[..., :half]], axis=-1)


def _head_blocks(cols_per_head):
    out = []
    for pieces in cols_per_head:
        k = pieces[0][0].shape[0]
        blk = jnp.zeros((k, LANE), F32)
        for arr, off in pieces:
            blk = blk.at[:, off:off + arr.shape[1]].set(arr)
        out.append(blk)
    return jnp.concatenate(out, axis=1)


def _proj_weights(lp):
    w_in = lp['w_in']
    d = w_in.shape[0]
    o = 0
    parts = {}
    for name, n in (('q', MLA_Q_RANK), ('kv', MLA_KV_RANK), ('kr', MLA_ROPE), ('na', 3 * NA_HEADS * NA_HEAD_DIM),
                    ('hy', 3 * HY_WIDTH), ('lu', LRU_WIDTH), ('lg', LRU_WIDTH), ('gt', 4 * d)):
        parts[name] = w_in[:, o:o + n]
        o += n
    zeros = lambda n: jnp.zeros((d, n), F32)
    kr_blk = jnp.concatenate([zeros(MLA_NOPE), parts['kr'], zeros(LANE - MLA_NOPE - MLA_ROPE)], axis=1)
    krr_blk = jnp.concatenate([zeros(MLA_NOPE), _rot_cols(parts['kr']), zeros(LANE - MLA_NOPE - MLA_ROPE)], axis=1)
    na_scale = NA_HEAD_DIM ** -0.5
    na_cols = []
    for which in range(3):
        for hd in range(NA_HEADS):
            lo = (which * NA_HEADS + hd) * NA_HEAD_DIM
            blk = parts['na'][:, lo:lo + NA_HEAD_DIM] * (na_scale if which == 0 else 1.0)
            na_cols.append(_pad_to(blk, LANE, 1))
    w1 = jnp.concatenate([_pad_to(parts['q'], MLA_Q_PAD, 1), parts['kv'], kr_blk, krr_blk] + na_cols
                         + [parts['hy'], parts['lu'], parts['lg']], axis=1).astype(BF16)
    mla_scale = (MLA_NOPE + MLA_ROPE) ** -0.5
    wuq = _pad_to(lp['mla_w_uq'], MLA_Q_PAD, 0) * mla_scale
    dq = MLA_NOPE + MLA_ROPE
    wuq_main = _head_blocks([[(wuq[:, hd * dq:hd * dq + dq], 0)] for hd in range(MLA_HEADS)])
    wuq_rot = _head_blocks([[(_rot_cols(wuq[:, hd * dq + MLA_NOPE:hd * dq + dq]), MLA_NOPE)]
                            for hd in range(MLA_HEADS)])
    dkv = MLA_NOPE + MLA_V
    wukv = lp['mla_w_ukv']
    wk = _head_blocks([[(wukv[:, hd * dkv:hd * dkv + MLA_NOPE], 0)] for hd in range(MLA_HEADS)])
    wv = _head_blocks([[(wukv[:, hd * dkv + MLA_NOPE:hd * dkv + dkv], 0)] for hd in range(MLA_HEADS)])
    gq = _pad_to(lp['mla_g_q'].reshape(1, -1), MLA_Q_PAD, 1)
    gkv = lp['mla_g_kv'].reshape(1, -1)
    return dict(w1=w1, w_gate=(0.5 * parts['gt']).astype(BF16), gq=gq, gkv=gkv, wuq=wuq_main.astype(BF16),
                wuq_rot=wuq_rot.astype(BF16), wk=wk.astype(BF16), wv=wv.astype(BF16))


def _rope_tables(seq, n_tok):
    t = jnp.arange(seq, dtype=I32)
    row = (t // GRID_W).astype(F32)
    col = (t % GRID_W).astype(F32)
    n_axis = MLA_ROPE // 4
    inv_freq = ROPE_THETA ** (-jnp.arange(n_axis, dtype=F32) / n_axis)
    ang = jnp.concatenate([row[:, None] * inv_freq, col[:, None] * inv_freq], axis=-1)
    cos = jnp.concatenate([jnp.cos(ang), jnp.cos(ang)], axis=-1)
    sin = jnp.concatenate([jnp.sin(ang), jnp.sin(ang)], axis=-1)
    cos_t = jnp.ones((n_tok, LANE), F32).at[:seq, MLA_NOPE:MLA_NOPE + MLA_ROPE].set(cos)
    sin_t = jnp.zeros((n_tok, LANE), F32).at[:seq, MLA_NOPE:MLA_NOPE + MLA_ROPE].set(sin)
    return cos_t, sin_t


def _batch_group(b, cap=2):
    return max(g for g in (4, 2, 1) if g <= cap and b % g == 0)


MOE_BATCH_GROUP = 4


def _kind_map(n_lat_tiles):
    return lambda b, i: (b, jnp.where(i >= n_lat_tiles, 1, 0), 0, 0)


def _project(stream, nt, modtab, g1, pw, rope, n_lat_tiles):
    b, _, d = stream[0].shape
    cos_t, sin_t = rope
    group = _batch_group(b)
    full = lambda a: pl.BlockSpec(a.shape, lambda bb, i: (0,) * a.ndim)
    head_out = lambda: pl.BlockSpec((group, MLA_HEADS, TM, LANE), lambda bb, i: (bb, 0, i, 0))
    tok_out = lambda n: pl.BlockSpec((group, TM, n), lambda bb, i: (bb, i, 0))
    head_shape = jax.ShapeDtypeStruct((b, MLA_HEADS, nt, LANE), BF16)
    pair_out = lambda: pl.BlockSpec((group, MLA_HEADS // 2, TM, 2 * LANE), lambda bb, i: (bb, 0, i, 0))
    pair_shape = jax.ShapeDtypeStruct((b, MLA_HEADS // 2, nt, 2 * LANE), BF16)
    tok_shape = lambda n: jax.ShapeDtypeStruct((b, nt, n), F32)
    weights = (g1, pw['w1'], pw['gq'], pw['gkv'], pw['wuq'], pw['wuq_rot'], pw['wk'], pw['wv'])
    return pl.pallas_call(
        functools.partial(_proj_kernel, n_lat_tiles=n_lat_tiles),
        grid=(b // group, nt // TM),
        in_specs=_stream_specs(stream, group, n_lat_tiles)
                 + [pl.BlockSpec((group, 1, 8, d), _kind_map(n_lat_tiles))]
                 + [full(w) for w in weights]
                 + [pl.BlockSpec((TM, LANE), lambda bb, i: (i, 0))] * 2,
        out_specs=[head_out(), head_out(), head_out(), head_out(), head_out(), pair_out(),
                   tok_out(3 * HY_WIDTH), tok_out(LRU_WIDTH), tok_out(LRU_WIDTH)],
        out_shape=[head_shape, head_shape, head_shape, head_shape, head_shape, pair_shape,
                   tok_shape(3 * HY_WIDTH), tok_shape(LRU_WIDTH), tok_shape(LRU_WIDTH)],
        compiler_params=_cparams("parallel", "parallel"),
        name="input_projection",
    )(stream[0], stream[1], modtab, *weights, cos_t, sin_t)


SOFTMAX_DEN_LANE = 64


def _softmax_pv(parts, lane_off=0):
    m = None
    for s, _ in parts:
        mm = jnp.max(s, axis=-1, keepdims=True)
        m = mm if m is None else jnp.maximum(m, mm)
    acc = None
    for s, v in parts:
        o = _dot(jnp.exp(s - m).astype(BF16), v)
        acc = o if acc is None else acc + o
    den = lane_off + SOFTMAX_DEN_LANE
    return acc[:, lane_off:lane_off + LANE] / acc[:, den:den + 1]


def _mla_kernel(q_ref, k_ref, v_ref, o_ref, *, seq, n_lat_tiles):
    i = pl.program_id(1)
    nt = k_ref.shape[2]

    def attend(lo, hi):
        for hd in range(MLA_HEADS):
            s = _dot_nt(q_ref[0, hd], k_ref[0, hd, lo:hi, :])
            o = _softmax_pv([(s, v_ref[0, hd, lo:hi, :])])
            o_ref[0, :, hd * LANE:(hd + 1) * LANE] = o.astype(BF16)

    @pl.when(i < n_lat_tiles)
    def _():
        attend(0, nt)

    @pl.when(i >= n_lat_tiles)
    def _():
        attend(seq, nt)


def _mla_attention(q, k, v, seq, tiles_used):
    b, h, nt, _ = q.shape
    kv_spec = pl.BlockSpec((1, h, nt, LANE), lambda bb, i: (bb, 0, 0, 0))
    return pl.pallas_call(
        functools.partial(_mla_kernel, seq=seq, n_lat_tiles=seq // TM),
        grid=(b, tiles_used),
        in_specs=[pl.BlockSpec((1, h, TM, LANE), lambda bb, i: (bb, 0, i, 0)), kv_spec, kv_spec],
        out_specs=pl.BlockSpec((1, TM, h * LANE), lambda bb, i: (bb, i, 0)),
        out_shape=jax.ShapeDtypeStruct((b, tiles_used * TM, h * LANE), BF16),
        compiler_params=_cparams("parallel", "arbitrary"),
        name="mla_attention",
    )(q, k, v)


def _na_bias_tables(rpb, rows):
    n_blk = rows // NA_TILE_ROWS
    col = np.arange(GRID_W)
    c0 = np.clip(col - NA_WIN_C // 2, 0, GRID_W - NA_WIN_C)
    in_win = (col[None, :] >= c0[:, None]) & (col[None, :] < c0[:, None] + NA_WIN_C)
    dc = np.clip(col[None, :] - col[:, None], 1 - NA_WIN_C, NA_WIN_C - 1) + NA_WIN_C - 1
    rpb = rpb.astype(F32)
    tables = []
    for j in (0, 1, n_blk - 1):
        w0 = min(max(NA_TILE_ROWS * j - NA_WIN_R // 2, 0), rows - NA_KEY_ROWS)
        r = NA_TILE_ROWS * j + np.arange(NA_TILE_ROWS)
        kr = w0 + np.arange(NA_KEY_ROWS)
        r0 = np.clip(r - NA_WIN_R // 2, 0, rows - NA_WIN_R)
        row_ok = (kr[None, :] >= r0[:, None]) & (kr[None, :] < r0[:, None] + NA_WIN_R)
        dr = np.clip(kr[None, :] - r[:, None] + NA_WIN_R - 1, 0, 2 * NA_WIN_R - 2)
        oh_r = jnp.asarray(np.eye(2 * NA_WIN_R - 1, dtype=np.float32)[dr.reshape(-1)])
        oh_c = jnp.asarray(np.eye(2 * NA_WIN_C - 1, dtype=np.float32)[dc.reshape(-1)])
        bias = jnp.einsum('ar,hrc,bc->hab', oh_r, rpb, oh_c, precision=lax.Precision.HIGHEST)
        bias = bias.reshape(NA_HEADS, NA_TILE_ROWS, NA_KEY_ROWS, GRID_W, GRID_W)
        mask = row_ok[:, :, None, None] & in_win[None, None, :, :]
        bias = jnp.where(jnp.asarray(mask)[None], bias, -jnp.inf)
        tables.append(bias.transpose(0, 1, 3, 2, 4).reshape(NA_HEADS, TM, NA_KEYS))
    return jnp.stack(tables)


def _na_kernel(q_ref, k_ref, v_ref, bias_ref, o_ref, *, seq, n_lat_tiles):
    i = pl.program_id(1)
    nt = k_ref.shape[2]
    rows = seq // GRID_W

    @pl.when(i < n_lat_tiles)
    def _():
        w0 = jnp.clip(NA_TILE_ROWS * i - NA_WIN_R // 2, 0, rows - NA_KEY_ROWS)
        start = pl.multiple_of(w0 * GRID_W, GRID_W)
        for hd in range(NA_HEADS):
            q = q_ref[0, hd]
            s_loc = _dot_nt(q, k_ref[0, hd, pl.ds(start, NA_KEYS), :]) + bias_ref[0, hd]
            s_ctx = _dot_nt(q, k_ref[0, hd, seq:nt, :])
            o = _softmax_pv([(s_loc, v_ref[0, hd // 2, pl.ds(start, NA_KEYS), :]),
                             (s_ctx, v_ref[0, hd // 2, seq:nt, :])], (hd % 2) * LANE)
            o_ref[0, :, hd * LANE:(hd + 1) * LANE] = o.astype(BF16)

    @pl.when(i >= n_lat_tiles)
    def _():
        for hd in range(NA_HEADS):
            s = _dot_nt(q_ref[0, hd], k_ref[0, hd, seq:nt, :])
            o = _softmax_pv([(s, v_ref[0, hd // 2, seq:nt, :])], (hd % 2) * LANE)
            o_ref[0, :, hd * LANE:(hd + 1) * LANE] = o.astype(BF16)


def _na_attention(q, k, v, bias, seq, tiles_used):
    b, h, nt, _ = q.shape
    n_lat = seq // TM
    kv_spec = pl.BlockSpec((1, h, nt, LANE), lambda bb, i: (bb, 0, 0, 0))
    cfg = lambda bb, i: (jnp.where(i == 0, 0, jnp.where(i >= n_lat - 1, 2, 1)), 0, 0, 0)
    return pl.pallas_call(
        functools.partial(_na_kernel, seq=seq, n_lat_tiles=n_lat),
        grid=(b, tiles_used),
        in_specs=[pl.BlockSpec((1, h, TM, LANE), lambda bb, i: (bb, 0, i, 0)), kv_spec,
                  pl.BlockSpec((1, h // 2, nt, 2 * LANE), lambda bb, i: (bb, 0, 0, 0)),
                  pl.BlockSpec((1, h, TM, NA_KEYS), cfg)],
        out_specs=pl.BlockSpec((1, TM, h * LANE), lambda bb, i: (bb, i, 0)),
        out_shape=jax.ShapeDtypeStruct((b, tiles_used * TM, h * LANE), BF16),
        compiler_params=_cparams("parallel", "arbitrary"),
        name="neighbourhood_attention",
    )(q, k, v, bias)


def _hp_dot(a, b):
    return jnp.dot(a, b, preferred_element_type=F32, precision=lax.Precision.HIGHEST)


def _hy_filter_kernel(f_ref, w1_ref, b1_ref, w2_ref, b2_ref, w3_ref, dec_ref, o_ref, obf_ref):
    f = f_ref[...]
    h = jnp.sin(_hp_dot(f, w1_ref[...]) + b1_ref[...])
    h = jnp.sin(_hp_dot(h, w2_ref[...]) + b2_ref[...])
    h = _hp_dot(h, w3_ref[...])
    h = h * jnp.exp(-f[:, 0:1] * jnp.abs(dec_ref[...]))
    o_ref[...] = h
    obf_ref[...] = h.astype(BF16)


def _hy_pos_features(length):
    t = jnp.linspace(0.0, 1.0, length, dtype=F32)[:, None]
    w = 2.0 * math.pi * jnp.arange(length, dtype=F32)[:, None] / length
    f = jnp.linspace(1e-4, HY_BANDS - 1, HY_BANDS, dtype=F32)[None, :]
    z = w * f
    return jnp.concatenate([t, jnp.cos(z), -jnp.sin(z)], axis=-1)


DFT_ROW_STEP = 64
DFT_ROWS = 256
DFT_FWD_COLS = 1024
DFT_INV_COLS = 512
FILTER_ROWS = 512


def _col_block(nc, cap):
    return min(nc, cap)


def _hyp_pre_kernel(z_ref, w_ref, b_ref, lat_o, lat_bf_o, ctx_o, ctx_bf_o, *, seq, ctx_len):
    w = w_ref[...]
    bias = b_ref[...]
    for lo, length, o_ref, obf_ref in ((0, seq, lat_o, lat_bf_o), (seq, ctx_len, ctx_o, ctx_bf_o)):
        h = length // 2
        even = z_ref[0, pl.ds(lo, h, stride=2), :]
        odd = z_ref[0, pl.ds(lo + 1, h, stride=2), :]
        row = lax.broadcasted_iota(I32, even.shape, 0)
        odd_prev = jnp.where(row == 0, 0.0, pltpu.roll(odd, 1, axis=0))
        even_next = jnp.where(row == h - 1, 0.0, pltpu.roll(even, h - 1, axis=0))
        y_even = bias + w[0:1] * odd_prev + w[1:2] * even + w[2:3] * odd
        y_odd = bias + w[0:1] * even + w[1:2] * odd + w[2:3] * even_next
        for r, y in enumerate((y_even, y_odd)):
            o_ref[0, r] = y
            obf_ref[0, r] = y.astype(BF16)


def _hyp_pre(hy, w, bvec, seq):
    b, nt, _ = hy.shape
    ctx_len = nt - seq
    wpad = _pad_to(w, 8, 0)
    per_plane = HY_WIDTH // LANE
    out_specs, out_shape = [], []
    for length in (seq, ctx_len):
        for dt in (F32, BF16):
            out_specs.append(pl.BlockSpec((1, 2, length // 2, LANE),
                                          lambda bb, g: (g // per_plane, 0, 0, bb * per_plane + g % per_plane)))
            out_shape.append(jax.ShapeDtypeStruct((3, 2, length // 2, b * HY_WIDTH), dt))
    return pl.pallas_call(
        functools.partial(_hyp_pre_kernel, seq=seq, ctx_len=ctx_len),
        grid=(b, 3 * per_plane),
        in_specs=[pl.BlockSpec((1, nt, LANE), lambda bb, g: (bb, 0, g)),
                  pl.BlockSpec((8, LANE), lambda bb, g: (0, g)),
                  pl.BlockSpec((1, LANE), lambda bb, g: (0, g))],
        out_specs=out_specs,
        out_shape=out_shape,
        compiler_params=_cparams("parallel", "parallel"),
        name="hyena_short_conv",
    )(hy, wpad, bvec.reshape(1, -1))


def _hyp_filters(length, lp):
    feats = _hy_pos_features(length)
    feats = feats.reshape(length // 2, 2, HY_EMB).transpose(1, 0, 2).reshape(length, HY_EMB)
    n_out = HY_ORDER * 2 * HY_WIDTH
    tl = min(length // 2, FILTER_ROWS)
    full = lambda a: pl.BlockSpec(a.shape, lambda i: (0,) * a.ndim)
    args = (lp['hy_w1'], lp['hy_b1'].reshape(1, -1), lp['hy_w2'], lp['hy_b2'].reshape(1, -1), lp['hy_w3'],
            lp['hy_decay'].reshape(1, n_out))
    filt, filt_bf = pl.pallas_call(
        _hy_filter_kernel,
        grid=(length // tl,),
        in_specs=[pl.BlockSpec((tl, HY_EMB), lambda i: (i, 0))] + [full(a) for a in args],
        out_specs=[pl.BlockSpec((tl, n_out), lambda i: (i, 0))] * 2,
        out_shape=[jax.ShapeDtypeStruct((length, n_out), F32), jax.ShapeDtypeStruct((length, n_out), BF16)],
        compiler_params=_cparams("parallel"),
        name="hyena_filter_mlp",
    )(feats, *args)
    return filt.reshape(2, length // 2, n_out), filt_bf.reshape(2, length // 2, n_out)


def _hyp_matrix_kernel(c1_ref, s1_ref, c2_ref, s2_ref, ck_ref, sk_ref, ckr_ref, skr_ref,
                       ce_o, se_o, co_o, so_o, set_o, cot_o, sot_o):
    c1, s1 = c1_ref[0], s1_ref[0]
    c2, s2 = c2_ref[...], s2_ref[...]
    cos = c1 * c2 - s1 * s2
    sin = s1 * c2 + c1 * s2
    row = lax.broadcasted_iota(I32, cos.shape, 0)
    col = lax.broadcasted_iota(I32, cos.shape, 1)
    first_row = (row + pl.program_id(0) * DFT_ROW_STEP) == 0
    alt_col = jnp.where(col % 2 == 0, 1.0, -1.0)
    alt_row = jnp.where(row % 2 == 0, 1.0, -1.0)
    ck, sk = ck_ref[...], sk_ref[...]
    ce_o[...] = cos.astype(BF16)
    se_o[...] = jnp.where(first_row, alt_col, -sin).astype(BF16)
    co_o[...] = (cos * ck - sin * sk).astype(BF16)
    so_o[...] = jnp.where(first_row, alt_col, -(sin * ck + cos * sk)).astype(BF16)
    ckr, skr = ckr_ref[...], skr_ref[...]
    set_o[...] = jnp.where(col == 0, alt_row, -sin).astype(BF16)
    cot_o[...] = (cos * ckr - sin * skr).astype(BF16)
    sot_o[...] = jnp.where(col == 0, alt_row, -(sin * ckr + cos * skr)).astype(BF16)


def _hyp_matrices(length):
    h = length // 2
    step = DFT_ROW_STEP
    m = jnp.arange(h, dtype=I32)

    def trig(kv):
        ang = ((kv[:, None] * m[None, :]) % length).astype(F32) * (2.0 * math.pi / length)
        return jnp.cos(ang), jnp.sin(ang)

    c1, s1 = trig(jnp.arange(h // step, dtype=I32) * step)
    c2, s2 = trig(jnp.arange(step, dtype=I32))
    half_angle = m.astype(F32) * (math.pi / length)
    ck, sk = jnp.cos(half_angle), jnp.sin(half_angle)
    coarse = pl.BlockSpec((1, 1, h), lambda j: (j, 0, 0))
    fine = pl.BlockSpec((step, h), lambda j: (0, 0))
    per_row = pl.BlockSpec((step, 1), lambda j: (j, 0))
    per_col = pl.BlockSpec((1, h), lambda j: (0, 0))
    out = pl.BlockSpec((step, h), lambda j: (j, 0))
    ce, se, co, so, se_t, co_t, so_t = pl.pallas_call(
        _hyp_matrix_kernel,
        grid=(h // step,),
        in_specs=[coarse, coarse, fine, fine, per_row, per_row, per_col, per_col],
        out_specs=[out] * 7,
        out_shape=[jax.ShapeDtypeStruct((h, h), BF16)] * 7,
        compiler_params=_cparams("parallel"),
        name="dft_matrices",
    )(c1[:, None, :], s1[:, None, :], c2, s2, ck[:, None], sk[:, None], ck[None, :], sk[None, :])
    return dict(fwd=(ce, se, co, so), inv=(ce, se_t, co_t, so_t))


def _hyp_fwd_kernel(ce_ref, se_ref, co_ref, so_ref, xe_ref, xo_ref, *rest, with_taps):
    xe, xo = xe_ref[...], xo_ref[...]
    a_re, a_im = _dot(ce_ref[...], xe), _dot(se_ref[...], xe)
    b_re, b_im = _dot(co_ref[...], xo), _dot(so_ref[...], xo)
    if not with_taps:
        for ref, val in zip(rest, (a_re, a_im, b_re, b_im)):
            ref[...] = val
        return
    t1re_ref, t1im_ref, t2re_ref, t2im_ref, sp_ref, gere_o, geim_o, gore_o, goim_o = rest
    t1re, t1im, t2re, t2im = t1re_ref[...], t1im_ref[...], t2re_ref[...], t2im_ref[...]
    sp = sp_ref[...]
    first_block = pl.program_id(1) == 0
    row8 = lax.broadcasted_iota(I32, (8, HY_WIDTH), 0)
    for bb in range(xe.shape[1] // HY_WIDTH):
        sl = slice(bb * HY_WIDTH, (bb + 1) * HY_WIDTH)
        are, aim, bre, bim = a_re[:, sl], a_im[:, sl], b_re[:, sl], b_im[:, sl]
        u1re, u1im = are + bre, aim + bim
        u2re, u2im = are - bre, bim - aim
        z1re, z1im = u1re * t1re - u1im * t1im, u1re * t1im + u1im * t1re
        z2re, z2im = u2re * t2re - u2im * t2im, u2re * t2im + u2im * t2re
        gere_o[:, sl] = (z1re + z2re).astype(BF16)
        geim_o[:, sl] = (z1im - z2im).astype(BF16)
        gore_o[:, sl] = (z1re - z2re).astype(BF16)
        goim_o[:, sl] = (z1im + z2im).astype(BF16)

        @pl.when(first_block)
        def _():
            u0, ul = u1re[0:8], u2re[0:8]
            a_s, b_s = aim[0:8], bim[0:8]
            dc, ny, mre, mim = sp[0:1], sp[1:2], sp[2:3], sp[3:4]
            first = row8 == 0
            gere_o[0:8, sl] = jnp.where(first, u0 * dc + ul * ny, (z1re + z2re)[0:8]).astype(BF16)
            gore_o[0:8, sl] = jnp.where(first, u0 * dc - ul * ny, (z1re - z2re)[0:8]).astype(BF16)
            geim_o[0:8, sl] = jnp.where(first, a_s * mre + b_s * mim, (z1im - z2im)[0:8]).astype(BF16)
            goim_o[0:8, sl] = jnp.where(first, b_s * mre - a_s * mim, (z1im + z2im)[0:8]).astype(BF16)


def _hyp_fwd(mats, x, plane, taps=None):
    _, _, h, nc = x.shape
    tk = min(h, DFT_ROWS)
    cb = _col_block(nc, DFT_FWD_COLS)
    grid = (nc // cb, h // tk)
    m_spec = pl.BlockSpec((tk, h), lambda c, j: (j, 0))
    x_spec = lambda r: pl.BlockSpec((None, None, h, cb), lambda c, j: (plane, r, 0, c))
    o_spec = pl.BlockSpec((tk, cb), lambda c, j: (j, c))
    if taps is None:
        return pl.pallas_call(
            functools.partial(_hyp_fwd_kernel, with_taps=False),
            grid=grid, in_specs=[m_spec] * 4 + [x_spec(0), x_spec(1)], out_specs=[o_spec] * 4,
            out_shape=[jax.ShapeDtypeStruct((h, nc), F32)] * 4,
            compiler_params=_cparams("parallel", "arbitrary"),
            name="hyena_dft_filters",
        )(*mats['fwd'], x, x)
    t_spec = pl.BlockSpec((tk, HY_WIDTH), lambda c, j: (j, 0))
    sp_spec = pl.BlockSpec((8, HY_WIDTH), lambda c, j: (0, 0))
    return pl.pallas_call(
        functools.partial(_hyp_fwd_kernel, with_taps=True),
        grid=grid, in_specs=[m_spec] * 4 + [x_spec(0), x_spec(1)] + [t_spec] * 4 + [sp_spec],
        out_specs=[o_spec] * 4,
        out_shape=[jax.ShapeDtypeStruct((h, nc), BF16)] * 4,
        compiler_params=_cparams("parallel", "arbitrary"),
        name="hyena_dft_forward",
    )(*mats['fwd'], x, x, *taps)


def _hyp_inv_kernel(ce_ref, set_ref, cot_ref, sot_ref, gere_ref, geim_ref, gore_ref, goim_ref,
                    gate_ref, prev_ref, bias_ref, *outs, last):
    conv = (_dot(ce_ref[...], gere_ref[...]) + _dot(set_ref[...], geim_ref[...]),
            _dot(cot_ref[...], gore_ref[...]) + _dot(sot_ref[...], goim_ref[...]))
    bias = bias_ref[...]
    for r in range(2):
        y = gate_ref[r] * (conv[r] + prev_ref[r] * bias)
        if last:
            (tok_o,) = outs
            for bb in range(y.shape[1] // HY_WIDTH):
                tok_o[bb, :, r * HY_WIDTH:(r + 1) * HY_WIDTH] = y[:, bb * HY_WIDTH:(bb + 1) * HY_WIDTH].astype(BF16)
        else:
            y_o, ybf_o = outs
            y_o[r] = y
            ybf_o[r] = y.astype(BF16)


def _hyp_inv(mats, g, gate, gate_plane, prev, prev_plane, bias_row, last):
    h, nc = g[0].shape
    tm = min(h, DFT_ROWS)
    cb = _col_block(nc, DFT_INV_COLS)
    grid = (nc // cb, h // tm)
    m_spec = pl.BlockSpec((tm, h), lambda c, i: (i, 0))
    g_spec = pl.BlockSpec((h, cb), lambda c, i: (0, c))
    e_spec = lambda plane: pl.BlockSpec((None, 2, tm, cb), lambda c, i: (plane, 0, i, c))
    b_spec = pl.BlockSpec((1, cb), lambda c, i: (0, c))
    if last:
        out_specs = [pl.BlockSpec((cb // HY_WIDTH, tm, 2 * HY_WIDTH), lambda c, i: (c, i, 0))]
        out_shape = [jax.ShapeDtypeStruct((nc // HY_WIDTH, h, 2 * HY_WIDTH), BF16)]
    else:
        out_specs = [e_spec(0), e_spec(0)]
        out_shape = [jax.ShapeDtypeStruct((1, 2, h, nc), F32), jax.ShapeDtypeStruct((1, 2, h, nc), BF16)]
    return pl.pallas_call(
        functools.partial(_hyp_inv_kernel, last=last),
        grid=grid, in_specs=[m_spec] * 4 + [g_spec] * 4 + [e_spec(gate_plane), e_spec(prev_plane), b_spec],
        out_specs=out_specs, out_shape=out_shape,
        compiler_params=_cparams("parallel", "arbitrary"),
        name="hyena_dft_inverse",
    )(*mats['inv'], *g, gate, prev, bias_row)


def _hyp_tap_tables(spec, filt, length):
    a_re, a_im, b_re, b_im = spec
    w = HY_WIDTH
    inv_n = 1.0 / (2 * length)
    tables = []
    for o in range(HY_ORDER):
        f_sl = slice((2 * o) * w, (2 * o + 1) * w)
        r_sl = slice((2 * o + 1) * w, (2 * o + 2) * w)
        hb0 = filt[0, 0:1, r_sl]
        f1re = (a_re + b_re)[:, f_sl] + (a_re + b_re)[:, r_sl] - hb0
        f1im = (a_im + b_im)[:, f_sl] - (a_im + b_im)[:, r_sl]
        f2re = (a_re - b_re)[:, f_sl] + (a_re - b_re)[:, r_sl] - hb0
        f2im = (b_im - a_im)[:, f_sl] - (b_im - a_im)[:, r_sl]
        dc = f1re[0:1]
        ny = f2re[0:1]
        mid_re = a_im[0:1, f_sl] + a_im[0:1, r_sl] - hb0
        mid_im = -b_im[0:1, f_sl] + b_im[0:1, r_sl]
        sp = jnp.concatenate([dc * inv_n, ny * inv_n, mid_re * (2 * inv_n), mid_im * (2 * inv_n),
                              jnp.zeros((4, w), F32)], axis=0)
        tables.append((f1re * (2 * inv_n), f1im * (2 * inv_n), f2re * (2 * inv_n), f2im * (2 * inv_n), sp))
    return tables


def _hyena_seq(mats, vx, vx_bf, lp, n_batch):
    h = vx.shape[2]
    length = 2 * h
    filt, filt_bf = _hyp_filters(length, lp)
    spec = _hyp_fwd(mats, filt_bf[None], 0)
    tables = _hyp_tap_tables(spec, filt, length)
    bias = lp['hy_bias'].astype(F32)
    y, y_bf = vx, vx_bf
    for o in range(HY_ORDER):
        g = _hyp_fwd(mats, y_bf, 0, tables[o])
        bias_row = jnp.tile(bias[o][None, :], (1, n_batch))
        last = o == HY_ORDER - 1
        res = _hyp_inv(mats, g, vx, o + 1, y, 0, bias_row, last)
        if last:
            return res[0].reshape(n_batch, length, HY_WIDTH)
        y, y_bf = res


def _lru_kernel(u_ref, g_ref, cw_ref, cb_ref, wa_ref, ba_ref, wx_ref, bx_ref, lam_ref, o_ref,
                pad_ref, y_ref, *, seq, ctx_len):
    tc = LRU_CHUNK
    halo = LRU_HALO
    width = LRU_WIDTH
    lat_off = halo
    ctx_off = 2 * halo + seq
    zero = jnp.zeros((halo, width), F32)
    pad_ref[0:halo, :] = zero
    pad_ref[lat_off:lat_off + seq, :] = u_ref[0, 0:seq, :]
    pad_ref[lat_off + seq:ctx_off, :] = zero
    pad_ref[ctx_off:ctx_off + ctx_len, :] = u_ref[0, seq:seq + ctx_len, :]
    pad_ref[ctx_off + ctx_len:ctx_off + ctx_len + halo, :] = zero
    row = lax.broadcasted_iota(I32, (tc, width), 0)
    n_win = tc + 2 * halo

    def chunk(pad_off, y_off, s, carry, d):
        wstart = pl.multiple_of(pad_off + s - halo, 8)
        win = pad_ref[pl.ds(wstart, n_win), :]
        cw = cw_ref[d]
        xc = cb_ref[d]
        for k in range(LRU_CONV):
            shift = (LRU_CONV - 1 - k) if d == 0 else -k
            rolled = win if shift == 0 else pltpu.roll(win, shift % n_win, axis=0)
            xc = xc + cw[k:k + 1] * rolled[halo:halo + tc]
        xb = xc.astype(BF16)
        r = _sigmoid(_dot(xb, wa_ref[d]) + ba_ref[d])
        gi = _sigmoid(_dot(xb, wx_ref[d]) + bx_ref[d])
        lam = lam_ref[d]
        softplus = jnp.maximum(-lam, 0.0) + jnp.log1p(jnp.exp(-jnp.abs(lam)))
        log_a = -LRU_C * r * softplus
        a = jnp.exp(log_a)
        bt = jnp.sqrt(-jnp.tanh(log_a) * (a * a + 1.0)) * (gi * xc)
        sft = 1
        while sft < tc:
            if d == 0:
                keep = row >= sft
                a_s = jnp.where(keep, pltpu.roll(a, sft, axis=0), 1.0)
                b_s = jnp.where(keep, pltpu.roll(bt, sft, axis=0), 0.0)
            else:
                keep = row < tc - sft
                a_s = jnp.where(keep, pltpu.roll(a, tc - sft, axis=0), 1.0)
                b_s = jnp.where(keep, pltpu.roll(bt, tc - sft, axis=0), 0.0)
            bt = a * b_s + bt
            a = a * a_s
            sft *= 2
        h = a * carry + bt
        yo = pl.multiple_of(y_off + s, 8)
        if d == 0:
            y_ref[pl.ds(yo, tc), :] = h
            return h[tc - 1:tc]
        y_ref[pl.ds(yo, tc), :] = y_ref[pl.ds(yo, tc), :] + h
        return h[0:1]

    n_lat = seq // tc
    n_ctx = ctx_len // tc
    for d in range(2):
        carry = jnp.zeros((1, width), F32)
        order = range(n_ctx) if d == 0 else range(n_ctx - 1, -1, -1)
        for c in order:
            carry = chunk(ctx_off, seq, c * tc, carry, d)

        def body(j, cr, d=d):
            jj = j if d == 0 else n_lat - 1 - j
            return chunk(lat_off, 0, jj * tc, cr, d)

        lax.fori_loop(0, n_lat, body, carry)
    o_ref[0] = (y_ref[...] * _gelu_tanh(g_ref[0])).astype(BF16)


def _block_diag(w):
    nd, nb, c, _ = w.shape
    out = jnp.zeros((nd, nb * c, nb * c), w.dtype)
    for n in range(nb):
        out = out.at[:, n * c:(n + 1) * c, n * c:(n + 1) * c].set(w[:, n])
    return out


def _lru_mixer(lu, lg, lp, seq):
    b, nt, w = lu.shape
    ctx_len = nt - seq
    row3 = lambda a: a.reshape(2, 1, w)
    args = (_pad_to(lp['lru_conv_w'], 8, 1), row3(lp['lru_conv_b']), _block_diag(lp['lru_wa']).astype(BF16),
            row3(lp['lru_ba']), _block_diag(lp['lru_wx']).astype(BF16), row3(lp['lru_bx']), row3(lp['lru_lambda']))
    full = lambda a: pl.BlockSpec(a.shape, lambda bb: (0,) * a.ndim)
    tok = pl.BlockSpec((1, nt, w), lambda bb: (bb, 0, 0))
    return pl.pallas_call(
        functools.partial(_lru_kernel, seq=seq, ctx_len=ctx_len),
        grid=(b,),
        in_specs=[tok, tok] + [full(a) for a in args],
        out_specs=tok,
        out_shape=jax.ShapeDtypeStruct((b, nt, w), BF16),
        scratch_shapes=[pltpu.VMEM((nt + 3 * LRU_HALO, w), F32), pltpu.VMEM((nt, w), F32)],
        compiler_params=_cparams("parallel"),
        name="rglru_scan",
    )(lu, lg, *args)


def _merge_kernel(xl_ref, xc_ref, mod_ref, g1_ref, a_ref, b_ref, c_ref, d_ref, wg_ref, wa_ref, wb_ref, wc_ref,
                  wd_ref, wo_ref, o_ref, *, n_lat_tiles):
    group, tm, dm = xl_ref.shape
    m = mod_ref[:, 0]
    x = _stream_tile(xl_ref, xc_ref, n_lat_tiles)
    h = _normmod(x, g1_ref[...], m[:, 1:2], m[:, 0:1]).reshape(group * tm, dm).astype(BF16)
    acc = None
    for k, (br, w) in enumerate(((a_ref, wa_ref), (b_ref, wb_ref), (c_ref, wc_ref), (d_ref, wd_ref))):
        term = (1.0 + jnp.tanh(_dot(h, wg_ref[:, k * dm:(k + 1) * dm]))) * _dot(
            br[...].reshape(group * tm, br.shape[-1]), w[...])
        acc = term if acc is None else acc + term
    y = _dot(acc.astype(BF16), wo_ref[...])
    o_ref[...] = x + m[:, 2:3] * y.reshape(group, tm, dm)


def _merge(stream, modtab, g1, branches, w_gate, lp, n_lat_tiles, tiles_used):
    b, _, d = stream[0].shape
    wbr = lp['w_branch']
    head_rows = lambda w, dv: jnp.concatenate(
        [_pad_to(w[hd * dv:(hd + 1) * dv], LANE, 0) for hd in range(4)], axis=0)
    wbr = 0.5 * wbr
    weights = (head_rows(wbr[0], MLA_V).astype(BF16), head_rows(wbr[1], NA_HEAD_DIM).astype(BF16),
               wbr[2].astype(BF16), wbr[3].astype(BF16), lp['w_out'].astype(BF16))
    group = _batch_group(b)
    full = lambda a: pl.BlockSpec(a.shape, lambda bb, i: (0,) * a.ndim)
    tok = lambda n: pl.BlockSpec((group, TM, n), lambda bb, i: (bb, i, 0))
    return pl.pallas_call(
        functools.partial(_merge_kernel, n_lat_tiles=n_lat_tiles),
        grid=(b // group, tiles_used),
        in_specs=_stream_specs(stream, group, n_lat_tiles)
                 + [pl.BlockSpec((group, 1, 8, d), _kind_map(n_lat_tiles)), full(g1)]
                 + [tok(br.shape[-1]) for br in branches] + [full(w_gate)] + [full(w) for w in weights],
        out_specs=tok(d),
        out_shape=jax.ShapeDtypeStruct((b, tiles_used * TM, d), F32),
        compiler_params=_cparams("parallel", "parallel"),
        name="merge_branches",
    )(stream[0], stream[1], modtab, g1, *branches, w_gate, *weights)


U32 = jnp.uint32
EXPERT_ROWS = 896


def _pack_pair(x):
    n = x.shape[-1] // 2
    hi = lax.bitcast_convert_type(x[:, :n].astype(BF16).astype(F32), U32)
    lo = lax.bitcast_convert_type(x[:, n:].astype(BF16).astype(F32), U32)
    return hi | (lo >> 16)


def _unpack_pair(p):
    hi = lax.bitcast_convert_type(p & jnp.uint32(0xFFFF0000), F32)
    lo = lax.bitcast_convert_type(p << 16, F32)
    return hi, lo


def _router_kernel(x_ref, mod_ref, g2_ref, rw_ref, rb_ref, tri_ref, h2_o, idx_o, wts_o, rank_o, cnt_o, carry):
    i = pl.program_id(0)

    @pl.when(i == 0)
    def _():
        carry[...] = jnp.zeros_like(carry)

    group, tm, d = x_ref.shape
    m = mod_ref[:, 0]
    h2 = _normmod(x_ref[...], g2_ref[...], m[:, 4:5], m[:, 3:4]).reshape(group * tm, d)
    half = h2.shape[-1] // 2
    h2_o[0] = _pack_pair(h2[:, :half])
    h2_o[1] = _pack_pair(h2[:, half:])
    logits = lax.dot_general(rw_ref[...], h2, (((1,), (1,)), ((), ())), preferred_element_type=F32,
                             precision=lax.Precision.HIGHEST)
    scores = _sigmoid(logits)
    biased = scores + rb_ref[...]
    expert = lax.broadcasted_iota(I32, scores.shape, 0)
    picks = []
    onehot_all = jnp.zeros(scores.shape, F32)
    for _ in range(TOP_K):
        best = jnp.max(biased, axis=0, keepdims=True)
        arg = jnp.min(jnp.where(biased == best, expert, N_EXPERTS), axis=0, keepdims=True)
        hit = expert == arg
        sel = jnp.sum(jnp.where(hit, scores, 0.0), axis=0, keepdims=True)
        biased = jnp.where(hit, -jnp.inf, biased)
        onehot_all = onehot_all + jnp.where(hit, 1.0, 0.0)
        picks.append((arg, hit, sel))
    total = picks[0][2]
    for _, _, sel in picks[1:]:
        total = total + sel
    earlier = _dot(onehot_all.astype(BF16), tri_ref[...]) + carry[...]
    pad_rows = TOPK_PAD - TOP_K
    ranks = [jnp.sum(jnp.where(hit, earlier, 0.0), axis=0, keepdims=True).astype(I32) for _, hit, _ in picks]
    scale = ROUTED_SCALE / total
    n_tok = scores.shape[1]
    idx_o[...] = jnp.concatenate([arg for arg, _, _ in picks] + [jnp.zeros((pad_rows, n_tok), I32)], axis=0)
    wts_o[...] = jnp.concatenate([sel * scale for _, _, sel in picks] + [jnp.zeros((pad_rows, n_tok), F32)], axis=0)
    rank_o[...] = jnp.concatenate(ranks + [jnp.zeros((pad_rows, n_tok), I32)], axis=0)
    carry[...] = carry[...] + jnp.sum(onehot_all, axis=1, keepdims=True)
    cnt_o[...] = carry[...]


def _tile_maps(n_groups, n_lat_tiles):
    tok = lambda f: (f % n_groups, f // n_groups, 0)
    mod = lambda f: (f % n_groups, jnp.where(f // n_groups >= n_lat_tiles, 1, 0), 0, 0)
    return tok, mod


def _route(x1, modtab, g2, lp, n_lat_tiles, tiles_used):
    b, _, d = x1.shape
    group = _batch_group(b, MOE_BATCH_GROUP)
    rows = group * TM
    t = b * tiles_used * TM
    rw = lp['router_w'].T
    rb = lp['router_bias'].reshape(-1, 1)
    tri = (np.arange(rows)[:, None] < np.arange(rows)[None, :]).astype(np.float32)
    tri = jnp.asarray(tri, BF16)
    per_tok = lambda: pl.BlockSpec((TOPK_PAD, rows), lambda i: (0, i))
    full = lambda a: pl.BlockSpec(a.shape, lambda i: (0,) * a.ndim)
    tok, mod = _tile_maps(b // group, n_lat_tiles)
    return pl.pallas_call(
        _router_kernel,
        grid=(t // rows,),
        in_specs=[pl.BlockSpec((group, TM, d), tok), pl.BlockSpec((group, 1, 8, d), mod),
                  full(g2), full(rw), full(rb), full(tri)],
        out_specs=[pl.BlockSpec((2, rows, d // 4), lambda i: (0, i, 0)), per_tok(), per_tok(), per_tok(),
                   pl.BlockSpec((N_EXPERTS, 1), lambda i: (0, 0))],
        out_shape=[jax.ShapeDtypeStruct((2, t, d // 4), U32), jax.ShapeDtypeStruct((TOPK_PAD, t), I32),
                   jax.ShapeDtypeStruct((TOPK_PAD, t), F32), jax.ShapeDtypeStruct((TOPK_PAD, t), I32),
                   jax.ShapeDtypeStruct((N_EXPERTS, 1), F32)],
        scratch_shapes=[pltpu.VMEM((N_EXPERTS, 1), F32)],
        compiler_params=_cparams("arbitrary"),
        name="moe_router",
    )(x1, modtab, g2, rw, rb, tri)


SC_WINDOW = 128


def _sc_mesh():
    return plsc.VectorSubcoreMesh(core_axis_name="c", subcore_axis_name="s")


def _sc_scatter_rows(src, idx, n_out):
    n, width = src.shape
    k_rep = idx.shape[0]
    half = n // SC_WINDOW // 2

    @functools.partial(pl.kernel, out_type=jax.ShapeDtypeStruct((n_out, width), src.dtype), mesh=_sc_mesh(),
                       scratch_types=[], name="moe_dispatch_sc")
    def scatter(src_hbm, idx_hbm, out_hbm):
        def body(x_vmem, *i_vmems):
            for i_vmem in i_vmems:
                pltpu.sync_copy(x_vmem, out_hbm.at[i_vmem.at[0]])

        pltpu.emit_pipeline(
            body,
            grid=(2, half),
            in_specs=[pl.BlockSpec((SC_WINDOW, width), lambda a, i: (a * half + i, 0))]
                     + [pl.BlockSpec((1, SC_WINDOW), lambda a, i, k=k: (k, a * half + i)) for k in range(k_rep)],
            out_specs=[],
            core_axis_name=("c", "s"),
            dimension_semantics=(pltpu.PARALLEL, pltpu.PARALLEL),
        )(src_hbm, *([idx_hbm] * k_rep))

    return scatter(src, idx)


def _sc_gather_rows(src, idx):
    k_rep, n = idx.shape
    width = src.shape[1]
    n_win = n // SC_WINDOW

    @functools.partial(pl.kernel, out_type=jax.ShapeDtypeStruct((k_rep * n, width), src.dtype), mesh=_sc_mesh(),
                       scratch_types=[], name="moe_gather_sc")
    def gather(src_hbm, idx_hbm, out_hbm):
        def body(i_vmem, o_vmem):
            pltpu.sync_copy(src_hbm.at[i_vmem.at[0]], o_vmem)

        pltpu.emit_pipeline(
            body,
            grid=(k_rep, n_win),
            in_specs=[pl.BlockSpec((1, SC_WINDOW), lambda k, i: (k, i))],
            out_specs=[pl.BlockSpec((SC_WINDOW, width), lambda k, i: (k * n_win + i, 0))],
            core_axis_name=("c", "s"),
            dimension_semantics=(pltpu.PARALLEL, pltpu.PARALLEL),
        )(idx_hbm, out_hbm)

    return gather(src, idx)


def _unpack_planes(p0, p1):
    return _unpack_pair(p0) + _unpack_pair(p1)


def _dot_quarters(parts, w_ref):
    q = parts[0].shape[-1]
    acc = None
    for j, part in enumerate(parts):
        term = _dot(part.astype(BF16), w_ref[j * q:(j + 1) * q, :])
        acc = term if acc is None else acc + term
    return acc


def _expert_kernel(be_ref, nv_ref, xs_ref, wg_ref, wu_ref, wd_ref, ys_o, wg_s, wu_s, wd_s):
    i = pl.program_id(0)
    prev = be_ref[jnp.maximum(i - 1, 0)]

    @pl.when((i == 0) | (be_ref[i] != prev))
    def _():
        wg_s[...] = (0.5 * wg_ref[0]).astype(BF16)
        wu_s[...] = wu_ref[0].astype(BF16)
        wd_s[...] = wd_ref[0].astype(BF16)

    @pl.when(nv_ref[i] > 0)
    def _():
        keep = lax.broadcasted_iota(I32, xs_ref.shape[1:], 0) < nv_ref[i]
        parts = _unpack_planes(jnp.where(keep, xs_ref[0], jnp.uint32(0)), jnp.where(keep, xs_ref[1], jnp.uint32(0)))
        hid = _half_silu(_dot_quarters(parts, wg_s)) * _dot_quarters(parts, wu_s)
        y = _dot(hid.astype(BF16), wd_s[...])
        half = y.shape[-1] // 2
        ys_o[0] = _pack_pair(y[:, :half])
        ys_o[1] = _pack_pair(y[:, half:])

    @pl.when(nv_ref[i] <= 0)
    def _():
        ys_o[...] = jnp.zeros_like(ys_o)


def _experts(xs, block_e, n_valid, weights, layer):
    _, n_rows, dq = xs.shape
    d = 4 * dq
    n_blocks = n_rows // EXPERT_ROWS
    hid = EXPERT_HIDDEN
    grid_spec = pltpu.PrefetchScalarGridSpec(
        num_scalar_prefetch=2,
        grid=(n_blocks,),
        in_specs=[pl.BlockSpec((2, EXPERT_ROWS, dq), lambda i, be, nv: (0, i, 0)),
                  pl.BlockSpec((None, 1, d, hid), lambda i, be, nv: (layer, be[i], 0, 0)),
                  pl.BlockSpec((None, 1, d, hid), lambda i, be, nv: (layer, be[i], 0, 0)),
                  pl.BlockSpec((None, 1, hid, d), lambda i, be, nv: (layer, be[i], 0, 0))],
        out_specs=pl.BlockSpec((2, EXPERT_ROWS, dq), lambda i, be, nv: (0, i, 0)),
        scratch_shapes=[pltpu.VMEM((d, hid), BF16), pltpu.VMEM((d, hid), BF16), pltpu.VMEM((hid, d), BF16)],
    )
    return pl.pallas_call(
        _expert_kernel,
        grid_spec=grid_spec,
        out_shape=jax.ShapeDtypeStruct((2, n_rows, dq), U32),
        compiler_params=_cparams("arbitrary"),
        name="moe_experts",
    )(block_e, n_valid, xs, *weights)


def _combine_kernel(g_ref, wts_ref, h2_ref, x_ref, mod_ref, sg_ref, su_ref, sd_ref, gf_ref, o_ref, *, final):
    parts = _unpack_planes(h2_ref[0], h2_ref[1])
    hid = _half_silu(_dot_quarters(parts, sg_ref)) * _dot_quarters(parts, su_ref)
    shared = _dot(hid.astype(BF16), sd_ref[...])
    wts = wts_ref[...]
    routed = None
    for k in range(TOP_K):
        w = wts[:, k:k + 1]
        terms = [w * part for part in _unpack_planes(g_ref[k, 0], g_ref[k, 1])]
        routed = terms if routed is None else [r + t for r, t in zip(routed, terms)]
    f = shared + jnp.concatenate(routed, axis=1)
    m = mod_ref[:, 0]
    x2 = x_ref[...] + m[:, 5:6] * f.reshape(x_ref.shape)
    if final:
        x2 = x2 * lax.rsqrt(jnp.mean(x2 * x2, axis=-1, keepdims=True) + NORM_EPS) * gf_ref[...]
    o_ref[...] = x2


def _combine(g, wts, h2p, x1, modtab, lp, g_final, n_lat_tiles, tiles_used, final):
    b, _, d = x1.shape
    dq = d // 4
    weights = ((0.5 * lp['sh_w_gate']).astype(BF16), lp['sh_w_up'].astype(BF16), lp['sh_w_down'].astype(BF16),
               g_final.reshape(1, -1))
    full = lambda a: pl.BlockSpec(a.shape, lambda i: (0,) * a.ndim)
    group = _batch_group(b, MOE_BATCH_GROUP)
    rows = group * TM
    tok, mod = _tile_maps(b // group, n_lat_tiles)
    return pl.pallas_call(
        functools.partial(_combine_kernel, final=final),
        grid=(b * tiles_used * TM // rows,),
        in_specs=[pl.BlockSpec((TOP_K, 2, rows, dq), lambda i: (0, 0, i, 0)),
                  pl.BlockSpec((rows, TOPK_PAD), lambda i: (i, 0)),
                  pl.BlockSpec((2, rows, dq), lambda i: (0, i, 0)),
                  pl.BlockSpec((group, TM, d), tok), pl.BlockSpec((group, 1, 8, d), mod)]
                 + [full(w) for w in weights],
        out_specs=pl.BlockSpec((group, TM, d), tok),
        out_shape=jax.ShapeDtypeStruct((b, tiles_used * TM, d), F32),
        compiler_params=_cparams("parallel"),
        name="moe_combine",
    )(g, wts, h2p, x1, modtab, *weights)


def _sorted_rows_kernel(idx_ref, rank_ref, ps_ref, o_ref):
    expert = lax.broadcasted_iota(I32, (N_EXPERTS, idx_ref.shape[1]), 0)
    starts = ps_ref[...]
    idx = idx_ref[...]
    rows = [jnp.sum(jnp.where(idx[k:k + 1] == expert, starts, 0), axis=0, keepdims=True) for k in range(TOP_K)]
    pad = jnp.zeros((TOPK_PAD - TOP_K, idx.shape[1]), I32)
    o_ref[...] = jnp.concatenate(rows + [pad], axis=0) + rank_ref[...]


def _sorted_rows(idx, rank, p_starts):
    t = idx.shape[1]
    cols = TM * max(k for k in (8, 4, 2, 1) if (t // TM) % k == 0)
    per_tok = pl.BlockSpec((TOPK_PAD, cols), lambda i: (0, i))
    return pl.pallas_call(
        _sorted_rows_kernel,
        grid=(t // cols,),
        in_specs=[per_tok, per_tok, pl.BlockSpec((N_EXPERTS, 1), lambda i: (0, 0))],
        out_specs=per_tok,
        out_shape=jax.ShapeDtypeStruct((TOPK_PAD, t), I32),
        compiler_params=_cparams("parallel"),
        name="moe_sorted_rows",
    )(idx, rank, p_starts.reshape(-1, 1))


def _moe(x1, modtab, g2, lp, g_final, n_lat_tiles, final):
    b, nt, d = x1.shape
    tiles_used = n_lat_tiles if final else nt // TM
    t = b * tiles_used * TM
    h2p, idx, wts, rank, cnt = _route(x1, modtab, g2, lp, n_lat_tiles, tiles_used)
    counts = cnt[:, 0].astype(I32)
    padded = (counts + EXPERT_ROWS - 1) // EXPERT_ROWS * EXPERT_ROWS
    p_ends = jnp.cumsum(padded)
    p_starts = p_ends - padded
    n_blocks = (t * TOP_K + N_EXPERTS * (EXPERT_ROWS - 1)) // EXPERT_ROWS
    n_rows = n_blocks * EXPERT_ROWS
    dest = _sorted_rows(idx, rank, p_starts)[:TOP_K]
    plane_idx = jnp.concatenate([dest, dest + n_rows], axis=1)
    blk_start = jnp.arange(n_blocks, dtype=I32) * EXPERT_ROWS
    block_e = jnp.minimum(jnp.sum((p_ends[None, :] <= blk_start[:, None]).astype(I32), axis=1), N_EXPERTS - 1)
    n_valid = jnp.clip((p_starts + counts)[block_e] - blk_start, 0, EXPERT_ROWS).astype(I32)
    dq = d // 4
    xs = _sc_scatter_rows(h2p.reshape(2 * t, dq), plane_idx, 2 * n_rows).reshape(2, n_rows, dq)
    ys = _experts(xs, block_e, n_valid, lp['expert_stacks'], lp['layer'])
    g = _sc_gather_rows(ys.reshape(2 * n_rows, dq), plane_idx).reshape(TOP_K, 2, t, dq)
    return _combine(g, wts.T, h2p, x1, modtab, lp, g_final, n_lat_tiles, tiles_used, final)


def _layer(stream, nt, c, c_ctx, lp, consts, g_final, seq, final):
    b = stream[0].shape[0]
    n_lat_tiles = seq // TM
    rope, mats_lat, mats_ctx = consts
    modtab = _mod_table(c, c_ctx, *lp['mod_stacks'], lp['layer'])
    g1 = lp['g_norm1'].reshape(1, -1)
    g2 = lp['g_norm2'].reshape(1, -1)
    pw = _proj_weights(lp)
    q, k, v, nq, nk, nv, hy, lu, lg = _project(stream, nt, modtab, g1, pw, rope, n_lat_tiles)
    tiles_used = n_lat_tiles if final else nt // TM
    br_a = _mla_attention(q, k, v, seq, tiles_used)
    br_b = _na_attention(nq, nk, nv, _na_bias_tables(lp['na_rpb'], seq // GRID_W), seq, tiles_used)
    pre = _hyp_pre(hy, lp['hy_short_w'], lp['hy_short_b'], seq)
    lat_f, lat_bf, ctx_f, ctx_bf = pre
    br_c = _hyena_seq(mats_lat, lat_f, lat_bf, lp, b)
    if not final:
        br_c = jnp.concatenate([br_c, _hyena_seq(mats_ctx, ctx_f, ctx_bf, lp, b)], axis=1)
    br_d = _lru_mixer(lu, lg, lp, seq)
    x1 = _merge(stream, modtab, g1, (br_a, br_b, br_c, br_d), pw['w_gate'], lp, n_lat_tiles, tiles_used)
    return _moe(x1, modtab, g2, lp, g_final, n_lat_tiles, final)


_LAYER_KEYS = ('w_mod', 'b_mod', 'g_norm1', 'g_norm2', 'w_in', 'mla_g_q', 'mla_w_uq', 'mla_g_kv', 'mla_w_ukv',
               'na_rpb', 'hy_short_w', 'hy_short_b', 'hy_w1', 'hy_b1', 'hy_w2', 'hy_b2', 'hy_w3', 'hy_decay',
               'hy_bias', 'lru_conv_w', 'lru_conv_b', 'lru_wa', 'lru_ba', 'lru_wx', 'lru_bx', 'lru_lambda',
               'w_branch', 'w_out', 'router_w', 'router_bias', 'exp_w_gate', 'exp_w_up', 'exp_w_down',
               'sh_w_gate', 'sh_w_up', 'sh_w_down')


def kernel(x, c, ctx, c_ctx, w_mod, b_mod, g_norm1, g_norm2, w_in, mla_g_q, mla_w_uq, mla_g_kv, mla_w_ukv, na_rpb, hy_short_w, hy_short_b, hy_w1, hy_b1, hy_w2, hy_b2, hy_w3, hy_decay, hy_bias, lru_conv_w, lru_conv_b, lru_wa, lru_ba, lru_wx, lru_bx, lru_lambda, w_branch, w_out, router_w, router_bias, exp_w_gate, exp_w_up, exp_w_down, sh_w_gate, sh_w_up, sh_w_down, g_final):
    stacked = dict(zip(_LAYER_KEYS, (w_mod, b_mod, g_norm1, g_norm2, w_in, mla_g_q, mla_w_uq, mla_g_kv, mla_w_ukv,
                                     na_rpb, hy_short_w, hy_short_b, hy_w1, hy_b1, hy_w2, hy_b2, hy_w3, hy_decay,
                                     hy_bias, lru_conv_w, lru_conv_b, lru_wa, lru_ba, lru_wx, lru_bx, lru_lambda,
                                     w_branch, w_out, router_w, router_bias, exp_w_gate, exp_w_up, exp_w_down,
                                     sh_w_gate, sh_w_up, sh_w_down)))
    b, seq, d = x.shape
    ctx_len = ctx.shape[1]
    depth = w_mod.shape[0]
    assert seq % TM == 0 and ctx_len % TM == 0 and seq // GRID_W >= NA_KEY_ROWS + 1
    nt = seq + ctx_len
    stream = (x, ctx, 0)
    consts = (_rope_tables(seq, seq + ctx_len), _hyp_matrices(seq), _hyp_matrices(ctx_len))
    for i in range(depth):
        big = ('w_mod', 'b_mod', 'exp_w_gate', 'exp_w_up', 'exp_w_down')
        lp = {name: w[i] for name, w in stacked.items() if name not in big}
        lp['layer'] = i
        lp['mod_stacks'] = (w_mod, b_mod)
        lp['expert_stacks'] = (exp_w_gate, exp_w_up, exp_w_down)
        xa = _layer(stream, nt, c, c_ctx, lp, consts, g_final, seq, i == depth - 1)
        stream = (xa, xa, seq // TM)
    return xa
```

```python
import functools
import math

import numpy as np
import jax
import jax.numpy as jnp
from jax import lax
from jax.experimental import pallas as pl
from jax.experimental.pallas import tpu as pltpu
from jax.experimental.pallas import tpu_sc as plsc

F32 = jnp.float32
BF16 = jnp.bfloat16
I32 = jnp.int32

TM = 256
LANE = 128
GRID_W = 64
N_MOD = 6
NORM_EPS = 1e-6

MLA_HEADS, MLA_NOPE, MLA_ROPE, MLA_V = 4, 64, 32, 64
MLA_Q_RANK, MLA_KV_RANK = 192, 128
MLA_Q_PAD = 256
ROPE_THETA = 10000.0

NA_HEADS, NA_HEAD_DIM, NA_WIN_R, NA_WIN_C = 4, 64, 8, 16
NA_TILE_ROWS = TM // GRID_W
NA_KEY_ROWS = NA_TILE_ROWS + NA_WIN_R - 1
NA_KEYS = NA_KEY_ROWS * GRID_W

HY_WIDTH, HY_ORDER, HY_SHORT, HY_BANDS, HY_FFN = 256, 2, 3, 16, 64
HY_EMB = 2 * HY_BANDS + 1

LRU_WIDTH, LRU_BLOCKS, LRU_CONV, LRU_C = 256, 4, 4, 8.0
LRU_CHUNK = 256
LRU_HALO = 8

N_EXPERTS, TOP_K, EXPERT_HIDDEN, ROUTED_SCALE, MOE_BLOCK = 64, 6, 256, 2.5, 256
TOPK_PAD = 8

VMEM_LIMIT = 56 * 1024 * 1024


def _cparams(*sem):
    return pltpu.CompilerParams(dimension_semantics=sem, vmem_limit_bytes=VMEM_LIMIT)


def _dot(a, b):
    return jnp.dot(a, b, preferred_element_type=F32)


def _dot_nt(a, b):
    return lax.dot_general(a, b, (((1,), (1,)), ((), ())), preferred_element_type=F32)


def _sigmoid(x):
    return jax.nn.sigmoid(x)


def _half_silu(g):
    return g * (1.0 + jnp.tanh(g))


def _silu(x):
    return x * _sigmoid(x)


def _gelu_tanh(x):
    return 0.5 * x * (1.0 + jnp.tanh(math.sqrt(2.0 / math.pi) * (x + 0.044715 * (x * x * x))))


def _normmod(x, g, scale, shift):
    y = x * lax.rsqrt(jnp.mean(x * x, axis=-1, keepdims=True) + NORM_EPS) * g
    return y * (1.0 + scale) + shift


MOD_COLS = 1024


def _mod_kernel(c_ref, w_ref, b_ref, o_ref):
    s = _silu(c_ref[...])
    o_ref[...] = _dot(s.astype(BF16), w_ref[...].astype(BF16)) + b_ref[...]


def _mod_table(c, c_ctx, w_mod, b_mod, layer):
    b, d = c.shape
    rows = -(-(b + 1) // 16) * 16
    cc = jnp.zeros((rows, d), F32).at[:b].set(c).at[b].set(c_ctx)
    tn = MOD_COLS
    mod = pl.pallas_call(
        _mod_kernel,
        grid=(N_MOD * d // tn,),
        in_specs=[pl.BlockSpec((rows, d), lambda j: (0, 0)),
                  pl.BlockSpec((None, d, tn), lambda j: (layer, 0, j)),
                  pl.BlockSpec((None, 1, tn), lambda j: (layer, 0, j))],
        out_specs=pl.BlockSpec((rows, tn), lambda j: (0, j)),
        out_shape=jax.ShapeDtypeStruct((rows, N_MOD * d), F32),
        compiler_params=_cparams("arbitrary"),
        name="mod_vectors",
    )(cc, w_mod, b_mod[:, None, :])
    lat = mod[:b].reshape(b, N_MOD, d)
    ctx = jnp.broadcast_to(mod[b].reshape(1, N_MOD, d), (b, N_MOD, d))
    tab = jnp.stack([lat, ctx], axis=1)
    return jnp.pad(tab, ((0, 0), (0, 0), (0, 8 - N_MOD), (0, 0)))


_C_QLAT, _C_KVLAT, _C_KR, _C_KRR, _C_NA, _C_HY, _C_LU, _C_LG, _C_END = (
    0, 256, 384, 512, 640, 640 + 3 * NA_HEADS * LANE, 640 + 1536 + 768, 640 + 1536 + 1024, 640 + 1536 + 1280)


def _stream_tile(lat_ref, ctx_ref, n_lat_tiles):
    return jnp.where(pl.program_id(1) < n_lat_tiles, lat_ref[...], ctx_ref[...])


def _stream_specs(stream, group, n_lat_tiles):
    lat_src, ctx_src, ctx_tile0 = stream
    d = lat_src.shape[-1]
    return [pl.BlockSpec((group, TM, d), lambda bb, i: (bb, jnp.minimum(i, n_lat_tiles - 1), 0)),
            pl.BlockSpec((group, TM, d), lambda bb, i: (bb, ctx_tile0 + jnp.maximum(i - n_lat_tiles, 0), 0))]


def _proj_kernel(xl_ref, xc_ref, mod_ref, g1_ref, w1_ref, gq_ref, gkv_ref, wuq_ref, wuqr_ref, wk_ref, wv_ref,
                 cos_ref, sin_ref, q_o, k_o, v_o, nq_o, nk_o, nv_o, hy_o, lu_o, lg_o, *, n_lat_tiles):
    group, tm, d = xl_ref.shape
    m = mod_ref[:, 0]
    x = _stream_tile(xl_ref, xc_ref, n_lat_tiles)
    h = _normmod(x, g1_ref[...], m[:, 1:2], m[:, 0:1]).reshape(group * tm, d)
    z = _dot(h.astype(BF16), w1_ref[...])
    qlat = z[:, _C_QLAT:_C_KVLAT]
    kvlat = z[:, _C_KVLAT:_C_KR]
    kr = z[:, _C_KR:_C_KRR]
    krr = z[:, _C_KRR:_C_NA]
    qn = qlat * lax.rsqrt(jnp.sum(qlat * qlat, axis=-1, keepdims=True) * (1.0 / MLA_Q_RANK) + NORM_EPS) * gq_ref[...]
    kvn = kvlat * lax.rsqrt(jnp.mean(kvlat * kvlat, axis=-1, keepdims=True) + NORM_EPS) * gkv_ref[...]
    qn = qn.astype(BF16)
    kvn = kvn.astype(BF16)
    q = _dot(qn, wuq_ref[...])
    qr = _dot(qn, wuqr_ref[...])
    kk = _dot(kvn, wk_ref[...])
    vv = _dot(kvn, wv_ref[...])
    cos = jnp.concatenate([cos_ref[...]] * group, axis=0)
    sin = jnp.concatenate([sin_ref[...]] * group, axis=0)
    krope = kr * cos + krr * sin
    den_lane = lax.broadcasted_iota(I32, cos.shape, 1) == SOFTMAX_DEN_LANE
    def put_heads(ref, hd, val):
        for g in range(group):
            ref[g, hd] = val[g * tm:(g + 1) * tm].astype(BF16)

    def put_pairs(ref, hd, val):
        off = (hd % 2) * LANE
        for g in range(group):
            ref[g, hd // 2, :, off:off + LANE] = val[g * tm:(g + 1) * tm].astype(BF16)

    for hd in range(MLA_HEADS):
        sl = slice(hd * LANE, (hd + 1) * LANE)
        put_heads(q_o, hd, q[:, sl] * cos + qr[:, sl] * sin)
        put_heads(k_o, hd, kk[:, sl] + krope)
        put_heads(v_o, hd, jnp.where(den_lane, 1.0, vv[:, sl]))
    for hd in range(NA_HEADS):
        for which, ref in enumerate((nq_o, nk_o, nv_o)):
            lo = _C_NA + (which * NA_HEADS + hd) * LANE
            blk = z[:, lo:lo + LANE]
            if which == 2:
                put_pairs(ref, hd, jnp.where(den_lane, 1.0, blk))
            else:
                put_heads(ref, hd, blk)
    hy_o[...] = z[:, _C_HY:_C_LU].reshape(group, tm, _C_LU - _C_HY)
    lu_o[...] = z[:, _C_LU:_C_LG].reshape(group, tm, _C_LG - _C_LU)
    lg_o[...] = z[:, _C_LG:_C_END].reshape(group, tm, _C_END - _C_LG)


def _pad_to(a, n, axis):
    pad = [(0, 0)] * a.ndim
    pad[axis] = (0, n - a.shape[axis])
    return jnp.pad(a, pad)


def _rot_cols(w):
    half = w.shape[-1] // 2
    return jnp.concatenate([-w[..., half:], w[..., :half]], axis=-1)


def _head_blocks(cols_per_head):
    out = []
    for pieces in cols_per_head:
        k = pieces[0][0].shape[0]
        blk = jnp.zeros((k, LANE), F32)
        for arr, off in pieces:
            blk = blk.at[:, off:off + arr.shape[1]].set(arr)
        out.append(blk)
    return jnp.concatenate(out, axis=1)


def _proj_weights(lp):
    w_in = lp['w_in']
    d = w_in.shape[0]
    o = 0
    parts = {}
    for name, n in (('q', MLA_Q_RANK), ('kv', MLA_KV_RANK), ('kr', MLA_ROPE), ('na', 3 * NA_HEADS * NA_HEAD_DIM),
                    ('hy', 3 * HY_WIDTH), ('lu', LRU_WIDTH), ('lg', LRU_WIDTH), ('gt', 4 * d)):
        parts[name] = w_in[:, o:o + n]
        o += n
    zeros = lambda n: jnp.zeros((d, n), F32)
    kr_blk = jnp.concatenate([zeros(MLA_NOPE), parts['kr'], zeros(LANE - MLA_NOPE - MLA_ROPE)], axis=1)
    krr_blk = jnp.concatenate([zeros(MLA_NOPE), _rot_cols(parts['kr']), zeros(LANE - MLA_NOPE - MLA_ROPE)], axis=1)
    na_scale = NA_HEAD_DIM ** -0.5
    na_cols = []
    for which in range(3):
        for hd in range(NA_HEADS):
            lo = (which * NA_HEADS + hd) * NA_HEAD_DIM
            blk = parts['na'][:, lo:lo + NA_HEAD_DIM] * (na_scale if which == 0 else 1.0)
            na_cols.append(_pad_to(blk, LANE, 1))
    w1 = jnp.concatenate([_pad_to(parts['q'], MLA_Q_PAD, 1), parts['kv'], kr_blk, krr_blk] + na_cols
                         + [parts['hy'], parts['lu'], parts['lg']], axis=1).astype(BF16)
    mla_scale = (MLA_NOPE + MLA_ROPE) ** -0.5
    wuq = _pad_to(lp['mla_w_uq'], MLA_Q_PAD, 0) * mla_scale
    dq = MLA_NOPE + MLA_ROPE
    wuq_main = _head_blocks([[(wuq[:, hd * dq:hd * dq + dq], 0)] for hd in range(MLA_HEADS)])
    wuq_rot = _head_blocks([[(_rot_cols(wuq[:, hd * dq + MLA_NOPE:hd * dq + dq]), MLA_NOPE)]
                            for hd in range(MLA_HEADS)])
    dkv = MLA_NOPE + MLA_V
    wukv = lp['mla_w_ukv']
    wk = _head_blocks([[(wukv[:, hd * dkv:hd * dkv + MLA_NOPE], 0)] for hd in range(MLA_HEADS)])
    wv = _head_blocks([[(wukv[:, hd * dkv + MLA_NOPE:hd * dkv + dkv], 0)] for hd in range(MLA_HEADS)])
    gq = _pad_to(lp['mla_g_q'].reshape(1, -1), MLA_Q_PAD, 1)
    gkv = lp['mla_g_kv'].reshape(1, -1)
    return dict(w1=w1, w_gate=(0.5 * parts['gt']).astype(BF16), gq=gq, gkv=gkv, wuq=wuq_main.astype(BF16),
                wuq_rot=wuq_rot.astype(BF16), wk=wk.astype(BF16), wv=wv.astype(BF16))


def _rope_tables(seq, n_tok):
    t = jnp.arange(seq, dtype=I32)
    row = (t // GRID_W).astype(F32)
    col = (t % GRID_W).astype(F32)
    n_axis = MLA_ROPE // 4
    inv_freq = ROPE_THETA ** (-jnp.arange(n_axis, dtype=F32) / n_axis)
    ang = jnp.concatenate([row[:, None] * inv_freq, col[:, None] * inv_freq], axis=-1)
    cos = jnp.concatenate([jnp.cos(ang), jnp.cos(ang)], axis=-1)
    sin = jnp.concatenate([jnp.sin(ang), jnp.sin(ang)], axis=-1)
    cos_t = jnp.ones((n_tok, LANE), F32).at[:seq, MLA_NOPE:MLA_NOPE + MLA_ROPE].set(cos)
    sin_t = jnp.zeros((n_tok, LANE), F32).at[:seq, MLA_NOPE:MLA_NOPE + MLA_ROPE].set(sin)
    return cos_t, sin_t


def _batch_group(b, cap=2):
    return max(g for g in (4, 2, 1) if g <= cap and b % g == 0)


MOE_BATCH_GROUP = 4


def _kind_map(n_lat_tiles):
    return lambda b, i: (b, jnp.where(i >= n_lat_tiles, 1, 0), 0, 0)


def _project(stream, nt, modtab, g1, pw, rope, n_lat_tiles):
    b, _, d = stream[0].shape
    cos_t, sin_t = rope
    group = _batch_group(b)
    full = lambda a: pl.BlockSpec(a.shape, lambda bb, i: (0,) * a.ndim)
    head_out = lambda: pl.BlockSpec((group, MLA_HEADS, TM, LANE), lambda bb, i: (bb, 0, i, 0))
    tok_out = lambda n: pl.BlockSpec((group, TM, n), lambda bb, i: (bb, i, 0))
    head_shape = jax.ShapeDtypeStruct((b, MLA_HEADS, nt, LANE), BF16)
    pair_out = lambda: pl.BlockSpec((group, MLA_HEADS // 2, TM, 2 * LANE), lambda bb, i: (bb, 0, i, 0))
    pair_shape = jax.ShapeDtypeStruct((b, MLA_HEADS // 2, nt, 2 * LANE), BF16)
    tok_shape = lambda n: jax.ShapeDtypeStruct((b, nt, n), F32)
    weights = (g1, pw['w1'], pw['gq'], pw['gkv'], pw['wuq'], pw['wuq_rot'], pw['wk'], pw['wv'])
    return pl.pallas_call(
        functools.partial(_proj_kernel, n_lat_tiles=n_lat_tiles),
        grid=(b // group, nt // TM),
        in_specs=_stream_specs(stream, group, n_lat_tiles)
                 + [pl.BlockSpec((group, 1, 8, d), _kind_map(n_lat_tiles))]
                 + [full(w) for w in weights]
                 + [pl.BlockSpec((TM, LANE), lambda bb, i: (i, 0))] * 2,
        out_specs=[head_out(), head_out(), head_out(), head_out(), head_out(), pair_out(),
                   tok_out(3 * HY_WIDTH), tok_out(LRU_WIDTH), tok_out(LRU_WIDTH)],
        out_shape=[head_shape, head_shape, head_shape, head_shape, head_shape, pair_shape,
                   tok_shape(3 * HY_WIDTH), tok_shape(LRU_WIDTH), tok_shape(LRU_WIDTH)],
        compiler_params=_cparams("parallel", "parallel"),
        name="input_projection",
    )(stream[0], stream[1], modtab, *weights, cos_t, sin_t)


SOFTMAX_DEN_LANE = 64


def _softmax_pv(parts, lane_off=0):
    m = None
    for s, _ in parts:
        mm = jnp.max(s, axis=-1, keepdims=True)
        m = mm if m is None else jnp.maximum(m, mm)
    acc = None
    for s, v in parts:
        o = _dot(jnp.exp(s - m).astype(BF16), v)
        acc = o if acc is None else acc + o
    den = lane_off + SOFTMAX_DEN_LANE
    return acc[:, lane_off:lane_off + LANE] / acc[:, den:den + 1]


def _mla_kernel(q_ref, k_ref, v_ref, o_ref, *, seq, n_lat_tiles):
    i = pl.program_id(1)
    nt = k_ref.shape[2]

    def attend(lo, hi):
        for hd in range(MLA_HEADS):
            s = _dot_nt(q_ref[0, hd], k_ref[0, hd, lo:hi, :])
            o = _softmax_pv([(s, v_ref[0, hd, lo:hi, :])])
            o_ref[0, :, hd * LANE:(hd + 1) * LANE] = o.astype(BF16)

    @pl.when(i < n_lat_tiles)
    def _():
        attend(0, nt)

    @pl.when(i >= n_lat_tiles)
    def _():
        attend(seq, nt)


def _mla_attention(q, k, v, seq, tiles_used):
    b, h, nt, _ = q.shape
    kv_spec = pl.BlockSpec((1, h, nt, LANE), lambda bb, i: (bb, 0, 0, 0))
    return pl.pallas_call(
        functools.partial(_mla_kernel, seq=seq, n_lat_tiles=seq // TM),
        grid=(b, tiles_used),
        in_specs=[pl.BlockSpec((1, h, TM, LANE), lambda bb, i: (bb, 0, i, 0)), kv_spec, kv_spec],
        out_specs=pl.BlockSpec((1, TM, h * LANE), lambda bb, i: (bb, i, 0)),
        out_shape=jax.ShapeDtypeStruct((b, tiles_used * TM, h * LANE), BF16),
        compiler_params=_cparams("parallel", "arbitrary"),
        name="mla_attention",
    )(q, k, v)


def _na_bias_tables(rpb, rows):
    n_blk = rows // NA_TILE_ROWS
    col = np.arange(GRID_W)
    c0 = np.clip(col - NA_WIN_C // 2, 0, GRID_W - NA_WIN_C)
    in_win = (col[None, :] >= c0[:, None]) & (col[None, :] < c0[:, None] + NA_WIN_C)
    dc = np.clip(col[None, :] - col[:, None], 1 - NA_WIN_C, NA_WIN_C - 1) + NA_WIN_C - 1
    rpb = rpb.astype(F32)
    tables = []
    for j in (0, 1, n_blk - 1):
        w0 = min(max(NA_TILE_ROWS * j - NA_WIN_R // 2, 0), rows - NA_KEY_ROWS)
        r = NA_TILE_ROWS * j + np.arange(NA_TILE_ROWS)
        kr = w0 + np.arange(NA_KEY_ROWS)
        r0 = np.clip(r - NA_WIN_R // 2, 0, rows - NA_WIN_R)
        row_ok = (kr[None, :] >= r0[:, None]) & (kr[None, :] < r0[:, None] + NA_WIN_R)
        dr = np.clip(kr[None, :] - r[:, None] + NA_WIN_R - 1, 0, 2 * NA_WIN_R - 2)
        oh_r = jnp.asarray(np.eye(2 * NA_WIN_R - 1, dtype=np.float32)[dr.reshape(-1)])
        oh_c = jnp.asarray(np.eye(2 * NA_WIN_C - 1, dtype=np.float32)[dc.reshape(-1)])
        bias = jnp.einsum('ar,hrc,bc->hab', oh_r, rpb, oh_c, precision=lax.Precision.HIGHEST)
        bias = bias.reshape(NA_HEADS, NA_TILE_ROWS, NA_KEY_ROWS, GRID_W, GRID_W)
        mask = row_ok[:, :, None, None] & in_win[None, None, :, :]
        bias = jnp.where(jnp.asarray(mask)[None], bias, -jnp.inf)
        tables.append(bias.transpose(0, 1, 3, 2, 4).reshape(NA_HEADS, TM, NA_KEYS))
    return jnp.stack(tables)


def _na_kernel(q_ref, k_ref, v_ref, bias_ref, o_ref, *, seq, n_lat_tiles):
    i = pl.program_id(1)
    nt = k_ref.shape[2]
    rows = seq // GRID_W

    @pl.when(i < n_lat_tiles)
    def _():
        w0 = jnp.clip(NA_TILE_ROWS * i - NA_WIN_R // 2, 0, rows - NA_KEY_ROWS)
        start = pl.multiple_of(w0 * GRID_W, GRID_W)
        for hd in range(NA_HEADS):
            q = q_ref[0, hd]
            s_loc = _dot_nt(q, k_ref[0, hd, pl.ds(start, NA_KEYS), :]) + bias_ref[0, hd]
            s_ctx = _dot_nt(q, k_ref[0, hd, seq:nt, :])
            o = _softmax_pv([(s_loc, v_ref[0, hd // 2, pl.ds(start, NA_KEYS), :]),
                             (s_ctx, v_ref[0, hd // 2, seq:nt, :])], (hd % 2) * LANE)
            o_ref[0, :, hd * LANE:(hd + 1) * LANE] = o.astype(BF16)

    @pl.when(i >= n_lat_tiles)
    def _():
        for hd in range(NA_HEADS):
            s = _dot_nt(q_ref[0, hd], k_ref[0, hd, seq:nt, :])
            o = _softmax_pv([(s, v_ref[0, hd // 2, seq:nt, :])], (hd % 2) * LANE)
            o_ref[0, :, hd * LANE:(hd + 1) * LANE] = o.astype(BF16)


def _na_attention(q, k, v, bias, seq, tiles_used):
    b, h, nt, _ = q.shape
    n_lat = seq // TM
    kv_spec = pl.BlockSpec((1, h, nt, LANE), lambda bb, i: (bb, 0, 0, 0))
    cfg = lambda bb, i: (jnp.where(i == 0, 0, jnp.where(i >= n_lat - 1, 2, 1)), 0, 0, 0)
    return pl.pallas_call(
        functools.partial(_na_kernel, seq=seq, n_lat_tiles=n_lat),
        grid=(b, tiles_used),
        in_specs=[pl.BlockSpec((1, h, TM, LANE), lambda bb, i: (bb, 0, i, 0)), kv_spec,
                  pl.BlockSpec((1, h // 2, nt, 2 * LANE), lambda bb, i: (bb, 0, 0, 0)),
                  pl.BlockSpec((1, h, TM, NA_KEYS), cfg)],
        out_specs=pl.BlockSpec((1, TM, h * LANE), lambda bb, i: (bb, i, 0)),
        out_shape=jax.ShapeDtypeStruct((b, tiles_used * TM, h * LANE), BF16),
        compiler_params=_cparams("parallel", "arbitrary"),
        name="neighbourhood_attention",
    )(q, k, v, bias)


def _hp_dot(a, b):
    return jnp.dot(a, b, preferred_element_type=F32, precision=lax.Precision.HIGHEST)


def _hy_filter_kernel(f_ref, w1_ref, b1_ref, w2_ref, b2_ref, w3_ref, dec_ref, o_ref, obf_ref):
    f = f_ref[...]
    h = jnp.sin(_hp_dot(f, w1_ref[...]) + b1_ref[...])
    h = jnp.sin(_hp_dot(h, w2_ref[...]) + b2_ref[...])
    h = _hp_dot(h, w3_ref[...])
    h = h * jnp.exp(-f[:, 0:1] * jnp.abs(dec_ref[...]))
    o_ref[...] = h
    obf_ref[...] = h.astype(BF16)


def _hy_pos_features(length):
    t = jnp.linspace(0.0, 1.0, length, dtype=F32)[:, None]
    w = 2.0 * math.pi * jnp.arange(length, dtype=F32)[:, None] / length
    f = jnp.linspace(1e-4, HY_BANDS - 1, HY_BANDS, dtype=F32)[None, :]
    z = w * f
    return jnp.concatenate([t, jnp.cos(z), -jnp.sin(z)], axis=-1)


DFT_ROW_STEP = 64
DFT_ROWS = 256
DFT_FWD_COLS = 1024
DFT_INV_COLS = 512
FILTER_ROWS = 512


def _col_block(nc, cap):
    return min(nc, cap)


def _hyp_pre_kernel(z_ref, w_ref, b_ref, lat_o, lat_bf_o, ctx_o, ctx_bf_o, *, seq, ctx_len):
    w = w_ref[...]
    bias = b_ref[...]
    for lo, length, o_ref, obf_ref in ((0, seq, lat_o, lat_bf_o), (seq, ctx_len, ctx_o, ctx_bf_o)):
        h = length // 2
        even = z_ref[0, pl.ds(lo, h, stride=2), :]
        odd = z_ref[0, pl.ds(lo + 1, h, stride=2), :]
        row = lax.broadcasted_iota(I32, even.shape, 0)
        odd_prev = jnp.where(row == 0, 0.0, pltpu.roll(odd, 1, axis=0))
        even_next = jnp.where(row == h - 1, 0.0, pltpu.roll(even, h - 1, axis=0))
        y_even = bias + w[0:1] * odd_prev + w[1:2] * even + w[2:3] * odd
        y_odd = bias + w[0:1] * even + w[1:2] * odd + w[2:3] * even_next
        for r, y in enumerate((y_even, y_odd)):
            o_ref[0, r] = y
            obf_ref[0, r] = y.astype(BF16)


def _hyp_pre(hy, w, bvec, seq):
    b, nt, _ = hy.shape
    ctx_len = nt - seq
    wpad = _pad_to(w, 8, 0)
    per_plane = HY_WIDTH // LANE
    out_specs, out_shape = [], []
    for length in (seq, ctx_len):
        for dt in (F32, BF16):
            out_specs.append(pl.BlockSpec((1, 2, length // 2, LANE),
                                          lambda bb, g: (g // per_plane, 0, 0, bb * per_plane + g % per_plane)))
            out_shape.append(jax.ShapeDtypeStruct((3, 2, length // 2, b * HY_WIDTH), dt))
    return pl.pallas_call(
        functools.partial(_hyp_pre_kernel, seq=seq, ctx_len=ctx_len),
        grid=(b, 3 * per_plane),
        in_specs=[pl.BlockSpec((1, nt, LANE), lambda bb, g: (bb, 0, g)),
                  pl.BlockSpec((8, LANE), lambda bb, g: (0, g)),
                  pl.BlockSpec((1, LANE), lambda bb, g: (0, g))],
        out_specs=out_specs,
        out_shape=out_shape,
        compiler_params=_cparams("parallel", "parallel"),
        name="hyena_short_conv",
    )(hy, wpad, bvec.reshape(1, -1))


def _hyp_filters(length, lp):
    feats = _hy_pos_features(length)
    feats = feats.reshape(length // 2, 2, HY_EMB).transpose(1, 0, 2).reshape(length, HY_EMB)
    n_out = HY_ORDER * 2 * HY_WIDTH
    tl = min(length // 2, FILTER_ROWS)
    full = lambda a: pl.BlockSpec(a.shape, lambda i: (0,) * a.ndim)
    args = (lp['hy_w1'], lp['hy_b1'].reshape(1, -1), lp['hy_w2'], lp['hy_b2'].reshape(1, -1), lp['hy_w3'],
            lp['hy_decay'].reshape(1, n_out))
    filt, filt_bf = pl.pallas_call(
        _hy_filter_kernel,
        grid=(length // tl,),
        in_specs=[pl.BlockSpec((tl, HY_EMB), lambda i: (i, 0))] + [full(a) for a in args],
        out_specs=[pl.BlockSpec((tl, n_out), lambda i: (i, 0))] * 2,
        out_shape=[jax.ShapeDtypeStruct((length, n_out), F32), jax.ShapeDtypeStruct((length, n_out), BF16)],
        compiler_params=_cparams("parallel"),
        name="hyena_filter_mlp",
    )(feats, *args)
    return filt.reshape(2, length // 2, n_out), filt_bf.reshape(2, length // 2, n_out)


def _hyp_matrix_kernel(c1_ref, s1_ref, c2_ref, s2_ref, ck_ref, sk_ref, ckr_ref, skr_ref,
                       ce_o, se_o, co_o, so_o, set_o, cot_o, sot_o):
    c1, s1 = c1_ref[0], s1_ref[0]
    c2, s2 = c2_ref[...], s2_ref[...]
    cos = c1 * c2 - s1 * s2
    sin = s1 * c2 + c1 * s2
    row = lax.broadcasted_iota(I32, cos.shape, 0)
    col = lax.broadcasted_iota(I32, cos.shape, 1)
    first_row = (row + pl.program_id(0) * DFT_ROW_STEP) == 0
    alt_col = jnp.where(col % 2 == 0, 1.0, -1.0)
    alt_row = jnp.where(row % 2 == 0, 1.0, -1.0)
    ck, sk = ck_ref[...], sk_ref[...]
    ce_o[...] = cos.astype(BF16)
    se_o[...] = jnp.where(first_row, alt_col, -sin).astype(BF16)
    co_o[...] = (cos * ck - sin * sk).astype(BF16)
    so_o[...] = jnp.where(first_row, alt_col, -(sin * ck + cos * sk)).astype(BF16)
    ckr, skr = ckr_ref[...], skr_ref[...]
    set_o[...] = jnp.where(col == 0, alt_row, -sin).astype(BF16)
    cot_o[...] = (cos * ckr - sin * skr).astype(BF16)
    sot_o[...] = jnp.where(col == 0, alt_row, -(sin * ckr + cos * skr)).astype(BF16)


def _hyp_matrices(length):
    h = length // 2
    step = DFT_ROW_STEP
    m = jnp.arange(h, dtype=I32)

    def trig(kv):
        ang = ((kv[:, None] * m[None, :]) % length).astype(F32) * (2.0 * math.pi / length)
        return jnp.cos(ang), jnp.sin(ang)

    c1, s1 = trig(jnp.arange(h // step, dtype=I32) * step)
    c2, s2 = trig(jnp.arange(step, dtype=I32))
    half_angle = m.astype(F32) * (math.pi / length)
    ck, sk = jnp.cos(half_angle), jnp.sin(half_angle)
    coarse = pl.BlockSpec((1, 1, h), lambda j: (j, 0, 0))
    fine = pl.BlockSpec((step, h), lambda j: (0, 0))
    per_row = pl.BlockSpec((step, 1), lambda j: (j, 0))
    per_col = pl.BlockSpec((1, h), lambda j: (0, 0))
    out = pl.BlockSpec((step, h), lambda j: (j, 0))
    ce, se, co, so, se_t, co_t, so_t = pl.pallas_call(
        _hyp_matrix_kernel,
        grid=(h // step,),
        in_specs=[coarse, coarse, fine, fine, per_row, per_row, per_col, per_col],
        out_specs=[out] * 7,
        out_shape=[jax.ShapeDtypeStruct((h, h), BF16)] * 7,
        compiler_params=_cparams("parallel"),
        name="dft_matrices",
    )(c1[:, None, :], s1[:, None, :], c2, s2, ck[:, None], sk[:, None], ck[None, :], sk[None, :])
    return dict(fwd=(ce, se, co, so), inv=(ce, se_t, co_t, so_t))


def _hyp_fwd_kernel(ce_ref, se_ref, co_ref, so_ref, xe_ref, xo_ref, *rest, with_taps):
    xe, xo = xe_ref[...], xo_ref[...]
    a_re, a_im = _dot(ce_ref[...], xe), _dot(se_ref[...], xe)
    b_re, b_im = _dot(co_ref[...], xo), _dot(so_ref[...], xo)
    if not with_taps:
        for ref, val in zip(rest, (a_re, a_im, b_re, b_im)):
            ref[...] = val
        return
    t1re_ref, t1im_ref, t2re_ref, t2im_ref, sp_ref, gere_o, geim_o, gore_o, goim_o = rest
    t1re, t1im, t2re, t2im = t1re_ref[...], t1im_ref[...], t2re_ref[...], t2im_ref[...]
    sp = sp_ref[...]
    first_block = pl.program_id(1) == 0
    row8 = lax.broadcasted_iota(I32, (8, HY_WIDTH), 0)
    for bb in range(xe.shape[1] // HY_WIDTH):
        sl = slice(bb * HY_WIDTH, (bb + 1) * HY_WIDTH)
        are, aim, bre, bim = a_re[:, sl], a_im[:, sl], b_re[:, sl], b_im[:, sl]
        u1re, u1im = are + bre, aim + bim
        u2re, u2im = are - bre, bim - aim
        z1re, z1im = u1re * t1re - u1im * t1im, u1re * t1im + u1im * t1re
        z2re, z2im = u2re * t2re - u2im * t2im, u2re * t2im + u2im * t2re
        gere_o[:, sl] = (z1re + z2re).astype(BF16)
        geim_o[:, sl] = (z1im - z2im).astype(BF16)
        gore_o[:, sl] = (z1re - z2re).astype(BF16)
        goim_o[:, sl] = (z1im + z2im).astype(BF16)

        @pl.when(first_block)
        def _():
            u0, ul = u1re[0:8], u2re[0:8]
            a_s, b_s = aim[0:8], bim[0:8]
            dc, ny, mre, mim = sp[0:1], sp[1:2], sp[2:3], sp[3:4]
            first = row8 == 0
            gere_o[0:8, sl] = jnp.where(first, u0 * dc + ul * ny, (z1re + z2re)[0:8]).astype(BF16)
            gore_o[0:8, sl] = jnp.where(first, u0 * dc - ul * ny, (z1re - z2re)[0:8]).astype(BF16)
            geim_o[0:8, sl] = jnp.where(first, a_s * mre + b_s * mim, (z1im - z2im)[0:8]).astype(BF16)
            goim_o[0:8, sl] = jnp.where(first, b_s * mre - a_s * mim, (z1im + z2im)[0:8]).astype(BF16)


def _hyp_fwd(mats, x, plane, taps=None):
    _, _, h, nc = x.shape
    tk = min(h, DFT_ROWS)
    cb = _col_block(nc, DFT_FWD_COLS)
    grid = (nc // cb, h // tk)
    m_spec = pl.BlockSpec((tk, h), lambda c, j: (j, 0))
    x_spec = lambda r: pl.BlockSpec((None, None, h, cb), lambda c, j: (plane, r, 0, c))
    o_spec = pl.BlockSpec((tk, cb), lambda c, j: (j, c))
    if taps is None:
        return pl.pallas_call(
            functools.partial(_hyp_fwd_kernel, with_taps=False),
            grid=grid, in_specs=[m_spec] * 4 + [x_spec(0), x_spec(1)], out_specs=[o_spec] * 4,
            out_shape=[jax.ShapeDtypeStruct((h, nc), F32)] * 4,
            compiler_params=_cparams("parallel", "arbitrary"),
            name="hyena_dft_filters",
        )(*mats['fwd'], x, x)
    t_spec = pl.BlockSpec((tk, HY_WIDTH), lambda c, j: (j, 0))
    sp_spec = pl.BlockSpec((8, HY_WIDTH), lambda c, j: (0, 0))
    return pl.pallas_call(
        functools.partial(_hyp_fwd_kernel, with_taps=True),
        grid=grid, in_specs=[m_spec] * 4 + [x_spec(0), x_spec(1)] + [t_spec] * 4 + [sp_spec],
        out_specs=[o_spec] * 4,
        out_shape=[jax.ShapeDtypeStruct((h, nc), BF16)] * 4,
        compiler_params=_cparams("parallel", "arbitrary"),
        name="hyena_dft_forward",
    )(*mats['fwd'], x, x, *taps)


def _hyp_inv_kernel(ce_ref, set_ref, cot_ref, sot_ref, gere_ref, geim_ref, gore_ref, goim_ref,
                    gate_ref, prev_ref, bias_ref, *outs, last):
    conv = (_dot(ce_ref[...], gere_ref[...]) + _dot(set_ref[...], geim_ref[...]),
            _dot(cot_ref[...], gore_ref[...]) + _dot(sot_ref[...], goim_ref[...]))
    bias = bias_ref[...]
    for r in range(2):
        y = gate_ref[r] * (conv[r] + prev_ref[r] * bias)
        if last:
            (tok_o,) = outs
            for bb in range(y.shape[1] // HY_WIDTH):
                tok_o[bb, :, r * HY_WIDTH:(r + 1) * HY_WIDTH] = y[:, bb * HY_WIDTH:(bb + 1) * HY_WIDTH].astype(BF16)
        else:
            y_o, ybf_o = outs
            y_o[r] = y
            ybf_o[r] = y.astype(BF16)


def _hyp_inv(mats, g, gate, gate_plane, prev, prev_plane, bias_row, last):
    h, nc = g[0].shape
    tm = min(h, DFT_ROWS)
    cb = _col_block(nc, DFT_INV_COLS)
    grid = (nc // cb, h // tm)
    m_spec = pl.BlockSpec((tm, h), lambda c, i: (i, 0))
    g_spec = pl.BlockSpec((h, cb), lambda c, i: (0, c))
    e_spec = lambda plane: pl.BlockSpec((None, 2, tm, cb), lambda c, i: (plane, 0, i, c))
    b_spec = pl.BlockSpec((1, cb), lambda c, i: (0, c))
    if last:
        out_specs = [pl.BlockSpec((cb // HY_WIDTH, tm, 2 * HY_WIDTH), lambda c, i: (c, i, 0))]
        out_shape = [jax.ShapeDtypeStruct((nc // HY_WIDTH, h, 2 * HY_WIDTH), BF16)]
    else:
        out_specs = [e_spec(0), e_spec(0)]
        out_shape = [jax.ShapeDtypeStruct((1, 2, h, nc), F32), jax.ShapeDtypeStruct((1, 2, h, nc), BF16)]
    return pl.pallas_call(
        functools.partial(_hyp_inv_kernel, last=last),
        grid=grid, in_specs=[m_spec] * 4 + [g_spec] * 4 + [e_spec(gate_plane), e_spec(prev_plane), b_spec],
        out_specs=out_specs, out_shape=out_shape,
        compiler_params=_cparams("parallel", "arbitrary"),
        name="hyena_dft_inverse",
    )(*mats['inv'], *g, gate, prev, bias_row)


def _hyp_tap_tables(spec, filt, length):
    a_re, a_im, b_re, b_im = spec
    w = HY_WIDTH
    inv_n = 1.0 / (2 * length)
    tables = []
    for o in range(HY_ORDER):
        f_sl = slice((2 * o) * w, (2 * o + 1) * w)
        r_sl = slice((2 * o + 1) * w, (2 * o + 2) * w)
        hb0 = filt[0, 0:1, r_sl]
        f1re = (a_re + b_re)[:, f_sl] + (a_re + b_re)[:, r_sl] - hb0
        f1im = (a_im + b_im)[:, f_sl] - (a_im + b_im)[:, r_sl]
        f2re = (a_re - b_re)[:, f_sl] + (a_re - b_re)[:, r_sl] - hb0
        f2im = (b_im - a_im)[:, f_sl] - (b_im - a_im)[:, r_sl]
        dc = f1re[0:1]
        ny = f2re[0:1]
        mid_re = a_im[0:1, f_sl] + a_im[0:1, r_sl] - hb0
        mid_im = -b_im[0:1, f_sl] + b_im[0:1, r_sl]
        sp = jnp.concatenate([dc * inv_n, ny * inv_n, mid_re * (2 * inv_n), mid_im * (2 * inv_n),
                              jnp.zeros((4, w), F32)], axis=0)
        tables.append((f1re * (2 * inv_n), f1im * (2 * inv_n), f2re * (2 * inv_n), f2im * (2 * inv_n), sp))
    return tables


def _hyena_seq(mats, vx, vx_bf, lp, n_batch):
    h = vx.shape[2]
    length = 2 * h
    filt, filt_bf = _hyp_filters(length, lp)
    spec = _hyp_fwd(mats, filt_bf[None], 0)
    tables = _hyp_tap_tables(spec, filt, length)
    bias = lp['hy_bias'].astype(F32)
    y, y_bf = vx, vx_bf
    for o in range(HY_ORDER):
        g = _hyp_fwd(mats, y_bf, 0, tables[o])
        bias_row = jnp.tile(bias[o][None, :], (1, n_batch))
        last = o == HY_ORDER - 1
        res = _hyp_inv(mats, g, vx, o + 1, y, 0, bias_row, last)
        if last:
            return res[0].reshape(n_batch, length, HY_WIDTH)
        y, y_bf = res


def _lru_kernel(u_ref, g_ref, cw_ref, cb_ref, wa_ref, ba_ref, wx_ref, bx_ref, lam_ref, o_ref,
                pad_ref, y_ref, *, seq, ctx_len):
    tc = LRU_CHUNK
    halo = LRU_HALO
    width = LRU_WIDTH
    lat_off = halo
    ctx_off = 2 * halo + seq
    zero = jnp.zeros((halo, width), F32)
    pad_ref[0:halo, :] = zero
    pad_ref[lat_off:lat_off + seq, :] = u_ref[0, 0:seq, :]
    pad_ref[lat_off + seq:ctx_off, :] = zero
    pad_ref[ctx_off:ctx_off + ctx_len, :] = u_ref[0, seq:seq + ctx_len, :]
    pad_ref[ctx_off + ctx_len:ctx_off + ctx_len + halo, :] = zero
    row = lax.broadcasted_iota(I32, (tc, width), 0)
    n_win = tc + 2 * halo

    def chunk(pad_off, y_off, s, carry, d):
        wstart = pl.multiple_of(pad_off + s - halo, 8)
        win = pad_ref[pl.ds(wstart, n_win), :]
        cw = cw_ref[d]
        xc = cb_ref[d]
        for k in range(LRU_CONV):
            shift = (LRU_CONV - 1 - k) if d == 0 else -k
            rolled = win if shift == 0 else pltpu.roll(win, shift % n_win, axis=0)
            xc = xc + cw[k:k + 1] * rolled[halo:halo + tc]
        xb = xc.astype(BF16)
        r = _sigmoid(_dot(xb, wa_ref[d]) + ba_ref[d])
        gi = _sigmoid(_dot(xb, wx_ref[d]) + bx_ref[d])
        lam = lam_ref[d]
        softplus = jnp.maximum(-lam, 0.0) + jnp.log1p(jnp.exp(-jnp.abs(lam)))
        log_a = -LRU_C * r * softplus
        a = jnp.exp(log_a)
        bt = jnp.sqrt(-jnp.tanh(log_a) * (a * a + 1.0)) * (gi * xc)
        sft = 1
        while sft < tc:
            if d == 0:
                keep = row >= sft
                a_s = jnp.where(keep, pltpu.roll(a, sft, axis=0), 1.0)
                b_s = jnp.where(keep, pltpu.roll(bt, sft, axis=0), 0.0)
            else:
                keep = row < tc - sft
                a_s = jnp.where(keep, pltpu.roll(a, tc - sft, axis=0), 1.0)
                b_s = jnp.where(keep, pltpu.roll(bt, tc - sft, axis=0), 0.0)
            bt = a * b_s + bt
            a = a * a_s
            sft *= 2
        h = a * carry + bt
        yo = pl.multiple_of(y_off + s, 8)
        if d == 0:
            y_ref[pl.ds(yo, tc), :] = h
            return h[tc - 1:tc]
        y_ref[pl.ds(yo, tc), :] = y_ref[pl.ds(yo, tc), :] + h
        return h[0:1]

    n_lat = seq // tc
    n_ctx = ctx_len // tc
    for d in range(2):
        carry = jnp.zeros((1, width), F32)
        order = range(n_ctx) if d == 0 else range(n_ctx - 1, -1, -1)
        for c in order:
            carry = chunk(ctx_off, seq, c * tc, carry, d)

        def body(j, cr, d=d):
            jj = j if d == 0 else n_lat - 1 - j
            return chunk(lat_off, 0, jj * tc, cr, d)

        lax.fori_loop(0, n_lat, body, carry)
    o_ref[0] = (y_ref[...] * _gelu_tanh(g_ref[0])).astype(BF16)


def _block_diag(w):
    nd, nb, c, _ = w.shape
    out = jnp.zeros((nd, nb * c, nb * c), w.dtype)
    for n in range(nb):
        out = out.at[:, n * c:(n + 1) * c, n * c:(n + 1) * c].set(w[:, n])
    return out


def _lru_mixer(lu, lg, lp, seq):
    b, nt, w = lu.shape
    ctx_len = nt - seq
    row3 = lambda a: a.reshape(2, 1, w)
    args = (_pad_to(lp['lru_conv_w'], 8, 1), row3(lp['lru_conv_b']), _block_diag(lp['lru_wa']).astype(BF16),
            row3(lp['lru_ba']), _block_diag(lp['lru_wx']).astype(BF16), row3(lp['lru_bx']), row3(lp['lru_lambda']))
    full = lambda a: pl.BlockSpec(a.shape, lambda bb: (0,) * a.ndim)
    tok = pl.BlockSpec((1, nt, w), lambda bb: (bb, 0, 0))
    return pl.pallas_call(
        functools.partial(_lru_kernel, seq=seq, ctx_len=ctx_len),
        grid=(b,),
        in_specs=[tok, tok] + [full(a) for a in args],
        out_specs=tok,
        out_shape=jax.ShapeDtypeStruct((b, nt, w), BF16),
        scratch_shapes=[pltpu.VMEM((nt + 3 * LRU_HALO, w), F32), pltpu.VMEM((nt, w), F32)],
        compiler_params=_cparams("parallel"),
        name="rglru_scan",
    )(lu, lg, *args)


def _merge_kernel(xl_ref, xc_ref, mod_ref, g1_ref, a_ref, b_ref, c_ref, d_ref, wg_ref, wa_ref, wb_ref, wc_ref,
                  wd_ref, wo_ref, o_ref, *, n_lat_tiles):
    group, tm, dm = xl_ref.shape
    m = mod_ref[:, 0]
    x = _stream_tile(xl_ref, xc_ref, n_lat_tiles)
    h = _normmod(x, g1_ref[...], m[:, 1:2], m[:, 0:1]).reshape(group * tm, dm).astype(BF16)
    acc = None
    for k, (br, w) in enumerate(((a_ref, wa_ref), (b_ref, wb_ref), (c_ref, wc_ref), (d_ref, wd_ref))):
        term = (1.0 + jnp.tanh(_dot(h, wg_ref[:, k * dm:(k + 1) * dm]))) * _dot(
            br[...].reshape(group * tm, br.shape[-1]), w[...])
        acc = term if acc is None else acc + term
    y = _dot(acc.astype(BF16), wo_ref[...])
    o_ref[...] = x + m[:, 2:3] * y.reshape(group, tm, dm)


def _merge(stream, modtab, g1, branches, w_gate, lp, n_lat_tiles, tiles_used):
    b, _, d = stream[0].shape
    wbr = lp['w_branch']
    head_rows = lambda w, dv: jnp.concatenate(
        [_pad_to(w[hd * dv:(hd + 1) * dv], LANE, 0) for hd in range(4)], axis=0)
    wbr = 0.5 * wbr
    weights = (head_rows(wbr[0], MLA_V).astype(BF16), head_rows(wbr[1], NA_HEAD_DIM).astype(BF16),
               wbr[2].astype(BF16), wbr[3].astype(BF16), lp['w_out'].astype(BF16))
    group = _batch_group(b)
    full = lambda a: pl.BlockSpec(a.shape, lambda bb, i: (0,) * a.ndim)
    tok = lambda n: pl.BlockSpec((group, TM, n), lambda bb, i: (bb, i, 0))
    return pl.pallas_call(
        functools.partial(_merge_kernel, n_lat_tiles=n_lat_tiles),
        grid=(b // group, tiles_used),
        in_specs=_stream_specs(stream, group, n_lat_tiles)
                 + [pl.BlockSpec((group, 1, 8, d), _kind_map(n_lat_tiles)), full(g1)]
                 + [tok(br.shape[-1]) for br in branches] + [full(w_gate)] + [full(w) for w in weights],
        out_specs=tok(d),
        out_shape=jax.ShapeDtypeStruct((b, tiles_used * TM, d), F32),
        compiler_params=_cparams("parallel", "parallel"),
        name="merge_branches",
    )(stream[0], stream[1], modtab, g1, *branches, w_gate, *weights)


U32 = jnp.uint32
EXPERT_ROWS = 896


def _pack_pair(x):
    n = x.shape[-1] // 2
    hi = lax.bitcast_convert_type(x[:, :n].astype(BF16).astype(F32), U32)
    lo = lax.bitcast_convert_type(x[:, n:].astype(BF16).astype(F32), U32)
    return hi | (lo >> 16)


def _unpack_pair(p):
    hi = lax.bitcast_convert_type(p & jnp.uint32(0xFFFF0000), F32)
    lo = lax.bitcast_convert_type(p << 16, F32)
    return hi, lo


def _router_kernel(x_ref, mod_ref, g2_ref, rw_ref, rb_ref, tri_ref, h2_o, idx_o, wts_o, rank_o, cnt_o, carry):
    i = pl.program_id(0)

    @pl.when(i == 0)
    def _():
        carry[...] = jnp.zeros_like(carry)

    group, tm, d = x_ref.shape
    m = mod_ref[:, 0]
    h2 = _normmod(x_ref[...], g2_ref[...], m[:, 4:5], m[:, 3:4]).reshape(group * tm, d)
    half = h2.shape[-1] // 2
    h2_o[0] = _pack_pair(h2[:, :half])
    h2_o[1] = _pack_pair(h2[:, half:])
    logits = lax.dot_general(rw_ref[...], h2, (((1,), (1,)), ((), ())), preferred_element_type=F32,
                             precision=lax.Precision.HIGHEST)
    scores = _sigmoid(logits)
    biased = scores + rb_ref[...]
    expert = lax.broadcasted_iota(I32, scores.shape, 0)
    picks = []
    onehot_all = jnp.zeros(scores.shape, F32)
    for _ in range(TOP_K):
        best = jnp.max(biased, axis=0, keepdims=True)
        arg = jnp.min(jnp.where(biased == best, expert, N_EXPERTS), axis=0, keepdims=True)
        hit = expert == arg
        sel = jnp.sum(jnp.where(hit, scores, 0.0), axis=0, keepdims=True)
        biased = jnp.where(hit, -jnp.inf, biased)
        onehot_all = onehot_all + jnp.where(hit, 1.0, 0.0)
        picks.append((arg, hit, sel))
    total = picks[0][2]
    for _, _, sel in picks[1:]:
        total = total + sel
    earlier = _dot(onehot_all.astype(BF16), tri_ref[...]) + carry[...]
    pad_rows = TOPK_PAD - TOP_K
    ranks = [jnp.sum(jnp.where(hit, earlier, 0.0), axis=0, keepdims=True).astype(I32) for _, hit, _ in picks]
    scale = ROUTED_SCALE / total
    n_tok = scores.shape[1]
    idx_o[...] = jnp.concatenate([arg for arg, _, _ in picks] + [jnp.zeros((pad_rows, n_tok), I32)], axis=0)
    wts_o[...] = jnp.concatenate([sel * scale for _, _, sel in picks] + [jnp.zeros((pad_rows, n_tok), F32)], axis=0)
    rank_o[...] = jnp.concatenate(ranks + [jnp.zeros((pad_rows, n_tok), I32)], axis=0)
    carry[...] = carry[...] + jnp.sum(onehot_all, axis=1, keepdims=True)
    cnt_o[...] = carry[...]


def _tile_maps(n_groups, n_lat_tiles, tile0):
    tok = lambda f: (f % n_groups, tile0 + f // n_groups, 0)
    mod = lambda f: (f % n_groups, jnp.where(tile0 + f // n_groups >= n_lat_tiles, 1, 0), 0, 0)
    return tok, mod


def _route(x1, modtab, g2, lp, n_lat_tiles, tile0, n_tiles):
    b, _, d = x1.shape
    group = _batch_group(b, MOE_BATCH_GROUP)
    rows = group * TM
    t = b * n_tiles * TM
    rw = lp['router_w'].T
    rb = lp['router_bias'].reshape(-1, 1)
    tri = (np.arange(rows)[:, None] < np.arange(rows)[None, :]).astype(np.float32)
    tri = jnp.asarray(tri, BF16)
    per_tok = lambda: pl.BlockSpec((TOPK_PAD, rows), lambda i: (0, i))
    full = lambda a: pl.BlockSpec(a.shape, lambda i: (0,) * a.ndim)
    tok, mod = _tile_maps(b // group, n_lat_tiles, tile0)
    return pl.pallas_call(
        _router_kernel,
        grid=(t // rows,),
        in_specs=[pl.BlockSpec((group, TM, d), tok), pl.BlockSpec((group, 1, 8, d), mod),
                  full(g2), full(rw), full(rb), full(tri)],
        out_specs=[pl.BlockSpec((2, rows, d // 4), lambda i: (0, i, 0)), per_tok(), per_tok(), per_tok(),
                   pl.BlockSpec((N_EXPERTS, 1), lambda i: (0, 0))],
        out_shape=[jax.ShapeDtypeStruct((2, t, d // 4), U32), jax.ShapeDtypeStruct((TOPK_PAD, t), I32),
                   jax.ShapeDtypeStruct((TOPK_PAD, t), F32), jax.ShapeDtypeStruct((TOPK_PAD, t), I32),
                   jax.ShapeDtypeStruct((N_EXPERTS, 1), F32)],
        scratch_shapes=[pltpu.VMEM((N_EXPERTS, 1), F32)],
        compiler_params=_cparams("arbitrary"),
        name="moe_router",
    )(x1, modtab, g2, rw, rb, tri)


SC_WINDOW = 128


def _sc_mesh():
    return plsc.VectorSubcoreMesh(core_axis_name="c", subcore_axis_name="s")


def _sc_scatter_rows(src, idx, n_out):
    n, width = src.shape
    k_rep = idx.shape[0]
    half = n // SC_WINDOW // 2

    @functools.partial(pl.kernel, out_type=jax.ShapeDtypeStruct((n_out, width), src.dtype), mesh=_sc_mesh(),
                       scratch_types=[], name="moe_dispatch_sc")
    def scatter(src_hbm, idx_hbm, out_hbm):
        def body(x_vmem, *i_vmems):
            for i_vmem in i_vmems:
                pltpu.sync_copy(x_vmem, out_hbm.at[i_vmem.at[0]])

        pltpu.emit_pipeline(
            body,
            grid=(2, half),
            in_specs=[pl.BlockSpec((SC_WINDOW, width), lambda a, i: (a * half + i, 0))]
                     + [pl.BlockSpec((1, SC_WINDOW), lambda a, i, k=k: (k, a * half + i)) for k in range(k_rep)],
            out_specs=[],
            core_axis_name=("c", "s"),
            dimension_semantics=(pltpu.PARALLEL, pltpu.PARALLEL),
        )(src_hbm, *([idx_hbm] * k_rep))

    return scatter(src, idx)


def _sc_gather_rows(src, idx):
    k_rep, n = idx.shape
    width = src.shape[1]
    n_win = n // SC_WINDOW

    @functools.partial(pl.kernel, out_type=jax.ShapeDtypeStruct((k_rep * n, width), src.dtype), mesh=_sc_mesh(),
                       scratch_types=[], name="moe_gather_sc")
    def gather(src_hbm, idx_hbm, out_hbm):
        def body(i_vmem, o_vmem):
            pltpu.sync_copy(src_hbm.at[i_vmem.at[0]], o_vmem)

        pltpu.emit_pipeline(
            body,
            grid=(k_rep, n_win),
            in_specs=[pl.BlockSpec((1, SC_WINDOW), lambda k, i: (k, i))],
            out_specs=[pl.BlockSpec((SC_WINDOW, width), lambda k, i: (k * n_win + i, 0))],
            core_axis_name=("c", "s"),
            dimension_semantics=(pltpu.PARALLEL, pltpu.PARALLEL),
        )(idx_hbm, out_hbm)

    return gather(src, idx)


def _unpack_planes(p0, p1):
    return _unpack_pair(p0) + _unpack_pair(p1)


def _dot_quarters(parts, w_ref):
    q = parts[0].shape[-1]
    acc = None
    for j, part in enumerate(parts):
        term = _dot(part.astype(BF16), w_ref[j * q:(j + 1) * q, :])
        acc = term if acc is None else acc + term
    return acc


def _expert_kernel(be_ref, nv_ref, xs_ref, wg_ref, wu_ref, wd_ref, ys_o, wg_s, wu_s, wd_s):
    i = pl.program_id(0)
    prev = be_ref[jnp.maximum(i - 1, 0)]

    @pl.when((i == 0) | (be_ref[i] != prev))
    def _():
        wg_s[...] = (0.5 * wg_ref[0]).astype(BF16)
        wu_s[...] = wu_ref[0].astype(BF16)
        wd_s[...] = wd_ref[0].astype(BF16)

    @pl.when(nv_ref[i] > 0)
    def _():
        keep = lax.broadcasted_iota(I32, xs_ref.shape[1:], 0) < nv_ref[i]
        parts = _unpack_planes(jnp.where(keep, xs_ref[0], jnp.uint32(0)), jnp.where(keep, xs_ref[1], jnp.uint32(0)))
        hid = _half_silu(_dot_quarters(parts, wg_s)) * _dot_quarters(parts, wu_s)
        y = _dot(hid.astype(BF16), wd_s[...])
        half = y.shape[-1] // 2
        ys_o[0] = _pack_pair(y[:, :half])
        ys_o[1] = _pack_pair(y[:, half:])

    @pl.when(nv_ref[i] <= 0)
    def _():
        ys_o[...] = jnp.zeros_like(ys_o)


def _experts(xs, block_e, n_valid, weights, layer):
    _, n_rows, dq = xs.shape
    d = 4 * dq
    n_blocks = n_rows // EXPERT_ROWS
    hid = EXPERT_HIDDEN
    grid_spec = pltpu.PrefetchScalarGridSpec(
        num_scalar_prefetch=2,
        grid=(n_blocks,),
        in_specs=[pl.BlockSpec((2, EXPERT_ROWS, dq), lambda i, be, nv: (0, i, 0)),
                  pl.BlockSpec((None, 1, d, hid), lambda i, be, nv: (layer, be[i], 0, 0)),
                  pl.BlockSpec((None, 1, d, hid), lambda i, be, nv: (layer, be[i], 0, 0)),
                  pl.BlockSpec((None, 1, hid, d), lambda i, be, nv: (layer, be[i], 0, 0))],
        out_specs=pl.BlockSpec((2, EXPERT_ROWS, dq), lambda i, be, nv: (0, i, 0)),
        scratch_shapes=[pltpu.VMEM((d, hid), BF16), pltpu.VMEM((d, hid), BF16), pltpu.VMEM((hid, d), BF16)],
    )
    return pl.pallas_call(
        _expert_kernel,
        grid_spec=grid_spec,
        out_shape=jax.ShapeDtypeStruct((2, n_rows, dq), U32),
        compiler_params=_cparams("arbitrary"),
        name="moe_experts",
    )(block_e, n_valid, xs, *weights)


def _combine_kernel(g_ref, wts_ref, h2_ref, x_ref, mod_ref, sg_ref, su_ref, sd_ref, gf_ref, *rest, final):
    o_ref = rest[-1]
    parts = _unpack_planes(h2_ref[0], h2_ref[1])
    hid = _half_silu(_dot_quarters(parts, sg_ref)) * _dot_quarters(parts, su_ref)
    shared = _dot(hid.astype(BF16), sd_ref[...])
    wts = wts_ref[...]
    routed = None
    for k in range(TOP_K):
        w = wts[:, k:k + 1]
        terms = [w * part for part in _unpack_planes(g_ref[k, 0], g_ref[k, 1])]
        routed = terms if routed is None else [r + t for r, t in zip(routed, terms)]
    f = shared + jnp.concatenate(routed, axis=1)
    m = mod_ref[:, 0]
    x2 = x_ref[...] + m[:, 5:6] * f.reshape(x_ref.shape)
    if final:
        x2 = x2 * lax.rsqrt(jnp.mean(x2 * x2, axis=-1, keepdims=True) + NORM_EPS) * gf_ref[...]
    o_ref[...] = x2


def _combine(g, wts, h2p, x1, modtab, lp, g_final, n_lat_tiles, tiles_total, tile0, n_tiles, final, prev):
    b, _, d = x1.shape
    dq = d // 4
    weights = ((0.5 * lp['sh_w_gate']).astype(BF16), lp['sh_w_up'].astype(BF16), lp['sh_w_down'].astype(BF16),
               g_final.reshape(1, -1))
    full = lambda a: pl.BlockSpec(a.shape, lambda i: (0,) * a.ndim)
    group = _batch_group(b, MOE_BATCH_GROUP)
    rows = group * TM
    tok, mod = _tile_maps(b // group, n_lat_tiles, tile0)
    inputs = (g, wts, h2p, x1, modtab, *weights) + (() if prev is None else (prev,))
    return pl.pallas_call(
        functools.partial(_combine_kernel, final=final),
        grid=(b * n_tiles * TM // rows,),
        in_specs=[pl.BlockSpec((TOP_K, 2, rows, dq), lambda i: (0, 0, i, 0)),
                  pl.BlockSpec((rows, TOPK_PAD), lambda i: (i, 0)),
                  pl.BlockSpec((2, rows, dq), lambda i: (0, i, 0)),
                  pl.BlockSpec((group, TM, d), tok), pl.BlockSpec((group, 1, 8, d), mod)]
                 + [full(w) for w in weights]
                 + ([] if prev is None else [pl.BlockSpec(memory_space=pl.ANY)]),
        out_specs=pl.BlockSpec((group, TM, d), tok),
        out_shape=jax.ShapeDtypeStruct((b, tiles_total * TM, d), F32),
        input_output_aliases={} if prev is None else {len(inputs) - 1: 0},
        compiler_params=_cparams("parallel"),
        name="moe_combine",
    )(*inputs)


def _sorted_rows_kernel(idx_ref, rank_ref, ps_ref, o_ref):
    expert = lax.broadcasted_iota(I32, (N_EXPERTS, idx_ref.shape[1]), 0)
    starts = ps_ref[...]
    idx = idx_ref[...]
    rows = [jnp.sum(jnp.where(idx[k:k + 1] == expert, starts, 0), axis=0, keepdims=True) for k in range(TOP_K)]
    pad = jnp.zeros((TOPK_PAD - TOP_K, idx.shape[1]), I32)
    o_ref[...] = jnp.concatenate(rows + [pad], axis=0) + rank_ref[...]


def _sorted_rows(idx, rank, p_starts):
    t = idx.shape[1]
    cols = TM * max(k for k in (8, 4, 2, 1) if (t // TM) % k == 0)
    per_tok = pl.BlockSpec((TOPK_PAD, cols), lambda i: (0, i))
    return pl.pallas_call(
        _sorted_rows_kernel,
        grid=(t // cols,),
        in_specs=[per_tok, per_tok, pl.BlockSpec((N_EXPERTS, 1), lambda i: (0, 0))],
        out_specs=per_tok,
        out_shape=jax.ShapeDtypeStruct((TOPK_PAD, t), I32),
        compiler_params=_cparams("parallel"),
        name="moe_sorted_rows",
    )(idx, rank, p_starts.reshape(-1, 1))


def _moe(x1, modtab, g2, lp, g_final, n_lat_tiles, final):
    tiles_used = n_lat_tiles if final else x1.shape[1] // TM
    first = (tiles_used + 1) // 2
    out = None
    for tile0, n_tiles in ((0, first), (first, tiles_used - first)):
        out = _moe_run(x1, modtab, g2, lp, g_final, n_lat_tiles, tiles_used, tile0, n_tiles, final, out)
    return out


def _moe_run(x1, modtab, g2, lp, g_final, n_lat_tiles, tiles_total, tile0, n_tiles, final, prev):
    b, _, d = x1.shape
    t = b * n_tiles * TM
    h2p, idx, wts, rank, cnt = _route(x1, modtab, g2, lp, n_lat_tiles, tile0, n_tiles)
    counts = cnt[:, 0].astype(I32)
    padded = (counts + EXPERT_ROWS - 1) // EXPERT_ROWS * EXPERT_ROWS
    p_ends = jnp.cumsum(padded)
    p_starts = p_ends - padded
    n_blocks = (t * TOP_K + N_EXPERTS * (EXPERT_ROWS - 1)) // EXPERT_ROWS
    n_rows = n_blocks * EXPERT_ROWS
    dest = _sorted_rows(idx, rank, p_starts)[:TOP_K]
    plane_idx = jnp.concatenate([dest, dest + n_rows], axis=1)
    blk_start = jnp.arange(n_blocks, dtype=I32) * EXPERT_ROWS
    block_e = jnp.minimum(jnp.sum((p_ends[None, :] <= blk_start[:, None]).astype(I32), axis=1), N_EXPERTS - 1)
    n_valid = jnp.clip((p_starts + counts)[block_e] - blk_start, 0, EXPERT_ROWS).astype(I32)
    dq = d // 4
    xs = _sc_scatter_rows(h2p.reshape(2 * t, dq), plane_idx, 2 * n_rows).reshape(2, n_rows, dq)
    ys = _experts(xs, block_e, n_valid, lp['expert_stacks'], lp['layer'])
    g = _sc_gather_rows(ys.reshape(2 * n_rows, dq), plane_idx).reshape(TOP_K, 2, t, dq)
    return _combine(g, wts.T, h2p, x1, modtab, lp, g_final, n_lat_tiles, tiles_total, tile0, n_tiles, final, prev)


def _layer(stream, nt, c, c_ctx, lp, consts, g_final, seq, final):
    b = stream[0].shape[0]
    n_lat_tiles = seq // TM
    rope, mats_lat, mats_ctx = consts
    modtab = _mod_table(c, c_ctx, *lp['mod_stacks'], lp['layer'])
    g1 = lp['g_norm1'].reshape(1, -1)
    g2 = lp['g_norm2'].reshape(1, -1)
    pw = _proj_weights(lp)
    q, k, v, nq, nk, nv, hy, lu, lg = _project(stream, nt, modtab, g1, pw, rope, n_lat_tiles)
    tiles_used = n_lat_tiles if final else nt // TM
    br_a = _mla_attention(q, k, v, seq, tiles_used)
    br_b = _na_attention(nq, nk, nv, _na_bias_tables(lp['na_rpb'], seq // GRID_W), seq, tiles_used)
    pre = _hyp_pre(hy, lp['hy_short_w'], lp['hy_short_b'], seq)
    lat_f, lat_bf, ctx_f, ctx_bf = pre
    br_c = _hyena_seq(mats_lat, lat_f, lat_bf, lp, b)
    if not final:
        br_c = jnp.concatenate([br_c, _hyena_seq(mats_ctx, ctx_f, ctx_bf, lp, b)], axis=1)
    br_d = _lru_mixer(lu, lg, lp, seq)
    x1 = _merge(stream, modtab, g1, (br_a, br_b, br_c, br_d), pw['w_gate'], lp, n_lat_tiles, tiles_used)
    return _moe(x1, modtab, g2, lp, g_final, n_lat_tiles, final)


_LAYER_KEYS = ('w_mod', 'b_mod', 'g_norm1', 'g_norm2', 'w_in', 'mla_g_q', 'mla_w_uq', 'mla_g_kv', 'mla_w_ukv',
               'na_rpb', 'hy_short_w', 'hy_short_b', 'hy_w1', 'hy_b1', 'hy_w2', 'hy_b2', 'hy_w3', 'hy_decay',
               'hy_bias', 'lru_conv_w', 'lru_conv_b', 'lru_wa', 'lru_ba', 'lru_wx', 'lru_bx', 'lru_lambda',
               'w_branch', 'w_out', 'router_w', 'router_bias', 'exp_w_gate', 'exp_w_up', 'exp_w_down',
               'sh_w_gate', 'sh_w_up', 'sh_w_down')


def kernel(x, c, ctx, c_ctx, w_mod, b_mod, g_norm1, g_norm2, w_in, mla_g_q, mla_w_uq, mla_g_kv, mla_w_ukv, na_rpb, hy_short_w, hy_short_b, hy_w1, hy_b1, hy_w2, hy_b2, hy_w3, hy_decay, hy_bias, lru_conv_w, lru_conv_b, lru_wa, lru_ba, lru_wx, lru_bx, lru_lambda, w_branch, w_out, router_w, router_bias, exp_w_gate, exp_w_up, exp_w_down, sh_w_gate, sh_w_up, sh_w_down, g_final):
    stacked = dict(zip(_LAYER_KEYS, (w_mod, b_mod, g_norm1, g_norm2, w_in, mla_g_q, mla_w_uq, mla_g_kv, mla_w_ukv,
                                     na_rpb, hy_short_w, hy_short_b, hy_w1, hy_b1, hy_w2, hy_b2, hy_w3, hy_decay,
                                     hy_bias, lru_conv_w, lru_conv_b, lru_wa, lru_ba, lru_wx, lru_bx, lru_lambda,
                                     w_branch, w_out, router_w, router_bias, exp_w_gate, exp_w_up, exp_w_down,
                                     sh_w_gate, sh_w_up, sh_w_down)))
    b, seq, d = x.shape
    ctx_len = ctx.shape[1]
    depth = w_mod.shape[0]
    assert seq % TM == 0 and ctx_len % TM == 0 and seq // GRID_W >= NA_KEY_ROWS + 1
    nt = seq + ctx_len
    stream = (x, ctx, 0)
    consts = (_rope_tables(seq, seq + ctx_len), _hyp_matrices(seq), _hyp_matrices(ctx_len))
    for i in range(depth):
        big = ('w_mod', 'b_mod', 'exp_w_gate', 'exp_w_up', 'exp_w_down')
        lp = {name: w[i] for name, w in stacked.items() if name not in big}
        lp['layer'] = i
        lp['mod_stacks'] = (w_mod, b_mod)
        lp['expert_stacks'] = (exp_w_gate, exp_w_up, exp_w_down)
        xa = _layer(stream, nt, c, c_ctx, lp, consts, g_final, seq, i == depth - 1)
        stream = (xa, xa, seq // TM)
    return xa
```

```python
import functools
import math

import numpy as np
import jax
import jax.numpy as jnp
from jax import lax
from jax.experimental import pallas as pl
from jax.experimental.pallas import tpu as pltpu
from jax.experimental.pallas import tpu_sc as plsc

F32 = jnp.float32
BF16 = jnp.bfloat16
I32 = jnp.int32

TM = 256
LANE = 128
GRID_W = 64
N_MOD = 6
NORM_EPS = 1e-6

MLA_HEADS, MLA_NOPE, MLA_ROPE, MLA_V = 4, 64, 32, 64
MLA_Q_RANK, MLA_KV_RANK = 192, 128
MLA_Q_PAD = 256
ROPE_THETA = 10000.0

NA_HEADS, NA_HEAD_DIM, NA_WIN_R, NA_WIN_C = 4, 64, 8, 16
NA_TILE_ROWS = TM // GRID_W
NA_KEY_ROWS = NA_TILE_ROWS + NA_WIN_R - 1
NA_KEYS = NA_KEY_ROWS * GRID_W

HY_WIDTH, HY_ORDER, HY_SHORT, HY_BANDS, HY_FFN = 256, 2, 3, 16, 64
HY_EMB = 2 * HY_BANDS + 1

LRU_WIDTH, LRU_BLOCKS, LRU_CONV, LRU_C = 256, 4, 4, 8.0
LRU_CHUNK = 256
LRU_HALO = 8

N_EXPERTS, TOP_K, EXPERT_HIDDEN, ROUTED_SCALE, MOE_BLOCK = 64, 6, 256, 2.5, 256
TOPK_PAD = 8

VMEM_LIMIT = 56 * 1024 * 1024


def _cparams(*sem):
    return pltpu.CompilerParams(dimension_semantics=sem, vmem_limit_bytes=VMEM_LIMIT)


def _dot(a, b):
    return jnp.dot(a, b, preferred_element_type=F32)


def _dot_nt(a, b):
    return lax.dot_general(a, b, (((1,), (1,)), ((), ())), preferred_element_type=F32)


def _sigmoid(x):
    return jax.nn.sigmoid(x)


def _half_silu(g):
    return g * (1.0 + jnp.tanh(g))


def _silu(x):
    return x * _sigmoid(x)


def _gelu_tanh(x):
    return 0.5 * x * (1.0 + jnp.tanh(math.sqrt(2.0 / math.pi) * (x + 0.044715 * (x * x * x))))


def _normmod(x, g, scale, shift):
    y = x * lax.rsqrt(jnp.mean(x * x, axis=-1, keepdims=True) + NORM_EPS) * g
    return y * (1.0 + scale) + shift


MOD_COLS = 1024


def _mod_kernel(c_ref, w_ref, b_ref, o_ref):
    s = _silu(c_ref[...])
    o_ref[...] = _dot(s.astype(BF16), w_ref[...].astype(BF16)) + b_ref[...]


def _mod_table(c, c_ctx, w_mod, b_mod, layer):
    b, d = c.shape
    rows = -(-(b + 1) // 16) * 16
    cc = jnp.zeros((rows, d), F32).at[:b].set(c).at[b].set(c_ctx)
    tn = MOD_COLS
    mod = pl.pallas_call(
        _mod_kernel,
        grid=(N_MOD * d // tn,),
        in_specs=[pl.BlockSpec((rows, d), lambda j: (0, 0)),
                  pl.BlockSpec((None, d, tn), lambda j: (layer, 0, j)),
                  pl.BlockSpec((None, 1, tn), lambda j: (layer, 0, j))],
        out_specs=pl.BlockSpec((rows, tn), lambda j: (0, j)),
        out_shape=jax.ShapeDtypeStruct((rows, N_MOD * d), F32),
        compiler_params=_cparams("arbitrary"),
        name="mod_vectors",
    )(cc, w_mod, b_mod[:, None, :])
    lat = mod[:b].reshape(b, N_MOD, d)
    ctx = jnp.broadcast_to(mod[b].reshape(1, N_MOD, d), (b, N_MOD, d))
    tab = jnp.stack([lat, ctx], axis=1)
    return jnp.pad(tab, ((0, 0), (0, 0), (0, 8 - N_MOD), (0, 0)))


_C_QLAT, _C_KVLAT, _C_KR, _C_KRR, _C_NA, _C_HY, _C_LU, _C_LG, _C_END = (
    0, 256, 384, 512, 640, 640 + 3 * NA_HEADS * LANE, 640 + 1536 + 768, 640 + 1536 + 1024, 640 + 1536 + 1280)


def _stream_tile(lat_ref, ctx_ref, n_lat_tiles):
    return jnp.where(pl.program_id(1) < n_lat_tiles, lat_ref[...], ctx_ref[...])


def _stream_specs(stream, group, n_lat_tiles):
    lat_src, ctx_src, ctx_tile0 = stream
    d = lat_src.shape[-1]
    return [pl.BlockSpec((group, TM, d), lambda bb, i: (bb, jnp.minimum(i, n_lat_tiles - 1), 0)),
            pl.BlockSpec((group, TM, d), lambda bb, i: (bb, ctx_tile0 + jnp.maximum(i - n_lat_tiles, 0), 0))]


def _proj_kernel(xl_ref, xc_ref, mod_ref, g1_ref, w1_ref, gq_ref, gkv_ref, wuq_ref, wuqr_ref, wk_ref, wv_ref,
                 cos_ref, sin_ref, q_o, k_o, v_o, nq_o, nk_o, nv_o, hy_o, lu_o, lg_o, *, n_lat_tiles):
    group, tm, d = xl_ref.shape
    m = mod_ref[:, 0]
    x = _stream_tile(xl_ref, xc_ref, n_lat_tiles)
    h = _normmod(x, g1_ref[...], m[:, 1:2], m[:, 0:1]).reshape(group * tm, d)
    z = _dot(h.astype(BF16), w1_ref[...])
    qlat = z[:, _C_QLAT:_C_KVLAT]
    kvlat = z[:, _C_KVLAT:_C_KR]
    kr = z[:, _C_KR:_C_KRR]
    krr = z[:, _C_KRR:_C_NA]
    qn = qlat * lax.rsqrt(jnp.sum(qlat * qlat, axis=-1, keepdims=True) * (1.0 / MLA_Q_RANK) + NORM_EPS) * gq_ref[...]
    kvn = kvlat * lax.rsqrt(jnp.mean(kvlat * kvlat, axis=-1, keepdims=True) + NORM_EPS) * gkv_ref[...]
    qn = qn.astype(BF16)
    kvn = kvn.astype(BF16)
    q = _dot(qn, wuq_ref[...])
    qr = _dot(qn, wuqr_ref[...])
    kk = _dot(kvn, wk_ref[...])
    vv = _dot(kvn, wv_ref[...])
    cos = jnp.concatenate([cos_ref[...]] * group, axis=0)
    sin = jnp.concatenate([sin_ref[...]] * group, axis=0)
    krope = kr * cos + krr * sin
    den_lane = lax.broadcasted_iota(I32, cos.shape, 1) == SOFTMAX_DEN_LANE
    def put_heads(ref, hd, val):
        for g in range(group):
            ref[g, hd] = val[g * tm:(g + 1) * tm].astype(BF16)

    def put_pairs(ref, hd, val):
        off = (hd % 2) * LANE
        for g in range(group):
            ref[g, hd // 2, :, off:off + LANE] = val[g * tm:(g + 1) * tm].astype(BF16)

    for hd in range(MLA_HEADS):
        sl = slice(hd * LANE, (hd + 1) * LANE)
        put_heads(q_o, hd, q[:, sl] * cos + qr[:, sl] * sin)
        put_heads(k_o, hd, kk[:, sl] + krope)
        put_heads(v_o, hd, jnp.where(den_lane, 1.0, vv[:, sl]))
    for hd in range(NA_HEADS):
        for which, ref in enumerate((nq_o, nk_o, nv_o)):
            lo = _C_NA + (which * NA_HEADS + hd) * LANE
            blk = z[:, lo:lo + LANE]
            if which == 2:
                put_pairs(ref, hd, jnp.where(den_lane, 1.0, blk))
            else:
                put_heads(ref, hd, blk)
    hy_o[...] = z[:, _C_HY:_C_LU].reshape(group, tm, _C_LU - _C_HY)
    lu_o[...] = z[:, _C_LU:_C_LG].reshape(group, tm, _C_LG - _C_LU)
    lg_o[...] = z[:, _C_LG:_C_END].reshape(group, tm, _C_END - _C_LG)


def _pad_to(a, n, axis):
    pad = [(0, 0)] * a.ndim
    pad[axis] = (0, n - a.shape[axis])
    return jnp.pad(a, pad)


def _rot_cols(w):
    half = w.shape[-1] // 2
    return jnp.concatenate([-w[..., half:], w[..., :half]], axis=-1)


def _head_blocks(cols_per_head):
    out = []
    for pieces in cols_per_head:
        k = pieces[0][0].shape[0]
        blk = jnp.zeros((k, LANE), F32)
        for arr, off in pieces:
            blk = blk.at[:, off:off + arr.shape[1]].set(arr)
        out.append(blk)
    return jnp.concatenate(out, axis=1)


def _proj_weights(lp):
    w_in = lp['w_in']
    d = w_in.shape[0]
    o = 0
    parts = {}
    for name, n in (('q', MLA_Q_RANK), ('kv', MLA_KV_RANK), ('kr', MLA_ROPE), ('na', 3 * NA_HEADS * NA_HEAD_DIM),
                    ('hy', 3 * HY_WIDTH), ('lu', LRU_WIDTH), ('lg', LRU_WIDTH), ('gt', 4 * d)):
        parts[name] = w_in[:, o:o + n]
        o += n
    zeros = lambda n: jnp.zeros((d, n), F32)
    kr_blk = jnp.concatenate([zeros(MLA_NOPE), parts['kr'], zeros(LANE - MLA_NOPE - MLA_ROPE)], axis=1)
    krr_blk = jnp.concatenate([zeros(MLA_NOPE), _rot_cols(parts['kr']), zeros(LANE - MLA_NOPE - MLA_ROPE)], axis=1)
    na_scale = NA_HEAD_DIM ** -0.5
    na_cols = []
    for which in range(3):
        for hd in range(NA_HEADS):
            lo = (which * NA_HEADS + hd) * NA_HEAD_DIM
            blk = parts['na'][:, lo:lo + NA_HEAD_DIM] * (na_scale if which == 0 else 1.0)
            na_cols.append(_pad_to(blk, LANE, 1))
    w1 = jnp.concatenate([_pad_to(parts['q'], MLA_Q_PAD, 1), parts['kv'], kr_blk, krr_blk] + na_cols
                         + [parts['hy'], parts['lu'], parts['lg']], axis=1).astype(BF16)
    mla_scale = (MLA_NOPE + MLA_ROPE) ** -0.5
    wuq = _pad_to(lp['mla_w_uq'], MLA_Q_PAD, 0) * mla_scale
    dq = MLA_NOPE + MLA_ROPE
    wuq_main = _head_blocks([[(wuq[:, hd * dq:hd * dq + dq], 0)] for hd in range(MLA_HEADS)])
    wuq_rot = _head_blocks([[(_rot_cols(wuq[:, hd * dq + MLA_NOPE:hd * dq + dq]), MLA_NOPE)]
                            for hd in range(MLA_HEADS)])
    dkv = MLA_NOPE + MLA_V
    wukv = lp['mla_w_ukv']
    wk = _head_blocks([[(wukv[:, hd * dkv:hd * dkv + MLA_NOPE], 0)] for hd in range(MLA_HEADS)])
    wv = _head_blocks([[(wukv[:, hd * dkv + MLA_NOPE:hd * dkv + dkv], 0)] for hd in range(MLA_HEADS)])
    gq = _pad_to(lp['mla_g_q'].reshape(1, -1), MLA_Q_PAD, 1)
    gkv = lp['mla_g_kv'].reshape(1, -1)
    return dict(w1=w1, w_gate=(0.5 * parts['gt']).astype(BF16), gq=gq, gkv=gkv, wuq=wuq_main.astype(BF16),
                wuq_rot=wuq_rot.astype(BF16), wk=wk.astype(BF16), wv=wv.astype(BF16))


def _rope_tables(seq, n_tok):
    t = jnp.arange(seq, dtype=I32)
    row = (t // GRID_W).astype(F32)
    col = (t % GRID_W).astype(F32)
    n_axis = MLA_ROPE // 4
    inv_freq = ROPE_THETA ** (-jnp.arange(n_axis, dtype=F32) / n_axis)
    ang = jnp.concatenate([row[:, None] * inv_freq, col[:, None] * inv_freq], axis=-1)
    cos = jnp.concatenate([jnp.cos(ang), jnp.cos(ang)], axis=-1)
    sin = jnp.concatenate([jnp.sin(ang), jnp.sin(ang)], axis=-1)
    cos_t = jnp.ones((n_tok, LANE), F32).at[:seq, MLA_NOPE:MLA_NOPE + MLA_ROPE].set(cos)
    sin_t = jnp.zeros((n_tok, LANE), F32).at[:seq, MLA_NOPE:MLA_NOPE + MLA_ROPE].set(sin)
    return cos_t, sin_t


def _batch_group(b, cap=2):
    return max(g for g in (4, 2, 1) if g <= cap and b % g == 0)


MOE_BATCH_GROUP = 4


def _kind_map(n_lat_tiles):
    return lambda b, i: (b, jnp.where(i >= n_lat_tiles, 1, 0), 0, 0)


def _project(stream, nt, modtab, g1, pw, rope, n_lat_tiles):
    b, _, d = stream[0].shape
    cos_t, sin_t = rope
    group = _batch_group(b)
    full = lambda a: pl.BlockSpec(a.shape, lambda bb, i: (0,) * a.ndim)
    head_out = lambda: pl.BlockSpec((group, MLA_HEADS, TM, LANE), lambda bb, i: (bb, 0, i, 0))
    tok_out = lambda n: pl.BlockSpec((group, TM, n), lambda bb, i: (bb, i, 0))
    head_shape = jax.ShapeDtypeStruct((b, MLA_HEADS, nt, LANE), BF16)
    pair_out = lambda: pl.BlockSpec((group, MLA_HEADS // 2, TM, 2 * LANE), lambda bb, i: (bb, 0, i, 0))
    pair_shape = jax.ShapeDtypeStruct((b, MLA_HEADS // 2, nt, 2 * LANE), BF16)
    tok_shape = lambda n: jax.ShapeDtypeStruct((b, nt, n), F32)
    weights = (g1, pw['w1'], pw['gq'], pw['gkv'], pw['wuq'], pw['wuq_rot'], pw['wk'], pw['wv'])
    return pl.pallas_call(
        functools.partial(_proj_kernel, n_lat_tiles=n_lat_tiles),
        grid=(b // group, nt // TM),
        in_specs=_stream_specs(stream, group, n_lat_tiles)
                 + [pl.BlockSpec((group, 1, 8, d), _kind_map(n_lat_tiles))]
                 + [full(w) for w in weights]
                 + [pl.BlockSpec((TM, LANE), lambda bb, i: (i, 0))] * 2,
        out_specs=[head_out(), head_out(), head_out(), head_out(), head_out(), pair_out(),
                   tok_out(3 * HY_WIDTH), tok_out(LRU_WIDTH), tok_out(LRU_WIDTH)],
        out_shape=[head_shape, head_shape, head_shape, head_shape, head_shape, pair_shape,
                   tok_shape(3 * HY_WIDTH), tok_shape(LRU_WIDTH), tok_shape(LRU_WIDTH)],
        compiler_params=_cparams("parallel", "parallel"),
        name="input_projection",
    )(stream[0], stream[1], modtab, *weights, cos_t, sin_t)


SOFTMAX_DEN_LANE = 64


def _softmax_pv(parts, lane_off=0):
    m = None
    for s, _ in parts:
        mm = jnp.max(s, axis=-1, keepdims=True)
        m = mm if m is None else jnp.maximum(m, mm)
    acc = None
    for s, v in parts:
        o = _dot(jnp.exp(s - m).astype(BF16), v)
        acc = o if acc is None else acc + o
    den = lane_off + SOFTMAX_DEN_LANE
    return acc[:, lane_off:lane_off + LANE] / acc[:, den:den + 1]


def _mla_kernel(q_ref, k_ref, v_ref, o_ref, *, seq, n_lat_tiles):
    i = pl.program_id(1)
    nt = k_ref.shape[2]

    def attend(lo, hi):
        for hd in range(MLA_HEADS):
            s = _dot_nt(q_ref[0, hd], k_ref[0, hd, lo:hi, :])
            o = _softmax_pv([(s, v_ref[0, hd, lo:hi, :])])
            o_ref[0, :, hd * LANE:(hd + 1) * LANE] = o.astype(BF16)

    @pl.when(i < n_lat_tiles)
    def _():
        attend(0, nt)

    @pl.when(i >= n_lat_tiles)
    def _():
        attend(seq, nt)


def _mla_attention(q, k, v, seq, tiles_used):
    b, h, nt, _ = q.shape
    kv_spec = pl.BlockSpec((1, h, nt, LANE), lambda bb, i: (bb, 0, 0, 0))
    return pl.pallas_call(
        functools.partial(_mla_kernel, seq=seq, n_lat_tiles=seq // TM),
        grid=(b, tiles_used),
        in_specs=[pl.BlockSpec((1, h, TM, LANE), lambda bb, i: (bb, 0, i, 0)), kv_spec, kv_spec],
        out_specs=pl.BlockSpec((1, TM, h * LANE), lambda bb, i: (bb, i, 0)),
        out_shape=jax.ShapeDtypeStruct((b, tiles_used * TM, h * LANE), BF16),
        compiler_params=_cparams("parallel", "arbitrary"),
        name="mla_attention",
    )(q, k, v)


def _na_bias_tables(rpb, rows):
    n_blk = rows // NA_TILE_ROWS
    col = np.arange(GRID_W)
    c0 = np.clip(col - NA_WIN_C // 2, 0, GRID_W - NA_WIN_C)
    in_win = (col[None, :] >= c0[:, None]) & (col[None, :] < c0[:, None] + NA_WIN_C)
    dc = np.clip(col[None, :] - col[:, None], 1 - NA_WIN_C, NA_WIN_C - 1) + NA_WIN_C - 1
    rpb = rpb.astype(F32)
    tables = []
    for j in (0, 1, n_blk - 1):
        w0 = min(max(NA_TILE_ROWS * j - NA_WIN_R // 2, 0), rows - NA_KEY_ROWS)
        r = NA_TILE_ROWS * j + np.arange(NA_TILE_ROWS)
        kr = w0 + np.arange(NA_KEY_ROWS)
        r0 = np.clip(r - NA_WIN_R // 2, 0, rows - NA_WIN_R)
        row_ok = (kr[None, :] >= r0[:, None]) & (kr[None, :] < r0[:, None] + NA_WIN_R)
        dr = np.clip(kr[None, :] - r[:, None] + NA_WIN_R - 1, 0, 2 * NA_WIN_R - 2)
        oh_r = jnp.asarray(np.eye(2 * NA_WIN_R - 1, dtype=np.float32)[dr.reshape(-1)])
        oh_c = jnp.asarray(np.eye(2 * NA_WIN_C - 1, dtype=np.float32)[dc.reshape(-1)])
        bias = jnp.einsum('ar,hrc,bc->hab', oh_r, rpb, oh_c, precision=lax.Precision.HIGHEST)
        bias = bias.reshape(NA_HEADS, NA_TILE_ROWS, NA_KEY_ROWS, GRID_W, GRID_W)
        mask = row_ok[:, :, None, None] & in_win[None, None, :, :]
        bias = jnp.where(jnp.asarray(mask)[None], bias, -jnp.inf)
        tables.append(bias.transpose(0, 1, 3, 2, 4).reshape(NA_HEADS, TM, NA_KEYS))
    return jnp.stack(tables)


def _na_kernel(q_ref, k_ref, v_ref, bias_ref, o_ref, *, seq, n_lat_tiles):
    i = pl.program_id(1)
    nt = k_ref.shape[2]
    rows = seq // GRID_W

    @pl.when(i < n_lat_tiles)
    def _():
        w0 = jnp.clip(NA_TILE_ROWS * i - NA_WIN_R // 2, 0, rows - NA_KEY_ROWS)
        start = pl.multiple_of(w0 * GRID_W, GRID_W)
        for hd in range(NA_HEADS):
            q = q_ref[0, hd]
            s_loc = _dot_nt(q, k_ref[0, hd, pl.ds(start, NA_KEYS), :]) + bias_ref[0, hd]
            s_ctx = _dot_nt(q, k_ref[0, hd, seq:nt, :])
            o = _softmax_pv([(s_loc, v_ref[0, hd // 2, pl.ds(start, NA_KEYS), :]),
                             (s_ctx, v_ref[0, hd // 2, seq:nt, :])], (hd % 2) * LANE)
            o_ref[0, :, hd * LANE:(hd + 1) * LANE] = o.astype(BF16)

    @pl.when(i >= n_lat_tiles)
    def _():
        for hd in range(NA_HEADS):
            s = _dot_nt(q_ref[0, hd], k_ref[0, hd, seq:nt, :])
            o = _softmax_pv([(s, v_ref[0, hd // 2, seq:nt, :])], (hd % 2) * LANE)
            o_ref[0, :, hd * LANE:(hd + 1) * LANE] = o.astype(BF16)


def _na_attention(q, k, v, bias, seq, tiles_used):
    b, h, nt, _ = q.shape
    n_lat = seq // TM
    kv_spec = pl.BlockSpec((1, h, nt, LANE), lambda bb, i: (bb, 0, 0, 0))
    cfg = lambda bb, i: (jnp.where(i == 0, 0, jnp.where(i >= n_lat - 1, 2, 1)), 0, 0, 0)
    return pl.pallas_call(
        functools.partial(_na_kernel, seq=seq, n_lat_tiles=n_lat),
        grid=(b, tiles_used),
        in_specs=[pl.BlockSpec((1, h, TM, LANE), lambda bb, i: (bb, 0, i, 0)), kv_spec,
                  pl.BlockSpec((1, h // 2, nt, 2 * LANE), lambda bb, i: (bb, 0, 0, 0)),
                  pl.BlockSpec((1, h, TM, NA_KEYS), cfg)],
        out_specs=pl.BlockSpec((1, TM, h * LANE), lambda bb, i: (bb, i, 0)),
        out_shape=jax.ShapeDtypeStruct((b, tiles_used * TM, h * LANE), BF16),
        compiler_params=_cparams("parallel", "arbitrary"),
        name="neighbourhood_attention",
    )(q, k, v, bias)


def _hp_dot(a, b):
    return jnp.dot(a, b, preferred_element_type=F32, precision=lax.Precision.HIGHEST)


def _hy_filter_kernel(f_ref, w1_ref, b1_ref, w2_ref, b2_ref, w3_ref, dec_ref, o_ref, obf_ref):
    f = f_ref[...]
    h = jnp.sin(_hp_dot(f, w1_ref[...]) + b1_ref[...])
    h = jnp.sin(_hp_dot(h, w2_ref[...]) + b2_ref[...])
    h = _hp_dot(h, w3_ref[...])
    h = h * jnp.exp(-f[:, 0:1] * jnp.abs(dec_ref[...]))
    o_ref[...] = h
    obf_ref[...] = h.astype(BF16)


def _hy_pos_features(length):
    t = jnp.linspace(0.0, 1.0, length, dtype=F32)[:, None]
    w = 2.0 * math.pi * jnp.arange(length, dtype=F32)[:, None] / length
    f = jnp.linspace(1e-4, HY_BANDS - 1, HY_BANDS, dtype=F32)[None, :]
    z = w * f
    return jnp.concatenate([t, jnp.cos(z), -jnp.sin(z)], axis=-1)


DFT_ROW_STEP = 64
DFT_ROWS = 256
DFT_FWD_COLS = 1024
DFT_INV_COLS = 512
FILTER_ROWS = 512


def _col_block(nc, cap):
    return min(nc, cap)


def _hyp_pre_kernel(z_ref, w_ref, b_ref, lat_o, lat_bf_o, ctx_o, ctx_bf_o, *, seq, ctx_len):
    w = w_ref[...]
    bias = b_ref[...]
    for lo, length, o_ref, obf_ref in ((0, seq, lat_o, lat_bf_o), (seq, ctx_len, ctx_o, ctx_bf_o)):
        h = length // 2
        even = z_ref[0, pl.ds(lo, h, stride=2), :]
        odd = z_ref[0, pl.ds(lo + 1, h, stride=2), :]
        row = lax.broadcasted_iota(I32, even.shape, 0)
        odd_prev = jnp.where(row == 0, 0.0, pltpu.roll(odd, 1, axis=0))
        even_next = jnp.where(row == h - 1, 0.0, pltpu.roll(even, h - 1, axis=0))
        y_even = bias + w[0:1] * odd_prev + w[1:2] * even + w[2:3] * odd
        y_odd = bias + w[0:1] * even + w[1:2] * odd + w[2:3] * even_next
        for r, y in enumerate((y_even, y_odd)):
            o_ref[0, r] = y
            obf_ref[0, r] = y.astype(BF16)


def _hyp_pre(hy, w, bvec, seq):
    b, nt, _ = hy.shape
    ctx_len = nt - seq
    wpad = _pad_to(w, 8, 0)
    per_plane = HY_WIDTH // LANE
    out_specs, out_shape = [], []
    for length in (seq, ctx_len):
        for dt in (F32, BF16):
            out_specs.append(pl.BlockSpec((1, 2, length // 2, LANE),
                                          lambda bb, g: (g // per_plane, 0, 0, bb * per_plane + g % per_plane)))
            out_shape.append(jax.ShapeDtypeStruct((3, 2, length // 2, b * HY_WIDTH), dt))
    return pl.pallas_call(
        functools.partial(_hyp_pre_kernel, seq=seq, ctx_len=ctx_len),
        grid=(b, 3 * per_plane),
        in_specs=[pl.BlockSpec((1, nt, LANE), lambda bb, g: (bb, 0, g)),
                  pl.BlockSpec((8, LANE), lambda bb, g: (0, g)),
                  pl.BlockSpec((1, LANE), lambda bb, g: (0, g))],
        out_specs=out_specs,
        out_shape=out_shape,
        compiler_params=_cparams("parallel", "parallel"),
        name="hyena_short_conv",
    )(hy, wpad, bvec.reshape(1, -1))


def _hyp_filters(length, lp):
    feats = _hy_pos_features(length)
    feats = feats.reshape(length // 2, 2, HY_EMB).transpose(1, 0, 2).reshape(length, HY_EMB)
    n_out = HY_ORDER * 2 * HY_WIDTH
    tl = min(length // 2, FILTER_ROWS)
    full = lambda a: pl.BlockSpec(a.shape, lambda i: (0,) * a.ndim)
    args = (lp['hy_w1'], lp['hy_b1'].reshape(1, -1), lp['hy_w2'], lp['hy_b2'].reshape(1, -1), lp['hy_w3'],
            lp['hy_decay'].reshape(1, n_out))
    filt, filt_bf = pl.pallas_call(
        _hy_filter_kernel,
        grid=(length // tl,),
        in_specs=[pl.BlockSpec((tl, HY_EMB), lambda i: (i, 0))] + [full(a) for a in args],
        out_specs=[pl.BlockSpec((tl, n_out), lambda i: (i, 0))] * 2,
        out_shape=[jax.ShapeDtypeStruct((length, n_out), F32), jax.ShapeDtypeStruct((length, n_out), BF16)],
        compiler_params=_cparams("parallel"),
        name="hyena_filter_mlp",
    )(feats, *args)
    return filt.reshape(2, length // 2, n_out), filt_bf.reshape(2, length // 2, n_out)


def _hyp_matrix_kernel(c1_ref, s1_ref, c2_ref, s2_ref, ck_ref, sk_ref, ckr_ref, skr_ref,
                       ce_o, se_o, co_o, so_o, set_o, cot_o, sot_o):
    c1, s1 = c1_ref[0], s1_ref[0]
    c2, s2 = c2_ref[...], s2_ref[...]
    cos = c1 * c2 - s1 * s2
    sin = s1 * c2 + c1 * s2
    row = lax.broadcasted_iota(I32, cos.shape, 0)
    col = lax.broadcasted_iota(I32, cos.shape, 1)
    first_row = (row + pl.program_id(0) * DFT_ROW_STEP) == 0
    alt_col = jnp.where(col % 2 == 0, 1.0, -1.0)
    alt_row = jnp.where(row % 2 == 0, 1.0, -1.0)
    ck, sk = ck_ref[...], sk_ref[...]
    ce_o[...] = cos.astype(BF16)
    se_o[...] = jnp.where(first_row, alt_col, -sin).astype(BF16)
    co_o[...] = (cos * ck - sin * sk).astype(BF16)
    so_o[...] = jnp.where(first_row, alt_col, -(sin * ck + cos * sk)).astype(BF16)
    ckr, skr = ckr_ref[...], skr_ref[...]
    set_o[...] = jnp.where(col == 0, alt_row, -sin).astype(BF16)
    cot_o[...] = (cos * ckr - sin * skr).astype(BF16)
    sot_o[...] = jnp.where(col == 0, alt_row, -(sin * ckr + cos * skr)).astype(BF16)


def _hyp_matrices(length):
    h = length // 2
    step = DFT_ROW_STEP
    m = jnp.arange(h, dtype=I32)

    def trig(kv):
        ang = ((kv[:, None] * m[None, :]) % length).astype(F32) * (2.0 * math.pi / length)
        return jnp.cos(ang), jnp.sin(ang)

    c1, s1 = trig(jnp.arange(h // step, dtype=I32) * step)
    c2, s2 = trig(jnp.arange(step, dtype=I32))
    half_angle = m.astype(F32) * (math.pi / length)
    ck, sk = jnp.cos(half_angle), jnp.sin(half_angle)
    coarse = pl.BlockSpec((1, 1, h), lambda j: (j, 0, 0))
    fine = pl.BlockSpec((step, h), lambda j: (0, 0))
    per_row = pl.BlockSpec((step, 1), lambda j: (j, 0))
    per_col = pl.BlockSpec((1, h), lambda j: (0, 0))
    out = pl.BlockSpec((step, h), lambda j: (j, 0))
    ce, se, co, so, se_t, co_t, so_t = pl.pallas_call(
        _hyp_matrix_kernel,
        grid=(h // step,),
        in_specs=[coarse, coarse, fine, fine, per_row, per_row, per_col, per_col],
        out_specs=[out] * 7,
        out_shape=[jax.ShapeDtypeStruct((h, h), BF16)] * 7,
        compiler_params=_cparams("parallel"),
        name="dft_matrices",
    )(c1[:, None, :], s1[:, None, :], c2, s2, ck[:, None], sk[:, None], ck[None, :], sk[None, :])
    return dict(fwd=(ce, se, co, so), inv=(ce, se_t, co_t, so_t))


def _hyp_fwd_kernel(ce_ref, se_ref, co_ref, so_ref, xe_ref, xo_ref, *rest, with_taps):
    xe, xo = xe_ref[...], xo_ref[...]
    a_re, a_im = _dot(ce_ref[...], xe), _dot(se_ref[...], xe)
    b_re, b_im = _dot(co_ref[...], xo), _dot(so_ref[...], xo)
    if not with_taps:
        for ref, val in zip(rest, (a_re, a_im, b_re, b_im)):
            ref[...] = val
        return
    t1re_ref, t1im_ref, t2re_ref, t2im_ref, sp_ref, gere_o, geim_o, gore_o, goim_o = rest
    t1re, t1im, t2re, t2im = t1re_ref[...], t1im_ref[...], t2re_ref[...], t2im_ref[...]
    sp = sp_ref[...]
    first_block = pl.program_id(1) == 0
    row8 = lax.broadcasted_iota(I32, (8, HY_WIDTH), 0)
    for bb in range(xe.shape[1] // HY_WIDTH):
        sl = slice(bb * HY_WIDTH, (bb + 1) * HY_WIDTH)
        are, aim, bre, bim = a_re[:, sl], a_im[:, sl], b_re[:, sl], b_im[:, sl]
        u1re, u1im = are + bre, aim + bim
        u2re, u2im = are - bre, bim - aim
        z1re, z1im = u1re * t1re - u1im * t1im, u1re * t1im + u1im * t1re
        z2re, z2im = u2re * t2re - u2im * t2im, u2re * t2im + u2im * t2re
        gere_o[:, sl] = (z1re + z2re).astype(BF16)
        geim_o[:, sl] = (z1im - z2im).astype(BF16)
        gore_o[:, sl] = (z1re - z2re).astype(BF16)
        goim_o[:, sl] = (z1im + z2im).astype(BF16)

        @pl.when(first_block)
        def _():
            u0, ul = u1re[0:8], u2re[0:8]
            a_s, b_s = aim[0:8], bim[0:8]
            dc, ny, mre, mim = sp[0:1], sp[1:2], sp[2:3], sp[3:4]
            first = row8 == 0
            gere_o[0:8, sl] = jnp.where(first, u0 * dc + ul * ny, (z1re + z2re)[0:8]).astype(BF16)
            gore_o[0:8, sl] = jnp.where(first, u0 * dc - ul * ny, (z1re - z2re)[0:8]).astype(BF16)
            geim_o[0:8, sl] = jnp.where(first, a_s * mre + b_s * mim, (z1im - z2im)[0:8]).astype(BF16)
            goim_o[0:8, sl] = jnp.where(first, b_s * mre - a_s * mim, (z1im + z2im)[0:8]).astype(BF16)


def _hyp_fwd(mats, x, plane, taps=None):
    _, _, h, nc = x.shape
    tk = min(h, DFT_ROWS)
    cb = _col_block(nc, DFT_FWD_COLS)
    grid = (nc // cb, h // tk)
    m_spec = pl.BlockSpec((tk, h), lambda c, j: (j, 0))
    x_spec = lambda r: pl.BlockSpec((None, None, h, cb), lambda c, j: (plane, r, 0, c))
    o_spec = pl.BlockSpec((tk, cb), lambda c, j: (j, c))
    if taps is None:
        return pl.pallas_call(
            functools.partial(_hyp_fwd_kernel, with_taps=False),
            grid=grid, in_specs=[m_spec] * 4 + [x_spec(0), x_spec(1)], out_specs=[o_spec] * 4,
            out_shape=[jax.ShapeDtypeStruct((h, nc), F32)] * 4,
            compiler_params=_cparams("parallel", "arbitrary"),
            name="hyena_dft_filters",
        )(*mats['fwd'], x, x)
    t_spec = pl.BlockSpec((tk, HY_WIDTH), lambda c, j: (j, 0))
    sp_spec = pl.BlockSpec((8, HY_WIDTH), lambda c, j: (0, 0))
    return pl.pallas_call(
        functools.partial(_hyp_fwd_kernel, with_taps=True),
        grid=grid, in_specs=[m_spec] * 4 + [x_spec(0), x_spec(1)] + [t_spec] * 4 + [sp_spec],
        out_specs=[o_spec] * 4,
        out_shape=[jax.ShapeDtypeStruct((h, nc), BF16)] * 4,
        compiler_params=_cparams("parallel", "arbitrary"),
        name="hyena_dft_forward",
    )(*mats['fwd'], x, x, *taps)


def _hyp_inv_kernel(ce_ref, set_ref, cot_ref, sot_ref, gere_ref, geim_ref, gore_ref, goim_ref,
                    gate_ref, prev_ref, bias_ref, *outs, last):
    conv = (_dot(ce_ref[...], gere_ref[...]) + _dot(set_ref[...], geim_ref[...]),
            _dot(cot_ref[...], gore_ref[...]) + _dot(sot_ref[...], goim_ref[...]))
    bias = bias_ref[...]
    for r in range(2):
        y = gate_ref[r] * (conv[r] + prev_ref[r] * bias)
        if last:
            (tok_o,) = outs
            for bb in range(y.shape[1] // HY_WIDTH):
                tok_o[bb, :, r * HY_WIDTH:(r + 1) * HY_WIDTH] = y[:, bb * HY_WIDTH:(bb + 1) * HY_WIDTH].astype(BF16)
        else:
            y_o, ybf_o = outs
            y_o[r] = y
            ybf_o[r] = y.astype(BF16)


def _hyp_inv(mats, g, gate, gate_plane, prev, prev_plane, bias_row, last):
    h, nc = g[0].shape
    tm = min(h, DFT_ROWS)
    cb = _col_block(nc, DFT_INV_COLS)
    grid = (nc // cb, h // tm)
    m_spec = pl.BlockSpec((tm, h), lambda c, i: (i, 0))
    g_spec = pl.BlockSpec((h, cb), lambda c, i: (0, c))
    e_spec = lambda plane: pl.BlockSpec((None, 2, tm, cb), lambda c, i: (plane, 0, i, c))
    b_spec = pl.BlockSpec((1, cb), lambda c, i: (0, c))
    if last:
        out_specs = [pl.BlockSpec((cb // HY_WIDTH, tm, 2 * HY_WIDTH), lambda c, i: (c, i, 0))]
        out_shape = [jax.ShapeDtypeStruct((nc // HY_WIDTH, h, 2 * HY_WIDTH), BF16)]
    else:
        out_specs = [e_spec(0), e_spec(0)]
        out_shape = [jax.ShapeDtypeStruct((1, 2, h, nc), F32), jax.ShapeDtypeStruct((1, 2, h, nc), BF16)]
    return pl.pallas_call(
        functools.partial(_hyp_inv_kernel, last=last),
        grid=grid, in_specs=[m_spec] * 4 + [g_spec] * 4 + [e_spec(gate_plane), e_spec(prev_plane), b_spec],
        out_specs=out_specs, out_shape=out_shape,
        compiler_params=_cparams("parallel", "arbitrary"),
        name="hyena_dft_inverse",
    )(*mats['inv'], *g, gate, prev, bias_row)


def _hyp_tap_tables(spec, filt, length):
    a_re, a_im, b_re, b_im = spec
    w = HY_WIDTH
    inv_n = 1.0 / (2 * length)
    tables = []
    for o in range(HY_ORDER):
        f_sl = slice((2 * o) * w, (2 * o + 1) * w)
        r_sl = slice((2 * o + 1) * w, (2 * o + 2) * w)
        hb0 = filt[0, 0:1, r_sl]
        f1re = (a_re + b_re)[:, f_sl] + (a_re + b_re)[:, r_sl] - hb0
        f1im = (a_im + b_im)[:, f_sl] - (a_im + b_im)[:, r_sl]
        f2re = (a_re - b_re)[:, f_sl] + (a_re - b_re)[:, r_sl] - hb0
        f2im = (b_im - a_im)[:, f_sl] - (b_im - a_im)[:, r_sl]
        dc = f1re[0:1]
        ny = f2re[0:1]
        mid_re = a_im[0:1, f_sl] + a_im[0:1, r_sl] - hb0
        mid_im = -b_im[0:1, f_sl] + b_im[0:1, r_sl]
        sp = jnp.concatenate([dc * inv_n, ny * inv_n, mid_re * (2 * inv_n), mid_im * (2 * inv_n),
                              jnp.zeros((4, w), F32)], axis=0)
        tables.append((f1re * (2 * inv_n), f1im * (2 * inv_n), f2re * (2 * inv_n), f2im * (2 * inv_n), sp))
    return tables


def _hyena_seq(mats, vx, vx_bf, lp, n_batch):
    h = vx.shape[2]
    length = 2 * h
    filt, filt_bf = _hyp_filters(length, lp)
    spec = _hyp_fwd(mats, filt_bf[None], 0)
    tables = _hyp_tap_tables(spec, filt, length)
    bias = lp['hy_bias'].astype(F32)
    y, y_bf = vx, vx_bf
    for o in range(HY_ORDER):
        g = _hyp_fwd(mats, y_bf, 0, tables[o])
        bias_row = jnp.tile(bias[o][None, :], (1, n_batch))
        last = o == HY_ORDER - 1
        res = _hyp_inv(mats, g, vx, o + 1, y, 0, bias_row, last)
        if last:
            return res[0].reshape(n_batch, length, HY_WIDTH)
        y, y_bf = res


def _lru_kernel(u_ref, g_ref, cw_ref, cb_ref, wa_ref, ba_ref, wx_ref, bx_ref, lam_ref, o_ref,
                pad_ref, y_ref, *, seq, ctx_len):
    tc = LRU_CHUNK
    halo = LRU_HALO
    width = LRU_WIDTH
    lat_off = halo
    ctx_off = 2 * halo + seq
    zero = jnp.zeros((halo, width), F32)
    pad_ref[0:halo, :] = zero
    pad_ref[lat_off:lat_off + seq, :] = u_ref[0, 0:seq, :]
    pad_ref[lat_off + seq:ctx_off, :] = zero
    pad_ref[ctx_off:ctx_off + ctx_len, :] = u_ref[0, seq:seq + ctx_len, :]
    pad_ref[ctx_off + ctx_len:ctx_off + ctx_len + halo, :] = zero
    row = lax.broadcasted_iota(I32, (tc, width), 0)
    n_win = tc + 2 * halo

    def chunk(pad_off, y_off, s, carry, d):
        wstart = pl.multiple_of(pad_off + s - halo, 8)
        win = pad_ref[pl.ds(wstart, n_win), :]
        cw = cw_ref[d]
        xc = cb_ref[d]
        for k in range(LRU_CONV):
            shift = (LRU_CONV - 1 - k) if d == 0 else -k
            rolled = win if shift == 0 else pltpu.roll(win, shift % n_win, axis=0)
            xc = xc + cw[k:k + 1] * rolled[halo:halo + tc]
        xb = xc.astype(BF16)
        r = _sigmoid(_dot(xb, wa_ref[d]) + ba_ref[d])
        gi = _sigmoid(_dot(xb, wx_ref[d]) + bx_ref[d])
        lam = lam_ref[d]
        softplus = jnp.maximum(-lam, 0.0) + jnp.log1p(jnp.exp(-jnp.abs(lam)))
        log_a = -LRU_C * r * softplus
        a = jnp.exp(log_a)
        bt = jnp.sqrt(-jnp.tanh(log_a) * (a * a + 1.0)) * (gi * xc)
        sft = 1
        while sft < tc:
            if d == 0:
                keep = row >= sft
                a_s = jnp.where(keep, pltpu.roll(a, sft, axis=0), 1.0)
                b_s = jnp.where(keep, pltpu.roll(bt, sft, axis=0), 0.0)
            else:
                keep = row < tc - sft
                a_s = jnp.where(keep, pltpu.roll(a, tc - sft, axis=0), 1.0)
                b_s = jnp.where(keep, pltpu.roll(bt, tc - sft, axis=0), 0.0)
            bt = a * b_s + bt
            a = a * a_s
            sft *= 2
        h = a * carry + bt
        yo = pl.multiple_of(y_off + s, 8)
        if d == 0:
            y_ref[pl.ds(yo, tc), :] = h
            return h[tc - 1:tc]
        y_ref[pl.ds(yo, tc), :] = y_ref[pl.ds(yo, tc), :] + h
        return h[0:1]

    n_lat = seq // tc
    n_ctx = ctx_len // tc
    for d in range(2):
        carry = jnp.zeros((1, width), F32)
        order = range(n_ctx) if d == 0 else range(n_ctx - 1, -1, -1)
        for c in order:
            carry = chunk(ctx_off, seq, c * tc, carry, d)

        def body(j, cr, d=d):
            jj = j if d == 0 else n_lat - 1 - j
            return chunk(lat_off, 0, jj * tc, cr, d)

        lax.fori_loop(0, n_lat, body, carry)
    o_ref[0] = (y_ref[...] * _gelu_tanh(g_ref[0])).astype(BF16)


def _block_diag(w):
    nd, nb, c, _ = w.shape
    out = jnp.zeros((nd, nb * c, nb * c), w.dtype)
    for n in range(nb):
        out = out.at[:, n * c:(n + 1) * c, n * c:(n + 1) * c].set(w[:, n])
    return out


def _lru_mixer(lu, lg, lp, seq):
    b, nt, w = lu.shape
    ctx_len = nt - seq
    row3 = lambda a: a.reshape(2, 1, w)
    args = (_pad_to(lp['lru_conv_w'], 8, 1), row3(lp['lru_conv_b']), _block_diag(lp['lru_wa']).astype(BF16),
            row3(lp['lru_ba']), _block_diag(lp['lru_wx']).astype(BF16), row3(lp['lru_bx']), row3(lp['lru_lambda']))
    full = lambda a: pl.BlockSpec(a.shape, lambda bb: (0,) * a.ndim)
    tok = pl.BlockSpec((1, nt, w), lambda bb: (bb, 0, 0))
    return pl.pallas_call(
        functools.partial(_lru_kernel, seq=seq, ctx_len=ctx_len),
        grid=(b,),
        in_specs=[tok, tok] + [full(a) for a in args],
        out_specs=tok,
        out_shape=jax.ShapeDtypeStruct((b, nt, w), BF16),
        scratch_shapes=[pltpu.VMEM((nt + 3 * LRU_HALO, w), F32), pltpu.VMEM((nt, w), F32)],
        compiler_params=_cparams("parallel"),
        name="rglru_scan",
    )(lu, lg, *args)


def _merge_kernel(xl_ref, xc_ref, mod_ref, g1_ref, a_ref, b_ref, c_ref, d_ref, wg_ref, wa_ref, wb_ref, wc_ref,
                  wd_ref, wo_ref, o_ref, *, n_lat_tiles):
    group, tm, dm = xl_ref.shape
    m = mod_ref[:, 0]
    x = _stream_tile(xl_ref, xc_ref, n_lat_tiles)
    h = _normmod(x, g1_ref[...], m[:, 1:2], m[:, 0:1]).reshape(group * tm, dm).astype(BF16)
    acc = None
    for k, (br, w) in enumerate(((a_ref, wa_ref), (b_ref, wb_ref), (c_ref, wc_ref), (d_ref, wd_ref))):
        term = (1.0 + jnp.tanh(_dot(h, wg_ref[:, k * dm:(k + 1) * dm]))) * _dot(
            br[...].reshape(group * tm, br.shape[-1]), w[...])
        acc = term if acc is None else acc + term
    y = _dot(acc.astype(BF16), wo_ref[...])
    o_ref[...] = x + m[:, 2:3] * y.reshape(group, tm, dm)


def _merge(stream, modtab, g1, branches, w_gate, lp, n_lat_tiles, tiles_used):
    b, _, d = stream[0].shape
    wbr = lp['w_branch']
    head_rows = lambda w, dv: jnp.concatenate(
        [_pad_to(w[hd * dv:(hd + 1) * dv], LANE, 0) for hd in range(4)], axis=0)
    wbr = 0.5 * wbr
    weights = (head_rows(wbr[0], MLA_V).astype(BF16), head_rows(wbr[1], NA_HEAD_DIM).astype(BF16),
               wbr[2].astype(BF16), wbr[3].astype(BF16), lp['w_out'].astype(BF16))
    group = _batch_group(b)
    full = lambda a: pl.BlockSpec(a.shape, lambda bb, i: (0,) * a.ndim)
    tok = lambda n: pl.BlockSpec((group, TM, n), lambda bb, i: (bb, i, 0))
    return pl.pallas_call(
        functools.partial(_merge_kernel, n_lat_tiles=n_lat_tiles),
        grid=(b // group, tiles_used),
        in_specs=_stream_specs(stream, group, n_lat_tiles)
                 + [pl.BlockSpec((group, 1, 8, d), _kind_map(n_lat_tiles)), full(g1)]
                 + [tok(br.shape[-1]) for br in branches] + [full(w_gate)] + [full(w) for w in weights],
        out_specs=tok(d),
        out_shape=jax.ShapeDtypeStruct((b, tiles_used * TM, d), F32),
        compiler_params=_cparams("parallel", "parallel"),
        name="merge_branches",
    )(stream[0], stream[1], modtab, g1, *branches, w_gate, *weights)


U32 = jnp.uint32
EXPERT_ROWS = 896


def _pack_pair(x):
    n = x.shape[-1] // 2
    hi = lax.bitcast_convert_type(x[:, :n].astype(BF16).astype(F32), U32)
    lo = lax.bitcast_convert_type(x[:, n:].astype(BF16).astype(F32), U32)
    return hi | (lo >> 16)


def _unpack_pair(p):
    hi = lax.bitcast_convert_type(p & jnp.uint32(0xFFFF0000), F32)
    lo = lax.bitcast_convert_type(p << 16, F32)
    return hi, lo


def _router_kernel(x_ref, mod_ref, g2_ref, rw_ref, rb_ref, tri_ref, h2_o, idx_o, wts_o, rank_o, cnt_o, carry):
    i = pl.program_id(0)

    @pl.when(i == 0)
    def _():
        carry[...] = jnp.zeros_like(carry)

    group, tm, d = x_ref.shape
    m = mod_ref[:, 0]
    h2 = _normmod(x_ref[...], g2_ref[...], m[:, 4:5], m[:, 3:4]).reshape(group * tm, d)
    half = h2.shape[-1] // 2
    h2_o[0] = _pack_pair(h2[:, :half])
    h2_o[1] = _pack_pair(h2[:, half:])
    logits = lax.dot_general(rw_ref[...], h2, (((1,), (1,)), ((), ())), preferred_element_type=F32,
                             precision=lax.Precision.HIGHEST)
    scores = _sigmoid(logits)
    biased = scores + rb_ref[...]
    expert = lax.broadcasted_iota(I32, scores.shape, 0)
    picks = []
    onehot_all = jnp.zeros(scores.shape, F32)
    for _ in range(TOP_K):
        best = jnp.max(biased, axis=0, keepdims=True)
        arg = jnp.min(jnp.where(biased == best, expert, N_EXPERTS), axis=0, keepdims=True)
        hit = expert == arg
        sel = jnp.sum(jnp.where(hit, scores, 0.0), axis=0, keepdims=True)
        biased = jnp.where(hit, -jnp.inf, biased)
        onehot_all = onehot_all + jnp.where(hit, 1.0, 0.0)
        picks.append((arg, hit, sel))
    total = picks[0][2]
    for _, _, sel in picks[1:]:
        total = total + sel
    earlier = _dot(onehot_all.astype(BF16), tri_ref[...]) + carry[...]
    pad_rows = TOPK_PAD - TOP_K
    ranks = [jnp.sum(jnp.where(hit, earlier, 0.0), axis=0, keepdims=True).astype(I32) for _, hit, _ in picks]
    scale = ROUTED_SCALE / total
    n_tok = scores.shape[1]
    idx_o[...] = jnp.concatenate([arg for arg, _, _ in picks] + [jnp.zeros((pad_rows, n_tok), I32)], axis=0)
    wts_o[...] = jnp.concatenate([sel * scale for _, _, sel in picks] + [jnp.zeros((pad_rows, n_tok), F32)], axis=0)
    rank_o[...] = jnp.concatenate(ranks + [jnp.zeros((pad_rows, n_tok), I32)], axis=0)
    carry[...] = carry[...] + jnp.sum(onehot_all, axis=1, keepdims=True)
    cnt_o[...] = carry[...]


def _tile_maps(n_groups, n_lat_tiles):
    tok = lambda f: (f % n_groups, f // n_groups, 0)
    mod = lambda f: (f % n_groups, jnp.where(f // n_groups >= n_lat_tiles, 1, 0), 0, 0)
    return tok, mod


def _route(x1, modtab, g2, lp, n_lat_tiles, tiles_used):
    b, _, d = x1.shape
    group = _batch_group(b, MOE_BATCH_GROUP)
    rows = group * TM
    t = b * tiles_used * TM
    rw = lp['router_w'].T
    rb = lp['router_bias'].reshape(-1, 1)
    tri = (np.arange(rows)[:, None] < np.arange(rows)[None, :]).astype(np.float32)
    tri = jnp.asarray(tri, BF16)
    per_tok = lambda: pl.BlockSpec((TOPK_PAD, rows), lambda i: (0, i))
    full = lambda a: pl.BlockSpec(a.shape, lambda i: (0,) * a.ndim)
    tok, mod = _tile_maps(b // group, n_lat_tiles)
    return pl.pallas_call(
        _router_kernel,
        grid=(t // rows,),
        in_specs=[pl.BlockSpec((group, TM, d), tok), pl.BlockSpec((group, 1, 8, d), mod),
                  full(g2), full(rw), full(rb), full(tri)],
        out_specs=[pl.BlockSpec((2, rows, d // 4), lambda i: (0, i, 0)), per_tok(), per_tok(), per_tok(),
                   pl.BlockSpec((N_EXPERTS, 1), lambda i: (0, 0))],
        out_shape=[jax.ShapeDtypeStruct((2, t, d // 4), U32), jax.ShapeDtypeStruct((TOPK_PAD, t), I32),
                   jax.ShapeDtypeStruct((TOPK_PAD, t), F32), jax.ShapeDtypeStruct((TOPK_PAD, t), I32),
                   jax.ShapeDtypeStruct((N_EXPERTS, 1), F32)],
        scratch_shapes=[pltpu.VMEM((N_EXPERTS, 1), F32)],
        compiler_params=_cparams("arbitrary"),
        name="moe_router",
    )(x1, modtab, g2, rw, rb, tri)


SC_WINDOW = 128


def _sc_mesh():
    return plsc.VectorSubcoreMesh(core_axis_name="c", subcore_axis_name="s")


def _sc_scatter_rows(src, idx, n_out):
    n, width = src.shape
    k_rep = idx.shape[0]
    half = n // SC_WINDOW // 2

    @functools.partial(pl.kernel, out_type=jax.ShapeDtypeStruct((n_out, width), src.dtype), mesh=_sc_mesh(),
                       scratch_types=[], name="moe_dispatch_sc")
    def scatter(src_hbm, idx_hbm, out_hbm):
        def body(x_vmem, *i_vmems):
            for i_vmem in i_vmems:
                pltpu.sync_copy(x_vmem, out_hbm.at[i_vmem.at[0]])

        pltpu.emit_pipeline(
            body,
            grid=(2, half),
            in_specs=[pl.BlockSpec((SC_WINDOW, width), lambda a, i: (a * half + i, 0))]
                     + [pl.BlockSpec((1, SC_WINDOW), lambda a, i, k=k: (k, a * half + i)) for k in range(k_rep)],
            out_specs=[],
            core_axis_name=("c", "s"),
            dimension_semantics=(pltpu.PARALLEL, pltpu.PARALLEL),
        )(src_hbm, *([idx_hbm] * k_rep))

    return scatter(src, idx)


def _sc_gather_rows(src, idx):
    k_rep, n = idx.shape
    width = src.shape[1]
    n_win = n // SC_WINDOW

    @functools.partial(pl.kernel, out_type=jax.ShapeDtypeStruct((k_rep * n, width), src.dtype), mesh=_sc_mesh(),
                       scratch_types=[], name="moe_gather_sc")
    def gather(src_hbm, idx_hbm, out_hbm):
        def body(i_vmem, o_vmem):
            pltpu.sync_copy(src_hbm.at[i_vmem.at[0]], o_vmem)

        pltpu.emit_pipeline(
            body,
            grid=(k_rep, n_win),
            in_specs=[pl.BlockSpec((1, SC_WINDOW), lambda k, i: (k, i))],
            out_specs=[pl.BlockSpec((SC_WINDOW, width), lambda k, i: (k * n_win + i, 0))],
            core_axis_name=("c", "s"),
            dimension_semantics=(pltpu.PARALLEL, pltpu.PARALLEL),
        )(idx_hbm, out_hbm)

    return gather(src, idx)


def _unpack_planes(p0, p1):
    return _unpack_pair(p0) + _unpack_pair(p1)


def _dot_quarters(parts, w_ref):
    q = parts[0].shape[-1]
    acc = None
    for j, part in enumerate(parts):
        term = _dot(part.astype(BF16), w_ref[j * q:(j + 1) * q, :])
        acc = term if acc is None else acc + term
    return acc


def _expert_kernel(be_ref, nv_ref, blk_ref, xs_ref, wg_ref, wu_ref, wd_ref, ys_o, wg_s, wu_s, wd_s):
    del blk_ref
    i = pl.program_id(0)
    prev = be_ref[jnp.maximum(i - 1, 0)]

    @pl.when((i == 0) | (be_ref[i] != prev))
    def _():
        wg_s[...] = (0.5 * wg_ref[0]).astype(BF16)
        wu_s[...] = wu_ref[0].astype(BF16)
        wd_s[...] = wd_ref[0].astype(BF16)

    @pl.when(nv_ref[i] > 0)
    def _():
        keep = lax.broadcasted_iota(I32, xs_ref.shape[1:], 0) < nv_ref[i]
        parts = _unpack_planes(jnp.where(keep, xs_ref[0], jnp.uint32(0)), jnp.where(keep, xs_ref[1], jnp.uint32(0)))
        hid = _half_silu(_dot_quarters(parts, wg_s)) * _dot_quarters(parts, wu_s)
        y = _dot(hid.astype(BF16), wd_s[...])
        half = y.shape[-1] // 2
        ys_o[0] = _pack_pair(y[:, :half])
        ys_o[1] = _pack_pair(y[:, half:])


def _experts(xs, block_e, n_valid, block_src, weights, layer):
    _, n_rows, dq = xs.shape
    d = 4 * dq
    n_blocks = n_rows // EXPERT_ROWS
    hid = EXPERT_HIDDEN
    grid_spec = pltpu.PrefetchScalarGridSpec(
        num_scalar_prefetch=3,
        grid=(n_blocks,),
        in_specs=[pl.BlockSpec((2, EXPERT_ROWS, dq), lambda i, be, nv, blk: (0, blk[i], 0)),
                  pl.BlockSpec((None, 1, d, hid), lambda i, be, nv, blk: (layer, be[i], 0, 0)),
                  pl.BlockSpec((None, 1, d, hid), lambda i, be, nv, blk: (layer, be[i], 0, 0)),
                  pl.BlockSpec((None, 1, hid, d), lambda i, be, nv, blk: (layer, be[i], 0, 0))],
        out_specs=pl.BlockSpec((2, EXPERT_ROWS, dq), lambda i, be, nv, blk: (0, blk[i], 0)),
        scratch_shapes=[pltpu.VMEM((d, hid), BF16), pltpu.VMEM((d, hid), BF16), pltpu.VMEM((hid, d), BF16)],
    )
    return pl.pallas_call(
        _expert_kernel,
        grid_spec=grid_spec,
        out_shape=jax.ShapeDtypeStruct((2, n_rows, dq), U32),
        compiler_params=_cparams("arbitrary"),
        name="moe_experts",
    )(block_e, n_valid, block_src, xs, *weights)


def _combine_kernel(g_ref, wts_ref, h2_ref, x_ref, mod_ref, sg_ref, su_ref, sd_ref, gf_ref, o_ref, *, final):
    parts = _unpack_planes(h2_ref[0], h2_ref[1])
    hid = _half_silu(_dot_quarters(parts, sg_ref)) * _dot_quarters(parts, su_ref)
    shared = _dot(hid.astype(BF16), sd_ref[...])
    wts = wts_ref[...]
    routed = None
    for k in range(TOP_K):
        w = wts[:, k:k + 1]
        terms = [w * part for part in _unpack_planes(g_ref[k, 0], g_ref[k, 1])]
        routed = terms if routed is None else [r + t for r, t in zip(routed, terms)]
    f = shared + jnp.concatenate(routed, axis=1)
    m = mod_ref[:, 0]
    x2 = x_ref[...] + m[:, 5:6] * f.reshape(x_ref.shape)
    if final:
        x2 = x2 * lax.rsqrt(jnp.mean(x2 * x2, axis=-1, keepdims=True) + NORM_EPS) * gf_ref[...]
    o_ref[...] = x2


def _combine(g, wts, h2p, x1, modtab, lp, g_final, n_lat_tiles, tiles_used, final):
    b, _, d = x1.shape
    dq = d // 4
    weights = ((0.5 * lp['sh_w_gate']).astype(BF16), lp['sh_w_up'].astype(BF16), lp['sh_w_down'].astype(BF16),
               g_final.reshape(1, -1))
    full = lambda a: pl.BlockSpec(a.shape, lambda i: (0,) * a.ndim)
    group = _batch_group(b, MOE_BATCH_GROUP)
    rows = group * TM
    tok, mod = _tile_maps(b // group, n_lat_tiles)
    return pl.pallas_call(
        functools.partial(_combine_kernel, final=final),
        grid=(b * tiles_used * TM // rows,),
        in_specs=[pl.BlockSpec((TOP_K, 2, rows, dq), lambda i: (0, 0, i, 0)),
                  pl.BlockSpec((rows, TOPK_PAD), lambda i: (i, 0)),
                  pl.BlockSpec((2, rows, dq), lambda i: (0, i, 0)),
                  pl.BlockSpec((group, TM, d), tok), pl.BlockSpec((group, 1, 8, d), mod)]
                 + [full(w) for w in weights],
        out_specs=pl.BlockSpec((group, TM, d), tok),
        out_shape=jax.ShapeDtypeStruct((b, tiles_used * TM, d), F32),
        compiler_params=_cparams("parallel"),
        name="moe_combine",
    )(g, wts, h2p, x1, modtab, *weights)


def _sorted_rows_kernel(idx_ref, rank_ref, ps_ref, o_ref):
    expert = lax.broadcasted_iota(I32, (N_EXPERTS, idx_ref.shape[1]), 0)
    starts = ps_ref[...]
    idx = idx_ref[...]
    rows = [jnp.sum(jnp.where(idx[k:k + 1] == expert, starts, 0), axis=0, keepdims=True) for k in range(TOP_K)]
    pad = jnp.zeros((TOPK_PAD - TOP_K, idx.shape[1]), I32)
    o_ref[...] = jnp.concatenate(rows + [pad], axis=0) + rank_ref[...]


def _sorted_rows(idx, rank, p_starts):
    t = idx.shape[1]
    cols = TM * max(k for k in (8, 4, 2, 1) if (t // TM) % k == 0)
    per_tok = pl.BlockSpec((TOPK_PAD, cols), lambda i: (0, i))
    return pl.pallas_call(
        _sorted_rows_kernel,
        grid=(t // cols,),
        in_specs=[per_tok, per_tok, pl.BlockSpec((N_EXPERTS, 1), lambda i: (0, 0))],
        out_specs=per_tok,
        out_shape=jax.ShapeDtypeStruct((TOPK_PAD, t), I32),
        compiler_params=_cparams("parallel"),
        name="moe_sorted_rows",
    )(idx, rank, p_starts.reshape(-1, 1))


def _moe(x1, modtab, g2, lp, g_final, n_lat_tiles, final):
    b, nt, d = x1.shape
    tiles_used = n_lat_tiles if final else nt // TM
    t = b * tiles_used * TM
    h2p, idx, wts, rank, cnt = _route(x1, modtab, g2, lp, n_lat_tiles, tiles_used)
    counts = cnt[:, 0].astype(I32)
    padded = (counts + EXPERT_ROWS - 1) // EXPERT_ROWS * EXPERT_ROWS
    p_ends = jnp.cumsum(padded)
    p_starts = p_ends - padded
    n_blocks = (t * TOP_K + N_EXPERTS * (EXPERT_ROWS - 1)) // EXPERT_ROWS
    n_rows = n_blocks * EXPERT_ROWS
    dest = _sorted_rows(idx, rank, p_starts)[:TOP_K]
    plane_idx = jnp.concatenate([dest, dest + n_rows], axis=1)
    blk_start = jnp.arange(n_blocks, dtype=I32) * EXPERT_ROWS
    block_e = jnp.minimum(jnp.sum((p_ends[None, :] <= blk_start[:, None]).astype(I32), axis=1), N_EXPERTS - 1)
    n_valid = jnp.clip((p_starts + counts)[block_e] - blk_start, 0, EXPERT_ROWS).astype(I32)
    dq = d // 4
    xs = _sc_scatter_rows(h2p.reshape(2 * t, dq), plane_idx, 2 * n_rows).reshape(2, n_rows, dq)
    block_src = jnp.minimum(jnp.arange(n_blocks, dtype=I32), p_ends[-1] // EXPERT_ROWS - 1)
    ys = _experts(xs, block_e, n_valid, block_src, lp['expert_stacks'], lp['layer'])
    g = _sc_gather_rows(ys.reshape(2 * n_rows, dq), plane_idx).reshape(TOP_K, 2, t, dq)
    return _combine(g, wts.T, h2p, x1, modtab, lp, g_final, n_lat_tiles, tiles_used, final)


def _layer(stream, nt, c, c_ctx, lp, consts, g_final, seq, final):
    b = stream[0].shape[0]
    n_lat_tiles = seq // TM
    rope, mats_lat, mats_ctx = consts
    modtab = _mod_table(c, c_ctx, *lp['mod_stacks'], lp['layer'])
    g1 = lp['g_norm1'].reshape(1, -1)
    g2 = lp['g_norm2'].reshape(1, -1)
    pw = _proj_weights(lp)
    q, k, v, nq, nk, nv, hy, lu, lg = _project(stream, nt, modtab, g1, pw, rope, n_lat_tiles)
    tiles_used = n_lat_tiles if final else nt // TM
    br_a = _mla_attention(q, k, v, seq, tiles_used)
    br_b = _na_attention(nq, nk, nv, _na_bias_tables(lp['na_rpb'], seq // GRID_W), seq, tiles_used)
    pre = _hyp_pre(hy, lp['hy_short_w'], lp['hy_short_b'], seq)
    lat_f, lat_bf, ctx_f, ctx_bf = pre
    br_c = _hyena_seq(mats_lat, lat_f, lat_bf, lp, b)
    if not final:
        br_c = jnp.concatenate([br_c, _hyena_seq(mats_ctx, ctx_f, ctx_bf, lp, b)], axis=1)
    br_d = _lru_mixer(lu, lg, lp, seq)
    x1 = _merge(stream, modtab, g1, (br_a, br_b, br_c, br_d), pw['w_gate'], lp, n_lat_tiles, tiles_used)
    return _moe(x1, modtab, g2, lp, g_final, n_lat_tiles, final)


_LAYER_KEYS = ('w_mod', 'b_mod', 'g_norm1', 'g_norm2', 'w_in', 'mla_g_q', 'mla_w_uq', 'mla_g_kv', 'mla_w_ukv',
               'na_rpb', 'hy_short_w', 'hy_short_b', 'hy_w1', 'hy_b1', 'hy_w2', 'hy_b2', 'hy_w3', 'hy_decay',
               'hy_bias', 'lru_conv_w', 'lru_conv_b', 'lru_wa', 'lru_ba', 'lru_wx', 'lru_bx', 'lru_lambda',
               'w_branch', 'w_out', 'router_w', 'router_bias', 'exp_w_gate', 'exp_w_up', 'exp_w_down',
               'sh_w_gate', 'sh_w_up', 'sh_w_down')


def kernel(x, c, ctx, c_ctx, w_mod, b_mod, g_norm1, g_norm2, w_in, mla_g_q, mla_w_uq, mla_g_kv, mla_w_ukv, na_rpb, hy_short_w, hy_short_b, hy_w1, hy_b1, hy_w2, hy_b2, hy_w3, hy_decay, hy_bias, lru_conv_w, lru_conv_b, lru_wa, lru_ba, lru_wx, lru_bx, lru_lambda, w_branch, w_out, router_w, router_bias, exp_w_gate, exp_w_up, exp_w_down, sh_w_gate, sh_w_up, sh_w_down, g_final):
    stacked = dict(zip(_LAYER_KEYS, (w_mod, b_mod, g_norm1, g_norm2, w_in, mla_g_q, mla_w_uq, mla_g_kv, mla_w_ukv,
                                     na_rpb, hy_short_w, hy_short_b, hy_w1, hy_b1, hy_w2, hy_b2, hy_w3, hy_decay,
                                     hy_bias, lru_conv_w, lru_conv_b, lru_wa, lru_ba, lru_wx, lru_bx, lru_lambda,
                                     w_branch, w_out, router_w, router_bias, exp_w_gate, exp_w_up, exp_w_down,
                                     sh_w_gate, sh_w_up, sh_w_down)))
    b, seq, d = x.shape
    ctx_len = ctx.shape[1]
    depth = w_mod.shape[0]
    assert seq % TM == 0 and ctx_len % TM == 0 and seq // GRID_W >= NA_KEY_ROWS + 1
    nt = seq + ctx_len
    stream = (x, ctx, 0)
    consts = (_rope_tables(seq, seq + ctx_len), _hyp_matrices(seq), _hyp_matrices(ctx_len))
    for i in range(depth):
        big = ('w_mod', 'b_mod', 'exp_w_gate', 'exp_w_up', 'exp_w_down')
        lp = {name: w[i] for name, w in stacked.items() if name not in big}
        lp['layer'] = i
        lp['mod_stacks'] = (w_mod, b_mod)
        lp['expert_stacks'] = (exp_w_gate, exp_w_up, exp_w_down)
        xa = _layer(stream, nt, c, c_ctx, lp, consts, g_final, seq, i == depth - 1)
        stream = (xa, xa, seq // TM)
    return xa
```

```python
import functools
import math

import numpy as np
import jax
import jax.numpy as jnp
from jax import lax
from jax.experimental import pallas as pl
from jax.experimental.pallas import tpu as pltpu
from jax.experimental.pallas import tpu_sc as plsc

F32 = jnp.float32
BF16 = jnp.bfloat16
I32 = jnp.int32

TM = 256
LANE = 128
GRID_W = 64
N_MOD = 6
NORM_EPS = 1e-6

MLA_HEADS, MLA_NOPE, MLA_ROPE, MLA_V = 4, 64, 32, 64
MLA_Q_RANK, MLA_KV_RANK = 192, 128
MLA_Q_PAD = 256
ROPE_THETA = 10000.0

NA_HEADS, NA_HEAD_DIM, NA_WIN_R, NA_WIN_C = 4, 64, 8, 16
NA_TILE_ROWS = TM // GRID_W
NA_KEY_ROWS = NA_TILE_ROWS + NA_WIN_R - 1
NA_KEYS = NA_KEY_ROWS * GRID_W

HY_WIDTH, HY_ORDER, HY_SHORT, HY_BANDS, HY_FFN = 256, 2, 3, 16, 64
HY_EMB = 2 * HY_BANDS + 1

LRU_WIDTH, LRU_BLOCKS, LRU_CONV, LRU_C = 256, 4, 4, 8.0
LRU_CHUNK = 256
LRU_HALO = 8

N_EXPERTS, TOP_K, EXPERT_HIDDEN, ROUTED_SCALE, MOE_BLOCK = 64, 6, 256, 2.5, 256
TOPK_PAD = 8

VMEM_LIMIT = 56 * 1024 * 1024


def _cparams(*sem):
    return pltpu.CompilerParams(dimension_semantics=sem, vmem_limit_bytes=VMEM_LIMIT)


def _dot(a, b):
    return jnp.dot(a, b, preferred_element_type=F32)


def _dot_nt(a, b):
    return lax.dot_general(a, b, (((1,), (1,)), ((), ())), preferred_element_type=F32)


def _sigmoid(x):
    return jax.nn.sigmoid(x)


def _half_silu(g):
    return g * (1.0 + jnp.tanh(g))


def _silu(x):
    return x * _sigmoid(x)


def _gelu_tanh(x):
    return 0.5 * x * (1.0 + jnp.tanh(math.sqrt(2.0 / math.pi) * (x + 0.044715 * (x * x * x))))


def _normmod(x, g, scale, shift):
    y = x * lax.rsqrt(jnp.mean(x * x, axis=-1, keepdims=True) + NORM_EPS) * g
    return y * (1.0 + scale) + shift


MOD_COLS = 1024


def _mod_kernel(c_ref, w_ref, b_ref, o_ref):
    s = _silu(c_ref[...])
    o_ref[...] = _dot(s.astype(BF16), w_ref[...].astype(BF16)) + b_ref[...]


def _mod_table(c, c_ctx, w_mod, b_mod, layer):
    b, d = c.shape
    rows = -(-(b + 1) // 16) * 16
    cc = jnp.zeros((rows, d), F32).at[:b].set(c).at[b].set(c_ctx)
    tn = MOD_COLS
    mod = pl.pallas_call(
        _mod_kernel,
        grid=(N_MOD * d // tn,),
        in_specs=[pl.BlockSpec((rows, d), lambda j: (0, 0)),
                  pl.BlockSpec((None, d, tn), lambda j: (layer, 0, j)),
                  pl.BlockSpec((None, 1, tn), lambda j: (layer, 0, j))],
        out_specs=pl.BlockSpec((rows, tn), lambda j: (0, j)),
        out_shape=jax.ShapeDtypeStruct((rows, N_MOD * d), F32),
        compiler_params=_cparams("arbitrary"),
        name="mod_vectors",
    )(cc, w_mod, b_mod[:, None, :])
    lat = mod[:b].reshape(b, N_MOD, d)
    ctx = jnp.broadcast_to(mod[b].reshape(1, N_MOD, d), (b, N_MOD, d))
    tab = jnp.stack([lat, ctx], axis=1)
    return jnp.pad(tab, ((0, 0), (0, 0), (0, 8 - N_MOD), (0, 0)))


_C_QLAT, _C_KVLAT, _C_KR, _C_KRR, _C_NA, _C_HY, _C_LU, _C_LG, _C_END = (
    0, 256, 384, 512, 640, 640 + 3 * NA_HEADS * LANE, 640 + 1536 + 768, 640 + 1536 + 1024, 640 + 1536 + 1280)


def _stream_tile(lat_ref, ctx_ref, n_lat_tiles):
    return jnp.where(pl.program_id(1) < n_lat_tiles, lat_ref[...], ctx_ref[...])


def _stream_specs(stream, group, n_lat_tiles):
    lat_src, ctx_src, ctx_tile0 = stream
    d = lat_src.shape[-1]
    return [pl.BlockSpec((group, TM, d), lambda bb, i: (bb, jnp.minimum(i, n_lat_tiles - 1), 0)),
            pl.BlockSpec((group, TM, d), lambda bb, i: (bb, ctx_tile0 + jnp.maximum(i - n_lat_tiles, 0), 0))]


def _proj_kernel(xl_ref, xc_ref, mod_ref, g1_ref, w1_ref, gq_ref, gkv_ref, wuq_ref, wuqr_ref, wk_ref, wv_ref,
                 cos_ref, sin_ref, q_o, k_o, v_o, nq_o, nk_o, nv_o, hy_o, lu_o, lg_o, *, n_lat_tiles):
    group, tm, d = xl_ref.shape
    m = mod_ref[:, 0]
    x = _stream_tile(xl_ref, xc_ref, n_lat_tiles)
    h = _normmod(x, g1_ref[...], m[:, 1:2], m[:, 0:1]).reshape(group * tm, d)
    z = _dot(h.astype(BF16), w1_ref[...])
    qlat = z[:, _C_QLAT:_C_KVLAT]
    kvlat = z[:, _C_KVLAT:_C_KR]
    kr = z[:, _C_KR:_C_KRR]
    krr = z[:, _C_KRR:_C_NA]
    qn = qlat * lax.rsqrt(jnp.sum(qlat * qlat, axis=-1, keepdims=True) * (1.0 / MLA_Q_RANK) + NORM_EPS) * gq_ref[...]
    kvn = kvlat * lax.rsqrt(jnp.mean(kvlat * kvlat, axis=-1, keepdims=True) + NORM_EPS) * gkv_ref[...]
    qn = qn.astype(BF16)
    kvn = kvn.astype(BF16)
    q = _dot(qn, wuq_ref[...])
    qr = _dot(qn, wuqr_ref[...])
    kk = _dot(kvn, wk_ref[...])
    vv = _dot(kvn, wv_ref[...])
    cos = jnp.concatenate([cos_ref[...]] * group, axis=0)
    sin = jnp.concatenate([sin_ref[...]] * group, axis=0)
    krope = kr * cos + krr * sin
    den_lane = lax.broadcasted_iota(I32, cos.shape, 1) == SOFTMAX_DEN_LANE
    def put_heads(ref, hd, val):
        for g in range(group):
            ref[g, hd] = val[g * tm:(g + 1) * tm].astype(BF16)

    def put_pairs(ref, hd, val):
        off = (hd % 2) * LANE
        for g in range(group):
            ref[g, hd // 2, :, off:off + LANE] = val[g * tm:(g + 1) * tm].astype(BF16)

    for hd in range(MLA_HEADS):
        sl = slice(hd * LANE, (hd + 1) * LANE)
        put_heads(q_o, hd, q[:, sl] * cos + qr[:, sl] * sin)
        put_heads(k_o, hd, kk[:, sl] + krope)
        put_heads(v_o, hd, jnp.where(den_lane, 1.0, vv[:, sl]))
    for hd in range(NA_HEADS):
        for which, ref in enumerate((nq_o, nk_o, nv_o)):
            lo = _C_NA + (which * NA_HEADS + hd) * LANE
            blk = z[:, lo:lo + LANE]
            if which == 2:
                put_pairs(ref, hd, jnp.where(den_lane, 1.0, blk))
            else:
                put_heads(ref, hd, blk)
    hy_o[...] = z[:, _C_HY:_C_LU].reshape(group, tm, _C_LU - _C_HY)
    lu_o[...] = z[:, _C_LU:_C_LG].reshape(group, tm, _C_LG - _C_LU)
    lg_o[...] = z[:, _C_LG:_C_END].reshape(group, tm, _C_END - _C_LG)


def _pad_to(a, n, axis):
    pad = [(0, 0)] * a.ndim
    pad[axis] = (0, n - a.shape[axis])
    return jnp.pad(a, pad)


def _rot_cols(w):
    half = w.shape[-1] // 2
    return jnp.concatenate([-w[..., half:], w[..., :half]], axis=-1)


def _head_blocks(cols_per_head):
    out = []
    for pieces in cols_per_head:
        k = pieces[0][0].shape[0]
        blk = jnp.zeros((k, LANE), F32)
        for arr, off in pieces:
            blk = blk.at[:, off:off + arr.shape[1]].set(arr)
        out.append(blk)
    return jnp.concatenate(out, axis=1)


def _proj_weights(lp):
    w_in = lp['w_in']
    d = w_in.shape[0]
    o = 0
    parts = {}
    for name, n in (('q', MLA_Q_RANK), ('kv', MLA_KV_RANK), ('kr', MLA_ROPE), ('na', 3 * NA_HEADS * NA_HEAD_DIM),
                    ('hy', 3 * HY_WIDTH), ('lu', LRU_WIDTH), ('lg', LRU_WIDTH), ('gt', 4 * d)):
        parts[name] = w_in[:, o:o + n]
        o += n
    zeros = lambda n: jnp.zeros((d, n), F32)
    kr_blk = jnp.concatenate([zeros(MLA_NOPE), parts['kr'], zeros(LANE - MLA_NOPE - MLA_ROPE)], axis=1)
    krr_blk = jnp.concatenate([zeros(MLA_NOPE), _rot_cols(parts['kr']), zeros(LANE - MLA_NOPE - MLA_ROPE)], axis=1)
    na_scale = NA_HEAD_DIM ** -0.5
    na_cols = []
    for which in range(3):
        for hd in range(NA_HEADS):
            lo = (which * NA_HEADS + hd) * NA_HEAD_DIM
            blk = parts['na'][:, lo:lo + NA_HEAD_DIM] * (na_scale if which == 0 else 1.0)
            na_cols.append(_pad_to(blk, LANE, 1))
    w1 = jnp.concatenate([_pad_to(parts['q'], MLA_Q_PAD, 1), parts['kv'], kr_blk, krr_blk] + na_cols
                         + [parts['hy'], parts['lu'], parts['lg']], axis=1).astype(BF16)
    mla_scale = (MLA_NOPE + MLA_ROPE) ** -0.5
    wuq = _pad_to(lp['mla_w_uq'], MLA_Q_PAD, 0) * mla_scale
    dq = MLA_NOPE + MLA_ROPE
    wuq_main = _head_blocks([[(wuq[:, hd * dq:hd * dq + dq], 0)] for hd in range(MLA_HEADS)])
    wuq_rot = _head_blocks([[(_rot_cols(wuq[:, hd * dq + MLA_NOPE:hd * dq + dq]), MLA_NOPE)]
                            for hd in range(MLA_HEADS)])
    dkv = MLA_NOPE + MLA_V
    wukv = lp['mla_w_ukv']
    wk = _head_blocks([[(wukv[:, hd * dkv:hd * dkv + MLA_NOPE], 0)] for hd in range(MLA_HEADS)])
    wv = _head_blocks([[(wukv[:, hd * dkv + MLA_NOPE:hd * dkv + dkv], 0)] for hd in range(MLA_HEADS)])
    gq = _pad_to(lp['mla_g_q'].reshape(1, -1), MLA_Q_PAD, 1)
    gkv = lp['mla_g_kv'].reshape(1, -1)
    return dict(w1=w1, w_gate=(0.5 * parts['gt']).astype(BF16), gq=gq, gkv=gkv, wuq=wuq_main.astype(BF16),
                wuq_rot=wuq_rot.astype(BF16), wk=wk.astype(BF16), wv=wv.astype(BF16))


def _rope_tables(seq, n_tok):
    t = jnp.arange(seq, dtype=I32)
    row = (t // GRID_W).astype(F32)
    col = (t % GRID_W).astype(F32)
    n_axis = MLA_ROPE // 4
    inv_freq = ROPE_THETA ** (-jnp.arange(n_axis, dtype=F32) / n_axis)
    ang = jnp.concatenate([row[:, None] * inv_freq, col[:, None] * inv_freq], axis=-1)
    cos = jnp.concatenate([jnp.cos(ang), jnp.cos(ang)], axis=-1)
    sin = jnp.concatenate([jnp.sin(ang), jnp.sin(ang)], axis=-1)
    cos_t = jnp.ones((n_tok, LANE), F32).at[:seq, MLA_NOPE:MLA_NOPE + MLA_ROPE].set(cos)
    sin_t = jnp.zeros((n_tok, LANE), F32).at[:seq, MLA_NOPE:MLA_NOPE + MLA_ROPE].set(sin)
    return cos_t, sin_t


def _batch_group(b, cap=2):
    return max(g for g in (4, 2, 1) if g <= cap and b % g == 0)


MOE_BATCH_GROUP = 4


def _kind_map(n_lat_tiles):
    return lambda b, i: (b, jnp.where(i >= n_lat_tiles, 1, 0), 0, 0)


def _project(stream, nt, modtab, g1, pw, rope, n_lat_tiles):
    b, _, d = stream[0].shape
    cos_t, sin_t = rope
    group = _batch_group(b)
    full = lambda a: pl.BlockSpec(a.shape, lambda bb, i: (0,) * a.ndim)
    head_out = lambda: pl.BlockSpec((group, MLA_HEADS, TM, LANE), lambda bb, i: (bb, 0, i, 0))
    tok_out = lambda n: pl.BlockSpec((group, TM, n), lambda bb, i: (bb, i, 0))
    head_shape = jax.ShapeDtypeStruct((b, MLA_HEADS, nt, LANE), BF16)
    pair_out = lambda: pl.BlockSpec((group, MLA_HEADS // 2, TM, 2 * LANE), lambda bb, i: (bb, 0, i, 0))
    pair_shape = jax.ShapeDtypeStruct((b, MLA_HEADS // 2, nt, 2 * LANE), BF16)
    tok_shape = lambda n: jax.ShapeDtypeStruct((b, nt, n), F32)
    weights = (g1, pw['w1'], pw['gq'], pw['gkv'], pw['wuq'], pw['wuq_rot'], pw['wk'], pw['wv'])
    return pl.pallas_call(
        functools.partial(_proj_kernel, n_lat_tiles=n_lat_tiles),
        grid=(b // group, nt // TM),
        in_specs=_stream_specs(stream, group, n_lat_tiles)
                 + [pl.BlockSpec((group, 1, 8, d), _kind_map(n_lat_tiles))]
                 + [full(w) for w in weights]
                 + [pl.BlockSpec((TM, LANE), lambda bb, i: (i, 0))] * 2,
        out_specs=[head_out(), head_out(), head_out(), head_out(), head_out(), pair_out(),
                   tok_out(3 * HY_WIDTH), tok_out(LRU_WIDTH), tok_out(LRU_WIDTH)],
        out_shape=[head_shape, head_shape, head_shape, head_shape, head_shape, pair_shape,
                   tok_shape(3 * HY_WIDTH), tok_shape(LRU_WIDTH), tok_shape(LRU_WIDTH)],
        compiler_params=_cparams("parallel", "parallel"),
        name="input_projection",
    )(stream[0], stream[1], modtab, *weights, cos_t, sin_t)


SOFTMAX_DEN_LANE = 64


def _softmax_pv(parts, lane_off=0):
    m = None
    for s, _ in parts:
        mm = jnp.max(s, axis=-1, keepdims=True)
        m = mm if m is None else jnp.maximum(m, mm)
    acc = None
    for s, v in parts:
        o = _dot(jnp.exp(s - m).astype(BF16), v)
        acc = o if acc is None else acc + o
    den = lane_off + SOFTMAX_DEN_LANE
    return acc[:, lane_off:lane_off + LANE] / acc[:, den:den + 1]


def _mla_kernel(q_ref, k_ref, v_ref, o_ref, *, seq, n_lat_tiles):
    i = pl.program_id(1)
    nt = k_ref.shape[2]

    def attend(lo, hi):
        for hd in range(MLA_HEADS):
            s = _dot_nt(q_ref[0, hd], k_ref[0, hd, lo:hi, :])
            o = _softmax_pv([(s, v_ref[0, hd, lo:hi, :])])
            o_ref[0, :, hd * LANE:(hd + 1) * LANE] = o.astype(BF16)

    @pl.when(i < n_lat_tiles)
    def _():
        attend(0, nt)

    @pl.when(i >= n_lat_tiles)
    def _():
        attend(seq, nt)


def _mla_attention(q, k, v, seq, tiles_used):
    b, h, nt, _ = q.shape
    kv_spec = pl.BlockSpec((1, h, nt, LANE), lambda bb, i: (bb, 0, 0, 0))
    return pl.pallas_call(
        functools.partial(_mla_kernel, seq=seq, n_lat_tiles=seq // TM),
        grid=(b, tiles_used),
        in_specs=[pl.BlockSpec((1, h, TM, LANE), lambda bb, i: (bb, 0, i, 0)), kv_spec, kv_spec],
        out_specs=pl.BlockSpec((1, TM, h * LANE), lambda bb, i: (bb, i, 0)),
        out_shape=jax.ShapeDtypeStruct((b, tiles_used * TM, h * LANE), BF16),
        compiler_params=_cparams("parallel", "arbitrary"),
        name="mla_attention",
    )(q, k, v)


def _na_bias_tables(rpb, rows):
    n_blk = rows // NA_TILE_ROWS
    col = np.arange(GRID_W)
    c0 = np.clip(col - NA_WIN_C // 2, 0, GRID_W - NA_WIN_C)
    in_win = (col[None, :] >= c0[:, None]) & (col[None, :] < c0[:, None] + NA_WIN_C)
    dc = np.clip(col[None, :] - col[:, None], 1 - NA_WIN_C, NA_WIN_C - 1) + NA_WIN_C - 1
    rpb = rpb.astype(F32)
    tables = []
    for j in (0, 1, n_blk - 1):
        w0 = min(max(NA_TILE_ROWS * j - NA_WIN_R // 2, 0), rows - NA_KEY_ROWS)
        r = NA_TILE_ROWS * j + np.arange(NA_TILE_ROWS)
        kr = w0 + np.arange(NA_KEY_ROWS)
        r0 = np.clip(r - NA_WIN_R // 2, 0, rows - NA_WIN_R)
        row_ok = (kr[None, :] >= r0[:, None]) & (kr[None, :] < r0[:, None] + NA_WIN_R)
        dr = np.clip(kr[None, :] - r[:, None] + NA_WIN_R - 1, 0, 2 * NA_WIN_R - 2)
        oh_r = jnp.asarray(np.eye(2 * NA_WIN_R - 1, dtype=np.float32)[dr.reshape(-1)])
        oh_c = jnp.asarray(np.eye(2 * NA_WIN_C - 1, dtype=np.float32)[dc.reshape(-1)])
        bias = jnp.einsum('ar,hrc,bc->hab', oh_r, rpb, oh_c, precision=lax.Precision.HIGHEST)
        bias = bias.reshape(NA_HEADS, NA_TILE_ROWS, NA_KEY_ROWS, GRID_W, GRID_W)
        mask = row_ok[:, :, None, None] & in_win[None, None, :, :]
        bias = jnp.where(jnp.asarray(mask)[None], bias, -jnp.inf)
        tables.append(bias.transpose(0, 1, 3, 2, 4).reshape(NA_HEADS, TM, NA_KEYS))
    return jnp.stack(tables)


def _na_kernel(q_ref, k_ref, v_ref, bias_ref, o_ref, *, seq, n_lat_tiles):
    i = pl.program_id(1)
    nt = k_ref.shape[2]
    rows = seq // GRID_W

    @pl.when(i < n_lat_tiles)
    def _():
        w0 = jnp.clip(NA_TILE_ROWS * i - NA_WIN_R // 2, 0, rows - NA_KEY_ROWS)
        start = pl.multiple_of(w0 * GRID_W, GRID_W)
        for hd in range(NA_HEADS):
            q = q_ref[0, hd]
            s_loc = _dot_nt(q, k_ref[0, hd, pl.ds(start, NA_KEYS), :]) + bias_ref[0, hd]
            s_ctx = _dot_nt(q, k_ref[0, hd, seq:nt, :])
            o = _softmax_pv([(s_loc, v_ref[0, hd // 2, pl.ds(start, NA_KEYS), :]),
                             (s_ctx, v_ref[0, hd // 2, seq:nt, :])], (hd % 2) * LANE)
            o_ref[0, :, hd * LANE:(hd + 1) * LANE] = o.astype(BF16)

    @pl.when(i >= n_lat_tiles)
    def _():
        for hd in range(NA_HEADS):
            s = _dot_nt(q_ref[0, hd], k_ref[0, hd, seq:nt, :])
            o = _softmax_pv([(s, v_ref[0, hd // 2, seq:nt, :])], (hd % 2) * LANE)
            o_ref[0, :, hd * LANE:(hd + 1) * LANE] = o.astype(BF16)


def _na_attention(q, k, v, bias, seq, tiles_used):
    b, h, nt, _ = q.shape
    n_lat = seq // TM
    kv_spec = pl.BlockSpec((1, h, nt, LANE), lambda bb, i: (bb, 0, 0, 0))
    cfg = lambda bb, i: (jnp.where(i == 0, 0, jnp.where(i >= n_lat - 1, 2, 1)), 0, 0, 0)
    return pl.pallas_call(
        functools.partial(_na_kernel, seq=seq, n_lat_tiles=n_lat),
        grid=(b, tiles_used),
        in_specs=[pl.BlockSpec((1, h, TM, LANE), lambda bb, i: (bb, 0, i, 0)), kv_spec,
                  pl.BlockSpec((1, h // 2, nt, 2 * LANE), lambda bb, i: (bb, 0, 0, 0)),
                  pl.BlockSpec((1, h, TM, NA_KEYS), cfg)],
        out_specs=pl.BlockSpec((1, TM, h * LANE), lambda bb, i: (bb, i, 0)),
        out_shape=jax.ShapeDtypeStruct((b, tiles_used * TM, h * LANE), BF16),
        compiler_params=_cparams("parallel", "arbitrary"),
        name="neighbourhood_attention",
    )(q, k, v, bias)


def _hp_dot(a, b):
    return jnp.dot(a, b, preferred_element_type=F32, precision=lax.Precision.HIGHEST)


def _hy_filter_kernel(f_ref, w1_ref, b1_ref, w2_ref, b2_ref, w3_ref, dec_ref, o_ref, obf_ref):
    f = f_ref[...]
    h = jnp.sin(_hp_dot(f, w1_ref[...]) + b1_ref[...])
    h = jnp.sin(_hp_dot(h, w2_ref[...]) + b2_ref[...])
    h = _hp_dot(h, w3_ref[...])
    h = h * jnp.exp(-f[:, 0:1] * jnp.abs(dec_ref[...]))
    o_ref[...] = h
    obf_ref[...] = h.astype(BF16)


def _hy_pos_features(length):
    t = jnp.linspace(0.0, 1.0, length, dtype=F32)[:, None]
    w = 2.0 * math.pi * jnp.arange(length, dtype=F32)[:, None] / length
    f = jnp.linspace(1e-4, HY_BANDS - 1, HY_BANDS, dtype=F32)[None, :]
    z = w * f
    return jnp.concatenate([t, jnp.cos(z), -jnp.sin(z)], axis=-1)


DFT_ROW_STEP = 64
DFT_ROWS = 256
DFT_FWD_COLS = 1024
DFT_INV_COLS = 512
FILTER_ROWS = 512


def _col_block(nc, cap):
    return min(nc, cap)


def _hyp_pre_kernel(z_ref, w_ref, b_ref, lat_o, lat_bf_o, ctx_o, ctx_bf_o, *, seq, ctx_len):
    w = w_ref[...]
    bias = b_ref[...]
    for lo, length, o_ref, obf_ref in ((0, seq, lat_o, lat_bf_o), (seq, ctx_len, ctx_o, ctx_bf_o)):
        h = length // 2
        even = z_ref[0, pl.ds(lo, h, stride=2), :]
        odd = z_ref[0, pl.ds(lo + 1, h, stride=2), :]
        row = lax.broadcasted_iota(I32, even.shape, 0)
        odd_prev = jnp.where(row == 0, 0.0, pltpu.roll(odd, 1, axis=0))
        even_next = jnp.where(row == h - 1, 0.0, pltpu.roll(even, h - 1, axis=0))
        y_even = bias + w[0:1] * odd_prev + w[1:2] * even + w[2:3] * odd
        y_odd = bias + w[0:1] * even + w[1:2] * odd + w[2:3] * even_next
        for r, y in enumerate((y_even, y_odd)):
            o_ref[0, r] = y
            obf_ref[0, r] = y.astype(BF16)


def _hyp_pre(hy, w, bvec, seq):
    b, nt, _ = hy.shape
    ctx_len = nt - seq
    wpad = _pad_to(w, 8, 0)
    per_plane = HY_WIDTH // LANE
    out_specs, out_shape = [], []
    for length in (seq, ctx_len):
        for dt in (F32, BF16):
            out_specs.append(pl.BlockSpec((1, 2, length // 2, LANE),
                                          lambda bb, g: (g // per_plane, 0, 0, bb * per_plane + g % per_plane)))
            out_shape.append(jax.ShapeDtypeStruct((3, 2, length // 2, b * HY_WIDTH), dt))
    return pl.pallas_call(
        functools.partial(_hyp_pre_kernel, seq=seq, ctx_len=ctx_len),
        grid=(b, 3 * per_plane),
        in_specs=[pl.BlockSpec((1, nt, LANE), lambda bb, g: (bb, 0, g)),
                  pl.BlockSpec((8, LANE), lambda bb, g: (0, g)),
                  pl.BlockSpec((1, LANE), lambda bb, g: (0, g))],
        out_specs=out_specs,
        out_shape=out_shape,
        compiler_params=_cparams("parallel", "parallel"),
        name="hyena_short_conv",
    )(hy, wpad, bvec.reshape(1, -1))


def _hyp_filters(length, lp):
    feats = _hy_pos_features(length)
    feats = feats.reshape(length // 2, 2, HY_EMB).transpose(1, 0, 2).reshape(length, HY_EMB)
    n_out = HY_ORDER * 2 * HY_WIDTH
    tl = min(length // 2, FILTER_ROWS)
    full = lambda a: pl.BlockSpec(a.shape, lambda i: (0,) * a.ndim)
    args = (lp['hy_w1'], lp['hy_b1'].reshape(1, -1), lp['hy_w2'], lp['hy_b2'].reshape(1, -1), lp['hy_w3'],
            lp['hy_decay'].reshape(1, n_out))
    filt, filt_bf = pl.pallas_call(
        _hy_filter_kernel,
        grid=(length // tl,),
        in_specs=[pl.BlockSpec((tl, HY_EMB), lambda i: (i, 0))] + [full(a) for a in args],
        out_specs=[pl.BlockSpec((tl, n_out), lambda i: (i, 0))] * 2,
        out_shape=[jax.ShapeDtypeStruct((length, n_out), F32), jax.ShapeDtypeStruct((length, n_out), BF16)],
        compiler_params=_cparams("parallel"),
        name="hyena_filter_mlp",
    )(feats, *args)
    return filt.reshape(2, length // 2, n_out), filt_bf.reshape(2, length // 2, n_out)


def _hyp_matrix_kernel(c1_ref, s1_ref, c2_ref, s2_ref, ck_ref, sk_ref, ckr_ref, skr_ref,
                       ce_o, se_o, co_o, so_o, set_o, cot_o, sot_o):
    c1, s1 = c1_ref[0], s1_ref[0]
    c2, s2 = c2_ref[...], s2_ref[...]
    cos = c1 * c2 - s1 * s2
    sin = s1 * c2 + c1 * s2
    row = lax.broadcasted_iota(I32, cos.shape, 0)
    col = lax.broadcasted_iota(I32, cos.shape, 1)
    first_row = (row + pl.program_id(0) * DFT_ROW_STEP) == 0
    alt_col = jnp.where(col % 2 == 0, 1.0, -1.0)
    alt_row = jnp.where(row % 2 == 0, 1.0, -1.0)
    ck, sk = ck_ref[...], sk_ref[...]
    ce_o[...] = cos.astype(BF16)
    se_o[...] = jnp.where(first_row, alt_col, -sin).astype(BF16)
    co_o[...] = (cos * ck - sin * sk).astype(BF16)
    so_o[...] = jnp.where(first_row, alt_col, -(sin * ck + cos * sk)).astype(BF16)
    ckr, skr = ckr_ref[...], skr_ref[...]
    set_o[...] = jnp.where(col == 0, alt_row, -sin).astype(BF16)
    cot_o[...] = (cos * ckr - sin * skr).astype(BF16)
    sot_o[...] = jnp.where(col == 0, alt_row, -(sin * ckr + cos * skr)).astype(BF16)


def _hyp_matrices(length):
    h = length // 2
    step = DFT_ROW_STEP
    m = jnp.arange(h, dtype=I32)

    def trig(kv):
        ang = ((kv[:, None] * m[None, :]) % length).astype(F32) * (2.0 * math.pi / length)
        return jnp.cos(ang), jnp.sin(ang)

    c1, s1 = trig(jnp.arange(h // step, dtype=I32) * step)
    c2, s2 = trig(jnp.arange(step, dtype=I32))
    half_angle = m.astype(F32) * (math.pi / length)
    ck, sk = jnp.cos(half_angle), jnp.sin(half_angle)
    coarse = pl.BlockSpec((1, 1, h), lambda j: (j, 0, 0))
    fine = pl.BlockSpec((step, h), lambda j: (0, 0))
    per_row = pl.BlockSpec((step, 1), lambda j: (j, 0))
    per_col = pl.BlockSpec((1, h), lambda j: (0, 0))
    out = pl.BlockSpec((step, h), lambda j: (j, 0))
    ce, se, co, so, se_t, co_t, so_t = pl.pallas_call(
        _hyp_matrix_kernel,
        grid=(h // step,),
        in_specs=[coarse, coarse, fine, fine, per_row, per_row, per_col, per_col],
        out_specs=[out] * 7,
        out_shape=[jax.ShapeDtypeStruct((h, h), BF16)] * 7,
        compiler_params=_cparams("parallel"),
        name="dft_matrices",
    )(c1[:, None, :], s1[:, None, :], c2, s2, ck[:, None], sk[:, None], ck[None, :], sk[None, :])
    return dict(fwd=(ce, se, co, so), inv=(ce, se_t, co_t, so_t))


def _hyp_fwd_kernel(ce_ref, se_ref, co_ref, so_ref, xe_ref, xo_ref, *rest, with_taps):
    xe, xo = xe_ref[...], xo_ref[...]
    a_re, a_im = _dot(ce_ref[...], xe), _dot(se_ref[...], xe)
    b_re, b_im = _dot(co_ref[...], xo), _dot(so_ref[...], xo)
    if not with_taps:
        for ref, val in zip(rest, (a_re, a_im, b_re, b_im)):
            ref[...] = val
        return
    t1re_ref, t1im_ref, t2re_ref, t2im_ref, sp_ref, gere_o, geim_o, gore_o, goim_o = rest
    t1re, t1im, t2re, t2im = t1re_ref[...], t1im_ref[...], t2re_ref[...], t2im_ref[...]
    sp = sp_ref[...]
    first_block = pl.program_id(1) == 0
    row8 = lax.broadcasted_iota(I32, (8, HY_WIDTH), 0)
    for bb in range(xe.shape[1] // HY_WIDTH):
        sl = slice(bb * HY_WIDTH, (bb + 1) * HY_WIDTH)
        are, aim, bre, bim = a_re[:, sl], a_im[:, sl], b_re[:, sl], b_im[:, sl]
        u1re, u1im = are + bre, aim + bim
        u2re, u2im = are - bre, bim - aim
        z1re, z1im = u1re * t1re - u1im * t1im, u1re * t1im + u1im * t1re
        z2re, z2im = u2re * t2re - u2im * t2im, u2re * t2im + u2im * t2re
        gere_o[:, sl] = (z1re + z2re).astype(BF16)
        geim_o[:, sl] = (z1im - z2im).astype(BF16)
        gore_o[:, sl] = (z1re - z2re).astype(BF16)
        goim_o[:, sl] = (z1im + z2im).astype(BF16)

        @pl.when(first_block)
        def _():
            u0, ul = u1re[0:8], u2re[0:8]
            a_s, b_s = aim[0:8], bim[0:8]
            dc, ny, mre, mim = sp[0:1], sp[1:2], sp[2:3], sp[3:4]
            first = row8 == 0
            gere_o[0:8, sl] = jnp.where(first, u0 * dc + ul * ny, (z1re + z2re)[0:8]).astype(BF16)
            gore_o[0:8, sl] = jnp.where(first, u0 * dc - ul * ny, (z1re - z2re)[0:8]).astype(BF16)
            geim_o[0:8, sl] = jnp.where(first, a_s * mre + b_s * mim, (z1im - z2im)[0:8]).astype(BF16)
            goim_o[0:8, sl] = jnp.where(first, b_s * mre - a_s * mim, (z1im + z2im)[0:8]).astype(BF16)


def _hyp_fwd(mats, x, plane, taps=None):
    _, _, h, nc = x.shape
    tk = min(h, DFT_ROWS)
    cb = _col_block(nc, DFT_FWD_COLS)
    grid = (nc // cb, h // tk)
    m_spec = pl.BlockSpec((tk, h), lambda c, j: (j, 0))
    x_spec = lambda r: pl.BlockSpec((None, None, h, cb), lambda c, j: (plane, r, 0, c))
    o_spec = pl.BlockSpec((tk, cb), lambda c, j: (j, c))
    if taps is None:
        return pl.pallas_call(
            functools.partial(_hyp_fwd_kernel, with_taps=False),
            grid=grid, in_specs=[m_spec] * 4 + [x_spec(0), x_spec(1)], out_specs=[o_spec] * 4,
            out_shape=[jax.ShapeDtypeStruct((h, nc), F32)] * 4,
            compiler_params=_cparams("parallel", "arbitrary"),
            name="hyena_dft_filters",
        )(*mats['fwd'], x, x)
    t_spec = pl.BlockSpec((tk, HY_WIDTH), lambda c, j: (j, 0))
    sp_spec = pl.BlockSpec((8, HY_WIDTH), lambda c, j: (0, 0))
    return pl.pallas_call(
        functools.partial(_hyp_fwd_kernel, with_taps=True),
        grid=grid, in_specs=[m_spec] * 4 + [x_spec(0), x_spec(1)] + [t_spec] * 4 + [sp_spec],
        out_specs=[o_spec] * 4,
        out_shape=[jax.ShapeDtypeStruct((h, nc), BF16)] * 4,
        compiler_params=_cparams("parallel", "arbitrary"),
        name="hyena_dft_forward",
    )(*mats['fwd'], x, x, *taps)


def _hyp_inv_kernel(ce_ref, set_ref, cot_ref, sot_ref, gere_ref, geim_ref, gore_ref, goim_ref,
                    gate_ref, prev_ref, bias_ref, *outs, last):
    conv = (_dot(ce_ref[...], gere_ref[...]) + _dot(set_ref[...], geim_ref[...]),
            _dot(cot_ref[...], gore_ref[...]) + _dot(sot_ref[...], goim_ref[...]))
    bias = bias_ref[...]
    for r in range(2):
        y = gate_ref[r] * (conv[r] + prev_ref[r] * bias)
        if last:
            (tok_o,) = outs
            for bb in range(y.shape[1] // HY_WIDTH):
                tok_o[bb, :, r * HY_WIDTH:(r + 1) * HY_WIDTH] = y[:, bb * HY_WIDTH:(bb + 1) * HY_WIDTH].astype(BF16)
        else:
            y_o, ybf_o = outs
            y_o[r] = y
            ybf_o[r] = y.astype(BF16)


def _hyp_inv(mats, g, gate, gate_plane, prev, prev_plane, bias_row, last):
    h, nc = g[0].shape
    tm = min(h, DFT_ROWS)
    cb = _col_block(nc, DFT_INV_COLS)
    grid = (nc // cb, h // tm)
    m_spec = pl.BlockSpec((tm, h), lambda c, i: (i, 0))
    g_spec = pl.BlockSpec((h, cb), lambda c, i: (0, c))
    e_spec = lambda plane: pl.BlockSpec((None, 2, tm, cb), lambda c, i: (plane, 0, i, c))
    b_spec = pl.BlockSpec((1, cb), lambda c, i: (0, c))
    if last:
        out_specs = [pl.BlockSpec((cb // HY_WIDTH, tm, 2 * HY_WIDTH), lambda c, i: (c, i, 0))]
        out_shape = [jax.ShapeDtypeStruct((nc // HY_WIDTH, h, 2 * HY_WIDTH), BF16)]
    else:
        out_specs = [e_spec(0), e_spec(0)]
        out_shape = [jax.ShapeDtypeStruct((1, 2, h, nc), F32), jax.ShapeDtypeStruct((1, 2, h, nc), BF16)]
    return pl.pallas_call(
        functools.partial(_hyp_inv_kernel, last=last),
        grid=grid, in_specs=[m_spec] * 4 + [g_spec] * 4 + [e_spec(gate_plane), e_spec(prev_plane), b_spec],
        out_specs=out_specs, out_shape=out_shape,
        compiler_params=_cparams("parallel", "arbitrary"),
        name="hyena_dft_inverse",
    )(*mats['inv'], *g, gate, prev, bias_row)


def _hyp_tap_tables(spec, filt, length):
    a_re, a_im, b_re, b_im = spec
    w = HY_WIDTH
    inv_n = 1.0 / (2 * length)
    tables = []
    for o in range(HY_ORDER):
        f_sl = slice((2 * o) * w, (2 * o + 1) * w)
        r_sl = slice((2 * o + 1) * w, (2 * o + 2) * w)
        hb0 = filt[0, 0:1, r_sl]
        f1re = (a_re + b_re)[:, f_sl] + (a_re + b_re)[:, r_sl] - hb0
        f1im = (a_im + b_im)[:, f_sl] - (a_im + b_im)[:, r_sl]
        f2re = (a_re - b_re)[:, f_sl] + (a_re - b_re)[:, r_sl] - hb0
        f2im = (b_im - a_im)[:, f_sl] - (b_im - a_im)[:, r_sl]
        dc = f1re[0:1]
        ny = f2re[0:1]
        mid_re = a_im[0:1, f_sl] + a_im[0:1, r_sl] - hb0
        mid_im = -b_im[0:1, f_sl] + b_im[0:1, r_sl]
        sp = jnp.concatenate([dc * inv_n, ny * inv_n, mid_re * (2 * inv_n), mid_im * (2 * inv_n),
                              jnp.zeros((4, w), F32)], axis=0)
        tables.append((f1re * (2 * inv_n), f1im * (2 * inv_n), f2re * (2 * inv_n), f2im * (2 * inv_n), sp))
    return tables


def _hyena_seq(mats, vx, vx_bf, lp, n_batch):
    h = vx.shape[2]
    length = 2 * h
    filt, filt_bf = _hyp_filters(length, lp)
    spec = _hyp_fwd(mats, filt_bf[None], 0)
    tables = _hyp_tap_tables(spec, filt, length)
    bias = lp['hy_bias'].astype(F32)
    y, y_bf = vx, vx_bf
    for o in range(HY_ORDER):
        g = _hyp_fwd(mats, y_bf, 0, tables[o])
        bias_row = jnp.tile(bias[o][None, :], (1, n_batch))
        last = o == HY_ORDER - 1
        res = _hyp_inv(mats, g, vx, o + 1, y, 0, bias_row, last)
        if last:
            return res[0].reshape(n_batch, length, HY_WIDTH)
        y, y_bf = res


def _lru_kernel(u_ref, g_ref, cw_ref, cb_ref, wa_ref, ba_ref, wx_ref, bx_ref, lam_ref, o_ref,
                pad_ref, y_ref, *, seq, ctx_len):
    tc = LRU_CHUNK
    halo = LRU_HALO
    width = LRU_WIDTH
    lat_off = halo
    ctx_off = 2 * halo + seq
    zero = jnp.zeros((halo, width), F32)
    pad_ref[0:halo, :] = zero
    pad_ref[lat_off:lat_off + seq, :] = u_ref[0, 0:seq, :]
    pad_ref[lat_off + seq:ctx_off, :] = zero
    pad_ref[ctx_off:ctx_off + ctx_len, :] = u_ref[0, seq:seq + ctx_len, :]
    pad_ref[ctx_off + ctx_len:ctx_off + ctx_len + halo, :] = zero
    row = lax.broadcasted_iota(I32, (tc, width), 0)
    n_win = tc + 2 * halo

    def chunk(pad_off, y_off, s, carry, d):
        wstart = pl.multiple_of(pad_off + s - halo, 8)
        win = pad_ref[pl.ds(wstart, n_win), :]
        cw = cw_ref[d]
        xc = cb_ref[d]
        for k in range(LRU_CONV):
            shift = (LRU_CONV - 1 - k) if d == 0 else -k
            rolled = win if shift == 0 else pltpu.roll(win, shift % n_win, axis=0)
            xc = xc + cw[k:k + 1] * rolled[halo:halo + tc]
        xb = xc.astype(BF16)
        r = _sigmoid(_dot(xb, wa_ref[d]) + ba_ref[d])
        gi = _sigmoid(_dot(xb, wx_ref[d]) + bx_ref[d])
        lam = lam_ref[d]
        softplus = jnp.maximum(-lam, 0.0) + jnp.log1p(jnp.exp(-jnp.abs(lam)))
        log_a = -LRU_C * r * softplus
        a = jnp.exp(log_a)
        bt = jnp.sqrt(-jnp.tanh(log_a) * (a * a + 1.0)) * (gi * xc)
        sft = 1
        while sft < tc:
            if d == 0:
                keep = row >= sft
                a_s = jnp.where(keep, pltpu.roll(a, sft, axis=0), 1.0)
                b_s = jnp.where(keep, pltpu.roll(bt, sft, axis=0), 0.0)
            else:
                keep = row < tc - sft
                a_s = jnp.where(keep, pltpu.roll(a, tc - sft, axis=0), 1.0)
                b_s = jnp.where(keep, pltpu.roll(bt, tc - sft, axis=0), 0.0)
            bt = a * b_s + bt
            a = a * a_s
            sft *= 2
        h = a * carry + bt
        yo = pl.multiple_of(y_off + s, 8)
        if d == 0:
            y_ref[pl.ds(yo, tc), :] = h
            return h[tc - 1:tc]
        y_ref[pl.ds(yo, tc), :] = y_ref[pl.ds(yo, tc), :] + h
        return h[0:1]

    n_lat = seq // tc
    n_ctx = ctx_len // tc
    for d in range(2):
        carry = jnp.zeros((1, width), F32)
        order = range(n_ctx) if d == 0 else range(n_ctx - 1, -1, -1)
        for c in order:
            carry = chunk(ctx_off, seq, c * tc, carry, d)

        def body(j, cr, d=d):
            jj = j if d == 0 else n_lat - 1 - j
            return chunk(lat_off, 0, jj * tc, cr, d)

        lax.fori_loop(0, n_lat, body, carry)
    o_ref[0] = (y_ref[...] * _gelu_tanh(g_ref[0])).astype(BF16)


def _block_diag(w):
    nd, nb, c, _ = w.shape
    out = jnp.zeros((nd, nb * c, nb * c), w.dtype)
    for n in range(nb):
        out = out.at[:, n * c:(n + 1) * c, n * c:(n + 1) * c].set(w[:, n])
    return out


def _lru_mixer(lu, lg, lp, seq):
    b, nt, w = lu.shape
    ctx_len = nt - seq
    row3 = lambda a: a.reshape(2, 1, w)
    args = (_pad_to(lp['lru_conv_w'], 8, 1), row3(lp['lru_conv_b']), _block_diag(lp['lru_wa']).astype(BF16),
            row3(lp['lru_ba']), _block_diag(lp['lru_wx']).astype(BF16), row3(lp['lru_bx']), row3(lp['lru_lambda']))
    full = lambda a: pl.BlockSpec(a.shape, lambda bb: (0,) * a.ndim)
    tok = pl.BlockSpec((1, nt, w), lambda bb: (bb, 0, 0))
    return pl.pallas_call(
        functools.partial(_lru_kernel, seq=seq, ctx_len=ctx_len),
        grid=(b,),
        in_specs=[tok, tok] + [full(a) for a in args],
        out_specs=tok,
        out_shape=jax.ShapeDtypeStruct((b, nt, w), BF16),
        scratch_shapes=[pltpu.VMEM((nt + 3 * LRU_HALO, w), F32), pltpu.VMEM((nt, w), F32)],
        compiler_params=_cparams("parallel"),
        name="rglru_scan",
    )(lu, lg, *args)


def _merge_kernel(xl_ref, xc_ref, mod_ref, g1_ref, a_ref, b_ref, c_ref, d_ref, wg_ref, wa_ref, wb_ref, wc_ref,
                  wd_ref, wo_ref, o_ref, *, n_lat_tiles):
    group, tm, dm = xl_ref.shape
    m = mod_ref[:, 0]
    x = _stream_tile(xl_ref, xc_ref, n_lat_tiles)
    h = _normmod(x, g1_ref[...], m[:, 1:2], m[:, 0:1]).reshape(group * tm, dm).astype(BF16)
    acc = None
    for k, (br, w) in enumerate(((a_ref, wa_ref), (b_ref, wb_ref), (c_ref, wc_ref), (d_ref, wd_ref))):
        term = (1.0 + jnp.tanh(_dot(h, wg_ref[:, k * dm:(k + 1) * dm]))) * _dot(
            br[...].reshape(group * tm, br.shape[-1]), w[...])
        acc = term if acc is None else acc + term
    y = _dot(acc.astype(BF16), wo_ref[...])
    o_ref[...] = x + m[:, 2:3] * y.reshape(group, tm, dm)


def _merge(stream, modtab, g1, branches, w_gate, lp, n_lat_tiles, tiles_used):
    b, _, d = stream[0].shape
    wbr = lp['w_branch']
    head_rows = lambda w, dv: jnp.concatenate(
        [_pad_to(w[hd * dv:(hd + 1) * dv], LANE, 0) for hd in range(4)], axis=0)
    wbr = 0.5 * wbr
    weights = (head_rows(wbr[0], MLA_V).astype(BF16), head_rows(wbr[1], NA_HEAD_DIM).astype(BF16),
               wbr[2].astype(BF16), wbr[3].astype(BF16), lp['w_out'].astype(BF16))
    group = _batch_group(b)
    full = lambda a: pl.BlockSpec(a.shape, lambda bb, i: (0,) * a.ndim)
    tok = lambda n: pl.BlockSpec((group, TM, n), lambda bb, i: (bb, i, 0))
    return pl.pallas_call(
        functools.partial(_merge_kernel, n_lat_tiles=n_lat_tiles),
        grid=(b // group, tiles_used),
        in_specs=_stream_specs(stream, group, n_lat_tiles)
                 + [pl.BlockSpec((group, 1, 8, d), _kind_map(n_lat_tiles)), full(g1)]
                 + [tok(br.shape[-1]) for br in branches] + [full(w_gate)] + [full(w) for w in weights],
        out_specs=tok(d),
        out_shape=jax.ShapeDtypeStruct((b, tiles_used * TM, d), F32),
        compiler_params=_cparams("parallel", "parallel"),
        name="merge_branches",
    )(stream[0], stream[1], modtab, g1, *branches, w_gate, *weights)


U32 = jnp.uint32
EXPERT_ROWS = 896


def _pack_pair(x):
    n = x.shape[-1] // 2
    hi = lax.bitcast_convert_type(x[:, :n].astype(BF16).astype(F32), U32)
    lo = lax.bitcast_convert_type(x[:, n:].astype(BF16).astype(F32), U32)
    return hi | (lo >> 16)


def _unpack_pair(p):
    hi = lax.bitcast_convert_type(p & jnp.uint32(0xFFFF0000), F32)
    lo = lax.bitcast_convert_type(p << 16, F32)
    return hi, lo


def _router_kernel(x_ref, mod_ref, g2_ref, rw_ref, rb_ref, tri_ref, h2_o, idx_o, wts_o, rank_o, cnt_o, carry):
    i = pl.program_id(0)

    @pl.when(i == 0)
    def _():
        carry[...] = jnp.zeros_like(carry)

    group, tm, d = x_ref.shape
    m = mod_ref[:, 0]
    h2 = _normmod(x_ref[...], g2_ref[...], m[:, 4:5], m[:, 3:4]).reshape(group * tm, d)
    half = h2.shape[-1] // 2
    h2_o[0] = _pack_pair(h2[:, :half])
    h2_o[1] = _pack_pair(h2[:, half:])
    h_hi = h2.astype(BF16)
    h_lo = (h2 - h_hi.astype(F32)).astype(BF16)
    logits = _dot_nt(rw_ref[0], h_hi) + _dot_nt(rw_ref[0], h_lo) + _dot_nt(rw_ref[1], h_hi)
    scores = _sigmoid(logits)
    biased = scores + rb_ref[...]
    expert = lax.broadcasted_iota(I32, scores.shape, 0)
    picks = []
    onehot_all = jnp.zeros(scores.shape, F32)
    for _ in range(TOP_K):
        best = jnp.max(biased, axis=0, keepdims=True)
        arg = jnp.min(jnp.where(biased == best, expert, N_EXPERTS), axis=0, keepdims=True)
        hit = expert == arg
        sel = jnp.sum(jnp.where(hit, scores, 0.0), axis=0, keepdims=True)
        biased = jnp.where(hit, -jnp.inf, biased)
        onehot_all = onehot_all + jnp.where(hit, 1.0, 0.0)
        picks.append((arg, hit, sel))
    total = picks[0][2]
    for _, _, sel in picks[1:]:
        total = total + sel
    earlier = _dot(onehot_all.astype(BF16), tri_ref[...]) + carry[...]
    pad_rows = TOPK_PAD - TOP_K
    ranks = [jnp.sum(jnp.where(hit, earlier, 0.0), axis=0, keepdims=True).astype(I32) for _, hit, _ in picks]
    scale = ROUTED_SCALE / total
    n_tok = scores.shape[1]
    idx_o[...] = jnp.concatenate([arg for arg, _, _ in picks] + [jnp.zeros((pad_rows, n_tok), I32)], axis=0)
    wts_o[...] = jnp.concatenate([sel * scale for _, _, sel in picks] + [jnp.zeros((pad_rows, n_tok), F32)], axis=0)
    rank_o[...] = jnp.concatenate(ranks + [jnp.zeros((pad_rows, n_tok), I32)], axis=0)
    carry[...] = carry[...] + jnp.sum(onehot_all, axis=1, keepdims=True)
    cnt_o[...] = carry[...]


def _tile_maps(n_groups, n_lat_tiles):
    tok = lambda f: (f % n_groups, f // n_groups, 0)
    mod = lambda f: (f % n_groups, jnp.where(f // n_groups >= n_lat_tiles, 1, 0), 0, 0)
    return tok, mod


def _route(x1, modtab, g2, lp, n_lat_tiles, tiles_used):
    b, _, d = x1.shape
    group = _batch_group(b, MOE_BATCH_GROUP)
    rows = group * TM
    t = b * tiles_used * TM
    rw = lp['router_w'].T
    rw_hi = rw.astype(BF16)
    rw = jnp.stack([rw_hi, (rw - rw_hi.astype(F32)).astype(BF16)])
    rb = lp['router_bias'].reshape(-1, 1)
    tri = (np.arange(rows)[:, None] < np.arange(rows)[None, :]).astype(np.float32)
    tri = jnp.asarray(tri, BF16)
    per_tok = lambda: pl.BlockSpec((TOPK_PAD, rows), lambda i: (0, i))
    full = lambda a: pl.BlockSpec(a.shape, lambda i: (0,) * a.ndim)
    tok, mod = _tile_maps(b // group, n_lat_tiles)
    return pl.pallas_call(
        _router_kernel,
        grid=(t // rows,),
        in_specs=[pl.BlockSpec((group, TM, d), tok), pl.BlockSpec((group, 1, 8, d), mod),
                  full(g2), full(rw), full(rb), full(tri)],
        out_specs=[pl.BlockSpec((2, rows, d // 4), lambda i: (0, i, 0)), per_tok(), per_tok(), per_tok(),
                   pl.BlockSpec((N_EXPERTS, 1), lambda i: (0, 0))],
        out_shape=[jax.ShapeDtypeStruct((2, t, d // 4), U32), jax.ShapeDtypeStruct((TOPK_PAD, t), I32),
                   jax.ShapeDtypeStruct((TOPK_PAD, t), F32), jax.ShapeDtypeStruct((TOPK_PAD, t), I32),
                   jax.ShapeDtypeStruct((N_EXPERTS, 1), F32)],
        scratch_shapes=[pltpu.VMEM((N_EXPERTS, 1), F32)],
        compiler_params=_cparams("arbitrary"),
        name="moe_router",
    )(x1, modtab, g2, rw, rb, tri)


SC_WINDOW = 128


def _sc_mesh():
    return plsc.VectorSubcoreMesh(core_axis_name="c", subcore_axis_name="s")


def _sc_scatter_rows(src, idx, n_out):
    n, width = src.shape
    k_rep = idx.shape[0]
    half = n // SC_WINDOW // 2

    @functools.partial(pl.kernel, out_type=jax.ShapeDtypeStruct((n_out, width), src.dtype), mesh=_sc_mesh(),
                       scratch_types=[], name="moe_dispatch_sc")
    def scatter(src_hbm, idx_hbm, out_hbm):
        def body(x_vmem, *i_vmems):
            for i_vmem in i_vmems:
                pltpu.sync_copy(x_vmem, out_hbm.at[i_vmem.at[0]])

        pltpu.emit_pipeline(
            body,
            grid=(2, half),
            in_specs=[pl.BlockSpec((SC_WINDOW, width), lambda a, i: (a * half + i, 0))]
                     + [pl.BlockSpec((1, SC_WINDOW), lambda a, i, k=k: (k, a * half + i)) for k in range(k_rep)],
            out_specs=[],
            core_axis_name=("c", "s"),
            dimension_semantics=(pltpu.PARALLEL, pltpu.PARALLEL),
        )(src_hbm, *([idx_hbm] * k_rep))

    return scatter(src, idx)


def _sc_gather_rows(src, idx):
    k_rep, n = idx.shape
    width = src.shape[1]
    n_win = n // SC_WINDOW

    @functools.partial(pl.kernel, out_type=jax.ShapeDtypeStruct((k_rep * n, width), src.dtype), mesh=_sc_mesh(),
                       scratch_types=[], name="moe_gather_sc")
    def gather(src_hbm, idx_hbm, out_hbm):
        def body(i_vmem, o_vmem):
            pltpu.sync_copy(src_hbm.at[i_vmem.at[0]], o_vmem)

        pltpu.emit_pipeline(
            body,
            grid=(k_rep, n_win),
            in_specs=[pl.BlockSpec((1, SC_WINDOW), lambda k, i: (k, i))],
            out_specs=[pl.BlockSpec((SC_WINDOW, width), lambda k, i: (k * n_win + i, 0))],
            core_axis_name=("c", "s"),
            dimension_semantics=(pltpu.PARALLEL, pltpu.PARALLEL),
        )(idx_hbm, out_hbm)

    return gather(src, idx)


def _unpack_planes(p0, p1):
    return _unpack_pair(p0) + _unpack_pair(p1)


def _dot_quarters(parts, w_ref):
    q = parts[0].shape[-1]
    acc = None
    for j, part in enumerate(parts):
        term = _dot(part.astype(BF16), w_ref[j * q:(j + 1) * q, :])
        acc = term if acc is None else acc + term
    return acc


def _expert_kernel(be_ref, nv_ref, blk_ref, xs_ref, wg_ref, wu_ref, wd_ref, ys_o, wg_s, wu_s, wd_s):
    del blk_ref
    i = pl.program_id(0)
    prev = be_ref[jnp.maximum(i - 1, 0)]

    @pl.when((i == 0) | (be_ref[i] != prev))
    def _():
        wg_s[...] = (0.5 * wg_ref[0]).astype(BF16)
        wu_s[...] = wu_ref[0].astype(BF16)
        wd_s[...] = wd_ref[0].astype(BF16)

    @pl.when(nv_ref[i] > 0)
    def _():
        keep = lax.broadcasted_iota(I32, xs_ref.shape[1:], 0) < nv_ref[i]
        parts = _unpack_planes(jnp.where(keep, xs_ref[0], jnp.uint32(0)), jnp.where(keep, xs_ref[1], jnp.uint32(0)))
        hid = _half_silu(_dot_quarters(parts, wg_s)) * _dot_quarters(parts, wu_s)
        y = _dot(hid.astype(BF16), wd_s[...])
        half = y.shape[-1] // 2
        ys_o[0] = _pack_pair(y[:, :half])
        ys_o[1] = _pack_pair(y[:, half:])


def _experts(xs, block_e, n_valid, block_src, weights, layer):
    _, n_rows, dq = xs.shape
    d = 4 * dq
    n_blocks = n_rows // EXPERT_ROWS
    hid = EXPERT_HIDDEN
    grid_spec = pltpu.PrefetchScalarGridSpec(
        num_scalar_prefetch=3,
        grid=(n_blocks,),
        in_specs=[pl.BlockSpec((2, EXPERT_ROWS, dq), lambda i, be, nv, blk: (0, blk[i], 0)),
                  pl.BlockSpec((None, 1, d, hid), lambda i, be, nv, blk: (layer, be[i], 0, 0)),
                  pl.BlockSpec((None, 1, d, hid), lambda i, be, nv, blk: (layer, be[i], 0, 0)),
                  pl.BlockSpec((None, 1, hid, d), lambda i, be, nv, blk: (layer, be[i], 0, 0))],
        out_specs=pl.BlockSpec((2, EXPERT_ROWS, dq), lambda i, be, nv, blk: (0, blk[i], 0)),
        scratch_shapes=[pltpu.VMEM((d, hid), BF16), pltpu.VMEM((d, hid), BF16), pltpu.VMEM((hid, d), BF16)],
    )
    return pl.pallas_call(
        _expert_kernel,
        grid_spec=grid_spec,
        out_shape=jax.ShapeDtypeStruct((2, n_rows, dq), U32),
        compiler_params=_cparams("arbitrary"),
        name="moe_experts",
    )(block_e, n_valid, block_src, xs, *weights)


def _combine_kernel(g_ref, wts_ref, h2_ref, x_ref, mod_ref, sg_ref, su_ref, sd_ref, gf_ref, o_ref, *, final):
    parts = _unpack_planes(h2_ref[0], h2_ref[1])
    hid = _half_silu(_dot_quarters(parts, sg_ref)) * _dot_quarters(parts, su_ref)
    shared = _dot(hid.astype(BF16), sd_ref[...])
    wts = wts_ref[...]
    routed = None
    for k in range(TOP_K):
        w = wts[:, k:k + 1]
        terms = [w * part for part in _unpack_planes(g_ref[k, 0], g_ref[k, 1])]
        routed = terms if routed is None else [r + t for r, t in zip(routed, terms)]
    f = shared + jnp.concatenate(routed, axis=1)
    m = mod_ref[:, 0]
    x2 = x_ref[...] + m[:, 5:6] * f.reshape(x_ref.shape)
    if final:
        x2 = x2 * lax.rsqrt(jnp.mean(x2 * x2, axis=-1, keepdims=True) + NORM_EPS) * gf_ref[...]
    o_ref[...] = x2


def _combine(g, wts, h2p, x1, modtab, lp, g_final, n_lat_tiles, tiles_used, final):
    b, _, d = x1.shape
    dq = d // 4
    weights = ((0.5 * lp['sh_w_gate']).astype(BF16), lp['sh_w_up'].astype(BF16), lp['sh_w_down'].astype(BF16),
               g_final.reshape(1, -1))
    full = lambda a: pl.BlockSpec(a.shape, lambda i: (0,) * a.ndim)
    group = _batch_group(b, MOE_BATCH_GROUP)
    rows = group * TM
    tok, mod = _tile_maps(b // group, n_lat_tiles)
    return pl.pallas_call(
        functools.partial(_combine_kernel, final=final),
        grid=(b * tiles_used * TM // rows,),
        in_specs=[pl.BlockSpec((TOP_K, 2, rows, dq), lambda i: (0, 0, i, 0)),
                  pl.BlockSpec((rows, TOPK_PAD), lambda i: (i, 0)),
                  pl.BlockSpec((2, rows, dq), lambda i: (0, i, 0)),
                  pl.BlockSpec((group, TM, d), tok), pl.BlockSpec((group, 1, 8, d), mod)]
                 + [full(w) for w in weights],
        out_specs=pl.BlockSpec((group, TM, d), tok),
        out_shape=jax.ShapeDtypeStruct((b, tiles_used * TM, d), F32),
        compiler_params=_cparams("parallel"),
        name="moe_combine",
    )(g, wts, h2p, x1, modtab, *weights)


def _sorted_rows_kernel(idx_ref, rank_ref, ps_ref, o_ref):
    expert = lax.broadcasted_iota(I32, (N_EXPERTS, idx_ref.shape[1]), 0)
    starts = ps_ref[...]
    idx = idx_ref[...]
    rows = [jnp.sum(jnp.where(idx[k:k + 1] == expert, starts, 0), axis=0, keepdims=True) for k in range(TOP_K)]
    pad = jnp.zeros((TOPK_PAD - TOP_K, idx.shape[1]), I32)
    o_ref[...] = jnp.concatenate(rows + [pad], axis=0) + rank_ref[...]


def _sorted_rows(idx, rank, p_starts):
    t = idx.shape[1]
    cols = TM * max(k for k in (8, 4, 2, 1) if (t // TM) % k == 0)
    per_tok = pl.BlockSpec((TOPK_PAD, cols), lambda i: (0, i))
    return pl.pallas_call(
        _sorted_rows_kernel,
        grid=(t // cols,),
        in_specs=[per_tok, per_tok, pl.BlockSpec((N_EXPERTS, 1), lambda i: (0, 0))],
        out_specs=per_tok,
        out_shape=jax.ShapeDtypeStruct((TOPK_PAD, t), I32),
        compiler_params=_cparams("parallel"),
        name="moe_sorted_rows",
    )(idx, rank, p_starts.reshape(-1, 1))


def _moe(x1, modtab, g2, lp, g_final, n_lat_tiles, final):
    b, nt, d = x1.shape
    tiles_used = n_lat_tiles if final else nt // TM
    t = b * tiles_used * TM
    h2p, idx, wts, rank, cnt = _route(x1, modtab, g2, lp, n_lat_tiles, tiles_used)
    counts = cnt[:, 0].astype(I32)
    padded = (counts + EXPERT_ROWS - 1) // EXPERT_ROWS * EXPERT_ROWS
    p_ends = jnp.cumsum(padded)
    p_starts = p_ends - padded
    n_blocks = (t * TOP_K + N_EXPERTS * (EXPERT_ROWS - 1)) // EXPERT_ROWS
    n_rows = n_blocks * EXPERT_ROWS
    dest = _sorted_rows(idx, rank, p_starts)[:TOP_K]
    plane_idx = jnp.concatenate([dest, dest + n_rows], axis=1)
    blk_start = jnp.arange(n_blocks, dtype=I32) * EXPERT_ROWS
    block_e = jnp.minimum(jnp.sum((p_ends[None, :] <= blk_start[:, None]).astype(I32), axis=1), N_EXPERTS - 1)
    n_valid = jnp.clip((p_starts + counts)[block_e] - blk_start, 0, EXPERT_ROWS).astype(I32)
    dq = d // 4
    xs = _sc_scatter_rows(h2p.reshape(2 * t, dq), plane_idx, 2 * n_rows).reshape(2, n_rows, dq)
    block_src = jnp.minimum(jnp.arange(n_blocks, dtype=I32), p_ends[-1] // EXPERT_ROWS - 1)
    ys = _experts(xs, block_e, n_valid, block_src, lp['expert_stacks'], lp['layer'])
    g = _sc_gather_rows(ys.reshape(2 * n_rows, dq), plane_idx).reshape(TOP_K, 2, t, dq)
    return _combine(g, wts.T, h2p, x1, modtab, lp, g_final, n_lat_tiles, tiles_used, final)


def _layer(stream, nt, c, c_ctx, lp, consts, g_final, seq, final):
    b = stream[0].shape[0]
    n_lat_tiles = seq // TM
    rope, mats_lat, mats_ctx = consts
    modtab = _mod_table(c, c_ctx, *lp['mod_stacks'], lp['layer'])
    g1 = lp['g_norm1'].reshape(1, -1)
    g2 = lp['g_norm2'].reshape(1, -1)
    pw = _proj_weights(lp)
    q, k, v, nq, nk, nv, hy, lu, lg = _project(stream, nt, modtab, g1, pw, rope, n_lat_tiles)
    tiles_used = n_lat_tiles if final else nt // TM
    br_a = _mla_attention(q, k, v, seq, tiles_used)
    br_b = _na_attention(nq, nk, nv, _na_bias_tables(lp['na_rpb'], seq // GRID_W), seq, tiles_used)
    pre = _hyp_pre(hy, lp['hy_short_w'], lp['hy_short_b'], seq)
    lat_f, lat_bf, ctx_f, ctx_bf = pre
    br_c = _hyena_seq(mats_lat, lat_f, lat_bf, lp, b)
    if not final:
        br_c = jnp.concatenate([br_c, _hyena_seq(mats_ctx, ctx_f, ctx_bf, lp, b)], axis=1)
    br_d = _lru_mixer(lu, lg, lp, seq)
    x1 = _merge(stream, modtab, g1, (br_a, br_b, br_c, br_d), pw['w_gate'], lp, n_lat_tiles, tiles_used)
    return _moe(x1, modtab, g2, lp, g_final, n_lat_tiles, final)


_LAYER_KEYS = ('w_mod', 'b_mod', 'g_norm1', 'g_norm2', 'w_in', 'mla_g_q', 'mla_w_uq', 'mla_g_kv', 'mla_w_ukv',
               'na_rpb', 'hy_short_w', 'hy_short_b', 'hy_w1', 'hy_b1', 'hy_w2', 'hy_b2', 'hy_w3', 'hy_decay',
               'hy_bias', 'lru_conv_w', 'lru_conv_b', 'lru_wa', 'lru_ba', 'lru_wx', 'lru_bx', 'lru_lambda',
               'w_branch', 'w_out', 'router_w', 'router_bias', 'exp_w_gate', 'exp_w_up', 'exp_w_down',
               'sh_w_gate', 'sh_w_up', 'sh_w_down')


def kernel(x, c, ctx, c_ctx, w_mod, b_mod, g_norm1, g_norm2, w_in, mla_g_q, mla_w_uq, mla_g_kv, mla_w_ukv, na_rpb, hy_short_w, hy_short_b, hy_w1, hy_b1, hy_w2, hy_b2, hy_w3, hy_decay, hy_bias, lru_conv_w, lru_conv_b, lru_wa, lru_ba, lru_wx, lru_bx, lru_lambda, w_branch, w_out, router_w, router_bias, exp_w_gate, exp_w_up, exp_w_down, sh_w_gate, sh_w_up, sh_w_down, g_final):
    stacked = dict(zip(_LAYER_KEYS, (w_mod, b_mod, g_norm1, g_norm2, w_in, mla_g_q, mla_w_uq, mla_g_kv, mla_w_ukv,
                                     na_rpb, hy_short_w, hy_short_b, hy_w1, hy_b1, hy_w2, hy_b2, hy_w3, hy_decay,
                                     hy_bias, lru_conv_w, lru_conv_b, lru_wa, lru_ba, lru_wx, lru_bx, lru_lambda,
                                     w_branch, w_out, router_w, router_bias, exp_w_gate, exp_w_up, exp_w_down,
                                     sh_w_gate, sh_w_up, sh_w_down)))
    b, seq, d = x.shape
    ctx_len = ctx.shape[1]
    depth = w_mod.shape[0]
    assert seq % TM == 0 and ctx_len % TM == 0 and seq // GRID_W >= NA_KEY_ROWS + 1
    nt = seq + ctx_len
    stream = (x, ctx, 0)
    consts = (_rope_tables(seq, seq + ctx_len), _hyp_matrices(seq), _hyp_matrices(ctx_len))
    for i in range(depth):
        big = ('w_mod', 'b_mod', 'exp_w_gate', 'exp_w_up', 'exp_w_down')
        lp = {name: w[i] for name, w in stacked.items() if name not in big}
        lp['layer'] = i
        lp['mod_stacks'] = (w_mod, b_mod)
        lp['expert_stacks'] = (exp_w_gate, exp_w_up, exp_w_down)
        xa = _layer(stream, nt, c, c_ctx, lp, consts, g_final, seq, i == depth - 1)
        stream = (xa, xa, seq // TM)
    return xa
```

```python
import functools
import math

import numpy as np
import jax
import jax.numpy as jnp
from jax import lax
from jax.experimental import pallas as pl
from jax.experimental.pallas import tpu as pltpu
from jax.experimental.pallas import tpu_sc as plsc

F32 = jnp.float32
BF16 = jnp.bfloat16
I32 = jnp.int32

TM = 256
LANE = 128
GRID_W = 64
N_MOD = 6
NORM_EPS = 1e-6

MLA_HEADS, MLA_NOPE, MLA_ROPE, MLA_V = 4, 64, 32, 64
MLA_Q_RANK, MLA_KV_RANK = 192, 128
MLA_Q_PAD = 256
ROPE_THETA = 10000.0

NA_HEADS, NA_HEAD_DIM, NA_WIN_R, NA_WIN_C = 4, 64, 8, 16
NA_TILE_ROWS = TM // GRID_W
NA_KEY_ROWS = NA_TILE_ROWS + NA_WIN_R - 1
NA_KEYS = NA_KEY_ROWS * GRID_W

HY_WIDTH, HY_ORDER, HY_SHORT, HY_BANDS, HY_FFN = 256, 2, 3, 16, 64
HY_EMB = 2 * HY_BANDS + 1

LRU_WIDTH, LRU_BLOCKS, LRU_CONV, LRU_C = 256, 4, 4, 8.0
LRU_CHUNK = 256
LRU_HALO = 8

N_EXPERTS, TOP_K, EXPERT_HIDDEN, ROUTED_SCALE, MOE_BLOCK = 64, 6, 256, 2.5, 256
TOPK_PAD = 8

VMEM_LIMIT = 56 * 1024 * 1024


def _cparams(*sem):
    return pltpu.CompilerParams(dimension_semantics=sem, vmem_limit_bytes=VMEM_LIMIT)


def _dot(a, b):
    return jnp.dot(a, b, preferred_element_type=F32)


def _dot_nt(a, b):
    return lax.dot_general(a, b, (((1,), (1,)), ((), ())), preferred_element_type=F32)


def _sigmoid(x):
    return jax.nn.sigmoid(x)


def _half_silu(g):
    return g * (1.0 + jnp.tanh(g))


def _silu(x):
    return x * _sigmoid(x)


def _gelu_tanh(x):
    return 0.5 * x * (1.0 + jnp.tanh(math.sqrt(2.0 / math.pi) * (x + 0.044715 * (x * x * x))))


def _normmod(x, g, scale, shift):
    y = x * lax.rsqrt(jnp.mean(x * x, axis=-1, keepdims=True) + NORM_EPS) * g
    return y * (1.0 + scale) + shift


MOD_COLS = 1024


def _mod_kernel(c_ref, w_ref, b_ref, o_ref):
    s = _silu(c_ref[...])
    o_ref[...] = _dot(s.astype(BF16), w_ref[...].astype(BF16)) + b_ref[...]


def _mod_table(c, c_ctx, w_mod, b_mod, layer):
    b, d = c.shape
    rows = -(-(b + 1) // 16) * 16
    cc = jnp.zeros((rows, d), F32).at[:b].set(c).at[b].set(c_ctx)
    tn = MOD_COLS
    mod = pl.pallas_call(
        _mod_kernel,
        grid=(N_MOD * d // tn,),
        in_specs=[pl.BlockSpec((rows, d), lambda j: (0, 0)),
                  pl.BlockSpec((None, d, tn), lambda j: (layer, 0, j)),
                  pl.BlockSpec((None, 1, tn), lambda j: (layer, 0, j))],
        out_specs=pl.BlockSpec((rows, tn), lambda j: (0, j)),
        out_shape=jax.ShapeDtypeStruct((rows, N_MOD * d), F32),
        compiler_params=_cparams("arbitrary"),
        name="mod_vectors",
    )(cc, w_mod, b_mod[:, None, :])
    lat = mod[:b].reshape(b, N_MOD, d)
    ctx = jnp.broadcast_to(mod[b].reshape(1, N_MOD, d), (b, N_MOD, d))
    tab = jnp.stack([lat, ctx], axis=1)
    return jnp.pad(tab, ((0, 0), (0, 0), (0, 8 - N_MOD), (0, 0)))


_C_QLAT, _C_KVLAT, _C_KR, _C_KRR, _C_NA, _C_HY, _C_LU, _C_LG, _C_END = (
    0, 256, 384, 512, 640, 640 + 3 * NA_HEADS * LANE, 640 + 1536 + 768, 640 + 1536 + 1024, 640 + 1536 + 1280)


def _stream_tile(lat_ref, ctx_ref, n_lat_tiles):
    return jnp.where(pl.program_id(1) < n_lat_tiles, lat_ref[...], ctx_ref[...])


def _stream_specs(stream, group, n_lat_tiles):
    lat_src, ctx_src, ctx_tile0 = stream
    d = lat_src.shape[-1]
    return [pl.BlockSpec((group, TM, d), lambda bb, i: (bb, jnp.minimum(i, n_lat_tiles - 1), 0)),
            pl.BlockSpec((group, TM, d), lambda bb, i: (bb, ctx_tile0 + jnp.maximum(i - n_lat_tiles, 0), 0))]


def _proj_kernel(xl_ref, xc_ref, mod_ref, g1_ref, w1_ref, gq_ref, gkv_ref, wuq_ref, wuqr_ref, wk_ref, wv_ref,
                 cos_ref, sin_ref, q_o, k_o, v_o, nq_o, nk_o, nv_o, hy_o, lu_o, lg_o, *, n_lat_tiles):
    group, tm, d = xl_ref.shape
    m = mod_ref[:, 0]
    x = _stream_tile(xl_ref, xc_ref, n_lat_tiles)
    h = _normmod(x, g1_ref[...], m[:, 1:2], m[:, 0:1]).reshape(group * tm, d)
    z = _dot(h.astype(BF16), w1_ref[...])
    qlat = z[:, _C_QLAT:_C_KVLAT]
    kvlat = z[:, _C_KVLAT:_C_KR]
    kr = z[:, _C_KR:_C_KRR]
    krr = z[:, _C_KRR:_C_NA]
    qn = qlat * lax.rsqrt(jnp.sum(qlat * qlat, axis=-1, keepdims=True) * (1.0 / MLA_Q_RANK) + NORM_EPS) * gq_ref[...]
    kvn = kvlat * lax.rsqrt(jnp.mean(kvlat * kvlat, axis=-1, keepdims=True) + NORM_EPS) * gkv_ref[...]
    qn = qn.astype(BF16)
    kvn = kvn.astype(BF16)
    q = _dot(qn, wuq_ref[...])
    qr = _dot(qn, wuqr_ref[...])
    kk = _dot(kvn, wk_ref[...])
    vv = _dot(kvn, wv_ref[...])
    cos = jnp.concatenate([cos_ref[...]] * group, axis=0)
    sin = jnp.concatenate([sin_ref[...]] * group, axis=0)
    krope = kr * cos + krr * sin
    den_lane = lax.broadcasted_iota(I32, cos.shape, 1) == SOFTMAX_DEN_LANE
    def put_heads(ref, hd, val):
        for g in range(group):
            ref[g, hd] = val[g * tm:(g + 1) * tm].astype(BF16)

    def put_pairs(ref, hd, val):
        off = (hd % 2) * LANE
        for g in range(group):
            ref[g, hd // 2, :, off:off + LANE] = val[g * tm:(g + 1) * tm].astype(BF16)

    for hd in range(MLA_HEADS):
        sl = slice(hd * LANE, (hd + 1) * LANE)
        put_heads(q_o, hd, q[:, sl] * cos + qr[:, sl] * sin)
        put_heads(k_o, hd, kk[:, sl] + krope)
        put_heads(v_o, hd, jnp.where(den_lane, 1.0, vv[:, sl]))
    for hd in range(NA_HEADS):
        for which, ref in enumerate((nq_o, nk_o, nv_o)):
            lo = _C_NA + (which * NA_HEADS + hd) * LANE
            blk = z[:, lo:lo + LANE]
            if which == 2:
                put_pairs(ref, hd, jnp.where(den_lane, 1.0, blk))
            else:
                put_heads(ref, hd, blk)
    hy_o[...] = z[:, _C_HY:_C_LU].reshape(group, tm, _C_LU - _C_HY)
    lu_o[...] = z[:, _C_LU:_C_LG].reshape(group, tm, _C_LG - _C_LU)
    lg_o[...] = z[:, _C_LG:_C_END].reshape(group, tm, _C_END - _C_LG)


def _pad_to(a, n, axis):
    pad = [(0, 0)] * a.ndim
    pad[axis] = (0, n - a.shape[axis])
    return jnp.pad(a, pad)


def _rot_cols(w):
    half = w.shape[-1] // 2
    return jnp.concatenate([-w[..., half:], w[..., :half]], axis=-1)


def _head_blocks(cols_per_head):
    out = []
    for pieces in cols_per_head:
        k = pieces[0][0].shape[0]
        blk = jnp.zeros((k, LANE), F32)
        for arr, off in pieces:
            blk = blk.at[:, off:off + arr.shape[1]].set(arr)
        out.append(blk)
    return jnp.concatenate(out, axis=1)


def _proj_weights(lp):
    w_in = lp['w_in']
    d = w_in.shape[0]
    o = 0
    parts = {}
    for name, n in (('q', MLA_Q_RANK), ('kv', MLA_KV_RANK), ('kr', MLA_ROPE), ('na', 3 * NA_HEADS * NA_HEAD_DIM),
                    ('hy', 3 * HY_WIDTH), ('lu', LRU_WIDTH), ('lg', LRU_WIDTH), ('gt', 4 * d)):
        parts[name] = w_in[:, o:o + n]
        o += n
    zeros = lambda n: jnp.zeros((d, n), F32)
    kr_blk = jnp.concatenate([zeros(MLA_NOPE), parts['kr'], zeros(LANE - MLA_NOPE - MLA_ROPE)], axis=1)
    krr_blk = jnp.concatenate([zeros(MLA_NOPE), _rot_cols(parts['kr']), zeros(LANE - MLA_NOPE - MLA_ROPE)], axis=1)
    na_scale = NA_HEAD_DIM ** -0.5
    na_cols = []
    for which in range(3):
        for hd in range(NA_HEADS):
            lo = (which * NA_HEADS + hd) * NA_HEAD_DIM
            blk = parts['na'][:, lo:lo + NA_HEAD_DIM] * (na_scale if which == 0 else 1.0)
            na_cols.append(_pad_to(blk, LANE, 1))
    w1 = jnp.concatenate([_pad_to(parts['q'], MLA_Q_PAD, 1), parts['kv'], kr_blk, krr_blk] + na_cols
                         + [parts['hy'], parts['lu'], parts['lg']], axis=1).astype(BF16)
    mla_scale = (MLA_NOPE + MLA_ROPE) ** -0.5
    wuq = _pad_to(lp['mla_w_uq'], MLA_Q_PAD, 0) * mla_scale
    dq = MLA_NOPE + MLA_ROPE
    wuq_main = _head_blocks([[(wuq[:, hd * dq:hd * dq + dq], 0)] for hd in range(MLA_HEADS)])
    wuq_rot = _head_blocks([[(_rot_cols(wuq[:, hd * dq + MLA_NOPE:hd * dq + dq]), MLA_NOPE)]
                            for hd in range(MLA_HEADS)])
    dkv = MLA_NOPE + MLA_V
    wukv = lp['mla_w_ukv']
    wk = _head_blocks([[(wukv[:, hd * dkv:hd * dkv + MLA_NOPE], 0)] for hd in range(MLA_HEADS)])
    wv = _head_blocks([[(wukv[:, hd * dkv + MLA_NOPE:hd * dkv + dkv], 0)] for hd in range(MLA_HEADS)])
    gq = _pad_to(lp['mla_g_q'].reshape(1, -1), MLA_Q_PAD, 1)
    gkv = lp['mla_g_kv'].reshape(1, -1)
    return dict(w1=w1, w_gate=(0.5 * parts['gt']).astype(BF16), gq=gq, gkv=gkv, wuq=wuq_main.astype(BF16),
                wuq_rot=wuq_rot.astype(BF16), wk=wk.astype(BF16), wv=wv.astype(BF16))


def _rope_tables(seq, n_tok):
    t = jnp.arange(seq, dtype=I32)
    row = (t // GRID_W).astype(F32)
    col = (t % GRID_W).astype(F32)
    n_axis = MLA_ROPE // 4
    inv_freq = ROPE_THETA ** (-jnp.arange(n_axis, dtype=F32) / n_axis)
    ang = jnp.concatenate([row[:, None] * inv_freq, col[:, None] * inv_freq], axis=-1)
    cos = jnp.concatenate([jnp.cos(ang), jnp.cos(ang)], axis=-1)
    sin = jnp.concatenate([jnp.sin(ang), jnp.sin(ang)], axis=-1)
    cos_t = jnp.ones((n_tok, LANE), F32).at[:seq, MLA_NOPE:MLA_NOPE + MLA_ROPE].set(cos)
    sin_t = jnp.zeros((n_tok, LANE), F32).at[:seq, MLA_NOPE:MLA_NOPE + MLA_ROPE].set(sin)
    return cos_t, sin_t


def _batch_group(b, cap=2):
    return max(g for g in (4, 2, 1) if g <= cap and b % g == 0)


MOE_BATCH_GROUP = 4


def _kind_map(n_lat_tiles):
    return lambda b, i: (b, jnp.where(i >= n_lat_tiles, 1, 0), 0, 0)


def _project(stream, nt, modtab, g1, pw, rope, n_lat_tiles):
    b, _, d = stream[0].shape
    cos_t, sin_t = rope
    group = _batch_group(b)
    full = lambda a: pl.BlockSpec(a.shape, lambda bb, i: (0,) * a.ndim)
    head_out = lambda: pl.BlockSpec((group, MLA_HEADS, TM, LANE), lambda bb, i: (bb, 0, i, 0))
    tok_out = lambda n: pl.BlockSpec((group, TM, n), lambda bb, i: (bb, i, 0))
    head_shape = jax.ShapeDtypeStruct((b, MLA_HEADS, nt, LANE), BF16)
    pair_out = lambda: pl.BlockSpec((group, MLA_HEADS // 2, TM, 2 * LANE), lambda bb, i: (bb, 0, i, 0))
    pair_shape = jax.ShapeDtypeStruct((b, MLA_HEADS // 2, nt, 2 * LANE), BF16)
    tok_shape = lambda n: jax.ShapeDtypeStruct((b, nt, n), F32)
    weights = (g1, pw['w1'], pw['gq'], pw['gkv'], pw['wuq'], pw['wuq_rot'], pw['wk'], pw['wv'])
    return pl.pallas_call(
        functools.partial(_proj_kernel, n_lat_tiles=n_lat_tiles),
        grid=(b // group, nt // TM),
        in_specs=_stream_specs(stream, group, n_lat_tiles)
                 + [pl.BlockSpec((group, 1, 8, d), _kind_map(n_lat_tiles))]
                 + [full(w) for w in weights]
                 + [pl.BlockSpec((TM, LANE), lambda bb, i: (i, 0))] * 2,
        out_specs=[head_out(), head_out(), head_out(), head_out(), head_out(), pair_out(),
                   tok_out(3 * HY_WIDTH), tok_out(LRU_WIDTH), tok_out(LRU_WIDTH)],
        out_shape=[head_shape, head_shape, head_shape, head_shape, head_shape, pair_shape,
                   tok_shape(3 * HY_WIDTH), tok_shape(LRU_WIDTH), tok_shape(LRU_WIDTH)],
        compiler_params=_cparams("parallel", "parallel"),
        name="input_projection",
    )(stream[0], stream[1], modtab, *weights, cos_t, sin_t)


SOFTMAX_DEN_LANE = 64


def _softmax_pv(parts, lane_off=0):
    m = None
    for s, _ in parts:
        mm = jnp.max(s, axis=-1, keepdims=True)
        m = mm if m is None else jnp.maximum(m, mm)
    acc = None
    for s, v in parts:
        o = _dot(jnp.exp(s - m).astype(BF16), v)
        acc = o if acc is None else acc + o
    den = lane_off + SOFTMAX_DEN_LANE
    return acc[:, lane_off:lane_off + LANE] / acc[:, den:den + 1]


def _mla_kernel(q_ref, k_ref, v_ref, o_ref, *, seq, n_lat_tiles):
    i = pl.program_id(1)
    nt = k_ref.shape[2]

    def attend(lo, hi):
        for hd in range(MLA_HEADS):
            s = _dot_nt(q_ref[0, hd], k_ref[0, hd, lo:hi, :])
            o = _softmax_pv([(s, v_ref[0, hd, lo:hi, :])])
            o_ref[0, :, hd * LANE:(hd + 1) * LANE] = o.astype(BF16)

    @pl.when(i < n_lat_tiles)
    def _():
        attend(0, nt)

    @pl.when(i >= n_lat_tiles)
    def _():
        attend(seq, nt)


def _mla_attention(q, k, v, seq, tiles_used):
    b, h, nt, _ = q.shape
    kv_spec = pl.BlockSpec((1, h, nt, LANE), lambda bb, i: (bb, 0, 0, 0))
    return pl.pallas_call(
        functools.partial(_mla_kernel, seq=seq, n_lat_tiles=seq // TM),
        grid=(b, tiles_used),
        in_specs=[pl.BlockSpec((1, h, TM, LANE), lambda bb, i: (bb, 0, i, 0)), kv_spec, kv_spec],
        out_specs=pl.BlockSpec((1, TM, h * LANE), lambda bb, i: (bb, i, 0)),
        out_shape=jax.ShapeDtypeStruct((b, tiles_used * TM, h * LANE), BF16),
        compiler_params=_cparams("parallel", "arbitrary"),
        name="mla_attention",
    )(q, k, v)


def _na_bias_tables(rpb, rows):
    n_blk = rows // NA_TILE_ROWS
    col = np.arange(GRID_W)
    c0 = np.clip(col - NA_WIN_C // 2, 0, GRID_W - NA_WIN_C)
    in_win = (col[None, :] >= c0[:, None]) & (col[None, :] < c0[:, None] + NA_WIN_C)
    dc = np.clip(col[None, :] - col[:, None], 1 - NA_WIN_C, NA_WIN_C - 1) + NA_WIN_C - 1
    rpb = rpb.astype(F32)
    tables = []
    for j in (0, 1, n_blk - 1):
        w0 = min(max(NA_TILE_ROWS * j - NA_WIN_R // 2, 0), rows - NA_KEY_ROWS)
        r = NA_TILE_ROWS * j + np.arange(NA_TILE_ROWS)
        kr = w0 + np.arange(NA_KEY_ROWS)
        r0 = np.clip(r - NA_WIN_R // 2, 0, rows - NA_WIN_R)
        row_ok = (kr[None, :] >= r0[:, None]) & (kr[None, :] < r0[:, None] + NA_WIN_R)
        dr = np.clip(kr[None, :] - r[:, None] + NA_WIN_R - 1, 0, 2 * NA_WIN_R - 2)
        oh_r = jnp.asarray(np.eye(2 * NA_WIN_R - 1, dtype=np.float32)[dr.reshape(-1)])
        oh_c = jnp.asarray(np.eye(2 * NA_WIN_C - 1, dtype=np.float32)[dc.reshape(-1)])
        bias = jnp.einsum('ar,hrc,bc->hab', oh_r, rpb, oh_c, precision=lax.Precision.HIGHEST)
        bias = bias.reshape(NA_HEADS, NA_TILE_ROWS, NA_KEY_ROWS, GRID_W, GRID_W)
        mask = row_ok[:, :, None, None] & in_win[None, None, :, :]
        bias = jnp.where(jnp.asarray(mask)[None], bias, -jnp.inf)
        tables.append(bias.transpose(0, 1, 3, 2, 4).reshape(NA_HEADS, TM, NA_KEYS))
    return jnp.stack(tables)


def _na_kernel(q_ref, k_ref, v_ref, bias_ref, o_ref, *, seq, n_lat_tiles):
    i = pl.program_id(1)
    nt = k_ref.shape[2]
    rows = seq // GRID_W

    @pl.when(i < n_lat_tiles)
    def _():
        w0 = jnp.clip(NA_TILE_ROWS * i - NA_WIN_R // 2, 0, rows - NA_KEY_ROWS)
        start = pl.multiple_of(w0 * GRID_W, GRID_W)
        for hd in range(NA_HEADS):
            q = q_ref[0, hd]
            s_loc = _dot_nt(q, k_ref[0, hd, pl.ds(start, NA_KEYS), :]) + bias_ref[0, hd]
            s_ctx = _dot_nt(q, k_ref[0, hd, seq:nt, :])
            o = _softmax_pv([(s_loc, v_ref[0, hd // 2, pl.ds(start, NA_KEYS), :]),
                             (s_ctx, v_ref[0, hd // 2, seq:nt, :])], (hd % 2) * LANE)
            o_ref[0, :, hd * LANE:(hd + 1) * LANE] = o.astype(BF16)

    @pl.when(i >= n_lat_tiles)
    def _():
        for hd in range(NA_HEADS):
            s = _dot_nt(q_ref[0, hd], k_ref[0, hd, seq:nt, :])
            o = _softmax_pv([(s, v_ref[0, hd // 2, seq:nt, :])], (hd % 2) * LANE)
            o_ref[0, :, hd * LANE:(hd + 1) * LANE] = o.astype(BF16)


def _na_attention(q, k, v, bias, seq, tiles_used):
    b, h, nt, _ = q.shape
    n_lat = seq // TM
    kv_spec = pl.BlockSpec((1, h, nt, LANE), lambda bb, i: (bb, 0, 0, 0))
    cfg = lambda bb, i: (jnp.where(i == 0, 0, jnp.where(i >= n_lat - 1, 2, 1)), 0, 0, 0)
    return pl.pallas_call(
        functools.partial(_na_kernel, seq=seq, n_lat_tiles=n_lat),
        grid=(b, tiles_used),
        in_specs=[pl.BlockSpec((1, h, TM, LANE), lambda bb, i: (bb, 0, i, 0)), kv_spec,
                  pl.BlockSpec((1, h // 2, nt, 2 * LANE), lambda bb, i: (bb, 0, 0, 0)),
                  pl.BlockSpec((1, h, TM, NA_KEYS), cfg)],
        out_specs=pl.BlockSpec((1, TM, h * LANE), lambda bb, i: (bb, i, 0)),
        out_shape=jax.ShapeDtypeStruct((b, tiles_used * TM, h * LANE), BF16),
        compiler_params=_cparams("parallel", "arbitrary"),
        name="neighbourhood_attention",
    )(q, k, v, bias)


def _hp_dot(a, b):
    a_hi, b_hi = a.astype(BF16), b.astype(BF16)
    a_lo, b_lo = (a - a_hi.astype(F32)).astype(BF16), (b - b_hi.astype(F32)).astype(BF16)
    return _dot(a_hi, b_hi) + _dot(a_hi, b_lo) + _dot(a_lo, b_hi)


def _hy_filter_kernel(f_ref, w1_ref, b1_ref, w2_ref, b2_ref, w3_ref, dec_ref, o_ref, obf_ref):
    f = f_ref[...]
    h = jnp.sin(_hp_dot(f, w1_ref[...]) + b1_ref[...])
    h = jnp.sin(_hp_dot(h, w2_ref[...]) + b2_ref[...])
    h = _hp_dot(h, w3_ref[...])
    h = h * jnp.exp(-f[:, 0:1] * jnp.abs(dec_ref[...]))
    o_ref[...] = h
    obf_ref[...] = h.astype(BF16)


def _hy_pos_features(length):
    t = jnp.linspace(0.0, 1.0, length, dtype=F32)[:, None]
    w = 2.0 * math.pi * jnp.arange(length, dtype=F32)[:, None] / length
    f = jnp.linspace(1e-4, HY_BANDS - 1, HY_BANDS, dtype=F32)[None, :]
    z = w * f
    return jnp.concatenate([t, jnp.cos(z), -jnp.sin(z)], axis=-1)


DFT_ROW_STEP = 64
DFT_ROWS = 256
DFT_FWD_COLS = 1024
DFT_INV_COLS = 512
FILTER_ROWS = 512


def _col_block(nc, cap):
    return min(nc, cap)


def _hyp_pre_kernel(z_ref, w_ref, b_ref, lat_o, lat_bf_o, ctx_o, ctx_bf_o, *, seq, ctx_len):
    w = w_ref[...]
    bias = b_ref[...]
    for lo, length, o_ref, obf_ref in ((0, seq, lat_o, lat_bf_o), (seq, ctx_len, ctx_o, ctx_bf_o)):
        h = length // 2
        even = z_ref[0, pl.ds(lo, h, stride=2), :]
        odd = z_ref[0, pl.ds(lo + 1, h, stride=2), :]
        row = lax.broadcasted_iota(I32, even.shape, 0)
        odd_prev = jnp.where(row == 0, 0.0, pltpu.roll(odd, 1, axis=0))
        even_next = jnp.where(row == h - 1, 0.0, pltpu.roll(even, h - 1, axis=0))
        y_even = bias + w[0:1] * odd_prev + w[1:2] * even + w[2:3] * odd
        y_odd = bias + w[0:1] * even + w[1:2] * odd + w[2:3] * even_next
        for r, y in enumerate((y_even, y_odd)):
            o_ref[0, r] = y
            obf_ref[0, r] = y.astype(BF16)


def _hyp_pre(hy, w, bvec, seq):
    b, nt, _ = hy.shape
    ctx_len = nt - seq
    wpad = _pad_to(w, 8, 0)
    per_plane = HY_WIDTH // LANE
    out_specs, out_shape = [], []
    for length in (seq, ctx_len):
        for dt in (F32, BF16):
            out_specs.append(pl.BlockSpec((1, 2, length // 2, LANE),
                                          lambda bb, g: (g // per_plane, 0, 0, bb * per_plane + g % per_plane)))
            out_shape.append(jax.ShapeDtypeStruct((3, 2, length // 2, b * HY_WIDTH), dt))
    return pl.pallas_call(
        functools.partial(_hyp_pre_kernel, seq=seq, ctx_len=ctx_len),
        grid=(b, 3 * per_plane),
        in_specs=[pl.BlockSpec((1, nt, LANE), lambda bb, g: (bb, 0, g)),
                  pl.BlockSpec((8, LANE), lambda bb, g: (0, g)),
                  pl.BlockSpec((1, LANE), lambda bb, g: (0, g))],
        out_specs=out_specs,
        out_shape=out_shape,
        compiler_params=_cparams("parallel", "parallel"),
        name="hyena_short_conv",
    )(hy, wpad, bvec.reshape(1, -1))


def _hyp_filters(length, lp):
    feats = _hy_pos_features(length)
    feats = feats.reshape(length // 2, 2, HY_EMB).transpose(1, 0, 2).reshape(length, HY_EMB)
    n_out = HY_ORDER * 2 * HY_WIDTH
    tl = min(length // 2, FILTER_ROWS)
    full = lambda a: pl.BlockSpec(a.shape, lambda i: (0,) * a.ndim)
    args = (lp['hy_w1'], lp['hy_b1'].reshape(1, -1), lp['hy_w2'], lp['hy_b2'].reshape(1, -1), lp['hy_w3'],
            lp['hy_decay'].reshape(1, n_out))
    filt, filt_bf = pl.pallas_call(
        _hy_filter_kernel,
        grid=(length // tl,),
        in_specs=[pl.BlockSpec((tl, HY_EMB), lambda i: (i, 0))] + [full(a) for a in args],
        out_specs=[pl.BlockSpec((tl, n_out), lambda i: (i, 0))] * 2,
        out_shape=[jax.ShapeDtypeStruct((length, n_out), F32), jax.ShapeDtypeStruct((length, n_out), BF16)],
        compiler_params=_cparams("parallel"),
        name="hyena_filter_mlp",
    )(feats, *args)
    return filt.reshape(2, length // 2, n_out), filt_bf.reshape(2, length // 2, n_out)


def _hyp_matrix_kernel(c1_ref, s1_ref, c2_ref, s2_ref, ck_ref, sk_ref, ckr_ref, skr_ref,
                       ce_o, se_o, co_o, so_o, set_o, cot_o, sot_o):
    c1, s1 = c1_ref[0], s1_ref[0]
    c2, s2 = c2_ref[...], s2_ref[...]
    cos = c1 * c2 - s1 * s2
    sin = s1 * c2 + c1 * s2
    row = lax.broadcasted_iota(I32, cos.shape, 0)
    col = lax.broadcasted_iota(I32, cos.shape, 1)
    first_row = (row + pl.program_id(0) * DFT_ROW_STEP) == 0
    alt_col = jnp.where(col % 2 == 0, 1.0, -1.0)
    alt_row = jnp.where(row % 2 == 0, 1.0, -1.0)
    ck, sk = ck_ref[...], sk_ref[...]
    ce_o[...] = cos.astype(BF16)
    se_o[...] = jnp.where(first_row, alt_col, -sin).astype(BF16)
    co_o[...] = (cos * ck - sin * sk).astype(BF16)
    so_o[...] = jnp.where(first_row, alt_col, -(sin * ck + cos * sk)).astype(BF16)
    ckr, skr = ckr_ref[...], skr_ref[...]
    set_o[...] = jnp.where(col == 0, alt_row, -sin).astype(BF16)
    cot_o[...] = (cos * ckr - sin * skr).astype(BF16)
    sot_o[...] = jnp.where(col == 0, alt_row, -(sin * ckr + cos * skr)).astype(BF16)


def _hyp_matrices(length):
    h = length // 2
    step = DFT_ROW_STEP
    m = jnp.arange(h, dtype=I32)

    def trig(kv):
        ang = ((kv[:, None] * m[None, :]) % length).astype(F32) * (2.0 * math.pi / length)
        return jnp.cos(ang), jnp.sin(ang)

    c1, s1 = trig(jnp.arange(h // step, dtype=I32) * step)
    c2, s2 = trig(jnp.arange(step, dtype=I32))
    half_angle = m.astype(F32) * (math.pi / length)
    ck, sk = jnp.cos(half_angle), jnp.sin(half_angle)
    coarse = pl.BlockSpec((1, 1, h), lambda j: (j, 0, 0))
    fine = pl.BlockSpec((step, h), lambda j: (0, 0))
    per_row = pl.BlockSpec((step, 1), lambda j: (j, 0))
    per_col = pl.BlockSpec((1, h), lambda j: (0, 0))
    out = pl.BlockSpec((step, h), lambda j: (j, 0))
    ce, se, co, so, se_t, co_t, so_t = pl.pallas_call(
        _hyp_matrix_kernel,
        grid=(h // step,),
        in_specs=[coarse, coarse, fine, fine, per_row, per_row, per_col, per_col],
        out_specs=[out] * 7,
        out_shape=[jax.ShapeDtypeStruct((h, h), BF16)] * 7,
        compiler_params=_cparams("parallel"),
        name="dft_matrices",
    )(c1[:, None, :], s1[:, None, :], c2, s2, ck[:, None], sk[:, None], ck[None, :], sk[None, :])
    return dict(fwd=(ce, se, co, so), inv=(ce, se_t, co_t, so_t))


def _hyp_fwd_kernel(ce_ref, se_ref, co_ref, so_ref, xe_ref, xo_ref, *rest, with_taps):
    xe, xo = xe_ref[...], xo_ref[...]
    a_re, a_im = _dot(ce_ref[...], xe), _dot(se_ref[...], xe)
    b_re, b_im = _dot(co_ref[...], xo), _dot(so_ref[...], xo)
    if not with_taps:
        for ref, val in zip(rest, (a_re, a_im, b_re, b_im)):
            ref[...] = val
        return
    t1re_ref, t1im_ref, t2re_ref, t2im_ref, sp_ref, gere_o, geim_o, gore_o, goim_o = rest
    t1re, t1im, t2re, t2im = t1re_ref[...], t1im_ref[...], t2re_ref[...], t2im_ref[...]
    sp = sp_ref[...]
    first_block = pl.program_id(1) == 0
    row8 = lax.broadcasted_iota(I32, (8, HY_WIDTH), 0)
    for bb in range(xe.shape[1] // HY_WIDTH):
        sl = slice(bb * HY_WIDTH, (bb + 1) * HY_WIDTH)
        are, aim, bre, bim = a_re[:, sl], a_im[:, sl], b_re[:, sl], b_im[:, sl]
        u1re, u1im = are + bre, aim + bim
        u2re, u2im = are - bre, bim - aim
        z1re, z1im = u1re * t1re - u1im * t1im, u1re * t1im + u1im * t1re
        z2re, z2im = u2re * t2re - u2im * t2im, u2re * t2im + u2im * t2re
        gere_o[:, sl] = (z1re + z2re).astype(BF16)
        geim_o[:, sl] = (z1im - z2im).astype(BF16)
        gore_o[:, sl] = (z1re - z2re).astype(BF16)
        goim_o[:, sl] = (z1im + z2im).astype(BF16)

        @pl.when(first_block)
        def _():
            u0, ul = u1re[0:8], u2re[0:8]
            a_s, b_s = aim[0:8], bim[0:8]
            dc, ny, mre, mim = sp[0:1], sp[1:2], sp[2:3], sp[3:4]
            first = row8 == 0
            gere_o[0:8, sl] = jnp.where(first, u0 * dc + ul * ny, (z1re + z2re)[0:8]).astype(BF16)
            gore_o[0:8, sl] = jnp.where(first, u0 * dc - ul * ny, (z1re - z2re)[0:8]).astype(BF16)
            geim_o[0:8, sl] = jnp.where(first, a_s * mre + b_s * mim, (z1im - z2im)[0:8]).astype(BF16)
            goim_o[0:8, sl] = jnp.where(first, b_s * mre - a_s * mim, (z1im + z2im)[0:8]).astype(BF16)


def _hyp_fwd(mats, x, plane, taps=None):
    _, _, h, nc = x.shape
    tk = min(h, DFT_ROWS)
    cb = _col_block(nc, DFT_FWD_COLS)
    grid = (nc // cb, h // tk)
    m_spec = pl.BlockSpec((tk, h), lambda c, j: (j, 0))
    x_spec = lambda r: pl.BlockSpec((None, None, h, cb), lambda c, j: (plane, r, 0, c))
    o_spec = pl.BlockSpec((tk, cb), lambda c, j: (j, c))
    if taps is None:
        return pl.pallas_call(
            functools.partial(_hyp_fwd_kernel, with_taps=False),
            grid=grid, in_specs=[m_spec] * 4 + [x_spec(0), x_spec(1)], out_specs=[o_spec] * 4,
            out_shape=[jax.ShapeDtypeStruct((h, nc), F32)] * 4,
            compiler_params=_cparams("parallel", "arbitrary"),
            name="hyena_dft_filters",
        )(*mats['fwd'], x, x)
    t_spec = pl.BlockSpec((tk, HY_WIDTH), lambda c, j: (j, 0))
    sp_spec = pl.BlockSpec((8, HY_WIDTH), lambda c, j: (0, 0))
    return pl.pallas_call(
        functools.partial(_hyp_fwd_kernel, with_taps=True),
        grid=grid, in_specs=[m_spec] * 4 + [x_spec(0), x_spec(1)] + [t_spec] * 4 + [sp_spec],
        out_specs=[o_spec] * 4,
        out_shape=[jax.ShapeDtypeStruct((h, nc), BF16)] * 4,
        compiler_params=_cparams("parallel", "arbitrary"),
        name="hyena_dft_forward",
    )(*mats['fwd'], x, x, *taps)


def _hyp_inv_kernel(ce_ref, set_ref, cot_ref, sot_ref, gere_ref, geim_ref, gore_ref, goim_ref,
                    gate_ref, prev_ref, bias_ref, *outs, last):
    conv = (_dot(ce_ref[...], gere_ref[...]) + _dot(set_ref[...], geim_ref[...]),
            _dot(cot_ref[...], gore_ref[...]) + _dot(sot_ref[...], goim_ref[...]))
    bias = bias_ref[...]
    for r in range(2):
        y = gate_ref[r] * (conv[r] + prev_ref[r] * bias)
        if last:
            (tok_o,) = outs
            for bb in range(y.shape[1] // HY_WIDTH):
                tok_o[bb, :, r * HY_WIDTH:(r + 1) * HY_WIDTH] = y[:, bb * HY_WIDTH:(bb + 1) * HY_WIDTH].astype(BF16)
        else:
            y_o, ybf_o = outs
            y_o[r] = y
            ybf_o[r] = y.astype(BF16)


def _hyp_inv(mats, g, gate, gate_plane, prev, prev_plane, bias_row, last):
    h, nc = g[0].shape
    tm = min(h, DFT_ROWS)
    cb = _col_block(nc, DFT_INV_COLS)
    grid = (nc // cb, h // tm)
    m_spec = pl.BlockSpec((tm, h), lambda c, i: (i, 0))
    g_spec = pl.BlockSpec((h, cb), lambda c, i: (0, c))
    e_spec = lambda plane: pl.BlockSpec((None, 2, tm, cb), lambda c, i: (plane, 0, i, c))
    b_spec = pl.BlockSpec((1, cb), lambda c, i: (0, c))
    if last:
        out_specs = [pl.BlockSpec((cb // HY_WIDTH, tm, 2 * HY_WIDTH), lambda c, i: (c, i, 0))]
        out_shape = [jax.ShapeDtypeStruct((nc // HY_WIDTH, h, 2 * HY_WIDTH), BF16)]
    else:
        out_specs = [e_spec(0), e_spec(0)]
        out_shape = [jax.ShapeDtypeStruct((1, 2, h, nc), F32), jax.ShapeDtypeStruct((1, 2, h, nc), BF16)]
    return pl.pallas_call(
        functools.partial(_hyp_inv_kernel, last=last),
        grid=grid, in_specs=[m_spec] * 4 + [g_spec] * 4 + [e_spec(gate_plane), e_spec(prev_plane), b_spec],
        out_specs=out_specs, out_shape=out_shape,
        compiler_params=_cparams("parallel", "arbitrary"),
        name="hyena_dft_inverse",
    )(*mats['inv'], *g, gate, prev, bias_row)


def _hyp_tap_tables(spec, filt, length):
    a_re, a_im, b_re, b_im = spec
    w = HY_WIDTH
    inv_n = 1.0 / (2 * length)
    tables = []
    for o in range(HY_ORDER):
        f_sl = slice((2 * o) * w, (2 * o + 1) * w)
        r_sl = slice((2 * o + 1) * w, (2 * o + 2) * w)
        hb0 = filt[0, 0:1, r_sl]
        f1re = (a_re + b_re)[:, f_sl] + (a_re + b_re)[:, r_sl] - hb0
        f1im = (a_im + b_im)[:, f_sl] - (a_im + b_im)[:, r_sl]
        f2re = (a_re - b_re)[:, f_sl] + (a_re - b_re)[:, r_sl] - hb0
        f2im = (b_im - a_im)[:, f_sl] - (b_im - a_im)[:, r_sl]
        dc = f1re[0:1]
        ny = f2re[0:1]
        mid_re = a_im[0:1, f_sl] + a_im[0:1, r_sl] - hb0
        mid_im = -b_im[0:1, f_sl] + b_im[0:1, r_sl]
        sp = jnp.concatenate([dc * inv_n, ny * inv_n, mid_re * (2 * inv_n), mid_im * (2 * inv_n),
                              jnp.zeros((4, w), F32)], axis=0)
        tables.append((f1re * (2 * inv_n), f1im * (2 * inv_n), f2re * (2 * inv_n), f2im * (2 * inv_n), sp))
    return tables


def _hyena_seq(mats, vx, vx_bf, lp, n_batch):
    h = vx.shape[2]
    length = 2 * h
    filt, filt_bf = _hyp_filters(length, lp)
    spec = _hyp_fwd(mats, filt_bf[None], 0)
    tables = _hyp_tap_tables(spec, filt, length)
    bias = lp['hy_bias'].astype(F32)
    y, y_bf = vx, vx_bf
    for o in range(HY_ORDER):
        g = _hyp_fwd(mats, y_bf, 0, tables[o])
        bias_row = jnp.tile(bias[o][None, :], (1, n_batch))
        last = o == HY_ORDER - 1
        res = _hyp_inv(mats, g, vx, o + 1, y, 0, bias_row, last)
        if last:
            return res[0].reshape(n_batch, length, HY_WIDTH)
        y, y_bf = res


def _lru_kernel(u_ref, g_ref, cw_ref, cb_ref, wa_ref, ba_ref, wx_ref, bx_ref, lam_ref, o_ref,
                pad_ref, y_ref, *, seq, ctx_len):
    tc = LRU_CHUNK
    halo = LRU_HALO
    width = LRU_WIDTH
    lat_off = halo
    ctx_off = 2 * halo + seq
    zero = jnp.zeros((halo, width), F32)
    pad_ref[0:halo, :] = zero
    pad_ref[lat_off:lat_off + seq, :] = u_ref[0, 0:seq, :]
    pad_ref[lat_off + seq:ctx_off, :] = zero
    pad_ref[ctx_off:ctx_off + ctx_len, :] = u_ref[0, seq:seq + ctx_len, :]
    pad_ref[ctx_off + ctx_len:ctx_off + ctx_len + halo, :] = zero
    row = lax.broadcasted_iota(I32, (tc, width), 0)
    n_win = tc + 2 * halo

    def chunk(pad_off, y_off, s, carry, d):
        wstart = pl.multiple_of(pad_off + s - halo, 8)
        win = pad_ref[pl.ds(wstart, n_win), :]
        cw = cw_ref[d]
        xc = cb_ref[d]
        for k in range(LRU_CONV):
            shift = (LRU_CONV - 1 - k) if d == 0 else -k
            rolled = win if shift == 0 else pltpu.roll(win, shift % n_win, axis=0)
            xc = xc + cw[k:k + 1] * rolled[halo:halo + tc]
        xb = xc.astype(BF16)
        r = _sigmoid(_dot(xb, wa_ref[d]) + ba_ref[d])
        gi = _sigmoid(_dot(xb, wx_ref[d]) + bx_ref[d])
        lam = lam_ref[d]
        softplus = jnp.maximum(-lam, 0.0) + jnp.log1p(jnp.exp(-jnp.abs(lam)))
        log_a = -LRU_C * r * softplus
        a = jnp.exp(log_a)
        bt = jnp.sqrt(-jnp.tanh(log_a) * (a * a + 1.0)) * (gi * xc)
        sft = 1
        while sft < tc:
            if d == 0:
                keep = row >= sft
                a_s = jnp.where(keep, pltpu.roll(a, sft, axis=0), 1.0)
                b_s = jnp.where(keep, pltpu.roll(bt, sft, axis=0), 0.0)
            else:
                keep = row < tc - sft
                a_s = jnp.where(keep, pltpu.roll(a, tc - sft, axis=0), 1.0)
                b_s = jnp.where(keep, pltpu.roll(bt, tc - sft, axis=0), 0.0)
            bt = a * b_s + bt
            a = a * a_s
            sft *= 2
        h = a * carry + bt
        yo = pl.multiple_of(y_off + s, 8)
        if d == 0:
            y_ref[pl.ds(yo, tc), :] = h
            return h[tc - 1:tc]
        y_ref[pl.ds(yo, tc), :] = y_ref[pl.ds(yo, tc), :] + h
        return h[0:1]

    n_lat = seq // tc
    n_ctx = ctx_len // tc
    for d in range(2):
        carry = jnp.zeros((1, width), F32)
        order = range(n_ctx) if d == 0 else range(n_ctx - 1, -1, -1)
        for c in order:
            carry = chunk(ctx_off, seq, c * tc, carry, d)

        def body(j, cr, d=d):
            jj = j if d == 0 else n_lat - 1 - j
            return chunk(lat_off, 0, jj * tc, cr, d)

        lax.fori_loop(0, n_lat, body, carry)
    o_ref[0] = (y_ref[...] * _gelu_tanh(g_ref[0])).astype(BF16)


def _block_diag(w):
    nd, nb, c, _ = w.shape
    out = jnp.zeros((nd, nb * c, nb * c), w.dtype)
    for n in range(nb):
        out = out.at[:, n * c:(n + 1) * c, n * c:(n + 1) * c].set(w[:, n])
    return out


def _lru_mixer(lu, lg, lp, seq):
    b, nt, w = lu.shape
    ctx_len = nt - seq
    row3 = lambda a: a.reshape(2, 1, w)
    args = (_pad_to(lp['lru_conv_w'], 8, 1), row3(lp['lru_conv_b']), _block_diag(lp['lru_wa']).astype(BF16),
            row3(lp['lru_ba']), _block_diag(lp['lru_wx']).astype(BF16), row3(lp['lru_bx']), row3(lp['lru_lambda']))
    full = lambda a: pl.BlockSpec(a.shape, lambda bb: (0,) * a.ndim)
    tok = pl.BlockSpec((1, nt, w), lambda bb: (bb, 0, 0))
    return pl.pallas_call(
        functools.partial(_lru_kernel, seq=seq, ctx_len=ctx_len),
        grid=(b,),
        in_specs=[tok, tok] + [full(a) for a in args],
        out_specs=tok,
        out_shape=jax.ShapeDtypeStruct((b, nt, w), BF16),
        scratch_shapes=[pltpu.VMEM((nt + 3 * LRU_HALO, w), F32), pltpu.VMEM((nt, w), F32)],
        compiler_params=_cparams("parallel"),
        name="rglru_scan",
    )(lu, lg, *args)


def _merge_kernel(xl_ref, xc_ref, mod_ref, g1_ref, a_ref, b_ref, c_ref, d_ref, wg_ref, wa_ref, wb_ref, wc_ref,
                  wd_ref, wo_ref, o_ref, *, n_lat_tiles):
    group, tm, dm = xl_ref.shape
    m = mod_ref[:, 0]
    x = _stream_tile(xl_ref, xc_ref, n_lat_tiles)
    h = _normmod(x, g1_ref[...], m[:, 1:2], m[:, 0:1]).reshape(group * tm, dm).astype(BF16)
    acc = None
    for k, (br, w) in enumerate(((a_ref, wa_ref), (b_ref, wb_ref), (c_ref, wc_ref), (d_ref, wd_ref))):
        term = (1.0 + jnp.tanh(_dot(h, wg_ref[:, k * dm:(k + 1) * dm]))) * _dot(
            br[...].reshape(group * tm, br.shape[-1]), w[...])
        acc = term if acc is None else acc + term
    y = _dot(acc.astype(BF16), wo_ref[...])
    o_ref[...] = x + m[:, 2:3] * y.reshape(group, tm, dm)


def _merge(stream, modtab, g1, branches, w_gate, lp, n_lat_tiles, tiles_used):
    b, _, d = stream[0].shape
    wbr = lp['w_branch']
    head_rows = lambda w, dv: jnp.concatenate(
        [_pad_to(w[hd * dv:(hd + 1) * dv], LANE, 0) for hd in range(4)], axis=0)
    wbr = 0.5 * wbr
    weights = (head_rows(wbr[0], MLA_V).astype(BF16), head_rows(wbr[1], NA_HEAD_DIM).astype(BF16),
               wbr[2].astype(BF16), wbr[3].astype(BF16), lp['w_out'].astype(BF16))
    group = _batch_group(b)
    full = lambda a: pl.BlockSpec(a.shape, lambda bb, i: (0,) * a.ndim)
    tok = lambda n: pl.BlockSpec((group, TM, n), lambda bb, i: (bb, i, 0))
    return pl.pallas_call(
        functools.partial(_merge_kernel, n_lat_tiles=n_lat_tiles),
        grid=(b // group, tiles_used),
        in_specs=_stream_specs(stream, group, n_lat_tiles)
                 + [pl.BlockSpec((group, 1, 8, d), _kind_map(n_lat_tiles)), full(g1)]
                 + [tok(br.shape[-1]) for br in branches] + [full(w_gate)] + [full(w) for w in weights],
        out_specs=tok(d),
        out_shape=jax.ShapeDtypeStruct((b, tiles_used * TM, d), F32),
        compiler_params=_cparams("parallel", "parallel"),
        name="merge_branches",
    )(stream[0], stream[1], modtab, g1, *branches, w_gate, *weights)


U32 = jnp.uint32
EXPERT_ROWS = 896


def _pack_pair(x):
    n = x.shape[-1] // 2
    hi = lax.bitcast_convert_type(x[:, :n].astype(BF16).astype(F32), U32)
    lo = lax.bitcast_convert_type(x[:, n:].astype(BF16).astype(F32), U32)
    return hi | (lo >> 16)


def _unpack_pair(p):
    hi = lax.bitcast_convert_type(p & jnp.uint32(0xFFFF0000), F32)
    lo = lax.bitcast_convert_type(p << 16, F32)
    return hi, lo


def _router_kernel(x_ref, mod_ref, g2_ref, rw_ref, rb_ref, tri_ref, h2_o, idx_o, wts_o, rank_o, cnt_o, carry):
    i = pl.program_id(0)

    @pl.when(i == 0)
    def _():
        carry[...] = jnp.zeros_like(carry)

    group, tm, d = x_ref.shape
    m = mod_ref[:, 0]
    h2 = _normmod(x_ref[...], g2_ref[...], m[:, 4:5], m[:, 3:4]).reshape(group * tm, d)
    half = h2.shape[-1] // 2
    h2_o[0] = _pack_pair(h2[:, :half])
    h2_o[1] = _pack_pair(h2[:, half:])
    h_hi = h2.astype(BF16)
    h_lo = (h2 - h_hi.astype(F32)).astype(BF16)
    logits = _dot_nt(rw_ref[0], h_hi) + _dot_nt(rw_ref[0], h_lo) + _dot_nt(rw_ref[1], h_hi)
    scores = _sigmoid(logits)
    biased = scores + rb_ref[...]
    expert = lax.broadcasted_iota(I32, scores.shape, 0)
    picks = []
    onehot_all = jnp.zeros(scores.shape, F32)
    for _ in range(TOP_K):
        best = jnp.max(biased, axis=0, keepdims=True)
        arg = jnp.min(jnp.where(biased == best, expert, N_EXPERTS), axis=0, keepdims=True)
        hit = expert == arg
        sel = jnp.sum(jnp.where(hit, scores, 0.0), axis=0, keepdims=True)
        biased = jnp.where(hit, -jnp.inf, biased)
        onehot_all = onehot_all + jnp.where(hit, 1.0, 0.0)
        picks.append((arg, hit, sel))
    total = picks[0][2]
    for _, _, sel in picks[1:]:
        total = total + sel
    earlier = _dot(onehot_all.astype(BF16), tri_ref[...]) + carry[...]
    pad_rows = TOPK_PAD - TOP_K
    ranks = [jnp.sum(jnp.where(hit, earlier, 0.0), axis=0, keepdims=True).astype(I32) for _, hit, _ in picks]
    scale = ROUTED_SCALE / total
    n_tok = scores.shape[1]
    idx_o[...] = jnp.concatenate([arg for arg, _, _ in picks] + [jnp.zeros((pad_rows, n_tok), I32)], axis=0)
    wts_o[...] = jnp.concatenate([sel * scale for _, _, sel in picks] + [jnp.zeros((pad_rows, n_tok), F32)], axis=0)
    rank_o[...] = jnp.concatenate(ranks + [jnp.zeros((pad_rows, n_tok), I32)], axis=0)
    carry[...] = carry[...] + jnp.sum(onehot_all, axis=1, keepdims=True)
    cnt_o[...] = carry[...]


def _tile_maps(n_groups, n_lat_tiles):
    tok = lambda f: (f % n_groups, f // n_groups, 0)
    mod = lambda f: (f % n_groups, jnp.where(f // n_groups >= n_lat_tiles, 1, 0), 0, 0)
    return tok, mod


def _route(x1, modtab, g2, lp, n_lat_tiles, tiles_used):
    b, _, d = x1.shape
    group = _batch_group(b, MOE_BATCH_GROUP)
    rows = group * TM
    t = b * tiles_used * TM
    rw = lp['router_w'].T
    rw_hi = rw.astype(BF16)
    rw = jnp.stack([rw_hi, (rw - rw_hi.astype(F32)).astype(BF16)])
    rb = lp['router_bias'].reshape(-1, 1)
    tri = (np.arange(rows)[:, None] < np.arange(rows)[None, :]).astype(np.float32)
    tri = jnp.asarray(tri, BF16)
    per_tok = lambda: pl.BlockSpec((TOPK_PAD, rows), lambda i: (0, i))
    full = lambda a: pl.BlockSpec(a.shape, lambda i: (0,) * a.ndim)
    tok, mod = _tile_maps(b // group, n_lat_tiles)
    return pl.pallas_call(
        _router_kernel,
        grid=(t // rows,),
        in_specs=[pl.BlockSpec((group, TM, d), tok), pl.BlockSpec((group, 1, 8, d), mod),
                  full(g2), full(rw), full(rb), full(tri)],
        out_specs=[pl.BlockSpec((2, rows, d // 4), lambda i: (0, i, 0)), per_tok(), per_tok(), per_tok(),
                   pl.BlockSpec((N_EXPERTS, 1), lambda i: (0, 0))],
        out_shape=[jax.ShapeDtypeStruct((2, t, d // 4), U32), jax.ShapeDtypeStruct((TOPK_PAD, t), I32),
                   jax.ShapeDtypeStruct((TOPK_PAD, t), F32), jax.ShapeDtypeStruct((TOPK_PAD, t), I32),
                   jax.ShapeDtypeStruct((N_EXPERTS, 1), F32)],
        scratch_shapes=[pltpu.VMEM((N_EXPERTS, 1), F32)],
        compiler_params=_cparams("arbitrary"),
        name="moe_router",
    )(x1, modtab, g2, rw, rb, tri)


SC_WINDOW = 128


def _sc_mesh():
    return plsc.VectorSubcoreMesh(core_axis_name="c", subcore_axis_name="s")


def _sc_scatter_rows(src, idx, n_out):
    n, width = src.shape
    k_rep = idx.shape[0]
    half = n // SC_WINDOW // 2

    @functools.partial(pl.kernel, out_type=jax.ShapeDtypeStruct((n_out, width), src.dtype), mesh=_sc_mesh(),
                       scratch_types=[], name="moe_dispatch_sc")
    def scatter(src_hbm, idx_hbm, out_hbm):
        def body(x_vmem, *i_vmems):
            for i_vmem in i_vmems:
                pltpu.sync_copy(x_vmem, out_hbm.at[i_vmem.at[0]])

        pltpu.emit_pipeline(
            body,
            grid=(2, half),
            in_specs=[pl.BlockSpec((SC_WINDOW, width), lambda a, i: (a * half + i, 0))]
                     + [pl.BlockSpec((1, SC_WINDOW), lambda a, i, k=k: (k, a * half + i)) for k in range(k_rep)],
            out_specs=[],
            core_axis_name=("c", "s"),
            dimension_semantics=(pltpu.PARALLEL, pltpu.PARALLEL),
        )(src_hbm, *([idx_hbm] * k_rep))

    return scatter(src, idx)


def _sc_gather_rows(src, idx):
    k_rep, n = idx.shape
    width = src.shape[1]
    n_win = n // SC_WINDOW

    @functools.partial(pl.kernel, out_type=jax.ShapeDtypeStruct((k_rep * n, width), src.dtype), mesh=_sc_mesh(),
                       scratch_types=[], name="moe_gather_sc")
    def gather(src_hbm, idx_hbm, out_hbm):
        def body(i_vmem, o_vmem):
            pltpu.sync_copy(src_hbm.at[i_vmem.at[0]], o_vmem)

        pltpu.emit_pipeline(
            body,
            grid=(k_rep, n_win),
            in_specs=[pl.BlockSpec((1, SC_WINDOW), lambda k, i: (k, i))],
            out_specs=[pl.BlockSpec((SC_WINDOW, width), lambda k, i: (k * n_win + i, 0))],
            core_axis_name=("c", "s"),
            dimension_semantics=(pltpu.PARALLEL, pltpu.PARALLEL),
        )(idx_hbm, out_hbm)

    return gather(src, idx)


def _unpack_planes(p0, p1):
    return _unpack_pair(p0) + _unpack_pair(p1)


def _dot_quarters(parts, w_ref):
    q = parts[0].shape[-1]
    acc = None
    for j, part in enumerate(parts):
        term = _dot(part.astype(BF16), w_ref[j * q:(j + 1) * q, :])
        acc = term if acc is None else acc + term
    return acc


def _expert_kernel(be_ref, nv_ref, blk_ref, xs_ref, wg_ref, wu_ref, wd_ref, ys_o, wg_s, wu_s, wd_s):
    del blk_ref
    i = pl.program_id(0)
    prev = be_ref[jnp.maximum(i - 1, 0)]

    @pl.when((i == 0) | (be_ref[i] != prev))
    def _():
        wg_s[...] = (0.5 * wg_ref[0]).astype(BF16)
        wu_s[...] = wu_ref[0].astype(BF16)
        wd_s[...] = wd_ref[0].astype(BF16)

    @pl.when(nv_ref[i] > 0)
    def _():
        keep = lax.broadcasted_iota(I32, xs_ref.shape[1:], 0) < nv_ref[i]
        parts = _unpack_planes(jnp.where(keep, xs_ref[0], jnp.uint32(0)), jnp.where(keep, xs_ref[1], jnp.uint32(0)))
        hid = _half_silu(_dot_quarters(parts, wg_s)) * _dot_quarters(parts, wu_s)
        y = _dot(hid.astype(BF16), wd_s[...])
        half = y.shape[-1] // 2
        ys_o[0] = _pack_pair(y[:, :half])
        ys_o[1] = _pack_pair(y[:, half:])


def _experts(xs, block_e, n_valid, block_src, weights, layer):
    _, n_rows, dq = xs.shape
    d = 4 * dq
    n_blocks = n_rows // EXPERT_ROWS
    hid = EXPERT_HIDDEN
    grid_spec = pltpu.PrefetchScalarGridSpec(
        num_scalar_prefetch=3,
        grid=(n_blocks,),
        in_specs=[pl.BlockSpec((2, EXPERT_ROWS, dq), lambda i, be, nv, blk: (0, blk[i], 0)),
                  pl.BlockSpec((None, 1, d, hid), lambda i, be, nv, blk: (layer, be[i], 0, 0)),
                  pl.BlockSpec((None, 1, d, hid), lambda i, be, nv, blk: (layer, be[i], 0, 0)),
                  pl.BlockSpec((None, 1, hid, d), lambda i, be, nv, blk: (layer, be[i], 0, 0))],
        out_specs=pl.BlockSpec((2, EXPERT_ROWS, dq), lambda i, be, nv, blk: (0, blk[i], 0)),
        scratch_shapes=[pltpu.VMEM((d, hid), BF16), pltpu.VMEM((d, hid), BF16), pltpu.VMEM((hid, d), BF16)],
    )
    return pl.pallas_call(
        _expert_kernel,
        grid_spec=grid_spec,
        out_shape=jax.ShapeDtypeStruct((2, n_rows, dq), U32),
        compiler_params=_cparams("arbitrary"),
        name="moe_experts",
    )(block_e, n_valid, block_src, xs, *weights)


def _combine_kernel(g_ref, wts_ref, h2_ref, x_ref, mod_ref, sg_ref, su_ref, sd_ref, gf_ref, o_ref, *, final):
    parts = _unpack_planes(h2_ref[0], h2_ref[1])
    hid = _half_silu(_dot_quarters(parts, sg_ref)) * _dot_quarters(parts, su_ref)
    shared = _dot(hid.astype(BF16), sd_ref[...])
    wts = wts_ref[...]
    routed = None
    for k in range(TOP_K):
        w = wts[:, k:k + 1]
        terms = [w * part for part in _unpack_planes(g_ref[k, 0], g_ref[k, 1])]
        routed = terms if routed is None else [r + t for r, t in zip(routed, terms)]
    f = shared + jnp.concatenate(routed, axis=1)
    m = mod_ref[:, 0]
    x2 = x_ref[...] + m[:, 5:6] * f.reshape(x_ref.shape)
    if final:
        x2 = x2 * lax.rsqrt(jnp.mean(x2 * x2, axis=-1, keepdims=True) + NORM_EPS) * gf_ref[...]
    o_ref[...] = x2


def _combine(g, wts, h2p, x1, modtab, lp, g_final, n_lat_tiles, tiles_used, final):
    b, _, d = x1.shape
    dq = d // 4
    weights = ((0.5 * lp['sh_w_gate']).astype(BF16), lp['sh_w_up'].astype(BF16), lp['sh_w_down'].astype(BF16),
               g_final.reshape(1, -1))
    full = lambda a: pl.BlockSpec(a.shape, lambda i: (0,) * a.ndim)
    group = _batch_group(b, MOE_BATCH_GROUP)
    rows = group * TM
    tok, mod = _tile_maps(b // group, n_lat_tiles)
    return pl.pallas_call(
        functools.partial(_combine_kernel, final=final),
        grid=(b * tiles_used * TM // rows,),
        in_specs=[pl.BlockSpec((TOP_K, 2, rows, dq), lambda i: (0, 0, i, 0)),
                  pl.BlockSpec((rows, TOPK_PAD), lambda i: (i, 0)),
                  pl.BlockSpec((2, rows, dq), lambda i: (0, i, 0)),
                  pl.BlockSpec((group, TM, d), tok), pl.BlockSpec((group, 1, 8, d), mod)]
                 + [full(w) for w in weights],
        out_specs=pl.BlockSpec((group, TM, d), tok),
        out_shape=jax.ShapeDtypeStruct((b, tiles_used * TM, d), F32),
        compiler_params=_cparams("parallel"),
        name="moe_combine",
    )(g, wts, h2p, x1, modtab, *weights)


def _sorted_rows_kernel(idx_ref, rank_ref, ps_ref, o_ref):
    expert = lax.broadcasted_iota(I32, (N_EXPERTS, idx_ref.shape[1]), 0)
    starts = ps_ref[...]
    idx = idx_ref[...]
    rows = [jnp.sum(jnp.where(idx[k:k + 1] == expert, starts, 0), axis=0, keepdims=True) for k in range(TOP_K)]
    pad = jnp.zeros((TOPK_PAD - TOP_K, idx.shape[1]), I32)
    o_ref[...] = jnp.concatenate(rows + [pad], axis=0) + rank_ref[...]


def _sorted_rows(idx, rank, p_starts):
    t = idx.shape[1]
    cols = TM * max(k for k in (8, 4, 2, 1) if (t // TM) % k == 0)
    per_tok = pl.BlockSpec((TOPK_PAD, cols), lambda i: (0, i))
    return pl.pallas_call(
        _sorted_rows_kernel,
        grid=(t // cols,),
        in_specs=[per_tok, per_tok, pl.BlockSpec((N_EXPERTS, 1), lambda i: (0, 0))],
        out_specs=per_tok,
        out_shape=jax.ShapeDtypeStruct((TOPK_PAD, t), I32),
        compiler_params=_cparams("parallel"),
        name="moe_sorted_rows",
    )(idx, rank, p_starts.reshape(-1, 1))


def _moe(x1, modtab, g2, lp, g_final, n_lat_tiles, final):
    b, nt, d = x1.shape
    tiles_used = n_lat_tiles if final else nt // TM
    t = b * tiles_used * TM
    h2p, idx, wts, rank, cnt = _route(x1, modtab, g2, lp, n_lat_tiles, tiles_used)
    counts = cnt[:, 0].astype(I32)
    padded = (counts + EXPERT_ROWS - 1) // EXPERT_ROWS * EXPERT_ROWS
    p_ends = jnp.cumsum(padded)
    p_starts = p_ends - padded
    n_blocks = (t * TOP_K + N_EXPERTS * (EXPERT_ROWS - 1)) // EXPERT_ROWS
    n_rows = n_blocks * EXPERT_ROWS
    dest = _sorted_rows(idx, rank, p_starts)[:TOP_K]
    plane_idx = jnp.concatenate([dest, dest + n_rows], axis=1)
    blk_start = jnp.arange(n_blocks, dtype=I32) * EXPERT_ROWS
    block_e = jnp.minimum(jnp.sum((p_ends[None, :] <= blk_start[:, None]).astype(I32), axis=1), N_EXPERTS - 1)
    n_valid = jnp.clip((p_starts + counts)[block_e] - blk_start, 0, EXPERT_ROWS).astype(I32)
    dq = d // 4
    xs = _sc_scatter_rows(h2p.reshape(2 * t, dq), plane_idx, 2 * n_rows).reshape(2, n_rows, dq)
    block_src = jnp.minimum(jnp.arange(n_blocks, dtype=I32), p_ends[-1] // EXPERT_ROWS - 1)
    ys = _experts(xs, block_e, n_valid, block_src, lp['expert_stacks'], lp['layer'])
    g = _sc_gather_rows(ys.reshape(2 * n_rows, dq), plane_idx).reshape(TOP_K, 2, t, dq)
    return _combine(g, wts.T, h2p, x1, modtab, lp, g_final, n_lat_tiles, tiles_used, final)


def _layer(stream, nt, c, c_ctx, lp, consts, g_final, seq, final):
    b = stream[0].shape[0]
    n_lat_tiles = seq // TM
    rope, mats_lat, mats_ctx = consts
    modtab = _mod_table(c, c_ctx, *lp['mod_stacks'], lp['layer'])
    g1 = lp['g_norm1'].reshape(1, -1)
    g2 = lp['g_norm2'].reshape(1, -1)
    pw = _proj_weights(lp)
    q, k, v, nq, nk, nv, hy, lu, lg = _project(stream, nt, modtab, g1, pw, rope, n_lat_tiles)
    tiles_used = n_lat_tiles if final else nt // TM
    br_a = _mla_attention(q, k, v, seq, tiles_used)
    br_b = _na_attention(nq, nk, nv, _na_bias_tables(lp['na_rpb'], seq // GRID_W), seq, tiles_used)
    pre = _hyp_pre(hy, lp['hy_short_w'], lp['hy_short_b'], seq)
    lat_f, lat_bf, ctx_f, ctx_bf = pre
    br_c = _hyena_seq(mats_lat, lat_f, lat_bf, lp, b)
    if not final:
        br_c = jnp.concatenate([br_c, _hyena_seq(mats_ctx, ctx_f, ctx_bf, lp, b)], axis=1)
    br_d = _lru_mixer(lu, lg, lp, seq)
    x1 = _merge(stream, modtab, g1, (br_a, br_b, br_c, br_d), pw['w_gate'], lp, n_lat_tiles, tiles_used)
    return _moe(x1, modtab, g2, lp, g_final, n_lat_tiles, final)


_LAYER_KEYS = ('w_mod', 'b_mod', 'g_norm1', 'g_norm2', 'w_in', 'mla_g_q', 'mla_w_uq', 'mla_g_kv', 'mla_w_ukv',
               'na_rpb', 'hy_short_w', 'hy_short_b', 'hy_w1', 'hy_b1', 'hy_w2', 'hy_b2', 'hy_w3', 'hy_decay',
               'hy_bias', 'lru_conv_w', 'lru_conv_b', 'lru_wa', 'lru_ba', 'lru_wx', 'lru_bx', 'lru_lambda',
               'w_branch', 'w_out', 'router_w', 'router_bias', 'exp_w_gate', 'exp_w_up', 'exp_w_down',
               'sh_w_gate', 'sh_w_up', 'sh_w_down')


def kernel(x, c, ctx, c_ctx, w_mod, b_mod, g_norm1, g_norm2, w_in, mla_g_q, mla_w_uq, mla_g_kv, mla_w_ukv, na_rpb, hy_short_w, hy_short_b, hy_w1, hy_b1, hy_w2, hy_b2, hy_w3, hy_decay, hy_bias, lru_conv_w, lru_conv_b, lru_wa, lru_ba, lru_wx, lru_bx, lru_lambda, w_branch, w_out, router_w, router_bias, exp_w_gate, exp_w_up, exp_w_down, sh_w_gate, sh_w_up, sh_w_down, g_final):
    stacked = dict(zip(_LAYER_KEYS, (w_mod, b_mod, g_norm1, g_norm2, w_in, mla_g_q, mla_w_uq, mla_g_kv, mla_w_ukv,
                                     na_rpb, hy_short_w, hy_short_b, hy_w1, hy_b1, hy_w2, hy_b2, hy_w3, hy_decay,
                                     hy_bias, lru_conv_w, lru_conv_b, lru_wa, lru_ba, lru_wx, lru_bx, lru_lambda,
                                     w_branch, w_out, router_w, router_bias, exp_w_gate, exp_w_up, exp_w_down,
                                     sh_w_gate, sh_w_up, sh_w_down)))
    b, seq, d = x.shape
    ctx_len = ctx.shape[1]
    depth = w_mod.shape[0]
    assert seq % TM == 0 and ctx_len % TM == 0 and seq // GRID_W >= NA_KEY_ROWS + 1
    nt = seq + ctx_len
    stream = (x, ctx, 0)
    consts = (_rope_tables(seq, seq + ctx_len), _hyp_matrices(seq), _hyp_matrices(ctx_len))
    for i in range(depth):
        big = ('w_mod', 'b_mod', 'exp_w_gate', 'exp_w_up', 'exp_w_down')
        lp = {name: w[i] for name, w in stacked.items() if name not in big}
        lp['layer'] = i
        lp['mod_stacks'] = (w_mod, b_mod)
        lp['expert_stacks'] = (exp_w_gate, exp_w_up, exp_w_down)
        xa = _layer(stream, nt, c, c_ctx, lp, consts, g_final, seq, i == depth - 1)
        stream = (xa, xa, seq // TM)
    return xa
```

```python
import functools
import math

import numpy as np
import jax
import jax.numpy as jnp
from jax import lax
from jax.experimental import pallas as pl
from jax.experimental.pallas import tpu as pltpu
from jax.experimental.pallas import tpu_sc as plsc

F32 = jnp.float32
BF16 = jnp.bfloat16
I32 = jnp.int32

TM = 256
LANE = 128
GRID_W = 64
N_MOD = 6
NORM_EPS = 1e-6

MLA_HEADS, MLA_NOPE, MLA_ROPE, MLA_V = 4, 64, 32, 64
MLA_Q_RANK, MLA_KV_RANK = 192, 128
MLA_Q_PAD = 256
ROPE_THETA = 10000.0

NA_HEADS, NA_HEAD_DIM, NA_WIN_R, NA_WIN_C = 4, 64, 8, 16
NA_TILE_ROWS = TM // GRID_W
NA_KEY_ROWS = NA_TILE_ROWS + NA_WIN_R - 1
NA_KEYS = NA_KEY_ROWS * GRID_W

HY_WIDTH, HY_ORDER, HY_SHORT, HY_BANDS, HY_FFN = 256, 2, 3, 16, 64
HY_EMB = 2 * HY_BANDS + 1

LRU_WIDTH, LRU_BLOCKS, LRU_CONV, LRU_C = 256, 4, 4, 8.0
LRU_CHUNK = 256
LRU_HALO = 8

N_EXPERTS, TOP_K, EXPERT_HIDDEN, ROUTED_SCALE, MOE_BLOCK = 64, 6, 256, 2.5, 256
TOPK_PAD = 8

VMEM_LIMIT = 56 * 1024 * 1024


def _cparams(*sem):
    return pltpu.CompilerParams(dimension_semantics=sem, vmem_limit_bytes=VMEM_LIMIT)


def _dot(a, b):
    return jnp.dot(a, b, preferred_element_type=F32)


def _dot_nt(a, b):
    return lax.dot_general(a, b, (((1,), (1,)), ((), ())), preferred_element_type=F32)


def _sigmoid(x):
    return jax.nn.sigmoid(x)


def _half_silu(g):
    return g * (1.0 + jnp.tanh(g))


def _silu(x):
    return x * _sigmoid(x)


def _gelu_tanh(x):
    return 0.5 * x * (1.0 + jnp.tanh(math.sqrt(2.0 / math.pi) * (x + 0.044715 * (x * x * x))))


def _normmod(x, g, scale, shift):
    y = x * lax.rsqrt(jnp.mean(x * x, axis=-1, keepdims=True) + NORM_EPS) * g
    return y * (1.0 + scale) + shift


MOD_COLS = 1024


def _mod_kernel(c_ref, w_ref, b_ref, o_ref):
    s = _silu(c_ref[...])
    o_ref[...] = _dot(s.astype(BF16), w_ref[...].astype(BF16)) + b_ref[...]


def _mod_table(c, c_ctx, w_mod, b_mod, layer):
    b, d = c.shape
    rows = -(-(b + 1) // 16) * 16
    cc = jnp.zeros((rows, d), F32).at[:b].set(c).at[b].set(c_ctx)
    tn = MOD_COLS
    mod = pl.pallas_call(
        _mod_kernel,
        grid=(N_MOD * d // tn,),
        in_specs=[pl.BlockSpec((rows, d), lambda j: (0, 0)),
                  pl.BlockSpec((None, d, tn), lambda j: (layer, 0, j)),
                  pl.BlockSpec((None, 1, tn), lambda j: (layer, 0, j))],
        out_specs=pl.BlockSpec((rows, tn), lambda j: (0, j)),
        out_shape=jax.ShapeDtypeStruct((rows, N_MOD * d), F32),
        compiler_params=_cparams("arbitrary"),
        name="mod_vectors",
    )(cc, w_mod, b_mod[:, None, :])
    lat = mod[:b].reshape(b, N_MOD, d)
    ctx = jnp.broadcast_to(mod[b].reshape(1, N_MOD, d), (b, N_MOD, d))
    tab = jnp.stack([lat, ctx], axis=1)
    return jnp.pad(tab, ((0, 0), (0, 0), (0, 8 - N_MOD), (0, 0)))


_C_QLAT, _C_KVLAT, _C_KR, _C_KRR, _C_NA, _C_HY, _C_LU, _C_LG, _C_END = (
    0, 256, 384, 512, 640, 640 + 3 * NA_HEADS * LANE, 640 + 1536 + 768, 640 + 1536 + 1024, 640 + 1536 + 1280)


def _stream_tile(lat_ref, ctx_ref, n_lat_tiles):
    return jnp.where(pl.program_id(1) < n_lat_tiles, lat_ref[...], ctx_ref[...])


def _stream_specs(stream, group, n_lat_tiles):
    lat_src, ctx_src, ctx_tile0 = stream
    d = lat_src.shape[-1]
    return [pl.BlockSpec((group, TM, d), lambda bb, i: (bb, jnp.minimum(i, n_lat_tiles - 1), 0)),
            pl.BlockSpec((group, TM, d), lambda bb, i: (bb, ctx_tile0 + jnp.maximum(i - n_lat_tiles, 0), 0))]


def _proj_kernel(xl_ref, xc_ref, mod_ref, g1_ref, w1_ref, gq_ref, gkv_ref, wuq_ref, wuqr_ref, wk_ref, wv_ref,
                 cos_ref, sin_ref, q_o, k_o, v_o, nq_o, nk_o, nv_o, hy_o, lu_o, lg_o, *, n_lat_tiles):
    group, tm, d = xl_ref.shape
    m = mod_ref[:, 0]
    x = _stream_tile(xl_ref, xc_ref, n_lat_tiles)
    h = _normmod(x, g1_ref[...], m[:, 1:2], m[:, 0:1]).reshape(group * tm, d)
    z = _dot(h.astype(BF16), w1_ref[...])
    qlat = z[:, _C_QLAT:_C_KVLAT]
    kvlat = z[:, _C_KVLAT:_C_KR]
    kr = z[:, _C_KR:_C_KRR]
    krr = z[:, _C_KRR:_C_NA]
    qn = qlat * lax.rsqrt(jnp.sum(qlat * qlat, axis=-1, keepdims=True) * (1.0 / MLA_Q_RANK) + NORM_EPS) * gq_ref[...]
    kvn = kvlat * lax.rsqrt(jnp.mean(kvlat * kvlat, axis=-1, keepdims=True) + NORM_EPS) * gkv_ref[...]
    qn = qn.astype(BF16)
    kvn = kvn.astype(BF16)
    q = _dot(qn, wuq_ref[...])
    qr = _dot(qn, wuqr_ref[...])
    kk = _dot(kvn, wk_ref[...])
    vv = _dot(kvn, wv_ref[...])
    cos = jnp.concatenate([cos_ref[...]] * group, axis=0)
    sin = jnp.concatenate([sin_ref[...]] * group, axis=0)
    krope = kr * cos + krr * sin
    den_lane = lax.broadcasted_iota(I32, cos.shape, 1) == SOFTMAX_DEN_LANE
    def put_heads(ref, hd, val):
        for g in range(group):
            ref[g, hd] = val[g * tm:(g + 1) * tm].astype(BF16)

    def put_pairs(ref, hd, val):
        off = (hd % 2) * LANE
        for g in range(group):
            ref[g, hd // 2, :, off:off + LANE] = val[g * tm:(g + 1) * tm].astype(BF16)

    for hd in range(MLA_HEADS):
        sl = slice(hd * LANE, (hd + 1) * LANE)
        put_heads(q_o, hd, q[:, sl] * cos + qr[:, sl] * sin)
        put_heads(k_o, hd, kk[:, sl] + krope)
        put_heads(v_o, hd, jnp.where(den_lane, 1.0, vv[:, sl]))
    for hd in range(NA_HEADS):
        for which, ref in enumerate((nq_o, nk_o, nv_o)):
            lo = _C_NA + (which * NA_HEADS + hd) * LANE
            blk = z[:, lo:lo + LANE]
            if which == 2:
                put_pairs(ref, hd, jnp.where(den_lane, 1.0, blk))
            else:
                put_heads(ref, hd, blk)
    hy_o[...] = z[:, _C_HY:_C_LU].reshape(group, tm, _C_LU - _C_HY)
    lu_o[...] = z[:, _C_LU:_C_LG].reshape(group, tm, _C_LG - _C_LU)
    lg_o[...] = z[:, _C_LG:_C_END].reshape(group, tm, _C_END - _C_LG)


def _pad_to(a, n, axis):
    pad = [(0, 0)] * a.ndim
    pad[axis] = (0, n - a.shape[axis])
    return jnp.pad(a, pad)


def _rot_cols(w):
    half = w.shape[-1] // 2
    return jnp.concatenate([-w[..., half:], w[..., :half]], axis=-1)


def _head_blocks(cols_per_head):
    out = []
    for pieces in cols_per_head:
        k = pieces[0][0].shape[0]
        blk = jnp.zeros((k, LANE), F32)
        for arr, off in pieces:
            blk = blk.at[:, off:off + arr.shape[1]].set(arr)
        out.append(blk)
    return jnp.concatenate(out, axis=1)


def _proj_weights(lp):
    w_in = lp['w_in']
    d = w_in.shape[0]
    o = 0
    parts = {}
    for name, n in (('q', MLA_Q_RANK), ('kv', MLA_KV_RANK), ('kr', MLA_ROPE), ('na', 3 * NA_HEADS * NA_HEAD_DIM),
                    ('hy', 3 * HY_WIDTH), ('lu', LRU_WIDTH), ('lg', LRU_WIDTH), ('gt', 4 * d)):
        parts[name] = w_in[:, o:o + n]
        o += n
    zeros = lambda n: jnp.zeros((d, n), F32)
    kr_blk = jnp.concatenate([zeros(MLA_NOPE), parts['kr'], zeros(LANE - MLA_NOPE - MLA_ROPE)], axis=1)
    krr_blk = jnp.concatenate([zeros(MLA_NOPE), _rot_cols(parts['kr']), zeros(LANE - MLA_NOPE - MLA_ROPE)], axis=1)
    na_scale = NA_HEAD_DIM ** -0.5
    na_cols = []
    for which in range(3):
        for hd in range(NA_HEADS):
            lo = (which * NA_HEADS + hd) * NA_HEAD_DIM
            blk = parts['na'][:, lo:lo + NA_HEAD_DIM] * (na_scale if which == 0 else 1.0)
            na_cols.append(_pad_to(blk, LANE, 1))
    w1 = jnp.concatenate([_pad_to(parts['q'], MLA_Q_PAD, 1), parts['kv'], kr_blk, krr_blk] + na_cols
                         + [parts['hy'], parts['lu'], parts['lg']], axis=1).astype(BF16)
    mla_scale = (MLA_NOPE + MLA_ROPE) ** -0.5
    wuq = _pad_to(lp['mla_w_uq'], MLA_Q_PAD, 0) * mla_scale
    dq = MLA_NOPE + MLA_ROPE
    wuq_main = _head_blocks([[(wuq[:, hd * dq:hd * dq + dq], 0)] for hd in range(MLA_HEADS)])
    wuq_rot = _head_blocks([[(_rot_cols(wuq[:, hd * dq + MLA_NOPE:hd * dq + dq]), MLA_NOPE)]
                            for hd in range(MLA_HEADS)])
    dkv = MLA_NOPE + MLA_V
    wukv = lp['mla_w_ukv']
    wk = _head_blocks([[(wukv[:, hd * dkv:hd * dkv + MLA_NOPE], 0)] for hd in range(MLA_HEADS)])
    wv = _head_blocks([[(wukv[:, hd * dkv + MLA_NOPE:hd * dkv + dkv], 0)] for hd in range(MLA_HEADS)])
    gq = _pad_to(lp['mla_g_q'].reshape(1, -1), MLA_Q_PAD, 1)
    gkv = lp['mla_g_kv'].reshape(1, -1)
    return dict(w1=w1, w_gate=(0.5 * parts['gt']).astype(BF16), gq=gq, gkv=gkv, wuq=wuq_main.astype(BF16),
                wuq_rot=wuq_rot.astype(BF16), wk=wk.astype(BF16), wv=wv.astype(BF16))


def _rope_tables(seq, n_tok):
    t = jnp.arange(seq, dtype=I32)
    row = (t // GRID_W).astype(F32)
    col = (t % GRID_W).astype(F32)
    n_axis = MLA_ROPE // 4
    inv_freq = ROPE_THETA ** (-jnp.arange(n_axis, dtype=F32) / n_axis)
    ang = jnp.concatenate([row[:, None] * inv_freq, col[:, None] * inv_freq], axis=-1)
    cos = jnp.concatenate([jnp.cos(ang), jnp.cos(ang)], axis=-1)
    sin = jnp.concatenate([jnp.sin(ang), jnp.sin(ang)], axis=-1)
    cos_t = jnp.ones((n_tok, LANE), F32).at[:seq, MLA_NOPE:MLA_NOPE + MLA_ROPE].set(cos)
    sin_t = jnp.zeros((n_tok, LANE), F32).at[:seq, MLA_NOPE:MLA_NOPE + MLA_ROPE].set(sin)
    return cos_t, sin_t


def _batch_group(b, cap=2):
    return max(g for g in (4, 2, 1) if g <= cap and b % g == 0)


MOE_BATCH_GROUP = 4


def _kind_map(n_lat_tiles):
    return lambda b, i: (b, jnp.where(i >= n_lat_tiles, 1, 0), 0, 0)


def _project(stream, nt, modtab, g1, pw, rope, n_lat_tiles):
    b, _, d = stream[0].shape
    cos_t, sin_t = rope
    group = _batch_group(b)
    full = lambda a: pl.BlockSpec(a.shape, lambda bb, i: (0,) * a.ndim)
    head_out = lambda: pl.BlockSpec((group, MLA_HEADS, TM, LANE), lambda bb, i: (bb, 0, i, 0))
    tok_out = lambda n: pl.BlockSpec((group, TM, n), lambda bb, i: (bb, i, 0))
    head_shape = jax.ShapeDtypeStruct((b, MLA_HEADS, nt, LANE), BF16)
    pair_out = lambda: pl.BlockSpec((group, MLA_HEADS // 2, TM, 2 * LANE), lambda bb, i: (bb, 0, i, 0))
    pair_shape = jax.ShapeDtypeStruct((b, MLA_HEADS // 2, nt, 2 * LANE), BF16)
    tok_shape = lambda n: jax.ShapeDtypeStruct((b, nt, n), F32)
    weights = (g1, pw['w1'], pw['gq'], pw['gkv'], pw['wuq'], pw['wuq_rot'], pw['wk'], pw['wv'])
    return pl.pallas_call(
        functools.partial(_proj_kernel, n_lat_tiles=n_lat_tiles),
        grid=(b // group, nt // TM),
        in_specs=_stream_specs(stream, group, n_lat_tiles)
                 + [pl.BlockSpec((group, 1, 8, d), _kind_map(n_lat_tiles))]
                 + [full(w) for w in weights]
                 + [pl.BlockSpec((TM, LANE), lambda bb, i: (i, 0))] * 2,
        out_specs=[head_out(), head_out(), head_out(), head_out(), head_out(), pair_out(),
                   tok_out(3 * HY_WIDTH), tok_out(LRU_WIDTH), tok_out(LRU_WIDTH)],
        out_shape=[head_shape, head_shape, head_shape, head_shape, head_shape, pair_shape,
                   tok_shape(3 * HY_WIDTH), tok_shape(LRU_WIDTH), tok_shape(LRU_WIDTH)],
        compiler_params=_cparams("parallel", "parallel"),
        name="input_projection",
    )(stream[0], stream[1], modtab, *weights, cos_t, sin_t)


SOFTMAX_DEN_LANE = 64


def _softmax_pv(parts, lane_off=0):
    m = None
    for s, _ in parts:
        mm = jnp.max(s, axis=-1, keepdims=True)
        m = mm if m is None else jnp.maximum(m, mm)
    acc = None
    for s, v in parts:
        o = _dot(jnp.exp(s - m).astype(BF16), v)
        acc = o if acc is None else acc + o
    den = lane_off + SOFTMAX_DEN_LANE
    return acc[:, lane_off:lane_off + LANE] / acc[:, den:den + 1]


def _mla_kernel(q_ref, k_ref, v_ref, o_ref, *, seq, n_lat_tiles):
    i = pl.program_id(1)
    nt = k_ref.shape[2]

    def attend(lo, hi):
        for hd in range(MLA_HEADS):
            s = _dot_nt(q_ref[0, hd], k_ref[0, hd, lo:hi, :])
            o = _softmax_pv([(s, v_ref[0, hd, lo:hi, :])])
            o_ref[0, :, hd * LANE:(hd + 1) * LANE] = o.astype(BF16)

    @pl.when(i < n_lat_tiles)
    def _():
        attend(0, nt)

    @pl.when(i >= n_lat_tiles)
    def _():
        attend(seq, nt)


def _mla_attention(q, k, v, seq, tiles_used):
    b, h, nt, _ = q.shape
    kv_spec = pl.BlockSpec((1, h, nt, LANE), lambda bb, i: (bb, 0, 0, 0))
    return pl.pallas_call(
        functools.partial(_mla_kernel, seq=seq, n_lat_tiles=seq // TM),
        grid=(b, tiles_used),
        in_specs=[pl.BlockSpec((1, h, TM, LANE), lambda bb, i: (bb, 0, i, 0)), kv_spec, kv_spec],
        out_specs=pl.BlockSpec((1, TM, h * LANE), lambda bb, i: (bb, i, 0)),
        out_shape=jax.ShapeDtypeStruct((b, tiles_used * TM, h * LANE), BF16),
        compiler_params=_cparams("parallel", "arbitrary"),
        name="mla_attention",
    )(q, k, v)


def _na_bias_tables(rpb, rows):
    n_blk = rows // NA_TILE_ROWS
    col = np.arange(GRID_W)
    c0 = np.clip(col - NA_WIN_C // 2, 0, GRID_W - NA_WIN_C)
    in_win = (col[None, :] >= c0[:, None]) & (col[None, :] < c0[:, None] + NA_WIN_C)
    dc = np.clip(col[None, :] - col[:, None], 1 - NA_WIN_C, NA_WIN_C - 1) + NA_WIN_C - 1
    rpb = rpb.astype(F32)
    tables = []
    for j in (0, 1, n_blk - 1):
        w0 = min(max(NA_TILE_ROWS * j - NA_WIN_R // 2, 0), rows - NA_KEY_ROWS)
        r = NA_TILE_ROWS * j + np.arange(NA_TILE_ROWS)
        kr = w0 + np.arange(NA_KEY_ROWS)
        r0 = np.clip(r - NA_WIN_R // 2, 0, rows - NA_WIN_R)
        row_ok = (kr[None, :] >= r0[:, None]) & (kr[None, :] < r0[:, None] + NA_WIN_R)
        dr = np.clip(kr[None, :] - r[:, None] + NA_WIN_R - 1, 0, 2 * NA_WIN_R - 2)
        oh_r = jnp.asarray(np.eye(2 * NA_WIN_R - 1, dtype=np.float32)[dr.reshape(-1)])
        oh_c = jnp.asarray(np.eye(2 * NA_WIN_C - 1, dtype=np.float32)[dc.reshape(-1)])
        bias = jnp.einsum('ar,hrc,bc->hab', oh_r, rpb, oh_c, precision=lax.Precision.HIGHEST)
        bias = bias.reshape(NA_HEADS, NA_TILE_ROWS, NA_KEY_ROWS, GRID_W, GRID_W)
        mask = row_ok[:, :, None, None] & in_win[None, None, :, :]
        bias = jnp.where(jnp.asarray(mask)[None], bias, -jnp.inf)
        tables.append(bias.transpose(0, 1, 3, 2, 4).reshape(NA_HEADS, TM, NA_KEYS))
    return jnp.stack(tables)


def _na_kernel(q_ref, k_ref, v_ref, bias_ref, o_ref, *, seq, n_lat_tiles):
    i = pl.program_id(1)
    nt = k_ref.shape[2]
    rows = seq // GRID_W

    @pl.when(i < n_lat_tiles)
    def _():
        w0 = jnp.clip(NA_TILE_ROWS * i - NA_WIN_R // 2, 0, rows - NA_KEY_ROWS)
        start = pl.multiple_of(w0 * GRID_W, GRID_W)
        for hd in range(NA_HEADS):
            q = q_ref[0, hd]
            s_loc = _dot_nt(q, k_ref[0, hd, pl.ds(start, NA_KEYS), :]) + bias_ref[0, hd]
            s_ctx = _dot_nt(q, k_ref[0, hd, seq:nt, :])
            o = _softmax_pv([(s_loc, v_ref[0, hd // 2, pl.ds(start, NA_KEYS), :]),
                             (s_ctx, v_ref[0, hd // 2, seq:nt, :])], (hd % 2) * LANE)
            o_ref[0, :, hd * LANE:(hd + 1) * LANE] = o.astype(BF16)

    @pl.when(i >= n_lat_tiles)
    def _():
        for hd in range(NA_HEADS):
            s = _dot_nt(q_ref[0, hd], k_ref[0, hd, seq:nt, :])
            o = _softmax_pv([(s, v_ref[0, hd // 2, seq:nt, :])], (hd % 2) * LANE)
            o_ref[0, :, hd * LANE:(hd + 1) * LANE] = o.astype(BF16)


def _na_attention(q, k, v, bias, seq, tiles_used):
    b, h, nt, _ = q.shape
    n_lat = seq // TM
    kv_spec = pl.BlockSpec((1, h, nt, LANE), lambda bb, i: (bb, 0, 0, 0))
    cfg = lambda bb, i: (jnp.where(i == 0, 0, jnp.where(i >= n_lat - 1, 2, 1)), 0, 0, 0)
    return pl.pallas_call(
        functools.partial(_na_kernel, seq=seq, n_lat_tiles=n_lat),
        grid=(b, tiles_used),
        in_specs=[pl.BlockSpec((1, h, TM, LANE), lambda bb, i: (bb, 0, i, 0)), kv_spec,
                  pl.BlockSpec((1, h // 2, nt, 2 * LANE), lambda bb, i: (bb, 0, 0, 0)),
                  pl.BlockSpec((1, h, TM, NA_KEYS), cfg)],
        out_specs=pl.BlockSpec((1, TM, h * LANE), lambda bb, i: (bb, i, 0)),
        out_shape=jax.ShapeDtypeStruct((b, tiles_used * TM, h * LANE), BF16),
        compiler_params=_cparams("parallel", "arbitrary"),
        name="neighbourhood_attention",
    )(q, k, v, bias)


def _hp_dot(a, b):
    return jnp.dot(a, b, preferred_element_type=F32, precision=lax.Precision.HIGHEST)


def _hy_filter_kernel(f_ref, w1_ref, b1_ref, w2_ref, b2_ref, w3_ref, dec_ref, o_ref, obf_ref):
    f = f_ref[...]
    h = jnp.sin(_hp_dot(f, w1_ref[...]) + b1_ref[...])
    h = jnp.sin(_hp_dot(h, w2_ref[...]) + b2_ref[...])
    h = _hp_dot(h, w3_ref[...])
    h = h * jnp.exp(-f[:, 0:1] * jnp.abs(dec_ref[...]))
    o_ref[...] = h
    obf_ref[...] = h.astype(BF16)


def _hy_pos_features(length):
    t = jnp.linspace(0.0, 1.0, length, dtype=F32)[:, None]
    w = 2.0 * math.pi * jnp.arange(length, dtype=F32)[:, None] / length
    f = jnp.linspace(1e-4, HY_BANDS - 1, HY_BANDS, dtype=F32)[None, :]
    z = w * f
    return jnp.concatenate([t, jnp.cos(z), -jnp.sin(z)], axis=-1)


DFT_ROW_STEP = 64
DFT_ROWS = 256
DFT_FWD_COLS = 1024
DFT_INV_COLS = 512
FILTER_ROWS = 512


def _col_block(nc, cap):
    return min(nc, cap)


def _hyp_pre_kernel(z_ref, w_ref, b_ref, lat_o, lat_bf_o, ctx_o, ctx_bf_o, *, seq, ctx_len):
    w = w_ref[...]
    bias = b_ref[...]
    for lo, length, o_ref, obf_ref in ((0, seq, lat_o, lat_bf_o), (seq, ctx_len, ctx_o, ctx_bf_o)):
        h = length // 2
        even = z_ref[0, pl.ds(lo, h, stride=2), :]
        odd = z_ref[0, pl.ds(lo + 1, h, stride=2), :]
        row = lax.broadcasted_iota(I32, even.shape, 0)
        odd_prev = jnp.where(row == 0, 0.0, pltpu.roll(odd, 1, axis=0))
        even_next = jnp.where(row == h - 1, 0.0, pltpu.roll(even, h - 1, axis=0))
        y_even = bias + w[0:1] * odd_prev + w[1:2] * even + w[2:3] * odd
        y_odd = bias + w[0:1] * even + w[1:2] * odd + w[2:3] * even_next
        for r, y in enumerate((y_even, y_odd)):
            o_ref[0, r] = y
            obf_ref[0, r] = y.astype(BF16)


def _hyp_pre(hy, w, bvec, seq):
    b, nt, _ = hy.shape
    ctx_len = nt - seq
    wpad = _pad_to(w, 8, 0)
    per_plane = HY_WIDTH // LANE
    out_specs, out_shape = [], []
    for length in (seq, ctx_len):
        for dt in (F32, BF16):
            out_specs.append(pl.BlockSpec((1, 2, length // 2, LANE),
                                          lambda bb, g: (g // per_plane, 0, 0, bb * per_plane + g % per_plane)))
            out_shape.append(jax.ShapeDtypeStruct((3, 2, length // 2, b * HY_WIDTH), dt))
    return pl.pallas_call(
        functools.partial(_hyp_pre_kernel, seq=seq, ctx_len=ctx_len),
        grid=(b, 3 * per_plane),
        in_specs=[pl.BlockSpec((1, nt, LANE), lambda bb, g: (bb, 0, g)),
                  pl.BlockSpec((8, LANE), lambda bb, g: (0, g)),
                  pl.BlockSpec((1, LANE), lambda bb, g: (0, g))],
        out_specs=out_specs,
        out_shape=out_shape,
        compiler_params=_cparams("parallel", "parallel"),
        name="hyena_short_conv",
    )(hy, wpad, bvec.reshape(1, -1))


def _hyp_filters(length, lp):
    feats = _hy_pos_features(length)
    feats = feats.reshape(length // 2, 2, HY_EMB).transpose(1, 0, 2).reshape(length, HY_EMB)
    n_out = HY_ORDER * 2 * HY_WIDTH
    tl = min(length // 2, FILTER_ROWS)
    full = lambda a: pl.BlockSpec(a.shape, lambda i: (0,) * a.ndim)
    args = (lp['hy_w1'], lp['hy_b1'].reshape(1, -1), lp['hy_w2'], lp['hy_b2'].reshape(1, -1), lp['hy_w3'],
            lp['hy_decay'].reshape(1, n_out))
    filt, filt_bf = pl.pallas_call(
        _hy_filter_kernel,
        grid=(length // tl,),
        in_specs=[pl.BlockSpec((tl, HY_EMB), lambda i: (i, 0))] + [full(a) for a in args],
        out_specs=[pl.BlockSpec((tl, n_out), lambda i: (i, 0))] * 2,
        out_shape=[jax.ShapeDtypeStruct((length, n_out), F32), jax.ShapeDtypeStruct((length, n_out), BF16)],
        compiler_params=_cparams("parallel"),
        name="hyena_filter_mlp",
    )(feats, *args)
    return filt.reshape(2, length // 2, n_out), filt_bf.reshape(2, length // 2, n_out)


def _hyp_matrix_kernel(c1_ref, s1_ref, c2_ref, s2_ref, ck_ref, sk_ref, ckr_ref, skr_ref,
                       ce_o, se_o, co_o, so_o, set_o, cot_o, sot_o):
    c1, s1 = c1_ref[0], s1_ref[0]
    c2, s2 = c2_ref[...], s2_ref[...]
    cos = c1 * c2 - s1 * s2
    sin = s1 * c2 + c1 * s2
    row = lax.broadcasted_iota(I32, cos.shape, 0)
    col = lax.broadcasted_iota(I32, cos.shape, 1)
    first_row = (row + pl.program_id(0) * DFT_ROW_STEP) == 0
    alt_col = jnp.where(col % 2 == 0, 1.0, -1.0)
    alt_row = jnp.where(row % 2 == 0, 1.0, -1.0)
    ck, sk = ck_ref[...], sk_ref[...]
    ce_o[...] = cos.astype(BF16)
    se_o[...] = jnp.where(first_row, alt_col, -sin).astype(BF16)
    co_o[...] = (cos * ck - sin * sk).astype(BF16)
    so_o[...] = jnp.where(first_row, alt_col, -(sin * ck + cos * sk)).astype(BF16)
    ckr, skr = ckr_ref[...], skr_ref[...]
    set_o[...] = jnp.where(col == 0, alt_row, -sin).astype(BF16)
    cot_o[...] = (cos * ckr - sin * skr).astype(BF16)
    sot_o[...] = jnp.where(col == 0, alt_row, -(sin * ckr + cos * skr)).astype(BF16)


def _hyp_matrices(length):
    h = length // 2
    step = DFT_ROW_STEP
    m = jnp.arange(h, dtype=I32)

    def trig(kv):
        ang = ((kv[:, None] * m[None, :]) % length).astype(F32) * (2.0 * math.pi / length)
        return jnp.cos(ang), jnp.sin(ang)

    c1, s1 = trig(jnp.arange(h // step, dtype=I32) * step)
    c2, s2 = trig(jnp.arange(step, dtype=I32))
    half_angle = m.astype(F32) * (math.pi / length)
    ck, sk = jnp.cos(half_angle), jnp.sin(half_angle)
    coarse = pl.BlockSpec((1, 1, h), lambda j: (j, 0, 0))
    fine = pl.BlockSpec((step, h), lambda j: (0, 0))
    per_row = pl.BlockSpec((step, 1), lambda j: (j, 0))
    per_col = pl.BlockSpec((1, h), lambda j: (0, 0))
    out = pl.BlockSpec((step, h), lambda j: (j, 0))
    ce, se, co, so, se_t, co_t, so_t = pl.pallas_call(
        _hyp_matrix_kernel,
        grid=(h // step,),
        in_specs=[coarse, coarse, fine, fine, per_row, per_row, per_col, per_col],
        out_specs=[out] * 7,
        out_shape=[jax.ShapeDtypeStruct((h, h), BF16)] * 7,
        compiler_params=_cparams("parallel"),
        name="dft_matrices",
    )(c1[:, None, :], s1[:, None, :], c2, s2, ck[:, None], sk[:, None], ck[None, :], sk[None, :])
    return dict(fwd=(ce, se, co, so), inv=(ce, se_t, co_t, so_t))


def _hyp_fwd_kernel(ce_ref, se_ref, co_ref, so_ref, xe_ref, xo_ref, *rest, with_taps):
    xe, xo = xe_ref[...], xo_ref[...]
    a_re, a_im = _dot(ce_ref[...], xe), _dot(se_ref[...], xe)
    b_re, b_im = _dot(co_ref[...], xo), _dot(so_ref[...], xo)
    if not with_taps:
        for ref, val in zip(rest, (a_re, a_im, b_re, b_im)):
            ref[...] = val
        return
    t1re_ref, t1im_ref, t2re_ref, t2im_ref, sp_ref, gere_o, geim_o, gore_o, goim_o = rest
    t1re, t1im, t2re, t2im = t1re_ref[...], t1im_ref[...], t2re_ref[...], t2im_ref[...]
    sp = sp_ref[...]
    first_block = pl.program_id(1) == 0
    row8 = lax.broadcasted_iota(I32, (8, HY_WIDTH), 0)
    for bb in range(xe.shape[1] // HY_WIDTH):
        sl = slice(bb * HY_WIDTH, (bb + 1) * HY_WIDTH)
        are, aim, bre, bim = a_re[:, sl], a_im[:, sl], b_re[:, sl], b_im[:, sl]
        u1re, u1im = are + bre, aim + bim
        u2re, u2im = are - bre, bim - aim
        z1re, z1im = u1re * t1re - u1im * t1im, u1re * t1im + u1im * t1re
        z2re, z2im = u2re * t2re - u2im * t2im, u2re * t2im + u2im * t2re
        gere_o[:, sl] = (z1re + z2re).astype(BF16)
        geim_o[:, sl] = (z1im - z2im).astype(BF16)
        gore_o[:, sl] = (z1re - z2re).astype(BF16)
        goim_o[:, sl] = (z1im + z2im).astype(BF16)

        @pl.when(first_block)
        def _():
            u0, ul = u1re[0:8], u2re[0:8]
            a_s, b_s = aim[0:8], bim[0:8]
            dc, ny, mre, mim = sp[0:1], sp[1:2], sp[2:3], sp[3:4]
            first = row8 == 0
            gere_o[0:8, sl] = jnp.where(first, u0 * dc + ul * ny, (z1re + z2re)[0:8]).astype(BF16)
            gore_o[0:8, sl] = jnp.where(first, u0 * dc - ul * ny, (z1re - z2re)[0:8]).astype(BF16)
            geim_o[0:8, sl] = jnp.where(first, a_s * mre + b_s * mim, (z1im - z2im)[0:8]).astype(BF16)
            goim_o[0:8, sl] = jnp.where(first, b_s * mre - a_s * mim, (z1im + z2im)[0:8]).astype(BF16)


def _hyp_fwd(mats, x, plane, taps=None):
    _, _, h, nc = x.shape
    tk = min(h, DFT_ROWS)
    cb = _col_block(nc, DFT_FWD_COLS)
    grid = (nc // cb, h // tk)
    m_spec = pl.BlockSpec((tk, h), lambda c, j: (j, 0))
    x_spec = lambda r: pl.BlockSpec((None, None, h, cb), lambda c, j: (plane, r, 0, c))
    o_spec = pl.BlockSpec((tk, cb), lambda c, j: (j, c))
    if taps is None:
        return pl.pallas_call(
            functools.partial(_hyp_fwd_kernel, with_taps=False),
            grid=grid, in_specs=[m_spec] * 4 + [x_spec(0), x_spec(1)], out_specs=[o_spec] * 4,
            out_shape=[jax.ShapeDtypeStruct((h, nc), F32)] * 4,
            compiler_params=_cparams("parallel", "arbitrary"),
            name="hyena_dft_filters",
        )(*mats['fwd'], x, x)
    t_spec = pl.BlockSpec((tk, HY_WIDTH), lambda c, j: (j, 0))
    sp_spec = pl.BlockSpec((8, HY_WIDTH), lambda c, j: (0, 0))
    return pl.pallas_call(
        functools.partial(_hyp_fwd_kernel, with_taps=True),
        grid=grid, in_specs=[m_spec] * 4 + [x_spec(0), x_spec(1)] + [t_spec] * 4 + [sp_spec],
        out_specs=[o_spec] * 4,
        out_shape=[jax.ShapeDtypeStruct((h, nc), BF16)] * 4,
        compiler_params=_cparams("parallel", "arbitrary"),
        name="hyena_dft_forward",
    )(*mats['fwd'], x, x, *taps)


def _hyp_inv_kernel(ce_ref, set_ref, cot_ref, sot_ref, gere_ref, geim_ref, gore_ref, goim_ref,
                    gate_ref, prev_ref, bias_ref, *outs, last):
    conv = (_dot(ce_ref[...], gere_ref[...]) + _dot(set_ref[...], geim_ref[...]),
            _dot(cot_ref[...], gore_ref[...]) + _dot(sot_ref[...], goim_ref[...]))
    bias = bias_ref[...]
    for r in range(2):
        y = gate_ref[r] * (conv[r] + prev_ref[r] * bias)
        if last:
            (tok_o,) = outs
            for bb in range(y.shape[1] // HY_WIDTH):
                tok_o[bb, :, r * HY_WIDTH:(r + 1) * HY_WIDTH] = y[:, bb * HY_WIDTH:(bb + 1) * HY_WIDTH].astype(BF16)
        else:
            y_o, ybf_o = outs
            y_o[r] = y
            ybf_o[r] = y.astype(BF16)


def _hyp_inv(mats, g, gate, gate_plane, prev, prev_plane, bias_row, last):
    h, nc = g[0].shape
    tm = min(h, DFT_ROWS)
    cb = _col_block(nc, DFT_INV_COLS)
    grid = (nc // cb, h // tm)
    m_spec = pl.BlockSpec((tm, h), lambda c, i: (i, 0))
    g_spec = pl.BlockSpec((h, cb), lambda c, i: (0, c))
    e_spec = lambda plane: pl.BlockSpec((None, 2, tm, cb), lambda c, i: (plane, 0, i, c))
    b_spec = pl.BlockSpec((1, cb), lambda c, i: (0, c))
    if last:
        out_specs = [pl.BlockSpec((cb // HY_WIDTH, tm, 2 * HY_WIDTH), lambda c, i: (c, i, 0))]
        out_shape = [jax.ShapeDtypeStruct((nc // HY_WIDTH, h, 2 * HY_WIDTH), BF16)]
    else:
        out_specs = [e_spec(0), e_spec(0)]
        out_shape = [jax.ShapeDtypeStruct((1, 2, h, nc), F32), jax.ShapeDtypeStruct((1, 2, h, nc), BF16)]
    return pl.pallas_call(
        functools.partial(_hyp_inv_kernel, last=last),
        grid=grid, in_specs=[m_spec] * 4 + [g_spec] * 4 + [e_spec(gate_plane), e_spec(prev_plane), b_spec],
        out_specs=out_specs, out_shape=out_shape,
        compiler_params=_cparams("parallel", "arbitrary"),
        name="hyena_dft_inverse",
    )(*mats['inv'], *g, gate, prev, bias_row)


def _hyp_tap_tables(spec, filt, length):
    a_re, a_im, b_re, b_im = spec
    w = HY_WIDTH
    inv_n = 1.0 / (2 * length)
    tables = []
    for o in range(HY_ORDER):
        f_sl = slice((2 * o) * w, (2 * o + 1) * w)
        r_sl = slice((2 * o + 1) * w, (2 * o + 2) * w)
        hb0 = filt[0, 0:1, r_sl]
        f1re = (a_re + b_re)[:, f_sl] + (a_re + b_re)[:, r_sl] - hb0
        f1im = (a_im + b_im)[:, f_sl] - (a_im + b_im)[:, r_sl]
        f2re = (a_re - b_re)[:, f_sl] + (a_re - b_re)[:, r_sl] - hb0
        f2im = (b_im - a_im)[:, f_sl] - (b_im - a_im)[:, r_sl]
        dc = f1re[0:1]
        ny = f2re[0:1]
        mid_re = a_im[0:1, f_sl] + a_im[0:1, r_sl] - hb0
        mid_im = -b_im[0:1, f_sl] + b_im[0:1, r_sl]
        sp = jnp.concatenate([dc * inv_n, ny * inv_n, mid_re * (2 * inv_n), mid_im * (2 * inv_n),
                              jnp.zeros((4, w), F32)], axis=0)
        tables.append((f1re * (2 * inv_n), f1im * (2 * inv_n), f2re * (2 * inv_n), f2im * (2 * inv_n), sp))
    return tables


def _hyena_seq(mats, vx, vx_bf, lp, n_batch):
    h = vx.shape[2]
    length = 2 * h
    filt, filt_bf = _hyp_filters(length, lp)
    spec = _hyp_fwd(mats, filt_bf[None], 0)
    tables = _hyp_tap_tables(spec, filt, length)
    bias = lp['hy_bias'].astype(F32)
    y, y_bf = vx, vx_bf
    for o in range(HY_ORDER):
        g = _hyp_fwd(mats, y_bf, 0, tables[o])
        bias_row = jnp.tile(bias[o][None, :], (1, n_batch))
        last = o == HY_ORDER - 1
        res = _hyp_inv(mats, g, vx, o + 1, y, 0, bias_row, last)
        if last:
            return res[0].reshape(n_batch, length, HY_WIDTH)
        y, y_bf = res


def _lru_kernel(u_ref, g_ref, cw_ref, cb_ref, wa_ref, ba_ref, wx_ref, bx_ref, lam_ref, o_ref,
                pad_ref, y_ref, *, seq, ctx_len):
    tc = LRU_CHUNK
    halo = LRU_HALO
    width = LRU_WIDTH
    lat_off = halo
    ctx_off = 2 * halo + seq
    zero = jnp.zeros((halo, width), F32)
    pad_ref[0:halo, :] = zero
    pad_ref[lat_off:lat_off + seq, :] = u_ref[0, 0:seq, :]
    pad_ref[lat_off + seq:ctx_off, :] = zero
    pad_ref[ctx_off:ctx_off + ctx_len, :] = u_ref[0, seq:seq + ctx_len, :]
    pad_ref[ctx_off + ctx_len:ctx_off + ctx_len + halo, :] = zero
    row = lax.broadcasted_iota(I32, (tc, width), 0)
    n_win = tc + 2 * halo

    def chunk(pad_off, y_off, s, carry, d):
        wstart = pl.multiple_of(pad_off + s - halo, 8)
        win = pad_ref[pl.ds(wstart, n_win), :]
        cw = cw_ref[d]
        xc = cb_ref[d]
        for k in range(LRU_CONV):
            shift = (LRU_CONV - 1 - k) if d == 0 else -k
            rolled = win if shift == 0 else pltpu.roll(win, shift % n_win, axis=0)
            xc = xc + cw[k:k + 1] * rolled[halo:halo + tc]
        xb = xc.astype(BF16)
        r = _sigmoid(_dot(xb, wa_ref[d]) + ba_ref[d])
        gi = _sigmoid(_dot(xb, wx_ref[d]) + bx_ref[d])
        lam = lam_ref[d]
        softplus = jnp.maximum(-lam, 0.0) + jnp.log1p(jnp.exp(-jnp.abs(lam)))
        log_a = -LRU_C * r * softplus
        a = jnp.exp(log_a)
        bt = jnp.sqrt(-jnp.tanh(log_a) * (a * a + 1.0)) * (gi * xc)
        sft = 1
        while sft < tc:
            if d == 0:
                keep = row >= sft
                a_s = jnp.where(keep, pltpu.roll(a, sft, axis=0), 1.0)
                b_s = jnp.where(keep, pltpu.roll(bt, sft, axis=0), 0.0)
            else:
                keep = row < tc - sft
                a_s = jnp.where(keep, pltpu.roll(a, tc - sft, axis=0), 1.0)
                b_s = jnp.where(keep, pltpu.roll(bt, tc - sft, axis=0), 0.0)
            bt = a * b_s + bt
            a = a * a_s
            sft *= 2
        h = a * carry + bt
        yo = pl.multiple_of(y_off + s, 8)
        if d == 0:
            y_ref[pl.ds(yo, tc), :] = h
            return h[tc - 1:tc]
        y_ref[pl.ds(yo, tc), :] = y_ref[pl.ds(yo, tc), :] + h
        return h[0:1]

    n_lat = seq // tc
    n_ctx = ctx_len // tc
    for d in range(2):
        carry = jnp.zeros((1, width), F32)
        order = range(n_ctx) if d == 0 else range(n_ctx - 1, -1, -1)
        for c in order:
            carry = chunk(ctx_off, seq, c * tc, carry, d)

        def body(j, cr, d=d):
            jj = j if d == 0 else n_lat - 1 - j
            return chunk(lat_off, 0, jj * tc, cr, d)

        lax.fori_loop(0, n_lat, body, carry)
    o_ref[0] = (y_ref[...] * _gelu_tanh(g_ref[0])).astype(BF16)


def _block_diag(w):
    nd, nb, c, _ = w.shape
    out = jnp.zeros((nd, nb * c, nb * c), w.dtype)
    for n in range(nb):
        out = out.at[:, n * c:(n + 1) * c, n * c:(n + 1) * c].set(w[:, n])
    return out


def _lru_mixer(lu, lg, lp, seq):
    b, nt, w = lu.shape
    ctx_len = nt - seq
    row3 = lambda a: a.reshape(2, 1, w)
    args = (_pad_to(lp['lru_conv_w'], 8, 1), row3(lp['lru_conv_b']), _block_diag(lp['lru_wa']).astype(BF16),
            row3(lp['lru_ba']), _block_diag(lp['lru_wx']).astype(BF16), row3(lp['lru_bx']), row3(lp['lru_lambda']))
    full = lambda a: pl.BlockSpec(a.shape, lambda bb: (0,) * a.ndim)
    tok = pl.BlockSpec((1, nt, w), lambda bb: (bb, 0, 0))
    return pl.pallas_call(
        functools.partial(_lru_kernel, seq=seq, ctx_len=ctx_len),
        grid=(b,),
        in_specs=[tok, tok] + [full(a) for a in args],
        out_specs=tok,
        out_shape=jax.ShapeDtypeStruct((b, nt, w), BF16),
        scratch_shapes=[pltpu.VMEM((nt + 3 * LRU_HALO, w), F32), pltpu.VMEM((nt, w), F32)],
        compiler_params=_cparams("parallel"),
        name="rglru_scan",
    )(lu, lg, *args)


def _merge_kernel(xl_ref, xc_ref, mod_ref, g1_ref, a_ref, b_ref, c_ref, d_ref, wg_ref, wa_ref, wb_ref, wc_ref,
                  wd_ref, wo_ref, o_ref, *, n_lat_tiles):
    group, tm, dm = xl_ref.shape
    m = mod_ref[:, 0]
    x = _stream_tile(xl_ref, xc_ref, n_lat_tiles)
    h = _normmod(x, g1_ref[...], m[:, 1:2], m[:, 0:1]).reshape(group * tm, dm).astype(BF16)
    acc = None
    for k, (br, w) in enumerate(((a_ref, wa_ref), (b_ref, wb_ref), (c_ref, wc_ref), (d_ref, wd_ref))):
        term = (1.0 + jnp.tanh(_dot(h, wg_ref[:, k * dm:(k + 1) * dm]))) * _dot(
            br[...].reshape(group * tm, br.shape[-1]), w[...])
        acc = term if acc is None else acc + term
    y = _dot(acc.astype(BF16), wo_ref[...])
    o_ref[...] = x + m[:, 2:3] * y.reshape(group, tm, dm)


def _merge(stream, modtab, g1, branches, w_gate, lp, n_lat_tiles, tiles_used):
    b, _, d = stream[0].shape
    wbr = lp['w_branch']
    head_rows = lambda w, dv: jnp.concatenate(
        [_pad_to(w[hd * dv:(hd + 1) * dv], LANE, 0) for hd in range(4)], axis=0)
    wbr = 0.5 * wbr
    weights = (head_rows(wbr[0], MLA_V).astype(BF16), head_rows(wbr[1], NA_HEAD_DIM).astype(BF16),
               wbr[2].astype(BF16), wbr[3].astype(BF16), lp['w_out'].astype(BF16))
    group = _batch_group(b)
    full = lambda a: pl.BlockSpec(a.shape, lambda bb, i: (0,) * a.ndim)
    tok = lambda n: pl.BlockSpec((group, TM, n), lambda bb, i: (bb, i, 0))
    return pl.pallas_call(
        functools.partial(_merge_kernel, n_lat_tiles=n_lat_tiles),
        grid=(b // group, tiles_used),
        in_specs=_stream_specs(stream, group, n_lat_tiles)
                 + [pl.BlockSpec((group, 1, 8, d), _kind_map(n_lat_tiles)), full(g1)]
                 + [tok(br.shape[-1]) for br in branches] + [full(w_gate)] + [full(w) for w in weights],
        out_specs=tok(d),
        out_shape=jax.ShapeDtypeStruct((b, tiles_used * TM, d), F32),
        compiler_params=_cparams("parallel", "parallel"),
        name="merge_branches",
    )(stream[0], stream[1], modtab, g1, *branches, w_gate, *weights)


U32 = jnp.uint32
EXPERT_ROWS = 896


def _pack_pair(x):
    n = x.shape[-1] // 2
    hi = lax.bitcast_convert_type(x[:, :n].astype(BF16).astype(F32), U32)
    lo = lax.bitcast_convert_type(x[:, n:].astype(BF16).astype(F32), U32)
    return hi | (lo >> 16)


def _unpack_pair(p):
    hi = lax.bitcast_convert_type(p & jnp.uint32(0xFFFF0000), F32)
    lo = lax.bitcast_convert_type(p << 16, F32)
    return hi, lo


def _router_kernel(x_ref, mod_ref, g2_ref, rw_ref, rb_ref, tri_ref, h2_o, idx_o, wts_o, rank_o, cnt_o, carry):
    i = pl.program_id(0)

    @pl.when(i == 0)
    def _():
        carry[...] = jnp.zeros_like(carry)

    group, tm, d = x_ref.shape
    m = mod_ref[:, 0]
    h2 = _normmod(x_ref[...], g2_ref[...], m[:, 4:5], m[:, 3:4]).reshape(group * tm, d)
    half = h2.shape[-1] // 2
    h2_o[0] = _pack_pair(h2[:, :half])
    h2_o[1] = _pack_pair(h2[:, half:])
    h_hi = h2.astype(BF16)
    h_lo = (h2 - h_hi.astype(F32)).astype(BF16)
    logits = _dot_nt(rw_ref[0], h_hi) + _dot_nt(rw_ref[0], h_lo) + _dot_nt(rw_ref[1], h_hi)
    scores = _sigmoid(logits)
    biased = scores + rb_ref[...]
    expert = lax.broadcasted_iota(I32, scores.shape, 0)
    picks = []
    onehot_all = jnp.zeros(scores.shape, F32)
    for _ in range(TOP_K):
        best = jnp.max(biased, axis=0, keepdims=True)
        arg = jnp.min(jnp.where(biased == best, expert, N_EXPERTS), axis=0, keepdims=True)
        hit = expert == arg
        sel = jnp.sum(jnp.where(hit, scores, 0.0), axis=0, keepdims=True)
        biased = jnp.where(hit, -jnp.inf, biased)
        onehot_all = onehot_all + jnp.where(hit, 1.0, 0.0)
        picks.append((arg, hit, sel))
    total = picks[0][2]
    for _, _, sel in picks[1:]:
        total = total + sel
    earlier = _dot(onehot_all.astype(BF16), tri_ref[...]) + carry[...]
    pad_rows = TOPK_PAD - TOP_K
    ranks = [jnp.sum(jnp.where(hit, earlier, 0.0), axis=0, keepdims=True).astype(I32) for _, hit, _ in picks]
    scale = ROUTED_SCALE / total
    n_tok = scores.shape[1]
    idx_o[...] = jnp.concatenate([arg for arg, _, _ in picks] + [jnp.zeros((pad_rows, n_tok), I32)], axis=0)
    wts_o[...] = jnp.concatenate([sel * scale for _, _, sel in picks] + [jnp.zeros((pad_rows, n_tok), F32)], axis=0)
    rank_o[...] = jnp.concatenate(ranks + [jnp.zeros((pad_rows, n_tok), I32)], axis=0)
    carry[...] = carry[...] + jnp.sum(onehot_all, axis=1, keepdims=True)
    cnt_o[...] = carry[...]


def _tile_maps(n_groups, n_lat_tiles):
    tok = lambda f: (f % n_groups, f // n_groups, 0)
    mod = lambda f: (f % n_groups, jnp.where(f // n_groups >= n_lat_tiles, 1, 0), 0, 0)
    return tok, mod


def _route(x1, modtab, g2, lp, n_lat_tiles, tiles_used):
    b, _, d = x1.shape
    group = _batch_group(b, MOE_BATCH_GROUP)
    rows = group * TM
    t = b * tiles_used * TM
    rw = lp['router_w'].T
    rw_hi = rw.astype(BF16)
    rw = jnp.stack([rw_hi, (rw - rw_hi.astype(F32)).astype(BF16)])
    rb = lp['router_bias'].reshape(-1, 1)
    tri = (np.arange(rows)[:, None] < np.arange(rows)[None, :]).astype(np.float32)
    tri = jnp.asarray(tri, BF16)
    per_tok = lambda: pl.BlockSpec((TOPK_PAD, rows), lambda i: (0, i))
    full = lambda a: pl.BlockSpec(a.shape, lambda i: (0,) * a.ndim)
    tok, mod = _tile_maps(b // group, n_lat_tiles)
    return pl.pallas_call(
        _router_kernel,
        grid=(t // rows,),
        in_specs=[pl.BlockSpec((group, TM, d), tok), pl.BlockSpec((group, 1, 8, d), mod),
                  full(g2), full(rw), full(rb), full(tri)],
        out_specs=[pl.BlockSpec((2, rows, d // 4), lambda i: (0, i, 0)), per_tok(), per_tok(), per_tok(),
                   pl.BlockSpec((N_EXPERTS, 1), lambda i: (0, 0))],
        out_shape=[jax.ShapeDtypeStruct((2, t, d // 4), U32), jax.ShapeDtypeStruct((TOPK_PAD, t), I32),
                   jax.ShapeDtypeStruct((TOPK_PAD, t), F32), jax.ShapeDtypeStruct((TOPK_PAD, t), I32),
                   jax.ShapeDtypeStruct((N_EXPERTS, 1), F32)],
        scratch_shapes=[pltpu.VMEM((N_EXPERTS, 1), F32)],
        compiler_params=_cparams("arbitrary"),
        name="moe_router",
    )(x1, modtab, g2, rw, rb, tri)


SC_WINDOW = 128


def _sc_mesh():
    return plsc.VectorSubcoreMesh(core_axis_name="c", subcore_axis_name="s")


def _sc_scatter_rows(src, idx, n_out):
    n, width = src.shape
    k_rep = idx.shape[0]
    half = n // SC_WINDOW // 2

    @functools.partial(pl.kernel, out_type=jax.ShapeDtypeStruct((n_out, width), src.dtype), mesh=_sc_mesh(),
                       scratch_types=[], name="moe_dispatch_sc")
    def scatter(src_hbm, idx_hbm, out_hbm):
        def body(x_vmem, *i_vmems):
            for i_vmem in i_vmems:
                pltpu.sync_copy(x_vmem, out_hbm.at[i_vmem.at[0]])

        pltpu.emit_pipeline(
            body,
            grid=(2, half),
            in_specs=[pl.BlockSpec((SC_WINDOW, width), lambda a, i: (a * half + i, 0))]
                     + [pl.BlockSpec((1, SC_WINDOW), lambda a, i, k=k: (k, a * half + i)) for k in range(k_rep)],
            out_specs=[],
            core_axis_name=("c", "s"),
            dimension_semantics=(pltpu.PARALLEL, pltpu.PARALLEL),
        )(src_hbm, *([idx_hbm] * k_rep))

    return scatter(src, idx)


def _sc_gather_rows(src, idx):
    k_rep, n = idx.shape
    width = src.shape[1]
    n_win = n // SC_WINDOW

    @functools.partial(pl.kernel, out_type=jax.ShapeDtypeStruct((k_rep * n, width), src.dtype), mesh=_sc_mesh(),
                       scratch_types=[], name="moe_gather_sc")
    def gather(src_hbm, idx_hbm, out_hbm):
        def body(i_vmem, o_vmem):
            pltpu.sync_copy(src_hbm.at[i_vmem.at[0]], o_vmem)

        pltpu.emit_pipeline(
            body,
            grid=(k_rep, n_win),
            in_specs=[pl.BlockSpec((1, SC_WINDOW), lambda k, i: (k, i))],
            out_specs=[pl.BlockSpec((SC_WINDOW, width), lambda k, i: (k * n_win + i, 0))],
            core_axis_name=("c", "s"),
            dimension_semantics=(pltpu.PARALLEL, pltpu.PARALLEL),
        )(idx_hbm, out_hbm)

    return gather(src, idx)


def _unpack_planes(p0, p1):
    return _unpack_pair(p0) + _unpack_pair(p1)


def _dot_quarters(parts, w_ref):
    q = parts[0].shape[-1]
    acc = None
    for j, part in enumerate(parts):
        term = _dot(part.astype(BF16), w_ref[j * q:(j + 1) * q, :])
        acc = term if acc is None else acc + term
    return acc


def _expert_kernel(be_ref, nv_ref, blk_ref, xs_ref, wg_ref, wu_ref, wd_ref, ys_o, wg_s, wu_s, wd_s):
    del blk_ref
    i = pl.program_id(0)
    prev = be_ref[jnp.maximum(i - 1, 0)]

    @pl.when((i == 0) | (be_ref[i] != prev))
    def _():
        wg_s[...] = (0.5 * wg_ref[0]).astype(BF16)
        wu_s[...] = wu_ref[0].astype(BF16)
        wd_s[...] = wd_ref[0].astype(BF16)

    @pl.when(nv_ref[i] > 0)
    def _():
        keep = lax.broadcasted_iota(I32, xs_ref.shape[1:], 0) < nv_ref[i]
        parts = _unpack_planes(jnp.where(keep, xs_ref[0], jnp.uint32(0)), jnp.where(keep, xs_ref[1], jnp.uint32(0)))
        hid = _half_silu(_dot_quarters(parts, wg_s)) * _dot_quarters(parts, wu_s)
        y = _dot(hid.astype(BF16), wd_s[...])
        half = y.shape[-1] // 2
        ys_o[0] = _pack_pair(y[:, :half])
        ys_o[1] = _pack_pair(y[:, half:])


def _experts(xs, block_e, n_valid, block_src, weights, layer):
    _, n_rows, dq = xs.shape
    d = 4 * dq
    n_blocks = n_rows // EXPERT_ROWS
    hid = EXPERT_HIDDEN
    grid_spec = pltpu.PrefetchScalarGridSpec(
        num_scalar_prefetch=3,
        grid=(n_blocks,),
        in_specs=[pl.BlockSpec((2, EXPERT_ROWS, dq), lambda i, be, nv, blk: (0, blk[i], 0)),
                  pl.BlockSpec((None, 1, d, hid), lambda i, be, nv, blk: (layer, be[i], 0, 0)),
                  pl.BlockSpec((None, 1, d, hid), lambda i, be, nv, blk: (layer, be[i], 0, 0)),
                  pl.BlockSpec((None, 1, hid, d), lambda i, be, nv, blk: (layer, be[i], 0, 0))],
        out_specs=pl.BlockSpec((2, EXPERT_ROWS, dq), lambda i, be, nv, blk: (0, blk[i], 0)),
        scratch_shapes=[pltpu.VMEM((d, hid), BF16), pltpu.VMEM((d, hid), BF16), pltpu.VMEM((hid, d), BF16)],
    )
    return pl.pallas_call(
        _expert_kernel,
        grid_spec=grid_spec,
        out_shape=jax.ShapeDtypeStruct((2, n_rows, dq), U32),
        compiler_params=_cparams("arbitrary"),
        name="moe_experts",
    )(block_e, n_valid, block_src, xs, *weights)


def _shared_expert_kernel(h2_ref, sg_ref, su_ref, sd_ref, o_ref):
    parts = _unpack_planes(h2_ref[0], h2_ref[1])
    hid = _half_silu(_dot_quarters(parts, sg_ref)) * _dot_quarters(parts, su_ref)
    y = _dot(hid.astype(BF16), sd_ref[...])
    half = y.shape[-1] // 2
    o_ref[0] = _pack_pair(y[:, :half])
    o_ref[1] = _pack_pair(y[:, half:])


def _shared_expert(h2p, lp):
    _, t, dq = h2p.shape
    rows = TM * max(k for k in (4, 2, 1) if (t // TM) % k == 0)
    weights = ((0.5 * lp['sh_w_gate']).astype(BF16), lp['sh_w_up'].astype(BF16), lp['sh_w_down'].astype(BF16))
    full = lambda a: pl.BlockSpec(a.shape, lambda i: (0,) * a.ndim)
    plane = pl.BlockSpec((2, rows, dq), lambda i: (0, i, 0))
    return pl.pallas_call(
        _shared_expert_kernel,
        grid=(t // rows,),
        in_specs=[plane] + [full(w) for w in weights],
        out_specs=plane,
        out_shape=jax.ShapeDtypeStruct(h2p.shape, U32),
        compiler_params=_cparams("parallel"),
        name="moe_shared_expert",
    )(h2p, *weights)


def _combine_kernel(g_ref, wts_ref, sh_ref, x_ref, mod_ref, gf_ref, o_ref, *, final):
    shared = jnp.concatenate(_unpack_planes(sh_ref[0], sh_ref[1]), axis=1)
    wts = wts_ref[...]
    routed = None
    for k in range(TOP_K):
        w = wts[:, k:k + 1]
        terms = [w * part for part in _unpack_planes(g_ref[k, 0], g_ref[k, 1])]
        routed = terms if routed is None else [r + t for r, t in zip(routed, terms)]
    f = shared + jnp.concatenate(routed, axis=1)
    m = mod_ref[:, 0]
    x2 = x_ref[...] + m[:, 5:6] * f.reshape(x_ref.shape)
    if final:
        x2 = x2 * lax.rsqrt(jnp.mean(x2 * x2, axis=-1, keepdims=True) + NORM_EPS) * gf_ref[...]
    o_ref[...] = x2


def _combine(g, wts, shared, x1, modtab, g_final, n_lat_tiles, tiles_used, final):
    b, _, d = x1.shape
    dq = d // 4
    weights = (g_final.reshape(1, -1),)
    full = lambda a: pl.BlockSpec(a.shape, lambda i: (0,) * a.ndim)
    router_group = _batch_group(b, MOE_BATCH_GROUP)
    group = _batch_group(b, 2)
    sub, n_groups = router_group // group, b // router_group
    rows = group * TM
    batch_blk = lambda i: ((i // sub) % n_groups) * sub + i % sub
    tile = lambda i: (i // sub) // n_groups
    tok = lambda i: (batch_blk(i), tile(i), 0)
    mod = lambda i: (batch_blk(i), jnp.where(tile(i) >= n_lat_tiles, 1, 0), 0, 0)
    return pl.pallas_call(
        functools.partial(_combine_kernel, final=final),
        grid=(b * tiles_used * TM // rows,),
        in_specs=[pl.BlockSpec((TOP_K, 2, rows, dq), lambda i: (0, 0, i, 0)),
                  pl.BlockSpec((rows, TOPK_PAD), lambda i: (i, 0)),
                  pl.BlockSpec((2, rows, dq), lambda i: (0, i, 0)),
                  pl.BlockSpec((group, TM, d), tok), pl.BlockSpec((group, 1, 8, d), mod)]
                 + [full(w) for w in weights],
        out_specs=pl.BlockSpec((group, TM, d), tok),
        out_shape=jax.ShapeDtypeStruct((b, tiles_used * TM, d), F32),
        compiler_params=_cparams("parallel"),
        name="moe_combine",
    )(g, wts, shared, x1, modtab, *weights)


def _sorted_rows_kernel(idx_ref, rank_ref, ps_ref, o_ref):
    expert = lax.broadcasted_iota(I32, (N_EXPERTS, idx_ref.shape[1]), 0)
    starts = ps_ref[...]
    idx = idx_ref[...]
    rows = [jnp.sum(jnp.where(idx[k:k + 1] == expert, starts, 0), axis=0, keepdims=True) for k in range(TOP_K)]
    pad = jnp.zeros((TOPK_PAD - TOP_K, idx.shape[1]), I32)
    o_ref[...] = jnp.concatenate(rows + [pad], axis=0) + rank_ref[...]


def _sorted_rows(idx, rank, p_starts):
    t = idx.shape[1]
    cols = TM * max(k for k in (8, 4, 2, 1) if (t // TM) % k == 0)
    per_tok = pl.BlockSpec((TOPK_PAD, cols), lambda i: (0, i))
    return pl.pallas_call(
        _sorted_rows_kernel,
        grid=(t // cols,),
        in_specs=[per_tok, per_tok, pl.BlockSpec((N_EXPERTS, 1), lambda i: (0, 0))],
        out_specs=per_tok,
        out_shape=jax.ShapeDtypeStruct((TOPK_PAD, t), I32),
        compiler_params=_cparams("parallel"),
        name="moe_sorted_rows",
    )(idx, rank, p_starts.reshape(-1, 1))


def _moe(x1, modtab, g2, lp, g_final, n_lat_tiles, final):
    b, nt, d = x1.shape
    tiles_used = n_lat_tiles if final else nt // TM
    t = b * tiles_used * TM
    h2p, idx, wts, rank, cnt = _route(x1, modtab, g2, lp, n_lat_tiles, tiles_used)
    shared = _shared_expert(h2p, lp)
    counts = cnt[:, 0].astype(I32)
    padded = (counts + EXPERT_ROWS - 1) // EXPERT_ROWS * EXPERT_ROWS
    p_ends = jnp.cumsum(padded)
    p_starts = p_ends - padded
    n_blocks = (t * TOP_K + N_EXPERTS * (EXPERT_ROWS - 1)) // EXPERT_ROWS
    n_rows = n_blocks * EXPERT_ROWS
    dest = _sorted_rows(idx, rank, p_starts)[:TOP_K]
    plane_idx = jnp.concatenate([dest, dest + n_rows], axis=1)
    blk_start = jnp.arange(n_blocks, dtype=I32) * EXPERT_ROWS
    block_e = jnp.minimum(jnp.sum((p_ends[None, :] <= blk_start[:, None]).astype(I32), axis=1), N_EXPERTS - 1)
    n_valid = jnp.clip((p_starts + counts)[block_e] - blk_start, 0, EXPERT_ROWS).astype(I32)
    dq = d // 4
    xs = _sc_scatter_rows(h2p.reshape(2 * t, dq), plane_idx, 2 * n_rows).reshape(2, n_rows, dq)
    block_src = jnp.minimum(jnp.arange(n_blocks, dtype=I32), p_ends[-1] // EXPERT_ROWS - 1)
    ys = _experts(xs, block_e, n_valid, block_src, lp['expert_stacks'], lp['layer'])
    g = _sc_gather_rows(ys.reshape(2 * n_rows, dq), plane_idx).reshape(TOP_K, 2, t, dq)
    return _combine(g, wts.T, shared, x1, modtab, g_final, n_lat_tiles, tiles_used, final)


def _layer(stream, nt, c, c_ctx, lp, consts, g_final, seq, final):
    b = stream[0].shape[0]
    n_lat_tiles = seq // TM
    rope, mats_lat, mats_ctx = consts
    modtab = _mod_table(c, c_ctx, *lp['mod_stacks'], lp['layer'])
    g1 = lp['g_norm1'].reshape(1, -1)
    g2 = lp['g_norm2'].reshape(1, -1)
    pw = _proj_weights(lp)
    q, k, v, nq, nk, nv, hy, lu, lg = _project(stream, nt, modtab, g1, pw, rope, n_lat_tiles)
    tiles_used = n_lat_tiles if final else nt // TM
    br_a = _mla_attention(q, k, v, seq, tiles_used)
    br_b = _na_attention(nq, nk, nv, _na_bias_tables(lp['na_rpb'], seq // GRID_W), seq, tiles_used)
    pre = _hyp_pre(hy, lp['hy_short_w'], lp['hy_short_b'], seq)
    lat_f, lat_bf, ctx_f, ctx_bf = pre
    br_c = _hyena_seq(mats_lat, lat_f, lat_bf, lp, b)
    if not final:
        br_c = jnp.concatenate([br_c, _hyena_seq(mats_ctx, ctx_f, ctx_bf, lp, b)], axis=1)
    br_d = _lru_mixer(lu, lg, lp, seq)
    x1 = _merge(stream, modtab, g1, (br_a, br_b, br_c, br_d), pw['w_gate'], lp, n_lat_tiles, tiles_used)
    return _moe(x1, modtab, g2, lp, g_final, n_lat_tiles, final)


_LAYER_KEYS = ('w_mod', 'b_mod', 'g_norm1', 'g_norm2', 'w_in', 'mla_g_q', 'mla_w_uq', 'mla_g_kv', 'mla_w_ukv',
               'na_rpb', 'hy_short_w', 'hy_short_b', 'hy_w1', 'hy_b1', 'hy_w2', 'hy_b2', 'hy_w3', 'hy_decay',
               'hy_bias', 'lru_conv_w', 'lru_conv_b', 'lru_wa', 'lru_ba', 'lru_wx', 'lru_bx', 'lru_lambda',
               'w_branch', 'w_out', 'router_w', 'router_bias', 'exp_w_gate', 'exp_w_up', 'exp_w_down',
               'sh_w_gate', 'sh_w_up', 'sh_w_down')


def kernel(x, c, ctx, c_ctx, w_mod, b_mod, g_norm1, g_norm2, w_in, mla_g_q, mla_w_uq, mla_g_kv, mla_w_ukv, na_rpb, hy_short_w, hy_short_b, hy_w1, hy_b1, hy_w2, hy_b2, hy_w3, hy_decay, hy_bias, lru_conv_w, lru_conv_b, lru_wa, lru_ba, lru_wx, lru_bx, lru_lambda, w_branch, w_out, router_w, router_bias, exp_w_gate, exp_w_up, exp_w_down, sh_w_gate, sh_w_up, sh_w_down, g_final):
    stacked = dict(zip(_LAYER_KEYS, (w_mod, b_mod, g_norm1, g_norm2, w_in, mla_g_q, mla_w_uq, mla_g_kv, mla_w_ukv,
                                     na_rpb, hy_short_w, hy_short_b, hy_w1, hy_b1, hy_w2, hy_b2, hy_w3, hy_decay,
                                     hy_bias, lru_conv_w, lru_conv_b, lru_wa, lru_ba, lru_wx, lru_bx, lru_lambda,
                                     w_branch, w_out, router_w, router_bias, exp_w_gate, exp_w_up, exp_w_down,
                                     sh_w_gate, sh_w_up, sh_w_down)))
    b, seq, d = x.shape
    ctx_len = ctx.shape[1]
    depth = w_mod.shape[0]
    assert seq % TM == 0 and ctx_len % TM == 0 and seq // GRID_W >= NA_KEY_ROWS + 1
    nt = seq + ctx_len
    stream = (x, ctx, 0)
    consts = (_rope_tables(seq, seq + ctx_len), _hyp_matrices(seq), _hyp_matrices(ctx_len))
    for i in range(depth):
        big = ('w_mod', 'b_mod', 'exp_w_gate', 'exp_w_up', 'exp_w_down')
        lp = {name: w[i] for name, w in stacked.items() if name not in big}
        lp['layer'] = i
        lp['mod_stacks'] = (w_mod, b_mod)
        lp['expert_stacks'] = (exp_w_gate, exp_w_up, exp_w_down)
        xa = _layer(stream, nt, c, c_ctx, lp, consts, g_final, seq, i == depth - 1)
        stream = (xa, xa, seq // TM)
    return xa
```
